```python
import math
import jax, jax.numpy as jnp
from jax import lax
import numpy as np

D_MODEL = 2048
BATCH = 8
SEQ = 8192
DEPTH = 1

D_FF = 5504
SSD_D_INNER = 2048
SSD_HEAD_DIM = 64
SSD_HEADS = SSD_D_INNER // SSD_HEAD_DIM
SSD_GROUPS = 4
SSD_HEADS_PER_GROUP = SSD_HEADS // SSD_GROUPS
SSD_STATE = 128
SSD_CONV = 4
SSD_CHUNK = 128
SSD_CONV_DIM = SSD_D_INNER + 2 * SSD_GROUPS * SSD_STATE
S5_WIDTH = 1024
S5_GROUP_SIZE = 16
S5_GROUPS = S5_WIDTH // S5_GROUP_SIZE
S5_STATE = 64
OFF_XBC = SSD_D_INNER
OFF_DT = OFF_XBC + SSD_CONV_DIM
OFF_U = OFF_DT + SSD_HEADS
OFF_GATES = OFF_U + S5_WIDTH
IN_COLS = OFF_GATES + 2 * D_MODEL
N_ADA = 9
EPS = 1e-6

kernel_name = "hybrid_ssd_s5_gated_macaron_block"


def rms_norm(x, g):
    xf = x.astype(jnp.float32)
    y = xf * lax.rsqrt(jnp.mean(xf * xf, axis=-1, keepdims=True) + EPS)
    return (y * g.astype(jnp.float32)).astype(x.dtype)


def modulate(x, g, shift, scale):
    return rms_norm(x, g) * (1 + scale) + shift


def swiglu(h, w_in, w_out):
    a, b = jnp.split(h @ w_in, 2, axis=-1)
    return (jax.nn.silu(a) * b) @ w_out


def causal_depthwise_conv(x, w, b):
    k_w = w.shape[0]
    s = x.shape[1]
    xp = jnp.pad(x, ((0, 0), (k_w - 1, 0), (0, 0)))
    out = b
    for k in range(k_w):
        out = out + xp[:, k:k + s] * w[k]
    return out


def ssd_mixer(xbc, z, dt_raw, conv_w, conv_b, dt_bias, a_log, d_skip, norm_w):
    f32 = jnp.float32
    bsz, s, _ = xbc.shape
    nc = s // SSD_CHUNK
    G, R, P, N, L = SSD_GROUPS, SSD_HEADS_PER_GROUP, SSD_HEAD_DIM, SSD_STATE, SSD_CHUNK
    xbc = jax.nn.silu(causal_depthwise_conv(xbc, conv_w, conv_b)).astype(f32)
    xs, bm, cm = jnp.split(xbc, [SSD_D_INNER, SSD_D_INNER + G * N], axis=-1)
    xs = xs.reshape(bsz, nc, L, G, R, P)
    bm = bm.reshape(bsz, nc, L, G, N)
    cm = cm.reshape(bsz, nc, L, G, N)
    dt = jax.nn.softplus(dt_raw.astype(f32) + dt_bias.astype(f32)).reshape(bsz, nc, L, G, R)
    a = -jnp.exp(a_log.astype(f32)).reshape(G, R)
    xdt = xs * dt[..., None]
    da_cs = jnp.cumsum(jnp.moveaxis(dt * a, 2, -1), axis=-1)
    causal = jnp.tril(jnp.ones((L, L), dtype=bool))
    seg = da_cs[..., :, None] - da_cs[..., None, :]
    decay_ls = jnp.exp(jnp.where(causal, seg, -jnp.inf))
    cb = jnp.einsum('bclgn,bcsgn->bcgls', cm, bm)
    y_diag = jnp.einsum('bcgls,bcgrls,bcsgrp->bclgrp', cb, decay_ls, xdt)
    decay_to_end = jnp.exp(da_cs[..., -1:] - da_cs)
    states = jnp.einsum('bclgn,bcgrl,bclgrp->bcgrpn', bm, decay_to_end, xdt)
    chunk_decay = jnp.exp(da_cs[..., -1])

    def step(h, inp):
        st, dec = inp
        return dec[..., None, None] * h + st, h

    h0 = jnp.zeros((bsz, G, R, P, N), f32)
    _, prev = lax.scan(step, h0, (jnp.moveaxis(states, 1, 0), jnp.moveaxis(chunk_decay, 1, 0)))
    prev = jnp.moveaxis(prev, 0, 1)
    y_off = jnp.einsum('bclgn,bcgrpn,bcgrl->bclgrp', cm, prev, jnp.exp(da_cs))
    y = y_diag + y_off + d_skip.astype(f32).reshape(G, R)[:, :, None] * xs
    y = y.reshape(bsz, s, G, R * P)
    yz = y * jax.nn.silu(z.astype(f32)).reshape(bsz, s, G, R * P)
    yz = yz * lax.rsqrt(jnp.mean(yz * yz, axis=-1, keepdims=True) + EPS)
    return (yz.reshape(bsz, s, SSD_D_INNER) * norm_w.astype(f32)).astype(z.dtype)


def s5_mixer(u, lambda_re, lambda_im, b_re, b_im, c_re, c_im, d_skip, log_dt):
    f32 = jnp.float32
    bsz, s, _ = u.shape
    uf = u.astype(f32).reshape(bsz, s, S5_GROUPS, S5_GROUP_SIZE)
    dt = jnp.exp(log_dt.astype(f32))[:, None]
    lr = jnp.minimum(lambda_re.astype(f32), -1e-4)
    li = lambda_im.astype(f32)
    mag = jnp.exp(lr * dt)
    ar = mag * jnp.cos(li * dt)
    ai = mag * jnp.sin(li * dt)
    den = lr * lr + li * li
    nr = ar - 1.0
    kr = (nr * lr + ai * li) / den
    ki = (ai * lr - nr * li) / den
    br = b_re.astype(f32)
    bi = b_im.astype(f32)
    bbar_re = kr[..., None] * br - ki[..., None] * bi
    bbar_im = kr[..., None] * bi + ki[..., None] * br
    bu_re = jnp.einsum('bsgi,gpi->bsgp', uf, bbar_re)
    bu_im = jnp.einsum('bsgi,gpi->bsgp', uf, bbar_im)
    a_re = jnp.broadcast_to(ar, bu_re.shape)
    a_im = jnp.broadcast_to(ai, bu_im.shape)

    def combine(e1, e2):
        a1r, a1i, b1r, b1i = e1
        a2r, a2i, b2r, b2i = e2
        return (a2r * a1r - a2i * a1i,
                a2r * a1i + a2i * a1r,
                a2r * b1r - a2i * b1i + b2r,
                a2r * b1i + a2i * b1r + b2i)

    _, _, s_re, s_im = lax.associative_scan(combine, (a_re, a_im, bu_re, bu_im), axis=1)
    y = (jnp.einsum('bsgp,gip->bsgi', s_re, c_re.astype(f32))
         - jnp.einsum('bsgp,gip->bsgi', s_im, c_im.astype(f32))
         + d_skip.astype(f32) * uf)
    return y.reshape(bsz, s, S5_WIDTH).astype(u.dtype)


def _fwd_setup_inputs(seed: int = 0) -> dict:
    key = jax.random.key(seed)
    ks = jax.random.split(key, 40)
    f32 = jnp.float32

    def nrm(k, shape, scale):
        return jax.random.normal(k, shape, f32) * scale

    dt_ssd = jnp.exp(jax.random.uniform(ks[11], (DEPTH, SSD_HEADS), f32, math.log(1e-3), math.log(1e-1)))
    lam_im = (jnp.pi * jnp.arange(S5_STATE, dtype=f32))[None, None, :] + nrm(ks[16], (DEPTH, S5_GROUPS, S5_STATE), 0.01)
    return {
        "x": nrm(ks[0], (BATCH, SEQ, D_MODEL), 1.0),
        "c": nrm(ks[1], (BATCH, D_MODEL), 1.0),
        "w_ada": nrm(ks[2], (DEPTH, D_MODEL, N_ADA * D_MODEL), D_MODEL ** -0.5),
        "b_ada": nrm(ks[3], (DEPTH, N_ADA * D_MODEL), 0.01),
        "norm_ffn1": 1.0 + nrm(ks[4], (DEPTH, D_MODEL), 0.02),
        "w_ffn1_in": nrm(ks[5], (DEPTH, D_MODEL, 2 * D_FF), D_MODEL ** -0.5),
        "w_ffn1_out": nrm(ks[6], (DEPTH, D_FF, D_MODEL), D_FF ** -0.5),
        "norm_mix": 1.0 + nrm(ks[7], (DEPTH, D_MODEL), 0.02),
        "w_in": nrm(ks[8], (DEPTH, D_MODEL, IN_COLS), D_MODEL ** -0.5),
        "conv_w": nrm(ks[9], (DEPTH, SSD_CONV, SSD_CONV_DIM), SSD_CONV ** -0.5),
        "conv_b": nrm(ks[10], (DEPTH, SSD_CONV_DIM), 0.01),
        "dt_bias": dt_ssd + jnp.log(-jnp.expm1(-dt_ssd)),
        "a_log": jnp.log(jax.random.uniform(ks[12], (DEPTH, SSD_HEADS), f32, 1.0, 16.0)),
        "d_ssd": 1.0 + nrm(ks[13], (DEPTH, SSD_HEADS), 0.02),
        "ssd_norm_w": 1.0 + nrm(ks[14], (DEPTH, SSD_D_INNER), 0.02),
        "w_a_proj": nrm(ks[15], (DEPTH, SSD_D_INNER, D_MODEL), SSD_D_INNER ** -0.5),
        "s5_lambda_re": -0.5 + nrm(ks[17], (DEPTH, S5_GROUPS, S5_STATE), 0.01),
        "s5_lambda_im": lam_im,
        "s5_b_re": nrm(ks[18], (DEPTH, S5_GROUPS, S5_STATE, S5_GROUP_SIZE), (2 * S5_GROUP_SIZE) ** -0.5),
        "s5_b_im": nrm(ks[19], (DEPTH, S5_GROUPS, S5_STATE, S5_GROUP_SIZE), (2 * S5_GROUP_SIZE) ** -0.5),
        "s5_c_re": nrm(ks[20], (DEPTH, S5_GROUPS, S5_GROUP_SIZE, S5_STATE), (2 * S5_STATE) ** -0.5),
        "s5_c_im": nrm(ks[21], (DEPTH, S5_GROUPS, S5_GROUP_SIZE, S5_STATE), (2 * S5_STATE) ** -0.5),
        "s5_d": nrm(ks[22], (DEPTH, S5_GROUPS, S5_GROUP_SIZE), 1.0),
        "s5_log_dt": jax.random.uniform(ks[23], (DEPTH, S5_GROUPS), f32, math.log(1e-3), math.log(1e-1)),
        "w_b_glu": nrm(ks[24], (DEPTH, S5_WIDTH, 2 * D_MODEL), S5_WIDTH ** -0.5),
        "w_out": nrm(ks[25], (DEPTH, D_MODEL, D_MODEL), D_MODEL ** -0.5),
        "norm_ffn2": 1.0 + nrm(ks[26], (DEPTH, D_MODEL), 0.02),
        "w_ffn2_in": nrm(ks[27], (DEPTH, D_MODEL, 2 * D_FF), D_MODEL ** -0.5),
        "w_ffn2_out": nrm(ks[28], (DEPTH, D_FF, D_MODEL), D_FF ** -0.5),
        "norm_final": 1.0 + nrm(ks[29], (D_MODEL,), 0.02),
    }


def _fwd_reference(x, c, w_ada, b_ada, norm_ffn1, w_ffn1_in, w_ffn1_out, norm_mix, w_in,
              conv_w, conv_b, dt_bias, a_log, d_ssd, ssd_norm_w, w_a_proj,
              s5_lambda_re, s5_lambda_im, s5_b_re, s5_b_im, s5_c_re, s5_c_im, s5_d, s5_log_dt,
              w_b_glu, w_out, norm_ffn2, w_ffn2_in, w_ffn2_out, norm_final):
    c_act = jax.nn.silu(c)
    for l in range(DEPTH):
        mods = (c_act @ w_ada[l] + b_ada[l])[:, None, :]
        (sh1, sc1, g1, sh2, sc2, g2, sh3, sc3, g3) = jnp.split(mods, N_ADA, axis=-1)

        h = modulate(x, norm_ffn1[l], sh1, sc1)
        x = x + 0.5 * g1 * swiglu(h, w_ffn1_in[l], w_ffn1_out[l])

        h = modulate(x, norm_mix[l], sh2, sc2)
        proj = h @ w_in[l]
        z, xbc, dt_raw, u, gates = jnp.split(proj, [OFF_XBC, OFF_DT, OFF_U, OFF_GATES], axis=-1)
        gate_a, gate_b = jnp.split(gates, 2, axis=-1)

        y_a = ssd_mixer(xbc, z, dt_raw, conv_w[l], conv_b[l], dt_bias[l], a_log[l], d_ssd[l], ssd_norm_w[l])
        p_a = y_a @ w_a_proj[l]

        y_b = s5_mixer(u, s5_lambda_re[l], s5_lambda_im[l], s5_b_re[l], s5_b_im[l],
                       s5_c_re[l], s5_c_im[l], s5_d[l], s5_log_dt[l])
        glu_a, glu_g = jnp.split(jax.nn.gelu(y_b) @ w_b_glu[l], 2, axis=-1)
        p_b = glu_a * jax.nn.sigmoid(glu_g)

        merged = jax.nn.sigmoid(gate_a) * p_a + jax.nn.sigmoid(gate_b) * p_b
        x = x + g2 * (merged @ w_out[l])

        h = modulate(x, norm_ffn2[l], sh3, sc3)
        x = x + 0.5 * g3 * swiglu(h, w_ffn2_in[l], w_ffn2_out[l])
    return rms_norm(x, norm_final)


import jax as _jax
import jax.numpy as _jnp

TWIN_FORMAT = 'train_step'
FWD_PARAMS = ['x', 'c', 'w_ada', 'b_ada', 'norm_ffn1', 'w_ffn1_in', 'w_ffn1_out', 'norm_mix', 'w_in', 'conv_w', 'conv_b', 'dt_bias', 'a_log', 'd_ssd', 'ssd_norm_w', 'w_a_proj', 's5_lambda_re', 's5_lambda_im', 's5_b_re', 's5_b_im', 's5_c_re', 's5_c_im', 's5_d', 's5_log_dt', 'w_b_glu', 'w_out', 'norm_ffn2', 'w_ffn2_in', 'w_ffn2_out', 'norm_final']
TWIN_WEIGHTS = ['w_ada', 'b_ada', 'norm_ffn1', 'w_ffn1_in', 'w_ffn1_out', 'norm_mix', 'w_in', 'conv_w', 'conv_b', 'dt_bias', 'a_log', 'd_ssd', 'ssd_norm_w', 'w_a_proj', 's5_lambda_re', 's5_lambda_im', 's5_b_re', 's5_b_im', 's5_c_re', 's5_c_im', 's5_d', 's5_log_dt', 'w_b_glu', 'w_out', 'norm_ffn2', 'w_ffn2_in', 'w_ffn2_out', 'norm_final']
TWIN_DIFF_INPUT = 'x'
TWIN_INPUTS = ['x', 'c', 'w_ada', 'b_ada', 'norm_ffn1', 'w_ffn1_in', 'w_ffn1_out', 'norm_mix', 'w_in', 'conv_w', 'conv_b', 'dt_bias', 'a_log', 'd_ssd', 'ssd_norm_w', 'w_a_proj', 's5_lambda_re', 's5_lambda_im', 's5_b_re', 's5_b_im', 's5_c_re', 's5_c_im', 's5_d', 's5_log_dt', 'w_b_glu', 'w_out', 'norm_ffn2', 'w_ffn2_in', 'w_ffn2_out', 'norm_final', 'loss_target', 'm_w_ada', 'm_b_ada', 'm_norm_ffn1', 'm_w_ffn1_in', 'm_w_ffn1_out', 'm_norm_mix', 'm_w_in', 'm_conv_w', 'm_conv_b', 'm_dt_bias', 'm_a_log', 'm_d_ssd', 'm_ssd_norm_w', 'm_w_a_proj', 'm_s5_lambda_re', 'm_s5_lambda_im', 'm_s5_b_re', 'm_s5_b_im', 'm_s5_c_re', 'm_s5_c_im', 'm_s5_d', 'm_s5_log_dt', 'm_w_b_glu', 'm_w_out', 'm_norm_ffn2', 'm_w_ffn2_in', 'm_w_ffn2_out', 'm_norm_final', 'v_w_ada', 'v_b_ada', 'v_norm_ffn1', 'v_w_ffn1_in', 'v_w_ffn1_out', 'v_norm_mix', 'v_w_in', 'v_conv_w', 'v_conv_b', 'v_dt_bias', 'v_a_log', 'v_d_ssd', 'v_ssd_norm_w', 'v_w_a_proj', 'v_s5_lambda_re', 'v_s5_lambda_im', 'v_s5_b_re', 'v_s5_b_im', 'v_s5_c_re', 'v_s5_c_im', 'v_s5_d', 'v_s5_log_dt', 'v_w_b_glu', 'v_w_out', 'v_norm_ffn2', 'v_w_ffn2_in', 'v_w_ffn2_out', 'v_norm_final']
TWIN_OUTPUTS = ['loss', 'grad_x', 'grad_w_ada', 'grad_b_ada', 'grad_norm_ffn1', 'grad_w_ffn1_in', 'grad_w_ffn1_out', 'grad_norm_mix', 'grad_w_in', 'grad_conv_w', 'grad_conv_b', 'grad_dt_bias', 'grad_a_log', 'grad_d_ssd', 'grad_ssd_norm_w', 'grad_w_a_proj', 'grad_s5_lambda_re', 'grad_s5_lambda_im', 'grad_s5_b_re', 'grad_s5_b_im', 'grad_s5_c_re', 'grad_s5_c_im', 'grad_s5_d', 'grad_s5_log_dt', 'grad_w_b_glu', 'grad_w_out', 'grad_norm_ffn2', 'grad_w_ffn2_in', 'grad_w_ffn2_out', 'grad_norm_final', 'delta_w_ada', 'delta_b_ada', 'delta_norm_ffn1', 'delta_w_ffn1_in', 'delta_w_ffn1_out', 'delta_norm_mix', 'delta_w_in', 'delta_conv_w', 'delta_conv_b', 'delta_dt_bias', 'delta_a_log', 'delta_d_ssd', 'delta_ssd_norm_w', 'delta_w_a_proj', 'delta_s5_lambda_re', 'delta_s5_lambda_im', 'delta_s5_b_re', 'delta_s5_b_im', 'delta_s5_c_re', 'delta_s5_c_im', 'delta_s5_d', 'delta_s5_log_dt', 'delta_w_b_glu', 'delta_w_out', 'delta_norm_ffn2', 'delta_w_ffn2_in', 'delta_w_ffn2_out', 'delta_norm_final', 'new_m_w_ada', 'new_m_b_ada', 'new_m_norm_ffn1', 'new_m_w_ffn1_in', 'new_m_w_ffn1_out', 'new_m_norm_mix', 'new_m_w_in', 'new_m_conv_w', 'new_m_conv_b', 'new_m_dt_bias', 'new_m_a_log', 'new_m_d_ssd', 'new_m_ssd_norm_w', 'new_m_w_a_proj', 'new_m_s5_lambda_re', 'new_m_s5_lambda_im', 'new_m_s5_b_re', 'new_m_s5_b_im', 'new_m_s5_c_re', 'new_m_s5_c_im', 'new_m_s5_d', 'new_m_s5_log_dt', 'new_m_w_b_glu', 'new_m_w_out', 'new_m_norm_ffn2', 'new_m_w_ffn2_in', 'new_m_w_ffn2_out', 'new_m_norm_final', 'new_v_w_ada', 'new_v_b_ada', 'new_v_norm_ffn1', 'new_v_w_ffn1_in', 'new_v_w_ffn1_out', 'new_v_norm_mix', 'new_v_w_in', 'new_v_conv_w', 'new_v_conv_b', 'new_v_dt_bias', 'new_v_a_log', 'new_v_d_ssd', 'new_v_ssd_norm_w', 'new_v_w_a_proj', 'new_v_s5_lambda_re', 'new_v_s5_lambda_im', 'new_v_s5_b_re', 'new_v_s5_b_im', 'new_v_s5_c_re', 'new_v_s5_c_im', 'new_v_s5_d', 'new_v_s5_log_dt', 'new_v_w_b_glu', 'new_v_w_out', 'new_v_norm_ffn2', 'new_v_w_ffn2_in', 'new_v_w_ffn2_out', 'new_v_norm_final']
TWIN_LEAF_KINDS = {'loss': 'loss', 'grad_x': 'grad_x', 'grad_w_ada': 'grad_w', 'grad_b_ada': 'grad_w', 'grad_norm_ffn1': 'grad_w', 'grad_w_ffn1_in': 'grad_w', 'grad_w_ffn1_out': 'grad_w', 'grad_norm_mix': 'grad_w', 'grad_w_in': 'grad_w', 'grad_conv_w': 'grad_w', 'grad_conv_b': 'grad_w', 'grad_dt_bias': 'grad_w', 'grad_a_log': 'grad_w', 'grad_d_ssd': 'grad_w', 'grad_ssd_norm_w': 'grad_w', 'grad_w_a_proj': 'grad_w', 'grad_s5_lambda_re': 'grad_w', 'grad_s5_lambda_im': 'grad_w', 'grad_s5_b_re': 'grad_w', 'grad_s5_b_im': 'grad_w', 'grad_s5_c_re': 'grad_w', 'grad_s5_c_im': 'grad_w', 'grad_s5_d': 'grad_w', 'grad_s5_log_dt': 'grad_w', 'grad_w_b_glu': 'grad_w', 'grad_w_out': 'grad_w', 'grad_norm_ffn2': 'grad_w', 'grad_w_ffn2_in': 'grad_w', 'grad_w_ffn2_out': 'grad_w', 'grad_norm_final': 'grad_w', 'delta_w_ada': 'delta_w', 'delta_b_ada': 'delta_w', 'delta_norm_ffn1': 'delta_w', 'delta_w_ffn1_in': 'delta_w', 'delta_w_ffn1_out': 'delta_w', 'delta_norm_mix': 'delta_w', 'delta_w_in': 'delta_w', 'delta_conv_w': 'delta_w', 'delta_conv_b': 'delta_w', 'delta_dt_bias': 'delta_w', 'delta_a_log': 'delta_w', 'delta_d_ssd': 'delta_w', 'delta_ssd_norm_w': 'delta_w', 'delta_w_a_proj': 'delta_w', 'delta_s5_lambda_re': 'delta_w', 'delta_s5_lambda_im': 'delta_w', 'delta_s5_b_re': 'delta_w', 'delta_s5_b_im': 'delta_w', 'delta_s5_c_re': 'delta_w', 'delta_s5_c_im': 'delta_w', 'delta_s5_d': 'delta_w', 'delta_s5_log_dt': 'delta_w', 'delta_w_b_glu': 'delta_w', 'delta_w_out': 'delta_w', 'delta_norm_ffn2': 'delta_w', 'delta_w_ffn2_in': 'delta_w', 'delta_w_ffn2_out': 'delta_w', 'delta_norm_final': 'delta_w', 'new_m_w_ada': 'new_m', 'new_m_b_ada': 'new_m', 'new_m_norm_ffn1': 'new_m', 'new_m_w_ffn1_in': 'new_m', 'new_m_w_ffn1_out': 'new_m', 'new_m_norm_mix': 'new_m', 'new_m_w_in': 'new_m', 'new_m_conv_w': 'new_m', 'new_m_conv_b': 'new_m', 'new_m_dt_bias': 'new_m', 'new_m_a_log': 'new_m', 'new_m_d_ssd': 'new_m', 'new_m_ssd_norm_w': 'new_m', 'new_m_w_a_proj': 'new_m', 'new_m_s5_lambda_re': 'new_m', 'new_m_s5_lambda_im': 'new_m', 'new_m_s5_b_re': 'new_m', 'new_m_s5_b_im': 'new_m', 'new_m_s5_c_re': 'new_m', 'new_m_s5_c_im': 'new_m', 'new_m_s5_d': 'new_m', 'new_m_s5_log_dt': 'new_m', 'new_m_w_b_glu': 'new_m', 'new_m_w_out': 'new_m', 'new_m_norm_ffn2': 'new_m', 'new_m_w_ffn2_in': 'new_m', 'new_m_w_ffn2_out': 'new_m', 'new_m_norm_final': 'new_m', 'new_v_w_ada': 'new_v', 'new_v_b_ada': 'new_v', 'new_v_norm_ffn1': 'new_v', 'new_v_w_ffn1_in': 'new_v', 'new_v_w_ffn1_out': 'new_v', 'new_v_norm_mix': 'new_v', 'new_v_w_in': 'new_v', 'new_v_conv_w': 'new_v', 'new_v_conv_b': 'new_v', 'new_v_dt_bias': 'new_v', 'new_v_a_log': 'new_v', 'new_v_d_ssd': 'new_v', 'new_v_ssd_norm_w': 'new_v', 'new_v_w_a_proj': 'new_v', 'new_v_s5_lambda_re': 'new_v', 'new_v_s5_lambda_im': 'new_v', 'new_v_s5_b_re': 'new_v', 'new_v_s5_b_im': 'new_v', 'new_v_s5_c_re': 'new_v', 'new_v_s5_c_im': 'new_v', 'new_v_s5_d': 'new_v', 'new_v_s5_log_dt': 'new_v', 'new_v_w_b_glu': 'new_v', 'new_v_w_out': 'new_v', 'new_v_norm_ffn2': 'new_v', 'new_v_w_ffn2_in': 'new_v', 'new_v_w_ffn2_out': 'new_v', 'new_v_norm_final': 'new_v'}


def _forward(args):
    return _fwd_reference(*[args[k] for k in FWD_PARAMS])


def _output_shape():
    def fwd():
        inp = _fwd_setup_inputs(0)
        return _fwd_reference(*[inp[k] for k in FWD_PARAMS])
    out = _jax.eval_shape(fwd)
    return out.shape, out.dtype

N_MICROBATCH = 1
ADAM_LR = 0.001
ADAM_B1 = 0.9
ADAM_B2 = 0.999
ADAM_EPS = 1e-08
ADAM_WD = 0.01
ADAM_STEP = 10
PER_EXAMPLE_BATCH_AXIS = {'x': 0, 'c': 0, 'loss_target': 0}
SHARED_INPUTS = []
_WEIGHT_DTYPES = {'w_ada': _jnp.float32, 'b_ada': _jnp.float32, 'norm_ffn1': _jnp.float32, 'w_ffn1_in': _jnp.float32, 'w_ffn1_out': _jnp.float32, 'norm_mix': _jnp.float32, 'w_in': _jnp.float32, 'conv_w': _jnp.float32, 'conv_b': _jnp.float32, 'dt_bias': _jnp.float32, 'a_log': _jnp.float32, 'd_ssd': _jnp.float32, 'ssd_norm_w': _jnp.float32, 'w_a_proj': _jnp.float32, 's5_lambda_re': _jnp.float32, 's5_lambda_im': _jnp.float32, 's5_b_re': _jnp.float32, 's5_b_im': _jnp.float32, 's5_c_re': _jnp.float32, 's5_c_im': _jnp.float32, 's5_d': _jnp.float32, 's5_log_dt': _jnp.float32, 'w_b_glu': _jnp.float32, 'w_out': _jnp.float32, 'norm_ffn2': _jnp.float32, 'w_ffn2_in': _jnp.float32, 'w_ffn2_out': _jnp.float32, 'norm_final': _jnp.float32}
MOMENT_SCALE = {'w_ada': 3.373589e-02, 'b_ada': 5.792609e-02, 'norm_ffn1': 5.449070e-02, 'w_ffn1_in': 2.566020e-02, 'w_ffn1_out': 4.145442e-02, 'norm_mix': 5.847965e-02, 'w_in': 2.987773e-02, 'conv_w': 3.399991e-02, 'conv_b': 3.140546e-02, 'dt_bias': 1.201984e-01, 'a_log': 1.770045e-01, 'd_ssd': 1.727411e-01, 'ssd_norm_w': 3.933039e-02, 'w_a_proj': 3.897699e-02, 's5_lambda_re': 4.646913e-03, 's5_lambda_im': 3.199024e-03, 's5_b_re': 1.420659e-03, 's5_b_im': 1.658306e-03, 's5_c_re': 3.264456e-03, 's5_c_im': 2.888380e-03, 's5_d': 3.184844e-02, 's5_log_dt': 6.509160e-01, 'w_b_glu': 1.424623e-02, 'w_out': 4.374507e-02, 'norm_ffn2': 4.515577e-02, 'w_ffn2_in': 2.171700e-02, 'w_ffn2_out': 3.512927e-02, 'norm_final': 3.206585e+01}


def _to_microbatches(a, axis):
    t = _jnp.moveaxis(a, axis, 0)
    t = t.reshape((N_MICROBATCH, t.shape[0] // N_MICROBATCH) + t.shape[1:])
    return _jnp.moveaxis(t, 1, axis + 1)


def setup_inputs(seed: int = 0) -> dict:
    inp = _fwd_setup_inputs(seed)
    key = _jax.random.fold_in(_jax.random.key(seed), 7919)
    shape, _ = _output_shape()
    out = dict(inp)
    out["loss_target"] = _jax.random.normal(_jax.random.fold_in(key, 0), shape, _jnp.float32)
    for i, name in enumerate(TWIN_WEIGHTS):
        w = inp[name].astype(_jnp.float32)
        if MOMENT_SCALE is None:
            s = _jnp.sqrt(_jnp.mean(_jnp.square(w)) + 1e-30)
        else:
            s = MOMENT_SCALE[name]
        km, kv = _jax.random.split(_jax.random.fold_in(key, i + 1))
        out[name] = w
        out["m_" + name] = s * _jax.random.normal(km, w.shape, _jnp.float32)
        out["v_" + name] = (s * s) * _jax.random.uniform(kv, w.shape, _jnp.float32, 0.5, 1.5)
    if N_MICROBATCH > 1:
        for name, axis in PER_EXAMPLE_BATCH_AXIS.items():
            out[name] = _to_microbatches(out[name], axis)
    return {'x': out['x'], 'c': out['c'], 'w_ada': out['w_ada'], 'b_ada': out['b_ada'], 'norm_ffn1': out['norm_ffn1'], 'w_ffn1_in': out['w_ffn1_in'], 'w_ffn1_out': out['w_ffn1_out'], 'norm_mix': out['norm_mix'], 'w_in': out['w_in'], 'conv_w': out['conv_w'], 'conv_b': out['conv_b'], 'dt_bias': out['dt_bias'], 'a_log': out['a_log'], 'd_ssd': out['d_ssd'], 'ssd_norm_w': out['ssd_norm_w'], 'w_a_proj': out['w_a_proj'], 's5_lambda_re': out['s5_lambda_re'], 's5_lambda_im': out['s5_lambda_im'], 's5_b_re': out['s5_b_re'], 's5_b_im': out['s5_b_im'], 's5_c_re': out['s5_c_re'], 's5_c_im': out['s5_c_im'], 's5_d': out['s5_d'], 's5_log_dt': out['s5_log_dt'], 'w_b_glu': out['w_b_glu'], 'w_out': out['w_out'], 'norm_ffn2': out['norm_ffn2'], 'w_ffn2_in': out['w_ffn2_in'], 'w_ffn2_out': out['w_ffn2_out'], 'norm_final': out['norm_final'], 'loss_target': out['loss_target'], 'm_w_ada': out['m_w_ada'], 'm_b_ada': out['m_b_ada'], 'm_norm_ffn1': out['m_norm_ffn1'], 'm_w_ffn1_in': out['m_w_ffn1_in'], 'm_w_ffn1_out': out['m_w_ffn1_out'], 'm_norm_mix': out['m_norm_mix'], 'm_w_in': out['m_w_in'], 'm_conv_w': out['m_conv_w'], 'm_conv_b': out['m_conv_b'], 'm_dt_bias': out['m_dt_bias'], 'm_a_log': out['m_a_log'], 'm_d_ssd': out['m_d_ssd'], 'm_ssd_norm_w': out['m_ssd_norm_w'], 'm_w_a_proj': out['m_w_a_proj'], 'm_s5_lambda_re': out['m_s5_lambda_re'], 'm_s5_lambda_im': out['m_s5_lambda_im'], 'm_s5_b_re': out['m_s5_b_re'], 'm_s5_b_im': out['m_s5_b_im'], 'm_s5_c_re': out['m_s5_c_re'], 'm_s5_c_im': out['m_s5_c_im'], 'm_s5_d': out['m_s5_d'], 'm_s5_log_dt': out['m_s5_log_dt'], 'm_w_b_glu': out['m_w_b_glu'], 'm_w_out': out['m_w_out'], 'm_norm_ffn2': out['m_norm_ffn2'], 'm_w_ffn2_in': out['m_w_ffn2_in'], 'm_w_ffn2_out': out['m_w_ffn2_out'], 'm_norm_final': out['m_norm_final'], 'v_w_ada': out['v_w_ada'], 'v_b_ada': out['v_b_ada'], 'v_norm_ffn1': out['v_norm_ffn1'], 'v_w_ffn1_in': out['v_w_ffn1_in'], 'v_w_ffn1_out': out['v_w_ffn1_out'], 'v_norm_mix': out['v_norm_mix'], 'v_w_in': out['v_w_in'], 'v_conv_w': out['v_conv_w'], 'v_conv_b': out['v_conv_b'], 'v_dt_bias': out['v_dt_bias'], 'v_a_log': out['v_a_log'], 'v_d_ssd': out['v_d_ssd'], 'v_ssd_norm_w': out['v_ssd_norm_w'], 'v_w_a_proj': out['v_w_a_proj'], 'v_s5_lambda_re': out['v_s5_lambda_re'], 'v_s5_lambda_im': out['v_s5_lambda_im'], 'v_s5_b_re': out['v_s5_b_re'], 'v_s5_b_im': out['v_s5_b_im'], 'v_s5_c_re': out['v_s5_c_re'], 'v_s5_c_im': out['v_s5_c_im'], 'v_s5_d': out['v_s5_d'], 'v_s5_log_dt': out['v_s5_log_dt'], 'v_w_b_glu': out['v_w_b_glu'], 'v_w_out': out['v_w_out'], 'v_norm_ffn2': out['v_norm_ffn2'], 'v_w_ffn2_in': out['v_w_ffn2_in'], 'v_w_ffn2_out': out['v_w_ffn2_out'], 'v_norm_final': out['v_norm_final']}


def _loss(weights, diff, rest, loss_target):
    with _jax.named_scope("forward"):
        args = {**rest, TWIN_DIFF_INPUT: diff, **{k: w.astype(_WEIGHT_DTYPES[k]) for k, w in weights.items()}}
        y = _forward(args)
    with _jax.named_scope("loss_head"):
        err = _jnp.square(y.astype(_jnp.float32) - loss_target)
        return 0.5 * _jnp.sum(_jnp.mean(err, axis=-1)) if err.ndim else 0.5 * err


def _adamw(w, g, m, v):
    m = ADAM_B1 * m + (1.0 - ADAM_B1) * g
    v = ADAM_B2 * v + (1.0 - ADAM_B2) * _jnp.square(g)
    m_hat = m / (1.0 - ADAM_B1 ** ADAM_STEP)
    v_hat = v / (1.0 - ADAM_B2 ** ADAM_STEP)
    delta = -ADAM_LR * (m_hat / (_jnp.sqrt(v_hat) + ADAM_EPS) + ADAM_WD * w)
    return delta, m, v


def reference(x, c, w_ada, b_ada, norm_ffn1, w_ffn1_in, w_ffn1_out, norm_mix, w_in, conv_w, conv_b, dt_bias, a_log, d_ssd, ssd_norm_w, w_a_proj, s5_lambda_re, s5_lambda_im, s5_b_re, s5_b_im, s5_c_re, s5_c_im, s5_d, s5_log_dt, w_b_glu, w_out, norm_ffn2, w_ffn2_in, w_ffn2_out, norm_final, loss_target, m_w_ada, m_b_ada, m_norm_ffn1, m_w_ffn1_in, m_w_ffn1_out, m_norm_mix, m_w_in, m_conv_w, m_conv_b, m_dt_bias, m_a_log, m_d_ssd, m_ssd_norm_w, m_w_a_proj, m_s5_lambda_re, m_s5_lambda_im, m_s5_b_re, m_s5_b_im, m_s5_c_re, m_s5_c_im, m_s5_d, m_s5_log_dt, m_w_b_glu, m_w_out, m_norm_ffn2, m_w_ffn2_in, m_w_ffn2_out, m_norm_final, v_w_ada, v_b_ada, v_norm_ffn1, v_w_ffn1_in, v_w_ffn1_out, v_norm_mix, v_w_in, v_conv_w, v_conv_b, v_dt_bias, v_a_log, v_d_ssd, v_ssd_norm_w, v_w_a_proj, v_s5_lambda_re, v_s5_lambda_im, v_s5_b_re, v_s5_b_im, v_s5_c_re, v_s5_c_im, v_s5_d, v_s5_log_dt, v_w_b_glu, v_w_out, v_norm_ffn2, v_w_ffn2_in, v_w_ffn2_out, v_norm_final):
    given = dict(x=x, c=c, w_ada=w_ada, b_ada=b_ada, norm_ffn1=norm_ffn1, w_ffn1_in=w_ffn1_in, w_ffn1_out=w_ffn1_out, norm_mix=norm_mix, w_in=w_in, conv_w=conv_w, conv_b=conv_b, dt_bias=dt_bias, a_log=a_log, d_ssd=d_ssd, ssd_norm_w=ssd_norm_w, w_a_proj=w_a_proj, s5_lambda_re=s5_lambda_re, s5_lambda_im=s5_lambda_im, s5_b_re=s5_b_re, s5_b_im=s5_b_im, s5_c_re=s5_c_re, s5_c_im=s5_c_im, s5_d=s5_d, s5_log_dt=s5_log_dt, w_b_glu=w_b_glu, w_out=w_out, norm_ffn2=norm_ffn2, w_ffn2_in=w_ffn2_in, w_ffn2_out=w_ffn2_out, norm_final=norm_final, loss_target=loss_target, m_w_ada=m_w_ada, m_b_ada=m_b_ada, m_norm_ffn1=m_norm_ffn1, m_w_ffn1_in=m_w_ffn1_in, m_w_ffn1_out=m_w_ffn1_out, m_norm_mix=m_norm_mix, m_w_in=m_w_in, m_conv_w=m_conv_w, m_conv_b=m_conv_b, m_dt_bias=m_dt_bias, m_a_log=m_a_log, m_d_ssd=m_d_ssd, m_ssd_norm_w=m_ssd_norm_w, m_w_a_proj=m_w_a_proj, m_s5_lambda_re=m_s5_lambda_re, m_s5_lambda_im=m_s5_lambda_im, m_s5_b_re=m_s5_b_re, m_s5_b_im=m_s5_b_im, m_s5_c_re=m_s5_c_re, m_s5_c_im=m_s5_c_im, m_s5_d=m_s5_d, m_s5_log_dt=m_s5_log_dt, m_w_b_glu=m_w_b_glu, m_w_out=m_w_out, m_norm_ffn2=m_norm_ffn2, m_w_ffn2_in=m_w_ffn2_in, m_w_ffn2_out=m_w_ffn2_out, m_norm_final=m_norm_final, v_w_ada=v_w_ada, v_b_ada=v_b_ada, v_norm_ffn1=v_norm_ffn1, v_w_ffn1_in=v_w_ffn1_in, v_w_ffn1_out=v_w_ffn1_out, v_norm_mix=v_norm_mix, v_w_in=v_w_in, v_conv_w=v_conv_w, v_conv_b=v_conv_b, v_dt_bias=v_dt_bias, v_a_log=v_a_log, v_d_ssd=v_d_ssd, v_ssd_norm_w=v_ssd_norm_w, v_w_a_proj=v_w_a_proj, v_s5_lambda_re=v_s5_lambda_re, v_s5_lambda_im=v_s5_lambda_im, v_s5_b_re=v_s5_b_re, v_s5_b_im=v_s5_b_im, v_s5_c_re=v_s5_c_re, v_s5_c_im=v_s5_c_im, v_s5_d=v_s5_d, v_s5_log_dt=v_s5_log_dt, v_w_b_glu=v_w_b_glu, v_w_out=v_w_out, v_norm_ffn2=v_norm_ffn2, v_w_ffn2_in=v_w_ffn2_in, v_w_ffn2_out=v_w_ffn2_out, v_norm_final=v_norm_final)
    weights = {n: given[n] for n in TWIN_WEIGHTS}
    shared = {n: given[n] for n in SHARED_INPUTS}
    per_example = {n: given[n] for n in ['x', 'c']}
    grad_fn = _jax.value_and_grad(_loss, argnums=(0, 1))

    def one_microbatch(ex, loss_target):
        ex = dict(ex)
        diff = ex.pop(TWIN_DIFF_INPUT)
        return grad_fn(weights, diff, {**shared, **ex}, loss_target)

    if N_MICROBATCH == 1:
        loss, (grad_w, grad_x) = one_microbatch(per_example, given["loss_target"])
    else:
        def body(carry, xs):
            loss_sum, grad_sum = carry
            l_k, (gw_k, gx_k) = one_microbatch(xs[0], xs[1])
            with _jax.named_scope("update"):
                return (loss_sum + l_k, _jax.tree.map(_jnp.add, grad_sum, gw_k)), gx_k

        init = (_jnp.zeros((), _jnp.float32), _jax.tree.map(_jnp.zeros_like, weights))
        (loss, grad_w), grad_x = _jax.lax.scan(body, init, (per_example, given["loss_target"]))
    with _jax.named_scope("update"):
        delta_w, new_m, new_v = {}, {}, {}
        for n in TWIN_WEIGHTS:
            delta_w[n], new_m[n], new_v[n] = _adamw(weights[n], grad_w[n], given["m_" + n], given["v_" + n])
    return (loss, grad_x, *[grad_w[n] for n in TWIN_WEIGHTS], *[delta_w[n] for n in TWIN_WEIGHTS],
            *[new_m[n] for n in TWIN_WEIGHTS], *[new_v[n] for n in TWIN_WEIGHTS])
```

```python
import functools
import math

import jax
import jax.numpy as jnp
from jax import lax
from jax.experimental import pallas as pl
from jax.experimental.pallas import tpu as pltpu

F32 = jnp.float32
BF16 = jnp.bfloat16
HI = lax.Precision.HIGHEST

N_DEV = 8
D = 2048
D_FF = 5504
D_FFP = 5632
NH = 32
HP = 64
NG = 4
NST = 128
LCH = 128
CONV_DIM = 3072
CONV_K = 4
S5W = 1024
S5NS = 8192
N_ADA = 9
EPS = 1e-6
IN_COLS = 10272
INP = 10752
P_GATES, P_Z, P_XBC, P_U, P_DT = 0, 4096, 6144, 9216, 10240
O_XBC, O_DT, O_U, O_GA, O_GB = 2048, 5120, 5152, 6176, 8224
NEG = -1e30
VMEM_LIMIT = 56 * 1024 * 1024

ADAM_LR, ADAM_B1, ADAM_B2, ADAM_EPS, ADAM_WD, ADAM_STEP = 0.001, 0.9, 0.999, 1e-08, 0.01, 10


def _cp(*sem):
    return pltpu.CompilerParams(dimension_semantics=sem, vmem_limit_bytes=VMEM_LIMIT)


def _tile(dim, pref):
    for t in (2048, 1024, 512, 256, 128):
        if t <= pref and dim % t == 0:
            return t
    return dim


def _vec(w, cb=0):
    return pl.BlockSpec((1, w), lambda *_: (0, cb))


def _row(tm, w, cb=0):
    return pl.BlockSpec((tm, w), lambda i: (i, cb))


def _stats(w):
    return pl.BlockSpec((8, w), lambda *_: (0, 0))


def _sigmoid(x):
    return 1.0 / (1.0 + jnp.exp(-x))


def _softplus(x):
    return jnp.maximum(x, 0.0) + jnp.log1p(jnp.exp(-jnp.abs(x)))


def _peer(k):
    x, y, c = lax.axis_index("x"), lax.axis_index("y"), lax.axis_index("c")
    return (x ^ ((k >> 2) & 1), y ^ ((k >> 1) & 1), c ^ (k & 1))


def _my_id():
    return 4 * lax.axis_index("x") + 2 * lax.axis_index("y") + lax.axis_index("c")


def all_gather(v, name):
    def body(v_ref, o_ref, send_sems, recv_sems, local_sem):
        me = _my_id()
        mine = pltpu.make_async_copy(v_ref, o_ref.at[me], local_sem)
        mine.start()
        copies = []
        for k in range(1, N_DEV):
            cp = pltpu.make_async_remote_copy(
                src_ref=v_ref, dst_ref=o_ref.at[me], send_sem=send_sems.at[k], recv_sem=recv_sems.at[k],
                device_id=_peer(k), device_id_type=pl.DeviceIdType.MESH)
            cp.start()
            copies.append(cp)
        for cp in copies:
            cp.wait()
        mine.wait()

    return pl.pallas_call(
        body, name=name,
        in_specs=[pl.BlockSpec(memory_space=pl.ANY)], out_specs=pl.BlockSpec(memory_space=pl.ANY),
        out_shape=jax.ShapeDtypeStruct((N_DEV,) + v.shape, v.dtype),
        scratch_shapes=[pltpu.SemaphoreType.DMA((N_DEV,)), pltpu.SemaphoreType.DMA((N_DEV,)), pltpu.SemaphoreType.DMA],
    )(v)


def exchange(parts, name):
    def body(p_ref, o_ref, send_sems, recv_sems, local_sem):
        me = _my_id()
        mine = pltpu.make_async_copy(p_ref.at[me], o_ref.at[me], local_sem)
        mine.start()
        copies = []
        for k in range(1, N_DEV):
            px, py, pc = _peer(k)
            cp = pltpu.make_async_remote_copy(
                src_ref=p_ref.at[4 * px + 2 * py + pc], dst_ref=o_ref.at[me], send_sem=send_sems.at[k],
                recv_sem=recv_sems.at[k], device_id=(px, py, pc), device_id_type=pl.DeviceIdType.MESH)
            cp.start()
            copies.append(cp)
        for cp in copies:
            cp.wait()
        mine.wait()

    return pl.pallas_call(
        body, name=name,
        in_specs=[pl.BlockSpec(memory_space=pl.ANY)], out_specs=pl.BlockSpec(memory_space=pl.ANY),
        out_shape=jax.ShapeDtypeStruct(parts.shape, parts.dtype),
        scratch_shapes=[pltpu.SemaphoreType.DMA((N_DEV,)), pltpu.SemaphoreType.DMA((N_DEV,)), pltpu.SemaphoreType.DMA],
    )(parts)


def mm(a, b, *, ta=False, tb=False, out_dtype=F32, tm=512, tn=1024, tk=2048, name):
    if ta:
        kd, m = a.shape
    else:
        m, kd = a.shape
    if tb:
        n, kd2 = b.shape
    else:
        kd2, n = b.shape
    assert kd == kd2, (a.shape, b.shape, ta, tb)
    tm, tn, tk = _tile(m, tm), _tile(n, tn), _tile(kd, tk)
    nk = kd // tk
    dims = (((0,) if ta else (1,), (1,) if tb else (0,)), ((), ()))

    def body(a_ref, b_ref, o_ref, *scr):
        p = lax.dot_general(a_ref[...], b_ref[...], dims, preferred_element_type=F32)
        if nk == 1:
            o_ref[...] = p.astype(o_ref.dtype)
        else:
            acc = scr[0]
            k = pl.program_id(2)

            @pl.when(k == 0)
            def _():
                acc[...] = p

            @pl.when(k > 0)
            def _():
                acc[...] += p

            @pl.when(k == nk - 1)
            def _():
                o_ref[...] = acc[...].astype(o_ref.dtype)

    a_spec = pl.BlockSpec((tk, tm), lambda j, i, k: (k, i)) if ta else pl.BlockSpec((tm, tk), lambda j, i, k: (i, k))
    b_spec = pl.BlockSpec((tn, tk), lambda j, i, k: (j, k)) if tb else pl.BlockSpec((tk, tn), lambda j, i, k: (k, j))
    return pl.pallas_call(
        body, name=name, grid=(n // tn, m // tm, nk), in_specs=[a_spec, b_spec],
        out_specs=pl.BlockSpec((tm, tn), lambda j, i, k: (i, j)),
        out_shape=jax.ShapeDtypeStruct((m, n), out_dtype),
        scratch_shapes=[pltpu.VMEM((tm, tn), F32)] if nk > 1 else [],
        compiler_params=_cp("parallel", "parallel", "arbitrary"),
    )(a, b)


def ada_fwd(c8, w_loc, b_loc, name):
    n = w_loc.shape[1]
    tn = 256

    def body(c_ref, w_ref, b_ref, o_ref):
        cv = c_ref[...]
        ca = cv * _sigmoid(cv)
        o_ref[...] = jnp.dot(ca, w_ref[...], precision=HI, preferred_element_type=F32) + b_ref[...]

    return pl.pallas_call(
        body, name=name, grid=(n // tn,),
        in_specs=[pl.BlockSpec((N_DEV, D), lambda j: (0, 0)), pl.BlockSpec((D, tn), lambda j: (0, j)),
                  pl.BlockSpec((1, tn), lambda j: (0, j))],
        out_specs=pl.BlockSpec((N_DEV, tn), lambda j: (0, j)),
        out_shape=jax.ShapeDtypeStruct((N_DEV, n), F32), compiler_params=_cp("parallel"),
    )(c8, w_loc, b_loc)


def ada_bwd(c8t, dm_loc, name):
    n = dm_loc.shape[1]
    tn = 256

    def body(c_ref, d_ref, o_ref):
        cv = c_ref[...]
        ca = cv * _sigmoid(cv)
        o_ref[...] = jnp.dot(ca, d_ref[...], precision=HI, preferred_element_type=F32)

    return pl.pallas_call(
        body, name=name, grid=(n // tn,),
        in_specs=[pl.BlockSpec((D, N_DEV), lambda j: (0, 0)), pl.BlockSpec((N_DEV, tn), lambda j: (0, j))],
        out_specs=pl.BlockSpec((D, tn), lambda j: (0, j)),
        out_shape=jax.ShapeDtypeStruct((D, n), F32), compiler_params=_cp("parallel"),
    )(c8t, dm_loc)


def mod_fwd(x, nw, mods, shk, sck, *, f=None, gk=None, gscale=1.0, name):
    t = x.shape[0]
    tm = min(256, t)
    res = f is not None

    def body(*refs):
        if res:
            x_ref, f_ref, g_ref, nw_ref, sh_ref, sc_ref, x1_ref, h_ref = refs
            xv = x_ref[...] + (gscale * g_ref[...]) * f_ref[...]
            x1_ref[...] = xv
        else:
            x_ref, nw_ref, sh_ref, sc_ref, h_ref = refs
            xv = x_ref[...]
        r = lax.rsqrt(jnp.mean(xv * xv, axis=-1, keepdims=True) + EPS)
        h_ref[...] = ((xv * r * nw_ref[...]) * (1.0 + sc_ref[...]) + sh_ref[...]).astype(BF16)

    ins = [x] + ([f, mods] if res else []) + [nw, mods, mods]
    specs = [_row(tm, D)] + ([_row(tm, D), _vec(D, gk)] if res else []) + [_vec(D), _vec(D, shk), _vec(D, sck)]
    outs = ([jax.ShapeDtypeStruct((t, D), F32)] if res else []) + [jax.ShapeDtypeStruct((t, D), BF16)]
    ospecs = ([_row(tm, D)] if res else []) + [_row(tm, D)]
    out = pl.pallas_call(body, name=name, grid=(t // tm,), in_specs=specs, out_specs=ospecs, out_shape=outs,
                         compiler_params=_cp("parallel"))(*ins)
    return out if res else out[0]


def final_fwd_bwd(x2, f3, mods, nf, tgt, name):
    t = x2.shape[0]
    tm = min(128, t)

    def body(x_ref, f_ref, g_ref, nf_ref, t_ref, dx_ref, df_ref, st_ref):
        @pl.when(pl.program_id(0) == 0)
        def _():
            st_ref[...] = jnp.zeros_like(st_ref)

        g = 0.5 * g_ref[...]
        fv = f_ref[...]
        xv = x_ref[...] + g * fv
        r = lax.rsqrt(jnp.mean(xv * xv, axis=-1, keepdims=True) + EPS)
        xh = xv * r
        nfv = nf_ref[...]
        e = xh * nfv - t_ref[...]
        st_ref[2:3, :] += jnp.sum(e * e, axis=0, keepdims=True)
        dy = e * (1.0 / D)
        st_ref[0:1, :] += jnp.sum(dy * xh, axis=0, keepdims=True)
        dxh = dy * nfv
        dx = r * (dxh - xh * jnp.mean(dxh * xh, axis=-1, keepdims=True))
        dx_ref[...] = dx
        df_ref[...] = (g * dx).astype(BF16)
        st_ref[1:2, :] += 0.5 * jnp.sum(fv * dx, axis=0, keepdims=True)

    return pl.pallas_call(
        body, name=name, grid=(t // tm,),
        in_specs=[_row(tm, D), _row(tm, D), _vec(D, 8), _vec(D), _row(tm, D)],
        out_specs=[_row(tm, D), _row(tm, D), _stats(D)],
        out_shape=[jax.ShapeDtypeStruct((t, D), F32), jax.ShapeDtypeStruct((t, D), BF16),
                   jax.ShapeDtypeStruct((8, D), F32)],
        compiler_params=_cp("arbitrary"),
    )(x2, f3, mods, nf, tgt)


def mod_bwd(x_in, dh, dx_out, nw, mods, sck, *, fprev=None, gk=None, gscale=1.0, name):
    t = x_in.shape[0]
    tm = min(128, t)
    gate = fprev is not None

    def body(*refs):
        if gate:
            x_ref, dh_ref, dxo_ref, nw_ref, sc_ref, f_ref, g_ref, dx_ref, df_ref, st_ref = refs
        else:
            x_ref, dh_ref, dxo_ref, nw_ref, sc_ref, dx_ref, st_ref = refs

        @pl.when(pl.program_id(0) == 0)
        def _():
            st_ref[...] = jnp.zeros_like(st_ref)

        xv = x_ref[...]
        dhv = dh_ref[...]
        r = lax.rsqrt(jnp.mean(xv * xv, axis=-1, keepdims=True) + EPS)
        xh = xv * r
        nwv = nw_ref[...]
        st_ref[0:1, :] += jnp.sum(dhv, axis=0, keepdims=True)
        st_ref[1:2, :] += jnp.sum(dhv * (xh * nwv), axis=0, keepdims=True)
        dn = dhv * (1.0 + sc_ref[...])
        st_ref[2:3, :] += jnp.sum(dn * xh, axis=0, keepdims=True)
        dxh = dn * nwv
        dx = dxo_ref[...] + r * (dxh - xh * jnp.mean(dxh * xh, axis=-1, keepdims=True))
        dx_ref[...] = dx
        if gate:
            df_ref[...] = ((gscale * g_ref[...]) * dx).astype(BF16)
            st_ref[3:4, :] += gscale * jnp.sum(f_ref[...] * dx, axis=0, keepdims=True)

    ins = [x_in, dh, dx_out, nw, mods] + ([fprev, mods] if gate else [])
    specs = [_row(tm, D), _row(tm, D), _row(tm, D), _vec(D), _vec(D, sck)] + ([_row(tm, D), _vec(D, gk)] if gate else [])
    outs = [jax.ShapeDtypeStruct((t, D), F32)] + ([jax.ShapeDtypeStruct((t, D), BF16)] if gate else []) + \
        [jax.ShapeDtypeStruct((8, D), F32)]
    ospecs = [_row(tm, D)] + ([_row(tm, D)] if gate else []) + [_stats(D)]
    return pl.pallas_call(body, name=name, grid=(t // tm,), in_specs=specs, out_specs=ospecs, out_shape=outs,
                          compiler_params=_cp("arbitrary"))(*ins)


def swiglu_fwd(ab, name):
    t = ab.shape[0]
    tm = min(256, t)
    cw = 512

    def body(ab_ref, o_ref):
        for j in range(D_FFP // cw):
            a = ab_ref[:, j * cw:(j + 1) * cw].astype(F32)
            b = ab_ref[:, D_FFP + j * cw:D_FFP + (j + 1) * cw].astype(F32)
            o_ref[:, j * cw:(j + 1) * cw] = (a * _sigmoid(a) * b).astype(BF16)

    return pl.pallas_call(body, name=name, grid=(t // tm,), in_specs=[_row(tm, 2 * D_FFP)], out_specs=_row(tm, D_FFP),
                          out_shape=jax.ShapeDtypeStruct((t, D_FFP), BF16), compiler_params=_cp("parallel"))(ab)


def swiglu_bwd(ab, dact, name):
    t = ab.shape[0]
    tm = min(256, t)
    cw = 512

    def body(ab_ref, d_ref, o_ref):
        for j in range(D_FFP // cw):
            a = ab_ref[:, j * cw:(j + 1) * cw].astype(F32)
            b = ab_ref[:, D_FFP + j * cw:D_FFP + (j + 1) * cw].astype(F32)
            dv = d_ref[:, j * cw:(j + 1) * cw]
            s = _sigmoid(a)
            o_ref[:, j * cw:(j + 1) * cw] = (dv * b * (s * (1.0 + a * (1.0 - s)))).astype(BF16)
            o_ref[:, D_FFP + j * cw:D_FFP + (j + 1) * cw] = (dv * (a * s)).astype(BF16)

    return pl.pallas_call(body, name=name, grid=(t // tm,), in_specs=[_row(tm, 2 * D_FFP), _row(tm, D_FFP)],
                          out_specs=_row(tm, 2 * D_FFP), out_shape=jax.ShapeDtypeStruct((t, 2 * D_FFP), BF16),
                          compiler_params=_cp("parallel"))(ab, dact)


def _conv_pre(cur, prev8, w, b, tm):
    full = jnp.concatenate([prev8, cur], axis=0)
    pre = b + w[3:4, :] * cur
    for k in range(CONV_K - 1):
        s = CONV_K - 1 - k
        pre = pre + w[k:k + 1, :] * pltpu.roll(full, s, 0)[8:8 + tm, :]
    return pre


def conv_fwd(proj, cw_full, cb_full, name):
    t = proj.shape[0]
    tm = min(256, t)
    cwid = 1024
    cb0 = P_XBC // cwid

    def body(x_ref, p_ref, w_ref, b_ref, o_ref):
        i = pl.program_id(1)
        prev8 = jnp.where(i == 0, 0.0, p_ref[...])
        pre = _conv_pre(x_ref[...], prev8, w_ref[...], b_ref[...], tm)
        o_ref[...] = pre * _sigmoid(pre)

    return pl.pallas_call(
        body, name=name, grid=(CONV_DIM // cwid, t // tm),
        in_specs=[pl.BlockSpec((tm, cwid), lambda j, i: (i, cb0 + j)),
                  pl.BlockSpec((8, cwid), lambda j, i: (jnp.maximum(i * (tm // 8) - 1, 0), cb0 + j)),
                  pl.BlockSpec((CONV_K, cwid), lambda j, i: (0, j)), pl.BlockSpec((1, cwid), lambda j, i: (0, j))],
        out_specs=pl.BlockSpec((tm, cwid), lambda j, i: (i, j)),
        out_shape=jax.ShapeDtypeStruct((t, CONV_DIM), F32), compiler_params=_cp("parallel", "parallel"),
    )(proj, proj, cw_full, cb_full)


def conv_bwd_pre(proj, dxc, cw_full, cb_full, name):
    t = proj.shape[0]
    tm = min(256, t)
    cwid = 1024
    cb0 = P_XBC // cwid

    def body(x_ref, p_ref, d_ref, w_ref, b_ref, o_ref, st_ref):
        i = pl.program_id(1)

        @pl.when(i == 0)
        def _():
            st_ref[...] = jnp.zeros_like(st_ref)

        cur = x_ref[...]
        prev8 = jnp.where(i == 0, 0.0, p_ref[...])
        pre = _conv_pre(cur, prev8, w_ref[...], b_ref[...], tm)
        s = _sigmoid(pre)
        dpre = d_ref[...] * (s * (1.0 + pre * (1.0 - s)))
        o_ref[...] = dpre
        st_ref[4:5, :] += jnp.sum(dpre, axis=0, keepdims=True)
        st_ref[3:4, :] += jnp.sum(dpre * cur, axis=0, keepdims=True)
        full = jnp.concatenate([prev8, cur], axis=0)
        for k in range(CONV_K - 1):
            sft = CONV_K - 1 - k
            st_ref[k:k + 1, :] += jnp.sum(dpre * pltpu.roll(full, sft, 0)[8:8 + tm, :], axis=0, keepdims=True)

    return pl.pallas_call(
        body, name=name, grid=(CONV_DIM // cwid, t // tm),
        in_specs=[pl.BlockSpec((tm, cwid), lambda j, i: (i, cb0 + j)),
                  pl.BlockSpec((8, cwid), lambda j, i: (jnp.maximum(i * (tm // 8) - 1, 0), cb0 + j)),
                  pl.BlockSpec((tm, cwid), lambda j, i: (i, j)),
                  pl.BlockSpec((CONV_K, cwid), lambda j, i: (0, j)), pl.BlockSpec((1, cwid), lambda j, i: (0, j))],
        out_specs=[pl.BlockSpec((tm, cwid), lambda j, i: (i, j)), pl.BlockSpec((8, cwid), lambda j, i: (0, j))],
        out_shape=[jax.ShapeDtypeStruct((t, CONV_DIM), F32), jax.ShapeDtypeStruct((8, CONV_DIM), F32)],
        compiler_params=_cp("parallel", "arbitrary"),
    )(proj, proj, dxc, cw_full, cb_full)


def conv_bwd_in(dpre, cw_full, name):
    t = dpre.shape[0]
    tm = min(256, t)
    cwid = 1024
    nt = t // tm

    def body(d_ref, n_ref, w_ref, o_ref):
        i = pl.program_id(1)
        cur = d_ref[...]
        nxt = jnp.where(i == nt - 1, 0.0, n_ref[...])
        full = jnp.concatenate([cur, nxt], axis=0)
        w = w_ref[...]
        acc = w[3:4, :] * cur
        for k in range(CONV_K - 1):
            s = CONV_K - 1 - k
            acc = acc + w[k:k + 1, :] * pltpu.roll(full, tm + 8 - s, 0)[0:tm, :]
        o_ref[...] = acc.astype(BF16)

    return pl.pallas_call(
        body, name=name, grid=(CONV_DIM // cwid, nt),
        in_specs=[pl.BlockSpec((tm, cwid), lambda j, i: (i, j)),
                  pl.BlockSpec((8, cwid), lambda j, i: (jnp.minimum((i + 1) * (tm // 8), t // 8 - 1), j)),
                  pl.BlockSpec((CONV_K, cwid), lambda j, i: (0, j))],
        out_specs=pl.BlockSpec((tm, cwid), lambda j, i: (i, j)),
        out_shape=jax.ShapeDtypeStruct((t, CONV_DIM), BF16), compiler_params=_cp("parallel", "parallel"),
    )(dpre, dpre, cw_full)


def _ssd_common(dt_ref, dtb_ref, al_ref, exp_ref, tri_ref):
    lane = lax.broadcasted_iota(jnp.int32, (1, 128), 1)
    a_row = jnp.where(lane < NH, -jnp.exp(al_ref[...]), 0.0)
    zraw = dt_ref[...] + dtb_ref[...]
    dtv = _softplus(zraw)
    cs = jnp.dot(tri_ref[...], dtv * a_row, precision=HI, preferred_element_type=F32)
    e = exp_ref[...]
    csx = jnp.dot(cs, e, precision=HI, preferred_element_type=F32)
    dtx = jnp.dot(dtv, e, precision=HI, preferred_element_type=F32)
    return a_row, zraw, dtv, cs, csx, dtx


def _nt(a, b):
    return lax.dot_general(a, b, (((1,), (1,)), ((), ())), preferred_element_type=F32)


def _dot(a, b):
    return jnp.dot(a, b, preferred_element_type=F32)


def ssd_fwd(xc, proj, dtb_row, alog_row, dx_row, expm, tri, name):
    t = xc.shape[0]
    nc = t // LCH

    def body(xs_ref, bm_ref, cm_ref, dt_ref, dtb_ref, al_ref, dxr_ref, exp_ref, tri_ref, y_ref, hs_ref, h_scr):
        @pl.when(pl.program_id(0) == 0)
        def _():
            h_scr[...] = jnp.zeros_like(h_scr)

        _, _, _, cs, csx, dtx = _ssd_common(dt_ref, dtb_ref, al_ref, exp_ref, tri_ref)
        cst = cs.T
        csl = csx[LCH - 1:LCH, :]
        xs = xs_ref[...]
        xd = xs * dtx
        xdw = xd * jnp.exp(csl - csx)
        ecs = jnp.exp(csx)
        ecl = jnp.exp(csl)
        tril = lax.broadcasted_iota(jnp.int32, (LCH, LCH), 0) >= lax.broadcasted_iota(jnp.int32, (LCH, LCH), 1)
        hs_ref[...] = h_scr[...]
        for g in range(NG):
            gc = slice(g * 512, (g + 1) * 512)
            bm = bm_ref[:, g * NST:(g + 1) * NST]
            cmb = cm_ref[:, g * NST:(g + 1) * NST].astype(BF16)
            gm = _nt(cmb, bm.astype(BF16))
            hg = h_scr[:, gc]
            yo = _dot(cmb, hg.astype(BF16)) * ecs[:, gc]
            st = _dot(bm.T.astype(BF16), xdw[:, gc].astype(BF16))
            for r in range(8):
                h = g * 8 + r
                hc = slice(h * HP, (h + 1) * HP)
                seg = cs[:, h:h + 1] - cst[h:h + 1, :]
                m = (gm * jnp.exp(jnp.where(tril, seg, NEG))).astype(BF16)
                yd = _dot(m, xd[:, hc].astype(BF16))
                y_ref[:, hc] = yd + yo[:, r * HP:(r + 1) * HP] + dxr_ref[:, hc] * xs[:, hc]
            h_scr[:, gc] = ecl[:, gc] * hg + st

    return pl.pallas_call(
        body, name=name, grid=(nc,),
        in_specs=[pl.BlockSpec((LCH, 2048), lambda c: (c, 0)), pl.BlockSpec((LCH, 512), lambda c: (c, 4)),
                  pl.BlockSpec((LCH, 512), lambda c: (c, 5)), pl.BlockSpec((LCH, 128), lambda c: (c, P_DT // 128)),
                  _vec(128), _vec(128), _vec(2048), pl.BlockSpec((128, 2048), lambda c: (0, 0)),
                  pl.BlockSpec((LCH, LCH), lambda c: (0, 0))],
        out_specs=[pl.BlockSpec((LCH, 2048), lambda c: (c, 0)), pl.BlockSpec((None, NST, 2048), lambda c: (c, 0, 0))],
        out_shape=[jax.ShapeDtypeStruct((t, 2048), F32), jax.ShapeDtypeStruct((nc, NST, 2048), F32)],
        scratch_shapes=[pltpu.VMEM((NST, 2048), F32)],
        compiler_params=_cp("arbitrary"),
    )(xc, xc, xc, proj, dtb_row, alog_row, dx_row, expm, tri)


def ssd_bwd(xc, proj, hsave, dy, dtb_row, alog_row, dx_row, expm, tri, name):
    t = xc.shape[0]
    nc = t // LCH

    def body(xs_ref, bm_ref, cm_ref, dt_ref, hs_ref, dy_ref, dtb_ref, al_ref, dxr_ref, exp_ref, tri_ref,
             dxc_ref, ddt_ref, st_ref, dh_scr, dxd_scr, dcsx_scr):
        @pl.when(pl.program_id(0) == 0)
        def _():
            dh_scr[...] = jnp.zeros_like(dh_scr)
            st_ref[...] = jnp.zeros_like(st_ref)

        a_row, zraw, dtv, cs, csx, dtx = _ssd_common(dt_ref, dtb_ref, al_ref, exp_ref, tri_ref)
        e = exp_ref[...]
        cst = cs.T
        csl = csx[LCH - 1:LCH, :]
        xs = xs_ref[...]
        xd = xs * dtx
        wend = jnp.exp(csl - csx)
        xdw = xd * wend
        ecs = jnp.exp(csx)
        ecl = jnp.exp(csl)
        ri = lax.broadcasted_iota(jnp.int32, (LCH, LCH), 0)
        ci = lax.broadcasted_iota(jnp.int32, (LCH, LCH), 1)
        tril = ri >= ci
        triu = ri <= ci
        lane = lax.broadcasted_iota(jnp.int32, (1, 128), 1)
        dyv = dy_ref[...]
        dxr = dxr_ref[...]
        st_ref[2:3, :] += lax.dot_general(jnp.sum(dyv * xs, axis=0, keepdims=True), e, (((1,), (1,)), ((), ())),
                                          precision=HI, preferred_element_type=F32)
        dcs = jnp.zeros((LCH, 128), F32)
        for g in range(NG):
            gc = slice(g * 512, (g + 1) * 512)
            bmb = bm_ref[:, g * NST:(g + 1) * NST].astype(BF16)
            cm = cm_ref[:, g * NST:(g + 1) * NST]
            cmb = cm.astype(BF16)
            hg = hs_ref[:, gc]
            hgb = hg.astype(BF16)
            dhc = dh_scr[:, gc]
            dhcb = dhc.astype(BF16)
            dyg = dyv[:, gc]
            yo = _dot(cmb, hgb) * ecs[:, gc]
            dq = (dyg * ecs[:, gc]).astype(BF16)
            dcm = _nt(dq, hgb)
            dh_yo = _dot(cm.T.astype(BF16), dq)
            dxdw = _dot(bmb, dhcb)
            dbm = _nt(xdw[:, gc].astype(BF16), dhcb)
            tt = dxdw * xdw[:, gc]
            dcsx_g = dyg * yo - tt
            dcsl_g = jnp.sum(tt, axis=0, keepdims=True) + jnp.sum(dhc * hg, axis=0, keepdims=True) * ecl[:, gc]
            dxd_scr[:, gc] = dxdw * wend[:, gc]
            dh_scr[:, gc] = ecl[:, gc] * dhc + dh_yo
            gm = _nt(cmb, bmb)
            gmt = _nt(bmb, cmb)
            dg = jnp.zeros((LCH, LCH), F32)
            dgt = jnp.zeros((LCH, LCH), F32)
            for r in range(8):
                h = g * 8 + r
                hc = slice(h * HP, (h + 1) * HP)
                seg = cs[:, h:h + 1] - cst[h:h + 1, :]
                lm = jnp.exp(jnp.where(tril, seg, NEG))
                lmt = jnp.exp(jnp.where(triu, -seg, NEG))
                mm_ = gm * lm
                mmt = gmt * lmt
                xdh = xd[:, hc].astype(BF16)
                dyh = dyv[:, hc].astype(BF16)
                dm = _nt(dyh, xdh)
                dmt = _nt(xdh, dyh)
                dxd_scr[:, hc] += _dot(mmt.astype(BF16), dyh)
                rs = jnp.sum(dm * mm_, axis=1, keepdims=True) - jnp.sum(dmt * mmt, axis=1, keepdims=True)
                dcs = dcs + rs * jnp.where(lane == h, 1.0, 0.0)
                dg = dg + dm * lm
                dgt = dgt + dmt * lmt
            dcm = dcm + _dot(dg.astype(BF16), bmb)
            dbm = dbm + _dot(dgt.astype(BF16), cmb)
            dxc_ref[:, 2048 + g * NST:2048 + (g + 1) * NST] = dbm
            dxc_ref[:, 2560 + g * NST:2560 + (g + 1) * NST] = dcm
            dcsx_scr[:, gc] = dcsx_g
            dcsx_scr[LCH - 1:LCH, gc] += dcsl_g
        dxd = dxd_scr[...]
        dxc_ref[:, 0:2048] = dxr * dyv + dxd * dtx
        ddtv = lax.dot_general(dxd * xs, e, (((1,), (1,)), ((), ())), precision=HI, preferred_element_type=F32)
        dcs = dcs + lax.dot_general(dcsx_scr[...], e, (((1,), (1,)), ((), ())), precision=HI,
                                    preferred_element_type=F32)
        dda = lax.dot_general(tri_ref[...], dcs, (((0,), (0,)), ((), ())), precision=HI, preferred_element_type=F32)
        ddtv = ddtv + dda * a_row
        st_ref[0:1, :] += jnp.sum(dda * dtv, axis=0, keepdims=True) * a_row
        ddt = ddtv * _sigmoid(zraw)
        ddt_ref[...] = ddt
        st_ref[1:2, :] += jnp.sum(ddt, axis=0, keepdims=True)

    rc = lambda c: nc - 1 - c
    return pl.pallas_call(
        body, name=name, grid=(nc,),
        in_specs=[pl.BlockSpec((LCH, 2048), lambda c: (rc(c), 0)), pl.BlockSpec((LCH, 512), lambda c: (rc(c), 4)),
                  pl.BlockSpec((LCH, 512), lambda c: (rc(c), 5)),
                  pl.BlockSpec((LCH, 128), lambda c: (rc(c), P_DT // 128)),
                  pl.BlockSpec((None, NST, 2048), lambda c: (rc(c), 0, 0)),
                  pl.BlockSpec((LCH, 2048), lambda c: (rc(c), 0)),
                  _vec(128), _vec(128), _vec(2048), pl.BlockSpec((128, 2048), lambda c: (0, 0)),
                  pl.BlockSpec((LCH, LCH), lambda c: (0, 0))],
        out_specs=[pl.BlockSpec((LCH, CONV_DIM), lambda c: (rc(c), 0)), pl.BlockSpec((LCH, 128), lambda c: (rc(c), 0)),
                   _stats(128)],
        out_shape=[jax.ShapeDtypeStruct((t, CONV_DIM), F32), jax.ShapeDtypeStruct((t, 128), F32),
                   jax.ShapeDtypeStruct((8, 128), F32)],
        scratch_shapes=[pltpu.VMEM((NST, 2048), F32), pltpu.VMEM((LCH, 2048), F32), pltpu.VMEM((LCH, 2048), F32)],
        compiler_params=_cp("arbitrary"),
    )(xc, xc, xc, proj, hsave, dy, dtb_row, alog_row, dx_row, expm, tri)


def ssd_out_fwd(y, proj, nw, name):
    t = y.shape[0]
    tm = min(256, t)

    def body(y_ref, z_ref, nw_ref, o_ref):
        for g in range(NG):
            gc = slice(g * 512, (g + 1) * 512)
            z = z_ref[:, gc]
            yz = y_ref[:, gc] * (z * _sigmoid(z))
            r = lax.rsqrt(jnp.mean(yz * yz, axis=-1, keepdims=True) + EPS)
            o_ref[:, gc] = (yz * r * nw_ref[:, gc]).astype(BF16)

    return pl.pallas_call(body, name=name, grid=(t // tm,),
                          in_specs=[_row(tm, 2048), _row(tm, 2048, P_Z // 2048), _vec(2048)],
                          out_specs=_row(tm, 2048), out_shape=jax.ShapeDtypeStruct((t, 2048), BF16),
                          compiler_params=_cp("parallel"))(y, proj, nw)


def ssd_out_bwd(y, proj, dya, nw, name):
    t = y.shape[0]
    tm = min(256, t)

    def body(y_ref, z_ref, d_ref, nw_ref, dy_ref, dz_ref, st_ref):
        @pl.when(pl.program_id(0) == 0)
        def _():
            st_ref[...] = jnp.zeros_like(st_ref)

        for g in range(NG):
            gc = slice(g * 512, (g + 1) * 512)
            z = z_ref[:, gc]
            yv = y_ref[:, gc]
            s = _sigmoid(z)
            sz = z * s
            yz = yv * sz
            r = lax.rsqrt(jnp.mean(yz * yz, axis=-1, keepdims=True) + EPS)
            yzn = yz * r
            dv = d_ref[:, gc]
            st_ref[0:1, gc] += jnp.sum(dv * yzn, axis=0, keepdims=True)
            dyn = dv * nw_ref[:, gc]
            dyz = r * (dyn - yzn * jnp.mean(dyn * yzn, axis=-1, keepdims=True))
            dy_ref[:, gc] = dyz * sz
            dz_ref[:, gc] = (dyz * yv * (s * (1.0 + z * (1.0 - s)))).astype(BF16)

    return pl.pallas_call(
        body, name=name, grid=(t // tm,),
        in_specs=[_row(tm, 2048), _row(tm, 2048, P_Z // 2048), _row(tm, 2048), _vec(2048)],
        out_specs=[_row(tm, 2048), _row(tm, 2048), _stats(2048)],
        out_shape=[jax.ShapeDtypeStruct((t, 2048), F32), jax.ShapeDtypeStruct((t, 2048), BF16),
                   jax.ShapeDtypeStruct((8, 2048), F32)],
        compiler_params=_cp("arbitrary"))(y, proj, dya, nw)


def s5_in(u, bsg, name):
    t = u.shape[0]
    tm = min(512, t)

    def body(u_ref, b_ref, o_ref):
        o_ref[...] = _dot(u_ref[...].astype(BF16), b_ref[...])

    return pl.pallas_call(
        body, name=name, grid=(8, t // tm),
        in_specs=[pl.BlockSpec((tm, 128), lambda s, i: (i, s)), pl.BlockSpec((None, 128, 1024), lambda s, i: (s, 0, 0))],
        out_specs=pl.BlockSpec((tm, 1024), lambda s, i: (i, s)),
        out_shape=jax.ShapeDtypeStruct((t, S5NS), F32), compiler_params=_cp("parallel", "parallel"))(u, bsg)


def s5_out(s, csg, u, d_row, name):
    t = u.shape[0]
    tm = min(512, t)

    def body(s_ref, c_ref, u_ref, d_ref, o_ref):
        o_ref[...] = _dot(s_ref[...].astype(BF16), c_ref[...]) + d_ref[...] * u_ref[...]

    return pl.pallas_call(
        body, name=name, grid=(8, t // tm),
        in_specs=[pl.BlockSpec((tm, 1024), lambda s, i: (i, s)), pl.BlockSpec((None, 1024, 128), lambda s, i: (s, 0, 0)),
                  pl.BlockSpec((tm, 128), lambda s, i: (i, s)), pl.BlockSpec((1, 128), lambda s, i: (0, s))],
        out_specs=pl.BlockSpec((tm, 128), lambda s, i: (i, s)),
        out_shape=jax.ShapeDtypeStruct((t, S5W), F32), compiler_params=_cp("parallel", "parallel"))(s, csg, u, d_row)


def s5_out_bwd(dy, csg, s, name):
    t = dy.shape[0]
    tm = min(512, t)

    def body(dy_ref, c_ref, s_ref, e_ref, dc_ref):
        @pl.when(pl.program_id(1) == 0)
        def _():
            dc_ref[...] = jnp.zeros_like(dc_ref)

        dyb = dy_ref[...].astype(BF16)
        e_ref[...] = _nt(dyb, c_ref[...])
        dc_ref[...] += lax.dot_general(s_ref[...].astype(BF16), dyb, (((0,), (0,)), ((), ())),
                                       preferred_element_type=F32)

    return pl.pallas_call(
        body, name=name, grid=(8, t // tm),
        in_specs=[pl.BlockSpec((tm, 128), lambda s, i: (i, s)), pl.BlockSpec((None, 1024, 128), lambda s, i: (s, 0, 0)),
                  pl.BlockSpec((tm, 1024), lambda s, i: (i, s))],
        out_specs=[pl.BlockSpec((tm, 1024), lambda s, i: (i, s)), pl.BlockSpec((None, 1024, 128), lambda s, i: (s, 0, 0))],
        out_shape=[jax.ShapeDtypeStruct((t, S5NS), F32), jax.ShapeDtypeStruct((8, 1024, 128), F32)],
        compiler_params=_cp("parallel", "arbitrary"))(dy, csg, s)


def s5_in_bwd(lam, bsg, u, dy, d_row, name):
    t = u.shape[0]
    tm = min(512, t)

    def body(l_ref, b_ref, u_ref, dy_ref, d_ref, du_ref, db_ref, dd_ref):
        @pl.when(pl.program_id(1) == 0)
        def _():
            db_ref[...] = jnp.zeros_like(db_ref)
            dd_ref[...] = jnp.zeros_like(dd_ref)

        lb = l_ref[...].astype(BF16)
        uv = u_ref[...]
        dyv = dy_ref[...]
        du_ref[...] = _nt(lb, b_ref[...]) + d_ref[...] * dyv
        db_ref[...] += lax.dot_general(uv.astype(BF16), lb, (((0,), (0,)), ((), ())), preferred_element_type=F32)
        dd_ref[...] += jnp.sum(dyv * uv, axis=0, keepdims=True)

    return pl.pallas_call(
        body, name=name, grid=(8, t // tm),
        in_specs=[pl.BlockSpec((tm, 1024), lambda s, i: (i, s)), pl.BlockSpec((None, 128, 1024), lambda s, i: (s, 0, 0)),
                  pl.BlockSpec((tm, 128), lambda s, i: (i, s)), pl.BlockSpec((tm, 128), lambda s, i: (i, s)),
                  pl.BlockSpec((1, 128), lambda s, i: (0, s))],
        out_specs=[pl.BlockSpec((tm, 128), lambda s, i: (i, s)), pl.BlockSpec((None, 128, 1024), lambda s, i: (s, 0, 0)),
                   pl.BlockSpec((1, 128), lambda s, i: (0, s))],
        out_shape=[jax.ShapeDtypeStruct((t, S5W), F32), jax.ShapeDtypeStruct((8, 128, 1024), F32),
                   jax.ShapeDtypeStruct((1, S5W), F32)],
        compiler_params=_cp("parallel", "arbitrary"))(lam, bsg, u, dy, d_row)


def _cstep(ar, ai, sr, si, br, bi):
    return ar * sr - ai * si + br, ar * si + ai * sr + bi


def s5_scan_ends(b3, a_re, a_im, reverse, name):
    lseg = b3.shape[0]
    ti = min(128, lseg)
    nb = lseg // ti
    sgn = -1.0 if reverse else 1.0

    def body(b_ref, ar_ref, ai_ref, o_ref, sr_scr, si_scr):
        tb = pl.program_id(1)

        @pl.when(tb == 0)
        def _():
            sr_scr[...] = jnp.zeros_like(sr_scr)
            si_scr[...] = jnp.zeros_like(si_scr)

        ar = ar_ref[...]
        ai = sgn * ai_ref[...]

        def step(k, carry):
            tt = ti - 1 - k if reverse else k
            return _cstep(ar, ai, carry[0], carry[1], b_ref[tt, :, 0:512], b_ref[tt, :, 512:1024])

        sr, si = lax.fori_loop(0, ti, step, (sr_scr[...], si_scr[...]), unroll=8)
        sr_scr[...] = sr
        si_scr[...] = si

        @pl.when(tb == nb - 1)
        def _():
            o_ref[:, 0:512] = sr
            o_ref[:, 512:1024] = si

    tmap = (lambda s, tb: (nb - 1 - tb, 0, s)) if reverse else (lambda s, tb: (tb, 0, s))
    return pl.pallas_call(
        body, name=name, grid=(8, nb),
        in_specs=[pl.BlockSpec((ti, 8, 1024), tmap), pl.BlockSpec((None, 8, 512), lambda s, tb: (s, 0, 0)),
                  pl.BlockSpec((None, 8, 512), lambda s, tb: (s, 0, 0))],
        out_specs=pl.BlockSpec((None, 8, 1024), lambda s, tb: (s, 0, 0)),
        out_shape=jax.ShapeDtypeStruct((8, 8, 1024), F32),
        scratch_shapes=[pltpu.VMEM((8, 512), F32), pltpu.VMEM((8, 512), F32)],
        compiler_params=_cp("parallel", "arbitrary"))(b3, a_re, a_im)


def s5_scan_init(ends, a_re, a_im, lseg, reverse, name):
    nsq = int(math.log2(lseg))
    assert 2 ** nsq == lseg
    sgn = -1.0 if reverse else 1.0
    order = list(range(7, -1, -1)) if reverse else list(range(8))

    def body(e_ref, ar_ref, ai_ref, o_ref):
        pr = ar_ref[0:1, :]
        pi = sgn * ai_ref[0:1, :]
        for _ in range(nsq):
            pr, pi = pr * pr - pi * pi, 2.0 * pr * pi
        prev_r = jnp.zeros((1, 512), F32)
        prev_i = jnp.zeros((1, 512), F32)
        j0 = order[0]
        o_ref[j0:j0 + 1, 0:512] = prev_r
        o_ref[j0:j0 + 1, 512:1024] = prev_i
        for idx in range(1, 8):
            j, jp = order[idx], order[idx - 1]
            prev_r, prev_i = _cstep(pr, pi, prev_r, prev_i, e_ref[jp:jp + 1, 0:512], e_ref[jp:jp + 1, 512:1024])
            o_ref[j:j + 1, 0:512] = prev_r
            o_ref[j:j + 1, 512:1024] = prev_i

    return pl.pallas_call(
        body, name=name, grid=(8,),
        in_specs=[pl.BlockSpec((None, 8, 1024), lambda s: (s, 0, 0)), pl.BlockSpec((None, 8, 512), lambda s: (s, 0, 0)),
                  pl.BlockSpec((None, 8, 512), lambda s: (s, 0, 0))],
        out_specs=pl.BlockSpec((None, 8, 1024), lambda s: (s, 0, 0)),
        out_shape=jax.ShapeDtypeStruct((8, 8, 1024), F32), compiler_params=_cp("parallel"))(ends, a_re, a_im)


def s5_scan_fwd(b3, init, a_re, a_im, name):
    lseg = b3.shape[0]
    ti = min(128, lseg)
    nb = lseg // ti

    def body(b_ref, i_ref, ar_ref, ai_ref, o_ref, sr_scr, si_scr):
        @pl.when(pl.program_id(1) == 0)
        def _():
            sr_scr[...] = i_ref[:, 0:512]
            si_scr[...] = i_ref[:, 512:1024]

        ar = ar_ref[...]
        ai = ai_ref[...]

        def step(k, carry):
            nr, ni = _cstep(ar, ai, carry[0], carry[1], b_ref[k, :, 0:512], b_ref[k, :, 512:1024])
            o_ref[k, :, 0:512] = nr
            o_ref[k, :, 512:1024] = ni
            return nr, ni

        sr, si = lax.fori_loop(0, ti, step, (sr_scr[...], si_scr[...]), unroll=8)
        sr_scr[...] = sr
        si_scr[...] = si

    return pl.pallas_call(
        body, name=name, grid=(8, nb),
        in_specs=[pl.BlockSpec((ti, 8, 1024), lambda s, tb: (tb, 0, s)), pl.BlockSpec((None, 8, 1024), lambda s, tb: (s, 0, 0)),
                  pl.BlockSpec((None, 8, 512), lambda s, tb: (s, 0, 0)), pl.BlockSpec((None, 8, 512), lambda s, tb: (s, 0, 0))],
        out_specs=pl.BlockSpec((ti, 8, 1024), lambda s, tb: (tb, 0, s)),
        out_shape=jax.ShapeDtypeStruct(b3.shape, F32),
        scratch_shapes=[pltpu.VMEM((8, 512), F32), pltpu.VMEM((8, 512), F32)],
        compiler_params=_cp("parallel", "arbitrary"))(b3, init, a_re, a_im)


def s5_scan_bwd(e3, linit, s3, sinit, a_re, a_im, name):
    lseg = e3.shape[0]
    ti = min(128, lseg)
    nb = lseg // ti

    def body(e_ref, li_ref, s_ref, sh_ref, si0_ref, ar_ref, ai_ref, o_ref, da_ref, lr_scr, lim_scr):
        tb = pl.program_id(1)

        @pl.when(tb == 0)
        def _():
            lr_scr[...] = li_ref[:, 0:512]
            lim_scr[...] = li_ref[:, 512:1024]
            da_ref[...] = jnp.zeros_like(da_ref)

        ar = ar_ref[...]
        ai = -ai_ref[...]

        def one(tt, lr, li, dar, dai, spr, spi):
            nr, ni = _cstep(ar, ai, lr, li, e_ref[tt, :, 0:512], e_ref[tt, :, 512:1024])
            o_ref[tt, :, 0:512] = nr
            o_ref[tt, :, 512:1024] = ni
            return nr, ni, dar + nr * spr + ni * spi, dai + ni * spr - nr * spi

        def step(k, carry):
            tt = ti - 1 - k
            return one(tt, *carry, s_ref[tt - 1, :, 0:512], s_ref[tt - 1, :, 512:1024])

        z = jnp.zeros((8, 512), F32)
        lr, li, dar, dai = lax.fori_loop(0, ti - 1, step, (lr_scr[...], lim_scr[...], z, z), unroll=8)
        first = tb == nb - 1
        spr = jnp.where(first, si0_ref[:, 0:512], sh_ref[0, :, 0:512])
        spi = jnp.where(first, si0_ref[:, 512:1024], sh_ref[0, :, 512:1024])
        lr, li, dar, dai = one(0, lr, li, dar, dai, spr, spi)
        lr_scr[...] = lr
        lim_scr[...] = li
        da_ref[:, 0:512] += dar
        da_ref[:, 512:1024] += dai

    rb = lambda tb: nb - 1 - tb
    return pl.pallas_call(
        body, name=name, grid=(8, nb),
        in_specs=[pl.BlockSpec((ti, 8, 1024), lambda s, tb: (rb(tb), 0, s)),
                  pl.BlockSpec((None, 8, 1024), lambda s, tb: (s, 0, 0)),
                  pl.BlockSpec((ti, 8, 1024), lambda s, tb: (rb(tb), 0, s)),
                  pl.BlockSpec((1, 8, 1024), lambda s, tb: (jnp.maximum(rb(tb) * ti - 1, 0), 0, s)),
                  pl.BlockSpec((None, 8, 1024), lambda s, tb: (s, 0, 0)),
                  pl.BlockSpec((None, 8, 512), lambda s, tb: (s, 0, 0)), pl.BlockSpec((None, 8, 512), lambda s, tb: (s, 0, 0))],
        out_specs=[pl.BlockSpec((ti, 8, 1024), lambda s, tb: (rb(tb), 0, s)),
                   pl.BlockSpec((None, 8, 1024), lambda s, tb: (s, 0, 0))],
        out_shape=[jax.ShapeDtypeStruct(e3.shape, F32), jax.ShapeDtypeStruct((8, 8, 1024), F32)],
        scratch_shapes=[pltpu.VMEM((8, 512), F32), pltpu.VMEM((8, 512), F32)],
        compiler_params=_cp("parallel", "arbitrary"))(e3, linit, s3, s3, sinit, a_re, a_im)


_GC = math.sqrt(2.0 / math.pi)


def gelu_fwd(y, name):
    t, w = y.shape
    tm = min(512, t)

    def body(y_ref, o_ref):
        v = y_ref[...]
        o_ref[...] = (0.5 * v * (1.0 + jnp.tanh(_GC * (v + 0.044715 * v * v * v)))).astype(BF16)

    return pl.pallas_call(body, name=name, grid=(t // tm,), in_specs=[_row(tm, w)], out_specs=_row(tm, w),
                          out_shape=jax.ShapeDtypeStruct((t, w), BF16), compiler_params=_cp("parallel"))(y)


def gelu_bwd(y, dg, name):
    t, w = y.shape
    tm = min(512, t)

    def body(y_ref, d_ref, o_ref):
        v = y_ref[...]
        th = jnp.tanh(_GC * (v + 0.044715 * v * v * v))
        o_ref[...] = d_ref[...] * (0.5 * (1.0 + th) + 0.5 * v * (1.0 - th * th) * _GC * (1.0 + 3.0 * 0.044715 * v * v))

    return pl.pallas_call(body, name=name, grid=(t // tm,), in_specs=[_row(tm, w), _row(tm, w)], out_specs=_row(tm, w),
                          out_shape=jax.ShapeDtypeStruct((t, w), F32), compiler_params=_cp("parallel"))(y, dg)


def merge_fwd(proj, pa, glu, name):
    t = pa.shape[0]
    tm = min(256, t)

    def body(g_ref, pa_ref, glu_ref, o_ref):
        pb = glu_ref[:, 0:D] * _sigmoid(glu_ref[:, D:2 * D])
        o_ref[...] = (_sigmoid(g_ref[:, 0:D]) * pa_ref[...] + _sigmoid(g_ref[:, D:2 * D]) * pb).astype(BF16)

    return pl.pallas_call(body, name=name, grid=(t // tm,), in_specs=[_row(tm, 2 * D), _row(tm, D), _row(tm, 2 * D)],
                          out_specs=_row(tm, D), out_shape=jax.ShapeDtypeStruct((t, D), BF16),
                          compiler_params=_cp("parallel"))(proj, pa, glu)


def merge_bwd(proj, pa, glu, dm, name):
    t = pa.shape[0]
    tm = min(256, t)

    def body(g_ref, pa_ref, glu_ref, dm_ref, dpa_ref, dglu_ref, dg_ref):
        dmv = dm_ref[...]
        pav = pa_ref[...]
        sa = _sigmoid(g_ref[:, 0:D])
        sb = _sigmoid(g_ref[:, D:2 * D])
        ga = glu_ref[:, 0:D]
        sg = _sigmoid(glu_ref[:, D:2 * D])
        pb = ga * sg
        dpb = sb * dmv
        dpa_ref[...] = (sa * dmv).astype(BF16)
        dglu_ref[:, 0:D] = (dpb * sg).astype(BF16)
        dglu_ref[:, D:2 * D] = (dpb * pb * (1.0 - sg)).astype(BF16)
        dg_ref[:, 0:D] = (dmv * pav * sa * (1.0 - sa)).astype(BF16)
        dg_ref[:, D:2 * D] = (dmv * pb * sb * (1.0 - sb)).astype(BF16)

    return pl.pallas_call(
        body, name=name, grid=(t // tm,),
        in_specs=[_row(tm, 2 * D), _row(tm, D), _row(tm, 2 * D), _row(tm, D)],
        out_specs=[_row(tm, D), _row(tm, 2 * D), _row(tm, 2 * D)],
        out_shape=[jax.ShapeDtypeStruct((t, D), BF16), jax.ShapeDtypeStruct((t, 2 * D), BF16),
                   jax.ShapeDtypeStruct((t, 2 * D), BF16)],
        compiler_params=_cp("parallel"))(proj, pa, glu, dm)


def adamw(w, parts, m, v, name):
    r, c = w.shape
    p = parts.shape[0]
    tr = r if r <= 128 else 128
    c1 = 1.0 - ADAM_B1 ** ADAM_STEP
    c2 = 1.0 - ADAM_B2 ** ADAM_STEP

    def body(w_ref, p_ref, m_ref, v_ref, g_ref, d_ref, nm_ref, nv_ref):
        g = p_ref[0].astype(F32)
        for k in range(1, p):
            g = g + p_ref[k].astype(F32)
        mn = ADAM_B1 * m_ref[...] + (1.0 - ADAM_B1) * g
        vn = ADAM_B2 * v_ref[...] + (1.0 - ADAM_B2) * (g * g)
        g_ref[...] = g
        nm_ref[...] = mn
        nv_ref[...] = vn
        d_ref[...] = -ADAM_LR * ((mn / c1) / (jnp.sqrt(vn / c2) + ADAM_EPS) + ADAM_WD * w_ref[...])

    spec = pl.BlockSpec((tr, c), lambda i: (i, 0))
    o = jax.ShapeDtypeStruct((r, c), F32)
    return pl.pallas_call(
        body, name=name, grid=(pl.cdiv(r, tr),),
        in_specs=[spec, pl.BlockSpec((p, tr, c), lambda i: (0, i, 0)), spec, spec],
        out_specs=[spec, spec, spec, spec], out_shape=[o, o, o, o], compiler_params=_cp("parallel"))(w, parts, m, v)


def _s5_discretise(lambda_re, lambda_im, log_dt, b_re, b_im):
    dt = jnp.exp(log_dt)[:, None]
    lr = jnp.minimum(lambda_re, -1e-4)
    li = lambda_im
    mag = jnp.exp(lr * dt)
    ar = mag * jnp.cos(li * dt)
    ai = mag * jnp.sin(li * dt)
    den = lr * lr + li * li
    nr = ar - 1.0
    kr = (nr * lr + ai * li) / den
    ki = (ai * lr - nr * li) / den
    bbar_re = kr[..., None] * b_re - ki[..., None] * b_im
    bbar_im = kr[..., None] * b_im + ki[..., None] * b_re
    return ar, ai, bbar_re, bbar_im


def _bsg_of(bb_re, bb_im):
    eye = jnp.eye(8, dtype=F32)
    f = lambda b: jnp.einsum("sgpi,gh->sgihp", b.reshape(8, 8, 64, 16), eye).reshape(8, 128, 512)
    return jnp.concatenate([f(bb_re), f(bb_im)], axis=2)


def _bsg_diag(dbsg):
    eye = jnp.eye(8, dtype=F32)
    f = lambda x: jnp.einsum("sgihp,gh->sgpi", x.reshape(8, 8, 16, 8, 64), eye).reshape(64, 64, 16)
    return f(dbsg[:, :, 0:512]), f(dbsg[:, :, 512:1024])


def _csg_of(c_re, c_im):
    eye = jnp.eye(8, dtype=F32)
    f = lambda c: jnp.einsum("sgip,gh->sgphi", c.reshape(8, 8, 16, 64), eye).reshape(8, 512, 128)
    return jnp.concatenate([f(c_re), -f(c_im)], axis=1)


def _csg_diag(dcsg):
    eye = jnp.eye(8, dtype=F32)
    f = lambda x: jnp.einsum("sgphi,gh->sgip", x.reshape(8, 8, 64, 8, 16), eye).reshape(64, 16, 64)
    return f(dcsg[:, 0:512, :]), -f(dcsg[:, 512:1024, :])


def _perm(a, t):
    return a.reshape(8, t // 8, a.shape[1]).transpose(1, 0, 2).reshape(t, a.shape[1])


def _unperm(a, t):
    return a.reshape(t // 8, 8, a.shape[1]).transpose(1, 0, 2).reshape(t, a.shape[1])


def _gather_cols(w, name):
    g = all_gather(w.astype(BF16), name)
    return g.transpose(1, 0, 2).reshape(w.shape[0], N_DEV * w.shape[1])


def _gather_rows(w, name):
    g = all_gather(w.astype(BF16), name)
    return g.reshape(N_DEV * w.shape[0], w.shape[1])


def _col_parts(g):
    r, c = g.shape
    return g.reshape(r, N_DEV, c // N_DEV).transpose(1, 0, 2)


def _row_parts(g):
    r, c = g.shape
    return g.reshape(N_DEV, r // N_DEV, c)


def _pad_ffn_in(w):
    z = jnp.zeros((D, D_FFP - D_FF), w.dtype)
    return jnp.concatenate([w[:, :D_FF], z, w[:, D_FF:], z], axis=1)


def _unpad_ffn_in(g):
    return jnp.concatenate([g[:, :D_FF], g[:, D_FFP:D_FFP + D_FF]], axis=1)


def _pad_w_in(w):
    z = jnp.zeros((D, INP - P_DT - NH), w.dtype)
    return jnp.concatenate([w[:, O_GA:O_GB], w[:, O_GB:IN_COLS], w[:, 0:O_XBC], w[:, O_XBC:O_DT], w[:, O_U:O_GA],
                            w[:, O_DT:O_U], z], axis=1)


def _unpad_w_in(g):
    return jnp.concatenate([g[:, P_Z:P_XBC], g[:, P_XBC:P_U], g[:, P_DT:P_DT + NH], g[:, P_U:P_DT],
                            g[:, 0:D], g[:, D:2 * D]], axis=1)


_PACK = (("b_ada", 18432), ("norm_ffn1", 2048), ("norm_mix", 2048), ("conv_b", 3072), ("dt_bias", 32), ("a_log", 32),
         ("d_ssd", 32), ("ssd_norm_w", 2048), ("s5_lambda_re", 4096), ("s5_lambda_im", 4096), ("s5_b_re", 65536),
         ("s5_b_im", 65536), ("s5_c_re", 65536), ("s5_c_im", 65536), ("s5_d", 1024), ("s5_log_dt", 64),
         ("norm_ffn2", 2048), ("norm_final", 2048), ("loss", 1))
_PACK_ROWS = 304
_PACK_W = 1024


def _pack(d):
    flat = jnp.concatenate([d[k].reshape(-1).astype(F32) for k, _ in _PACK])
    return jnp.pad(flat, (0, _PACK_ROWS * _PACK_W - flat.shape[0])).reshape(_PACK_ROWS, _PACK_W)


def _unpack(a):
    flat = a.reshape(-1)
    out, off = {}, 0
    for k, n in _PACK:
        out[k] = flat[off:off + n]
        off += n
    return out


def _ffn_fwd(h, w_in_p, w_out_p, tag):
    ab = mm(h, w_in_p, out_dtype=BF16, name=tag + "_in")
    act = swiglu_fwd(ab, name=tag + "_act")
    f = mm(act, w_out_p, tk=512, name=tag + "_out")
    return ab, act, f


def _ffn_bwd(df, h, ab, act, w_in_p, w_out_p, tag):
    dact = mm(df, w_out_p, tb=True, name=tag + "_dact")
    dw_out = mm(act, df, ta=True, tm=512, tn=1024, tk=1024, name=tag + "_dwout")
    dab = swiglu_bwd(ab, dact, name=tag + "_dab")
    dh = mm(dab, w_in_p, tb=True, tk=1024, name=tag + "_dh")
    dw_in = mm(h, dab, ta=True, tm=1024, tn=1024, tk=1024, name=tag + "_dwin")
    return dh, dw_in, dw_out


def kernel(x, c, w_ada, b_ada, norm_ffn1, w_ffn1_in, w_ffn1_out, norm_mix, w_in, conv_w, conv_b, dt_bias, a_log, d_ssd, ssd_norm_w, w_a_proj, s5_lambda_re, s5_lambda_im, s5_b_re, s5_b_im, s5_c_re, s5_c_im, s5_d, s5_log_dt, w_b_glu, w_out, norm_ffn2, w_ffn2_in, w_ffn2_out, norm_final, loss_target, m_w_ada, m_b_ada, m_norm_ffn1, m_w_ffn1_in, m_w_ffn1_out, m_norm_mix, m_w_in, m_conv_w, m_conv_b, m_dt_bias, m_a_log, m_d_ssd, m_ssd_norm_w, m_w_a_proj, m_s5_lambda_re, m_s5_lambda_im, m_s5_b_re, m_s5_b_im, m_s5_c_re, m_s5_c_im, m_s5_d, m_s5_log_dt, m_w_b_glu, m_w_out, m_norm_ffn2, m_w_ffn2_in, m_w_ffn2_out, m_norm_final, v_w_ada, v_b_ada, v_norm_ffn1, v_w_ffn1_in, v_w_ffn1_out, v_norm_mix, v_w_in, v_conv_w, v_conv_b, v_dt_bias, v_a_log, v_d_ssd, v_ssd_norm_w, v_w_a_proj, v_s5_lambda_re, v_s5_lambda_im, v_s5_b_re, v_s5_b_im, v_s5_c_re, v_s5_c_im, v_s5_d, v_s5_log_dt, v_w_b_glu, v_w_out, v_norm_ffn2, v_w_ffn2_in, v_w_ffn2_out, v_norm_final):
    args = dict(locals())
    t = x.shape[1]
    me = _my_id()
    xt = x[0]
    tgt = loss_target[0]
    small = {k: args[k] for k, _ in _PACK if k != "loss"}

    wf1i = _pad_ffn_in(_gather_cols(w_ffn1_in[0], "ag_ffn1_in"))
    wf1o = jnp.pad(_gather_rows(w_ffn1_out[0], "ag_ffn1_out"), ((0, D_FFP - D_FF), (0, 0)))
    wf2i = _pad_ffn_in(_gather_cols(w_ffn2_in[0], "ag_ffn2_in"))
    wf2o = jnp.pad(_gather_rows(w_ffn2_out[0], "ag_ffn2_out"), ((0, D_FFP - D_FF), (0, 0)))
    winp = _pad_w_in(_gather_cols(w_in[0], "ag_w_in"))
    wap = _gather_rows(w_a_proj[0], "ag_w_a_proj")
    wbg = _gather_cols(w_b_glu[0], "ag_w_b_glu")
    wo = _gather_rows(w_out[0], "ag_w_out")
    convw = all_gather(conv_w[0], "ag_conv_w").transpose(1, 0, 2).reshape(CONV_K, CONV_DIM)

    c8 = all_gather(c, "ag_c").reshape(N_DEV, D)
    b_loc = lax.dynamic_slice(b_ada, (0, me * (N_ADA * D // N_DEV)), (1, N_ADA * D // N_DEV))
    m8 = ada_fwd(c8, w_ada[0], b_loc, "ada_fwd")
    mods = exchange(m8.reshape(N_DEV, 1, -1), "xc_mods").reshape(1, N_ADA * D)

    h1 = mod_fwd(xt, norm_ffn1, mods, 0, 1, name="mod1")
    ab1, act1, f1 = _ffn_fwd(h1, wf1i, wf1o, "ffn1")
    x1, h2 = mod_fwd(xt, norm_mix, mods, 3, 4, f=f1, gk=2, gscale=0.5, name="mod2")
    proj = mm(h2, winp, tn=512, name="w_in")
    cb_row = conv_b
    xc = conv_fwd(proj, convw, cb_row, "conv_fwd")
    row128 = lambda v: jnp.pad(v.reshape(1, -1), ((0, 0), (0, 128 - v.size)))
    dtb_row, alog_row = row128(dt_bias), row128(a_log)
    dx_row = jnp.repeat(d_ssd.reshape(-1), HP).reshape(1, 2048)
    expm = (jnp.arange(128)[:, None] == (jnp.arange(2048)[None, :] // HP)).astype(F32)
    tri = (jnp.arange(LCH)[:, None] >= jnp.arange(LCH)[None, :]).astype(F32)
    y_ssd, hsave = ssd_fwd(xc, proj, dtb_row, alog_row, dx_row, expm, tri, "ssd_fwd")
    ya = ssd_out_fwd(y_ssd, proj, ssd_norm_w, "ssd_out")
    pa = mm(ya, wap, name="w_a_proj")

    s5p = (s5_lambda_re[0], s5_lambda_im[0], s5_log_dt[0], s5_b_re[0], s5_b_im[0])
    (ar, ai, bb_re, bb_im), s5_vjp = jax.vjp(_s5_discretise, *s5p)
    a_re8 = jnp.broadcast_to(ar.reshape(8, 1, 512), (8, 8, 512))
    a_im8 = jnp.broadcast_to(ai.reshape(8, 1, 512), (8, 8, 512))
    bsg = _bsg_of(bb_re, bb_im).astype(BF16)
    csg = _csg_of(s5_c_re[0], s5_c_im[0]).astype(BF16)
    d_row = s5_d.reshape(1, S5W)
    lseg = t // 8
    u_p = _perm(proj[:, P_U:P_U + S5W], t)
    bu3 = s5_in(u_p, bsg, "s5_in").reshape(lseg, 8, S5NS)
    sinit = s5_scan_init(s5_scan_ends(bu3, a_re8, a_im8, False, "s5_ends_f"), a_re8, a_im8, lseg, False, "s5_init_f")
    s3 = s5_scan_fwd(bu3, sinit, a_re8, a_im8, "s5_scan_f")
    s2 = s3.reshape(t, S5NS)
    yb_p = s5_out(s2, csg, u_p, d_row, "s5_out")
    yb = _unperm(yb_p, t)
    gy = gelu_fwd(yb, "gelu")
    glu = mm(gy, wbg, name="w_b_glu")
    merged = merge_fwd(proj, pa, glu, "merge")
    o = mm(merged, wo, name="w_out")
    x2, h3 = mod_fwd(x1, norm_ffn2, mods, 6, 7, f=o, gk=5, gscale=1.0, name="mod3")
    ab3, act3, f3 = _ffn_fwd(h3, wf2i, wf2o, "ffn2")

    dx3, df3, st_fin = final_fwd_bwd(x2, f3, mods, norm_final.reshape(1, D), tgt, "final")
    dh3, dwf2i, dwf2o = _ffn_bwd(df3, h3, ab3, act3, wf2i, wf2o, "ffn2")
    dx2, do, st3 = mod_bwd(x2, dh3, dx3, norm_ffn2, mods, 7, fprev=o, gk=5, gscale=1.0, name="mod3_bwd")

    dmerged = mm(do, wo, tb=True, name="w_out_dx")
    dwo = mm(merged, do, ta=True, tm=1024, tn=1024, tk=1024, name="w_out_dw")
    dpa, dglu, dgates = merge_bwd(proj, pa, glu, dmerged, "merge_bwd")
    dgy = mm(dglu, wbg, tb=True, name="w_b_glu_dx")
    dwbg = mm(gy, dglu, ta=True, tm=1024, tn=1024, tk=1024, name="w_b_glu_dw")
    dyb_p = _perm(gelu_bwd(yb, dgy, "gelu_bwd"), t)
    e2, dcsg = s5_out_bwd(dyb_p, csg, s2, "s5_out_bwd")
    e3 = e2.reshape(lseg, 8, S5NS)
    linit = s5_scan_init(s5_scan_ends(e3, a_re8, a_im8, True, "s5_ends_b"), a_re8, a_im8, lseg, True, "s5_init_b")
    lam3, da8 = s5_scan_bwd(e3, linit, s3, sinit, a_re8, a_im8, "s5_scan_b")
    du_p, dbsg, dd_row = s5_in_bwd(lam3.reshape(t, S5NS), bsg, u_p, dyb_p, d_row, "s5_in_bwd")
    du = _unperm(du_p, t).astype(BF16)
    da = jnp.sum(da8, axis=1)
    dbb_re, dbb_im = _bsg_diag(dbsg)
    g_lre, g_lim, g_ldt, g_bre, g_bim = s5_vjp((da[:, 0:512].reshape(64, 64), da[:, 512:1024].reshape(64, 64),
                                                dbb_re, dbb_im))
    g_cre, g_cim = _csg_diag(dcsg)

    dya = mm(dpa, wap, tb=True, name="w_a_proj_dx")
    dwap = mm(ya, dpa, ta=True, tm=1024, tn=1024, tk=1024, name="w_a_proj_dw")
    dy_ssd, dz, st_sn = ssd_out_bwd(y_ssd, proj, dya, ssd_norm_w, "ssd_out_bwd")
    dxc, ddt, st_ssd = ssd_bwd(xc, proj, hsave, dy_ssd, dtb_row, alog_row, dx_row, expm, tri, "ssd_bwd")
    dpre, st_cv = conv_bwd_pre(proj, dxc, convw, cb_row, "conv_bwd_pre")
    dxbc = conv_bwd_in(dpre, convw, "conv_bwd_in")
    dproj = jnp.concatenate([dgates, dz, dxbc, du, ddt.astype(BF16), jnp.zeros((t, INP - P_DT - 128), BF16)], axis=1)
    dh2 = mm(dproj, winp, tb=True, tk=1024, name="w_in_dx")
    dwinp = mm(h2, dproj, ta=True, tm=1024, tn=512, tk=1024, name="w_in_dw")
    dx1, df1, st2 = mod_bwd(x1, dh2, dx2, norm_mix, mods, 4, fprev=f1, gk=2, gscale=0.5, name="mod2_bwd")
    dh1, dwf1i, dwf1o = _ffn_bwd(df1, h1, ab1, act1, wf1i, wf1o, "ffn1")
    gx, st1 = mod_bwd(xt, dh1, dx1, norm_ffn1, mods, 1, name="mod1_bwd")

    dmods = jnp.concatenate([st1[0], st1[1], st2[3], st2[0], st2[1], st3[3], st3[0], st3[1], st_fin[1]])
    part = {"b_ada": dmods, "norm_ffn1": st1[2], "norm_mix": st2[2], "conv_b": st_cv[4], "dt_bias": st_ssd[1, 0:NH],
            "a_log": st_ssd[0, 0:NH], "d_ssd": st_ssd[2, 0:NH], "ssd_norm_w": st_sn[0], "s5_lambda_re": g_lre,
            "s5_lambda_im": g_lim, "s5_b_re": g_bre, "s5_b_im": g_bim, "s5_c_re": g_cre, "s5_c_im": g_cim,
            "s5_d": dd_row, "s5_log_dt": g_ldt, "norm_ffn2": st3[2], "norm_final": st_fin[0],
            "loss": (0.5 / D) * jnp.sum(st_fin[2])}
    zero = {"loss": jnp.zeros((1,), F32)}
    gath = all_gather(_pack(part), "ag_small")
    sg, sd, sm, sv = adamw(_pack({**small, **zero}), gath, _pack({**{k: args["m_" + k] for k in small}, **zero}),
                           _pack({**{k: args["v_" + k] for k in small}, **zero}), "adamw_small")
    sg, sd, sm, sv = _unpack(sg), _unpack(sd), _unpack(sm), _unpack(sv)
    loss = sg["loss"][0]

    dm_loc = lax.dynamic_slice(gath.reshape(N_DEV, -1)[:, 0:N_ADA * D], (0, me * (N_ADA * D // N_DEV)),
                               (N_DEV, N_ADA * D // N_DEV))
    g_ada = ada_bwd(c8.T, dm_loc, "ada_bwd")
    big = {
        "w_ada": g_ada[None],
        "w_ffn1_in": exchange(_col_parts(_unpad_ffn_in(dwf1i)), "xc_ffn1_in"),
        "w_ffn1_out": exchange(_row_parts(dwf1o[:D_FF]), "xc_ffn1_out"),
        "w_in": exchange(_col_parts(_unpad_w_in(dwinp)), "xc_w_in"),
        "conv_w": exchange(_col_parts(st_cv[0:CONV_K]), "xc_conv_w"),
        "w_a_proj": exchange(_row_parts(dwap), "xc_w_a_proj"),
        "w_b_glu": exchange(_col_parts(dwbg), "xc_w_b_glu"),
        "w_out": exchange(_row_parts(dwo), "xc_w_out"),
        "w_ffn2_in": exchange(_col_parts(_unpad_ffn_in(dwf2i)), "xc_ffn2_in"),
        "w_ffn2_out": exchange(_row_parts(dwf2o[:D_FF]), "xc_ffn2_out"),
    }
    res = {}
    for k, parts in big.items():
        res[k] = adamw(args[k][0], parts, args["m_" + k][0], args["v_" + k][0], "adamw_" + k)

    names = ["w_ada", "b_ada", "norm_ffn1", "w_ffn1_in", "w_ffn1_out", "norm_mix", "w_in", "conv_w", "conv_b", "dt_bias",
             "a_log", "d_ssd", "ssd_norm_w", "w_a_proj", "s5_lambda_re", "s5_lambda_im", "s5_b_re", "s5_b_im", "s5_c_re",
             "s5_c_im", "s5_d", "s5_log_dt", "w_b_glu", "w_out", "norm_ffn2", "w_ffn2_in", "w_ffn2_out", "norm_final"]
    outs = [loss, gx[None]]
    for q, src in enumerate((sg, sd, sm, sv)):
        for k in names:
            if k in res:
                outs.append(res[k][q][None])
            else:
                outs.append(src[k].reshape(args[k].shape))
    return tuple(outs)
```

```python
import functools
import math

import jax
import jax.numpy as jnp
from jax import lax
from jax.experimental import pallas as pl
from jax.experimental.pallas import tpu as pltpu

F32 = jnp.float32
BF16 = jnp.bfloat16
HI = lax.Precision.HIGHEST

N_DEV = 8
D = 2048
D_FF = 5504
D_FFP = 5632
NH = 32
HP = 64
NG = 4
NST = 128
LCH = 128
CONV_DIM = 3072
CONV_K = 4
S5W = 1024
S5NS = 8192
N_ADA = 9
EPS = 1e-6
IN_COLS = 10272
INP = 10752
P_GATES, P_Z, P_XBC, P_U, P_DT = 0, 4096, 6144, 9216, 10240
O_XBC, O_DT, O_U, O_GA, O_GB = 2048, 5120, 5152, 6176, 8224
NEG = -1e30
VMEM_LIMIT = 56 * 1024 * 1024

ADAM_LR, ADAM_B1, ADAM_B2, ADAM_EPS, ADAM_WD, ADAM_STEP = 0.001, 0.9, 0.999, 1e-08, 0.01, 10


def _cp(*sem):
    return pltpu.CompilerParams(dimension_semantics=sem, vmem_limit_bytes=VMEM_LIMIT)


def _tile(dim, pref):
    if dim <= pref or dim % pref == 0:
        return min(dim, pref)
    for t in (2048, 1024, 512, 256, 128):
        if t <= pref and dim % t == 0:
            return t
    return dim


def _vec(w, cb=0):
    return pl.BlockSpec((1, w), lambda *_: (0, cb))


def _row(tm, w, cb=0):
    return pl.BlockSpec((tm, w), lambda i: (i, cb))


def _stats(w):
    return pl.BlockSpec((8, w), lambda *_: (0, 0))


def _sigmoid(x):
    return 1.0 / (1.0 + jnp.exp(-x))


def _softplus(x):
    return jnp.maximum(x, 0.0) + jnp.log1p(jnp.exp(-jnp.abs(x)))


def _peer(k):
    x, y, c = lax.axis_index("x"), lax.axis_index("y"), lax.axis_index("c")
    return (x ^ ((k >> 2) & 1), y ^ ((k >> 1) & 1), c ^ (k & 1))


def _my_id():
    return 4 * lax.axis_index("x") + 2 * lax.axis_index("y") + lax.axis_index("c")


def _comm_out_shape(kind, v):
    return jax.ShapeDtypeStruct(((N_DEV,) + v.shape) if kind == "ag" else v.shape, v.dtype)


def _comm_scratch(n):
    return [pltpu.SemaphoreType.DMA((n * N_DEV,)), pltpu.SemaphoreType.DMA((n * N_DEV,))]


def _comm_copies(kinds, srcs, dsts, send_sems, recv_sems):
    me = _my_id()
    out = []
    for q, (kind, src, dst) in enumerate(zip(kinds, srcs, dsts)):
        out.append(pltpu.make_async_copy(src if kind == "ag" else src.at[me], dst.at[me], send_sems.at[q * N_DEV]))
        for k in range(1, N_DEV):
            px, py, pc = _peer(k)
            out.append(pltpu.make_async_remote_copy(
                src_ref=src if kind == "ag" else src.at[4 * px + 2 * py + pc], dst_ref=dst.at[me],
                send_sem=send_sems.at[q * N_DEV + k], recv_sem=recv_sems.at[q * N_DEV + k],
                device_id=(px, py, pc), device_id_type=pl.DeviceIdType.MESH))
    return out


def comm_call(items, name):
    kinds = [k for k, _ in items]
    n = len(items)

    def body(*refs):
        cps = _comm_copies(kinds, refs[:n], refs[n:2 * n], refs[2 * n], refs[2 * n + 1])
        for cp in cps:
            cp.start()
        for cp in cps:
            cp.wait()

    return pl.pallas_call(
        body, name=name,
        in_specs=[pl.BlockSpec(memory_space=pl.ANY)] * n, out_specs=[pl.BlockSpec(memory_space=pl.ANY)] * n,
        out_shape=[_comm_out_shape(k, v) for k, v in items], scratch_shapes=_comm_scratch(n),
    )(*[v for _, v in items])


def all_gather(v, name):
    return comm_call([("ag", v)], name)[0]


def mm(a, b, *, ta=False, tb=False, out_dtype=F32, tm=512, tn=1024, tk=2048, name, comm=()):
    if ta:
        kd, m = a.shape
    else:
        m, kd = a.shape
    if tb:
        n, kd2 = b.shape
    else:
        kd2, n = b.shape
    assert kd == kd2, (a.shape, b.shape, ta, tb)
    tm, tn, tk = _tile(m, tm), _tile(n, tn), _tile(kd, tk)
    nk = kd // tk
    grid = (n // tn, m // tm, nk)
    dims = (((0,) if ta else (1,), (1,) if tb else (0,)), ((), ()))
    nc = len(comm)
    kinds = [k for k, _ in comm]

    def body(*refs):
        a_ref, b_ref = refs[0], refs[1]
        o_ref = refs[2 + nc]
        scr = refs[3 + 2 * nc:]
        if nc:
            csrc, cdst = refs[2:2 + nc], refs[3 + nc:3 + 2 * nc]
            send_sems, recv_sems = scr[-2], scr[-1]
            pid = [pl.program_id(d) for d in range(3)]

            @pl.when((pid[0] == 0) & (pid[1] == 0) & (pid[2] == 0))
            def _():
                for cp in _comm_copies(kinds, csrc, cdst, send_sems, recv_sems):
                    cp.start()

        p = lax.dot_general(a_ref[...], b_ref[...], dims, preferred_element_type=F32)
        if nk == 1:
            o_ref[...] = p.astype(o_ref.dtype)
        else:
            acc = scr[0]
            k = pl.program_id(2)

            @pl.when(k == 0)
            def _():
                acc[...] = p

            @pl.when(k > 0)
            def _():
                acc[...] += p

            @pl.when(k == nk - 1)
            def _():
                o_ref[...] = acc[...].astype(o_ref.dtype)

        if nc:
            @pl.when((pid[0] == grid[0] - 1) & (pid[1] == grid[1] - 1) & (pid[2] == grid[2] - 1))
            def _():
                for cp in _comm_copies(kinds, csrc, cdst, send_sems, recv_sems):
                    cp.wait()

    a_spec = pl.BlockSpec((tk, tm), lambda j, i, k: (k, i)) if ta else pl.BlockSpec((tm, tk), lambda j, i, k: (i, k))
    b_spec = pl.BlockSpec((tn, tk), lambda j, i, k: (j, k)) if tb else pl.BlockSpec((tk, tn), lambda j, i, k: (k, j))
    hbm = pl.BlockSpec(memory_space=pl.ANY)
    out = pl.pallas_call(
        body, name=name, grid=grid, in_specs=[a_spec, b_spec] + [hbm] * nc,
        out_specs=[pl.BlockSpec((tm, tn), lambda j, i, k: (i, j))] + [hbm] * nc,
        out_shape=[jax.ShapeDtypeStruct((m, n), out_dtype)] + [_comm_out_shape(k, v) for k, v in comm],
        scratch_shapes=([pltpu.VMEM((tm, tn), F32)] if nk > 1 else []) + (_comm_scratch(nc) if nc else []),
        compiler_params=_cp("arbitrary", "arbitrary", "arbitrary") if nc else _cp("parallel", "parallel", "arbitrary"),
    )(a, b, *[v for _, v in comm])
    return (out[0], list(out[1:])) if nc else out[0]


def ada_fwd(c8, w_loc, b_loc, name):
    n = w_loc.shape[1]
    tn = 256

    def body(c_ref, w_ref, b_ref, o_ref):
        cv = c_ref[...]
        ca = cv * _sigmoid(cv)
        o_ref[...] = jnp.dot(ca, w_ref[...], precision=HI, preferred_element_type=F32) + b_ref[...]

    return pl.pallas_call(
        body, name=name, grid=(n // tn,),
        in_specs=[pl.BlockSpec((N_DEV, D), lambda j: (0, 0)), pl.BlockSpec((D, tn), lambda j: (0, j)),
                  pl.BlockSpec((1, tn), lambda j: (0, j))],
        out_specs=pl.BlockSpec((N_DEV, tn), lambda j: (0, j)),
        out_shape=jax.ShapeDtypeStruct((N_DEV, n), F32), compiler_params=_cp("parallel"),
    )(c8, w_loc, b_loc)


def ada_bwd(c8t, dm_loc, name):
    n = dm_loc.shape[1]
    tn = 256

    def body(c_ref, d_ref, o_ref):
        cv = c_ref[...]
        ca = cv * _sigmoid(cv)
        o_ref[...] = jnp.dot(ca, d_ref[...], precision=HI, preferred_element_type=F32)

    return pl.pallas_call(
        body, name=name, grid=(n // tn,),
        in_specs=[pl.BlockSpec((D, N_DEV), lambda j: (0, 0)), pl.BlockSpec((N_DEV, tn), lambda j: (0, j))],
        out_specs=pl.BlockSpec((D, tn), lambda j: (0, j)),
        out_shape=jax.ShapeDtypeStruct((D, n), F32), compiler_params=_cp("parallel"),
    )(c8t, dm_loc)


def mod_fwd(x, nw, mods, shk, sck, *, f=None, gk=None, gscale=1.0, name):
    t = x.shape[0]
    tm = min(256, t)
    res = f is not None

    def body(*refs):
        if res:
            x_ref, f_ref, g_ref, nw_ref, sh_ref, sc_ref, x1_ref, h_ref = refs
            xv = x_ref[...] + (gscale * g_ref[...]) * f_ref[...]
            x1_ref[...] = xv
        else:
            x_ref, nw_ref, sh_ref, sc_ref, h_ref = refs
            xv = x_ref[...]
        r = lax.rsqrt(jnp.mean(xv * xv, axis=-1, keepdims=True) + EPS)
        h_ref[...] = ((xv * r * nw_ref[...]) * (1.0 + sc_ref[...]) + sh_ref[...]).astype(BF16)

    ins = [x] + ([f, mods] if res else []) + [nw, mods, mods]
    specs = [_row(tm, D)] + ([_row(tm, D), _vec(D, gk)] if res else []) + [_vec(D), _vec(D, shk), _vec(D, sck)]
    outs = ([jax.ShapeDtypeStruct((t, D), F32)] if res else []) + [jax.ShapeDtypeStruct((t, D), BF16)]
    ospecs = ([_row(tm, D)] if res else []) + [_row(tm, D)]
    out = pl.pallas_call(body, name=name, grid=(t // tm,), in_specs=specs, out_specs=ospecs, out_shape=outs,
                         compiler_params=_cp("parallel"))(*ins)
    return out if res else out[0]


def final_fwd_bwd(x2, f3, mods, nf, tgt, name):
    t = x2.shape[0]
    tm = min(128, t)

    def body(x_ref, f_ref, g_ref, nf_ref, t_ref, dx_ref, df_ref, st_ref):
        @pl.when(pl.program_id(0) == 0)
        def _():
            st_ref[...] = jnp.zeros_like(st_ref)

        g = 0.5 * g_ref[...]
        fv = f_ref[...]
        xv = x_ref[...] + g * fv
        r = lax.rsqrt(jnp.mean(xv * xv, axis=-1, keepdims=True) + EPS)
        xh = xv * r
        nfv = nf_ref[...]
        e = xh * nfv - t_ref[...]
        st_ref[2:3, :] += jnp.sum(e * e, axis=0, keepdims=True)
        dy = e * (1.0 / D)
        st_ref[0:1, :] += jnp.sum(dy * xh, axis=0, keepdims=True)
        dxh = dy * nfv
        dx = r * (dxh - xh * jnp.mean(dxh * xh, axis=-1, keepdims=True))
        dx_ref[...] = dx
        df_ref[...] = (g * dx).astype(BF16)
        st_ref[1:2, :] += 0.5 * jnp.sum(fv * dx, axis=0, keepdims=True)

    return pl.pallas_call(
        body, name=name, grid=(t // tm,),
        in_specs=[_row(tm, D), _row(tm, D), _vec(D, 8), _vec(D), _row(tm, D)],
        out_specs=[_row(tm, D), _row(tm, D), _stats(D)],
        out_shape=[jax.ShapeDtypeStruct((t, D), F32), jax.ShapeDtypeStruct((t, D), BF16),
                   jax.ShapeDtypeStruct((8, D), F32)],
        compiler_params=_cp("arbitrary"),
    )(x2, f3, mods, nf, tgt)


def mod_bwd(x_in, dh, dx_out, nw, mods, sck, *, fprev=None, gk=None, gscale=1.0, name):
    t = x_in.shape[0]
    tm = min(128, t)
    gate = fprev is not None

    def body(*refs):
        if gate:
            x_ref, dh_ref, dxo_ref, nw_ref, sc_ref, f_ref, g_ref, dx_ref, df_ref, st_ref = refs
        else:
            x_ref, dh_ref, dxo_ref, nw_ref, sc_ref, dx_ref, st_ref = refs

        @pl.when(pl.program_id(0) == 0)
        def _():
            st_ref[...] = jnp.zeros_like(st_ref)

        xv = x_ref[...]
        dhv = dh_ref[...]
        r = lax.rsqrt(jnp.mean(xv * xv, axis=-1, keepdims=True) + EPS)
        xh = xv * r
        nwv = nw_ref[...]
        st_ref[0:1, :] += jnp.sum(dhv, axis=0, keepdims=True)
        st_ref[1:2, :] += jnp.sum(dhv * (xh * nwv), axis=0, keepdims=True)
        dn = dhv * (1.0 + sc_ref[...])
        st_ref[2:3, :] += jnp.sum(dn * xh, axis=0, keepdims=True)
        dxh = dn * nwv
        dx = dxo_ref[...] + r * (dxh - xh * jnp.mean(dxh * xh, axis=-1, keepdims=True))
        dx_ref[...] = dx
        if gate:
            df_ref[...] = ((gscale * g_ref[...]) * dx).astype(BF16)
            st_ref[3:4, :] += gscale * jnp.sum(f_ref[...] * dx, axis=0, keepdims=True)

    ins = [x_in, dh, dx_out, nw, mods] + ([fprev, mods] if gate else [])
    specs = [_row(tm, D), _row(tm, D), _row(tm, D), _vec(D), _vec(D, sck)] + ([_row(tm, D), _vec(D, gk)] if gate else [])
    outs = [jax.ShapeDtypeStruct((t, D), F32)] + ([jax.ShapeDtypeStruct((t, D), BF16)] if gate else []) + \
        [jax.ShapeDtypeStruct((8, D), F32)]
    ospecs = [_row(tm, D)] + ([_row(tm, D)] if gate else []) + [_stats(D)]
    return pl.pallas_call(body, name=name, grid=(t // tm,), in_specs=specs, out_specs=ospecs, out_shape=outs,
                          compiler_params=_cp("arbitrary"))(*ins)


def swiglu_fwd(ab, name):
    t = ab.shape[0]
    tm = min(256, t)
    cw = 512

    def body(ab_ref, o_ref):
        for j in range(D_FFP // cw):
            a = ab_ref[:, j * cw:(j + 1) * cw].astype(F32)
            b = ab_ref[:, D_FFP + j * cw:D_FFP + (j + 1) * cw].astype(F32)
            o_ref[:, j * cw:(j + 1) * cw] = (a * _sigmoid(a) * b).astype(BF16)

    return pl.pallas_call(body, name=name, grid=(t // tm,), in_specs=[_row(tm, 2 * D_FFP)], out_specs=_row(tm, D_FFP),
                          out_shape=jax.ShapeDtypeStruct((t, D_FFP), BF16), compiler_params=_cp("parallel"))(ab)


def swiglu_bwd(ab, dact, name):
    t = ab.shape[0]
    tm = min(256, t)
    cw = 512

    def body(ab_ref, d_ref, o_ref):
        for j in range(D_FFP // cw):
            a = ab_ref[:, j * cw:(j + 1) * cw].astype(F32)
            b = ab_ref[:, D_FFP + j * cw:D_FFP + (j + 1) * cw].astype(F32)
            dv = d_ref[:, j * cw:(j + 1) * cw]
            s = _sigmoid(a)
            o_ref[:, j * cw:(j + 1) * cw] = (dv * b * (s * (1.0 + a * (1.0 - s)))).astype(BF16)
            o_ref[:, D_FFP + j * cw:D_FFP + (j + 1) * cw] = (dv * (a * s)).astype(BF16)

    return pl.pallas_call(body, name=name, grid=(t // tm,), in_specs=[_row(tm, 2 * D_FFP), _row(tm, D_FFP)],
                          out_specs=_row(tm, 2 * D_FFP), out_shape=jax.ShapeDtypeStruct((t, 2 * D_FFP), BF16),
                          compiler_params=_cp("parallel"))(ab, dact)


def _conv_pre(cur, prev8, w, b, tm):
    full = jnp.concatenate([prev8, cur], axis=0)
    pre = b + w[3:4, :] * cur
    for k in range(CONV_K - 1):
        s = CONV_K - 1 - k
        pre = pre + w[k:k + 1, :] * pltpu.roll(full, s, 0)[8:8 + tm, :]
    return pre


def conv_fwd(proj, cw_full, cb_full, name):
    t = proj.shape[0]
    tm = min(256, t)
    cwid = 1024
    cb0 = P_XBC // cwid

    def body(x_ref, p_ref, w_ref, b_ref, o_ref):
        i = pl.program_id(1)
        prev8 = jnp.where(i == 0, 0.0, p_ref[...])
        pre = _conv_pre(x_ref[...], prev8, w_ref[...], b_ref[...], tm)
        o_ref[...] = pre * _sigmoid(pre)

    return pl.pallas_call(
        body, name=name, grid=(CONV_DIM // cwid, t // tm),
        in_specs=[pl.BlockSpec((tm, cwid), lambda j, i: (i, cb0 + j)),
                  pl.BlockSpec((8, cwid), lambda j, i: (jnp.maximum(i * (tm // 8) - 1, 0), cb0 + j)),
                  pl.BlockSpec((CONV_K, cwid), lambda j, i: (0, j)), pl.BlockSpec((1, cwid), lambda j, i: (0, j))],
        out_specs=pl.BlockSpec((tm, cwid), lambda j, i: (i, j)),
        out_shape=jax.ShapeDtypeStruct((t, CONV_DIM), F32), compiler_params=_cp("parallel", "parallel"),
    )(proj, proj, cw_full, cb_full)


def conv_bwd_pre(proj, dxc, cw_full, cb_full, name):
    t = proj.shape[0]
    tm = min(256, t)
    cwid = 1024
    cb0 = P_XBC // cwid

    def body(x_ref, p_ref, d_ref, w_ref, b_ref, o_ref, st_ref):
        i = pl.program_id(1)

        @pl.when(i == 0)
        def _():
            st_ref[...] = jnp.zeros_like(st_ref)

        cur = x_ref[...]
        prev8 = jnp.where(i == 0, 0.0, p_ref[...])
        pre = _conv_pre(cur, prev8, w_ref[...], b_ref[...], tm)
        s = _sigmoid(pre)
        dpre = d_ref[...] * (s * (1.0 + pre * (1.0 - s)))
        o_ref[...] = dpre
        st_ref[4:5, :] += jnp.sum(dpre, axis=0, keepdims=True)
        st_ref[3:4, :] += jnp.sum(dpre * cur, axis=0, keepdims=True)
        full = jnp.concatenate([prev8, cur], axis=0)
        for k in range(CONV_K - 1):
            sft = CONV_K - 1 - k
            st_ref[k:k + 1, :] += jnp.sum(dpre * pltpu.roll(full, sft, 0)[8:8 + tm, :], axis=0, keepdims=True)

    return pl.pallas_call(
        body, name=name, grid=(CONV_DIM // cwid, t // tm),
        in_specs=[pl.BlockSpec((tm, cwid), lambda j, i: (i, cb0 + j)),
                  pl.BlockSpec((8, cwid), lambda j, i: (jnp.maximum(i * (tm // 8) - 1, 0), cb0 + j)),
                  pl.BlockSpec((tm, cwid), lambda j, i: (i, j)),
                  pl.BlockSpec((CONV_K, cwid), lambda j, i: (0, j)), pl.BlockSpec((1, cwid), lambda j, i: (0, j))],
        out_specs=[pl.BlockSpec((tm, cwid), lambda j, i: (i, j)), pl.BlockSpec((8, cwid), lambda j, i: (0, j))],
        out_shape=[jax.ShapeDtypeStruct((t, CONV_DIM), F32), jax.ShapeDtypeStruct((8, CONV_DIM), F32)],
        compiler_params=_cp("parallel", "arbitrary"),
    )(proj, proj, dxc, cw_full, cb_full)


def conv_bwd_in(dpre, cw_full, name):
    t = dpre.shape[0]
    tm = min(256, t)
    cwid = 1024
    nt = t // tm

    def body(d_ref, n_ref, w_ref, o_ref):
        i = pl.program_id(1)
        cur = d_ref[...]
        nxt = jnp.where(i == nt - 1, 0.0, n_ref[...])
        full = jnp.concatenate([cur, nxt], axis=0)
        w = w_ref[...]
        acc = w[3:4, :] * cur
        for k in range(CONV_K - 1):
            s = CONV_K - 1 - k
            acc = acc + w[k:k + 1, :] * pltpu.roll(full, tm + 8 - s, 0)[0:tm, :]
        o_ref[...] = acc.astype(BF16)

    return pl.pallas_call(
        body, name=name, grid=(CONV_DIM // cwid, nt),
        in_specs=[pl.BlockSpec((tm, cwid), lambda j, i: (i, j)),
                  pl.BlockSpec((8, cwid), lambda j, i: (jnp.minimum((i + 1) * (tm // 8), t // 8 - 1), j)),
                  pl.BlockSpec((CONV_K, cwid), lambda j, i: (0, j))],
        out_specs=pl.BlockSpec((tm, cwid), lambda j, i: (i, j)),
        out_shape=jax.ShapeDtypeStruct((t, CONV_DIM), BF16), compiler_params=_cp("parallel", "parallel"),
    )(dpre, dpre, cw_full)


def _ssd_common(dt_ref, dtb_ref, al_ref, exp_ref, tri_ref):
    lane = lax.broadcasted_iota(jnp.int32, (1, 128), 1)
    a_row = jnp.where(lane < NH, -jnp.exp(al_ref[...]), 0.0)
    zraw = dt_ref[...] + dtb_ref[...]
    dtv = _softplus(zraw)
    cs = jnp.dot(tri_ref[...], dtv * a_row, precision=HI, preferred_element_type=F32)
    e = exp_ref[...]
    csx = jnp.dot(cs, e, precision=HI, preferred_element_type=F32)
    dtx = jnp.dot(dtv, e, precision=HI, preferred_element_type=F32)
    return a_row, zraw, dtv, cs, csx, dtx


def _nt(a, b):
    return lax.dot_general(a, b, (((1,), (1,)), ((), ())), preferred_element_type=F32)


def _dot(a, b):
    return jnp.dot(a, b, preferred_element_type=F32)


def ssd_fwd(xc, proj, dtb_row, alog_row, dx_row, expm, tri, name):
    t = xc.shape[0]
    nc = t // LCH

    def body(xs_ref, bm_ref, cm_ref, dt_ref, dtb_ref, al_ref, dxr_ref, exp_ref, tri_ref, y_ref, hs_ref, h_scr):
        @pl.when(pl.program_id(0) == 0)
        def _():
            h_scr[...] = jnp.zeros_like(h_scr)

        _, _, _, cs, csx, dtx = _ssd_common(dt_ref, dtb_ref, al_ref, exp_ref, tri_ref)
        cst = cs.T
        csl = csx[LCH - 1:LCH, :]
        xs = xs_ref[...]
        xd = xs * dtx
        xdw = xd * jnp.exp(csl - csx)
        ecs = jnp.exp(csx)
        ecl = jnp.exp(csl)
        tril = lax.broadcasted_iota(jnp.int32, (LCH, LCH), 0) >= lax.broadcasted_iota(jnp.int32, (LCH, LCH), 1)
        hs_ref[...] = h_scr[...]
        for g in range(NG):
            gc = slice(g * 512, (g + 1) * 512)
            bm = bm_ref[:, g * NST:(g + 1) * NST]
            cmb = cm_ref[:, g * NST:(g + 1) * NST].astype(BF16)
            gm = _nt(cmb, bm.astype(BF16))
            hg = h_scr[:, gc]
            yo = _dot(cmb, hg.astype(BF16)) * ecs[:, gc]
            st = _dot(bm.T.astype(BF16), xdw[:, gc].astype(BF16))
            for r in range(8):
                h = g * 8 + r
                hc = slice(h * HP, (h + 1) * HP)
                seg = cs[:, h:h + 1] - cst[h:h + 1, :]
                m = (gm * jnp.exp(jnp.where(tril, seg, NEG))).astype(BF16)
                yd = _dot(m, xd[:, hc].astype(BF16))
                y_ref[:, hc] = yd + yo[:, r * HP:(r + 1) * HP] + dxr_ref[:, hc] * xs[:, hc]
            h_scr[:, gc] = ecl[:, gc] * hg + st

    return pl.pallas_call(
        body, name=name, grid=(nc,),
        in_specs=[pl.BlockSpec((LCH, 2048), lambda c: (c, 0)), pl.BlockSpec((LCH, 512), lambda c: (c, 4)),
                  pl.BlockSpec((LCH, 512), lambda c: (c, 5)), pl.BlockSpec((LCH, 128), lambda c: (c, P_DT // 128)),
                  _vec(128), _vec(128), _vec(2048), pl.BlockSpec((128, 2048), lambda c: (0, 0)),
                  pl.BlockSpec((LCH, LCH), lambda c: (0, 0))],
        out_specs=[pl.BlockSpec((LCH, 2048), lambda c: (c, 0)), pl.BlockSpec((None, NST, 2048), lambda c: (c, 0, 0))],
        out_shape=[jax.ShapeDtypeStruct((t, 2048), F32), jax.ShapeDtypeStruct((nc, NST, 2048), F32)],
        scratch_shapes=[pltpu.VMEM((NST, 2048), F32)],
        compiler_params=_cp("arbitrary"),
    )(xc, xc, xc, proj, dtb_row, alog_row, dx_row, expm, tri)


def ssd_bwd(xc, proj, hsave, dy, dtb_row, alog_row, dx_row, expm, tri, name):
    t = xc.shape[0]
    nc = t // LCH

    def body(xs_ref, bm_ref, cm_ref, dt_ref, hs_ref, dy_ref, dtb_ref, al_ref, dxr_ref, exp_ref, tri_ref,
             dxc_ref, ddt_ref, st_ref, dh_scr, dxd_scr, dcsx_scr):
        @pl.when(pl.program_id(0) == 0)
        def _():
            dh_scr[...] = jnp.zeros_like(dh_scr)
            st_ref[...] = jnp.zeros_like(st_ref)

        a_row, zraw, dtv, cs, csx, dtx = _ssd_common(dt_ref, dtb_ref, al_ref, exp_ref, tri_ref)
        e = exp_ref[...]
        cst = cs.T
        csl = csx[LCH - 1:LCH, :]
        xs = xs_ref[...]
        xd = xs * dtx
        wend = jnp.exp(csl - csx)
        xdw = xd * wend
        ecs = jnp.exp(csx)
        ecl = jnp.exp(csl)
        ri = lax.broadcasted_iota(jnp.int32, (LCH, LCH), 0)
        ci = lax.broadcasted_iota(jnp.int32, (LCH, LCH), 1)
        tril = ri >= ci
        triu = ri <= ci
        lane = lax.broadcasted_iota(jnp.int32, (1, 128), 1)
        dyv = dy_ref[...]
        dxr = dxr_ref[...]
        st_ref[2:3, :] += lax.dot_general(jnp.sum(dyv * xs, axis=0, keepdims=True), e, (((1,), (1,)), ((), ())),
                                          precision=HI, preferred_element_type=F32)
        dcs = jnp.zeros((LCH, 128), F32)
        for g in range(NG):
            gc = slice(g * 512, (g + 1) * 512)
            bmb = bm_ref[:, g * NST:(g + 1) * NST].astype(BF16)
            cm = cm_ref[:, g * NST:(g + 1) * NST]
            cmb = cm.astype(BF16)
            hg = hs_ref[:, gc]
            hgb = hg.astype(BF16)
            dhc = dh_scr[:, gc]
            dhcb = dhc.astype(BF16)
            dyg = dyv[:, gc]
            yo = _dot(cmb, hgb) * ecs[:, gc]
            dq = (dyg * ecs[:, gc]).astype(BF16)
            dcm = _nt(dq, hgb)
            dh_yo = _dot(cm.T.astype(BF16), dq)
            dxdw = _dot(bmb, dhcb)
            dbm = _nt(xdw[:, gc].astype(BF16), dhcb)
            tt = dxdw * xdw[:, gc]
            dcsx_g = dyg * yo - tt
            dcsl_g = jnp.sum(tt, axis=0, keepdims=True) + jnp.sum(dhc * hg, axis=0, keepdims=True) * ecl[:, gc]
            dxd_scr[:, gc] = dxdw * wend[:, gc]
            dh_scr[:, gc] = ecl[:, gc] * dhc + dh_yo
            gm = _nt(cmb, bmb)
            gmt = _nt(bmb, cmb)
            dg = jnp.zeros((LCH, LCH), F32)
            dgt = jnp.zeros((LCH, LCH), F32)
            for r in range(8):
                h = g * 8 + r
                hc = slice(h * HP, (h + 1) * HP)
                seg = cs[:, h:h + 1] - cst[h:h + 1, :]
                lm = jnp.exp(jnp.where(tril, seg, NEG))
                lmt = jnp.exp(jnp.where(triu, -seg, NEG))
                mm_ = gm * lm
                mmt = gmt * lmt
                xdh = xd[:, hc].astype(BF16)
                dyh = dyv[:, hc].astype(BF16)
                dm = _nt(dyh, xdh)
                dmt = _nt(xdh, dyh)
                dxd_scr[:, hc] += _dot(mmt.astype(BF16), dyh)
                rs = jnp.sum(dm * mm_, axis=1, keepdims=True) - jnp.sum(dmt * mmt, axis=1, keepdims=True)
                dcs = dcs + rs * jnp.where(lane == h, 1.0, 0.0)
                dg = dg + dm * lm
                dgt = dgt + dmt * lmt
            dcm = dcm + _dot(dg.astype(BF16), bmb)
            dbm = dbm + _dot(dgt.astype(BF16), cmb)
            dxc_ref[:, 2048 + g * NST:2048 + (g + 1) * NST] = dbm
            dxc_ref[:, 2560 + g * NST:2560 + (g + 1) * NST] = dcm
            dcsx_scr[:, gc] = dcsx_g
            dcsx_scr[LCH - 1:LCH, gc] += dcsl_g
        dxd = dxd_scr[...]
        dxc_ref[:, 0:2048] = dxr * dyv + dxd * dtx
        ddtv = lax.dot_general(dxd * xs, e, (((1,), (1,)), ((), ())), precision=HI, preferred_element_type=F32)
        dcs = dcs + lax.dot_general(dcsx_scr[...], e, (((1,), (1,)), ((), ())), precision=HI,
                                    preferred_element_type=F32)
        dda = lax.dot_general(tri_ref[...], dcs, (((0,), (0,)), ((), ())), precision=HI, preferred_element_type=F32)
        ddtv = ddtv + dda * a_row
        st_ref[0:1, :] += jnp.sum(dda * dtv, axis=0, keepdims=True) * a_row
        ddt = ddtv * _sigmoid(zraw)
        ddt_ref[...] = ddt
        st_ref[1:2, :] += jnp.sum(ddt, axis=0, keepdims=True)

    rc = lambda c: nc - 1 - c
    return pl.pallas_call(
        body, name=name, grid=(nc,),
        in_specs=[pl.BlockSpec((LCH, 2048), lambda c: (rc(c), 0)), pl.BlockSpec((LCH, 512), lambda c: (rc(c), 4)),
                  pl.BlockSpec((LCH, 512), lambda c: (rc(c), 5)),
                  pl.BlockSpec((LCH, 128), lambda c: (rc(c), P_DT // 128)),
                  pl.BlockSpec((None, NST, 2048), lambda c: (rc(c), 0, 0)),
                  pl.BlockSpec((LCH, 2048), lambda c: (rc(c), 0)),
                  _vec(128), _vec(128), _vec(2048), pl.BlockSpec((128, 2048), lambda c: (0, 0)),
                  pl.BlockSpec((LCH, LCH), lambda c: (0, 0))],
        out_specs=[pl.BlockSpec((LCH, CONV_DIM), lambda c: (rc(c), 0)), pl.BlockSpec((LCH, 128), lambda c: (rc(c), 0)),
                   _stats(128)],
        out_shape=[jax.ShapeDtypeStruct((t, CONV_DIM), F32), jax.ShapeDtypeStruct((t, 128), F32),
                   jax.ShapeDtypeStruct((8, 128), F32)],
        scratch_shapes=[pltpu.VMEM((NST, 2048), F32), pltpu.VMEM((LCH, 2048), F32), pltpu.VMEM((LCH, 2048), F32)],
        compiler_params=_cp("arbitrary"),
    )(xc, xc, xc, proj, hsave, dy, dtb_row, alog_row, dx_row, expm, tri)


def ssd_out_fwd(y, proj, nw, name):
    t = y.shape[0]
    tm = min(256, t)

    def body(y_ref, z_ref, nw_ref, o_ref):
        for g in range(NG):
            gc = slice(g * 512, (g + 1) * 512)
            z = z_ref[:, gc]
            yz = y_ref[:, gc] * (z * _sigmoid(z))
            r = lax.rsqrt(jnp.mean(yz * yz, axis=-1, keepdims=True) + EPS)
            o_ref[:, gc] = (yz * r * nw_ref[:, gc]).astype(BF16)

    return pl.pallas_call(body, name=name, grid=(t // tm,),
                          in_specs=[_row(tm, 2048), _row(tm, 2048, P_Z // 2048), _vec(2048)],
                          out_specs=_row(tm, 2048), out_shape=jax.ShapeDtypeStruct((t, 2048), BF16),
                          compiler_params=_cp("parallel"))(y, proj, nw)


def ssd_out_bwd(y, proj, dya, nw, name):
    t = y.shape[0]
    tm = min(256, t)

    def body(y_ref, z_ref, d_ref, nw_ref, dy_ref, dz_ref, st_ref):
        @pl.when(pl.program_id(0) == 0)
        def _():
            st_ref[...] = jnp.zeros_like(st_ref)

        for g in range(NG):
            gc = slice(g * 512, (g + 1) * 512)
            z = z_ref[:, gc]
            yv = y_ref[:, gc]
            s = _sigmoid(z)
            sz = z * s
            yz = yv * sz
            r = lax.rsqrt(jnp.mean(yz * yz, axis=-1, keepdims=True) + EPS)
            yzn = yz * r
            dv = d_ref[:, gc]
            st_ref[0:1, gc] += jnp.sum(dv * yzn, axis=0, keepdims=True)
            dyn = dv * nw_ref[:, gc]
            dyz = r * (dyn - yzn * jnp.mean(dyn * yzn, axis=-1, keepdims=True))
            dy_ref[:, gc] = dyz * sz
            dz_ref[:, gc] = (dyz * yv * (s * (1.0 + z * (1.0 - s)))).astype(BF16)

    return pl.pallas_call(
        body, name=name, grid=(t // tm,),
        in_specs=[_row(tm, 2048), _row(tm, 2048, P_Z // 2048), _row(tm, 2048), _vec(2048)],
        out_specs=[_row(tm, 2048), _row(tm, 2048), _stats(2048)],
        out_shape=[jax.ShapeDtypeStruct((t, 2048), F32), jax.ShapeDtypeStruct((t, 2048), BF16),
                   jax.ShapeDtypeStruct((8, 2048), F32)],
        compiler_params=_cp("arbitrary"))(y, proj, dya, nw)


def s5_in(u, bsg, name):
    t = u.shape[0]
    tm = min(512, t)

    def body(u_ref, b_ref, o_ref):
        o_ref[...] = _dot(u_ref[...].astype(BF16), b_ref[...])

    return pl.pallas_call(
        body, name=name, grid=(8, t // tm),
        in_specs=[pl.BlockSpec((tm, 128), lambda s, i: (i, s)), pl.BlockSpec((None, 128, 1024), lambda s, i: (s, 0, 0))],
        out_specs=pl.BlockSpec((tm, 1024), lambda s, i: (i, s)),
        out_shape=jax.ShapeDtypeStruct((t, S5NS), F32), compiler_params=_cp("parallel", "parallel"))(u, bsg)


def s5_out(s, csg, u, d_row, name):
    t = u.shape[0]
    tm = min(512, t)

    def body(s_ref, c_ref, u_ref, d_ref, o_ref):
        o_ref[...] = _dot(s_ref[...].astype(BF16), c_ref[...]) + d_ref[...] * u_ref[...]

    return pl.pallas_call(
        body, name=name, grid=(8, t // tm),
        in_specs=[pl.BlockSpec((tm, 1024), lambda s, i: (i, s)), pl.BlockSpec((None, 1024, 128), lambda s, i: (s, 0, 0)),
                  pl.BlockSpec((tm, 128), lambda s, i: (i, s)), pl.BlockSpec((1, 128), lambda s, i: (0, s))],
        out_specs=pl.BlockSpec((tm, 128), lambda s, i: (i, s)),
        out_shape=jax.ShapeDtypeStruct((t, S5W), F32), compiler_params=_cp("parallel", "parallel"))(s, csg, u, d_row)


def s5_out_bwd(dy, csg, s, name):
    t = dy.shape[0]
    tm = min(512, t)

    def body(dy_ref, c_ref, s_ref, e_ref, dc_ref):
        @pl.when(pl.program_id(1) == 0)
        def _():
            dc_ref[...] = jnp.zeros_like(dc_ref)

        dyb = dy_ref[...].astype(BF16)
        e_ref[...] = _nt(dyb, c_ref[...])
        dc_ref[...] += lax.dot_general(s_ref[...].astype(BF16), dyb, (((0,), (0,)), ((), ())),
                                       preferred_element_type=F32)

    return pl.pallas_call(
        body, name=name, grid=(8, t // tm),
        in_specs=[pl.BlockSpec((tm, 128), lambda s, i: (i, s)), pl.BlockSpec((None, 1024, 128), lambda s, i: (s, 0, 0)),
                  pl.BlockSpec((tm, 1024), lambda s, i: (i, s))],
        out_specs=[pl.BlockSpec((tm, 1024), lambda s, i: (i, s)), pl.BlockSpec((None, 1024, 128), lambda s, i: (s, 0, 0))],
        out_shape=[jax.ShapeDtypeStruct((t, S5NS), F32), jax.ShapeDtypeStruct((8, 1024, 128), F32)],
        compiler_params=_cp("parallel", "arbitrary"))(dy, csg, s)


def s5_in_bwd(lam, bsg, u, dy, d_row, name):
    t = u.shape[0]
    tm = min(512, t)

    def body(l_ref, b_ref, u_ref, dy_ref, d_ref, du_ref, db_ref, dd_ref):
        @pl.when(pl.program_id(1) == 0)
        def _():
            db_ref[...] = jnp.zeros_like(db_ref)
            dd_ref[...] = jnp.zeros_like(dd_ref)

        lb = l_ref[...].astype(BF16)
        uv = u_ref[...]
        dyv = dy_ref[...]
        du_ref[...] = _nt(lb, b_ref[...]) + d_ref[...] * dyv
        db_ref[...] += lax.dot_general(uv.astype(BF16), lb, (((0,), (0,)), ((), ())), preferred_element_type=F32)
        dd_ref[...] += jnp.sum(dyv * uv, axis=0, keepdims=True)

    return pl.pallas_call(
        body, name=name, grid=(8, t // tm),
        in_specs=[pl.BlockSpec((tm, 1024), lambda s, i: (i, s)), pl.BlockSpec((None, 128, 1024), lambda s, i: (s, 0, 0)),
                  pl.BlockSpec((tm, 128), lambda s, i: (i, s)), pl.BlockSpec((tm, 128), lambda s, i: (i, s)),
                  pl.BlockSpec((1, 128), lambda s, i: (0, s))],
        out_specs=[pl.BlockSpec((tm, 128), lambda s, i: (i, s)), pl.BlockSpec((None, 128, 1024), lambda s, i: (s, 0, 0)),
                   pl.BlockSpec((1, 128), lambda s, i: (0, s))],
        out_shape=[jax.ShapeDtypeStruct((t, S5W), F32), jax.ShapeDtypeStruct((8, 128, 1024), F32),
                   jax.ShapeDtypeStruct((1, S5W), F32)],
        compiler_params=_cp("parallel", "arbitrary"))(lam, bsg, u, dy, d_row)


def _cstep(ar, ai, sr, si, br, bi):
    return ar * sr - ai * si + br, ar * si + ai * sr + bi


def s5_scan_ends(b3, a_re, a_im, reverse, name):
    lseg = b3.shape[0]
    ti = min(128, lseg)
    nb = lseg // ti
    sgn = -1.0 if reverse else 1.0

    def body(b_ref, ar_ref, ai_ref, o_ref, sr_scr, si_scr):
        tb = pl.program_id(1)

        @pl.when(tb == 0)
        def _():
            sr_scr[...] = jnp.zeros_like(sr_scr)
            si_scr[...] = jnp.zeros_like(si_scr)

        ar = ar_ref[...]
        ai = sgn * ai_ref[...]

        def step(k, carry):
            tt = ti - 1 - k if reverse else k
            return _cstep(ar, ai, carry[0], carry[1], b_ref[tt, :, 0:512], b_ref[tt, :, 512:1024])

        sr, si = lax.fori_loop(0, ti, step, (sr_scr[...], si_scr[...]), unroll=8)
        sr_scr[...] = sr
        si_scr[...] = si

        @pl.when(tb == nb - 1)
        def _():
            o_ref[:, 0:512] = sr
            o_ref[:, 512:1024] = si

    tmap = (lambda s, tb: (nb - 1 - tb, 0, s)) if reverse else (lambda s, tb: (tb, 0, s))
    return pl.pallas_call(
        body, name=name, grid=(8, nb),
        in_specs=[pl.BlockSpec((ti, 8, 1024), tmap), pl.BlockSpec((None, 8, 512), lambda s, tb: (s, 0, 0)),
                  pl.BlockSpec((None, 8, 512), lambda s, tb: (s, 0, 0))],
        out_specs=pl.BlockSpec((None, 8, 1024), lambda s, tb: (s, 0, 0)),
        out_shape=jax.ShapeDtypeStruct((8, 8, 1024), F32),
        scratch_shapes=[pltpu.VMEM((8, 512), F32), pltpu.VMEM((8, 512), F32)],
        compiler_params=_cp("parallel", "arbitrary"))(b3, a_re, a_im)


def s5_scan_init(ends, a_re, a_im, lseg, reverse, name):
    nsq = int(math.log2(lseg))
    assert 2 ** nsq == lseg
    sgn = -1.0 if reverse else 1.0
    order = list(range(7, -1, -1)) if reverse else list(range(8))

    def body(e_ref, ar_ref, ai_ref, o_ref):
        pr = ar_ref[0:1, :]
        pi = sgn * ai_ref[0:1, :]
        for _ in range(nsq):
            pr, pi = pr * pr - pi * pi, 2.0 * pr * pi
        prev_r = jnp.zeros((1, 512), F32)
        prev_i = jnp.zeros((1, 512), F32)
        j0 = order[0]
        o_ref[j0:j0 + 1, 0:512] = prev_r
        o_ref[j0:j0 + 1, 512:1024] = prev_i
        for idx in range(1, 8):
            j, jp = order[idx], order[idx - 1]
            prev_r, prev_i = _cstep(pr, pi, prev_r, prev_i, e_ref[jp:jp + 1, 0:512], e_ref[jp:jp + 1, 512:1024])
            o_ref[j:j + 1, 0:512] = prev_r
            o_ref[j:j + 1, 512:1024] = prev_i

    return pl.pallas_call(
        body, name=name, grid=(8,),
        in_specs=[pl.BlockSpec((None, 8, 1024), lambda s: (s, 0, 0)), pl.BlockSpec((None, 8, 512), lambda s: (s, 0, 0)),
                  pl.BlockSpec((None, 8, 512), lambda s: (s, 0, 0))],
        out_specs=pl.BlockSpec((None, 8, 1024), lambda s: (s, 0, 0)),
        out_shape=jax.ShapeDtypeStruct((8, 8, 1024), F32), compiler_params=_cp("parallel"))(ends, a_re, a_im)


def s5_scan_fwd(b3, init, a_re, a_im, name):
    lseg = b3.shape[0]
    ti = min(128, lseg)
    nb = lseg // ti

    def body(b_ref, i_ref, ar_ref, ai_ref, o_ref, sr_scr, si_scr):
        @pl.when(pl.program_id(1) == 0)
        def _():
            sr_scr[...] = i_ref[:, 0:512]
            si_scr[...] = i_ref[:, 512:1024]

        ar = ar_ref[...]
        ai = ai_ref[...]

        def step(k, carry):
            nr, ni = _cstep(ar, ai, carry[0], carry[1], b_ref[k, :, 0:512], b_ref[k, :, 512:1024])
            o_ref[k, :, 0:512] = nr
            o_ref[k, :, 512:1024] = ni
            return nr, ni

        sr, si = lax.fori_loop(0, ti, step, (sr_scr[...], si_scr[...]), unroll=8)
        sr_scr[...] = sr
        si_scr[...] = si

    return pl.pallas_call(
        body, name=name, grid=(8, nb),
        in_specs=[pl.BlockSpec((ti, 8, 1024), lambda s, tb: (tb, 0, s)), pl.BlockSpec((None, 8, 1024), lambda s, tb: (s, 0, 0)),
                  pl.BlockSpec((None, 8, 512), lambda s, tb: (s, 0, 0)), pl.BlockSpec((None, 8, 512), lambda s, tb: (s, 0, 0))],
        out_specs=pl.BlockSpec((ti, 8, 1024), lambda s, tb: (tb, 0, s)),
        out_shape=jax.ShapeDtypeStruct(b3.shape, F32),
        scratch_shapes=[pltpu.VMEM((8, 512), F32), pltpu.VMEM((8, 512), F32)],
        compiler_params=_cp("parallel", "arbitrary"))(b3, init, a_re, a_im)


def s5_scan_bwd(e3, linit, s3, sinit, a_re, a_im, name):
    lseg = e3.shape[0]
    ti = min(128, lseg)
    nb = lseg // ti

    def body(e_ref, li_ref, s_ref, sh_ref, si0_ref, ar_ref, ai_ref, o_ref, da_ref, lr_scr, lim_scr):
        tb = pl.program_id(1)

        @pl.when(tb == 0)
        def _():
            lr_scr[...] = li_ref[:, 0:512]
            lim_scr[...] = li_ref[:, 512:1024]
            da_ref[...] = jnp.zeros_like(da_ref)

        ar = ar_ref[...]
        ai = -ai_ref[...]

        def one(tt, lr, li, dar, dai, spr, spi):
            nr, ni = _cstep(ar, ai, lr, li, e_ref[tt, :, 0:512], e_ref[tt, :, 512:1024])
            o_ref[tt, :, 0:512] = nr
            o_ref[tt, :, 512:1024] = ni
            return nr, ni, dar + nr * spr + ni * spi, dai + ni * spr - nr * spi

        def step(k, carry):
            tt = ti - 1 - k
            return one(tt, *carry, s_ref[tt - 1, :, 0:512], s_ref[tt - 1, :, 512:1024])

        z = jnp.zeros((8, 512), F32)
        lr, li, dar, dai = lax.fori_loop(0, ti - 1, step, (lr_scr[...], lim_scr[...], z, z), unroll=8)
        first = tb == nb - 1
        spr = jnp.where(first, si0_ref[:, 0:512], sh_ref[0, :, 0:512])
        spi = jnp.where(first, si0_ref[:, 512:1024], sh_ref[0, :, 512:1024])
        lr, li, dar, dai = one(0, lr, li, dar, dai, spr, spi)
        lr_scr[...] = lr
        lim_scr[...] = li
        da_ref[:, 0:512] += dar
        da_ref[:, 512:1024] += dai

    rb = lambda tb: nb - 1 - tb
    return pl.pallas_call(
        body, name=name, grid=(8, nb),
        in_specs=[pl.BlockSpec((ti, 8, 1024), lambda s, tb: (rb(tb), 0, s)),
                  pl.BlockSpec((None, 8, 1024), lambda s, tb: (s, 0, 0)),
                  pl.BlockSpec((ti, 8, 1024), lambda s, tb: (rb(tb), 0, s)),
                  pl.BlockSpec((1, 8, 1024), lambda s, tb: (jnp.maximum(rb(tb) * ti - 1, 0), 0, s)),
                  pl.BlockSpec((None, 8, 1024), lambda s, tb: (s, 0, 0)),
                  pl.BlockSpec((None, 8, 512), lambda s, tb: (s, 0, 0)), pl.BlockSpec((None, 8, 512), lambda s, tb: (s, 0, 0))],
        out_specs=[pl.BlockSpec((ti, 8, 1024), lambda s, tb: (rb(tb), 0, s)),
                   pl.BlockSpec((None, 8, 1024), lambda s, tb: (s, 0, 0))],
        out_shape=[jax.ShapeDtypeStruct(e3.shape, F32), jax.ShapeDtypeStruct((8, 8, 1024), F32)],
        scratch_shapes=[pltpu.VMEM((8, 512), F32), pltpu.VMEM((8, 512), F32)],
        compiler_params=_cp("parallel", "arbitrary"))(e3, linit, s3, s3, sinit, a_re, a_im)


_GC = math.sqrt(2.0 / math.pi)


def gelu_fwd(y, name):
    t, w = y.shape
    tm = min(512, t)

    def body(y_ref, o_ref):
        v = y_ref[...]
        o_ref[...] = (0.5 * v * (1.0 + jnp.tanh(_GC * (v + 0.044715 * v * v * v)))).astype(BF16)

    return pl.pallas_call(body, name=name, grid=(t // tm,), in_specs=[_row(tm, w)], out_specs=_row(tm, w),
                          out_shape=jax.ShapeDtypeStruct((t, w), BF16), compiler_params=_cp("parallel"))(y)


def gelu_bwd(y, dg, name):
    t, w = y.shape
    tm = min(512, t)

    def body(y_ref, d_ref, o_ref):
        v = y_ref[...]
        th = jnp.tanh(_GC * (v + 0.044715 * v * v * v))
        o_ref[...] = d_ref[...] * (0.5 * (1.0 + th) + 0.5 * v * (1.0 - th * th) * _GC * (1.0 + 3.0 * 0.044715 * v * v))

    return pl.pallas_call(body, name=name, grid=(t // tm,), in_specs=[_row(tm, w), _row(tm, w)], out_specs=_row(tm, w),
                          out_shape=jax.ShapeDtypeStruct((t, w), F32), compiler_params=_cp("parallel"))(y, dg)


def merge_fwd(proj, pa, glu, name):
    t = pa.shape[0]
    tm = min(256, t)

    def body(g_ref, pa_ref, glu_ref, o_ref):
        pb = glu_ref[:, 0:D] * _sigmoid(glu_ref[:, D:2 * D])
        o_ref[...] = (_sigmoid(g_ref[:, 0:D]) * pa_ref[...] + _sigmoid(g_ref[:, D:2 * D]) * pb).astype(BF16)

    return pl.pallas_call(body, name=name, grid=(t // tm,), in_specs=[_row(tm, 2 * D), _row(tm, D), _row(tm, 2 * D)],
                          out_specs=_row(tm, D), out_shape=jax.ShapeDtypeStruct((t, D), BF16),
                          compiler_params=_cp("parallel"))(proj, pa, glu)


def merge_bwd(proj, pa, glu, dm, name):
    t = pa.shape[0]
    tm = min(256, t)

    def body(g_ref, pa_ref, glu_ref, dm_ref, dpa_ref, dglu_ref, dg_ref):
        dmv = dm_ref[...]
        pav = pa_ref[...]
        sa = _sigmoid(g_ref[:, 0:D])
        sb = _sigmoid(g_ref[:, D:2 * D])
        ga = glu_ref[:, 0:D]
        sg = _sigmoid(glu_ref[:, D:2 * D])
        pb = ga * sg
        dpb = sb * dmv
        dpa_ref[...] = (sa * dmv).astype(BF16)
        dglu_ref[:, 0:D] = (dpb * sg).astype(BF16)
        dglu_ref[:, D:2 * D] = (dpb * pb * (1.0 - sg)).astype(BF16)
        dg_ref[:, 0:D] = (dmv * pav * sa * (1.0 - sa)).astype(BF16)
        dg_ref[:, D:2 * D] = (dmv * pb * sb * (1.0 - sb)).astype(BF16)

    return pl.pallas_call(
        body, name=name, grid=(t // tm,),
        in_specs=[_row(tm, 2 * D), _row(tm, D), _row(tm, 2 * D), _row(tm, D)],
        out_specs=[_row(tm, D), _row(tm, 2 * D), _row(tm, 2 * D)],
        out_shape=[jax.ShapeDtypeStruct((t, D), BF16), jax.ShapeDtypeStruct((t, 2 * D), BF16),
                   jax.ShapeDtypeStruct((t, 2 * D), BF16)],
        compiler_params=_cp("parallel"))(proj, pa, glu, dm)


def adamw(w, parts, m, v, name):
    r, c = w.shape
    p = parts.shape[0]
    tr = r if r <= 128 else 128
    c1 = 1.0 - ADAM_B1 ** ADAM_STEP
    c2 = 1.0 - ADAM_B2 ** ADAM_STEP

    def body(w_ref, p_ref, m_ref, v_ref, g_ref, d_ref, nm_ref, nv_ref):
        g = p_ref[0].astype(F32)
        for k in range(1, p):
            g = g + p_ref[k].astype(F32)
        mn = ADAM_B1 * m_ref[...] + (1.0 - ADAM_B1) * g
        vn = ADAM_B2 * v_ref[...] + (1.0 - ADAM_B2) * (g * g)
        g_ref[...] = g
        nm_ref[...] = mn
        nv_ref[...] = vn
        d_ref[...] = -ADAM_LR * ((mn / c1) / (jnp.sqrt(vn / c2) + ADAM_EPS) + ADAM_WD * w_ref[...])

    spec = pl.BlockSpec((tr, c), lambda i: (i, 0))
    o = jax.ShapeDtypeStruct((r, c), F32)
    return pl.pallas_call(
        body, name=name, grid=(pl.cdiv(r, tr),),
        in_specs=[spec, pl.BlockSpec((p, tr, c), lambda i: (0, i, 0)), spec, spec],
        out_specs=[spec, spec, spec, spec], out_shape=[o, o, o, o], compiler_params=_cp("parallel"))(w, parts, m, v)


def _s5_discretise(lambda_re, lambda_im, log_dt, b_re, b_im):
    dt = jnp.exp(log_dt)[:, None]
    lr = jnp.minimum(lambda_re, -1e-4)
    li = lambda_im
    mag = jnp.exp(lr * dt)
    ar = mag * jnp.cos(li * dt)
    ai = mag * jnp.sin(li * dt)
    den = lr * lr + li * li
    nr = ar - 1.0
    kr = (nr * lr + ai * li) / den
    ki = (ai * lr - nr * li) / den
    bbar_re = kr[..., None] * b_re - ki[..., None] * b_im
    bbar_im = kr[..., None] * b_im + ki[..., None] * b_re
    return ar, ai, bbar_re, bbar_im


def _bsg_of(bb_re, bb_im):
    eye = jnp.eye(8, dtype=F32)
    f = lambda b: jnp.einsum("sgpi,gh->sgihp", b.reshape(8, 8, 64, 16), eye).reshape(8, 128, 512)
    return jnp.concatenate([f(bb_re), f(bb_im)], axis=2)


def _bsg_diag(dbsg):
    eye = jnp.eye(8, dtype=F32)
    f = lambda x: jnp.einsum("sgihp,gh->sgpi", x.reshape(8, 8, 16, 8, 64), eye).reshape(64, 64, 16)
    return f(dbsg[:, :, 0:512]), f(dbsg[:, :, 512:1024])


def _csg_of(c_re, c_im):
    eye = jnp.eye(8, dtype=F32)
    f = lambda c: jnp.einsum("sgip,gh->sgphi", c.reshape(8, 8, 16, 64), eye).reshape(8, 512, 128)
    return jnp.concatenate([f(c_re), -f(c_im)], axis=1)


def _csg_diag(dcsg):
    eye = jnp.eye(8, dtype=F32)
    f = lambda x: jnp.einsum("sgphi,gh->sgip", x.reshape(8, 8, 64, 8, 16), eye).reshape(64, 16, 64)
    return f(dcsg[:, 0:512, :]), -f(dcsg[:, 512:1024, :])


def _perm(a, t):
    return a.reshape(8, t // 8, a.shape[1]).transpose(1, 0, 2).reshape(t, a.shape[1])


def _unperm(a, t):
    return a.reshape(t // 8, 8, a.shape[1]).transpose(1, 0, 2).reshape(t, a.shape[1])


def _cols(g):
    return g.transpose(1, 0, 2).reshape(g.shape[1], N_DEV * g.shape[2])


def _rows(g):
    return g.reshape(N_DEV * g.shape[1], g.shape[2])


def _col_parts(g):
    r, c = g.shape
    return g.reshape(r, N_DEV, c // N_DEV).transpose(1, 0, 2)


def _row_parts(g):
    r, c = g.shape
    return g.reshape(N_DEV, r // N_DEV, c)


def _pad_ffn_in(w):
    z = jnp.zeros((D, D_FFP - D_FF), w.dtype)
    return jnp.concatenate([w[:, :D_FF], z, w[:, D_FF:], z], axis=1)


def _unpad_ffn_in(g):
    return jnp.concatenate([g[:, :D_FF], g[:, D_FFP:D_FFP + D_FF]], axis=1)


def _pad_w_in(w):
    z = jnp.zeros((D, INP - P_DT - NH), w.dtype)
    return jnp.concatenate([w[:, O_GA:O_GB], w[:, O_GB:IN_COLS], w[:, 0:O_XBC], w[:, O_XBC:O_DT], w[:, O_U:O_GA],
                            w[:, O_DT:O_U], z], axis=1)


def _unpad_w_in(g):
    return jnp.concatenate([g[:, P_Z:P_XBC], g[:, P_XBC:P_U], g[:, P_DT:P_DT + NH], g[:, P_U:P_DT],
                            g[:, 0:D], g[:, D:2 * D]], axis=1)


_PACK = (("b_ada", 18432), ("norm_ffn1", 2048), ("norm_mix", 2048), ("conv_b", 3072), ("dt_bias", 32), ("a_log", 32),
         ("d_ssd", 32), ("ssd_norm_w", 2048), ("s5_lambda_re", 4096), ("s5_lambda_im", 4096), ("s5_b_re", 65536),
         ("s5_b_im", 65536), ("s5_c_re", 65536), ("s5_c_im", 65536), ("s5_d", 1024), ("s5_log_dt", 64),
         ("norm_ffn2", 2048), ("norm_final", 2048), ("loss", 1))
_PACK_ROWS = 304
_PACK_W = 1024


def _pack(d):
    flat = jnp.concatenate([d[k].reshape(-1).astype(F32) for k, _ in _PACK])
    return jnp.pad(flat, (0, _PACK_ROWS * _PACK_W - flat.shape[0])).reshape(_PACK_ROWS, _PACK_W)


def _unpack(a):
    flat = a.reshape(-1)
    out, off = {}, 0
    for k, n in _PACK:
        out[k] = flat[off:off + n]
        off += n
    return out


def _ffn_bwd(df, h, ab, act, w_in_t, w_out_t, tag):
    dact = mm(df, w_out_t, tn=512, name=tag + "_dact")
    dw_out = mm(act, df, ta=True, out_dtype=BF16, tm=512, tn=1024, tk=1024, name=tag + "_dwout")
    dab = swiglu_bwd(ab, dact, name=tag + "_dab")
    dw_in, (x_out,) = mm(h, dab, ta=True, out_dtype=BF16, tm=1024, tn=1024, tk=1024, name=tag + "_dwin",
                         comm=[("xc", _row_parts(dw_out[:D_FF]))])
    dh, (x_in,) = mm(dab, w_in_t, tk=2816, name=tag + "_dh", comm=[("xc", _col_parts(_unpad_ffn_in(dw_in)))])
    return dh, x_in, x_out


def kernel(x, c, w_ada, b_ada, norm_ffn1, w_ffn1_in, w_ffn1_out, norm_mix, w_in, conv_w, conv_b, dt_bias, a_log, d_ssd, ssd_norm_w, w_a_proj, s5_lambda_re, s5_lambda_im, s5_b_re, s5_b_im, s5_c_re, s5_c_im, s5_d, s5_log_dt, w_b_glu, w_out, norm_ffn2, w_ffn2_in, w_ffn2_out, norm_final, loss_target, m_w_ada, m_b_ada, m_norm_ffn1, m_w_ffn1_in, m_w_ffn1_out, m_norm_mix, m_w_in, m_conv_w, m_conv_b, m_dt_bias, m_a_log, m_d_ssd, m_ssd_norm_w, m_w_a_proj, m_s5_lambda_re, m_s5_lambda_im, m_s5_b_re, m_s5_b_im, m_s5_c_re, m_s5_c_im, m_s5_d, m_s5_log_dt, m_w_b_glu, m_w_out, m_norm_ffn2, m_w_ffn2_in, m_w_ffn2_out, m_norm_final, v_w_ada, v_b_ada, v_norm_ffn1, v_w_ffn1_in, v_w_ffn1_out, v_norm_mix, v_w_in, v_conv_w, v_conv_b, v_dt_bias, v_a_log, v_d_ssd, v_ssd_norm_w, v_w_a_proj, v_s5_lambda_re, v_s5_lambda_im, v_s5_b_re, v_s5_b_im, v_s5_c_re, v_s5_c_im, v_s5_d, v_s5_log_dt, v_w_b_glu, v_w_out, v_norm_ffn2, v_w_ffn2_in, v_w_ffn2_out, v_norm_final):
    args = dict(locals())
    t = x.shape[1]
    me = _my_id()
    xt = x[0]
    tgt = loss_target[0]
    small = {k: args[k] for k, _ in _PACK if k != "loss"}

    bf = lambda w: w[0].astype(BF16)
    pad_rows = lambda w: jnp.pad(w, ((0, D_FFP - D_FF), (0, 0)))

    c8 = all_gather(c, "ag_c").reshape(N_DEV, D)
    b_loc = lax.dynamic_slice(b_ada, (0, me * (N_ADA * D // N_DEV)), (1, N_ADA * D // N_DEV))
    m8 = ada_fwd(c8, w_ada[0], b_loc, "ada_fwd")
    mods, g_f1i, g_cw = comm_call([("xc", m8.reshape(N_DEV, 1, -1)), ("ag", bf(w_ffn1_in)), ("ag", conv_w[0])],
                                  "xc_mods_ag_ffn1_in")
    mods = mods.reshape(1, N_ADA * D)
    convw = _cols(g_cw)
    wf1i = _pad_ffn_in(_cols(g_f1i))

    h1 = mod_fwd(xt, norm_ffn1, mods, 0, 1, name="mod1")
    ab1, (g_f1o, g_wap, g_wo) = mm(h1, wf1i, out_dtype=BF16, name="ffn1_in",
                                   comm=[("ag", bf(w_ffn1_out)), ("ag", bf(w_a_proj)), ("ag", bf(w_out))])
    wf1o, wap, wo = pad_rows(_rows(g_f1o)), _rows(g_wap), _rows(g_wo)
    act1 = swiglu_fwd(ab1, name="ffn1_act")
    f1, (g_win,) = mm(act1, wf1o, tk=5632, name="ffn1_out", comm=[("ag", bf(w_in))])
    winp = _pad_w_in(_cols(g_win))
    x1, h2 = mod_fwd(xt, norm_mix, mods, 3, 4, f=f1, gk=2, gscale=0.5, name="mod2")
    proj, (g_f2i, g_wbg) = mm(h2, winp, tm=1024, tn=512, name="w_in", comm=[("ag", bf(w_ffn2_in)), ("ag", bf(w_b_glu))])
    wf2i, wbg = _pad_ffn_in(_cols(g_f2i)), _cols(g_wbg)
    cb_row = conv_b
    xc = conv_fwd(proj, convw, cb_row, "conv_fwd")
    row128 = lambda v: jnp.pad(v.reshape(1, -1), ((0, 0), (0, 128 - v.size)))
    dtb_row, alog_row = row128(dt_bias), row128(a_log)
    dx_row = jnp.repeat(d_ssd.reshape(-1), HP).reshape(1, 2048)
    expm = (jnp.arange(128)[:, None] == (jnp.arange(2048)[None, :] // HP)).astype(F32)
    tri = (jnp.arange(LCH)[:, None] >= jnp.arange(LCH)[None, :]).astype(F32)
    y_ssd, hsave = ssd_fwd(xc, proj, dtb_row, alog_row, dx_row, expm, tri, "ssd_fwd")
    ya = ssd_out_fwd(y_ssd, proj, ssd_norm_w, "ssd_out")
    pa = mm(ya, wap, name="w_a_proj")

    s5p = (s5_lambda_re[0], s5_lambda_im[0], s5_log_dt[0], s5_b_re[0], s5_b_im[0])
    (ar, ai, bb_re, bb_im), s5_vjp = jax.vjp(_s5_discretise, *s5p)
    a_re8 = jnp.broadcast_to(ar.reshape(8, 1, 512), (8, 8, 512))
    a_im8 = jnp.broadcast_to(ai.reshape(8, 1, 512), (8, 8, 512))
    bsg = _bsg_of(bb_re, bb_im).astype(BF16)
    csg = _csg_of(s5_c_re[0], s5_c_im[0]).astype(BF16)
    d_row = s5_d.reshape(1, S5W)
    lseg = t // 8
    u_p = _perm(proj[:, P_U:P_U + S5W], t)
    bu3 = s5_in(u_p, bsg, "s5_in").reshape(lseg, 8, S5NS)
    sinit = s5_scan_init(s5_scan_ends(bu3, a_re8, a_im8, False, "s5_ends_f"), a_re8, a_im8, lseg, False, "s5_init_f")
    s3 = s5_scan_fwd(bu3, sinit, a_re8, a_im8, "s5_scan_f")
    s2 = s3.reshape(t, S5NS)
    yb_p = s5_out(s2, csg, u_p, d_row, "s5_out")
    yb = _unperm(yb_p, t)
    gy = gelu_fwd(yb, "gelu")
    glu = mm(gy, wbg, name="w_b_glu")
    merged = merge_fwd(proj, pa, glu, "merge")
    o = mm(merged, wo, name="w_out")
    x2, h3 = mod_fwd(x1, norm_ffn2, mods, 6, 7, f=o, gk=5, gscale=1.0, name="mod3")
    ab3, (g_f2o,) = mm(h3, wf2i, out_dtype=BF16, name="ffn2_in", comm=[("ag", bf(w_ffn2_out))])
    wf2o = pad_rows(_rows(g_f2o))
    act3 = swiglu_fwd(ab3, name="ffn2_act")
    f3 = mm(act3, wf2o, tk=5632, name="ffn2_out")

    dx3, df3, st_fin = final_fwd_bwd(x2, f3, mods, norm_final.reshape(1, D), tgt, "final")
    dh3, x_f2i, x_f2o = _ffn_bwd(df3, h3, ab3, act3, wf2i.T, wf2o.T, "ffn2")
    dx2, do, st3 = mod_bwd(x2, dh3, dx3, norm_ffn2, mods, 7, fprev=o, gk=5, gscale=1.0, name="mod3_bwd")

    dmerged = mm(do, wo.T, name="w_out_dx")
    dwo = mm(merged, do, ta=True, out_dtype=BF16, tm=1024, tn=1024, tk=1024, name="w_out_dw")
    dpa, dglu, dgates = merge_bwd(proj, pa, glu, dmerged, "merge_bwd")
    dwbg = mm(gy, dglu, ta=True, out_dtype=BF16, tm=1024, tn=1024, tk=1024, name="w_b_glu_dw")
    dgy, (x_wo,) = mm(dglu, wbg.T, name="w_b_glu_dx", comm=[("xc", _row_parts(dwo))])
    dyb_p = _perm(gelu_bwd(yb, dgy, "gelu_bwd"), t)
    e2, dcsg = s5_out_bwd(dyb_p, csg, s2, "s5_out_bwd")
    e3 = e2.reshape(lseg, 8, S5NS)
    linit = s5_scan_init(s5_scan_ends(e3, a_re8, a_im8, True, "s5_ends_b"), a_re8, a_im8, lseg, True, "s5_init_b")
    lam3, da8 = s5_scan_bwd(e3, linit, s3, sinit, a_re8, a_im8, "s5_scan_b")
    du_p, dbsg, dd_row = s5_in_bwd(lam3.reshape(t, S5NS), bsg, u_p, dyb_p, d_row, "s5_in_bwd")
    du = _unperm(du_p, t).astype(BF16)
    da = jnp.sum(da8, axis=1)
    dbb_re, dbb_im = _bsg_diag(dbsg)
    g_lre, g_lim, g_ldt, g_bre, g_bim = s5_vjp((da[:, 0:512].reshape(64, 64), da[:, 512:1024].reshape(64, 64),
                                                dbb_re, dbb_im))
    g_cre, g_cim = _csg_diag(dcsg)

    dwap = mm(ya, dpa, ta=True, out_dtype=BF16, tm=1024, tn=1024, tk=1024, name="w_a_proj_dw")
    dya, (x_wbg,) = mm(dpa, wap.T, name="w_a_proj_dx", comm=[("xc", _col_parts(dwbg))])
    dy_ssd, dz, st_sn = ssd_out_bwd(y_ssd, proj, dya, ssd_norm_w, "ssd_out_bwd")
    dxc, ddt, st_ssd = ssd_bwd(xc, proj, hsave, dy_ssd, dtb_row, alog_row, dx_row, expm, tri, "ssd_bwd")
    dpre, st_cv = conv_bwd_pre(proj, dxc, convw, cb_row, "conv_bwd_pre")
    dxbc = conv_bwd_in(dpre, convw, "conv_bwd_in")
    dproj = jnp.concatenate([dgates, dz, dxbc, du, ddt.astype(BF16), jnp.zeros((t, INP - P_DT - 128), BF16)], axis=1)
    dwinp = mm(h2, dproj, ta=True, out_dtype=BF16, tm=1024, tn=512, tk=1024, name="w_in_dw")
    dh2, (x_win, x_wap, x_cw) = mm(
        dproj, winp.T, tk=2688, name="w_in_dx",
        comm=[("xc", _col_parts(_unpad_w_in(dwinp))), ("xc", _row_parts(dwap)), ("xc", _col_parts(st_cv[0:CONV_K]))])
    dx1, df1, st2 = mod_bwd(x1, dh2, dx2, norm_mix, mods, 4, fprev=f1, gk=2, gscale=0.5, name="mod2_bwd")
    dh1, x_f1i, x_f1o = _ffn_bwd(df1, h1, ab1, act1, wf1i.T, wf1o.T, "ffn1")
    gx, st1 = mod_bwd(xt, dh1, dx1, norm_ffn1, mods, 1, name="mod1_bwd")

    dmods = jnp.concatenate([st1[0], st1[1], st2[3], st2[0], st2[1], st3[3], st3[0], st3[1], st_fin[1]])
    part = {"b_ada": dmods, "norm_ffn1": st1[2], "norm_mix": st2[2], "conv_b": st_cv[4], "dt_bias": st_ssd[1, 0:NH],
            "a_log": st_ssd[0, 0:NH], "d_ssd": st_ssd[2, 0:NH], "ssd_norm_w": st_sn[0], "s5_lambda_re": g_lre,
            "s5_lambda_im": g_lim, "s5_b_re": g_bre, "s5_b_im": g_bim, "s5_c_re": g_cre, "s5_c_im": g_cim,
            "s5_d": dd_row, "s5_log_dt": g_ldt, "norm_ffn2": st3[2], "norm_final": st_fin[0],
            "loss": (0.5 / D) * jnp.sum(st_fin[2])}
    zero = {"loss": jnp.zeros((1,), F32)}
    gath = all_gather(_pack(part), "ag_small")
    sg, sd, sm, sv = adamw(_pack({**small, **zero}), gath, _pack({**{k: args["m_" + k] for k in small}, **zero}),
                           _pack({**{k: args["v_" + k] for k in small}, **zero}), "adamw_small")
    sg, sd, sm, sv = _unpack(sg), _unpack(sd), _unpack(sm), _unpack(sv)
    loss = sg["loss"][0]

    dm_loc = lax.dynamic_slice(gath.reshape(N_DEV, -1)[:, 0:N_ADA * D], (0, me * (N_ADA * D // N_DEV)),
                               (N_DEV, N_ADA * D // N_DEV))
    g_ada = ada_bwd(c8.T, dm_loc, "ada_bwd")
    big = {"w_ada": g_ada[None], "w_ffn1_in": x_f1i, "w_ffn1_out": x_f1o, "w_in": x_win, "conv_w": x_cw,
           "w_a_proj": x_wap, "w_b_glu": x_wbg, "w_out": x_wo, "w_ffn2_in": x_f2i, "w_ffn2_out": x_f2o}
    res = {}
    for k, parts in big.items():
        res[k] = adamw(args[k][0], parts, args["m_" + k][0], args["v_" + k][0], "adamw_" + k)

    names = ["w_ada", "b_ada", "norm_ffn1", "w_ffn1_in", "w_ffn1_out", "norm_mix", "w_in", "conv_w", "conv_b", "dt_bias",
             "a_log", "d_ssd", "ssd_norm_w", "w_a_proj", "s5_lambda_re", "s5_lambda_im", "s5_b_re", "s5_b_im", "s5_c_re",
             "s5_c_im", "s5_d", "s5_log_dt", "w_b_glu", "w_out", "norm_ffn2", "w_ffn2_in", "w_ffn2_out", "norm_final"]
    outs = [loss, gx[None]]
    for q, src in enumerate((sg, sd, sm, sv)):
        for k in names:
            if k in res:
                outs.append(res[k][q][None])
            else:
                outs.append(src[k].reshape(args[k].shape))
    return tuple(outs)
```

```python
import functools
import math

import jax
import jax.numpy as jnp
from jax import lax
from jax.experimental import pallas as pl
from jax.experimental.pallas import tpu as pltpu

F32 = jnp.float32
BF16 = jnp.bfloat16
HI = lax.Precision.HIGHEST

N_DEV = 8
D = 2048
D_FF = 5504
D_FFP = 5632
NH = 32
HP = 64
NG = 4
NST = 128
LCH = 128
CONV_DIM = 3072
CONV_K = 4
S5W = 1024
S5NS = 8192
N_ADA = 9
EPS = 1e-6
IN_COLS = 10272
INP = 10752
P_GATES, P_Z, P_XBC, P_U, P_DT = 0, 4096, 6144, 9216, 10240
O_XBC, O_DT, O_U, O_GA, O_GB = 2048, 5120, 5152, 6176, 8224
NEG = -1e30
VMEM_LIMIT = 56 * 1024 * 1024

ADAM_LR, ADAM_B1, ADAM_B2, ADAM_EPS, ADAM_WD, ADAM_STEP = 0.001, 0.9, 0.999, 1e-08, 0.01, 10


def _cp(*sem):
    return pltpu.CompilerParams(dimension_semantics=sem, vmem_limit_bytes=VMEM_LIMIT)


def _tile(dim, pref):
    if dim <= pref or dim % pref == 0:
        return min(dim, pref)
    for t in (2048, 1024, 512, 256, 128):
        if t <= pref and dim % t == 0:
            return t
    return dim


def _vec(w, cb=0):
    return pl.BlockSpec((1, w), lambda *_: (0, cb))


def _row(tm, w, cb=0):
    return pl.BlockSpec((tm, w), lambda i: (i, cb))


def _stats(w):
    return pl.BlockSpec((8, w), lambda *_: (0, 0))


def _sigmoid(x):
    return 1.0 / (1.0 + jnp.exp(-x))


def _softplus(x):
    return jnp.maximum(x, 0.0) + jnp.log1p(jnp.exp(-jnp.abs(x)))


def _peer(k):
    x, y, c = lax.axis_index("x"), lax.axis_index("y"), lax.axis_index("c")
    return (x ^ ((k >> 2) & 1), y ^ ((k >> 1) & 1), c ^ (k & 1))


def _my_id():
    return 4 * lax.axis_index("x") + 2 * lax.axis_index("y") + lax.axis_index("c")


def _comm_out_shape(kind, v):
    return jax.ShapeDtypeStruct(((N_DEV,) + v.shape) if kind == "ag" else v.shape, v.dtype)


def _comm_scratch(n):
    return [pltpu.SemaphoreType.DMA((n * N_DEV,)), pltpu.SemaphoreType.DMA((n * N_DEV,))]


def _comm_copies(kinds, srcs, dsts, send_sems, recv_sems):
    me = _my_id()
    out = []
    for q, (kind, src, dst) in enumerate(zip(kinds, srcs, dsts)):
        out.append(pltpu.make_async_copy(src if kind == "ag" else src.at[me], dst.at[me], send_sems.at[q * N_DEV]))
        for k in range(1, N_DEV):
            px, py, pc = _peer(k)
            out.append(pltpu.make_async_remote_copy(
                src_ref=src if kind == "ag" else src.at[4 * px + 2 * py + pc], dst_ref=dst.at[me],
                send_sem=send_sems.at[q * N_DEV + k], recv_sem=recv_sems.at[q * N_DEV + k],
                device_id=(px, py, pc), device_id_type=pl.DeviceIdType.MESH))
    return out


def comm_call(items, name):
    kinds = [k for k, _ in items]
    n = len(items)

    def body(*refs):
        cps = _comm_copies(kinds, refs[:n], refs[n:2 * n], refs[2 * n], refs[2 * n + 1])
        for cp in cps:
            cp.start()
        for cp in cps:
            cp.wait()

    return pl.pallas_call(
        body, name=name,
        in_specs=[pl.BlockSpec(memory_space=pl.ANY)] * n, out_specs=[pl.BlockSpec(memory_space=pl.ANY)] * n,
        out_shape=[_comm_out_shape(k, v) for k, v in items], scratch_shapes=_comm_scratch(n),
    )(*[v for _, v in items])


def all_gather(v, name):
    return comm_call([("ag", v)], name)[0]


def _pcall(body, args, *, name, grid, in_specs, out_specs, out_shape, scratch_shapes=(), sem, comm=()):
    nc, n_in, n_out = len(comm), len(in_specs), len(out_shape)
    if not nc:
        return pl.pallas_call(body, name=name, grid=grid, in_specs=list(in_specs), out_specs=list(out_specs),
                              out_shape=list(out_shape), scratch_shapes=list(scratch_shapes),
                              compiler_params=_cp(*sem))(*args)
    kinds = [k for k, _ in comm]

    def carried(*refs):
        ins, csrc = refs[:n_in], refs[n_in:n_in + nc]
        outs, cdst = refs[n_in + nc:n_in + nc + n_out], refs[n_in + nc + n_out:n_in + 2 * nc + n_out]
        scr = refs[n_in + 2 * nc + n_out:]
        first, last = None, None
        for d, g in enumerate(grid):
            p = pl.program_id(d)
            first = (p == 0) if first is None else first & (p == 0)
            last = (p == g - 1) if last is None else last & (p == g - 1)

        @pl.when(first)
        def _():
            for cp in _comm_copies(kinds, csrc, cdst, scr[-2], scr[-1]):
                cp.start()

        body(*ins, *outs, *scr[:-2])

        @pl.when(last)
        def _():
            for cp in _comm_copies(kinds, csrc, cdst, scr[-2], scr[-1]):
                cp.wait()

    hbm = pl.BlockSpec(memory_space=pl.ANY)
    out = pl.pallas_call(
        carried, name=name, grid=grid, in_specs=list(in_specs) + [hbm] * nc, out_specs=list(out_specs) + [hbm] * nc,
        out_shape=list(out_shape) + [_comm_out_shape(k, v) for k, v in comm],
        scratch_shapes=list(scratch_shapes) + _comm_scratch(nc), compiler_params=_cp(*(("arbitrary",) * len(grid))),
    )(*args, *[v for _, v in comm])
    return list(out[:n_out]), list(out[n_out:])


def mm(a, b, *, ta=False, out_dtype=F32, tm=512, tn=1024, tk=2048, i_outer=False, a_halves=False, b_halves=False,
       name, comm=()):
    if a_halves:
        m, kd = a.shape[1], 2 * a.shape[2]
    elif ta:
        kd, m = a.shape
    else:
        m, kd = a.shape
    kd2, n = (b.shape[1], 2 * b.shape[2]) if b_halves else b.shape
    assert kd == kd2 and not (ta and a_halves), (a.shape, b.shape, ta)
    tm, tn, tk = _tile(m, tm), _tile(n // 2 if b_halves else n, tn), _tile(kd // 2 if a_halves else kd, tk)
    nk = kd // tk
    nkh, njh = nk // 2, n // tn // 2
    grid = (m // tm, n // tn, nk) if i_outer else (n // tn, m // tm, nk)
    dims = (((0,) if ta else (1,), (0,)), ((), ()))

    def ix(f):
        return (lambda i, j, k: f(i, j, k)) if i_outer else (lambda j, i, k: f(i, j, k))

    def body(a_ref, b_ref, o_ref, *scr):
        p = lax.dot_general(a_ref[...], b_ref[...], dims, preferred_element_type=F32)
        if nk == 1:
            o_ref[...] = p.astype(o_ref.dtype)
        else:
            acc = scr[0]
            k = pl.program_id(2)

            @pl.when(k == 0)
            def _():
                acc[...] = p

            @pl.when(k > 0)
            def _():
                acc[...] += p

            @pl.when(k == nk - 1)
            def _():
                o_ref[...] = acc[...].astype(o_ref.dtype)

    if a_halves:
        a_spec = pl.BlockSpec((None, tm, tk), ix(lambda i, j, k: (k // nkh, i, k % nkh)))
    elif ta:
        a_spec = pl.BlockSpec((tk, tm), ix(lambda i, j, k: (k, i)))
    else:
        a_spec = pl.BlockSpec((tm, tk), ix(lambda i, j, k: (i, k)))
    if b_halves:
        b_spec = pl.BlockSpec((None, tk, tn), ix(lambda i, j, k: (j // njh, k, j % njh)))
    else:
        b_spec = pl.BlockSpec((tk, tn), ix(lambda i, j, k: (k, j)))
    out = _pcall(body, (a, b), name=name, grid=grid, in_specs=[a_spec, b_spec],
                 out_specs=[pl.BlockSpec((tm, tn), ix(lambda i, j, k: (i, j)))],
                 out_shape=[jax.ShapeDtypeStruct((m, n), out_dtype)],
                 scratch_shapes=[pltpu.VMEM((tm, tn), F32)] if nk > 1 else [],
                 sem=("parallel", "parallel", "arbitrary"), comm=comm)
    return (out[0][0], out[1]) if comm else out[0]


def ffn_in_act(h, w, name, comm=()):
    t = h.shape[0]
    tm, tn = _tile(t, 512), 512
    nj = D_FFP // tn

    def body(h_ref, wa_ref, wb_ref, ab_ref, act_ref):
        hv = h_ref[...]
        pa = _dot(hv, wa_ref[...])
        pb = _dot(hv, wb_ref[...])
        ab_ref[0] = pa.astype(BF16)
        ab_ref[1] = pb.astype(BF16)
        act_ref[...] = (pa * _sigmoid(pa) * pb).astype(BF16)

    out = _pcall(body, (h, w, w), name=name, grid=(nj, t // tm),
                 in_specs=[pl.BlockSpec((tm, D), lambda j, i: (i, 0)), pl.BlockSpec((D, tn), lambda j, i: (0, j)),
                           pl.BlockSpec((D, tn), lambda j, i: (0, nj + j))],
                 out_specs=[pl.BlockSpec((2, tm, tn), lambda j, i: (0, i, j)), pl.BlockSpec((tm, tn), lambda j, i: (i, j))],
                 out_shape=[jax.ShapeDtypeStruct((2, t, D_FFP), BF16), jax.ShapeDtypeStruct((t, D_FFP), BF16)],
                 sem=("parallel", "parallel"), comm=comm)
    return (out[0][0], out[0][1], out[1]) if comm else (out[0], out[1])


def ffn_dab(df, w_out_t, ab, name):
    t = df.shape[0]
    tm, tn = _tile(t, 1024), 512

    def body(d_ref, w_ref, ab_ref, o_ref):
        dv = _dot(d_ref[...], w_ref[...])
        a = ab_ref[0].astype(F32)
        b = ab_ref[1].astype(F32)
        s = _sigmoid(a)
        o_ref[0] = (dv * b * (s * (1.0 + a * (1.0 - s)))).astype(BF16)
        o_ref[1] = (dv * (a * s)).astype(BF16)

    return _pcall(body, (df, w_out_t, ab), name=name, grid=(D_FFP // tn, t // tm),
                  in_specs=[pl.BlockSpec((tm, D), lambda j, i: (i, 0)), pl.BlockSpec((D, tn), lambda j, i: (0, j)),
                            pl.BlockSpec((2, tm, tn), lambda j, i: (0, i, j))],
                  out_specs=[pl.BlockSpec((2, tm, tn), lambda j, i: (0, i, j))],
                  out_shape=[jax.ShapeDtypeStruct((2, t, D_FFP), BF16)], sem=("parallel", "parallel"))[0]


def ada_fwd(c8, w_loc, b_loc, name):
    n = w_loc.shape[1]
    tn = 256

    def body(c_ref, w_ref, b_ref, o_ref):
        cv = c_ref[...]
        ca = cv * _sigmoid(cv)
        o_ref[...] = jnp.dot(ca, w_ref[...], precision=HI, preferred_element_type=F32) + b_ref[...]

    return pl.pallas_call(
        body, name=name, grid=(n // tn,),
        in_specs=[pl.BlockSpec((N_DEV, D), lambda j: (0, 0)), pl.BlockSpec((D, tn), lambda j: (0, j)),
                  pl.BlockSpec((1, tn), lambda j: (0, j))],
        out_specs=pl.BlockSpec((N_DEV, tn), lambda j: (0, j)),
        out_shape=jax.ShapeDtypeStruct((N_DEV, n), F32), compiler_params=_cp("parallel"),
    )(c8, w_loc, b_loc)


def ada_bwd(c8t, dm_loc, name):
    n = dm_loc.shape[1]
    tn = 256

    def body(c_ref, d_ref, o_ref):
        cv = c_ref[...]
        ca = cv * _sigmoid(cv)
        o_ref[...] = jnp.dot(ca, d_ref[...], precision=HI, preferred_element_type=F32)

    return pl.pallas_call(
        body, name=name, grid=(n // tn,),
        in_specs=[pl.BlockSpec((D, N_DEV), lambda j: (0, 0)), pl.BlockSpec((N_DEV, tn), lambda j: (0, j))],
        out_specs=pl.BlockSpec((D, tn), lambda j: (0, j)),
        out_shape=jax.ShapeDtypeStruct((D, n), F32), compiler_params=_cp("parallel"),
    )(c8t, dm_loc)


def mod_fwd(x, nw, mods, shk, sck, *, f=None, gk=None, gscale=1.0, name):
    t = x.shape[0]
    tm = min(256, t)
    res = f is not None

    def body(*refs):
        if res:
            x_ref, f_ref, g_ref, nw_ref, sh_ref, sc_ref, x1_ref, h_ref = refs
            xv = x_ref[...] + (gscale * g_ref[...]) * f_ref[...]
            x1_ref[...] = xv
        else:
            x_ref, nw_ref, sh_ref, sc_ref, h_ref = refs
            xv = x_ref[...]
        r = lax.rsqrt(jnp.mean(xv * xv, axis=-1, keepdims=True) + EPS)
        h_ref[...] = ((xv * r * nw_ref[...]) * (1.0 + sc_ref[...]) + sh_ref[...]).astype(BF16)

    ins = [x] + ([f, mods] if res else []) + [nw, mods, mods]
    specs = [_row(tm, D)] + ([_row(tm, D), _vec(D, gk)] if res else []) + [_vec(D), _vec(D, shk), _vec(D, sck)]
    outs = ([jax.ShapeDtypeStruct((t, D), F32)] if res else []) + [jax.ShapeDtypeStruct((t, D), BF16)]
    ospecs = ([_row(tm, D)] if res else []) + [_row(tm, D)]
    out = pl.pallas_call(body, name=name, grid=(t // tm,), in_specs=specs, out_specs=ospecs, out_shape=outs,
                         compiler_params=_cp("parallel"))(*ins)
    return out if res else out[0]


def final_fwd_bwd(x2, f3, mods, nf, tgt, name):
    t = x2.shape[0]
    tm = min(128, t)

    def body(x_ref, f_ref, g_ref, nf_ref, t_ref, dx_ref, df_ref, st_ref):
        @pl.when(pl.program_id(0) == 0)
        def _():
            st_ref[...] = jnp.zeros_like(st_ref)

        g = 0.5 * g_ref[...]
        fv = f_ref[...]
        xv = x_ref[...] + g * fv
        r = lax.rsqrt(jnp.mean(xv * xv, axis=-1, keepdims=True) + EPS)
        xh = xv * r
        nfv = nf_ref[...]
        e = xh * nfv - t_ref[...]
        st_ref[2:3, :] += jnp.sum(e * e, axis=0, keepdims=True)
        dy = e * (1.0 / D)
        st_ref[0:1, :] += jnp.sum(dy * xh, axis=0, keepdims=True)
        dxh = dy * nfv
        dx = r * (dxh - xh * jnp.mean(dxh * xh, axis=-1, keepdims=True))
        dx_ref[...] = dx
        df_ref[...] = (g * dx).astype(BF16)
        st_ref[1:2, :] += 0.5 * jnp.sum(fv * dx, axis=0, keepdims=True)

    return pl.pallas_call(
        body, name=name, grid=(t // tm,),
        in_specs=[_row(tm, D), _row(tm, D), _vec(D, 8), _vec(D), _row(tm, D)],
        out_specs=[_row(tm, D), _row(tm, D), _stats(D)],
        out_shape=[jax.ShapeDtypeStruct((t, D), F32), jax.ShapeDtypeStruct((t, D), BF16),
                   jax.ShapeDtypeStruct((8, D), F32)],
        compiler_params=_cp("arbitrary"),
    )(x2, f3, mods, nf, tgt)


def mod_bwd(x_in, dh, dx_out, nw, mods, sck, *, fprev=None, gk=None, gscale=1.0, name):
    t = x_in.shape[0]
    tm = min(128, t)
    gate = fprev is not None

    def body(*refs):
        if gate:
            x_ref, dh_ref, dxo_ref, nw_ref, sc_ref, f_ref, g_ref, dx_ref, df_ref, st_ref = refs
        else:
            x_ref, dh_ref, dxo_ref, nw_ref, sc_ref, dx_ref, st_ref = refs

        @pl.when(pl.program_id(0) == 0)
        def _():
            st_ref[...] = jnp.zeros_like(st_ref)

        xv = x_ref[...]
        dhv = dh_ref[...]
        r = lax.rsqrt(jnp.mean(xv * xv, axis=-1, keepdims=True) + EPS)
        xh = xv * r
        nwv = nw_ref[...]
        st_ref[0:1, :] += jnp.sum(dhv, axis=0, keepdims=True)
        st_ref[1:2, :] += jnp.sum(dhv * (xh * nwv), axis=0, keepdims=True)
        dn = dhv * (1.0 + sc_ref[...])
        st_ref[2:3, :] += jnp.sum(dn * xh, axis=0, keepdims=True)
        dxh = dn * nwv
        dx = dxo_ref[...] + r * (dxh - xh * jnp.mean(dxh * xh, axis=-1, keepdims=True))
        dx_ref[...] = dx
        if gate:
            df_ref[...] = ((gscale * g_ref[...]) * dx).astype(BF16)
            st_ref[3:4, :] += gscale * jnp.sum(f_ref[...] * dx, axis=0, keepdims=True)

    ins = [x_in, dh, dx_out, nw, mods] + ([fprev, mods] if gate else [])
    specs = [_row(tm, D), _row(tm, D), _row(tm, D), _vec(D), _vec(D, sck)] + ([_row(tm, D), _vec(D, gk)] if gate else [])
    outs = [jax.ShapeDtypeStruct((t, D), F32)] + ([jax.ShapeDtypeStruct((t, D), BF16)] if gate else []) + \
        [jax.ShapeDtypeStruct((8, D), F32)]
    ospecs = [_row(tm, D)] + ([_row(tm, D)] if gate else []) + [_stats(D)]
    return pl.pallas_call(body, name=name, grid=(t // tm,), in_specs=specs, out_specs=ospecs, out_shape=outs,
                          compiler_params=_cp("arbitrary"))(*ins)


def _conv_pre(cur, prev8, w, b, tm):
    full = jnp.concatenate([prev8, cur], axis=0)
    pre = b + w[3:4, :] * cur
    for k in range(CONV_K - 1):
        s = CONV_K - 1 - k
        pre = pre + w[k:k + 1, :] * pltpu.roll(full, s, 0)[8:8 + tm, :]
    return pre


def conv_fwd(proj, cw_full, cb_full, name):
    t = proj.shape[0]
    tm = min(256, t)
    cwid = 1024
    cb0 = P_XBC // cwid

    def body(x_ref, p_ref, w_ref, b_ref, o_ref):
        i = pl.program_id(1)
        prev8 = jnp.where(i == 0, 0.0, p_ref[...])
        pre = _conv_pre(x_ref[...], prev8, w_ref[...], b_ref[...], tm)
        o_ref[...] = pre * _sigmoid(pre)

    return pl.pallas_call(
        body, name=name, grid=(CONV_DIM // cwid, t // tm),
        in_specs=[pl.BlockSpec((tm, cwid), lambda j, i: (i, cb0 + j)),
                  pl.BlockSpec((8, cwid), lambda j, i: (jnp.maximum(i * (tm // 8) - 1, 0), cb0 + j)),
                  pl.BlockSpec((CONV_K, cwid), lambda j, i: (0, j)), pl.BlockSpec((1, cwid), lambda j, i: (0, j))],
        out_specs=pl.BlockSpec((tm, cwid), lambda j, i: (i, j)),
        out_shape=jax.ShapeDtypeStruct((t, CONV_DIM), F32), compiler_params=_cp("parallel", "parallel"),
    )(proj, proj, cw_full, cb_full)


def conv_bwd_pre(proj, dxc, cw_full, cb_full, name):
    t = proj.shape[0]
    tm = min(256, t)
    cwid = 1024
    cb0 = P_XBC // cwid

    def body(x_ref, p_ref, d_ref, w_ref, b_ref, o_ref, st_ref):
        i = pl.program_id(1)

        @pl.when(i == 0)
        def _():
            st_ref[...] = jnp.zeros_like(st_ref)

        cur = x_ref[...]
        prev8 = jnp.where(i == 0, 0.0, p_ref[...])
        pre = _conv_pre(cur, prev8, w_ref[...], b_ref[...], tm)
        s = _sigmoid(pre)
        dpre = d_ref[...] * (s * (1.0 + pre * (1.0 - s)))
        o_ref[...] = dpre
        st_ref[4:5, :] += jnp.sum(dpre, axis=0, keepdims=True)
        st_ref[3:4, :] += jnp.sum(dpre * cur, axis=0, keepdims=True)
        full = jnp.concatenate([prev8, cur], axis=0)
        for k in range(CONV_K - 1):
            sft = CONV_K - 1 - k
            st_ref[k:k + 1, :] += jnp.sum(dpre * pltpu.roll(full, sft, 0)[8:8 + tm, :], axis=0, keepdims=True)

    return pl.pallas_call(
        body, name=name, grid=(CONV_DIM // cwid, t // tm),
        in_specs=[pl.BlockSpec((tm, cwid), lambda j, i: (i, cb0 + j)),
                  pl.BlockSpec((8, cwid), lambda j, i: (jnp.maximum(i * (tm // 8) - 1, 0), cb0 + j)),
                  pl.BlockSpec((tm, cwid), lambda j, i: (i, j)),
                  pl.BlockSpec((CONV_K, cwid), lambda j, i: (0, j)), pl.BlockSpec((1, cwid), lambda j, i: (0, j))],
        out_specs=[pl.BlockSpec((tm, cwid), lambda j, i: (i, j)), pl.BlockSpec((8, cwid), lambda j, i: (0, j))],
        out_shape=[jax.ShapeDtypeStruct((t, CONV_DIM), F32), jax.ShapeDtypeStruct((8, CONV_DIM), F32)],
        compiler_params=_cp("parallel", "arbitrary"),
    )(proj, proj, dxc, cw_full, cb_full)


def conv_bwd_in(dpre, cw_full, name):
    t = dpre.shape[0]
    tm = min(256, t)
    cwid = 1024
    nt = t // tm

    def body(d_ref, n_ref, w_ref, o_ref):
        i = pl.program_id(1)
        cur = d_ref[...]
        nxt = jnp.where(i == nt - 1, 0.0, n_ref[...])
        full = jnp.concatenate([cur, nxt], axis=0)
        w = w_ref[...]
        acc = w[3:4, :] * cur
        for k in range(CONV_K - 1):
            s = CONV_K - 1 - k
            acc = acc + w[k:k + 1, :] * pltpu.roll(full, tm + 8 - s, 0)[0:tm, :]
        o_ref[...] = acc.astype(BF16)

    return pl.pallas_call(
        body, name=name, grid=(CONV_DIM // cwid, nt),
        in_specs=[pl.BlockSpec((tm, cwid), lambda j, i: (i, j)),
                  pl.BlockSpec((8, cwid), lambda j, i: (jnp.minimum((i + 1) * (tm // 8), t // 8 - 1), j)),
                  pl.BlockSpec((CONV_K, cwid), lambda j, i: (0, j))],
        out_specs=pl.BlockSpec((tm, cwid), lambda j, i: (i, j)),
        out_shape=jax.ShapeDtypeStruct((t, CONV_DIM), BF16), compiler_params=_cp("parallel", "parallel"),
    )(dpre, dpre, cw_full)


def _ssd_common(dt_ref, dtb_ref, al_ref, exp_ref, tri_ref):
    lane = lax.broadcasted_iota(jnp.int32, (1, 128), 1)
    a_row = jnp.where(lane < NH, -jnp.exp(al_ref[...]), 0.0)
    zraw = dt_ref[...] + dtb_ref[...]
    dtv = _softplus(zraw)
    cs = jnp.dot(tri_ref[...], dtv * a_row, precision=HI, preferred_element_type=F32)
    e = exp_ref[...]
    csx = jnp.dot(cs, e, precision=HI, preferred_element_type=F32)
    dtx = jnp.dot(dtv, e, precision=HI, preferred_element_type=F32)
    return a_row, zraw, dtv, cs, csx, dtx


def _nt(a, b):
    return lax.dot_general(a, b, (((1,), (1,)), ((), ())), preferred_element_type=F32)


def _dot(a, b):
    return jnp.dot(a, b, preferred_element_type=F32)


def ssd_fwd(xc, proj, dtb_row, alog_row, dx_row, expm, tri, name):
    t = xc.shape[0]
    nc = t // LCH

    def body(xs_ref, bm_ref, cm_ref, dt_ref, dtb_ref, al_ref, dxr_ref, exp_ref, tri_ref, y_ref, hs_ref, h_scr):
        @pl.when(pl.program_id(0) == 0)
        def _():
            h_scr[...] = jnp.zeros_like(h_scr)

        _, _, _, cs, csx, dtx = _ssd_common(dt_ref, dtb_ref, al_ref, exp_ref, tri_ref)
        cst = cs.T
        csl = csx[LCH - 1:LCH, :]
        xs = xs_ref[...]
        xd = xs * dtx
        xdw = xd * jnp.exp(csl - csx)
        ecs = jnp.exp(csx)
        ecl = jnp.exp(csl)
        tril = lax.broadcasted_iota(jnp.int32, (LCH, LCH), 0) >= lax.broadcasted_iota(jnp.int32, (LCH, LCH), 1)
        hs_ref[...] = h_scr[...]
        for g in range(NG):
            gc = slice(g * 512, (g + 1) * 512)
            bm = bm_ref[:, g * NST:(g + 1) * NST]
            cmb = cm_ref[:, g * NST:(g + 1) * NST].astype(BF16)
            gm = _nt(cmb, bm.astype(BF16))
            hg = h_scr[:, gc]
            yo = _dot(cmb, hg.astype(BF16)) * ecs[:, gc]
            st = _dot(bm.T.astype(BF16), xdw[:, gc].astype(BF16))
            for r in range(8):
                h = g * 8 + r
                hc = slice(h * HP, (h + 1) * HP)
                seg = cs[:, h:h + 1] - cst[h:h + 1, :]
                m = (gm * jnp.exp(jnp.where(tril, seg, NEG))).astype(BF16)
                yd = _dot(m, xd[:, hc].astype(BF16))
                y_ref[:, hc] = yd + yo[:, r * HP:(r + 1) * HP] + dxr_ref[:, hc] * xs[:, hc]
            h_scr[:, gc] = ecl[:, gc] * hg + st

    return pl.pallas_call(
        body, name=name, grid=(nc,),
        in_specs=[pl.BlockSpec((LCH, 2048), lambda c: (c, 0)), pl.BlockSpec((LCH, 512), lambda c: (c, 4)),
                  pl.BlockSpec((LCH, 512), lambda c: (c, 5)), pl.BlockSpec((LCH, 128), lambda c: (c, P_DT // 128)),
                  _vec(128), _vec(128), _vec(2048), pl.BlockSpec((128, 2048), lambda c: (0, 0)),
                  pl.BlockSpec((LCH, LCH), lambda c: (0, 0))],
        out_specs=[pl.BlockSpec((LCH, 2048), lambda c: (c, 0)), pl.BlockSpec((None, NST, 2048), lambda c: (c, 0, 0))],
        out_shape=[jax.ShapeDtypeStruct((t, 2048), F32), jax.ShapeDtypeStruct((nc, NST, 2048), F32)],
        scratch_shapes=[pltpu.VMEM((NST, 2048), F32)],
        compiler_params=_cp("arbitrary"),
    )(xc, xc, xc, proj, dtb_row, alog_row, dx_row, expm, tri)


def ssd_bwd(xc, proj, hsave, dy, dtb_row, alog_row, dx_row, expm, tri, name):
    t = xc.shape[0]
    nc = t // LCH

    def body(xs_ref, bm_ref, cm_ref, dt_ref, hs_ref, dy_ref, dtb_ref, al_ref, dxr_ref, exp_ref, tri_ref,
             dxc_ref, ddt_ref, st_ref, dh_scr, dxd_scr, dcsx_scr):
        @pl.when(pl.program_id(0) == 0)
        def _():
            dh_scr[...] = jnp.zeros_like(dh_scr)
            st_ref[...] = jnp.zeros_like(st_ref)

        a_row, zraw, dtv, cs, csx, dtx = _ssd_common(dt_ref, dtb_ref, al_ref, exp_ref, tri_ref)
        e = exp_ref[...]
        cst = cs.T
        csl = csx[LCH - 1:LCH, :]
        xs = xs_ref[...]
        xd = xs * dtx
        wend = jnp.exp(csl - csx)
        xdw = xd * wend
        ecs = jnp.exp(csx)
        ecl = jnp.exp(csl)
        ri = lax.broadcasted_iota(jnp.int32, (LCH, LCH), 0)
        ci = lax.broadcasted_iota(jnp.int32, (LCH, LCH), 1)
        tril = ri >= ci
        triu = ri <= ci
        lane = lax.broadcasted_iota(jnp.int32, (1, 128), 1)
        dyv = dy_ref[...]
        dxr = dxr_ref[...]
        st_ref[2:3, :] += lax.dot_general(jnp.sum(dyv * xs, axis=0, keepdims=True), e, (((1,), (1,)), ((), ())),
                                          precision=HI, preferred_element_type=F32)
        dcs = jnp.zeros((LCH, 128), F32)
        for g in range(NG):
            gc = slice(g * 512, (g + 1) * 512)
            bmb = bm_ref[:, g * NST:(g + 1) * NST].astype(BF16)
            cm = cm_ref[:, g * NST:(g + 1) * NST]
            cmb = cm.astype(BF16)
            hg = hs_ref[:, gc]
            hgb = hg.astype(BF16)
            dhc = dh_scr[:, gc]
            dhcb = dhc.astype(BF16)
            dyg = dyv[:, gc]
            yo = _dot(cmb, hgb) * ecs[:, gc]
            dq = (dyg * ecs[:, gc]).astype(BF16)
            dcm = _nt(dq, hgb)
            dh_yo = _dot(cm.T.astype(BF16), dq)
            dxdw = _dot(bmb, dhcb)
            dbm = _nt(xdw[:, gc].astype(BF16), dhcb)
            tt = dxdw * xdw[:, gc]
            dcsx_g = dyg * yo - tt
            dcsl_g = jnp.sum(tt, axis=0, keepdims=True) + jnp.sum(dhc * hg, axis=0, keepdims=True) * ecl[:, gc]
            dxd_scr[:, gc] = dxdw * wend[:, gc]
            dh_scr[:, gc] = ecl[:, gc] * dhc + dh_yo
            gm = _nt(cmb, bmb)
            gmt = _nt(bmb, cmb)
            dg = jnp.zeros((LCH, LCH), F32)
            dgt = jnp.zeros((LCH, LCH), F32)
            for r in range(8):
                h = g * 8 + r
                hc = slice(h * HP, (h + 1) * HP)
                seg = cs[:, h:h + 1] - cst[h:h + 1, :]
                lm = jnp.exp(jnp.where(tril, seg, NEG))
                lmt = jnp.exp(jnp.where(triu, -seg, NEG))
                mm_ = gm * lm
                mmt = gmt * lmt
                xdh = xd[:, hc].astype(BF16)
                dyh = dyv[:, hc].astype(BF16)
                dm = _nt(dyh, xdh)
                dmt = _nt(xdh, dyh)
                dxd_scr[:, hc] += _dot(mmt.astype(BF16), dyh)
                rs = jnp.sum(dm * mm_, axis=1, keepdims=True) - jnp.sum(dmt * mmt, axis=1, keepdims=True)
                dcs = dcs + rs * jnp.where(lane == h, 1.0, 0.0)
                dg = dg + dm * lm
                dgt = dgt + dmt * lmt
            dcm = dcm + _dot(dg.astype(BF16), bmb)
            dbm = dbm + _dot(dgt.astype(BF16), cmb)
            dxc_ref[:, 2048 + g * NST:2048 + (g + 1) * NST] = dbm
            dxc_ref[:, 2560 + g * NST:2560 + (g + 1) * NST] = dcm
            dcsx_scr[:, gc] = dcsx_g
            dcsx_scr[LCH - 1:LCH, gc] += dcsl_g
        dxd = dxd_scr[...]
        dxc_ref[:, 0:2048] = dxr * dyv + dxd * dtx
        ddtv = lax.dot_general(dxd * xs, e, (((1,), (1,)), ((), ())), precision=HI, preferred_element_type=F32)
        dcs = dcs + lax.dot_general(dcsx_scr[...], e, (((1,), (1,)), ((), ())), precision=HI,
                                    preferred_element_type=F32)
        dda = lax.dot_general(tri_ref[...], dcs, (((0,), (0,)), ((), ())), precision=HI, preferred_element_type=F32)
        ddtv = ddtv + dda * a_row
        st_ref[0:1, :] += jnp.sum(dda * dtv, axis=0, keepdims=True) * a_row
        ddt = ddtv * _sigmoid(zraw)
        ddt_ref[...] = ddt
        st_ref[1:2, :] += jnp.sum(ddt, axis=0, keepdims=True)

    rc = lambda c: nc - 1 - c
    return pl.pallas_call(
        body, name=name, grid=(nc,),
        in_specs=[pl.BlockSpec((LCH, 2048), lambda c: (rc(c), 0)), pl.BlockSpec((LCH, 512), lambda c: (rc(c), 4)),
                  pl.BlockSpec((LCH, 512), lambda c: (rc(c), 5)),
                  pl.BlockSpec((LCH, 128), lambda c: (rc(c), P_DT // 128)),
                  pl.BlockSpec((None, NST, 2048), lambda c: (rc(c), 0, 0)),
                  pl.BlockSpec((LCH, 2048), lambda c: (rc(c), 0)),
                  _vec(128), _vec(128), _vec(2048), pl.BlockSpec((128, 2048), lambda c: (0, 0)),
                  pl.BlockSpec((LCH, LCH), lambda c: (0, 0))],
        out_specs=[pl.BlockSpec((LCH, CONV_DIM), lambda c: (rc(c), 0)), pl.BlockSpec((LCH, 128), lambda c: (rc(c), 0)),
                   _stats(128)],
        out_shape=[jax.ShapeDtypeStruct((t, CONV_DIM), F32), jax.ShapeDtypeStruct((t, 128), F32),
                   jax.ShapeDtypeStruct((8, 128), F32)],
        scratch_shapes=[pltpu.VMEM((NST, 2048), F32), pltpu.VMEM((LCH, 2048), F32), pltpu.VMEM((LCH, 2048), F32)],
        compiler_params=_cp("arbitrary"),
    )(xc, xc, xc, proj, hsave, dy, dtb_row, alog_row, dx_row, expm, tri)


def ssd_out_fwd(y, proj, nw, name):
    t = y.shape[0]
    tm = min(256, t)

    def body(y_ref, z_ref, nw_ref, o_ref):
        for g in range(NG):
            gc = slice(g * 512, (g + 1) * 512)
            z = z_ref[:, gc]
            yz = y_ref[:, gc] * (z * _sigmoid(z))
            r = lax.rsqrt(jnp.mean(yz * yz, axis=-1, keepdims=True) + EPS)
            o_ref[:, gc] = (yz * r * nw_ref[:, gc]).astype(BF16)

    return pl.pallas_call(body, name=name, grid=(t // tm,),
                          in_specs=[_row(tm, 2048), _row(tm, 2048, P_Z // 2048), _vec(2048)],
                          out_specs=_row(tm, 2048), out_shape=jax.ShapeDtypeStruct((t, 2048), BF16),
                          compiler_params=_cp("parallel"))(y, proj, nw)


def ssd_out_bwd(y, proj, dya, nw, name):
    t = y.shape[0]
    tm = min(256, t)

    def body(y_ref, z_ref, d_ref, nw_ref, dy_ref, dz_ref, st_ref):
        @pl.when(pl.program_id(0) == 0)
        def _():
            st_ref[...] = jnp.zeros_like(st_ref)

        for g in range(NG):
            gc = slice(g * 512, (g + 1) * 512)
            z = z_ref[:, gc]
            yv = y_ref[:, gc]
            s = _sigmoid(z)
            sz = z * s
            yz = yv * sz
            r = lax.rsqrt(jnp.mean(yz * yz, axis=-1, keepdims=True) + EPS)
            yzn = yz * r
            dv = d_ref[:, gc]
            st_ref[0:1, gc] += jnp.sum(dv * yzn, axis=0, keepdims=True)
            dyn = dv * nw_ref[:, gc]
            dyz = r * (dyn - yzn * jnp.mean(dyn * yzn, axis=-1, keepdims=True))
            dy_ref[:, gc] = dyz * sz
            dz_ref[:, gc] = (dyz * yv * (s * (1.0 + z * (1.0 - s)))).astype(BF16)

    return pl.pallas_call(
        body, name=name, grid=(t // tm,),
        in_specs=[_row(tm, 2048), _row(tm, 2048, P_Z // 2048), _row(tm, 2048), _vec(2048)],
        out_specs=[_row(tm, 2048), _row(tm, 2048), _stats(2048)],
        out_shape=[jax.ShapeDtypeStruct((t, 2048), F32), jax.ShapeDtypeStruct((t, 2048), BF16),
                   jax.ShapeDtypeStruct((8, 2048), F32)],
        compiler_params=_cp("arbitrary"))(y, proj, dya, nw)


def s5_in(u, bsg, name):
    t = u.shape[0]
    tm = min(512, t)

    def body(u_ref, b_ref, o_ref):
        o_ref[...] = _dot(u_ref[...].astype(BF16), b_ref[...])

    return pl.pallas_call(
        body, name=name, grid=(8, t // tm),
        in_specs=[pl.BlockSpec((tm, 128), lambda s, i: (i, s)), pl.BlockSpec((None, 128, 1024), lambda s, i: (s, 0, 0))],
        out_specs=pl.BlockSpec((tm, 1024), lambda s, i: (i, s)),
        out_shape=jax.ShapeDtypeStruct((t, S5NS), F32), compiler_params=_cp("parallel", "parallel"))(u, bsg)


def s5_out(s, csg, u, d_row, name):
    t = u.shape[0]
    tm = min(512, t)

    def body(s_ref, c_ref, u_ref, d_ref, o_ref):
        o_ref[...] = _dot(s_ref[...].astype(BF16), c_ref[...]) + d_ref[...] * u_ref[...]

    return pl.pallas_call(
        body, name=name, grid=(8, t // tm),
        in_specs=[pl.BlockSpec((tm, 1024), lambda s, i: (i, s)), pl.BlockSpec((None, 1024, 128), lambda s, i: (s, 0, 0)),
                  pl.BlockSpec((tm, 128), lambda s, i: (i, s)), pl.BlockSpec((1, 128), lambda s, i: (0, s))],
        out_specs=pl.BlockSpec((tm, 128), lambda s, i: (i, s)),
        out_shape=jax.ShapeDtypeStruct((t, S5W), F32), compiler_params=_cp("parallel", "parallel"))(s, csg, u, d_row)


def s5_out_bwd(dy, csg, s, name):
    t = dy.shape[0]
    tm = min(512, t)

    def body(dy_ref, c_ref, s_ref, e_ref, dc_ref):
        @pl.when(pl.program_id(1) == 0)
        def _():
            dc_ref[...] = jnp.zeros_like(dc_ref)

        dyb = dy_ref[...].astype(BF16)
        e_ref[...] = _nt(dyb, c_ref[...])
        dc_ref[...] += lax.dot_general(s_ref[...].astype(BF16), dyb, (((0,), (0,)), ((), ())),
                                       preferred_element_type=F32)

    return pl.pallas_call(
        body, name=name, grid=(8, t // tm),
        in_specs=[pl.BlockSpec((tm, 128), lambda s, i: (i, s)), pl.BlockSpec((None, 1024, 128), lambda s, i: (s, 0, 0)),
                  pl.BlockSpec((tm, 1024), lambda s, i: (i, s))],
        out_specs=[pl.BlockSpec((tm, 1024), lambda s, i: (i, s)), pl.BlockSpec((None, 1024, 128), lambda s, i: (s, 0, 0))],
        out_shape=[jax.ShapeDtypeStruct((t, S5NS), F32), jax.ShapeDtypeStruct((8, 1024, 128), F32)],
        compiler_params=_cp("parallel", "arbitrary"))(dy, csg, s)


def s5_in_bwd(lam, bsg, u, dy, d_row, name):
    t = u.shape[0]
    tm = min(512, t)

    def body(l_ref, b_ref, u_ref, dy_ref, d_ref, du_ref, db_ref, dd_ref):
        @pl.when(pl.program_id(1) == 0)
        def _():
            db_ref[...] = jnp.zeros_like(db_ref)
            dd_ref[...] = jnp.zeros_like(dd_ref)

        lb = l_ref[...].astype(BF16)
        uv = u_ref[...]
        dyv = dy_ref[...]
        du_ref[...] = _nt(lb, b_ref[...]) + d_ref[...] * dyv
        db_ref[...] += lax.dot_general(uv.astype(BF16), lb, (((0,), (0,)), ((), ())), preferred_element_type=F32)
        dd_ref[...] += jnp.sum(dyv * uv, axis=0, keepdims=True)

    return pl.pallas_call(
        body, name=name, grid=(8, t // tm),
        in_specs=[pl.BlockSpec((tm, 1024), lambda s, i: (i, s)), pl.BlockSpec((None, 128, 1024), lambda s, i: (s, 0, 0)),
                  pl.BlockSpec((tm, 128), lambda s, i: (i, s)), pl.BlockSpec((tm, 128), lambda s, i: (i, s)),
                  pl.BlockSpec((1, 128), lambda s, i: (0, s))],
        out_specs=[pl.BlockSpec((tm, 128), lambda s, i: (i, s)), pl.BlockSpec((None, 128, 1024), lambda s, i: (s, 0, 0)),
                   pl.BlockSpec((1, 128), lambda s, i: (0, s))],
        out_shape=[jax.ShapeDtypeStruct((t, S5W), F32), jax.ShapeDtypeStruct((8, 128, 1024), F32),
                   jax.ShapeDtypeStruct((1, S5W), F32)],
        compiler_params=_cp("parallel", "arbitrary"))(lam, bsg, u, dy, d_row)


def _cstep(ar, ai, sr, si, br, bi):
    return ar * sr - ai * si + br, ar * si + ai * sr + bi


def s5_scan_ends(b3, a_re, a_im, reverse, name):
    lseg = b3.shape[0]
    ti = min(128, lseg)
    nb = lseg // ti
    sgn = -1.0 if reverse else 1.0

    def body(b_ref, ar_ref, ai_ref, o_ref, sr_scr, si_scr):
        tb = pl.program_id(1)

        @pl.when(tb == 0)
        def _():
            sr_scr[...] = jnp.zeros_like(sr_scr)
            si_scr[...] = jnp.zeros_like(si_scr)

        ar = ar_ref[...]
        ai = sgn * ai_ref[...]

        def step(k, carry):
            tt = ti - 1 - k if reverse else k
            return _cstep(ar, ai, carry[0], carry[1], b_ref[tt, :, 0:512], b_ref[tt, :, 512:1024])

        sr, si = lax.fori_loop(0, ti, step, (sr_scr[...], si_scr[...]), unroll=8)
        sr_scr[...] = sr
        si_scr[...] = si

        @pl.when(tb == nb - 1)
        def _():
            o_ref[:, 0:512] = sr
            o_ref[:, 512:1024] = si

    tmap = (lambda s, tb: (nb - 1 - tb, 0, s)) if reverse else (lambda s, tb: (tb, 0, s))
    return pl.pallas_call(
        body, name=name, grid=(8, nb),
        in_specs=[pl.BlockSpec((ti, 8, 1024), tmap), pl.BlockSpec((None, 8, 512), lambda s, tb: (s, 0, 0)),
                  pl.BlockSpec((None, 8, 512), lambda s, tb: (s, 0, 0))],
        out_specs=pl.BlockSpec((None, 8, 1024), lambda s, tb: (s, 0, 0)),
        out_shape=jax.ShapeDtypeStruct((8, 8, 1024), F32),
        scratch_shapes=[pltpu.VMEM((8, 512), F32), pltpu.VMEM((8, 512), F32)],
        compiler_params=_cp("parallel", "arbitrary"))(b3, a_re, a_im)


def s5_scan_init(ends, a_re, a_im, lseg, reverse, name):
    nsq = int(math.log2(lseg))
    assert 2 ** nsq == lseg
    sgn = -1.0 if reverse else 1.0
    order = list(range(7, -1, -1)) if reverse else list(range(8))

    def body(e_ref, ar_ref, ai_ref, o_ref):
        pr = ar_ref[0:1, :]
        pi = sgn * ai_ref[0:1, :]
        for _ in range(nsq):
            pr, pi = pr * pr - pi * pi, 2.0 * pr * pi
        prev_r = jnp.zeros((1, 512), F32)
        prev_i = jnp.zeros((1, 512), F32)
        j0 = order[0]
        o_ref[j0:j0 + 1, 0:512] = prev_r
        o_ref[j0:j0 + 1, 512:1024] = prev_i
        for idx in range(1, 8):
            j, jp = order[idx], order[idx - 1]
            prev_r, prev_i = _cstep(pr, pi, prev_r, prev_i, e_ref[jp:jp + 1, 0:512], e_ref[jp:jp + 1, 512:1024])
            o_ref[j:j + 1, 0:512] = prev_r
            o_ref[j:j + 1, 512:1024] = prev_i

    return pl.pallas_call(
        body, name=name, grid=(8,),
        in_specs=[pl.BlockSpec((None, 8, 1024), lambda s: (s, 0, 0)), pl.BlockSpec((None, 8, 512), lambda s: (s, 0, 0)),
                  pl.BlockSpec((None, 8, 512), lambda s: (s, 0, 0))],
        out_specs=pl.BlockSpec((None, 8, 1024), lambda s: (s, 0, 0)),
        out_shape=jax.ShapeDtypeStruct((8, 8, 1024), F32), compiler_params=_cp("parallel"))(ends, a_re, a_im)


def s5_scan_fwd(b3, init, a_re, a_im, name):
    lseg = b3.shape[0]
    ti = min(128, lseg)
    nb = lseg // ti

    def body(b_ref, i_ref, ar_ref, ai_ref, o_ref, sr_scr, si_scr):
        @pl.when(pl.program_id(1) == 0)
        def _():
            sr_scr[...] = i_ref[:, 0:512]
            si_scr[...] = i_ref[:, 512:1024]

        ar = ar_ref[...]
        ai = ai_ref[...]

        def step(k, carry):
            nr, ni = _cstep(ar, ai, carry[0], carry[1], b_ref[k, :, 0:512], b_ref[k, :, 512:1024])
            o_ref[k, :, 0:512] = nr
            o_ref[k, :, 512:1024] = ni
            return nr, ni

        sr, si = lax.fori_loop(0, ti, step, (sr_scr[...], si_scr[...]), unroll=8)
        sr_scr[...] = sr
        si_scr[...] = si

    return pl.pallas_call(
        body, name=name, grid=(8, nb),
        in_specs=[pl.BlockSpec((ti, 8, 1024), lambda s, tb: (tb, 0, s)), pl.BlockSpec((None, 8, 1024), lambda s, tb: (s, 0, 0)),
                  pl.BlockSpec((None, 8, 512), lambda s, tb: (s, 0, 0)), pl.BlockSpec((None, 8, 512), lambda s, tb: (s, 0, 0))],
        out_specs=pl.BlockSpec((ti, 8, 1024), lambda s, tb: (tb, 0, s)),
        out_shape=jax.ShapeDtypeStruct(b3.shape, F32),
        scratch_shapes=[pltpu.VMEM((8, 512), F32), pltpu.VMEM((8, 512), F32)],
        compiler_params=_cp("parallel", "arbitrary"))(b3, init, a_re, a_im)


def s5_scan_bwd(e3, linit, s3, sinit, a_re, a_im, name):
    lseg = e3.shape[0]
    ti = min(128, lseg)
    nb = lseg // ti

    def body(e_ref, li_ref, s_ref, sh_ref, si0_ref, ar_ref, ai_ref, o_ref, da_ref, lr_scr, lim_scr):
        tb = pl.program_id(1)

        @pl.when(tb == 0)
        def _():
            lr_scr[...] = li_ref[:, 0:512]
            lim_scr[...] = li_ref[:, 512:1024]
            da_ref[...] = jnp.zeros_like(da_ref)

        ar = ar_ref[...]
        ai = -ai_ref[...]

        def one(tt, lr, li, dar, dai, spr, spi):
            nr, ni = _cstep(ar, ai, lr, li, e_ref[tt, :, 0:512], e_ref[tt, :, 512:1024])
            o_ref[tt, :, 0:512] = nr
            o_ref[tt, :, 512:1024] = ni
            return nr, ni, dar + nr * spr + ni * spi, dai + ni * spr - nr * spi

        def step(k, carry):
            tt = ti - 1 - k
            return one(tt, *carry, s_ref[tt - 1, :, 0:512], s_ref[tt - 1, :, 512:1024])

        z = jnp.zeros((8, 512), F32)
        lr, li, dar, dai = lax.fori_loop(0, ti - 1, step, (lr_scr[...], lim_scr[...], z, z), unroll=8)
        first = tb == nb - 1
        spr = jnp.where(first, si0_ref[:, 0:512], sh_ref[0, :, 0:512])
        spi = jnp.where(first, si0_ref[:, 512:1024], sh_ref[0, :, 512:1024])
        lr, li, dar, dai = one(0, lr, li, dar, dai, spr, spi)
        lr_scr[...] = lr
        lim_scr[...] = li
        da_ref[:, 0:512] += dar
        da_ref[:, 512:1024] += dai

    rb = lambda tb: nb - 1 - tb
    return pl.pallas_call(
        body, name=name, grid=(8, nb),
        in_specs=[pl.BlockSpec((ti, 8, 1024), lambda s, tb: (rb(tb), 0, s)),
                  pl.BlockSpec((None, 8, 1024), lambda s, tb: (s, 0, 0)),
                  pl.BlockSpec((ti, 8, 1024), lambda s, tb: (rb(tb), 0, s)),
                  pl.BlockSpec((1, 8, 1024), lambda s, tb: (jnp.maximum(rb(tb) * ti - 1, 0), 0, s)),
                  pl.BlockSpec((None, 8, 1024), lambda s, tb: (s, 0, 0)),
                  pl.BlockSpec((None, 8, 512), lambda s, tb: (s, 0, 0)), pl.BlockSpec((None, 8, 512), lambda s, tb: (s, 0, 0))],
        out_specs=[pl.BlockSpec((ti, 8, 1024), lambda s, tb: (rb(tb), 0, s)),
                   pl.BlockSpec((None, 8, 1024), lambda s, tb: (s, 0, 0))],
        out_shape=[jax.ShapeDtypeStruct(e3.shape, F32), jax.ShapeDtypeStruct((8, 8, 1024), F32)],
        scratch_shapes=[pltpu.VMEM((8, 512), F32), pltpu.VMEM((8, 512), F32)],
        compiler_params=_cp("parallel", "arbitrary"))(e3, linit, s3, s3, sinit, a_re, a_im)


_GC = math.sqrt(2.0 / math.pi)


def gelu_fwd(y, name):
    t, w = y.shape
    tm = min(512, t)

    def body(y_ref, o_ref):
        v = y_ref[...]
        o_ref[...] = (0.5 * v * (1.0 + jnp.tanh(_GC * (v + 0.044715 * v * v * v)))).astype(BF16)

    return pl.pallas_call(body, name=name, grid=(t // tm,), in_specs=[_row(tm, w)], out_specs=_row(tm, w),
                          out_shape=jax.ShapeDtypeStruct((t, w), BF16), compiler_params=_cp("parallel"))(y)


def gelu_bwd(y, dg, name):
    t, w = y.shape
    tm = min(512, t)

    def body(y_ref, d_ref, o_ref):
        v = y_ref[...]
        th = jnp.tanh(_GC * (v + 0.044715 * v * v * v))
        o_ref[...] = d_ref[...] * (0.5 * (1.0 + th) + 0.5 * v * (1.0 - th * th) * _GC * (1.0 + 3.0 * 0.044715 * v * v))

    return pl.pallas_call(body, name=name, grid=(t // tm,), in_specs=[_row(tm, w), _row(tm, w)], out_specs=_row(tm, w),
                          out_shape=jax.ShapeDtypeStruct((t, w), F32), compiler_params=_cp("parallel"))(y, dg)


def merge_fwd(proj, pa, glu, name):
    t = pa.shape[0]
    tm = min(256, t)

    def body(g_ref, pa_ref, glu_ref, o_ref):
        pb = glu_ref[:, 0:D] * _sigmoid(glu_ref[:, D:2 * D])
        o_ref[...] = (_sigmoid(g_ref[:, 0:D]) * pa_ref[...] + _sigmoid(g_ref[:, D:2 * D]) * pb).astype(BF16)

    return pl.pallas_call(body, name=name, grid=(t // tm,), in_specs=[_row(tm, 2 * D), _row(tm, D), _row(tm, 2 * D)],
                          out_specs=_row(tm, D), out_shape=jax.ShapeDtypeStruct((t, D), BF16),
                          compiler_params=_cp("parallel"))(proj, pa, glu)


def merge_bwd(proj, pa, glu, dm, name):
    t = pa.shape[0]
    tm = min(256, t)

    def body(g_ref, pa_ref, glu_ref, dm_ref, dpa_ref, dglu_ref, dg_ref):
        dmv = dm_ref[...]
        pav = pa_ref[...]
        sa = _sigmoid(g_ref[:, 0:D])
        sb = _sigmoid(g_ref[:, D:2 * D])
        ga = glu_ref[:, 0:D]
        sg = _sigmoid(glu_ref[:, D:2 * D])
        pb = ga * sg
        dpb = sb * dmv
        dpa_ref[...] = (sa * dmv).astype(BF16)
        dglu_ref[:, 0:D] = (dpb * sg).astype(BF16)
        dglu_ref[:, D:2 * D] = (dpb * pb * (1.0 - sg)).astype(BF16)
        dg_ref[:, 0:D] = (dmv * pav * sa * (1.0 - sa)).astype(BF16)
        dg_ref[:, D:2 * D] = (dmv * pb * sb * (1.0 - sb)).astype(BF16)

    return pl.pallas_call(
        body, name=name, grid=(t // tm,),
        in_specs=[_row(tm, 2 * D), _row(tm, D), _row(tm, 2 * D), _row(tm, D)],
        out_specs=[_row(tm, D), _row(tm, 2 * D), _row(tm, 2 * D)],
        out_shape=[jax.ShapeDtypeStruct((t, D), BF16), jax.ShapeDtypeStruct((t, 2 * D), BF16),
                   jax.ShapeDtypeStruct((t, 2 * D), BF16)],
        compiler_params=_cp("parallel"))(proj, pa, glu, dm)


def adamw(w, parts, m, v, name):
    r, c = w.shape
    p = parts.shape[0]
    tr = r if r <= 128 else 128
    c1 = 1.0 - ADAM_B1 ** ADAM_STEP
    c2 = 1.0 - ADAM_B2 ** ADAM_STEP

    def body(w_ref, p_ref, m_ref, v_ref, g_ref, d_ref, nm_ref, nv_ref):
        g = p_ref[0].astype(F32)
        for k in range(1, p):
            g = g + p_ref[k].astype(F32)
        mn = ADAM_B1 * m_ref[...] + (1.0 - ADAM_B1) * g
        vn = ADAM_B2 * v_ref[...] + (1.0 - ADAM_B2) * (g * g)
        g_ref[...] = g
        nm_ref[...] = mn
        nv_ref[...] = vn
        d_ref[...] = -ADAM_LR * ((mn / c1) / (jnp.sqrt(vn / c2) + ADAM_EPS) + ADAM_WD * w_ref[...])

    spec = pl.BlockSpec((tr, c), lambda i: (i, 0))
    o = jax.ShapeDtypeStruct((r, c), F32)
    return pl.pallas_call(
        body, name=name, grid=(pl.cdiv(r, tr),),
        in_specs=[spec, pl.BlockSpec((p, tr, c), lambda i: (0, i, 0)), spec, spec],
        out_specs=[spec, spec, spec, spec], out_shape=[o, o, o, o], compiler_params=_cp("parallel"))(w, parts, m, v)


def _s5_discretise(lambda_re, lambda_im, log_dt, b_re, b_im):
    dt = jnp.exp(log_dt)[:, None]
    lr = jnp.minimum(lambda_re, -1e-4)
    li = lambda_im
    mag = jnp.exp(lr * dt)
    ar = mag * jnp.cos(li * dt)
    ai = mag * jnp.sin(li * dt)
    den = lr * lr + li * li
    nr = ar - 1.0
    kr = (nr * lr + ai * li) / den
    ki = (ai * lr - nr * li) / den
    bbar_re = kr[..., None] * b_re - ki[..., None] * b_im
    bbar_im = kr[..., None] * b_im + ki[..., None] * b_re
    return ar, ai, bbar_re, bbar_im


def _bsg_of(bb_re, bb_im):
    eye = jnp.eye(8, dtype=F32)
    f = lambda b: jnp.einsum("sgpi,gh->sgihp", b.reshape(8, 8, 64, 16), eye).reshape(8, 128, 512)
    return jnp.concatenate([f(bb_re), f(bb_im)], axis=2)


def _bsg_diag(dbsg):
    eye = jnp.eye(8, dtype=F32)
    f = lambda x: jnp.einsum("sgihp,gh->sgpi", x.reshape(8, 8, 16, 8, 64), eye).reshape(64, 64, 16)
    return f(dbsg[:, :, 0:512]), f(dbsg[:, :, 512:1024])


def _csg_of(c_re, c_im):
    eye = jnp.eye(8, dtype=F32)
    f = lambda c: jnp.einsum("sgip,gh->sgphi", c.reshape(8, 8, 16, 64), eye).reshape(8, 512, 128)
    return jnp.concatenate([f(c_re), -f(c_im)], axis=1)


def _csg_diag(dcsg):
    eye = jnp.eye(8, dtype=F32)
    f = lambda x: jnp.einsum("sgphi,gh->sgip", x.reshape(8, 8, 64, 8, 16), eye).reshape(64, 16, 64)
    return f(dcsg[:, 0:512, :]), -f(dcsg[:, 512:1024, :])


def _perm(a, t):
    return a.reshape(8, t // 8, a.shape[1]).transpose(1, 0, 2).reshape(t, a.shape[1])


def _unperm(a, t):
    return a.reshape(t // 8, 8, a.shape[1]).transpose(1, 0, 2).reshape(t, a.shape[1])


def _cols(g):
    return g.transpose(1, 0, 2).reshape(g.shape[1], N_DEV * g.shape[2])


def _rows(g):
    return g.reshape(N_DEV * g.shape[1], g.shape[2])


def _col_parts(g):
    r, c = g.shape
    return g.reshape(r, N_DEV, c // N_DEV).transpose(1, 0, 2)


def _row_parts(g):
    r, c = g.shape
    return g.reshape(N_DEV, r // N_DEV, c)


def _pad_ffn_in(w):
    z = jnp.zeros((D, D_FFP - D_FF), w.dtype)
    return jnp.concatenate([w[:, :D_FF], z, w[:, D_FF:], z], axis=1)


def _unpad_ffn_in(g):
    return jnp.concatenate([g[:, :D_FF], g[:, D_FFP:D_FFP + D_FF]], axis=1)


def _pad_w_in(w):
    z = jnp.zeros((D, INP - P_DT - NH), w.dtype)
    return jnp.concatenate([w[:, O_GA:O_GB], w[:, O_GB:IN_COLS], w[:, 0:O_XBC], w[:, O_XBC:O_DT], w[:, O_U:O_GA],
                            w[:, O_DT:O_U], z], axis=1)


def _unpad_w_in(g):
    return jnp.concatenate([g[:, P_Z:P_XBC], g[:, P_XBC:P_U], g[:, P_DT:P_DT + NH], g[:, P_U:P_DT],
                            g[:, 0:D], g[:, D:2 * D]], axis=1)


_PACK = (("b_ada", 18432), ("norm_ffn1", 2048), ("norm_mix", 2048), ("conv_b", 3072), ("dt_bias", 32), ("a_log", 32),
         ("d_ssd", 32), ("ssd_norm_w", 2048), ("s5_lambda_re", 4096), ("s5_lambda_im", 4096), ("s5_b_re", 65536),
         ("s5_b_im", 65536), ("s5_c_re", 65536), ("s5_c_im", 65536), ("s5_d", 1024), ("s5_log_dt", 64),
         ("norm_ffn2", 2048), ("norm_final", 2048), ("loss", 1))
_PACK_ROWS = 304
_PACK_W = 1024


def _pack(d):
    flat = jnp.concatenate([d[k].reshape(-1).astype(F32) for k, _ in _PACK])
    return jnp.pad(flat, (0, _PACK_ROWS * _PACK_W - flat.shape[0])).reshape(_PACK_ROWS, _PACK_W)


def _unpack(a):
    flat = a.reshape(-1)
    out, off = {}, 0
    for k, n in _PACK:
        out[k] = flat[off:off + n]
        off += n
    return out


_TA = dict(tm=512, tn=512, tk=8192)


def _ffn_bwd(df, h, ab, act, w_in_t, w_out_t, tag):
    dab = ffn_dab(df, w_out_t, ab, tag + "_dab")
    dw_out = mm(act, df, ta=True, out_dtype=BF16, i_outer=True, name=tag + "_dwout", **_TA)
    dw_in, (x_out,) = mm(h, dab, ta=True, b_halves=True, out_dtype=BF16, i_outer=True, name=tag + "_dwin",
                         comm=[("xc", _row_parts(dw_out[:D_FF]))], **_TA)
    dh, (x_in,) = mm(dab, w_in_t, a_halves=True, tk=5632, name=tag + "_dh",
                     comm=[("xc", _col_parts(_unpad_ffn_in(dw_in)))])
    return dh, x_in, x_out


def kernel(x, c, w_ada, b_ada, norm_ffn1, w_ffn1_in, w_ffn1_out, norm_mix, w_in, conv_w, conv_b, dt_bias, a_log, d_ssd, ssd_norm_w, w_a_proj, s5_lambda_re, s5_lambda_im, s5_b_re, s5_b_im, s5_c_re, s5_c_im, s5_d, s5_log_dt, w_b_glu, w_out, norm_ffn2, w_ffn2_in, w_ffn2_out, norm_final, loss_target, m_w_ada, m_b_ada, m_norm_ffn1, m_w_ffn1_in, m_w_ffn1_out, m_norm_mix, m_w_in, m_conv_w, m_conv_b, m_dt_bias, m_a_log, m_d_ssd, m_ssd_norm_w, m_w_a_proj, m_s5_lambda_re, m_s5_lambda_im, m_s5_b_re, m_s5_b_im, m_s5_c_re, m_s5_c_im, m_s5_d, m_s5_log_dt, m_w_b_glu, m_w_out, m_norm_ffn2, m_w_ffn2_in, m_w_ffn2_out, m_norm_final, v_w_ada, v_b_ada, v_norm_ffn1, v_w_ffn1_in, v_w_ffn1_out, v_norm_mix, v_w_in, v_conv_w, v_conv_b, v_dt_bias, v_a_log, v_d_ssd, v_ssd_norm_w, v_w_a_proj, v_s5_lambda_re, v_s5_lambda_im, v_s5_b_re, v_s5_b_im, v_s5_c_re, v_s5_c_im, v_s5_d, v_s5_log_dt, v_w_b_glu, v_w_out, v_norm_ffn2, v_w_ffn2_in, v_w_ffn2_out, v_norm_final):
    args = dict(locals())
    t = x.shape[1]
    me = _my_id()
    xt = x[0]
    tgt = loss_target[0]
    small = {k: args[k] for k, _ in _PACK if k != "loss"}

    bf = lambda w: w[0].astype(BF16)
    pad_rows = lambda w: jnp.pad(w, ((0, D_FFP - D_FF), (0, 0)))

    c8 = all_gather(c, "ag_c").reshape(N_DEV, D)
    b_loc = lax.dynamic_slice(b_ada, (0, me * (N_ADA * D // N_DEV)), (1, N_ADA * D // N_DEV))
    m8 = ada_fwd(c8, w_ada[0], b_loc, "ada_fwd")
    mods, g_f1i, g_cw = comm_call([("xc", m8.reshape(N_DEV, 1, -1)), ("ag", bf(w_ffn1_in)), ("ag", conv_w[0])],
                                  "xc_mods_ag_ffn1_in")
    mods = mods.reshape(1, N_ADA * D)
    convw = _cols(g_cw)
    wf1i = _pad_ffn_in(_cols(g_f1i))

    h1 = mod_fwd(xt, norm_ffn1, mods, 0, 1, name="mod1")
    ab1, act1, (g_f1o, g_wap, g_wo) = ffn_in_act(
        h1, wf1i, "ffn1_in", comm=[("ag", bf(w_ffn1_out)), ("ag", bf(w_a_proj)), ("ag", bf(w_out))])
    wf1o, wap, wo = pad_rows(_rows(g_f1o)), _rows(g_wap), _rows(g_wo)
    f1, (g_win,) = mm(act1, wf1o, tk=5632, name="ffn1_out", comm=[("ag", bf(w_in))])
    winp = _pad_w_in(_cols(g_win))
    x1, h2 = mod_fwd(xt, norm_mix, mods, 3, 4, f=f1, gk=2, gscale=0.5, name="mod2")
    proj, (g_f2i, g_wbg) = mm(h2, winp, tm=1024, tn=512, i_outer=True, name="w_in",
                              comm=[("ag", bf(w_ffn2_in)), ("ag", bf(w_b_glu))])
    wf2i, wbg = _pad_ffn_in(_cols(g_f2i)), _cols(g_wbg)
    cb_row = conv_b
    xc = conv_fwd(proj, convw, cb_row, "conv_fwd")
    row128 = lambda v: jnp.pad(v.reshape(1, -1), ((0, 0), (0, 128 - v.size)))
    dtb_row, alog_row = row128(dt_bias), row128(a_log)
    dx_row = jnp.repeat(d_ssd.reshape(-1), HP).reshape(1, 2048)
    expm = (jnp.arange(128)[:, None] == (jnp.arange(2048)[None, :] // HP)).astype(F32)
    tri = (jnp.arange(LCH)[:, None] >= jnp.arange(LCH)[None, :]).astype(F32)
    y_ssd, hsave = ssd_fwd(xc, proj, dtb_row, alog_row, dx_row, expm, tri, "ssd_fwd")
    ya = ssd_out_fwd(y_ssd, proj, ssd_norm_w, "ssd_out")
    pa = mm(ya, wap, name="w_a_proj")

    s5p = (s5_lambda_re[0], s5_lambda_im[0], s5_log_dt[0], s5_b_re[0], s5_b_im[0])
    (ar, ai, bb_re, bb_im), s5_vjp = jax.vjp(_s5_discretise, *s5p)
    a_re8 = jnp.broadcast_to(ar.reshape(8, 1, 512), (8, 8, 512))
    a_im8 = jnp.broadcast_to(ai.reshape(8, 1, 512), (8, 8, 512))
    bsg = _bsg_of(bb_re, bb_im).astype(BF16)
    csg = _csg_of(s5_c_re[0], s5_c_im[0]).astype(BF16)
    d_row = s5_d.reshape(1, S5W)
    lseg = t // 8
    u_p = _perm(proj[:, P_U:P_U + S5W], t)
    bu3 = s5_in(u_p, bsg, "s5_in").reshape(lseg, 8, S5NS)
    sinit = s5_scan_init(s5_scan_ends(bu3, a_re8, a_im8, False, "s5_ends_f"), a_re8, a_im8, lseg, False, "s5_init_f")
    s3 = s5_scan_fwd(bu3, sinit, a_re8, a_im8, "s5_scan_f")
    s2 = s3.reshape(t, S5NS)
    yb_p = s5_out(s2, csg, u_p, d_row, "s5_out")
    yb = _unperm(yb_p, t)
    gy = gelu_fwd(yb, "gelu")
    glu = mm(gy, wbg, name="w_b_glu")
    merged = merge_fwd(proj, pa, glu, "merge")
    o = mm(merged, wo, name="w_out")
    x2, h3 = mod_fwd(x1, norm_ffn2, mods, 6, 7, f=o, gk=5, gscale=1.0, name="mod3")
    ab3, act3, (g_f2o,) = ffn_in_act(h3, wf2i, "ffn2_in", comm=[("ag", bf(w_ffn2_out))])
    wf2o = pad_rows(_rows(g_f2o))
    f3 = mm(act3, wf2o, tk=5632, name="ffn2_out")

    dx3, df3, st_fin = final_fwd_bwd(x2, f3, mods, norm_final.reshape(1, D), tgt, "final")
    dh3, x_f2i, x_f2o = _ffn_bwd(df3, h3, ab3, act3, wf2i.T, wf2o.T, "ffn2")
    dx2, do, st3 = mod_bwd(x2, dh3, dx3, norm_ffn2, mods, 7, fprev=o, gk=5, gscale=1.0, name="mod3_bwd")

    dmerged = mm(do, wo.T, name="w_out_dx")
    dwo = mm(merged, do, ta=True, out_dtype=BF16, i_outer=True, name="w_out_dw", **_TA)
    dpa, dglu, dgates = merge_bwd(proj, pa, glu, dmerged, "merge_bwd")
    dwbg = mm(gy, dglu, ta=True, out_dtype=BF16, i_outer=True, name="w_b_glu_dw", **_TA)
    dgy, (x_wo,) = mm(dglu, wbg.T, name="w_b_glu_dx", comm=[("xc", _row_parts(dwo))])
    dyb_p = _perm(gelu_bwd(yb, dgy, "gelu_bwd"), t)
    e2, dcsg = s5_out_bwd(dyb_p, csg, s2, "s5_out_bwd")
    e3 = e2.reshape(lseg, 8, S5NS)
    linit = s5_scan_init(s5_scan_ends(e3, a_re8, a_im8, True, "s5_ends_b"), a_re8, a_im8, lseg, True, "s5_init_b")
    lam3, da8 = s5_scan_bwd(e3, linit, s3, sinit, a_re8, a_im8, "s5_scan_b")
    du_p, dbsg, dd_row = s5_in_bwd(lam3.reshape(t, S5NS), bsg, u_p, dyb_p, d_row, "s5_in_bwd")
    du = _unperm(du_p, t).astype(BF16)
    da = jnp.sum(da8, axis=1)
    dbb_re, dbb_im = _bsg_diag(dbsg)
    g_lre, g_lim, g_ldt, g_bre, g_bim = s5_vjp((da[:, 0:512].reshape(64, 64), da[:, 512:1024].reshape(64, 64),
                                                dbb_re, dbb_im))
    g_cre, g_cim = _csg_diag(dcsg)

    dwap = mm(ya, dpa, ta=True, out_dtype=BF16, i_outer=True, name="w_a_proj_dw", **_TA)
    dya, (x_wbg,) = mm(dpa, wap.T, name="w_a_proj_dx", comm=[("xc", _col_parts(dwbg))])
    dy_ssd, dz, st_sn = ssd_out_bwd(y_ssd, proj, dya, ssd_norm_w, "ssd_out_bwd")
    dxc, ddt, st_ssd = ssd_bwd(xc, proj, hsave, dy_ssd, dtb_row, alog_row, dx_row, expm, tri, "ssd_bwd")
    dpre, st_cv = conv_bwd_pre(proj, dxc, convw, cb_row, "conv_bwd_pre")
    dxbc = conv_bwd_in(dpre, convw, "conv_bwd_in")
    dproj = jnp.concatenate([dgates, dz, dxbc, du, ddt.astype(BF16), jnp.zeros((t, INP - P_DT - 128), BF16)], axis=1)
    dwinp = mm(h2, dproj, ta=True, out_dtype=BF16, i_outer=True, name="w_in_dw", **_TA)
    dh2, (x_win, x_wap, x_cw) = mm(
        dproj, winp.T, tk=5376, name="w_in_dx",
        comm=[("xc", _col_parts(_unpad_w_in(dwinp))), ("xc", _row_parts(dwap)), ("xc", _col_parts(st_cv[0:CONV_K]))])
    dx1, df1, st2 = mod_bwd(x1, dh2, dx2, norm_mix, mods, 4, fprev=f1, gk=2, gscale=0.5, name="mod2_bwd")
    dh1, x_f1i, x_f1o = _ffn_bwd(df1, h1, ab1, act1, wf1i.T, wf1o.T, "ffn1")
    gx, st1 = mod_bwd(xt, dh1, dx1, norm_ffn1, mods, 1, name="mod1_bwd")

    dmods = jnp.concatenate([st1[0], st1[1], st2[3], st2[0], st2[1], st3[3], st3[0], st3[1], st_fin[1]])
    part = {"b_ada": dmods, "norm_ffn1": st1[2], "norm_mix": st2[2], "conv_b": st_cv[4], "dt_bias": st_ssd[1, 0:NH],
            "a_log": st_ssd[0, 0:NH], "d_ssd": st_ssd[2, 0:NH], "ssd_norm_w": st_sn[0], "s5_lambda_re": g_lre,
            "s5_lambda_im": g_lim, "s5_b_re": g_bre, "s5_b_im": g_bim, "s5_c_re": g_cre, "s5_c_im": g_cim,
            "s5_d": dd_row, "s5_log_dt": g_ldt, "norm_ffn2": st3[2], "norm_final": st_fin[0],
            "loss": (0.5 / D) * jnp.sum(st_fin[2])}
    zero = {"loss": jnp.zeros((1,), F32)}
    gath = all_gather(_pack(part), "ag_small")
    sg, sd, sm, sv = adamw(_pack({**small, **zero}), gath, _pack({**{k: args["m_" + k] for k in small}, **zero}),
                           _pack({**{k: args["v_" + k] for k in small}, **zero}), "adamw_small")
    sg, sd, sm, sv = _unpack(sg), _unpack(sd), _unpack(sm), _unpack(sv)
    loss = sg["loss"][0]

    dm_loc = lax.dynamic_slice(gath.reshape(N_DEV, -1)[:, 0:N_ADA * D], (0, me * (N_ADA * D // N_DEV)),
                               (N_DEV, N_ADA * D // N_DEV))
    g_ada = ada_bwd(c8.T, dm_loc, "ada_bwd")
    big = {"w_ada": g_ada[None], "w_ffn1_in": x_f1i, "w_ffn1_out": x_f1o, "w_in": x_win, "conv_w": x_cw,
           "w_a_proj": x_wap, "w_b_glu": x_wbg, "w_out": x_wo, "w_ffn2_in": x_f2i, "w_ffn2_out": x_f2o}
    res = {}
    for k, parts in big.items():
        res[k] = adamw(args[k][0], parts, args["m_" + k][0], args["v_" + k][0], "adamw_" + k)

    names = ["w_ada", "b_ada", "norm_ffn1", "w_ffn1_in", "w_ffn1_out", "norm_mix", "w_in", "conv_w", "conv_b", "dt_bias",
             "a_log", "d_ssd", "ssd_norm_w", "w_a_proj", "s5_lambda_re", "s5_lambda_im", "s5_b_re", "s5_b_im", "s5_c_re",
             "s5_c_im", "s5_d", "s5_log_dt", "w_b_glu", "w_out", "norm_ffn2", "w_ffn2_in", "w_ffn2_out", "norm_final"]
    outs = [loss, gx[None]]
    for q, src in enumerate((sg, sd, sm, sv)):
        for k in names:
            if k in res:
                outs.append(res[k][q][None])
            else:
                outs.append(src[k].reshape(args[k].shape))
    return tuple(outs)
```

```python
import functools
import math

import jax
import jax.numpy as jnp
from jax import lax
from jax.experimental import pallas as pl
from jax.experimental.pallas import tpu as pltpu

F32 = jnp.float32
BF16 = jnp.bfloat16
HI = lax.Precision.HIGHEST

N_DEV = 8
D = 2048
D_FF = 5504
D_FFP = 5632
NH = 32
HP = 64
NG = 4
NST = 128
LCH = 128
CONV_DIM = 3072
CONV_K = 4
S5W = 1024
S5NS = 8192
N_ADA = 9
EPS = 1e-6
IN_COLS = 10272
INP = 10752
P_GATES, P_Z, P_XBC, P_U, P_DT = 0, 4096, 6144, 9216, 10240
O_XBC, O_DT, O_U, O_GA, O_GB = 2048, 5120, 5152, 6176, 8224
NEG = -1e30
VMEM_LIMIT = 56 * 1024 * 1024

ADAM_LR, ADAM_B1, ADAM_B2, ADAM_EPS, ADAM_WD, ADAM_STEP = 0.001, 0.9, 0.999, 1e-08, 0.01, 10


def _cp(*sem):
    return pltpu.CompilerParams(dimension_semantics=sem, vmem_limit_bytes=VMEM_LIMIT)


def _tile(dim, pref):
    if dim <= pref or dim % pref == 0:
        return min(dim, pref)
    for t in (2048, 1024, 512, 256, 128):
        if t <= pref and dim % t == 0:
            return t
    return dim


def _vec(w, cb=0):
    return pl.BlockSpec((1, w), lambda *_: (0, cb))


def _row(tm, w, cb=0):
    return pl.BlockSpec((tm, w), lambda i: (i, cb))


def _stats(w):
    return pl.BlockSpec((8, w), lambda *_: (0, 0))


def _sigmoid(x):
    return 1.0 / (1.0 + jnp.exp(-x))


def _softplus(x):
    return jnp.maximum(x, 0.0) + jnp.log1p(jnp.exp(-jnp.abs(x)))


def _peer(k):
    x, y, c = lax.axis_index("x"), lax.axis_index("y"), lax.axis_index("c")
    return (x ^ ((k >> 2) & 1), y ^ ((k >> 1) & 1), c ^ (k & 1))


def _my_id():
    return 4 * lax.axis_index("x") + 2 * lax.axis_index("y") + lax.axis_index("c")


def _comm_out_shape(kind, v):
    return jax.ShapeDtypeStruct(((N_DEV,) + v.shape) if kind == "ag" else v.shape, v.dtype)


def _comm_scratch(n):
    return [pltpu.SemaphoreType.DMA((n * N_DEV,)), pltpu.SemaphoreType.DMA((n * N_DEV,))]


class _Comm:
    def __init__(self, kinds, srcs, dsts, send_sems, recv_sems):
        self.items = list(zip(kinds, srcs, dsts))
        self.send_sems, self.recv_sems = send_sems, recv_sems
        x, y, c = lax.axis_index("x"), lax.axis_index("y"), lax.axis_index("c")
        self.me = 4 * x + 2 * y + c
        self.sib = (x, y, 1 - c)
        self.chips = [(1 - x, y), (x, 1 - y), (1 - x, 1 - y)]
        self.c = c

    @staticmethod
    def _id(p):
        return 4 * p[0] + 2 * p[1] + p[2]

    def _push(self, q, k, src, slot, to):
        return pltpu.make_async_remote_copy(
            src_ref=src, dst_ref=self.items[q][2].at[slot], send_sem=self.send_sems.at[q * N_DEV + k],
            recv_sem=self.recv_sems.at[q * N_DEV + k], device_id=to, device_id_type=pl.DeviceIdType.MESH)

    def _local(self, q):
        kind, src, dst = self.items[q]
        return pltpu.make_async_copy(src if kind == "ag" else src.at[self.me], dst.at[self.me],
                                     self.send_sems.at[q * N_DEV])

    def _direct(self, q):
        kind, src, dst = self.items[q]
        if kind == "xc":
            out = []
            for k in range(1, N_DEV):
                p = _peer(k)
                out.append((k, self._push(q, k, src.at[self._id(p)], self.me, p)))
            return out
        out = [(1, self._push(q, 1, src, self.me, self.sib))]
        for j, chip in enumerate(self.chips):
            out.append((2 + j, self._push(q, 2 + j, src, self.me, (*chip, self.c))))
        return out

    def _forwards(self, q):
        dst = self.items[q][2]
        out = []
        for j, chip in enumerate(self.chips):
            slot = self._id((*chip, self.c))
            out.append((2 + j, 5 + j, self._push(q, 5 + j, dst.at[slot], slot, self.sib)))
        return out

    def start(self):
        for q in range(len(self.items)):
            self._local(q).start()
            for _, cp in self._direct(q):
                cp.start()

    def forward(self):
        for q, (kind, _, dst) in enumerate(self.items):
            if kind != "ag":
                continue
            for k_in, _, fwd in self._forwards(q):
                self._push(q, k_in, dst.at[self.me], self.me, self.sib).wait_recv()
                fwd.start()

    def finish(self):
        for q, (kind, _, dst) in enumerate(self.items):
            self._local(q).wait()
            if kind == "xc":
                for _, cp in self._direct(q):
                    cp.wait()
                continue
            for k, cp in self._direct(q):
                cp.wait_send()
                if k == 1:
                    cp.wait_recv()
            for _, _, fwd in self._forwards(q):
                fwd.wait()


def comm_call(items, name):
    kinds = [k for k, _ in items]
    n = len(items)

    def body(*refs):
        cm = _Comm(kinds, refs[:n], refs[n:2 * n], refs[2 * n], refs[2 * n + 1])
        cm.start()
        cm.forward()
        cm.finish()

    return pl.pallas_call(
        body, name=name,
        in_specs=[pl.BlockSpec(memory_space=pl.ANY)] * n, out_specs=[pl.BlockSpec(memory_space=pl.ANY)] * n,
        out_shape=[_comm_out_shape(k, v) for k, v in items], scratch_shapes=_comm_scratch(n),
    )(*[v for _, v in items])


def all_gather(v, name):
    return comm_call([("ag", v)], name)[0]


def _pcall(body, args, *, name, grid, in_specs, out_specs, out_shape, scratch_shapes=(), sem, comm=()):
    nc, n_in, n_out = len(comm), len(in_specs), len(out_shape)
    if not nc:
        return pl.pallas_call(body, name=name, grid=grid, in_specs=list(in_specs), out_specs=list(out_specs),
                              out_shape=list(out_shape), scratch_shapes=list(scratch_shapes),
                              compiler_params=_cp(*sem))(*args)
    kinds = [k for k, _ in comm]
    steps = math.prod(grid)
    fwd_step = (3 * steps) // 5

    def carried(*refs):
        ins, csrc = refs[:n_in], refs[n_in:n_in + nc]
        outs, cdst = refs[n_in + nc:n_in + nc + n_out], refs[n_in + nc + n_out:n_in + 2 * nc + n_out]
        scr = refs[n_in + 2 * nc + n_out:]
        cm = _Comm(kinds, csrc, cdst, scr[-2], scr[-1])
        step = 0
        for d, g in enumerate(grid):
            step = step * g + pl.program_id(d)

        @pl.when(step == 0)
        def _():
            cm.start()

        body(*ins, *outs, *scr[:-2])

        @pl.when(step == fwd_step)
        def _():
            cm.forward()

        @pl.when(step == steps - 1)
        def _():
            cm.finish()

    hbm = pl.BlockSpec(memory_space=pl.ANY)
    out = pl.pallas_call(
        carried, name=name, grid=grid, in_specs=list(in_specs) + [hbm] * nc, out_specs=list(out_specs) + [hbm] * nc,
        out_shape=list(out_shape) + [_comm_out_shape(k, v) for k, v in comm],
        scratch_shapes=list(scratch_shapes) + _comm_scratch(nc), compiler_params=_cp(*(("arbitrary",) * len(grid))),
    )(*args, *[v for _, v in comm])
    return list(out[:n_out]), list(out[n_out:])


def mm(a, b, *, ta=False, out_dtype=F32, tm=512, tn=1024, tk=2048, i_outer=False, a_halves=False, b_halves=False,
       name, comm=()):
    if a_halves:
        m, kd = a.shape[1], 2 * a.shape[2]
    elif ta:
        kd, m = a.shape
    else:
        m, kd = a.shape
    kd2, n = (b.shape[1], 2 * b.shape[2]) if b_halves else b.shape
    assert kd == kd2 and not (ta and a_halves), (a.shape, b.shape, ta)
    tm, tn, tk = _tile(m, tm), _tile(n // 2 if b_halves else n, tn), _tile(kd // 2 if a_halves else kd, tk)
    nk = kd // tk
    nkh, njh = nk // 2, n // tn // 2
    grid = (m // tm, n // tn, nk) if i_outer else (n // tn, m // tm, nk)
    dims = (((0,) if ta else (1,), (0,)), ((), ()))

    def ix(f):
        return (lambda i, j, k: f(i, j, k)) if i_outer else (lambda j, i, k: f(i, j, k))

    def body(a_ref, b_ref, o_ref, *scr):
        p = lax.dot_general(a_ref[...], b_ref[...], dims, preferred_element_type=F32)
        if nk == 1:
            o_ref[...] = p.astype(o_ref.dtype)
        else:
            acc = scr[0]
            k = pl.program_id(2)

            @pl.when(k == 0)
            def _():
                acc[...] = p

            @pl.when(k > 0)
            def _():
                acc[...] += p

            @pl.when(k == nk - 1)
            def _():
                o_ref[...] = acc[...].astype(o_ref.dtype)

    if a_halves:
        a_spec = pl.BlockSpec((None, tm, tk), ix(lambda i, j, k: (k // nkh, i, k % nkh)))
    elif ta:
        a_spec = pl.BlockSpec((tk, tm), ix(lambda i, j, k: (k, i)))
    else:
        a_spec = pl.BlockSpec((tm, tk), ix(lambda i, j, k: (i, k)))
    if b_halves:
        b_spec = pl.BlockSpec((None, tk, tn), ix(lambda i, j, k: (j // njh, k, j % njh)))
    else:
        b_spec = pl.BlockSpec((tk, tn), ix(lambda i, j, k: (k, j)))
    out = _pcall(body, (a, b), name=name, grid=grid, in_specs=[a_spec, b_spec],
                 out_specs=[pl.BlockSpec((tm, tn), ix(lambda i, j, k: (i, j)))],
                 out_shape=[jax.ShapeDtypeStruct((m, n), out_dtype)],
                 scratch_shapes=[pltpu.VMEM((tm, tn), F32)] if nk > 1 else [],
                 sem=("parallel", "parallel", "arbitrary"), comm=comm)
    return (out[0][0], out[1]) if comm else out[0]


def ffn_in_act(h, w, name, comm=()):
    t = h.shape[0]
    tm, tn = _tile(t, 512), 512
    nj = D_FFP // tn

    def body(h_ref, wa_ref, wb_ref, ab_ref, act_ref):
        hv = h_ref[...]
        pa = _dot(hv, wa_ref[...])
        pb = _dot(hv, wb_ref[...])
        ab_ref[0] = pa.astype(BF16)
        ab_ref[1] = pb.astype(BF16)
        act_ref[...] = (pa * _sigmoid(pa) * pb).astype(BF16)

    out = _pcall(body, (h, w, w), name=name, grid=(nj, t // tm),
                 in_specs=[pl.BlockSpec((tm, D), lambda j, i: (i, 0)), pl.BlockSpec((D, tn), lambda j, i: (0, j)),
                           pl.BlockSpec((D, tn), lambda j, i: (0, nj + j))],
                 out_specs=[pl.BlockSpec((2, tm, tn), lambda j, i: (0, i, j)), pl.BlockSpec((tm, tn), lambda j, i: (i, j))],
                 out_shape=[jax.ShapeDtypeStruct((2, t, D_FFP), BF16), jax.ShapeDtypeStruct((t, D_FFP), BF16)],
                 sem=("parallel", "parallel"), comm=comm)
    return (out[0][0], out[0][1], out[1]) if comm else (out[0], out[1])


def ffn_dab(df, w_out_t, ab, name):
    t = df.shape[0]
    tm, tn = _tile(t, 1024), 512

    def body(d_ref, w_ref, ab_ref, o_ref):
        dv = _dot(d_ref[...], w_ref[...])
        a = ab_ref[0].astype(F32)
        b = ab_ref[1].astype(F32)
        s = _sigmoid(a)
        o_ref[0] = (dv * b * (s * (1.0 + a * (1.0 - s)))).astype(BF16)
        o_ref[1] = (dv * (a * s)).astype(BF16)

    return _pcall(body, (df, w_out_t, ab), name=name, grid=(D_FFP // tn, t // tm),
                  in_specs=[pl.BlockSpec((tm, D), lambda j, i: (i, 0)), pl.BlockSpec((D, tn), lambda j, i: (0, j)),
                            pl.BlockSpec((2, tm, tn), lambda j, i: (0, i, j))],
                  out_specs=[pl.BlockSpec((2, tm, tn), lambda j, i: (0, i, j))],
                  out_shape=[jax.ShapeDtypeStruct((2, t, D_FFP), BF16)], sem=("parallel", "parallel"))[0]


def ada_fwd(c8, w_loc, b_loc, name):
    n = w_loc.shape[1]
    tn = 256

    def body(c_ref, w_ref, b_ref, o_ref):
        cv = c_ref[...]
        ca = cv * _sigmoid(cv)
        o_ref[...] = jnp.dot(ca, w_ref[...], precision=HI, preferred_element_type=F32) + b_ref[...]

    return pl.pallas_call(
        body, name=name, grid=(n // tn,),
        in_specs=[pl.BlockSpec((N_DEV, D), lambda j: (0, 0)), pl.BlockSpec((D, tn), lambda j: (0, j)),
                  pl.BlockSpec((1, tn), lambda j: (0, j))],
        out_specs=pl.BlockSpec((N_DEV, tn), lambda j: (0, j)),
        out_shape=jax.ShapeDtypeStruct((N_DEV, n), F32), compiler_params=_cp("parallel"),
    )(c8, w_loc, b_loc)


def ada_bwd(c8t, dm_loc, name):
    n = dm_loc.shape[1]
    tn = 256

    def body(c_ref, d_ref, o_ref):
        cv = c_ref[...]
        ca = cv * _sigmoid(cv)
        o_ref[...] = jnp.dot(ca, d_ref[...], precision=HI, preferred_element_type=F32)

    return pl.pallas_call(
        body, name=name, grid=(n // tn,),
        in_specs=[pl.BlockSpec((D, N_DEV), lambda j: (0, 0)), pl.BlockSpec((N_DEV, tn), lambda j: (0, j))],
        out_specs=pl.BlockSpec((D, tn), lambda j: (0, j)),
        out_shape=jax.ShapeDtypeStruct((D, n), F32), compiler_params=_cp("parallel"),
    )(c8t, dm_loc)


def mod_fwd(x, nw, mods, shk, sck, *, f=None, gk=None, gscale=1.0, name):
    t = x.shape[0]
    tm = min(256, t)
    res = f is not None

    def body(*refs):
        if res:
            x_ref, f_ref, g_ref, nw_ref, sh_ref, sc_ref, x1_ref, h_ref = refs
            xv = x_ref[...] + (gscale * g_ref[...]) * f_ref[...]
            x1_ref[...] = xv
        else:
            x_ref, nw_ref, sh_ref, sc_ref, h_ref = refs
            xv = x_ref[...]
        r = lax.rsqrt(jnp.mean(xv * xv, axis=-1, keepdims=True) + EPS)
        h_ref[...] = ((xv * r * nw_ref[...]) * (1.0 + sc_ref[...]) + sh_ref[...]).astype(BF16)

    ins = [x] + ([f, mods] if res else []) + [nw, mods, mods]
    specs = [_row(tm, D)] + ([_row(tm, D), _vec(D, gk)] if res else []) + [_vec(D), _vec(D, shk), _vec(D, sck)]
    outs = ([jax.ShapeDtypeStruct((t, D), F32)] if res else []) + [jax.ShapeDtypeStruct((t, D), BF16)]
    ospecs = ([_row(tm, D)] if res else []) + [_row(tm, D)]
    out = pl.pallas_call(body, name=name, grid=(t // tm,), in_specs=specs, out_specs=ospecs, out_shape=outs,
                         compiler_params=_cp("parallel"))(*ins)
    return out if res else out[0]


def final_fwd_bwd(x2, f3, mods, nf, tgt, name):
    t = x2.shape[0]
    tm = min(128, t)

    def body(x_ref, f_ref, g_ref, nf_ref, t_ref, dx_ref, df_ref, st_ref):
        @pl.when(pl.program_id(0) == 0)
        def _():
            st_ref[...] = jnp.zeros_like(st_ref)

        g = 0.5 * g_ref[...]
        fv = f_ref[...]
        xv = x_ref[...] + g * fv
        r = lax.rsqrt(jnp.mean(xv * xv, axis=-1, keepdims=True) + EPS)
        xh = xv * r
        nfv = nf_ref[...]
        e = xh * nfv - t_ref[...]
        st_ref[2:3, :] += jnp.sum(e * e, axis=0, keepdims=True)
        dy = e * (1.0 / D)
        st_ref[0:1, :] += jnp.sum(dy * xh, axis=0, keepdims=True)
        dxh = dy * nfv
        dx = r * (dxh - xh * jnp.mean(dxh * xh, axis=-1, keepdims=True))
        dx_ref[...] = dx
        df_ref[...] = (g * dx).astype(BF16)
        st_ref[1:2, :] += 0.5 * jnp.sum(fv * dx, axis=0, keepdims=True)

    return pl.pallas_call(
        body, name=name, grid=(t // tm,),
        in_specs=[_row(tm, D), _row(tm, D), _vec(D, 8), _vec(D), _row(tm, D)],
        out_specs=[_row(tm, D), _row(tm, D), _stats(D)],
        out_shape=[jax.ShapeDtypeStruct((t, D), F32), jax.ShapeDtypeStruct((t, D), BF16),
                   jax.ShapeDtypeStruct((8, D), F32)],
        compiler_params=_cp("arbitrary"),
    )(x2, f3, mods, nf, tgt)


def mod_bwd(x_in, dh, dx_out, nw, mods, sck, *, fprev=None, gk=None, gscale=1.0, name):
    t = x_in.shape[0]
    tm = min(128, t)
    gate = fprev is not None

    def body(*refs):
        if gate:
            x_ref, dh_ref, dxo_ref, nw_ref, sc_ref, f_ref, g_ref, dx_ref, df_ref, st_ref = refs
        else:
            x_ref, dh_ref, dxo_ref, nw_ref, sc_ref, dx_ref, st_ref = refs

        @pl.when(pl.program_id(0) == 0)
        def _():
            st_ref[...] = jnp.zeros_like(st_ref)

        xv = x_ref[...]
        dhv = dh_ref[...]
        r = lax.rsqrt(jnp.mean(xv * xv, axis=-1, keepdims=True) + EPS)
        xh = xv * r
        nwv = nw_ref[...]
        st_ref[0:1, :] += jnp.sum(dhv, axis=0, keepdims=True)
        st_ref[1:2, :] += jnp.sum(dhv * (xh * nwv), axis=0, keepdims=True)
        dn = dhv * (1.0 + sc_ref[...])
        st_ref[2:3, :] += jnp.sum(dn * xh, axis=0, keepdims=True)
        dxh = dn * nwv
        dx = dxo_ref[...] + r * (dxh - xh * jnp.mean(dxh * xh, axis=-1, keepdims=True))
        dx_ref[...] = dx
        if gate:
            df_ref[...] = ((gscale * g_ref[...]) * dx).astype(BF16)
            st_ref[3:4, :] += gscale * jnp.sum(f_ref[...] * dx, axis=0, keepdims=True)

    ins = [x_in, dh, dx_out, nw, mods] + ([fprev, mods] if gate else [])
    specs = [_row(tm, D), _row(tm, D), _row(tm, D), _vec(D), _vec(D, sck)] + ([_row(tm, D), _vec(D, gk)] if gate else [])
    outs = [jax.ShapeDtypeStruct((t, D), F32)] + ([jax.ShapeDtypeStruct((t, D), BF16)] if gate else []) + \
        [jax.ShapeDtypeStruct((8, D), F32)]
    ospecs = [_row(tm, D)] + ([_row(tm, D)] if gate else []) + [_stats(D)]
    return pl.pallas_call(body, name=name, grid=(t // tm,), in_specs=specs, out_specs=ospecs, out_shape=outs,
                          compiler_params=_cp("arbitrary"))(*ins)


def _conv_pre(cur, prev8, w, b, tm):
    full = jnp.concatenate([prev8, cur], axis=0)
    pre = b + w[3:4, :] * cur
    for k in range(CONV_K - 1):
        s = CONV_K - 1 - k
        pre = pre + w[k:k + 1, :] * pltpu.roll(full, s, 0)[8:8 + tm, :]
    return pre


def conv_fwd(proj, cw_full, cb_full, name):
    t = proj.shape[0]
    tm = min(256, t)
    cwid = 1024
    cb0 = P_XBC // cwid

    def body(x_ref, p_ref, w_ref, b_ref, o_ref):
        i = pl.program_id(1)
        prev8 = jnp.where(i == 0, 0.0, p_ref[...])
        pre = _conv_pre(x_ref[...], prev8, w_ref[...], b_ref[...], tm)
        o_ref[...] = pre * _sigmoid(pre)

    return pl.pallas_call(
        body, name=name, grid=(CONV_DIM // cwid, t // tm),
        in_specs=[pl.BlockSpec((tm, cwid), lambda j, i: (i, cb0 + j)),
                  pl.BlockSpec((8, cwid), lambda j, i: (jnp.maximum(i * (tm // 8) - 1, 0), cb0 + j)),
                  pl.BlockSpec((CONV_K, cwid), lambda j, i: (0, j)), pl.BlockSpec((1, cwid), lambda j, i: (0, j))],
        out_specs=pl.BlockSpec((tm, cwid), lambda j, i: (i, j)),
        out_shape=jax.ShapeDtypeStruct((t, CONV_DIM), F32), compiler_params=_cp("parallel", "parallel"),
    )(proj, proj, cw_full, cb_full)


def conv_bwd_pre(proj, dxc, cw_full, cb_full, name):
    t = proj.shape[0]
    tm = min(256, t)
    cwid = 1024
    cb0 = P_XBC // cwid

    def body(x_ref, p_ref, d_ref, w_ref, b_ref, o_ref, st_ref):
        i = pl.program_id(1)

        @pl.when(i == 0)
        def _():
            st_ref[...] = jnp.zeros_like(st_ref)

        cur = x_ref[...]
        prev8 = jnp.where(i == 0, 0.0, p_ref[...])
        pre = _conv_pre(cur, prev8, w_ref[...], b_ref[...], tm)
        s = _sigmoid(pre)
        dpre = d_ref[...] * (s * (1.0 + pre * (1.0 - s)))
        o_ref[...] = dpre
        st_ref[4:5, :] += jnp.sum(dpre, axis=0, keepdims=True)
        st_ref[3:4, :] += jnp.sum(dpre * cur, axis=0, keepdims=True)
        full = jnp.concatenate([prev8, cur], axis=0)
        for k in range(CONV_K - 1):
            sft = CONV_K - 1 - k
            st_ref[k:k + 1, :] += jnp.sum(dpre * pltpu.roll(full, sft, 0)[8:8 + tm, :], axis=0, keepdims=True)

    return pl.pallas_call(
        body, name=name, grid=(CONV_DIM // cwid, t // tm),
        in_specs=[pl.BlockSpec((tm, cwid), lambda j, i: (i, cb0 + j)),
                  pl.BlockSpec((8, cwid), lambda j, i: (jnp.maximum(i * (tm // 8) - 1, 0), cb0 + j)),
                  pl.BlockSpec((tm, cwid), lambda j, i: (i, j)),
                  pl.BlockSpec((CONV_K, cwid), lambda j, i: (0, j)), pl.BlockSpec((1, cwid), lambda j, i: (0, j))],
        out_specs=[pl.BlockSpec((tm, cwid), lambda j, i: (i, j)), pl.BlockSpec((8, cwid), lambda j, i: (0, j))],
        out_shape=[jax.ShapeDtypeStruct((t, CONV_DIM), F32), jax.ShapeDtypeStruct((8, CONV_DIM), F32)],
        compiler_params=_cp("parallel", "arbitrary"),
    )(proj, proj, dxc, cw_full, cb_full)


def conv_bwd_in(dpre, cw_full, name):
    t = dpre.shape[0]
    tm = min(256, t)
    cwid = 1024
    nt = t // tm

    def body(d_ref, n_ref, w_ref, o_ref):
        i = pl.program_id(1)
        cur = d_ref[...]
        nxt = jnp.where(i == nt - 1, 0.0, n_ref[...])
        full = jnp.concatenate([cur, nxt], axis=0)
        w = w_ref[...]
        acc = w[3:4, :] * cur
        for k in range(CONV_K - 1):
            s = CONV_K - 1 - k
            acc = acc + w[k:k + 1, :] * pltpu.roll(full, tm + 8 - s, 0)[0:tm, :]
        o_ref[...] = acc.astype(BF16)

    return pl.pallas_call(
        body, name=name, grid=(CONV_DIM // cwid, nt),
        in_specs=[pl.BlockSpec((tm, cwid), lambda j, i: (i, j)),
                  pl.BlockSpec((8, cwid), lambda j, i: (jnp.minimum((i + 1) * (tm // 8), t // 8 - 1), j)),
                  pl.BlockSpec((CONV_K, cwid), lambda j, i: (0, j))],
        out_specs=pl.BlockSpec((tm, cwid), lambda j, i: (i, j)),
        out_shape=jax.ShapeDtypeStruct((t, CONV_DIM), BF16), compiler_params=_cp("parallel", "parallel"),
    )(dpre, dpre, cw_full)


def _nt(a, b):
    return lax.dot_general(a, b, (((1,), (1,)), ((), ())), preferred_element_type=F32)


def _dot(a, b):
    return jnp.dot(a, b, preferred_element_type=F32)


def _head_lanes():
    return lax.broadcasted_iota(jnp.int32, (1, 128), 1) < NH


def _expand_heads(x, e3):
    x = jnp.where(_head_lanes(), x, 0.0)
    hi = x.astype(BF16).astype(F32)
    r1 = x - hi
    mid = r1.astype(BF16).astype(F32)
    packed = hi + pltpu.roll(mid, NH, 1) + pltpu.roll(r1 - mid, 2 * NH, 1)
    return _dot(packed.astype(BF16), e3)


def _reduce_heads(v, e3):
    hi = v.astype(BF16)
    lo = (v - hi.astype(F32)).astype(BF16)
    return jnp.where(_head_lanes(), _nt(hi, e3) + _nt(lo, e3), 0.0)


def _ssd_common(dt_ref, dtb_ref, al_ref, exp_ref, tri_ref):
    a_row = jnp.where(_head_lanes(), -jnp.exp(al_ref[...]), 0.0)
    zraw = dt_ref[...] + dtb_ref[...]
    dtv = _softplus(zraw)
    cs = jnp.dot(tri_ref[...], dtv * a_row, precision=HI, preferred_element_type=F32)
    e3 = exp_ref[...]
    return a_row, zraw, dtv, cs, _expand_heads(cs, e3), _expand_heads(dtv, e3)


def ssd_fwd(xc, proj, dtb_row, alog_row, dx_row, expm, tri, name):
    t = xc.shape[0]
    nc = t // LCH

    def body(xs_ref, bm_ref, cm_ref, dt_ref, dtb_ref, al_ref, dxr_ref, exp_ref, tri_ref, y_ref, hs_ref, h_scr):
        @pl.when(pl.program_id(0) == 0)
        def _():
            h_scr[...] = jnp.zeros_like(h_scr)

        _, _, _, cs, csx, dtx = _ssd_common(dt_ref, dtb_ref, al_ref, exp_ref, tri_ref)
        cst = cs.T
        csl = csx[LCH - 1:LCH, :]
        xs = xs_ref[...]
        xd = xs * dtx
        xdw = xd * jnp.exp(csl - csx)
        ecs = jnp.exp(csx)
        ecl = jnp.exp(csl)
        tril = lax.broadcasted_iota(jnp.int32, (LCH, LCH), 0) >= lax.broadcasted_iota(jnp.int32, (LCH, LCH), 1)
        hs_ref[...] = h_scr[...]
        for g in range(NG):
            gc = slice(g * 512, (g + 1) * 512)
            bm = bm_ref[:, g * NST:(g + 1) * NST]
            cmb = cm_ref[:, g * NST:(g + 1) * NST].astype(BF16)
            gm = _nt(cmb, bm.astype(BF16))
            hg = h_scr[:, gc]
            yo = _dot(cmb, hg.astype(BF16)) * ecs[:, gc]
            st = _dot(bm.T.astype(BF16), xdw[:, gc].astype(BF16))
            for r in range(8):
                h = g * 8 + r
                hc = slice(h * HP, (h + 1) * HP)
                seg = cs[:, h:h + 1] - cst[h:h + 1, :]
                m = (gm * jnp.exp(jnp.where(tril, seg, NEG))).astype(BF16)
                yd = _dot(m, xd[:, hc].astype(BF16))
                y_ref[:, hc] = yd + yo[:, r * HP:(r + 1) * HP] + dxr_ref[:, hc] * xs[:, hc]
            h_scr[:, gc] = ecl[:, gc] * hg + st

    return pl.pallas_call(
        body, name=name, grid=(nc,),
        in_specs=[pl.BlockSpec((LCH, 2048), lambda c: (c, 0)), pl.BlockSpec((LCH, 512), lambda c: (c, 4)),
                  pl.BlockSpec((LCH, 512), lambda c: (c, 5)), pl.BlockSpec((LCH, 128), lambda c: (c, P_DT // 128)),
                  _vec(128), _vec(128), _vec(2048), pl.BlockSpec((128, 2048), lambda c: (0, 0)),
                  pl.BlockSpec((LCH, LCH), lambda c: (0, 0))],
        out_specs=[pl.BlockSpec((LCH, 2048), lambda c: (c, 0)), pl.BlockSpec((None, NST, 2048), lambda c: (c, 0, 0))],
        out_shape=[jax.ShapeDtypeStruct((t, 2048), F32), jax.ShapeDtypeStruct((nc, NST, 2048), F32)],
        scratch_shapes=[pltpu.VMEM((NST, 2048), F32)],
        compiler_params=_cp("arbitrary"),
    )(xc, xc, xc, proj, dtb_row, alog_row, dx_row, expm, tri)


def ssd_bwd(xc, proj, hsave, dy, dtb_row, alog_row, dx_row, expm, tri, name):
    t = xc.shape[0]
    nc = t // LCH

    def body(xs_ref, bm_ref, cm_ref, dt_ref, hs_ref, dy_ref, dtb_ref, al_ref, dxr_ref, exp_ref, tri_ref,
             dxc_ref, ddt_ref, st_ref, dh_scr, dxd_scr, dcsx_scr):
        @pl.when(pl.program_id(0) == 0)
        def _():
            dh_scr[...] = jnp.zeros_like(dh_scr)
            st_ref[...] = jnp.zeros_like(st_ref)

        a_row, zraw, dtv, cs, csx, dtx = _ssd_common(dt_ref, dtb_ref, al_ref, exp_ref, tri_ref)
        e = exp_ref[...]
        cst = cs.T
        csl = csx[LCH - 1:LCH, :]
        xs = xs_ref[...]
        xd = xs * dtx
        wend = jnp.exp(csl - csx)
        xdw = xd * wend
        ecs = jnp.exp(csx)
        ecl = jnp.exp(csl)
        ri = lax.broadcasted_iota(jnp.int32, (LCH, LCH), 0)
        ci = lax.broadcasted_iota(jnp.int32, (LCH, LCH), 1)
        tril = ri >= ci
        triu = ri <= ci
        lane = lax.broadcasted_iota(jnp.int32, (1, 128), 1)
        dyv = dy_ref[...]
        dxr = dxr_ref[...]
        st_ref[2:3, :] += _reduce_heads(jnp.sum(dyv * xs, axis=0, keepdims=True), e)
        dcs = jnp.zeros((LCH, 128), F32)
        for g in range(NG):
            gc = slice(g * 512, (g + 1) * 512)
            bmb = bm_ref[:, g * NST:(g + 1) * NST].astype(BF16)
            cm = cm_ref[:, g * NST:(g + 1) * NST]
            cmb = cm.astype(BF16)
            hg = hs_ref[:, gc]
            hgb = hg.astype(BF16)
            dhc = dh_scr[:, gc]
            dhcb = dhc.astype(BF16)
            dyg = dyv[:, gc]
            yo = _dot(cmb, hgb) * ecs[:, gc]
            dq = (dyg * ecs[:, gc]).astype(BF16)
            dcm = _nt(dq, hgb)
            dh_yo = _dot(cm.T.astype(BF16), dq)
            dxdw = _dot(bmb, dhcb)
            dbm = _nt(xdw[:, gc].astype(BF16), dhcb)
            tt = dxdw * xdw[:, gc]
            dcsx_g = dyg * yo - tt
            dcsl_g = jnp.sum(tt, axis=0, keepdims=True) + jnp.sum(dhc * hg, axis=0, keepdims=True) * ecl[:, gc]
            dxd_scr[:, gc] = dxdw * wend[:, gc]
            dh_scr[:, gc] = ecl[:, gc] * dhc + dh_yo
            gm = _nt(cmb, bmb)
            gmt = _nt(bmb, cmb)
            dg = jnp.zeros((LCH, LCH), F32)
            dgt = jnp.zeros((LCH, LCH), F32)
            for r in range(8):
                h = g * 8 + r
                hc = slice(h * HP, (h + 1) * HP)
                seg = cs[:, h:h + 1] - cst[h:h + 1, :]
                lm = jnp.exp(jnp.where(tril, seg, NEG))
                lmt = jnp.exp(jnp.where(triu, -seg, NEG))
                mm_ = gm * lm
                mmt = gmt * lmt
                xdh = xd[:, hc].astype(BF16)
                dyh = dyv[:, hc].astype(BF16)
                dm = _nt(dyh, xdh)
                dmt = _nt(xdh, dyh)
                dxd_scr[:, hc] += _dot(mmt.astype(BF16), dyh)
                rs = jnp.sum(dm * mm_, axis=1, keepdims=True) - jnp.sum(dmt * mmt, axis=1, keepdims=True)
                dcs = dcs + rs * jnp.where(lane == h, 1.0, 0.0)
                dg = dg + dm * lm
                dgt = dgt + dmt * lmt
            dcm = dcm + _dot(dg.astype(BF16), bmb)
            dbm = dbm + _dot(dgt.astype(BF16), cmb)
            dxc_ref[:, 2048 + g * NST:2048 + (g + 1) * NST] = dbm
            dxc_ref[:, 2560 + g * NST:2560 + (g + 1) * NST] = dcm
            dcsx_scr[:, gc] = dcsx_g
            dcsx_scr[LCH - 1:LCH, gc] += dcsl_g
        dxd = dxd_scr[...]
        dxc_ref[:, 0:2048] = dxr * dyv + dxd * dtx
        ddtv = _reduce_heads(dxd * xs, e)
        dcs = dcs + _reduce_heads(dcsx_scr[...], e)
        dda =lax.dot_general(tri_ref[...], dcs, (((0,), (0,)), ((), ())), precision=HI, preferred_element_type=F32)
        ddtv = ddtv + dda * a_row
        st_ref[0:1, :] += jnp.sum(dda * dtv, axis=0, keepdims=True) * a_row
        ddt = ddtv * _sigmoid(zraw)
        ddt_ref[...] = ddt
        st_ref[1:2, :] += jnp.sum(ddt, axis=0, keepdims=True)

    rc = lambda c: nc - 1 - c
    return pl.pallas_call(
        body, name=name, grid=(nc,),
        in_specs=[pl.BlockSpec((LCH, 2048), lambda c: (rc(c), 0)), pl.BlockSpec((LCH, 512), lambda c: (rc(c), 4)),
                  pl.BlockSpec((LCH, 512), lambda c: (rc(c), 5)),
                  pl.BlockSpec((LCH, 128), lambda c: (rc(c), P_DT // 128)),
                  pl.BlockSpec((None, NST, 2048), lambda c: (rc(c), 0, 0)),
                  pl.BlockSpec((LCH, 2048), lambda c: (rc(c), 0)),
                  _vec(128), _vec(128), _vec(2048), pl.BlockSpec((128, 2048), lambda c: (0, 0)),
                  pl.BlockSpec((LCH, LCH), lambda c: (0, 0))],
        out_specs=[pl.BlockSpec((LCH, CONV_DIM), lambda c: (rc(c), 0)), pl.BlockSpec((LCH, 128), lambda c: (rc(c), 0)),
                   _stats(128)],
        out_shape=[jax.ShapeDtypeStruct((t, CONV_DIM), F32), jax.ShapeDtypeStruct((t, 128), F32),
                   jax.ShapeDtypeStruct((8, 128), F32)],
        scratch_shapes=[pltpu.VMEM((NST, 2048), F32), pltpu.VMEM((LCH, 2048), F32), pltpu.VMEM((LCH, 2048), F32)],
        compiler_params=_cp("arbitrary"),
    )(xc, xc, xc, proj, hsave, dy, dtb_row, alog_row, dx_row, expm, tri)


def ssd_out_fwd(y, proj, nw, name):
    t = y.shape[0]
    tm = min(256, t)

    def body(y_ref, z_ref, nw_ref, o_ref):
        for g in range(NG):
            gc = slice(g * 512, (g + 1) * 512)
            z = z_ref[:, gc]
            yz = y_ref[:, gc] * (z * _sigmoid(z))
            r = lax.rsqrt(jnp.mean(yz * yz, axis=-1, keepdims=True) + EPS)
            o_ref[:, gc] = (yz * r * nw_ref[:, gc]).astype(BF16)

    return pl.pallas_call(body, name=name, grid=(t // tm,),
                          in_specs=[_row(tm, 2048), _row(tm, 2048, P_Z // 2048), _vec(2048)],
                          out_specs=_row(tm, 2048), out_shape=jax.ShapeDtypeStruct((t, 2048), BF16),
                          compiler_params=_cp("parallel"))(y, proj, nw)


def ssd_out_bwd(y, proj, dya, nw, name):
    t = y.shape[0]
    tm = min(256, t)

    def body(y_ref, z_ref, d_ref, nw_ref, dy_ref, dz_ref, st_ref):
        @pl.when(pl.program_id(0) == 0)
        def _():
            st_ref[...] = jnp.zeros_like(st_ref)

        for g in range(NG):
            gc = slice(g * 512, (g + 1) * 512)
            z = z_ref[:, gc]
            yv = y_ref[:, gc]
            s = _sigmoid(z)
            sz = z * s
            yz = yv * sz
            r = lax.rsqrt(jnp.mean(yz * yz, axis=-1, keepdims=True) + EPS)
            yzn = yz * r
            dv = d_ref[:, gc]
            st_ref[0:1, gc] += jnp.sum(dv * yzn, axis=0, keepdims=True)
            dyn = dv * nw_ref[:, gc]
            dyz = r * (dyn - yzn * jnp.mean(dyn * yzn, axis=-1, keepdims=True))
            dy_ref[:, gc] = dyz * sz
            dz_ref[:, gc] = (dyz * yv * (s * (1.0 + z * (1.0 - s)))).astype(BF16)

    return pl.pallas_call(
        body, name=name, grid=(t // tm,),
        in_specs=[_row(tm, 2048), _row(tm, 2048, P_Z // 2048), _row(tm, 2048), _vec(2048)],
        out_specs=[_row(tm, 2048), _row(tm, 2048), _stats(2048)],
        out_shape=[jax.ShapeDtypeStruct((t, 2048), F32), jax.ShapeDtypeStruct((t, 2048), BF16),
                   jax.ShapeDtypeStruct((8, 2048), F32)],
        compiler_params=_cp("arbitrary"))(y, proj, dya, nw)


def _cstep(ar, ai, sr, si, br, bi):
    return ar * sr - ai * si + br, ar * si + ai * sr + bi


def _local_ends(x_ref, nsteps, ar, ai, sr_scr, si_scr, end_ref, first, last, reverse):
    @pl.when(first)
    def _():
        sr_scr[...] = jnp.zeros_like(sr_scr)
        si_scr[...] = jnp.zeros_like(si_scr)

    def step(k, carry):
        tt = nsteps - 1 - k if reverse else k
        return _cstep(ar, ai, carry[0], carry[1], x_ref[tt, :, 0:512], x_ref[tt, :, 512:1024])

    sr, si = lax.fori_loop(0, nsteps, step, (sr_scr[...], si_scr[...]), unroll=8)
    sr_scr[...] = sr
    si_scr[...] = si

    @pl.when(last)
    def _():
        end_ref[:, 0:512] = sr
        end_ref[:, 512:1024] = si


def s5_in(u, bsg, a_re, a_im, name):
    t = u.shape[0]
    tm = min(512, t)
    nt = t // tm

    def body(u_ref, b_ref, ar_ref, ai_ref, o_ref, e_ref, sr_scr, si_scr):
        i = pl.program_id(1)
        o_ref[...] = _dot(u_ref[...].astype(BF16), b_ref[...]).reshape(tm // 8, 8, 1024)
        _local_ends(o_ref, tm // 8, ar_ref[...], ai_ref[...], sr_scr, si_scr, e_ref, i == 0, i == nt - 1, False)

    return pl.pallas_call(
        body, name=name, grid=(8, nt),
        in_specs=[pl.BlockSpec((tm, 128), lambda s, i: (i, s)), pl.BlockSpec((None, 128, 1024), lambda s, i: (s, 0, 0)),
                  pl.BlockSpec((None, 8, 512), lambda s, i: (s, 0, 0)), pl.BlockSpec((None, 8, 512), lambda s, i: (s, 0, 0))],
        out_specs=[pl.BlockSpec((tm // 8, 8, 1024), lambda s, i: (i, 0, s)),
                   pl.BlockSpec((None, 8, 1024), lambda s, i: (s, 0, 0))],
        out_shape=[jax.ShapeDtypeStruct((t // 8, 8, S5NS), F32), jax.ShapeDtypeStruct((8, 8, 1024), F32)],
        scratch_shapes=[pltpu.VMEM((8, 512), F32), pltpu.VMEM((8, 512), F32)],
        compiler_params=_cp("parallel", "arbitrary"))(u, bsg, a_re, a_im)


def s5_out(s, csg, u, d_row, name):
    t = u.shape[0]
    tm = min(512, t)

    def body(s_ref, c_ref, u_ref, d_ref, o_ref):
        o_ref[...] = _dot(s_ref[...].astype(BF16), c_ref[...]) + d_ref[...] * u_ref[...]

    return pl.pallas_call(
        body, name=name, grid=(8, t // tm),
        in_specs=[pl.BlockSpec((tm, 1024), lambda s, i: (i, s)), pl.BlockSpec((None, 1024, 128), lambda s, i: (s, 0, 0)),
                  pl.BlockSpec((tm, 128), lambda s, i: (i, s)), pl.BlockSpec((1, 128), lambda s, i: (0, s))],
        out_specs=pl.BlockSpec((tm, 128), lambda s, i: (i, s)),
        out_shape=jax.ShapeDtypeStruct((t, S5W), F32), compiler_params=_cp("parallel", "parallel"))(s, csg, u, d_row)


def s5_out_bwd(dy, csg, s, a_re, a_im, name):
    t = dy.shape[0]
    tm = min(512, t)
    nt = t // tm

    def body(dy_ref, c_ref, s_ref, ar_ref, ai_ref, e_ref, dc_ref, end_ref, sr_scr, si_scr):
        i = pl.program_id(1)

        @pl.when(i == 0)
        def _():
            dc_ref[...] = jnp.zeros_like(dc_ref)

        dyb = dy_ref[...].astype(BF16)
        e_ref[...] = _nt(dyb, c_ref[...]).reshape(tm // 8, 8, 1024)
        dc_ref[...] += lax.dot_general(s_ref[...].astype(BF16), dyb, (((0,), (0,)), ((), ())),
                                       preferred_element_type=F32)
        _local_ends(e_ref, tm // 8, ar_ref[...], -ai_ref[...], sr_scr, si_scr, end_ref, i == 0, i == nt - 1, True)

    rv = lambda i: nt - 1 - i
    return pl.pallas_call(
        body, name=name, grid=(8, nt),
        in_specs=[pl.BlockSpec((tm, 128), lambda s, i: (rv(i), s)), pl.BlockSpec((None, 1024, 128), lambda s, i: (s, 0, 0)),
                  pl.BlockSpec((tm, 1024), lambda s, i: (rv(i), s)),
                  pl.BlockSpec((None, 8, 512), lambda s, i: (s, 0, 0)), pl.BlockSpec((None, 8, 512), lambda s, i: (s, 0, 0))],
        out_specs=[pl.BlockSpec((tm // 8, 8, 1024), lambda s, i: (rv(i), 0, s)),
                   pl.BlockSpec((None, 1024, 128), lambda s, i: (s, 0, 0)),
                   pl.BlockSpec((None, 8, 1024), lambda s, i: (s, 0, 0))],
        out_shape=[jax.ShapeDtypeStruct((t // 8, 8, S5NS), F32), jax.ShapeDtypeStruct((8, 1024, 128), F32),
                   jax.ShapeDtypeStruct((8, 8, 1024), F32)],
        scratch_shapes=[pltpu.VMEM((8, 512), F32), pltpu.VMEM((8, 512), F32)],
        compiler_params=_cp("parallel", "arbitrary"))(dy, csg, s, a_re, a_im)


def s5_in_bwd(lam, bsg, u, dy, d_row, name):
    t = u.shape[0]
    tm = min(512, t)

    def body(l_ref, b_ref, u_ref, dy_ref, d_ref, du_ref, db_ref, dd_ref):
        @pl.when(pl.program_id(1) == 0)
        def _():
            db_ref[...] = jnp.zeros_like(db_ref)
            dd_ref[...] = jnp.zeros_like(dd_ref)

        lb = l_ref[...].astype(BF16)
        uv = u_ref[...]
        dyv = dy_ref[...]
        du_ref[...] = _nt(lb, b_ref[...]) + d_ref[...] * dyv
        db_ref[...] += lax.dot_general(uv.astype(BF16), lb, (((0,), (0,)), ((), ())), preferred_element_type=F32)
        dd_ref[...] += jnp.sum(dyv * uv, axis=0, keepdims=True)

    return pl.pallas_call(
        body, name=name, grid=(8, t // tm),
        in_specs=[pl.BlockSpec((tm, 1024), lambda s, i: (i, s)), pl.BlockSpec((None, 128, 1024), lambda s, i: (s, 0, 0)),
                  pl.BlockSpec((tm, 128), lambda s, i: (i, s)), pl.BlockSpec((tm, 128), lambda s, i: (i, s)),
                  pl.BlockSpec((1, 128), lambda s, i: (0, s))],
        out_specs=[pl.BlockSpec((tm, 128), lambda s, i: (i, s)), pl.BlockSpec((None, 128, 1024), lambda s, i: (s, 0, 0)),
                   pl.BlockSpec((1, 128), lambda s, i: (0, s))],
        out_shape=[jax.ShapeDtypeStruct((t, S5W), F32), jax.ShapeDtypeStruct((8, 128, 1024), F32),
                   jax.ShapeDtypeStruct((1, S5W), F32)],
        compiler_params=_cp("parallel", "arbitrary"))(lam, bsg, u, dy, d_row)


def s5_scan_init(ends, a_re, a_im, lseg, reverse, name):
    nsq = int(math.log2(lseg))
    assert 2 ** nsq == lseg
    sgn = -1.0 if reverse else 1.0
    order = list(range(7, -1, -1)) if reverse else list(range(8))

    def body(e_ref, ar_ref, ai_ref, o_ref):
        pr = ar_ref[0:1, :]
        pi = sgn * ai_ref[0:1, :]
        for _ in range(nsq):
            pr, pi = pr * pr - pi * pi, 2.0 * pr * pi
        prev_r = jnp.zeros((1, 512), F32)
        prev_i = jnp.zeros((1, 512), F32)
        j0 = order[0]
        o_ref[j0:j0 + 1, 0:512] = prev_r
        o_ref[j0:j0 + 1, 512:1024] = prev_i
        for idx in range(1, 8):
            j, jp = order[idx], order[idx - 1]
            prev_r, prev_i = _cstep(pr, pi, prev_r, prev_i, e_ref[jp:jp + 1, 0:512], e_ref[jp:jp + 1, 512:1024])
            o_ref[j:j + 1, 0:512] = prev_r
            o_ref[j:j + 1, 512:1024] = prev_i

    return pl.pallas_call(
        body, name=name, grid=(8,),
        in_specs=[pl.BlockSpec((None, 8, 1024), lambda s: (s, 0, 0)), pl.BlockSpec((None, 8, 512), lambda s: (s, 0, 0)),
                  pl.BlockSpec((None, 8, 512), lambda s: (s, 0, 0))],
        out_specs=pl.BlockSpec((None, 8, 1024), lambda s: (s, 0, 0)),
        out_shape=jax.ShapeDtypeStruct((8, 8, 1024), F32), compiler_params=_cp("parallel"))(ends, a_re, a_im)


def s5_scan_fwd(b3, init, a_re, a_im, name):
    lseg = b3.shape[0]
    ti = min(128, lseg)
    nb = lseg // ti

    def body(b_ref, i_ref, ar_ref, ai_ref, o_ref, sr_scr, si_scr):
        @pl.when(pl.program_id(1) == 0)
        def _():
            sr_scr[...] = i_ref[:, 0:512]
            si_scr[...] = i_ref[:, 512:1024]

        ar = ar_ref[...]
        ai = ai_ref[...]

        def step(k, carry):
            nr, ni = _cstep(ar, ai, carry[0], carry[1], b_ref[k, :, 0:512], b_ref[k, :, 512:1024])
            o_ref[k, :, 0:512] = nr
            o_ref[k, :, 512:1024] = ni
            return nr, ni

        sr, si = lax.fori_loop(0, ti, step, (sr_scr[...], si_scr[...]), unroll=8)
        sr_scr[...] = sr
        si_scr[...] = si

    return pl.pallas_call(
        body, name=name, grid=(8, nb),
        in_specs=[pl.BlockSpec((ti, 8, 1024), lambda s, tb: (tb, 0, s)), pl.BlockSpec((None, 8, 1024), lambda s, tb: (s, 0, 0)),
                  pl.BlockSpec((None, 8, 512), lambda s, tb: (s, 0, 0)), pl.BlockSpec((None, 8, 512), lambda s, tb: (s, 0, 0))],
        out_specs=pl.BlockSpec((ti, 8, 1024), lambda s, tb: (tb, 0, s)),
        out_shape=jax.ShapeDtypeStruct(b3.shape, F32),
        scratch_shapes=[pltpu.VMEM((8, 512), F32), pltpu.VMEM((8, 512), F32)],
        compiler_params=_cp("parallel", "arbitrary"))(b3, init, a_re, a_im)


def s5_scan_bwd(e3, linit, s3, sinit, a_re, a_im, name):
    lseg = e3.shape[0]
    ti = min(128, lseg)
    nb = lseg // ti

    def body(e_ref, li_ref, s_ref, sh_ref, si0_ref, ar_ref, ai_ref, o_ref, da_ref, lr_scr, lim_scr):
        tb = pl.program_id(1)

        @pl.when(tb == 0)
        def _():
            lr_scr[...] = li_ref[:, 0:512]
            lim_scr[...] = li_ref[:, 512:1024]
            da_ref[...] = jnp.zeros_like(da_ref)

        ar = ar_ref[...]
        ai = -ai_ref[...]

        def one(tt, lr, li, dar, dai, spr, spi):
            nr, ni = _cstep(ar, ai, lr, li, e_ref[tt, :, 0:512], e_ref[tt, :, 512:1024])
            o_ref[tt, :, 0:512] = nr
            o_ref[tt, :, 512:1024] = ni
            return nr, ni, dar + nr * spr + ni * spi, dai + ni * spr - nr * spi

        def step(k, carry):
            tt = ti - 1 - k
            return one(tt, *carry, s_ref[tt - 1, :, 0:512], s_ref[tt - 1, :, 512:1024])

        z = jnp.zeros((8, 512), F32)
        lr, li, dar, dai = lax.fori_loop(0, ti - 1, step, (lr_scr[...], lim_scr[...], z, z), unroll=8)
        first = tb == nb - 1
        spr = jnp.where(first, si0_ref[:, 0:512], sh_ref[0, :, 0:512])
        spi = jnp.where(first, si0_ref[:, 512:1024], sh_ref[0, :, 512:1024])
        lr, li, dar, dai = one(0, lr, li, dar, dai, spr, spi)
        lr_scr[...] = lr
        lim_scr[...] = li
        da_ref[:, 0:512] += dar
        da_ref[:, 512:1024] += dai

    rb = lambda tb: nb - 1 - tb
    return pl.pallas_call(
        body, name=name, grid=(8, nb),
        in_specs=[pl.BlockSpec((ti, 8, 1024), lambda s, tb: (rb(tb), 0, s)),
                  pl.BlockSpec((None, 8, 1024), lambda s, tb: (s, 0, 0)),
                  pl.BlockSpec((ti, 8, 1024), lambda s, tb: (rb(tb), 0, s)),
                  pl.BlockSpec((1, 8, 1024), lambda s, tb: (jnp.maximum(rb(tb) * ti - 1, 0), 0, s)),
                  pl.BlockSpec((None, 8, 1024), lambda s, tb: (s, 0, 0)),
                  pl.BlockSpec((None, 8, 512), lambda s, tb: (s, 0, 0)), pl.BlockSpec((None, 8, 512), lambda s, tb: (s, 0, 0))],
        out_specs=[pl.BlockSpec((ti, 8, 1024), lambda s, tb: (rb(tb), 0, s)),
                   pl.BlockSpec((None, 8, 1024), lambda s, tb: (s, 0, 0))],
        out_shape=[jax.ShapeDtypeStruct(e3.shape, F32), jax.ShapeDtypeStruct((8, 8, 1024), F32)],
        scratch_shapes=[pltpu.VMEM((8, 512), F32), pltpu.VMEM((8, 512), F32)],
        compiler_params=_cp("parallel", "arbitrary"))(e3, linit, s3, s3, sinit, a_re, a_im)


_GC = math.sqrt(2.0 / math.pi)


def gelu_fwd(y, name):
    t, w = y.shape
    tm = min(512, t)

    def body(y_ref, o_ref):
        v = y_ref[...]
        o_ref[...] = (0.5 * v * (1.0 + jnp.tanh(_GC * (v + 0.044715 * v * v * v)))).astype(BF16)

    return pl.pallas_call(body, name=name, grid=(t // tm,), in_specs=[_row(tm, w)], out_specs=_row(tm, w),
                          out_shape=jax.ShapeDtypeStruct((t, w), BF16), compiler_params=_cp("parallel"))(y)


def gelu_bwd(y, dg, name):
    t, w = y.shape
    tm = min(512, t)

    def body(y_ref, d_ref, o_ref):
        v = y_ref[...]
        th = jnp.tanh(_GC * (v + 0.044715 * v * v * v))
        o_ref[...] = d_ref[...] * (0.5 * (1.0 + th) + 0.5 * v * (1.0 - th * th) * _GC * (1.0 + 3.0 * 0.044715 * v * v))

    return pl.pallas_call(body, name=name, grid=(t // tm,), in_specs=[_row(tm, w), _row(tm, w)], out_specs=_row(tm, w),
                          out_shape=jax.ShapeDtypeStruct((t, w), F32), compiler_params=_cp("parallel"))(y, dg)


def merge_fwd(proj, pa, glu, name):
    t = pa.shape[0]
    tm = min(256, t)

    def body(g_ref, pa_ref, glu_ref, o_ref):
        pb = glu_ref[:, 0:D] * _sigmoid(glu_ref[:, D:2 * D])
        o_ref[...] = (_sigmoid(g_ref[:, 0:D]) * pa_ref[...] + _sigmoid(g_ref[:, D:2 * D]) * pb).astype(BF16)

    return pl.pallas_call(body, name=name, grid=(t // tm,), in_specs=[_row(tm, 2 * D), _row(tm, D), _row(tm, 2 * D)],
                          out_specs=_row(tm, D), out_shape=jax.ShapeDtypeStruct((t, D), BF16),
                          compiler_params=_cp("parallel"))(proj, pa, glu)


def merge_bwd(proj, pa, glu, dm, name):
    t = pa.shape[0]
    tm = min(256, t)

    def body(g_ref, pa_ref, glu_ref, dm_ref, dpa_ref, dglu_ref, dg_ref):
        dmv = dm_ref[...]
        pav = pa_ref[...]
        sa = _sigmoid(g_ref[:, 0:D])
        sb = _sigmoid(g_ref[:, D:2 * D])
        ga = glu_ref[:, 0:D]
        sg = _sigmoid(glu_ref[:, D:2 * D])
        pb = ga * sg
        dpb = sb * dmv
        dpa_ref[...] = (sa * dmv).astype(BF16)
        dglu_ref[:, 0:D] = (dpb * sg).astype(BF16)
        dglu_ref[:, D:2 * D] = (dpb * pb * (1.0 - sg)).astype(BF16)
        dg_ref[:, 0:D] = (dmv * pav * sa * (1.0 - sa)).astype(BF16)
        dg_ref[:, D:2 * D] = (dmv * pb * sb * (1.0 - sb)).astype(BF16)

    return pl.pallas_call(
        body, name=name, grid=(t // tm,),
        in_specs=[_row(tm, 2 * D), _row(tm, D), _row(tm, 2 * D), _row(tm, D)],
        out_specs=[_row(tm, D), _row(tm, 2 * D), _row(tm, 2 * D)],
        out_shape=[jax.ShapeDtypeStruct((t, D), BF16), jax.ShapeDtypeStruct((t, 2 * D), BF16),
                   jax.ShapeDtypeStruct((t, 2 * D), BF16)],
        compiler_params=_cp("parallel"))(proj, pa, glu, dm)


def adamw(w, parts, m, v, name):
    r, c = w.shape
    p = parts.shape[0]
    tr = r if r <= 128 else 128
    c1 = 1.0 - ADAM_B1 ** ADAM_STEP
    c2 = 1.0 - ADAM_B2 ** ADAM_STEP

    def body(w_ref, p_ref, m_ref, v_ref, g_ref, d_ref, nm_ref, nv_ref):
        g = p_ref[0].astype(F32)
        for k in range(1, p):
            g = g + p_ref[k].astype(F32)
        mn = ADAM_B1 * m_ref[...] + (1.0 - ADAM_B1) * g
        vn = ADAM_B2 * v_ref[...] + (1.0 - ADAM_B2) * (g * g)
        g_ref[...] = g
        nm_ref[...] = mn
        nv_ref[...] = vn
        d_ref[...] = -ADAM_LR * ((mn / c1) / (jnp.sqrt(vn / c2) + ADAM_EPS) + ADAM_WD * w_ref[...])

    spec = pl.BlockSpec((tr, c), lambda i: (i, 0))
    o = jax.ShapeDtypeStruct((r, c), F32)
    return pl.pallas_call(
        body, name=name, grid=(pl.cdiv(r, tr),),
        in_specs=[spec, pl.BlockSpec((p, tr, c), lambda i: (0, i, 0)), spec, spec],
        out_specs=[spec, spec, spec, spec], out_shape=[o, o, o, o], compiler_params=_cp("parallel"))(w, parts, m, v)


def _s5_discretise(lambda_re, lambda_im, log_dt, b_re, b_im):
    dt = jnp.exp(log_dt)[:, None]
    lr = jnp.minimum(lambda_re, -1e-4)
    li = lambda_im
    mag = jnp.exp(lr * dt)
    ar = mag * jnp.cos(li * dt)
    ai = mag * jnp.sin(li * dt)
    den = lr * lr + li * li
    nr = ar - 1.0
    kr = (nr * lr + ai * li) / den
    ki = (ai * lr - nr * li) / den
    bbar_re = kr[..., None] * b_re - ki[..., None] * b_im
    bbar_im = kr[..., None] * b_im + ki[..., None] * b_re
    return ar, ai, bbar_re, bbar_im


def _bsg_of(bb_re, bb_im):
    eye = jnp.eye(8, dtype=F32)
    f = lambda b: jnp.einsum("sgpi,gh->sgihp", b.reshape(8, 8, 64, 16), eye).reshape(8, 128, 512)
    return jnp.concatenate([f(bb_re), f(bb_im)], axis=2)


def _bsg_diag(dbsg):
    eye = jnp.eye(8, dtype=F32)
    f = lambda x: jnp.einsum("sgihp,gh->sgpi", x.reshape(8, 8, 16, 8, 64), eye).reshape(64, 64, 16)
    return f(dbsg[:, :, 0:512]), f(dbsg[:, :, 512:1024])


def _csg_of(c_re, c_im):
    eye = jnp.eye(8, dtype=F32)
    f = lambda c: jnp.einsum("sgip,gh->sgphi", c.reshape(8, 8, 16, 64), eye).reshape(8, 512, 128)
    return jnp.concatenate([f(c_re), -f(c_im)], axis=1)


def _csg_diag(dcsg):
    eye = jnp.eye(8, dtype=F32)
    f = lambda x: jnp.einsum("sgphi,gh->sgip", x.reshape(8, 8, 64, 8, 16), eye).reshape(64, 16, 64)
    return f(dcsg[:, 0:512, :]), -f(dcsg[:, 512:1024, :])


def _perm(a, t):
    return a.reshape(8, t // 8, a.shape[1]).transpose(1, 0, 2).reshape(t, a.shape[1])


def _unperm(a, t):
    return a.reshape(t // 8, 8, a.shape[1]).transpose(1, 0, 2).reshape(t, a.shape[1])


def _cols(g):
    return g.transpose(1, 0, 2).reshape(g.shape[1], N_DEV * g.shape[2])


def _rows(g):
    return g.reshape(N_DEV * g.shape[1], g.shape[2])


def _col_parts(g):
    r, c = g.shape
    return g.reshape(r, N_DEV, c // N_DEV).transpose(1, 0, 2)


def _row_parts(g):
    r, c = g.shape
    return g.reshape(N_DEV, r // N_DEV, c)


def _pad_ffn_in(w):
    z = jnp.zeros((D, D_FFP - D_FF), w.dtype)
    return jnp.concatenate([w[:, :D_FF], z, w[:, D_FF:], z], axis=1)


def _unpad_ffn_in(g):
    return jnp.concatenate([g[:, :D_FF], g[:, D_FFP:D_FFP + D_FF]], axis=1)


def _pad_w_in(w):
    z = jnp.zeros((D, INP - P_DT - NH), w.dtype)
    return jnp.concatenate([w[:, O_GA:O_GB], w[:, O_GB:IN_COLS], w[:, 0:O_XBC], w[:, O_XBC:O_DT], w[:, O_U:O_GA],
                            w[:, O_DT:O_U], z], axis=1)


def _unpad_w_in(g):
    return jnp.concatenate([g[:, P_Z:P_XBC], g[:, P_XBC:P_U], g[:, P_DT:P_DT + NH], g[:, P_U:P_DT],
                            g[:, 0:D], g[:, D:2 * D]], axis=1)


_PACK = (("b_ada", 18432), ("norm_ffn1", 2048), ("norm_mix", 2048), ("conv_b", 3072), ("dt_bias", 32), ("a_log", 32),
         ("d_ssd", 32), ("ssd_norm_w", 2048), ("s5_lambda_re", 4096), ("s5_lambda_im", 4096), ("s5_b_re", 65536),
         ("s5_b_im", 65536), ("s5_c_re", 65536), ("s5_c_im", 65536), ("s5_d", 1024), ("s5_log_dt", 64),
         ("norm_ffn2", 2048), ("norm_final", 2048), ("loss", 1))
_PACK_ROWS = 304
_PACK_W = 1024


def _pack(d):
    flat = jnp.concatenate([d[k].reshape(-1).astype(F32) for k, _ in _PACK])
    return jnp.pad(flat, (0, _PACK_ROWS * _PACK_W - flat.shape[0])).reshape(_PACK_ROWS, _PACK_W)


def _unpack(a):
    flat = a.reshape(-1)
    out, off = {}, 0
    for k, n in _PACK:
        out[k] = flat[off:off + n]
        off += n
    return out


_TA = dict(tm=512, tn=512, tk=8192)


def _ffn_bwd(df, h, ab, act, w_in_t, w_out_t, tag):
    dab = ffn_dab(df, w_out_t, ab, tag + "_dab")
    dw_out = mm(act, df, ta=True, out_dtype=BF16, i_outer=True, name=tag + "_dwout", **_TA)
    dw_in, (x_out,) = mm(h, dab, ta=True, b_halves=True, out_dtype=BF16, i_outer=True, name=tag + "_dwin",
                         comm=[("xc", _row_parts(dw_out[:D_FF]))], **_TA)
    dh, (x_in,) = mm(dab, w_in_t, a_halves=True, tk=5632, name=tag + "_dh",
                     comm=[("xc", _col_parts(_unpad_ffn_in(dw_in)))])
    return dh, x_in, x_out


def kernel(x, c, w_ada, b_ada, norm_ffn1, w_ffn1_in, w_ffn1_out, norm_mix, w_in, conv_w, conv_b, dt_bias, a_log, d_ssd, ssd_norm_w, w_a_proj, s5_lambda_re, s5_lambda_im, s5_b_re, s5_b_im, s5_c_re, s5_c_im, s5_d, s5_log_dt, w_b_glu, w_out, norm_ffn2, w_ffn2_in, w_ffn2_out, norm_final, loss_target, m_w_ada, m_b_ada, m_norm_ffn1, m_w_ffn1_in, m_w_ffn1_out, m_norm_mix, m_w_in, m_conv_w, m_conv_b, m_dt_bias, m_a_log, m_d_ssd, m_ssd_norm_w, m_w_a_proj, m_s5_lambda_re, m_s5_lambda_im, m_s5_b_re, m_s5_b_im, m_s5_c_re, m_s5_c_im, m_s5_d, m_s5_log_dt, m_w_b_glu, m_w_out, m_norm_ffn2, m_w_ffn2_in, m_w_ffn2_out, m_norm_final, v_w_ada, v_b_ada, v_norm_ffn1, v_w_ffn1_in, v_w_ffn1_out, v_norm_mix, v_w_in, v_conv_w, v_conv_b, v_dt_bias, v_a_log, v_d_ssd, v_ssd_norm_w, v_w_a_proj, v_s5_lambda_re, v_s5_lambda_im, v_s5_b_re, v_s5_b_im, v_s5_c_re, v_s5_c_im, v_s5_d, v_s5_log_dt, v_w_b_glu, v_w_out, v_norm_ffn2, v_w_ffn2_in, v_w_ffn2_out, v_norm_final):
    args = dict(locals())
    t = x.shape[1]
    me = _my_id()
    xt = x[0]
    tgt = loss_target[0]
    small = {k: args[k] for k, _ in _PACK if k != "loss"}

    bf = lambda w: w[0].astype(BF16)
    pad_rows = lambda w: jnp.pad(w, ((0, D_FFP - D_FF), (0, 0)))

    c8 = all_gather(c, "ag_c").reshape(N_DEV, D)
    b_loc = lax.dynamic_slice(b_ada, (0, me * (N_ADA * D // N_DEV)), (1, N_ADA * D // N_DEV))
    m8 = ada_fwd(c8, w_ada[0], b_loc, "ada_fwd")
    mods, g_f1i, g_cw = comm_call([("xc", m8.reshape(N_DEV, 1, -1)), ("ag", bf(w_ffn1_in)), ("ag", conv_w[0])],
                                  "xc_mods_ag_ffn1_in")
    mods = mods.reshape(1, N_ADA * D)
    convw = _cols(g_cw)
    wf1i = _pad_ffn_in(_cols(g_f1i))

    h1 = mod_fwd(xt, norm_ffn1, mods, 0, 1, name="mod1")
    ab1, act1, (g_f1o, g_wap, g_wo) = ffn_in_act(
        h1, wf1i, "ffn1_in", comm=[("ag", bf(w_ffn1_out)), ("ag", bf(w_a_proj)), ("ag", bf(w_out))])
    wf1o, wap, wo = pad_rows(_rows(g_f1o)), _rows(g_wap), _rows(g_wo)
    f1, (g_win,) = mm(act1, wf1o, tk=5632, name="ffn1_out", comm=[("ag", bf(w_in))])
    winp = _pad_w_in(_cols(g_win))
    x1, h2 = mod_fwd(xt, norm_mix, mods, 3, 4, f=f1, gk=2, gscale=0.5, name="mod2")
    proj, (g_f2i, g_wbg) = mm(h2, winp, tm=1024, tn=512, i_outer=True, name="w_in",
                              comm=[("ag", bf(w_ffn2_in)), ("ag", bf(w_b_glu))])
    wf2i, wbg = _pad_ffn_in(_cols(g_f2i)), _cols(g_wbg)
    cb_row = conv_b
    xc = conv_fwd(proj, convw, cb_row, "conv_fwd")
    row128 = lambda v: jnp.pad(v.reshape(1, -1), ((0, 0), (0, 128 - v.size)))
    dtb_row, alog_row = row128(dt_bias), row128(a_log)
    dx_row = jnp.repeat(d_ssd.reshape(-1), HP).reshape(1, 2048)
    rows = jnp.arange(128)[:, None]
    expm = ((rows % NH == jnp.arange(2048)[None, :] // HP) & (rows < 3 * NH)).astype(BF16)
    tri = (jnp.arange(LCH)[:, None] >= jnp.arange(LCH)[None, :]).astype(F32)
    y_ssd, hsave = ssd_fwd(xc, proj, dtb_row, alog_row, dx_row, expm, tri, "ssd_fwd")
    ya = ssd_out_fwd(y_ssd, proj, ssd_norm_w, "ssd_out")
    pa = mm(ya, wap, name="w_a_proj")

    s5p = (s5_lambda_re[0], s5_lambda_im[0], s5_log_dt[0], s5_b_re[0], s5_b_im[0])
    (ar, ai, bb_re, bb_im), s5_vjp = jax.vjp(_s5_discretise, *s5p)
    a_re8 = jnp.broadcast_to(ar.reshape(8, 1, 512), (8, 8, 512))
    a_im8 = jnp.broadcast_to(ai.reshape(8, 1, 512), (8, 8, 512))
    bsg = _bsg_of(bb_re, bb_im).astype(BF16)
    csg = _csg_of(s5_c_re[0], s5_c_im[0]).astype(BF16)
    d_row = s5_d.reshape(1, S5W)
    lseg = t // 8
    u_p = _perm(proj[:, P_U:P_U + S5W], t)
    bu3, ends_f = s5_in(u_p, bsg, a_re8, a_im8, "s5_in")
    sinit = s5_scan_init(ends_f, a_re8, a_im8, lseg, False, "s5_init_f")
    s3 = s5_scan_fwd(bu3, sinit, a_re8, a_im8, "s5_scan_f")
    s2 = s3.reshape(t, S5NS)
    yb_p = s5_out(s2, csg, u_p, d_row, "s5_out")
    yb = _unperm(yb_p, t)
    gy = gelu_fwd(yb, "gelu")
    glu = mm(gy, wbg, name="w_b_glu")
    merged = merge_fwd(proj, pa, glu, "merge")
    o = mm(merged, wo, name="w_out")
    x2, h3 = mod_fwd(x1, norm_ffn2, mods, 6, 7, f=o, gk=5, gscale=1.0, name="mod3")
    ab3, act3, (g_f2o,) = ffn_in_act(h3, wf2i, "ffn2_in", comm=[("ag", bf(w_ffn2_out))])
    wf2o = pad_rows(_rows(g_f2o))
    f3 = mm(act3, wf2o, tk=5632, name="ffn2_out")

    dx3, df3, st_fin = final_fwd_bwd(x2, f3, mods, norm_final.reshape(1, D), tgt, "final")
    dh3, x_f2i, x_f2o = _ffn_bwd(df3, h3, ab3, act3, wf2i.T, wf2o.T, "ffn2")
    dx2, do, st3 = mod_bwd(x2, dh3, dx3, norm_ffn2, mods, 7, fprev=o, gk=5, gscale=1.0, name="mod3_bwd")

    dmerged = mm(do, wo.T, name="w_out_dx")
    dwo = mm(merged, do, ta=True, out_dtype=BF16, i_outer=True, name="w_out_dw", **_TA)
    dpa, dglu, dgates = merge_bwd(proj, pa, glu, dmerged, "merge_bwd")
    dwbg = mm(gy, dglu, ta=True, out_dtype=BF16, i_outer=True, name="w_b_glu_dw", **_TA)
    dgy, (x_wo,) = mm(dglu, wbg.T, name="w_b_glu_dx", comm=[("xc", _row_parts(dwo))])
    dyb_p = _perm(gelu_bwd(yb, dgy, "gelu_bwd"), t)
    e3, dcsg, ends_b = s5_out_bwd(dyb_p, csg, s2, a_re8, a_im8, "s5_out_bwd")
    linit = s5_scan_init(ends_b, a_re8, a_im8, lseg, True, "s5_init_b")
    lam3, da8 = s5_scan_bwd(e3, linit, s3, sinit, a_re8, a_im8, "s5_scan_b")
    du_p, dbsg, dd_row = s5_in_bwd(lam3.reshape(t, S5NS), bsg, u_p, dyb_p, d_row, "s5_in_bwd")
    du = _unperm(du_p, t).astype(BF16)
    da = jnp.sum(da8, axis=1)
    dbb_re, dbb_im = _bsg_diag(dbsg)
    g_lre, g_lim, g_ldt, g_bre, g_bim = s5_vjp((da[:, 0:512].reshape(64, 64), da[:, 512:1024].reshape(64, 64),
                                                dbb_re, dbb_im))
    g_cre, g_cim = _csg_diag(dcsg)

    dwap = mm(ya, dpa, ta=True, out_dtype=BF16, i_outer=True, name="w_a_proj_dw", **_TA)
    dya, (x_wbg,) = mm(dpa, wap.T, name="w_a_proj_dx", comm=[("xc", _col_parts(dwbg))])
    dy_ssd, dz, st_sn = ssd_out_bwd(y_ssd, proj, dya, ssd_norm_w, "ssd_out_bwd")
    dxc, ddt, st_ssd = ssd_bwd(xc, proj, hsave, dy_ssd, dtb_row, alog_row, dx_row, expm, tri, "ssd_bwd")
    dpre, st_cv = conv_bwd_pre(proj, dxc, convw, cb_row, "conv_bwd_pre")
    dxbc = conv_bwd_in(dpre, convw, "conv_bwd_in")
    dproj = jnp.concatenate([dgates, dz, dxbc, du, ddt.astype(BF16), jnp.zeros((t, INP - P_DT - 128), BF16)], axis=1)
    dwinp, (x_wap, x_cw) = mm(h2, dproj, ta=True, out_dtype=BF16, i_outer=True, name="w_in_dw",
                              comm=[("xc", _row_parts(dwap)), ("xc", _col_parts(st_cv[0:CONV_K]))], **_TA)
    dh2, (x_win,) = mm(dproj, winp.T, tk=5376, name="w_in_dx", comm=[("xc", _col_parts(_unpad_w_in(dwinp)))])
    dx1, df1, st2 = mod_bwd(x1, dh2, dx2, norm_mix, mods, 4, fprev=f1, gk=2, gscale=0.5, name="mod2_bwd")
    dh1, x_f1i, x_f1o = _ffn_bwd(df1, h1, ab1, act1, wf1i.T, wf1o.T, "ffn1")
    gx, st1 = mod_bwd(xt, dh1, dx1, norm_ffn1, mods, 1, name="mod1_bwd")

    dmods = jnp.concatenate([st1[0], st1[1], st2[3], st2[0], st2[1], st3[3], st3[0], st3[1], st_fin[1]])
    part = {"b_ada": dmods, "norm_ffn1": st1[2], "norm_mix": st2[2], "conv_b": st_cv[4], "dt_bias": st_ssd[1, 0:NH],
            "a_log": st_ssd[0, 0:NH], "d_ssd": st_ssd[2, 0:NH], "ssd_norm_w": st_sn[0], "s5_lambda_re": g_lre,
            "s5_lambda_im": g_lim, "s5_b_re": g_bre, "s5_b_im": g_bim, "s5_c_re": g_cre, "s5_c_im": g_cim,
            "s5_d": dd_row, "s5_log_dt": g_ldt, "norm_ffn2": st3[2], "norm_final": st_fin[0],
            "loss": (0.5 / D) * jnp.sum(st_fin[2])}
    zero = {"loss": jnp.zeros((1,), F32)}
    gath = all_gather(_pack(part), "ag_small")
    sg, sd, sm, sv = adamw(_pack({**small, **zero}), gath, _pack({**{k: args["m_" + k] for k in small}, **zero}),
                           _pack({**{k: args["v_" + k] for k in small}, **zero}), "adamw_small")
    sg, sd, sm, sv = _unpack(sg), _unpack(sd), _unpack(sm), _unpack(sv)
    loss = sg["loss"][0]

    dm_loc = lax.dynamic_slice(gath.reshape(N_DEV, -1)[:, 0:N_ADA * D], (0, me * (N_ADA * D // N_DEV)),
                               (N_DEV, N_ADA * D // N_DEV))
    g_ada = ada_bwd(c8.T, dm_loc, "ada_bwd")
    big = {"w_ada": g_ada[None], "w_ffn1_in": x_f1i, "w_ffn1_out": x_f1o, "w_in": x_win, "conv_w": x_cw,
           "w_a_proj": x_wap, "w_b_glu": x_wbg, "w_out": x_wo, "w_ffn2_in": x_f2i, "w_ffn2_out": x_f2o}
    res = {}
    for k, parts in big.items():
        res[k] = adamw(args[k][0], parts, args["m_" + k][0], args["v_" + k][0], "adamw_" + k)

    names = ["w_ada", "b_ada", "norm_ffn1", "w_ffn1_in", "w_ffn1_out", "norm_mix", "w_in", "conv_w", "conv_b", "dt_bias",
             "a_log", "d_ssd", "ssd_norm_w", "w_a_proj", "s5_lambda_re", "s5_lambda_im", "s5_b_re", "s5_b_im", "s5_c_re",
             "s5_c_im", "s5_d", "s5_log_dt", "w_b_glu", "w_out", "norm_ffn2", "w_ffn2_in", "w_ffn2_out", "norm_final"]
    outs = [loss, gx[None]]
    for q, src in enumerate((sg, sd, sm, sv)):
        for k in names:
            if k in res:
                outs.append(res[k][q][None])
            else:
                outs.append(src[k].reshape(args[k].shape))
    return tuple(outs)
```

```python
import functools
import math

import jax
import jax.numpy as jnp
from jax import lax
from jax.experimental import pallas as pl
from jax.experimental.pallas import tpu as pltpu

F32 = jnp.float32
BF16 = jnp.bfloat16
HI = lax.Precision.HIGHEST

N_DEV = 8
D = 2048
D_FF = 5504
D_FFP = 5632
NH = 32
HP = 64
NG = 4
NST = 128
LCH = 128
CONV_DIM = 3072
CONV_K = 4
S5W = 1024
S5NS = 8192
N_ADA = 9
EPS = 1e-6
IN_COLS = 10272
INP = 10752
P_GATES, P_Z, P_XBC, P_U, P_DT = 0, 4096, 6144, 9216, 10240
O_XBC, O_DT, O_U, O_GA, O_GB = 2048, 5120, 5152, 6176, 8224
NEG = -1e30
VMEM_LIMIT = 56 * 1024 * 1024

ADAM_LR, ADAM_B1, ADAM_B2, ADAM_EPS, ADAM_WD, ADAM_STEP = 0.001, 0.9, 0.999, 1e-08, 0.01, 10


def _cp(*sem):
    return pltpu.CompilerParams(dimension_semantics=sem, vmem_limit_bytes=VMEM_LIMIT)


def _tile(dim, pref):
    if dim <= pref or dim % pref == 0:
        return min(dim, pref)
    for t in (2048, 1024, 512, 256, 128):
        if t <= pref and dim % t == 0:
            return t
    return dim


def _vec(w, cb=0):
    return pl.BlockSpec((1, w), lambda *_: (0, cb))


def _row(tm, w, cb=0):
    return pl.BlockSpec((tm, w), lambda i: (i, cb))


def _stats(w):
    return pl.BlockSpec((8, w), lambda *_: (0, 0))


def _sigmoid(x):
    return 1.0 / (1.0 + jnp.exp(-x))


def _softplus(x):
    return jnp.maximum(x, 0.0) + jnp.log1p(jnp.exp(-jnp.abs(x)))


def _peer(k):
    x, y, c = lax.axis_index("x"), lax.axis_index("y"), lax.axis_index("c")
    return (x ^ ((k >> 2) & 1), y ^ ((k >> 1) & 1), c ^ (k & 1))


def _my_id():
    return 4 * lax.axis_index("x") + 2 * lax.axis_index("y") + lax.axis_index("c")


def _comm_out_shape(kind, v):
    return jax.ShapeDtypeStruct(((N_DEV,) + v.shape) if kind == "ag" else v.shape, v.dtype)


def _comm_scratch(n):
    return [pltpu.SemaphoreType.DMA((n * N_DEV,)), pltpu.SemaphoreType.DMA((n * N_DEV,))]


class _Comm:
    def __init__(self, kinds, srcs, dsts, send_sems, recv_sems):
        self.items = list(zip(kinds, srcs, dsts))
        self.send_sems, self.recv_sems = send_sems, recv_sems
        x, y, c = lax.axis_index("x"), lax.axis_index("y"), lax.axis_index("c")
        self.me = 4 * x + 2 * y + c
        self.sib = (x, y, 1 - c)
        self.chips = [(1 - x, y), (x, 1 - y), (1 - x, 1 - y)]
        self.c = c

    @staticmethod
    def _id(p):
        return 4 * p[0] + 2 * p[1] + p[2]

    def _push(self, q, k, src, slot, to):
        return pltpu.make_async_remote_copy(
            src_ref=src, dst_ref=self.items[q][2].at[slot], send_sem=self.send_sems.at[q * N_DEV + k],
            recv_sem=self.recv_sems.at[q * N_DEV + k], device_id=to, device_id_type=pl.DeviceIdType.MESH)

    def _local(self, q):
        kind, src, dst = self.items[q]
        return pltpu.make_async_copy(src if kind == "ag" else src.at[self.me], dst.at[self.me],
                                     self.send_sems.at[q * N_DEV])

    def _direct(self, q):
        kind, src, dst = self.items[q]
        if kind == "xc":
            out = []
            for k in range(1, N_DEV):
                p = _peer(k)
                out.append((k, self._push(q, k, src.at[self._id(p)], self.me, p)))
            return out
        out = [(1, self._push(q, 1, src, self.me, self.sib))]
        for j, chip in enumerate(self.chips):
            out.append((2 + j, self._push(q, 2 + j, src, self.me, (*chip, self.c))))
        return out

    def _forwards(self, q):
        dst = self.items[q][2]
        out = []
        for j, chip in enumerate(self.chips):
            slot = self._id((*chip, self.c))
            out.append((2 + j, 5 + j, self._push(q, 5 + j, dst.at[slot], slot, self.sib)))
        return out

    def start(self):
        for q in range(len(self.items)):
            self._local(q).start()
            for _, cp in self._direct(q):
                cp.start()

    def forward(self):
        for q, (kind, _, dst) in enumerate(self.items):
            if kind != "ag":
                continue
            for k_in, _, fwd in self._forwards(q):
                self._push(q, k_in, dst.at[self.me], self.me, self.sib).wait_recv()
                fwd.start()

    def finish(self):
        for q, (kind, _, dst) in enumerate(self.items):
            self._local(q).wait()
            if kind == "xc":
                for _, cp in self._direct(q):
                    cp.wait()
                continue
            for k, cp in self._direct(q):
                cp.wait_send()
                if k == 1:
                    cp.wait_recv()
            for _, _, fwd in self._forwards(q):
                fwd.wait()


def comm_call(items, name):
    kinds = [k for k, _ in items]
    n = len(items)

    def body(*refs):
        cm = _Comm(kinds, refs[:n], refs[n:2 * n], refs[2 * n], refs[2 * n + 1])
        cm.start()
        cm.forward()
        cm.finish()

    return pl.pallas_call(
        body, name=name,
        in_specs=[pl.BlockSpec(memory_space=pl.ANY)] * n, out_specs=[pl.BlockSpec(memory_space=pl.ANY)] * n,
        out_shape=[_comm_out_shape(k, v) for k, v in items], scratch_shapes=_comm_scratch(n),
    )(*[v for _, v in items])


def all_gather(v, name):
    return comm_call([("ag", v)], name)[0]


def _pcall(body, args, *, name, grid, in_specs, out_specs, out_shape, scratch_shapes=(), sem, comm=()):
    nc, n_in, n_out = len(comm), len(in_specs), len(out_shape)
    if not nc:
        return pl.pallas_call(body, name=name, grid=grid, in_specs=list(in_specs), out_specs=list(out_specs),
                              out_shape=list(out_shape), scratch_shapes=list(scratch_shapes),
                              compiler_params=_cp(*sem))(*args)
    kinds = [k for k, _ in comm]
    steps = math.prod(grid)
    fwd_step = (3 * steps) // 5

    def carried(*refs):
        ins, csrc = refs[:n_in], refs[n_in:n_in + nc]
        outs, cdst = refs[n_in + nc:n_in + nc + n_out], refs[n_in + nc + n_out:n_in + 2 * nc + n_out]
        scr = refs[n_in + 2 * nc + n_out:]
        cm = _Comm(kinds, csrc, cdst, scr[-2], scr[-1])
        step = 0
        for d, g in enumerate(grid):
            step = step * g + pl.program_id(d)

        @pl.when(step == 0)
        def _():
            cm.start()

        body(*ins, *outs, *scr[:-2])

        @pl.when(step == fwd_step)
        def _():
            cm.forward()

        @pl.when(step == steps - 1)
        def _():
            cm.finish()

    hbm = pl.BlockSpec(memory_space=pl.ANY)
    out = pl.pallas_call(
        carried, name=name, grid=grid, in_specs=list(in_specs) + [hbm] * nc, out_specs=list(out_specs) + [hbm] * nc,
        out_shape=list(out_shape) + [_comm_out_shape(k, v) for k, v in comm],
        scratch_shapes=list(scratch_shapes) + _comm_scratch(nc), compiler_params=_cp(*(("arbitrary",) * len(grid))),
    )(*args, *[v for _, v in comm])
    return list(out[:n_out]), list(out[n_out:])


def mm(a, b, *, ta=False, out_dtype=F32, tm=512, tn=1024, tk=2048, i_outer=False, a_halves=False, b_halves=False,
       name, comm=()):
    if a_halves:
        m, kd = a.shape[1], 2 * a.shape[2]
    elif ta:
        kd, m = a.shape
    else:
        m, kd = a.shape
    kd2, n = (b.shape[1], 2 * b.shape[2]) if b_halves else b.shape
    assert kd == kd2 and not (ta and a_halves), (a.shape, b.shape, ta)
    tm, tn, tk = _tile(m, tm), _tile(n // 2 if b_halves else n, tn), _tile(kd // 2 if a_halves else kd, tk)
    nk = kd // tk
    nkh, njh = nk // 2, n // tn // 2
    grid = (m // tm, n // tn, nk) if i_outer else (n // tn, m // tm, nk)
    dims = (((0,) if ta else (1,), (0,)), ((), ()))

    def ix(f):
        return (lambda i, j, k: f(i, j, k)) if i_outer else (lambda j, i, k: f(i, j, k))

    def body(a_ref, b_ref, o_ref, *scr):
        p = lax.dot_general(a_ref[...], b_ref[...], dims, preferred_element_type=F32)
        if nk == 1:
            o_ref[...] = p.astype(o_ref.dtype)
        else:
            acc = scr[0]
            k = pl.program_id(2)

            @pl.when(k == 0)
            def _():
                acc[...] = p

            @pl.when(k > 0)
            def _():
                acc[...] += p

            @pl.when(k == nk - 1)
            def _():
                o_ref[...] = acc[...].astype(o_ref.dtype)

    if a_halves:
        a_spec = pl.BlockSpec((None, tm, tk), ix(lambda i, j, k: (k // nkh, i, k % nkh)))
    elif ta:
        a_spec = pl.BlockSpec((tk, tm), ix(lambda i, j, k: (k, i)))
    else:
        a_spec = pl.BlockSpec((tm, tk), ix(lambda i, j, k: (i, k)))
    if b_halves:
        b_spec = pl.BlockSpec((None, tk, tn), ix(lambda i, j, k: (j // njh, k, j % njh)))
    else:
        b_spec = pl.BlockSpec((tk, tn), ix(lambda i, j, k: (k, j)))
    out = _pcall(body, (a, b), name=name, grid=grid, in_specs=[a_spec, b_spec],
                 out_specs=[pl.BlockSpec((tm, tn), ix(lambda i, j, k: (i, j)))],
                 out_shape=[jax.ShapeDtypeStruct((m, n), out_dtype)],
                 scratch_shapes=[pltpu.VMEM((tm, tn), F32)] if nk > 1 else [],
                 sem=("parallel", "parallel", "arbitrary"), comm=comm)
    return (out[0][0], out[1]) if comm else out[0]


def ffn_in_act(h, w, name, comm=()):
    t = h.shape[0]
    tm, tn = _tile(t, 512), 512
    nj = D_FFP // tn

    def body(h_ref, wa_ref, wb_ref, ab_ref, act_ref):
        hv = h_ref[...]
        pa = _dot(hv, wa_ref[...])
        pb = _dot(hv, wb_ref[...])
        ab_ref[0] = pa.astype(BF16)
        ab_ref[1] = pb.astype(BF16)
        act_ref[...] = (pa * _sigmoid(pa) * pb).astype(BF16)

    out = _pcall(body, (h, w, w), name=name, grid=(nj, t // tm),
                 in_specs=[pl.BlockSpec((tm, D), lambda j, i: (i, 0)), pl.BlockSpec((D, tn), lambda j, i: (0, j)),
                           pl.BlockSpec((D, tn), lambda j, i: (0, nj + j))],
                 out_specs=[pl.BlockSpec((2, tm, tn), lambda j, i: (0, i, j)), pl.BlockSpec((tm, tn), lambda j, i: (i, j))],
                 out_shape=[jax.ShapeDtypeStruct((2, t, D_FFP), BF16), jax.ShapeDtypeStruct((t, D_FFP), BF16)],
                 sem=("parallel", "parallel"), comm=comm)
    return (out[0][0], out[0][1], out[1]) if comm else (out[0], out[1])


def ffn_dab(df, w_out_t, ab, name):
    t = df.shape[0]
    tm, tn = _tile(t, 1024), 512

    def body(d_ref, w_ref, ab_ref, o_ref):
        dv = _dot(d_ref[...], w_ref[...])
        a = ab_ref[0].astype(F32)
        b = ab_ref[1].astype(F32)
        s = _sigmoid(a)
        o_ref[0] = (dv * b * (s * (1.0 + a * (1.0 - s)))).astype(BF16)
        o_ref[1] = (dv * (a * s)).astype(BF16)

    return _pcall(body, (df, w_out_t, ab), name=name, grid=(D_FFP // tn, t // tm),
                  in_specs=[pl.BlockSpec((tm, D), lambda j, i: (i, 0)), pl.BlockSpec((D, tn), lambda j, i: (0, j)),
                            pl.BlockSpec((2, tm, tn), lambda j, i: (0, i, j))],
                  out_specs=[pl.BlockSpec((2, tm, tn), lambda j, i: (0, i, j))],
                  out_shape=[jax.ShapeDtypeStruct((2, t, D_FFP), BF16)], sem=("parallel", "parallel"))[0]


def ada_fwd(c8, w_loc, b_loc, name):
    n = w_loc.shape[1]
    tn = 256

    def body(c_ref, w_ref, b_ref, o_ref):
        cv = c_ref[...]
        ca = cv * _sigmoid(cv)
        o_ref[...] = jnp.dot(ca, w_ref[...], precision=HI, preferred_element_type=F32) + b_ref[...]

    return pl.pallas_call(
        body, name=name, grid=(n // tn,),
        in_specs=[pl.BlockSpec((N_DEV, D), lambda j: (0, 0)), pl.BlockSpec((D, tn), lambda j: (0, j)),
                  pl.BlockSpec((1, tn), lambda j: (0, j))],
        out_specs=pl.BlockSpec((N_DEV, tn), lambda j: (0, j)),
        out_shape=jax.ShapeDtypeStruct((N_DEV, n), F32), compiler_params=_cp("parallel"),
    )(c8, w_loc, b_loc)


def ada_bwd(c8t, dm_loc, name):
    n = dm_loc.shape[1]
    tn = 256

    def body(c_ref, d_ref, o_ref):
        cv = c_ref[...]
        ca = cv * _sigmoid(cv)
        o_ref[...] = jnp.dot(ca, d_ref[...], precision=HI, preferred_element_type=F32)

    return pl.pallas_call(
        body, name=name, grid=(n // tn,),
        in_specs=[pl.BlockSpec((D, N_DEV), lambda j: (0, 0)), pl.BlockSpec((N_DEV, tn), lambda j: (0, j))],
        out_specs=pl.BlockSpec((D, tn), lambda j: (0, j)),
        out_shape=jax.ShapeDtypeStruct((D, n), F32), compiler_params=_cp("parallel"),
    )(c8t, dm_loc)


def mod_fwd(x, nw, mods, shk, sck, *, f=None, gk=None, gscale=1.0, name):
    t = x.shape[0]
    tm = min(256, t)
    res = f is not None

    def body(*refs):
        if res:
            x_ref, f_ref, g_ref, nw_ref, sh_ref, sc_ref, x1_ref, h_ref = refs
            xv = x_ref[...] + (gscale * g_ref[...]) * f_ref[...]
            x1_ref[...] = xv
        else:
            x_ref, nw_ref, sh_ref, sc_ref, h_ref = refs
            xv = x_ref[...]
        r = lax.rsqrt(jnp.mean(xv * xv, axis=-1, keepdims=True) + EPS)
        h_ref[...] = ((xv * r * nw_ref[...]) * (1.0 + sc_ref[...]) + sh_ref[...]).astype(BF16)

    ins = [x] + ([f, mods] if res else []) + [nw, mods, mods]
    specs = [_row(tm, D)] + ([_row(tm, D), _vec(D, gk)] if res else []) + [_vec(D), _vec(D, shk), _vec(D, sck)]
    outs = ([jax.ShapeDtypeStruct((t, D), F32)] if res else []) + [jax.ShapeDtypeStruct((t, D), BF16)]
    ospecs = ([_row(tm, D)] if res else []) + [_row(tm, D)]
    out = pl.pallas_call(body, name=name, grid=(t // tm,), in_specs=specs, out_specs=ospecs, out_shape=outs,
                         compiler_params=_cp("parallel"))(*ins)
    return out if res else out[0]


def final_fwd_bwd(x2, f3, mods, nf, tgt, name):
    t = x2.shape[0]
    tm = min(128, t)

    def body(x_ref, f_ref, g_ref, nf_ref, t_ref, dx_ref, df_ref, st_ref):
        @pl.when(pl.program_id(0) == 0)
        def _():
            st_ref[...] = jnp.zeros_like(st_ref)

        g = 0.5 * g_ref[...]
        fv = f_ref[...]
        xv = x_ref[...] + g * fv
        r = lax.rsqrt(jnp.mean(xv * xv, axis=-1, keepdims=True) + EPS)
        xh = xv * r
        nfv = nf_ref[...]
        e = xh * nfv - t_ref[...]
        st_ref[2:3, :] += jnp.sum(e * e, axis=0, keepdims=True)
        dy = e * (1.0 / D)
        st_ref[0:1, :] += jnp.sum(dy * xh, axis=0, keepdims=True)
        dxh = dy * nfv
        dx = r * (dxh - xh * jnp.mean(dxh * xh, axis=-1, keepdims=True))
        dx_ref[...] = dx
        df_ref[...] = (g * dx).astype(BF16)
        st_ref[1:2, :] += 0.5 * jnp.sum(fv * dx, axis=0, keepdims=True)

    return pl.pallas_call(
        body, name=name, grid=(t // tm,),
        in_specs=[_row(tm, D), _row(tm, D), _vec(D, 8), _vec(D), _row(tm, D)],
        out_specs=[_row(tm, D), _row(tm, D), _stats(D)],
        out_shape=[jax.ShapeDtypeStruct((t, D), F32), jax.ShapeDtypeStruct((t, D), BF16),
                   jax.ShapeDtypeStruct((8, D), F32)],
        compiler_params=_cp("arbitrary"),
    )(x2, f3, mods, nf, tgt)


def mod_bwd(x_in, dh, dx_out, nw, mods, sck, *, fprev=None, gk=None, gscale=1.0, name):
    t = x_in.shape[0]
    tm = min(128, t)
    gate = fprev is not None

    def body(*refs):
        if gate:
            x_ref, dh_ref, dxo_ref, nw_ref, sc_ref, f_ref, g_ref, dx_ref, df_ref, st_ref = refs
        else:
            x_ref, dh_ref, dxo_ref, nw_ref, sc_ref, dx_ref, st_ref = refs

        @pl.when(pl.program_id(0) == 0)
        def _():
            st_ref[...] = jnp.zeros_like(st_ref)

        xv = x_ref[...]
        dhv = dh_ref[...]
        r = lax.rsqrt(jnp.mean(xv * xv, axis=-1, keepdims=True) + EPS)
        xh = xv * r
        nwv = nw_ref[...]
        st_ref[0:1, :] += jnp.sum(dhv, axis=0, keepdims=True)
        st_ref[1:2, :] += jnp.sum(dhv * (xh * nwv), axis=0, keepdims=True)
        dn = dhv * (1.0 + sc_ref[...])
        st_ref[2:3, :] += jnp.sum(dn * xh, axis=0, keepdims=True)
        dxh = dn * nwv
        dx = dxo_ref[...] + r * (dxh - xh * jnp.mean(dxh * xh, axis=-1, keepdims=True))
        dx_ref[...] = dx
        if gate:
            df_ref[...] = ((gscale * g_ref[...]) * dx).astype(BF16)
            st_ref[3:4, :] += gscale * jnp.sum(f_ref[...] * dx, axis=0, keepdims=True)

    ins = [x_in, dh, dx_out, nw, mods] + ([fprev, mods] if gate else [])
    specs = [_row(tm, D), _row(tm, D), _row(tm, D), _vec(D), _vec(D, sck)] + ([_row(tm, D), _vec(D, gk)] if gate else [])
    outs = [jax.ShapeDtypeStruct((t, D), F32)] + ([jax.ShapeDtypeStruct((t, D), BF16)] if gate else []) + \
        [jax.ShapeDtypeStruct((8, D), F32)]
    ospecs = [_row(tm, D)] + ([_row(tm, D)] if gate else []) + [_stats(D)]
    return pl.pallas_call(body, name=name, grid=(t // tm,), in_specs=specs, out_specs=ospecs, out_shape=outs,
                          compiler_params=_cp("arbitrary"))(*ins)


def _conv_pre(cur, prev8, w, b, tm):
    full = jnp.concatenate([prev8, cur], axis=0)
    pre = b + w[3:4, :] * cur
    for k in range(CONV_K - 1):
        s = CONV_K - 1 - k
        pre = pre + w[k:k + 1, :] * pltpu.roll(full, s, 0)[8:8 + tm, :]
    return pre


def conv_fwd(proj, cw_full, cb_full, name):
    t = proj.shape[0]
    tm = min(256, t)
    cwid = 1024
    cb0 = P_XBC // cwid

    def body(x_ref, p_ref, w_ref, b_ref, o_ref):
        i = pl.program_id(1)
        prev8 = jnp.where(i == 0, 0.0, p_ref[...].astype(F32)[8:16])
        pre = _conv_pre(x_ref[...].astype(F32), prev8, w_ref[...], b_ref[...], tm)
        o_ref[...] = pre * _sigmoid(pre)

    return pl.pallas_call(
        body, name=name, grid=(CONV_DIM // cwid, t // tm),
        in_specs=[pl.BlockSpec((tm, cwid), lambda j, i: (i, cb0 + j)),
                  pl.BlockSpec((16, cwid), lambda j, i: (jnp.maximum(i * (tm // 16) - 1, 0), cb0 + j)),
                  pl.BlockSpec((CONV_K, cwid), lambda j, i: (0, j)), pl.BlockSpec((1, cwid), lambda j, i: (0, j))],
        out_specs=pl.BlockSpec((tm, cwid), lambda j, i: (i, j)),
        out_shape=jax.ShapeDtypeStruct((t, CONV_DIM), F32), compiler_params=_cp("parallel", "parallel"),
    )(proj, proj, cw_full, cb_full)


def conv_bwd_pre(proj, dxc, cw_full, cb_full, name):
    t = proj.shape[0]
    tm = min(256, t)
    cwid = 1024
    cb0 = P_XBC // cwid

    def body(x_ref, p_ref, d_ref, w_ref, b_ref, o_ref, st_ref):
        i = pl.program_id(1)

        @pl.when(i == 0)
        def _():
            st_ref[...] = jnp.zeros_like(st_ref)

        cur = x_ref[...].astype(F32)
        prev8 = jnp.where(i == 0, 0.0, p_ref[...].astype(F32)[8:16])
        pre = _conv_pre(cur, prev8, w_ref[...], b_ref[...], tm)
        s = _sigmoid(pre)
        dpre = d_ref[...] * (s * (1.0 + pre * (1.0 - s)))
        o_ref[...] = dpre
        st_ref[4:5, :] += jnp.sum(dpre, axis=0, keepdims=True)
        st_ref[3:4, :] += jnp.sum(dpre * cur, axis=0, keepdims=True)
        full = jnp.concatenate([prev8, cur], axis=0)
        for k in range(CONV_K - 1):
            sft = CONV_K - 1 - k
            st_ref[k:k + 1, :] += jnp.sum(dpre * pltpu.roll(full, sft, 0)[8:8 + tm, :], axis=0, keepdims=True)

    return pl.pallas_call(
        body, name=name, grid=(CONV_DIM // cwid, t // tm),
        in_specs=[pl.BlockSpec((tm, cwid), lambda j, i: (i, cb0 + j)),
                  pl.BlockSpec((16, cwid), lambda j, i: (jnp.maximum(i * (tm // 16) - 1, 0), cb0 + j)),
                  pl.BlockSpec((tm, cwid), lambda j, i: (i, j)),
                  pl.BlockSpec((CONV_K, cwid), lambda j, i: (0, j)), pl.BlockSpec((1, cwid), lambda j, i: (0, j))],
        out_specs=[pl.BlockSpec((tm, cwid), lambda j, i: (i, j)), pl.BlockSpec((8, cwid), lambda j, i: (0, j))],
        out_shape=[jax.ShapeDtypeStruct((t, CONV_DIM), F32), jax.ShapeDtypeStruct((8, CONV_DIM), F32)],
        compiler_params=_cp("parallel", "arbitrary"),
    )(proj, proj, dxc, cw_full, cb_full)


def conv_bwd_in(dpre, cw_full, name):
    t = dpre.shape[0]
    tm = min(256, t)
    cwid = 1024
    nt = t // tm

    def body(d_ref, n_ref, w_ref, o_ref):
        i = pl.program_id(1)
        cur = d_ref[...]
        nxt = jnp.where(i == nt - 1, 0.0, n_ref[...])
        full = jnp.concatenate([cur, nxt], axis=0)
        w = w_ref[...]
        acc = w[3:4, :] * cur
        for k in range(CONV_K - 1):
            s = CONV_K - 1 - k
            acc = acc + w[k:k + 1, :] * pltpu.roll(full, tm + 8 - s, 0)[0:tm, :]
        o_ref[...] = acc.astype(BF16)

    return pl.pallas_call(
        body, name=name, grid=(CONV_DIM // cwid, nt),
        in_specs=[pl.BlockSpec((tm, cwid), lambda j, i: (i, j)),
                  pl.BlockSpec((8, cwid), lambda j, i: (jnp.minimum((i + 1) * (tm // 8), t // 8 - 1), j)),
                  pl.BlockSpec((CONV_K, cwid), lambda j, i: (0, j))],
        out_specs=pl.BlockSpec((tm, cwid), lambda j, i: (i, j)),
        out_shape=jax.ShapeDtypeStruct((t, CONV_DIM), BF16), compiler_params=_cp("parallel", "parallel"),
    )(dpre, dpre, cw_full)


def _nt(a, b):
    return lax.dot_general(a, b, (((1,), (1,)), ((), ())), preferred_element_type=F32)


def _dot(a, b):
    return jnp.dot(a, b, preferred_element_type=F32)


def _head_lanes():
    return lax.broadcasted_iota(jnp.int32, (1, 128), 1) < NH


def _expand_heads(x, e3):
    x = jnp.where(_head_lanes(), x, 0.0)
    hi = x.astype(BF16).astype(F32)
    r1 = x - hi
    mid = r1.astype(BF16).astype(F32)
    packed = hi + pltpu.roll(mid, NH, 1) + pltpu.roll(r1 - mid, 2 * NH, 1)
    return _dot(packed.astype(BF16), e3)


def _reduce_heads(v, e3):
    hi = v.astype(BF16)
    lo = (v - hi.astype(F32)).astype(BF16)
    return jnp.where(_head_lanes(), _nt(hi, e3) + _nt(lo, e3), 0.0)


def _ssd_common(dt_ref, dtb_ref, al_ref, exp_ref, tri_ref):
    a_row = jnp.where(_head_lanes(), -jnp.exp(al_ref[...]), 0.0)
    zraw = dt_ref[...] + dtb_ref[...]
    dtv = _softplus(zraw)
    cs = jnp.dot(tri_ref[...], dtv * a_row, precision=HI, preferred_element_type=F32)
    e3 = exp_ref[...]
    return a_row, zraw, dtv, cs, _expand_heads(cs, e3), _expand_heads(dtv, e3)


def ssd_fwd(xc, proj, dtb_row, alog_row, dx_row, expm, tri, name):
    t = xc.shape[0]
    nc = t // LCH

    def body(xs_ref, bm_ref, cm_ref, dt_ref, dtb_ref, al_ref, dxr_ref, exp_ref, tri_ref, y_ref, hs_ref, h_scr):
        @pl.when(pl.program_id(0) == 0)
        def _():
            h_scr[...] = jnp.zeros_like(h_scr)

        _, _, _, cs, csx, dtx = _ssd_common(dt_ref, dtb_ref, al_ref, exp_ref, tri_ref)
        cst = cs.T
        csl = csx[LCH - 1:LCH, :]
        xs = xs_ref[...]
        xd = xs * dtx
        xdw = xd * jnp.exp(csl - csx)
        ecs = jnp.exp(csx)
        ecl = jnp.exp(csl)
        tril = lax.broadcasted_iota(jnp.int32, (LCH, LCH), 0) >= lax.broadcasted_iota(jnp.int32, (LCH, LCH), 1)
        hs_ref[...] = h_scr[...]
        for g in range(NG):
            gc = slice(g * 512, (g + 1) * 512)
            bm = bm_ref[:, g * NST:(g + 1) * NST]
            cmb = cm_ref[:, g * NST:(g + 1) * NST].astype(BF16)
            gm = _nt(cmb, bm.astype(BF16))
            hg = h_scr[:, gc]
            yo = _dot(cmb, hg.astype(BF16)) * ecs[:, gc]
            st = _dot(bm.T.astype(BF16), xdw[:, gc].astype(BF16))
            for r in range(8):
                h = g * 8 + r
                hc = slice(h * HP, (h + 1) * HP)
                seg = cs[:, h:h + 1] - cst[h:h + 1, :]
                m = (gm * jnp.exp(jnp.where(tril, seg, NEG))).astype(BF16)
                yd = _dot(m, xd[:, hc].astype(BF16))
                y_ref[:, hc] = yd + yo[:, r * HP:(r + 1) * HP] + dxr_ref[:, hc] * xs[:, hc]
            h_scr[:, gc] = ecl[:, gc] * hg + st

    return pl.pallas_call(
        body, name=name, grid=(nc,),
        in_specs=[pl.BlockSpec((LCH, 2048), lambda c: (c, 0)), pl.BlockSpec((LCH, 512), lambda c: (c, 4)),
                  pl.BlockSpec((LCH, 512), lambda c: (c, 5)), pl.BlockSpec((LCH, 128), lambda c: (c, 0)),
                  _vec(128), _vec(128), _vec(2048), pl.BlockSpec((128, 2048), lambda c: (0, 0)),
                  pl.BlockSpec((LCH, LCH), lambda c: (0, 0))],
        out_specs=[pl.BlockSpec((LCH, 2048), lambda c: (c, 0)), pl.BlockSpec((None, NST, 2048), lambda c: (c, 0, 0))],
        out_shape=[jax.ShapeDtypeStruct((t, 2048), F32), jax.ShapeDtypeStruct((nc, NST, 2048), F32)],
        scratch_shapes=[pltpu.VMEM((NST, 2048), F32)],
        compiler_params=_cp("arbitrary"),
    )(xc, xc, xc, proj, dtb_row, alog_row, dx_row, expm, tri)


def ssd_bwd(xc, proj, hsave, dy, dtb_row, alog_row, dx_row, expm, tri, name):
    t = xc.shape[0]
    nc = t // LCH

    def body(xs_ref, bm_ref, cm_ref, dt_ref, hs_ref, dy_ref, dtb_ref, al_ref, dxr_ref, exp_ref, tri_ref,
             dxc_ref, ddt_ref, st_ref, dh_scr, dxd_scr, dcsx_scr):
        @pl.when(pl.program_id(0) == 0)
        def _():
            dh_scr[...] = jnp.zeros_like(dh_scr)
            st_ref[...] = jnp.zeros_like(st_ref)

        a_row, zraw, dtv, cs, csx, dtx = _ssd_common(dt_ref, dtb_ref, al_ref, exp_ref, tri_ref)
        e = exp_ref[...]
        cst = cs.T
        csl = csx[LCH - 1:LCH, :]
        xs = xs_ref[...]
        xd = xs * dtx
        wend = jnp.exp(csl - csx)
        xdw = xd * wend
        ecs = jnp.exp(csx)
        ecl = jnp.exp(csl)
        ri = lax.broadcasted_iota(jnp.int32, (LCH, LCH), 0)
        ci = lax.broadcasted_iota(jnp.int32, (LCH, LCH), 1)
        tril = ri >= ci
        triu = ri <= ci
        lane = lax.broadcasted_iota(jnp.int32, (1, 128), 1)
        dyv = dy_ref[...]
        dxr = dxr_ref[...]
        st_ref[2:3, :] += _reduce_heads(jnp.sum(dyv * xs, axis=0, keepdims=True), e)
        dcs = jnp.zeros((LCH, 128), F32)
        for g in range(NG):
            gc = slice(g * 512, (g + 1) * 512)
            bmb = bm_ref[:, g * NST:(g + 1) * NST].astype(BF16)
            cm = cm_ref[:, g * NST:(g + 1) * NST]
            cmb = cm.astype(BF16)
            hg = hs_ref[:, gc]
            hgb = hg.astype(BF16)
            dhc = dh_scr[:, gc]
            dhcb = dhc.astype(BF16)
            dyg = dyv[:, gc]
            yo = _dot(cmb, hgb) * ecs[:, gc]
            dq = (dyg * ecs[:, gc]).astype(BF16)
            dcm = _nt(dq, hgb)
            dh_yo = _dot(cm.T.astype(BF16), dq)
            dxdw = _dot(bmb, dhcb)
            dbm = _nt(xdw[:, gc].astype(BF16), dhcb)
            tt = dxdw * xdw[:, gc]
            dcsx_g = dyg * yo - tt
            dcsl_g = jnp.sum(tt, axis=0, keepdims=True) + jnp.sum(dhc * hg, axis=0, keepdims=True) * ecl[:, gc]
            dxd_scr[:, gc] = dxdw * wend[:, gc]
            dh_scr[:, gc] = ecl[:, gc] * dhc + dh_yo
            gm = _nt(cmb, bmb)
            gmt = _nt(bmb, cmb)
            dg = jnp.zeros((LCH, LCH), F32)
            dgt = jnp.zeros((LCH, LCH), F32)
            for r in range(8):
                h = g * 8 + r
                hc = slice(h * HP, (h + 1) * HP)
                seg = cs[:, h:h + 1] - cst[h:h + 1, :]
                lm = jnp.exp(jnp.where(tril, seg, NEG))
                lmt = jnp.exp(jnp.where(triu, -seg, NEG))
                mm_ = gm * lm
                mmt = gmt * lmt
                xdh = xd[:, hc].astype(BF16)
                dyh = dyv[:, hc].astype(BF16)
                dm = _nt(dyh, xdh)
                dmt = _nt(xdh, dyh)
                dxd_scr[:, hc] += _dot(mmt.astype(BF16), dyh)
                rs = jnp.sum(dm * mm_, axis=1, keepdims=True) - jnp.sum(dmt * mmt, axis=1, keepdims=True)
                dcs = dcs + rs * jnp.where(lane == h, 1.0, 0.0)
                dg = dg + dm * lm
                dgt = dgt + dmt * lmt
            dcm = dcm + _dot(dg.astype(BF16), bmb)
            dbm = dbm + _dot(dgt.astype(BF16), cmb)
            dxc_ref[:, 2048 + g * NST:2048 + (g + 1) * NST] = dbm
            dxc_ref[:, 2560 + g * NST:2560 + (g + 1) * NST] = dcm
            dcsx_scr[:, gc] = dcsx_g
            dcsx_scr[LCH - 1:LCH, gc] += dcsl_g
        dxd = dxd_scr[...]
        dxc_ref[:, 0:2048] = dxr * dyv + dxd * dtx
        ddtv = _reduce_heads(dxd * xs, e)
        dcs = dcs + _reduce_heads(dcsx_scr[...], e)
        dda =lax.dot_general(tri_ref[...], dcs, (((0,), (0,)), ((), ())), precision=HI, preferred_element_type=F32)
        ddtv = ddtv + dda * a_row
        st_ref[0:1, :] += jnp.sum(dda * dtv, axis=0, keepdims=True) * a_row
        ddt = ddtv * _sigmoid(zraw)
        ddt_ref[...] = ddt
        st_ref[1:2, :] += jnp.sum(ddt, axis=0, keepdims=True)

    rc = lambda c: nc - 1 - c
    return pl.pallas_call(
        body, name=name, grid=(nc,),
        in_specs=[pl.BlockSpec((LCH, 2048), lambda c: (rc(c), 0)), pl.BlockSpec((LCH, 512), lambda c: (rc(c), 4)),
                  pl.BlockSpec((LCH, 512), lambda c: (rc(c), 5)),
                  pl.BlockSpec((LCH, 128), lambda c: (rc(c), 0)),
                  pl.BlockSpec((None, NST, 2048), lambda c: (rc(c), 0, 0)),
                  pl.BlockSpec((LCH, 2048), lambda c: (rc(c), 0)),
                  _vec(128), _vec(128), _vec(2048), pl.BlockSpec((128, 2048), lambda c: (0, 0)),
                  pl.BlockSpec((LCH, LCH), lambda c: (0, 0))],
        out_specs=[pl.BlockSpec((LCH, CONV_DIM), lambda c: (rc(c), 0)), pl.BlockSpec((LCH, 128), lambda c: (rc(c), 0)),
                   _stats(128)],
        out_shape=[jax.ShapeDtypeStruct((t, CONV_DIM), F32), jax.ShapeDtypeStruct((t, 128), F32),
                   jax.ShapeDtypeStruct((8, 128), F32)],
        scratch_shapes=[pltpu.VMEM((NST, 2048), F32), pltpu.VMEM((LCH, 2048), F32), pltpu.VMEM((LCH, 2048), F32)],
        compiler_params=_cp("arbitrary"),
    )(xc, xc, xc, proj, hsave, dy, dtb_row, alog_row, dx_row, expm, tri)


def ssd_out_fwd(y, proj, nw, name):
    t = y.shape[0]
    tm = min(256, t)

    def body(y_ref, z_ref, nw_ref, o_ref):
        for g in range(NG):
            gc = slice(g * 512, (g + 1) * 512)
            z = z_ref[:, gc].astype(F32)
            yz = y_ref[:, gc] * (z * _sigmoid(z))
            r = lax.rsqrt(jnp.mean(yz * yz, axis=-1, keepdims=True) + EPS)
            o_ref[:, gc] = (yz * r * nw_ref[:, gc]).astype(BF16)

    return pl.pallas_call(body, name=name, grid=(t // tm,),
                          in_specs=[_row(tm, 2048), _row(tm, 2048, P_Z // 2048), _vec(2048)],
                          out_specs=_row(tm, 2048), out_shape=jax.ShapeDtypeStruct((t, 2048), BF16),
                          compiler_params=_cp("parallel"))(y, proj, nw)


def ssd_out_bwd(y, proj, dya, nw, name):
    t = y.shape[0]
    tm = min(256, t)

    def body(y_ref, z_ref, d_ref, nw_ref, dy_ref, dz_ref, st_ref):
        @pl.when(pl.program_id(0) == 0)
        def _():
            st_ref[...] = jnp.zeros_like(st_ref)

        for g in range(NG):
            gc = slice(g * 512, (g + 1) * 512)
            z = z_ref[:, gc].astype(F32)
            yv = y_ref[:, gc]
            s = _sigmoid(z)
            sz = z * s
            yz = yv * sz
            r = lax.rsqrt(jnp.mean(yz * yz, axis=-1, keepdims=True) + EPS)
            yzn = yz * r
            dv = d_ref[:, gc]
            st_ref[0:1, gc] += jnp.sum(dv * yzn, axis=0, keepdims=True)
            dyn = dv * nw_ref[:, gc]
            dyz = r * (dyn - yzn * jnp.mean(dyn * yzn, axis=-1, keepdims=True))
            dy_ref[:, gc] = dyz * sz
            dz_ref[:, gc] = (dyz * yv * (s * (1.0 + z * (1.0 - s)))).astype(BF16)

    return pl.pallas_call(
        body, name=name, grid=(t // tm,),
        in_specs=[_row(tm, 2048), _row(tm, 2048, P_Z // 2048), _row(tm, 2048), _vec(2048)],
        out_specs=[_row(tm, 2048), _row(tm, 2048), _stats(2048)],
        out_shape=[jax.ShapeDtypeStruct((t, 2048), F32), jax.ShapeDtypeStruct((t, 2048), BF16),
                   jax.ShapeDtypeStruct((8, 2048), F32)],
        compiler_params=_cp("arbitrary"))(y, proj, dya, nw)


def _cstep(ar, ai, sr, si, br, bi):
    return ar * sr - ai * si + br, ar * si + ai * sr + bi


def _halves(v):
    return (v[0:8, 0:512], v[0:8, 512:1024]), (v[8:16, 0:512], v[8:16, 512:1024])


def _slab(r1, i1, r2, i2):
    return jnp.concatenate([jnp.concatenate([r1, i1], axis=1), jnp.concatenate([r2, i2], axis=1)], axis=0).astype(BF16)


def _local_ends(x_ref, nslab, ar, ai, sr_scr, si_scr, end_ref, first, last, reverse):
    @pl.when(first)
    def _():
        sr_scr[...] = jnp.zeros_like(sr_scr)
        si_scr[...] = jnp.zeros_like(si_scr)

    def step(k, carry):
        s1, s2 = _halves(x_ref[nslab - 1 - k if reverse else k].astype(F32))
        if reverse:
            s1, s2 = s2, s1
        return _cstep(ar, ai, *_cstep(ar, ai, carry[0], carry[1], *s1), *s2)

    sr, si = lax.fori_loop(0, nslab, step, (sr_scr[...], si_scr[...]), unroll=4)
    sr_scr[...] = sr
    si_scr[...] = si

    @pl.when(last)
    def _():
        end_ref[:, 0:512] = sr
        end_ref[:, 512:1024] = si


def s5_in(u, bsg, a_re, a_im, name):
    t = u.shape[0]
    tm = min(512, t)
    nt = t // tm

    def body(u_ref, b_ref, ar_ref, ai_ref, o_ref, e_ref, sr_scr, si_scr):
        i = pl.program_id(1)
        o_ref[...] = _dot(u_ref[...].astype(BF16), b_ref[...]).astype(BF16).reshape(tm // 16, 16, 1024)
        _local_ends(o_ref, tm // 16, ar_ref[...], ai_ref[...], sr_scr, si_scr, e_ref, i == 0, i == nt - 1, False)

    return pl.pallas_call(
        body, name=name, grid=(8, nt),
        in_specs=[pl.BlockSpec((tm, 128), lambda s, i: (i, s)), pl.BlockSpec((None, 128, 1024), lambda s, i: (s, 0, 0)),
                  pl.BlockSpec((None, 8, 512), lambda s, i: (s, 0, 0)), pl.BlockSpec((None, 8, 512), lambda s, i: (s, 0, 0))],
        out_specs=[pl.BlockSpec((tm // 16, 16, 1024), lambda s, i: (i, 0, s)),
                   pl.BlockSpec((None, 8, 1024), lambda s, i: (s, 0, 0))],
        out_shape=[jax.ShapeDtypeStruct((t // 16, 16, S5NS), BF16), jax.ShapeDtypeStruct((8, 8, 1024), F32)],
        scratch_shapes=[pltpu.VMEM((8, 512), F32), pltpu.VMEM((8, 512), F32)],
        compiler_params=_cp("parallel", "arbitrary"))(u, bsg, a_re, a_im)


def s5_out(s, csg, u, d_row, name):
    t = u.shape[0]
    tm = min(512, t)

    def body(s_ref, c_ref, u_ref, d_ref, o_ref):
        o_ref[...] = _dot(s_ref[...].astype(BF16), c_ref[...]) + d_ref[...] * u_ref[...].astype(F32)

    return pl.pallas_call(
        body, name=name, grid=(8, t // tm),
        in_specs=[pl.BlockSpec((tm, 1024), lambda s, i: (i, s)), pl.BlockSpec((None, 1024, 128), lambda s, i: (s, 0, 0)),
                  pl.BlockSpec((tm, 128), lambda s, i: (i, s)), pl.BlockSpec((1, 128), lambda s, i: (0, s))],
        out_specs=pl.BlockSpec((tm, 128), lambda s, i: (i, s)),
        out_shape=jax.ShapeDtypeStruct((t, S5W), F32), compiler_params=_cp("parallel", "parallel"))(s, csg, u, d_row)


def s5_out_bwd(dy, csg, s, a_re, a_im, name):
    t = dy.shape[0]
    tm = min(512, t)
    nt = t // tm

    def body(dy_ref, c_ref, s_ref, ar_ref, ai_ref, e_ref, dc_ref, end_ref, sr_scr, si_scr):
        i = pl.program_id(1)

        @pl.when(i == 0)
        def _():
            dc_ref[...] = jnp.zeros_like(dc_ref)

        dyb = dy_ref[...].astype(BF16)
        e_ref[...] = _nt(dyb, c_ref[...]).astype(BF16).reshape(tm // 16, 16, 1024)
        dc_ref[...] += lax.dot_general(s_ref[...], dyb, (((0,), (0,)), ((), ())), preferred_element_type=F32)
        _local_ends(e_ref, tm // 16, ar_ref[...], -ai_ref[...], sr_scr, si_scr, end_ref, i == 0, i == nt - 1, True)

    rv = lambda i: nt - 1 - i
    return pl.pallas_call(
        body, name=name, grid=(8, nt),
        in_specs=[pl.BlockSpec((tm, 128), lambda s, i: (rv(i), s)), pl.BlockSpec((None, 1024, 128), lambda s, i: (s, 0, 0)),
                  pl.BlockSpec((tm, 1024), lambda s, i: (rv(i), s)),
                  pl.BlockSpec((None, 8, 512), lambda s, i: (s, 0, 0)), pl.BlockSpec((None, 8, 512), lambda s, i: (s, 0, 0))],
        out_specs=[pl.BlockSpec((tm // 16, 16, 1024), lambda s, i: (rv(i), 0, s)),
                   pl.BlockSpec((None, 1024, 128), lambda s, i: (s, 0, 0)),
                   pl.BlockSpec((None, 8, 1024), lambda s, i: (s, 0, 0))],
        out_shape=[jax.ShapeDtypeStruct((t // 16, 16, S5NS), BF16), jax.ShapeDtypeStruct((8, 1024, 128), F32),
                   jax.ShapeDtypeStruct((8, 8, 1024), F32)],
        scratch_shapes=[pltpu.VMEM((8, 512), F32), pltpu.VMEM((8, 512), F32)],
        compiler_params=_cp("parallel", "arbitrary"))(dy, csg, s, a_re, a_im)


def s5_in_bwd(lam, bsg, u, dy, d_row, name):
    t = u.shape[0]
    tm = min(512, t)

    def body(l_ref, b_ref, u_ref, dy_ref, d_ref, du_ref, db_ref, dd_ref):
        @pl.when(pl.program_id(1) == 0)
        def _():
            db_ref[...] = jnp.zeros_like(db_ref)
            dd_ref[...] = jnp.zeros_like(dd_ref)

        lb = l_ref[...].astype(BF16)
        uv = u_ref[...].astype(F32)
        dyv = dy_ref[...]
        du_ref[...] = _nt(lb, b_ref[...]) + d_ref[...] * dyv
        db_ref[...] += lax.dot_general(uv.astype(BF16), lb, (((0,), (0,)), ((), ())), preferred_element_type=F32)
        dd_ref[...] += jnp.sum(dyv * uv, axis=0, keepdims=True)

    return pl.pallas_call(
        body, name=name, grid=(8, t // tm),
        in_specs=[pl.BlockSpec((tm, 1024), lambda s, i: (i, s)), pl.BlockSpec((None, 128, 1024), lambda s, i: (s, 0, 0)),
                  pl.BlockSpec((tm, 128), lambda s, i: (i, s)), pl.BlockSpec((tm, 128), lambda s, i: (i, s)),
                  pl.BlockSpec((1, 128), lambda s, i: (0, s))],
        out_specs=[pl.BlockSpec((tm, 128), lambda s, i: (i, s)), pl.BlockSpec((None, 128, 1024), lambda s, i: (s, 0, 0)),
                   pl.BlockSpec((1, 128), lambda s, i: (0, s))],
        out_shape=[jax.ShapeDtypeStruct((t, S5W), F32), jax.ShapeDtypeStruct((8, 128, 1024), F32),
                   jax.ShapeDtypeStruct((1, S5W), F32)],
        compiler_params=_cp("parallel", "arbitrary"))(lam, bsg, u, dy, d_row)


def s5_scan_init(ends, a_re, a_im, lseg, reverse, name):
    nsq = int(math.log2(lseg))
    assert 2 ** nsq == lseg
    sgn = -1.0 if reverse else 1.0
    order = list(range(7, -1, -1)) if reverse else list(range(8))

    def body(e_ref, ar_ref, ai_ref, o_ref):
        pr = ar_ref[0:1, :]
        pi = sgn * ai_ref[0:1, :]
        for _ in range(nsq):
            pr, pi = pr * pr - pi * pi, 2.0 * pr * pi
        prev_r = jnp.zeros((1, 512), F32)
        prev_i = jnp.zeros((1, 512), F32)
        j0 = order[0]
        o_ref[j0:j0 + 1, 0:512] = prev_r
        o_ref[j0:j0 + 1, 512:1024] = prev_i
        for idx in range(1, 8):
            j, jp = order[idx], order[idx - 1]
            prev_r, prev_i = _cstep(pr, pi, prev_r, prev_i, e_ref[jp:jp + 1, 0:512], e_ref[jp:jp + 1, 512:1024])
            o_ref[j:j + 1, 0:512] = prev_r
            o_ref[j:j + 1, 512:1024] = prev_i

    return pl.pallas_call(
        body, name=name, grid=(8,),
        in_specs=[pl.BlockSpec((None, 8, 1024), lambda s: (s, 0, 0)), pl.BlockSpec((None, 8, 512), lambda s: (s, 0, 0)),
                  pl.BlockSpec((None, 8, 512), lambda s: (s, 0, 0))],
        out_specs=pl.BlockSpec((None, 8, 1024), lambda s: (s, 0, 0)),
        out_shape=jax.ShapeDtypeStruct((8, 8, 1024), F32), compiler_params=_cp("parallel"))(ends, a_re, a_im)


def s5_scan_fwd(b3, init, a_re, a_im, name):
    nslab = b3.shape[0]
    ti = min(64, nslab)
    nb = nslab // ti

    def body(b_ref, i_ref, ar_ref, ai_ref, o_ref, sr_scr, si_scr):
        @pl.when(pl.program_id(1) == 0)
        def _():
            sr_scr[...] = i_ref[:, 0:512]
            si_scr[...] = i_ref[:, 512:1024]

        ar = ar_ref[...]
        ai = ai_ref[...]

        def step(k, carry):
            b1, b2 = _halves(b_ref[k].astype(F32))
            r1, i1 = _cstep(ar, ai, carry[0], carry[1], *b1)
            r2, i2 = _cstep(ar, ai, r1, i1, *b2)
            o_ref[k] = _slab(r1, i1, r2, i2)
            return r2, i2

        sr, si = lax.fori_loop(0, ti, step, (sr_scr[...], si_scr[...]), unroll=4)
        sr_scr[...] = sr
        si_scr[...] = si

    return pl.pallas_call(
        body, name=name, grid=(8, nb),
        in_specs=[pl.BlockSpec((ti, 16, 1024), lambda s, tb: (tb, 0, s)), pl.BlockSpec((None, 8, 1024), lambda s, tb: (s, 0, 0)),
                  pl.BlockSpec((None, 8, 512), lambda s, tb: (s, 0, 0)), pl.BlockSpec((None, 8, 512), lambda s, tb: (s, 0, 0))],
        out_specs=pl.BlockSpec((ti, 16, 1024), lambda s, tb: (tb, 0, s)),
        out_shape=jax.ShapeDtypeStruct(b3.shape, BF16),
        scratch_shapes=[pltpu.VMEM((8, 512), F32), pltpu.VMEM((8, 512), F32)],
        compiler_params=_cp("parallel", "arbitrary"))(b3, init, a_re, a_im)


def s5_scan_bwd(e3, linit, s3, sinit, a_re, a_im, name):
    nslab = e3.shape[0]
    ti = min(64, nslab)
    nb = nslab // ti

    def body(e_ref, li_ref, s_ref, sh_ref, si0_ref, ar_ref, ai_ref, o_ref, da_ref, lr_scr, lim_scr):
        tb = pl.program_id(1)

        @pl.when(tb == 0)
        def _():
            lr_scr[...] = li_ref[:, 0:512]
            lim_scr[...] = li_ref[:, 512:1024]
            da_ref[...] = jnp.zeros_like(da_ref)

        ar = ar_ref[...]
        ai = -ai_ref[...]

        def slab(kk, lr, li, dar, dai, sp):
            e1, e2 = _halves(e_ref[kk].astype(F32))
            s1, _ = _halves(s_ref[kk].astype(F32))
            r2, i2 = _cstep(ar, ai, lr, li, *e2)
            dar = dar + r2 * s1[0] + i2 * s1[1]
            dai = dai + i2 * s1[0] - r2 * s1[1]
            r1, i1 = _cstep(ar, ai, r2, i2, *e1)
            dar = dar + r1 * sp[0] + i1 * sp[1]
            dai = dai + i1 * sp[0] - r1 * sp[1]
            o_ref[kk] = _slab(r1, i1, r2, i2)
            return r1, i1, dar, dai

        def step(k, carry):
            kk = ti - 1 - k
            return slab(kk, *carry, _halves(s_ref[kk - 1].astype(F32))[1])

        z = jnp.zeros((8, 512), F32)
        lr, li, dar, dai = lax.fori_loop(0, ti - 1, step, (lr_scr[...], lim_scr[...], z, z), unroll=2)
        first = tb == nb - 1
        halo = _halves(sh_ref[0].astype(F32))[1]
        sp = (jnp.where(first, si0_ref[:, 0:512], halo[0]), jnp.where(first, si0_ref[:, 512:1024], halo[1]))
        lr, li, dar, dai = slab(0, lr, li, dar, dai, sp)
        lr_scr[...] = lr
        lim_scr[...] = li
        da_ref[:, 0:512] += dar
        da_ref[:, 512:1024] += dai

    rb = lambda tb: nb - 1 - tb
    return pl.pallas_call(
        body, name=name, grid=(8, nb),
        in_specs=[pl.BlockSpec((ti, 16, 1024), lambda s, tb: (rb(tb), 0, s)),
                  pl.BlockSpec((None, 8, 1024), lambda s, tb: (s, 0, 0)),
                  pl.BlockSpec((ti, 16, 1024), lambda s, tb: (rb(tb), 0, s)),
                  pl.BlockSpec((1, 16, 1024), lambda s, tb: (jnp.maximum(rb(tb) * ti - 1, 0), 0, s)),
                  pl.BlockSpec((None, 8, 1024), lambda s, tb: (s, 0, 0)),
                  pl.BlockSpec((None, 8, 512), lambda s, tb: (s, 0, 0)), pl.BlockSpec((None, 8, 512), lambda s, tb: (s, 0, 0))],
        out_specs=[pl.BlockSpec((ti, 16, 1024), lambda s, tb: (rb(tb), 0, s)),
                   pl.BlockSpec((None, 8, 1024), lambda s, tb: (s, 0, 0))],
        out_shape=[jax.ShapeDtypeStruct(e3.shape, BF16), jax.ShapeDtypeStruct((8, 8, 1024), F32)],
        scratch_shapes=[pltpu.VMEM((8, 512), F32), pltpu.VMEM((8, 512), F32)],
        compiler_params=_cp("parallel", "arbitrary"))(e3, linit, s3, s3, sinit, a_re, a_im)


_GC = math.sqrt(2.0 / math.pi)


def gelu_fwd(y, name):
    t, w = y.shape
    tm = min(512, t)

    def body(y_ref, o_ref):
        v = y_ref[...]
        o_ref[...] = (0.5 * v * (1.0 + jnp.tanh(_GC * (v + 0.044715 * v * v * v)))).astype(BF16)

    return pl.pallas_call(body, name=name, grid=(t // tm,), in_specs=[_row(tm, w)], out_specs=_row(tm, w),
                          out_shape=jax.ShapeDtypeStruct((t, w), BF16), compiler_params=_cp("parallel"))(y)


def gelu_bwd(y, dg, name):
    t, w = y.shape
    tm = min(512, t)

    def body(y_ref, d_ref, o_ref):
        v = y_ref[...]
        th = jnp.tanh(_GC * (v + 0.044715 * v * v * v))
        o_ref[...] = d_ref[...] * (0.5 * (1.0 + th) + 0.5 * v * (1.0 - th * th) * _GC * (1.0 + 3.0 * 0.044715 * v * v))

    return pl.pallas_call(body, name=name, grid=(t // tm,), in_specs=[_row(tm, w), _row(tm, w)], out_specs=_row(tm, w),
                          out_shape=jax.ShapeDtypeStruct((t, w), F32), compiler_params=_cp("parallel"))(y, dg)


def merge_fwd(proj, pa, glu, name):
    t = pa.shape[0]
    tm = min(256, t)

    def body(g_ref, pa_ref, glu_ref, o_ref):
        pb = glu_ref[:, 0:D].astype(F32) * _sigmoid(glu_ref[:, D:2 * D].astype(F32))
        o_ref[...] = (_sigmoid(g_ref[:, 0:D].astype(F32)) * pa_ref[...].astype(F32)
                      + _sigmoid(g_ref[:, D:2 * D].astype(F32)) * pb).astype(BF16)

    return pl.pallas_call(body, name=name, grid=(t // tm,), in_specs=[_row(tm, 2 * D), _row(tm, D), _row(tm, 2 * D)],
                          out_specs=_row(tm, D), out_shape=jax.ShapeDtypeStruct((t, D), BF16),
                          compiler_params=_cp("parallel"))(proj, pa, glu)


def merge_bwd(proj, pa, glu, dm, name):
    t = pa.shape[0]
    tm = min(256, t)

    def body(g_ref, pa_ref, glu_ref, dm_ref, dpa_ref, dglu_ref, dg_ref):
        dmv = dm_ref[...]
        pav = pa_ref[...].astype(F32)
        sa = _sigmoid(g_ref[:, 0:D].astype(F32))
        sb = _sigmoid(g_ref[:, D:2 * D].astype(F32))
        ga = glu_ref[:, 0:D].astype(F32)
        sg = _sigmoid(glu_ref[:, D:2 * D].astype(F32))
        pb = ga * sg
        dpb = sb * dmv
        dpa_ref[...] = (sa * dmv).astype(BF16)
        dglu_ref[:, 0:D] = (dpb * sg).astype(BF16)
        dglu_ref[:, D:2 * D] = (dpb * pb * (1.0 - sg)).astype(BF16)
        dg_ref[:, 0:D] = (dmv * pav * sa * (1.0 - sa)).astype(BF16)
        dg_ref[:, D:2 * D] = (dmv * pb * sb * (1.0 - sb)).astype(BF16)

    return pl.pallas_call(
        body, name=name, grid=(t // tm,),
        in_specs=[_row(tm, 2 * D), _row(tm, D), _row(tm, 2 * D), _row(tm, D)],
        out_specs=[_row(tm, D), _row(tm, 2 * D), _row(tm, 2 * D)],
        out_shape=[jax.ShapeDtypeStruct((t, D), BF16), jax.ShapeDtypeStruct((t, 2 * D), BF16),
                   jax.ShapeDtypeStruct((t, 2 * D), BF16)],
        compiler_params=_cp("parallel"))(proj, pa, glu, dm)


def adamw(w, parts, m, v, name):
    r, c = w.shape
    p = parts.shape[0]
    tr = r if r <= 128 else 128
    c1 = 1.0 - ADAM_B1 ** ADAM_STEP
    c2 = 1.0 - ADAM_B2 ** ADAM_STEP

    def body(w_ref, p_ref, m_ref, v_ref, g_ref, d_ref, nm_ref, nv_ref):
        g = p_ref[0].astype(F32)
        for k in range(1, p):
            g = g + p_ref[k].astype(F32)
        mn = ADAM_B1 * m_ref[...] + (1.0 - ADAM_B1) * g
        vn = ADAM_B2 * v_ref[...] + (1.0 - ADAM_B2) * (g * g)
        g_ref[...] = g
        nm_ref[...] = mn
        nv_ref[...] = vn
        d_ref[...] = -ADAM_LR * ((mn / c1) / (jnp.sqrt(vn / c2) + ADAM_EPS) + ADAM_WD * w_ref[...])

    spec = pl.BlockSpec((tr, c), lambda i: (i, 0))
    o = jax.ShapeDtypeStruct((r, c), F32)
    return pl.pallas_call(
        body, name=name, grid=(pl.cdiv(r, tr),),
        in_specs=[spec, pl.BlockSpec((p, tr, c), lambda i: (0, i, 0)), spec, spec],
        out_specs=[spec, spec, spec, spec], out_shape=[o, o, o, o], compiler_params=_cp("parallel"))(w, parts, m, v)


def _s5_discretise(lambda_re, lambda_im, log_dt, b_re, b_im):
    dt = jnp.exp(log_dt)[:, None]
    lr = jnp.minimum(lambda_re, -1e-4)
    li = lambda_im
    mag = jnp.exp(lr * dt)
    ar = mag * jnp.cos(li * dt)
    ai = mag * jnp.sin(li * dt)
    den = lr * lr + li * li
    nr = ar - 1.0
    kr = (nr * lr + ai * li) / den
    ki = (ai * lr - nr * li) / den
    bbar_re = kr[..., None] * b_re - ki[..., None] * b_im
    bbar_im = kr[..., None] * b_im + ki[..., None] * b_re
    return ar, ai, bbar_re, bbar_im


def _bsg_of(bb_re, bb_im):
    eye = jnp.eye(8, dtype=F32)
    f = lambda b: jnp.einsum("sgpi,gh->sgihp", b.reshape(8, 8, 64, 16), eye).reshape(8, 128, 512)
    return jnp.concatenate([f(bb_re), f(bb_im)], axis=2)


def _bsg_diag(dbsg):
    eye = jnp.eye(8, dtype=F32)
    f = lambda x: jnp.einsum("sgihp,gh->sgpi", x.reshape(8, 8, 16, 8, 64), eye).reshape(64, 64, 16)
    return f(dbsg[:, :, 0:512]), f(dbsg[:, :, 512:1024])


def _csg_of(c_re, c_im):
    eye = jnp.eye(8, dtype=F32)
    f = lambda c: jnp.einsum("sgip,gh->sgphi", c.reshape(8, 8, 16, 64), eye).reshape(8, 512, 128)
    return jnp.concatenate([f(c_re), -f(c_im)], axis=1)


def _csg_diag(dcsg):
    eye = jnp.eye(8, dtype=F32)
    f = lambda x: jnp.einsum("sgphi,gh->sgip", x.reshape(8, 8, 64, 8, 16), eye).reshape(64, 16, 64)
    return f(dcsg[:, 0:512, :]), -f(dcsg[:, 512:1024, :])


def _perm(a, t):
    return a.reshape(8, t // 8, a.shape[1]).transpose(1, 0, 2).reshape(t, a.shape[1])


def _unperm(a, t):
    return a.reshape(t // 8, 8, a.shape[1]).transpose(1, 0, 2).reshape(t, a.shape[1])


def _cols(g):
    return g.transpose(1, 0, 2).reshape(g.shape[1], N_DEV * g.shape[2])


def _rows(g):
    return g.reshape(N_DEV * g.shape[1], g.shape[2])


def _col_parts(g):
    r, c = g.shape
    return g.reshape(r, N_DEV, c // N_DEV).transpose(1, 0, 2)


def _row_parts(g):
    r, c = g.shape
    return g.reshape(N_DEV, r // N_DEV, c)


def _pad_ffn_in(w):
    z = jnp.zeros((D, D_FFP - D_FF), w.dtype)
    return jnp.concatenate([w[:, :D_FF], z, w[:, D_FF:], z], axis=1)


def _unpad_ffn_in(g):
    return jnp.concatenate([g[:, :D_FF], g[:, D_FFP:D_FFP + D_FF]], axis=1)


def _pad_w_in(w):
    z = jnp.zeros((D, INP - P_DT - NH), w.dtype)
    return jnp.concatenate([w[:, O_GA:O_GB], w[:, O_GB:IN_COLS], w[:, 0:O_XBC], w[:, O_XBC:O_DT], w[:, O_U:O_GA],
                            w[:, O_DT:O_U], z], axis=1)


def _unpad_w_in(g):
    return jnp.concatenate([g[:, P_Z:P_XBC], g[:, P_XBC:P_U], g[:, P_DT:P_DT + NH], g[:, P_U:P_DT],
                            g[:, 0:D], g[:, D:2 * D]], axis=1)


_PACK = (("b_ada", 18432), ("norm_ffn1", 2048), ("norm_mix", 2048), ("conv_b", 3072), ("dt_bias", 32), ("a_log", 32),
         ("d_ssd", 32), ("ssd_norm_w", 2048), ("s5_lambda_re", 4096), ("s5_lambda_im", 4096), ("s5_b_re", 65536),
         ("s5_b_im", 65536), ("s5_c_re", 65536), ("s5_c_im", 65536), ("s5_d", 1024), ("s5_log_dt", 64),
         ("norm_ffn2", 2048), ("norm_final", 2048), ("loss", 1))
_PACK_ROWS = 304
_PACK_W = 1024


def _pack(d):
    flat = jnp.concatenate([d[k].reshape(-1).astype(F32) for k, _ in _PACK])
    return jnp.pad(flat, (0, _PACK_ROWS * _PACK_W - flat.shape[0])).reshape(_PACK_ROWS, _PACK_W)


def _unpack(a):
    flat = a.reshape(-1)
    out, off = {}, 0
    for k, n in _PACK:
        out[k] = flat[off:off + n]
        off += n
    return out


_TA = dict(tm=512, tn=512, tk=8192)


def _ffn_bwd(df, h, ab, act, w_in_t, w_out_t, tag):
    dab = ffn_dab(df, w_out_t, ab, tag + "_dab")
    dw_out = mm(act, df, ta=True, out_dtype=BF16, i_outer=True, name=tag + "_dwout", **_TA)
    dw_in, (x_out,) = mm(h, dab, ta=True, b_halves=True, out_dtype=BF16, i_outer=True, name=tag + "_dwin",
                         comm=[("xc", _row_parts(dw_out[:D_FF]))], **_TA)
    dh, (x_in,) = mm(dab, w_in_t, a_halves=True, tk=5632, name=tag + "_dh",
                     comm=[("xc", _col_parts(_unpad_ffn_in(dw_in)))])
    return dh, x_in, x_out


def kernel(x, c, w_ada, b_ada, norm_ffn1, w_ffn1_in, w_ffn1_out, norm_mix, w_in, conv_w, conv_b, dt_bias, a_log, d_ssd, ssd_norm_w, w_a_proj, s5_lambda_re, s5_lambda_im, s5_b_re, s5_b_im, s5_c_re, s5_c_im, s5_d, s5_log_dt, w_b_glu, w_out, norm_ffn2, w_ffn2_in, w_ffn2_out, norm_final, loss_target, m_w_ada, m_b_ada, m_norm_ffn1, m_w_ffn1_in, m_w_ffn1_out, m_norm_mix, m_w_in, m_conv_w, m_conv_b, m_dt_bias, m_a_log, m_d_ssd, m_ssd_norm_w, m_w_a_proj, m_s5_lambda_re, m_s5_lambda_im, m_s5_b_re, m_s5_b_im, m_s5_c_re, m_s5_c_im, m_s5_d, m_s5_log_dt, m_w_b_glu, m_w_out, m_norm_ffn2, m_w_ffn2_in, m_w_ffn2_out, m_norm_final, v_w_ada, v_b_ada, v_norm_ffn1, v_w_ffn1_in, v_w_ffn1_out, v_norm_mix, v_w_in, v_conv_w, v_conv_b, v_dt_bias, v_a_log, v_d_ssd, v_ssd_norm_w, v_w_a_proj, v_s5_lambda_re, v_s5_lambda_im, v_s5_b_re, v_s5_b_im, v_s5_c_re, v_s5_c_im, v_s5_d, v_s5_log_dt, v_w_b_glu, v_w_out, v_norm_ffn2, v_w_ffn2_in, v_w_ffn2_out, v_norm_final):
    args = dict(locals())
    t = x.shape[1]
    me = _my_id()
    xt = x[0]
    tgt = loss_target[0]
    small = {k: args[k] for k, _ in _PACK if k != "loss"}

    bf = lambda w: w[0].astype(BF16)
    pad_rows = lambda w: jnp.pad(w, ((0, D_FFP - D_FF), (0, 0)))

    c8 = all_gather(c, "ag_c").reshape(N_DEV, D)
    b_loc = lax.dynamic_slice(b_ada, (0, me * (N_ADA * D // N_DEV)), (1, N_ADA * D // N_DEV))
    m8 = ada_fwd(c8, w_ada[0], b_loc, "ada_fwd")
    mods, g_f1i, g_cw = comm_call([("xc", m8.reshape(N_DEV, 1, -1)), ("ag", bf(w_ffn1_in)), ("ag", conv_w[0])],
                                  "xc_mods_ag_ffn1_in")
    mods = mods.reshape(1, N_ADA * D)
    convw = _cols(g_cw)
    wf1i = _pad_ffn_in(_cols(g_f1i))

    h1 = mod_fwd(xt, norm_ffn1, mods, 0, 1, name="mod1")
    ab1, act1, (g_f1o, g_win, g_wap) = ffn_in_act(
        h1, wf1i, "ffn1_in", comm=[("ag", bf(w_ffn1_out)), ("ag", bf(w_in)), ("ag", bf(w_a_proj))])
    wf1o, winp, wap = pad_rows(_rows(g_f1o)), _pad_w_in(_cols(g_win)), _rows(g_wap)
    f1, (g_wo, g_wbg) = mm(act1, wf1o, tk=5632, name="ffn1_out", comm=[("ag", bf(w_out)), ("ag", bf(w_b_glu))])
    wo, wbg = _rows(g_wo), _cols(g_wbg)
    x1, h2 = mod_fwd(xt, norm_mix, mods, 3, 4, f=f1, gk=2, gscale=0.5, name="mod2")
    proj, (g_f2i,) = mm(h2, winp, out_dtype=BF16, tm=1024, tn=512, i_outer=True, name="w_in",
                        comm=[("ag", bf(w_ffn2_in))])
    dtraw = mm(h2, winp[:, P_DT:P_DT + 128], tn=128, name="w_in_dt")
    wf2i = _pad_ffn_in(_cols(g_f2i))
    cb_row = conv_b
    xc = conv_fwd(proj, convw, cb_row, "conv_fwd")
    row128 = lambda v: jnp.pad(v.reshape(1, -1), ((0, 0), (0, 128 - v.size)))
    dtb_row, alog_row = row128(dt_bias), row128(a_log)
    dx_row = jnp.repeat(d_ssd.reshape(-1), HP).reshape(1, 2048)
    rows = jnp.arange(128)[:, None]
    expm = ((rows % NH == jnp.arange(2048)[None, :] // HP) & (rows < 3 * NH)).astype(BF16)
    tri = (jnp.arange(LCH)[:, None] >= jnp.arange(LCH)[None, :]).astype(F32)
    y_ssd, hsave = ssd_fwd(xc, dtraw, dtb_row, alog_row, dx_row, expm, tri, "ssd_fwd")
    ya = ssd_out_fwd(y_ssd, proj, ssd_norm_w, "ssd_out")
    pa = mm(ya, wap, out_dtype=BF16, name="w_a_proj")

    s5p = (s5_lambda_re[0], s5_lambda_im[0], s5_log_dt[0], s5_b_re[0], s5_b_im[0])
    (ar, ai, bb_re, bb_im), s5_vjp = jax.vjp(_s5_discretise, *s5p)
    a_re8 = jnp.broadcast_to(ar.reshape(8, 1, 512), (8, 8, 512))
    a_im8 = jnp.broadcast_to(ai.reshape(8, 1, 512), (8, 8, 512))
    bsg = _bsg_of(bb_re, bb_im).astype(BF16)
    csg = _csg_of(s5_c_re[0], s5_c_im[0]).astype(BF16)
    d_row = s5_d.reshape(1, S5W)
    lseg = t // 8
    u_p = _perm(proj[:, P_U:P_U + S5W], t)
    bu3, ends_f = s5_in(u_p, bsg, a_re8, a_im8, "s5_in")
    sinit = s5_scan_init(ends_f, a_re8, a_im8, lseg, False, "s5_init_f")
    s3 = s5_scan_fwd(bu3, sinit, a_re8, a_im8, "s5_scan_f")
    s2 = s3.reshape(t, S5NS)
    yb_p = s5_out(s2, csg, u_p, d_row, "s5_out")
    yb = _unperm(yb_p, t)
    gy = gelu_fwd(yb, "gelu")
    glu = mm(gy, wbg, out_dtype=BF16, name="w_b_glu")
    merged = merge_fwd(proj, pa, glu, "merge")
    o = mm(merged, wo, name="w_out")
    x2, h3 = mod_fwd(x1, norm_ffn2, mods, 6, 7, f=o, gk=5, gscale=1.0, name="mod3")
    ab3, act3, (g_f2o,) = ffn_in_act(h3, wf2i, "ffn2_in", comm=[("ag", bf(w_ffn2_out))])
    wf2o = pad_rows(_rows(g_f2o))
    f3 = mm(act3, wf2o, tk=5632, name="ffn2_out")

    dx3, df3, st_fin = final_fwd_bwd(x2, f3, mods, norm_final.reshape(1, D), tgt, "final")
    dh3, x_f2i, x_f2o = _ffn_bwd(df3, h3, ab3, act3, wf2i.T, wf2o.T, "ffn2")
    dx2, do, st3 = mod_bwd(x2, dh3, dx3, norm_ffn2, mods, 7, fprev=o, gk=5, gscale=1.0, name="mod3_bwd")

    dmerged = mm(do, wo.T, name="w_out_dx")
    dwo = mm(merged, do, ta=True, out_dtype=BF16, i_outer=True, name="w_out_dw", **_TA)
    dpa, dglu, dgates = merge_bwd(proj, pa, glu, dmerged, "merge_bwd")
    dwbg = mm(gy, dglu, ta=True, out_dtype=BF16, i_outer=True, name="w_b_glu_dw", **_TA)
    dgy, (x_wo,) = mm(dglu, wbg.T, name="w_b_glu_dx", comm=[("xc", _row_parts(dwo))])
    dyb_p = _perm(gelu_bwd(yb, dgy, "gelu_bwd"), t)
    e3, dcsg, ends_b = s5_out_bwd(dyb_p, csg, s2, a_re8, a_im8, "s5_out_bwd")
    linit = s5_scan_init(ends_b, a_re8, a_im8, lseg, True, "s5_init_b")
    lam3, da8 = s5_scan_bwd(e3, linit, s3, sinit, a_re8, a_im8, "s5_scan_b")
    du_p, dbsg, dd_row = s5_in_bwd(lam3.reshape(t, S5NS), bsg, u_p, dyb_p, d_row, "s5_in_bwd")
    du = _unperm(du_p, t).astype(BF16)
    da = jnp.sum(da8, axis=1)
    dbb_re, dbb_im = _bsg_diag(dbsg)
    g_lre, g_lim, g_ldt, g_bre, g_bim = s5_vjp((da[:, 0:512].reshape(64, 64), da[:, 512:1024].reshape(64, 64),
                                                dbb_re, dbb_im))
    g_cre, g_cim = _csg_diag(dcsg)

    dwap = mm(ya, dpa, ta=True, out_dtype=BF16, i_outer=True, name="w_a_proj_dw", **_TA)
    dya, (x_wbg,) = mm(dpa, wap.T, name="w_a_proj_dx", comm=[("xc", _col_parts(dwbg))])
    dy_ssd, dz, st_sn = ssd_out_bwd(y_ssd, proj, dya, ssd_norm_w, "ssd_out_bwd")
    dxc, ddt, st_ssd = ssd_bwd(xc, dtraw, hsave, dy_ssd, dtb_row, alog_row, dx_row, expm, tri, "ssd_bwd")
    dpre, st_cv = conv_bwd_pre(proj, dxc, convw, cb_row, "conv_bwd_pre")
    dxbc = conv_bwd_in(dpre, convw, "conv_bwd_in")
    dproj = jnp.concatenate([dgates, dz, dxbc, du, ddt.astype(BF16), jnp.zeros((t, INP - P_DT - 128), BF16)], axis=1)
    dwinp, (x_wap, x_cw) = mm(h2, dproj, ta=True, out_dtype=BF16, i_outer=True, name="w_in_dw",
                              comm=[("xc", _row_parts(dwap)), ("xc", _col_parts(st_cv[0:CONV_K]))], **_TA)
    dh2, (x_win,) = mm(dproj, winp.T, tk=5376, name="w_in_dx", comm=[("xc", _col_parts(_unpad_w_in(dwinp)))])
    dx1, df1, st2 = mod_bwd(x1, dh2, dx2, norm_mix, mods, 4, fprev=f1, gk=2, gscale=0.5, name="mod2_bwd")
    dh1, x_f1i, x_f1o = _ffn_bwd(df1, h1, ab1, act1, wf1i.T, wf1o.T, "ffn1")
    gx, st1 = mod_bwd(xt, dh1, dx1, norm_ffn1, mods, 1, name="mod1_bwd")

    dmods = jnp.concatenate([st1[0], st1[1], st2[3], st2[0], st2[1], st3[3], st3[0], st3[1], st_fin[1]])
    part = {"b_ada": dmods, "norm_ffn1": st1[2], "norm_mix": st2[2], "conv_b": st_cv[4], "dt_bias": st_ssd[1, 0:NH],
            "a_log": st_ssd[0, 0:NH], "d_ssd": st_ssd[2, 0:NH], "ssd_norm_w": st_sn[0], "s5_lambda_re": g_lre,
            "s5_lambda_im": g_lim, "s5_b_re": g_bre, "s5_b_im": g_bim, "s5_c_re": g_cre, "s5_c_im": g_cim,
            "s5_d": dd_row, "s5_log_dt": g_ldt, "norm_ffn2": st3[2], "norm_final": st_fin[0],
            "loss": (0.5 / D) * jnp.sum(st_fin[2])}
    zero = {"loss": jnp.zeros((1,), F32)}
    gath = all_gather(_pack(part), "ag_small")
    sg, sd, sm, sv = adamw(_pack({**small, **zero}), gath, _pack({**{k: args["m_" + k] for k in small}, **zero}),
                           _pack({**{k: args["v_" + k] for k in small}, **zero}), "adamw_small")
    sg, sd, sm, sv = _unpack(sg), _unpack(sd), _unpack(sm), _unpack(sv)
    loss = sg["loss"][0]

    dm_loc = lax.dynamic_slice(gath.reshape(N_DEV, -1)[:, 0:N_ADA * D], (0, me * (N_ADA * D // N_DEV)),
                               (N_DEV, N_ADA * D // N_DEV))
    g_ada = ada_bwd(c8.T, dm_loc, "ada_bwd")
    big = {"w_ada": g_ada[None], "w_ffn1_in": x_f1i, "w_ffn1_out": x_f1o, "w_in": x_win, "conv_w": x_cw,
           "w_a_proj": x_wap, "w_b_glu": x_wbg, "w_out": x_wo, "w_ffn2_in": x_f2i, "w_ffn2_out": x_f2o}
    res = {}
    for k, parts in big.items():
        res[k] = adamw(args[k][0], parts, args["m_" + k][0], args["v_" + k][0], "adamw_" + k)

    names = ["w_ada", "b_ada", "norm_ffn1", "w_ffn1_in", "w_ffn1_out", "norm_mix", "w_in", "conv_w", "conv_b", "dt_bias",
             "a_log", "d_ssd", "ssd_norm_w", "w_a_proj", "s5_lambda_re", "s5_lambda_im", "s5_b_re", "s5_b_im", "s5_c_re",
             "s5_c_im", "s5_d", "s5_log_dt", "w_b_glu", "w_out", "norm_ffn2", "w_ffn2_in", "w_ffn2_out", "norm_final"]
    outs = [loss, gx[None]]
    for q, src in enumerate((sg, sd, sm, sv)):
        for k in names:
            if k in res:
                outs.append(res[k][q][None])
            else:
                outs.append(src[k].reshape(args[k].shape))
    return tuple(outs)
```

```python
import functools
import math

import jax
import jax.numpy as jnp
from jax import lax
from jax.experimental import pallas as pl
from jax.experimental.pallas import tpu as pltpu

F32 = jnp.float32
BF16 = jnp.bfloat16
HI = lax.Precision.HIGHEST

N_DEV = 8
D = 2048
D_FF = 5504
D_FFP = 5632
NH = 32
HP = 64
NG = 4
NST = 128
LCH = 128
CONV_DIM = 3072
CONV_K = 4
S5W = 1024
S5NS = 8192
N_ADA = 9
EPS = 1e-6
IN_COLS = 10272
INP = 10752
P_GATES, P_Z, P_XBC, P_U, P_DT = 0, 4096, 6144, 9216, 10240
O_XBC, O_DT, O_U, O_GA, O_GB = 2048, 5120, 5152, 6176, 8224
NEG = -1e30
VMEM_LIMIT = 56 * 1024 * 1024

ADAM_LR, ADAM_B1, ADAM_B2, ADAM_EPS, ADAM_WD, ADAM_STEP = 0.001, 0.9, 0.999, 1e-08, 0.01, 10


def _cp(*sem):
    return pltpu.CompilerParams(dimension_semantics=sem, vmem_limit_bytes=VMEM_LIMIT)


def _tile(dim, pref):
    if dim <= pref or dim % pref == 0:
        return min(dim, pref)
    for t in (2048, 1024, 512, 256, 128):
        if t <= pref and dim % t == 0:
            return t
    return dim


def _vec(w, cb=0):
    return pl.BlockSpec((1, w), lambda *_: (0, cb))


def _row(tm, w, cb=0):
    return pl.BlockSpec((tm, w), lambda i: (i, cb))


def _stats(w):
    return pl.BlockSpec((8, w), lambda *_: (0, 0))


def _sigmoid(x):
    return 1.0 / (1.0 + jnp.exp(-x))


def _softplus(x):
    return jnp.maximum(x, 0.0) + jnp.log1p(jnp.exp(-jnp.abs(x)))


def _peer(k):
    x, y, c = lax.axis_index("x"), lax.axis_index("y"), lax.axis_index("c")
    return (x ^ ((k >> 2) & 1), y ^ ((k >> 1) & 1), c ^ (k & 1))


def _my_id():
    return 4 * lax.axis_index("x") + 2 * lax.axis_index("y") + lax.axis_index("c")


def _comm_out_shape(kind, v):
    shape = {"ag": (N_DEV,) + v.shape, "xc": v.shape, "agc": (v.shape[0], N_DEV * v.shape[1]),
             "xcc": (N_DEV, v.shape[0], v.shape[1] // N_DEV)}[kind]
    return jax.ShapeDtypeStruct(shape, v.dtype)


def _comm_scratch(n):
    return [pltpu.SemaphoreType.DMA((n * N_DEV,)), pltpu.SemaphoreType.DMA((n * N_DEV,))]


class _Comm:
    def __init__(self, kinds, srcs, dsts, send_sems, recv_sems):
        self.items = list(zip(kinds, srcs, dsts))
        self.send_sems, self.recv_sems = send_sems, recv_sems
        x, y, c = lax.axis_index("x"), lax.axis_index("y"), lax.axis_index("c")
        self.me = 4 * x + 2 * y + c
        self.sib = (x, y, 1 - c)
        self.chips = [(1 - x, y), (x, 1 - y), (1 - x, 1 - y)]
        self.c = c

    @staticmethod
    def _id(p):
        return 4 * p[0] + 2 * p[1] + p[2]

    def _src(self, q, d):
        kind, src, _ = self.items[q]
        if kind == "xc":
            return src.at[d]
        if kind == "xcc":
            w = src.shape[1] // N_DEV
            return src.at[:, pl.ds(pl.multiple_of(d * w, 128), w)]
        return src

    def _slot(self, q, d):
        kind, _, dst = self.items[q]
        if kind == "agc":
            w = dst.shape[1] // N_DEV
            return dst.at[:, pl.ds(pl.multiple_of(d * w, 128), w)]
        return dst.at[d]

    def _push(self, q, k, src, slot, to):
        return pltpu.make_async_remote_copy(
            src_ref=src, dst_ref=self._slot(q, slot), send_sem=self.send_sems.at[q * N_DEV + k],
            recv_sem=self.recv_sems.at[q * N_DEV + k], device_id=to, device_id_type=pl.DeviceIdType.MESH)

    def _local(self, q):
        return pltpu.make_async_copy(self._src(q, self.me), self._slot(q, self.me), self.send_sems.at[q * N_DEV])

    def _direct(self, q):
        kind = self.items[q][0]
        if kind in ("xc", "xcc"):
            out = []
            for k in range(1, N_DEV):
                p = _peer(k)
                out.append((k, self._push(q, k, self._src(q, self._id(p)), self.me, p)))
            return out
        src = self.items[q][1]
        out = [(1, self._push(q, 1, src, self.me, self.sib))]
        for j, chip in enumerate(self.chips):
            out.append((2 + j, self._push(q, 2 + j, src, self.me, (*chip, self.c))))
        return out

    def _forwards(self, q):
        out = []
        for j, chip in enumerate(self.chips):
            slot = self._id((*chip, self.c))
            out.append((2 + j, 5 + j, self._push(q, 5 + j, self._slot(q, slot), slot, self.sib)))
        return out

    def start(self):
        for q in range(len(self.items)):
            self._local(q).start()
            for _, cp in self._direct(q):
                cp.start()

    def forward(self):
        for q, (kind, _, _) in enumerate(self.items):
            if kind not in ("ag", "agc"):
                continue
            for k_in, _, fwd in self._forwards(q):
                self._push(q, k_in, self._slot(q, self.me), self.me, self.sib).wait_recv()
                fwd.start()

    def finish(self):
        for q, (kind, _, _) in enumerate(self.items):
            self._local(q).wait()
            if kind in ("xc", "xcc"):
                for _, cp in self._direct(q):
                    cp.wait()
                continue
            for k, cp in self._direct(q):
                cp.wait_send()
                if k == 1:
                    cp.wait_recv()
            for _, _, fwd in self._forwards(q):
                fwd.wait()


def comm_call(items, name):
    kinds = [k for k, _ in items]
    n = len(items)

    def body(*refs):
        cm = _Comm(kinds, refs[:n], refs[n:2 * n], refs[2 * n], refs[2 * n + 1])
        cm.start()
        cm.forward()
        cm.finish()

    return pl.pallas_call(
        body, name=name,
        in_specs=[pl.BlockSpec(memory_space=pl.ANY)] * n, out_specs=[pl.BlockSpec(memory_space=pl.ANY)] * n,
        out_shape=[_comm_out_shape(k, v) for k, v in items], scratch_shapes=_comm_scratch(n),
    )(*[v for _, v in items])


def all_gather(v, name):
    return comm_call([("ag", v)], name)[0]


def _pcall(body, args, *, name, grid, in_specs, out_specs, out_shape, scratch_shapes=(), sem, comm=(), fwd=0.6):
    nc, n_in, n_out = len(comm), len(in_specs), len(out_shape)
    if not nc:
        return pl.pallas_call(body, name=name, grid=grid, in_specs=list(in_specs), out_specs=list(out_specs),
                              out_shape=list(out_shape), scratch_shapes=list(scratch_shapes),
                              compiler_params=_cp(*sem))(*args)
    kinds = [k for k, _ in comm]
    steps = math.prod(grid)
    fwd_step = min(int(fwd * steps), steps - 1)

    def carried(*refs):
        ins, csrc = refs[:n_in], refs[n_in:n_in + nc]
        outs, cdst = refs[n_in + nc:n_in + nc + n_out], refs[n_in + nc + n_out:n_in + 2 * nc + n_out]
        scr = refs[n_in + 2 * nc + n_out:]
        cm = _Comm(kinds, csrc, cdst, scr[-2], scr[-1])
        step = 0
        for d, g in enumerate(grid):
            step = step * g + pl.program_id(d)

        @pl.when(step == 0)
        def _():
            cm.start()

        body(*ins, *outs, *scr[:-2])

        @pl.when(step == fwd_step)
        def _():
            cm.forward()

        @pl.when(step == steps - 1)
        def _():
            cm.finish()

    hbm = pl.BlockSpec(memory_space=pl.ANY)
    out = pl.pallas_call(
        carried, name=name, grid=grid, in_specs=list(in_specs) + [hbm] * nc, out_specs=list(out_specs) + [hbm] * nc,
        out_shape=list(out_shape) + [_comm_out_shape(k, v) for k, v in comm],
        scratch_shapes=list(scratch_shapes) + _comm_scratch(nc), compiler_params=_cp(*(("arbitrary",) * len(grid))),
    )(*args, *[v for _, v in comm])
    return list(out[:n_out]), list(out[n_out:])


def mm(a, b, *, ta=False, out_dtype=F32, tm=512, tn=1024, tk=2048, i_outer=False, a_halves=False, b_halves=False,
       name, comm=()):
    if a_halves:
        m, kd = a.shape[1], 2 * a.shape[2]
    elif ta:
        kd, m = a.shape
    else:
        m, kd = a.shape
    kd2, n = (b.shape[1], 2 * b.shape[2]) if b_halves else b.shape
    assert kd == kd2 and not (ta and a_halves), (a.shape, b.shape, ta)
    tm, tn, tk = _tile(m, tm), _tile(n // 2 if b_halves else n, tn), _tile(kd // 2 if a_halves else kd, tk)
    nk = kd // tk
    nkh, njh = nk // 2, n // tn // 2
    grid = (m // tm, n // tn, nk) if i_outer else (n // tn, m // tm, nk)
    dims = (((0,) if ta else (1,), (0,)), ((), ()))

    def ix(f):
        return (lambda i, j, k: f(i, j, k)) if i_outer else (lambda j, i, k: f(i, j, k))

    def body(a_ref, b_ref, o_ref, *scr):
        p = lax.dot_general(a_ref[...], b_ref[...], dims, preferred_element_type=F32)
        if nk == 1:
            o_ref[...] = p.astype(o_ref.dtype)
        else:
            acc = scr[0]
            k = pl.program_id(2)

            @pl.when(k == 0)
            def _():
                acc[...] = p

            @pl.when(k > 0)
            def _():
                acc[...] += p

            @pl.when(k == nk - 1)
            def _():
                o_ref[...] = acc[...].astype(o_ref.dtype)

    if a_halves:
        a_spec = pl.BlockSpec((None, tm, tk), ix(lambda i, j, k: (k // nkh, i, k % nkh)))
    elif ta:
        a_spec = pl.BlockSpec((tk, tm), ix(lambda i, j, k: (k, i)))
    else:
        a_spec = pl.BlockSpec((tm, tk), ix(lambda i, j, k: (i, k)))
    if b_halves:
        b_spec = pl.BlockSpec((None, tk, tn), ix(lambda i, j, k: (j // njh, k, j % njh)))
    else:
        b_spec = pl.BlockSpec((tk, tn), ix(lambda i, j, k: (k, j)))
    out = _pcall(body, (a, b), name=name, grid=grid, in_specs=[a_spec, b_spec],
                 out_specs=[pl.BlockSpec((tm, tn), ix(lambda i, j, k: (i, j)))],
                 out_shape=[jax.ShapeDtypeStruct((m, n), out_dtype)],
                 scratch_shapes=[pltpu.VMEM((tm, tn), F32)] if nk > 1 else [],
                 sem=("parallel", "parallel", "arbitrary"), comm=comm)
    return (out[0][0], out[1]) if comm else out[0]


def ffn_in_act(h, w, name, comm=(), fwd=0.6):
    t = h.shape[0]
    tm, tn = _tile(t, 512), 512
    nj = D_FFP // tn

    def body(h_ref, wa_ref, wb_ref, ab_ref, act_ref):
        hv = h_ref[...]
        pa = _dot(hv, wa_ref[...])
        pb = _dot(hv, wb_ref[...])
        ab_ref[0] = pa.astype(BF16)
        ab_ref[1] = pb.astype(BF16)
        act_ref[...] = (pa * _sigmoid(pa) * pb).astype(BF16)

    out = _pcall(body, (h, w, w), name=name, grid=(nj, t // tm),
                 in_specs=[pl.BlockSpec((tm, D), lambda j, i: (i, 0)), pl.BlockSpec((D, tn), lambda j, i: (0, j)),
                           pl.BlockSpec((D, tn), lambda j, i: (0, nj + j))],
                 out_specs=[pl.BlockSpec((2, tm, tn), lambda j, i: (0, i, j)), pl.BlockSpec((tm, tn), lambda j, i: (i, j))],
                 out_shape=[jax.ShapeDtypeStruct((2, t, D_FFP), BF16), jax.ShapeDtypeStruct((t, D_FFP), BF16)],
                 sem=("parallel", "parallel"), comm=comm, fwd=fwd)
    return (out[0][0], out[0][1], out[1]) if comm else (out[0], out[1])


def ffn_dab(df, w_out_t, ab, name):
    t = df.shape[0]
    tm, tn = _tile(t, 1024), 512

    def body(d_ref, w_ref, ab_ref, o_ref):
        dv = _dot(d_ref[...], w_ref[...])
        a = ab_ref[0].astype(F32)
        b = ab_ref[1].astype(F32)
        s = _sigmoid(a)
        o_ref[0] = (dv * b * (s * (1.0 + a * (1.0 - s)))).astype(BF16)
        o_ref[1] = (dv * (a * s)).astype(BF16)

    return _pcall(body, (df, w_out_t, ab), name=name, grid=(D_FFP // tn, t // tm),
                  in_specs=[pl.BlockSpec((tm, D), lambda j, i: (i, 0)), pl.BlockSpec((D, tn), lambda j, i: (0, j)),
                            pl.BlockSpec((2, tm, tn), lambda j, i: (0, i, j))],
                  out_specs=[pl.BlockSpec((2, tm, tn), lambda j, i: (0, i, j))],
                  out_shape=[jax.ShapeDtypeStruct((2, t, D_FFP), BF16)], sem=("parallel", "parallel"))[0]


def ada_fwd(c8, w_loc, b_loc, name):
    n = w_loc.shape[1]
    tn = 256

    def body(c_ref, w_ref, b_ref, o_ref):
        cv = c_ref[...]
        ca = cv * _sigmoid(cv)
        o_ref[...] = jnp.dot(ca, w_ref[...], precision=HI, preferred_element_type=F32) + b_ref[...]

    return pl.pallas_call(
        body, name=name, grid=(n // tn,),
        in_specs=[pl.BlockSpec((N_DEV, D), lambda j: (0, 0)), pl.BlockSpec((D, tn), lambda j: (0, j)),
                  pl.BlockSpec((1, tn), lambda j: (0, j))],
        out_specs=pl.BlockSpec((N_DEV, tn), lambda j: (0, j)),
        out_shape=jax.ShapeDtypeStruct((N_DEV, n), F32), compiler_params=_cp("parallel"),
    )(c8, w_loc, b_loc)


def ada_bwd(c8t, dm_loc, name):
    n = dm_loc.shape[1]
    tn = 256

    def body(c_ref, d_ref, o_ref):
        cv = c_ref[...]
        ca = cv * _sigmoid(cv)
        o_ref[...] = jnp.dot(ca, d_ref[...], precision=HI, preferred_element_type=F32)

    return pl.pallas_call(
        body, name=name, grid=(n // tn,),
        in_specs=[pl.BlockSpec((D, N_DEV), lambda j: (0, 0)), pl.BlockSpec((N_DEV, tn), lambda j: (0, j))],
        out_specs=pl.BlockSpec((D, tn), lambda j: (0, j)),
        out_shape=jax.ShapeDtypeStruct((D, n), F32), compiler_params=_cp("parallel"),
    )(c8t, dm_loc)


def mod_fwd(x, nw, mods, shk, sck, *, f=None, gk=None, gscale=1.0, name):
    t = x.shape[0]
    tm = min(256, t)
    res = f is not None

    def body(*refs):
        if res:
            x_ref, f_ref, g_ref, nw_ref, sh_ref, sc_ref, x1_ref, h_ref = refs
            xv = x_ref[...] + (gscale * g_ref[...]) * f_ref[...]
            x1_ref[...] = xv
        else:
            x_ref, nw_ref, sh_ref, sc_ref, h_ref = refs
            xv = x_ref[...]
        r = lax.rsqrt(jnp.mean(xv * xv, axis=-1, keepdims=True) + EPS)
        h_ref[...] = ((xv * r * nw_ref[...]) * (1.0 + sc_ref[...]) + sh_ref[...]).astype(BF16)

    ins = [x] + ([f, mods] if res else []) + [nw, mods, mods]
    specs = [_row(tm, D)] + ([_row(tm, D), _vec(D, gk)] if res else []) + [_vec(D), _vec(D, shk), _vec(D, sck)]
    outs = ([jax.ShapeDtypeStruct((t, D), F32)] if res else []) + [jax.ShapeDtypeStruct((t, D), BF16)]
    ospecs = ([_row(tm, D)] if res else []) + [_row(tm, D)]
    out = pl.pallas_call(body, name=name, grid=(t // tm,), in_specs=specs, out_specs=ospecs, out_shape=outs,
                         compiler_params=_cp("parallel"))(*ins)
    return out if res else out[0]


def final_fwd_bwd(x2, f3, mods, nf, tgt, name):
    t = x2.shape[0]
    tm = min(128, t)

    def body(x_ref, f_ref, g_ref, nf_ref, t_ref, dx_ref, df_ref, st_ref):
        @pl.when(pl.program_id(0) == 0)
        def _():
            st_ref[...] = jnp.zeros_like(st_ref)

        g = 0.5 * g_ref[...]
        fv = f_ref[...]
        xv = x_ref[...] + g * fv
        r = lax.rsqrt(jnp.mean(xv * xv, axis=-1, keepdims=True) + EPS)
        xh = xv * r
        nfv = nf_ref[...]
        e = xh * nfv - t_ref[...]
        st_ref[2:3, :] += jnp.sum(e * e, axis=0, keepdims=True)
        dy = e * (1.0 / D)
        st_ref[0:1, :] += jnp.sum(dy * xh, axis=0, keepdims=True)
        dxh = dy * nfv
        dx = r * (dxh - xh * jnp.mean(dxh * xh, axis=-1, keepdims=True))
        dx_ref[...] = dx
        df_ref[...] = (g * dx).astype(BF16)
        st_ref[1:2, :] += 0.5 * jnp.sum(fv * dx, axis=0, keepdims=True)

    return pl.pallas_call(
        body, name=name, grid=(t // tm,),
        in_specs=[_row(tm, D), _row(tm, D), _vec(D, 8), _vec(D), _row(tm, D)],
        out_specs=[_row(tm, D), _row(tm, D), _stats(D)],
        out_shape=[jax.ShapeDtypeStruct((t, D), F32), jax.ShapeDtypeStruct((t, D), BF16),
                   jax.ShapeDtypeStruct((8, D), F32)],
        compiler_params=_cp("arbitrary"),
    )(x2, f3, mods, nf, tgt)


def mod_bwd(x_in, dh, dx_out, nw, mods, sck, *, fprev=None, gk=None, gscale=1.0, name):
    t = x_in.shape[0]
    tm = min(128, t)
    gate = fprev is not None

    def body(*refs):
        if gate:
            x_ref, dh_ref, dxo_ref, nw_ref, sc_ref, f_ref, g_ref, dx_ref, df_ref, st_ref = refs
        else:
            x_ref, dh_ref, dxo_ref, nw_ref, sc_ref, dx_ref, st_ref = refs

        @pl.when(pl.program_id(0) == 0)
        def _():
            st_ref[...] = jnp.zeros_like(st_ref)

        xv = x_ref[...]
        dhv = dh_ref[...]
        r = lax.rsqrt(jnp.mean(xv * xv, axis=-1, keepdims=True) + EPS)
        xh = xv * r
        nwv = nw_ref[...]
        st_ref[0:1, :] += jnp.sum(dhv, axis=0, keepdims=True)
        st_ref[1:2, :] += jnp.sum(dhv * (xh * nwv), axis=0, keepdims=True)
        dn = dhv * (1.0 + sc_ref[...])
        st_ref[2:3, :] += jnp.sum(dn * xh, axis=0, keepdims=True)
        dxh = dn * nwv
        dx = dxo_ref[...] + r * (dxh - xh * jnp.mean(dxh * xh, axis=-1, keepdims=True))
        dx_ref[...] = dx
        if gate:
            df_ref[...] = ((gscale * g_ref[...]) * dx).astype(BF16)
            st_ref[3:4, :] += gscale * jnp.sum(f_ref[...] * dx, axis=0, keepdims=True)

    ins = [x_in, dh, dx_out, nw, mods] + ([fprev, mods] if gate else [])
    specs = [_row(tm, D), _row(tm, D), _row(tm, D), _vec(D), _vec(D, sck)] + ([_row(tm, D), _vec(D, gk)] if gate else [])
    outs = [jax.ShapeDtypeStruct((t, D), F32)] + ([jax.ShapeDtypeStruct((t, D), BF16)] if gate else []) + \
        [jax.ShapeDtypeStruct((8, D), F32)]
    ospecs = [_row(tm, D)] + ([_row(tm, D)] if gate else []) + [_stats(D)]
    return pl.pallas_call(body, name=name, grid=(t // tm,), in_specs=specs, out_specs=ospecs, out_shape=outs,
                          compiler_params=_cp("arbitrary"))(*ins)


def _conv_pre(cur, prev8, w, b, tm):
    full = jnp.concatenate([prev8, cur], axis=0)
    pre = b + w[3:4, :] * cur
    for k in range(CONV_K - 1):
        s = CONV_K - 1 - k
        pre = pre + w[k:k + 1, :] * pltpu.roll(full, s, 0)[8:8 + tm, :]
    return pre


def conv_fwd(proj, cw_full, cb_full, name):
    t = proj.shape[0]
    tm = min(256, t)
    cwid = 1024
    cb0 = P_XBC // cwid

    def body(x_ref, p_ref, w_ref, b_ref, o_ref):
        i = pl.program_id(1)
        prev8 = jnp.where(i == 0, 0.0, p_ref[...].astype(F32)[8:16])
        pre = _conv_pre(x_ref[...].astype(F32), prev8, w_ref[...], b_ref[...], tm)
        o_ref[...] = pre * _sigmoid(pre)

    return pl.pallas_call(
        body, name=name, grid=(CONV_DIM // cwid, t // tm),
        in_specs=[pl.BlockSpec((tm, cwid), lambda j, i: (i, cb0 + j)),
                  pl.BlockSpec((16, cwid), lambda j, i: (jnp.maximum(i * (tm // 16) - 1, 0), cb0 + j)),
                  pl.BlockSpec((CONV_K, cwid), lambda j, i: (0, j)), pl.BlockSpec((1, cwid), lambda j, i: (0, j))],
        out_specs=pl.BlockSpec((tm, cwid), lambda j, i: (i, j)),
        out_shape=jax.ShapeDtypeStruct((t, CONV_DIM), F32), compiler_params=_cp("parallel", "parallel"),
    )(proj, proj, cw_full, cb_full)


def conv_bwd_pre(proj, dxc, cw_full, cb_full, name):
    t = proj.shape[0]
    tm = min(256, t)
    cwid = 1024
    cb0 = P_XBC // cwid

    def body(x_ref, p_ref, d_ref, w_ref, b_ref, o_ref, st_ref):
        i = pl.program_id(1)

        @pl.when(i == 0)
        def _():
            st_ref[...] = jnp.zeros_like(st_ref)

        cur = x_ref[...].astype(F32)
        prev8 = jnp.where(i == 0, 0.0, p_ref[...].astype(F32)[8:16])
        pre = _conv_pre(cur, prev8, w_ref[...], b_ref[...], tm)
        s = _sigmoid(pre)
        dpre = d_ref[...] * (s * (1.0 + pre * (1.0 - s)))
        o_ref[...] = dpre
        st_ref[4:5, :] += jnp.sum(dpre, axis=0, keepdims=True)
        st_ref[3:4, :] += jnp.sum(dpre * cur, axis=0, keepdims=True)
        full = jnp.concatenate([prev8, cur], axis=0)
        for k in range(CONV_K - 1):
            sft = CONV_K - 1 - k
            st_ref[k:k + 1, :] += jnp.sum(dpre * pltpu.roll(full, sft, 0)[8:8 + tm, :], axis=0, keepdims=True)

    return pl.pallas_call(
        body, name=name, grid=(CONV_DIM // cwid, t // tm),
        in_specs=[pl.BlockSpec((tm, cwid), lambda j, i: (i, cb0 + j)),
                  pl.BlockSpec((16, cwid), lambda j, i: (jnp.maximum(i * (tm // 16) - 1, 0), cb0 + j)),
                  pl.BlockSpec((tm, cwid), lambda j, i: (i, j)),
                  pl.BlockSpec((CONV_K, cwid), lambda j, i: (0, j)), pl.BlockSpec((1, cwid), lambda j, i: (0, j))],
        out_specs=[pl.BlockSpec((tm, cwid), lambda j, i: (i, j)), pl.BlockSpec((8, cwid), lambda j, i: (0, j))],
        out_shape=[jax.ShapeDtypeStruct((t, CONV_DIM), F32), jax.ShapeDtypeStruct((8, CONV_DIM), F32)],
        compiler_params=_cp("parallel", "arbitrary"),
    )(proj, proj, dxc, cw_full, cb_full)


def conv_bwd_in(dpre, cw_full, name):
    t = dpre.shape[0]
    tm = min(256, t)
    cwid = 1024
    nt = t // tm

    def body(d_ref, n_ref, w_ref, o_ref):
        i = pl.program_id(1)
        cur = d_ref[...]
        nxt = jnp.where(i == nt - 1, 0.0, n_ref[...])
        full = jnp.concatenate([cur, nxt], axis=0)
        w = w_ref[...]
        acc = w[3:4, :] * cur
        for k in range(CONV_K - 1):
            s = CONV_K - 1 - k
            acc = acc + w[k:k + 1, :] * pltpu.roll(full, tm + 8 - s, 0)[0:tm, :]
        o_ref[...] = acc.astype(BF16)

    return pl.pallas_call(
        body, name=name, grid=(CONV_DIM // cwid, nt),
        in_specs=[pl.BlockSpec((tm, cwid), lambda j, i: (i, j)),
                  pl.BlockSpec((8, cwid), lambda j, i: (jnp.minimum((i + 1) * (tm // 8), t // 8 - 1), j)),
                  pl.BlockSpec((CONV_K, cwid), lambda j, i: (0, j))],
        out_specs=pl.BlockSpec((tm, cwid), lambda j, i: (i, j)),
        out_shape=jax.ShapeDtypeStruct((t, CONV_DIM), BF16), compiler_params=_cp("parallel", "parallel"),
    )(dpre, dpre, cw_full)


def _nt(a, b):
    return lax.dot_general(a, b, (((1,), (1,)), ((), ())), preferred_element_type=F32)


def _dot(a, b):
    return jnp.dot(a, b, preferred_element_type=F32)


def _head_lanes():
    return lax.broadcasted_iota(jnp.int32, (1, 128), 1) < NH


def _expand_heads(x, e3):
    x = jnp.where(_head_lanes(), x, 0.0)
    hi = x.astype(BF16).astype(F32)
    r1 = x - hi
    mid = r1.astype(BF16).astype(F32)
    packed = hi + pltpu.roll(mid, NH, 1) + pltpu.roll(r1 - mid, 2 * NH, 1)
    return _dot(packed.astype(BF16), e3)


def _reduce_heads(v, e3):
    hi = v.astype(BF16)
    lo = (v - hi.astype(F32)).astype(BF16)
    return jnp.where(_head_lanes(), _nt(hi, e3) + _nt(lo, e3), 0.0)


def _ssd_common(dt_ref, dtb_ref, al_ref, exp_ref, tri_ref):
    a_row = jnp.where(_head_lanes(), -jnp.exp(al_ref[...]), 0.0)
    zraw = dt_ref[...] + dtb_ref[...]
    dtv = _softplus(zraw)
    cs = jnp.dot(tri_ref[...], dtv * a_row, precision=HI, preferred_element_type=F32)
    e3 = exp_ref[...]
    return a_row, zraw, dtv, cs, _expand_heads(cs, e3), _expand_heads(dtv, e3)


def ssd_fwd(xc, proj, dtb_row, alog_row, dx_row, expm, tri, name):
    t = xc.shape[0]
    nc = t // LCH

    def body(xs_ref, bm_ref, cm_ref, dt_ref, dtb_ref, al_ref, dxr_ref, exp_ref, tri_ref, y_ref, hs_ref, h_scr):
        @pl.when(pl.program_id(0) == 0)
        def _():
            h_scr[...] = jnp.zeros_like(h_scr)

        _, _, _, cs, csx, dtx = _ssd_common(dt_ref, dtb_ref, al_ref, exp_ref, tri_ref)
        cst = cs.T
        csl = csx[LCH - 1:LCH, :]
        xs = xs_ref[...]
        xd = xs * dtx
        xdw = xd * jnp.exp(csl - csx)
        ecs = jnp.exp(csx)
        ecl = jnp.exp(csl)
        tril = lax.broadcasted_iota(jnp.int32, (LCH, LCH), 0) >= lax.broadcasted_iota(jnp.int32, (LCH, LCH), 1)
        hs_ref[...] = h_scr[...]
        for g in range(NG):
            gc = slice(g * 512, (g + 1) * 512)
            bm = bm_ref[:, g * NST:(g + 1) * NST]
            cmb = cm_ref[:, g * NST:(g + 1) * NST].astype(BF16)
            gm = _nt(cmb, bm.astype(BF16))
            hg = h_scr[:, gc]
            yo = _dot(cmb, hg.astype(BF16)) * ecs[:, gc]
            st = _dot(bm.T.astype(BF16), xdw[:, gc].astype(BF16))
            for r in range(8):
                h = g * 8 + r
                hc = slice(h * HP, (h + 1) * HP)
                seg = cs[:, h:h + 1] - cst[h:h + 1, :]
                m = (gm * jnp.exp(jnp.where(tril, seg, NEG))).astype(BF16)
                yd = _dot(m, xd[:, hc].astype(BF16))
                y_ref[:, hc] = yd + yo[:, r * HP:(r + 1) * HP] + dxr_ref[:, hc] * xs[:, hc]
            h_scr[:, gc] = ecl[:, gc] * hg + st

    return pl.pallas_call(
        body, name=name, grid=(nc,),
        in_specs=[pl.BlockSpec((LCH, 2048), lambda c: (c, 0)), pl.BlockSpec((LCH, 512), lambda c: (c, 4)),
                  pl.BlockSpec((LCH, 512), lambda c: (c, 5)), pl.BlockSpec((LCH, 128), lambda c: (c, 0)),
                  _vec(128), _vec(128), _vec(2048), pl.BlockSpec((128, 2048), lambda c: (0, 0)),
                  pl.BlockSpec((LCH, LCH), lambda c: (0, 0))],
        out_specs=[pl.BlockSpec((LCH, 2048), lambda c: (c, 0)), pl.BlockSpec((None, NST, 2048), lambda c: (c, 0, 0))],
        out_shape=[jax.ShapeDtypeStruct((t, 2048), F32), jax.ShapeDtypeStruct((nc, NST, 2048), F32)],
        scratch_shapes=[pltpu.VMEM((NST, 2048), F32)],
        compiler_params=_cp("arbitrary"),
    )(xc, xc, xc, proj, dtb_row, alog_row, dx_row, expm, tri)


def ssd_bwd(xc, proj, hsave, dy, dtb_row, alog_row, dx_row, expm, tri, name):
    t = xc.shape[0]
    nc = t // LCH

    def body(xs_ref, bm_ref, cm_ref, dt_ref, hs_ref, dy_ref, dtb_ref, al_ref, dxr_ref, exp_ref, tri_ref,
             dxc_ref, ddt_ref, st_ref, dh_scr, dxd_scr, dcsx_scr):
        @pl.when(pl.program_id(0) == 0)
        def _():
            dh_scr[...] = jnp.zeros_like(dh_scr)
            st_ref[...] = jnp.zeros_like(st_ref)

        a_row, zraw, dtv, cs, csx, dtx = _ssd_common(dt_ref, dtb_ref, al_ref, exp_ref, tri_ref)
        e = exp_ref[...]
        cst = cs.T
        csl = csx[LCH - 1:LCH, :]
        xs = xs_ref[...]
        xd = xs * dtx
        wend = jnp.exp(csl - csx)
        xdw = xd * wend
        ecs = jnp.exp(csx)
        ecl = jnp.exp(csl)
        ri = lax.broadcasted_iota(jnp.int32, (LCH, LCH), 0)
        ci = lax.broadcasted_iota(jnp.int32, (LCH, LCH), 1)
        tril = ri >= ci
        triu = ri <= ci
        lane = lax.broadcasted_iota(jnp.int32, (1, 128), 1)
        dyv = dy_ref[...]
        dxr = dxr_ref[...]
        st_ref[2:3, :] += _reduce_heads(jnp.sum(dyv * xs, axis=0, keepdims=True), e)
        dcs = jnp.zeros((LCH, 128), F32)
        for g in range(NG):
            gc = slice(g * 512, (g + 1) * 512)
            bmb = bm_ref[:, g * NST:(g + 1) * NST].astype(BF16)
            cm = cm_ref[:, g * NST:(g + 1) * NST]
            cmb = cm.astype(BF16)
            hg = hs_ref[:, gc]
            hgb = hg.astype(BF16)
            dhc = dh_scr[:, gc]
            dhcb = dhc.astype(BF16)
            dyg = dyv[:, gc]
            yo = _dot(cmb, hgb) * ecs[:, gc]
            dq = (dyg * ecs[:, gc]).astype(BF16)
            dcm = _nt(dq, hgb)
            dh_yo = _dot(cm.T.astype(BF16), dq)
            dxdw = _dot(bmb, dhcb)
            dbm = _nt(xdw[:, gc].astype(BF16), dhcb)
            tt = dxdw * xdw[:, gc]
            dcsx_g = dyg * yo - tt
            dcsl_g = jnp.sum(tt, axis=0, keepdims=True) + jnp.sum(dhc * hg, axis=0, keepdims=True) * ecl[:, gc]
            dxd_scr[:, gc] = dxdw * wend[:, gc]
            dh_scr[:, gc] = ecl[:, gc] * dhc + dh_yo
            gm = _nt(cmb, bmb)
            gmt = _nt(bmb, cmb)
            dg = jnp.zeros((LCH, LCH), F32)
            dgt = jnp.zeros((LCH, LCH), F32)
            for r in range(8):
                h = g * 8 + r
                hc = slice(h * HP, (h + 1) * HP)
                seg = cs[:, h:h + 1] - cst[h:h + 1, :]
                lm = jnp.exp(jnp.where(tril, seg, NEG))
                lmt = jnp.exp(jnp.where(triu, -seg, NEG))
                mm_ = gm * lm
                mmt = gmt * lmt
                xdh = xd[:, hc].astype(BF16)
                dyh = dyv[:, hc].astype(BF16)
                dm = _nt(dyh, xdh)
                dmt = _nt(xdh, dyh)
                dxd_scr[:, hc] += _dot(mmt.astype(BF16), dyh)
                rs = jnp.sum(dm * mm_, axis=1, keepdims=True) - jnp.sum(dmt * mmt, axis=1, keepdims=True)
                dcs = dcs + rs * jnp.where(lane == h, 1.0, 0.0)
                dg = dg + dm * lm
                dgt = dgt + dmt * lmt
            dcm = dcm + _dot(dg.astype(BF16), bmb)
            dbm = dbm + _dot(dgt.astype(BF16), cmb)
            dxc_ref[:, 2048 + g * NST:2048 + (g + 1) * NST] = dbm
            dxc_ref[:, 2560 + g * NST:2560 + (g + 1) * NST] = dcm
            dcsx_scr[:, gc] = dcsx_g
            dcsx_scr[LCH - 1:LCH, gc] += dcsl_g
        dxd = dxd_scr[...]
        dxc_ref[:, 0:2048] = dxr * dyv + dxd * dtx
        ddtv = _reduce_heads(dxd * xs, e)
        dcs = dcs + _reduce_heads(dcsx_scr[...], e)
        dda =lax.dot_general(tri_ref[...], dcs, (((0,), (0,)), ((), ())), precision=HI, preferred_element_type=F32)
        ddtv = ddtv + dda * a_row
        st_ref[0:1, :] += jnp.sum(dda * dtv, axis=0, keepdims=True) * a_row
        ddt = ddtv * _sigmoid(zraw)
        ddt_ref[...] = ddt
        st_ref[1:2, :] += jnp.sum(ddt, axis=0, keepdims=True)

    rc = lambda c: nc - 1 - c
    return pl.pallas_call(
        body, name=name, grid=(nc,),
        in_specs=[pl.BlockSpec((LCH, 2048), lambda c: (rc(c), 0)), pl.BlockSpec((LCH, 512), lambda c: (rc(c), 4)),
                  pl.BlockSpec((LCH, 512), lambda c: (rc(c), 5)),
                  pl.BlockSpec((LCH, 128), lambda c: (rc(c), 0)),
                  pl.BlockSpec((None, NST, 2048), lambda c: (rc(c), 0, 0)),
                  pl.BlockSpec((LCH, 2048), lambda c: (rc(c), 0)),
                  _vec(128), _vec(128), _vec(2048), pl.BlockSpec((128, 2048), lambda c: (0, 0)),
                  pl.BlockSpec((LCH, LCH), lambda c: (0, 0))],
        out_specs=[pl.BlockSpec((LCH, CONV_DIM), lambda c: (rc(c), 0)), pl.BlockSpec((LCH, 128), lambda c: (rc(c), 0)),
                   _stats(128)],
        out_shape=[jax.ShapeDtypeStruct((t, CONV_DIM), F32), jax.ShapeDtypeStruct((t, 128), F32),
                   jax.ShapeDtypeStruct((8, 128), F32)],
        scratch_shapes=[pltpu.VMEM((NST, 2048), F32), pltpu.VMEM((LCH, 2048), F32), pltpu.VMEM((LCH, 2048), F32)],
        compiler_params=_cp("arbitrary"),
    )(xc, xc, xc, proj, hsave, dy, dtb_row, alog_row, dx_row, expm, tri)


def ssd_out_fwd(y, proj, nw, name):
    t = y.shape[0]
    tm = min(256, t)

    def body(y_ref, z_ref, nw_ref, o_ref):
        for g in range(NG):
            gc = slice(g * 512, (g + 1) * 512)
            z = z_ref[:, gc].astype(F32)
            yz = y_ref[:, gc] * (z * _sigmoid(z))
            r = lax.rsqrt(jnp.mean(yz * yz, axis=-1, keepdims=True) + EPS)
            o_ref[:, gc] = (yz * r * nw_ref[:, gc]).astype(BF16)

    return pl.pallas_call(body, name=name, grid=(t // tm,),
                          in_specs=[_row(tm, 2048), _row(tm, 2048, P_Z // 2048), _vec(2048)],
                          out_specs=_row(tm, 2048), out_shape=jax.ShapeDtypeStruct((t, 2048), BF16),
                          compiler_params=_cp("parallel"))(y, proj, nw)


def ssd_out_bwd(y, proj, dya, nw, name):
    t = y.shape[0]
    tm = min(256, t)

    def body(y_ref, z_ref, d_ref, nw_ref, dy_ref, dz_ref, st_ref):
        @pl.when(pl.program_id(0) == 0)
        def _():
            st_ref[...] = jnp.zeros_like(st_ref)

        for g in range(NG):
            gc = slice(g * 512, (g + 1) * 512)
            z = z_ref[:, gc].astype(F32)
            yv = y_ref[:, gc]
            s = _sigmoid(z)
            sz = z * s
            yz = yv * sz
            r = lax.rsqrt(jnp.mean(yz * yz, axis=-1, keepdims=True) + EPS)
            yzn = yz * r
            dv = d_ref[:, gc]
            st_ref[0:1, gc] += jnp.sum(dv * yzn, axis=0, keepdims=True)
            dyn = dv * nw_ref[:, gc]
            dyz = r * (dyn - yzn * jnp.mean(dyn * yzn, axis=-1, keepdims=True))
            dy_ref[:, gc] = dyz * sz
            dz_ref[:, gc] = (dyz * yv * (s * (1.0 + z * (1.0 - s)))).astype(BF16)

    return pl.pallas_call(
        body, name=name, grid=(t // tm,),
        in_specs=[_row(tm, 2048), _row(tm, 2048, P_Z // 2048), _row(tm, 2048), _vec(2048)],
        out_specs=[_row(tm, 2048), _row(tm, 2048), _stats(2048)],
        out_shape=[jax.ShapeDtypeStruct((t, 2048), F32), jax.ShapeDtypeStruct((t, 2048), BF16),
                   jax.ShapeDtypeStruct((8, 2048), F32)],
        compiler_params=_cp("arbitrary"))(y, proj, dya, nw)


def _cstep(ar, ai, sr, si, br, bi):
    return ar * sr - ai * si + br, ar * si + ai * sr + bi


def _halves(v):
    return (v[0:8, 0:512], v[0:8, 512:1024]), (v[8:16, 0:512], v[8:16, 512:1024])


def _slab(r1, i1, r2, i2):
    return jnp.concatenate([jnp.concatenate([r1, i1], axis=1), jnp.concatenate([r2, i2], axis=1)], axis=0).astype(BF16)


def _local_ends(x_ref, nslab, ar, ai, sr_scr, si_scr, end_ref, first, last, reverse):
    @pl.when(first)
    def _():
        sr_scr[...] = jnp.zeros_like(sr_scr)
        si_scr[...] = jnp.zeros_like(si_scr)

    def step(k, carry):
        s1, s2 = _halves(x_ref[nslab - 1 - k if reverse else k].astype(F32))
        if reverse:
            s1, s2 = s2, s1
        return _cstep(ar, ai, *_cstep(ar, ai, carry[0], carry[1], *s1), *s2)

    sr, si = lax.fori_loop(0, nslab, step, (sr_scr[...], si_scr[...]), unroll=4)
    sr_scr[...] = sr
    si_scr[...] = si

    @pl.when(last)
    def _():
        end_ref[:, 0:512] = sr
        end_ref[:, 512:1024] = si


def s5_in(u, bsg, a_re, a_im, name):
    t = u.shape[0]
    tm = min(512, t)
    nt = t // tm

    def body(u_ref, b_ref, ar_ref, ai_ref, o_ref, e_ref, sr_scr, si_scr):
        i = pl.program_id(1)
        o_ref[...] = _dot(u_ref[...].astype(BF16), b_ref[...]).astype(BF16).reshape(tm // 16, 16, 1024)
        _local_ends(o_ref, tm // 16, ar_ref[...], ai_ref[...], sr_scr, si_scr, e_ref, i == 0, i == nt - 1, False)

    return pl.pallas_call(
        body, name=name, grid=(8, nt),
        in_specs=[pl.BlockSpec((tm, 128), lambda s, i: (i, s)), pl.BlockSpec((None, 128, 1024), lambda s, i: (s, 0, 0)),
                  pl.BlockSpec((None, 8, 512), lambda s, i: (s, 0, 0)), pl.BlockSpec((None, 8, 512), lambda s, i: (s, 0, 0))],
        out_specs=[pl.BlockSpec((tm // 16, 16, 1024), lambda s, i: (i, 0, s)),
                   pl.BlockSpec((None, 8, 1024), lambda s, i: (s, 0, 0))],
        out_shape=[jax.ShapeDtypeStruct((t // 16, 16, S5NS), BF16), jax.ShapeDtypeStruct((8, 8, 1024), F32)],
        scratch_shapes=[pltpu.VMEM((8, 512), F32), pltpu.VMEM((8, 512), F32)],
        compiler_params=_cp("parallel", "arbitrary"))(u, bsg, a_re, a_im)


def s5_out(s, csg, u, d_row, name):
    t = u.shape[0]
    tm = min(512, t)

    def body(s_ref, c_ref, u_ref, d_ref, o_ref):
        o_ref[...] = _dot(s_ref[...].astype(BF16), c_ref[...]) + d_ref[...] * u_ref[...].astype(F32)

    return pl.pallas_call(
        body, name=name, grid=(8, t // tm),
        in_specs=[pl.BlockSpec((tm, 1024), lambda s, i: (i, s)), pl.BlockSpec((None, 1024, 128), lambda s, i: (s, 0, 0)),
                  pl.BlockSpec((tm, 128), lambda s, i: (i, s)), pl.BlockSpec((1, 128), lambda s, i: (0, s))],
        out_specs=pl.BlockSpec((tm, 128), lambda s, i: (i, s)),
        out_shape=jax.ShapeDtypeStruct((t, S5W), F32), compiler_params=_cp("parallel", "parallel"))(s, csg, u, d_row)


def s5_out_bwd(dy, csg, s, a_re, a_im, name):
    t = dy.shape[0]
    tm = min(512, t)
    nt = t // tm

    def body(dy_ref, c_ref, s_ref, ar_ref, ai_ref, e_ref, dc_ref, end_ref, sr_scr, si_scr):
        i = pl.program_id(1)

        @pl.when(i == 0)
        def _():
            dc_ref[...] = jnp.zeros_like(dc_ref)

        dyb = dy_ref[...].astype(BF16)
        e_ref[...] = _nt(dyb, c_ref[...]).astype(BF16).reshape(tm // 16, 16, 1024)
        dc_ref[...] += lax.dot_general(s_ref[...], dyb, (((0,), (0,)), ((), ())), preferred_element_type=F32)
        _local_ends(e_ref, tm // 16, ar_ref[...], -ai_ref[...], sr_scr, si_scr, end_ref, i == 0, i == nt - 1, True)

    rv = lambda i: nt - 1 - i
    return pl.pallas_call(
        body, name=name, grid=(8, nt),
        in_specs=[pl.BlockSpec((tm, 128), lambda s, i: (rv(i), s)), pl.BlockSpec((None, 1024, 128), lambda s, i: (s, 0, 0)),
                  pl.BlockSpec((tm, 1024), lambda s, i: (rv(i), s)),
                  pl.BlockSpec((None, 8, 512), lambda s, i: (s, 0, 0)), pl.BlockSpec((None, 8, 512), lambda s, i: (s, 0, 0))],
        out_specs=[pl.BlockSpec((tm // 16, 16, 1024), lambda s, i: (rv(i), 0, s)),
                   pl.BlockSpec((None, 1024, 128), lambda s, i: (s, 0, 0)),
                   pl.BlockSpec((None, 8, 1024), lambda s, i: (s, 0, 0))],
        out_shape=[jax.ShapeDtypeStruct((t // 16, 16, S5NS), BF16), jax.ShapeDtypeStruct((8, 1024, 128), F32),
                   jax.ShapeDtypeStruct((8, 8, 1024), F32)],
        scratch_shapes=[pltpu.VMEM((8, 512), F32), pltpu.VMEM((8, 512), F32)],
        compiler_params=_cp("parallel", "arbitrary"))(dy, csg, s, a_re, a_im)


def s5_in_bwd(lam, bsg, u, dy, d_row, name):
    t = u.shape[0]
    tm = min(512, t)

    def body(l_ref, b_ref, u_ref, dy_ref, d_ref, du_ref, db_ref, dd_ref):
        @pl.when(pl.program_id(1) == 0)
        def _():
            db_ref[...] = jnp.zeros_like(db_ref)
            dd_ref[...] = jnp.zeros_like(dd_ref)

        lb = l_ref[...].astype(BF16)
        uv = u_ref[...].astype(F32)
        dyv = dy_ref[...]
        du_ref[...] = _nt(lb, b_ref[...]) + d_ref[...] * dyv
        db_ref[...] += lax.dot_general(uv.astype(BF16), lb, (((0,), (0,)), ((), ())), preferred_element_type=F32)
        dd_ref[...] += jnp.sum(dyv * uv, axis=0, keepdims=True)

    return pl.pallas_call(
        body, name=name, grid=(8, t // tm),
        in_specs=[pl.BlockSpec((tm, 1024), lambda s, i: (i, s)), pl.BlockSpec((None, 128, 1024), lambda s, i: (s, 0, 0)),
                  pl.BlockSpec((tm, 128), lambda s, i: (i, s)), pl.BlockSpec((tm, 128), lambda s, i: (i, s)),
                  pl.BlockSpec((1, 128), lambda s, i: (0, s))],
        out_specs=[pl.BlockSpec((tm, 128), lambda s, i: (i, s)), pl.BlockSpec((None, 128, 1024), lambda s, i: (s, 0, 0)),
                   pl.BlockSpec((1, 128), lambda s, i: (0, s))],
        out_shape=[jax.ShapeDtypeStruct((t, S5W), F32), jax.ShapeDtypeStruct((8, 128, 1024), F32),
                   jax.ShapeDtypeStruct((1, S5W), F32)],
        compiler_params=_cp("parallel", "arbitrary"))(lam, bsg, u, dy, d_row)


def s5_scan_init(ends, a_re, a_im, lseg, reverse, name):
    nsq = int(math.log2(lseg))
    assert 2 ** nsq == lseg
    sgn = -1.0 if reverse else 1.0
    order = list(range(7, -1, -1)) if reverse else list(range(8))

    def body(e_ref, ar_ref, ai_ref, o_ref):
        pr = ar_ref[0:1, :]
        pi = sgn * ai_ref[0:1, :]
        for _ in range(nsq):
            pr, pi = pr * pr - pi * pi, 2.0 * pr * pi
        prev_r = jnp.zeros((1, 512), F32)
        prev_i = jnp.zeros((1, 512), F32)
        j0 = order[0]
        o_ref[j0:j0 + 1, 0:512] = prev_r
        o_ref[j0:j0 + 1, 512:1024] = prev_i
        for idx in range(1, 8):
            j, jp = order[idx], order[idx - 1]
            prev_r, prev_i = _cstep(pr, pi, prev_r, prev_i, e_ref[jp:jp + 1, 0:512], e_ref[jp:jp + 1, 512:1024])
            o_ref[j:j + 1, 0:512] = prev_r
            o_ref[j:j + 1, 512:1024] = prev_i

    return pl.pallas_call(
        body, name=name, grid=(8,),
        in_specs=[pl.BlockSpec((None, 8, 1024), lambda s: (s, 0, 0)), pl.BlockSpec((None, 8, 512), lambda s: (s, 0, 0)),
                  pl.BlockSpec((None, 8, 512), lambda s: (s, 0, 0))],
        out_specs=pl.BlockSpec((None, 8, 1024), lambda s: (s, 0, 0)),
        out_shape=jax.ShapeDtypeStruct((8, 8, 1024), F32), compiler_params=_cp("parallel"))(ends, a_re, a_im)


def s5_scan_fwd(b3, init, a_re, a_im, name):
    nslab = b3.shape[0]
    ti = min(64, nslab)
    nb = nslab // ti

    def body(b_ref, i_ref, ar_ref, ai_ref, o_ref, sr_scr, si_scr):
        @pl.when(pl.program_id(1) == 0)
        def _():
            sr_scr[...] = i_ref[:, 0:512]
            si_scr[...] = i_ref[:, 512:1024]

        ar = ar_ref[...]
        ai = ai_ref[...]

        def step(k, carry):
            b1, b2 = _halves(b_ref[k].astype(F32))
            r1, i1 = _cstep(ar, ai, carry[0], carry[1], *b1)
            r2, i2 = _cstep(ar, ai, r1, i1, *b2)
            o_ref[k] = _slab(r1, i1, r2, i2)
            return r2, i2

        sr, si = lax.fori_loop(0, ti, step, (sr_scr[...], si_scr[...]), unroll=4)
        sr_scr[...] = sr
        si_scr[...] = si

    return pl.pallas_call(
        body, name=name, grid=(8, nb),
        in_specs=[pl.BlockSpec((ti, 16, 1024), lambda s, tb: (tb, 0, s)), pl.BlockSpec((None, 8, 1024), lambda s, tb: (s, 0, 0)),
                  pl.BlockSpec((None, 8, 512), lambda s, tb: (s, 0, 0)), pl.BlockSpec((None, 8, 512), lambda s, tb: (s, 0, 0))],
        out_specs=pl.BlockSpec((ti, 16, 1024), lambda s, tb: (tb, 0, s)),
        out_shape=jax.ShapeDtypeStruct(b3.shape, BF16),
        scratch_shapes=[pltpu.VMEM((8, 512), F32), pltpu.VMEM((8, 512), F32)],
        compiler_params=_cp("parallel", "arbitrary"))(b3, init, a_re, a_im)


def s5_scan_bwd(e3, linit, s3, sinit, a_re, a_im, name):
    nslab = e3.shape[0]
    ti = min(64, nslab)
    nb = nslab // ti

    def body(e_ref, li_ref, s_ref, sh_ref, si0_ref, ar_ref, ai_ref, o_ref, da_ref, lr_scr, lim_scr):
        tb = pl.program_id(1)

        @pl.when(tb == 0)
        def _():
            lr_scr[...] = li_ref[:, 0:512]
            lim_scr[...] = li_ref[:, 512:1024]
            da_ref[...] = jnp.zeros_like(da_ref)

        ar = ar_ref[...]
        ai = -ai_ref[...]

        def slab(kk, lr, li, dar, dai, sp):
            e1, e2 = _halves(e_ref[kk].astype(F32))
            s1, _ = _halves(s_ref[kk].astype(F32))
            r2, i2 = _cstep(ar, ai, lr, li, *e2)
            dar = dar + r2 * s1[0] + i2 * s1[1]
            dai = dai + i2 * s1[0] - r2 * s1[1]
            r1, i1 = _cstep(ar, ai, r2, i2, *e1)
            dar = dar + r1 * sp[0] + i1 * sp[1]
            dai = dai + i1 * sp[0] - r1 * sp[1]
            o_ref[kk] = _slab(r1, i1, r2, i2)
            return r1, i1, dar, dai

        def step(k, carry):
            kk = ti - 1 - k
            return slab(kk, *carry, _halves(s_ref[kk - 1].astype(F32))[1])

        z = jnp.zeros((8, 512), F32)
        lr, li, dar, dai = lax.fori_loop(0, ti - 1, step, (lr_scr[...], lim_scr[...], z, z), unroll=2)
        first = tb == nb - 1
        halo = _halves(sh_ref[0].astype(F32))[1]
        sp = (jnp.where(first, si0_ref[:, 0:512], halo[0]), jnp.where(first, si0_ref[:, 512:1024], halo[1]))
        lr, li, dar, dai = slab(0, lr, li, dar, dai, sp)
        lr_scr[...] = lr
        lim_scr[...] = li
        da_ref[:, 0:512] += dar
        da_ref[:, 512:1024] += dai

    rb = lambda tb: nb - 1 - tb
    return pl.pallas_call(
        body, name=name, grid=(8, nb),
        in_specs=[pl.BlockSpec((ti, 16, 1024), lambda s, tb: (rb(tb), 0, s)),
                  pl.BlockSpec((None, 8, 1024), lambda s, tb: (s, 0, 0)),
                  pl.BlockSpec((ti, 16, 1024), lambda s, tb: (rb(tb), 0, s)),
                  pl.BlockSpec((1, 16, 1024), lambda s, tb: (jnp.maximum(rb(tb) * ti - 1, 0), 0, s)),
                  pl.BlockSpec((None, 8, 1024), lambda s, tb: (s, 0, 0)),
                  pl.BlockSpec((None, 8, 512), lambda s, tb: (s, 0, 0)), pl.BlockSpec((None, 8, 512), lambda s, tb: (s, 0, 0))],
        out_specs=[pl.BlockSpec((ti, 16, 1024), lambda s, tb: (rb(tb), 0, s)),
                   pl.BlockSpec((None, 8, 1024), lambda s, tb: (s, 0, 0))],
        out_shape=[jax.ShapeDtypeStruct(e3.shape, BF16), jax.ShapeDtypeStruct((8, 8, 1024), F32)],
        scratch_shapes=[pltpu.VMEM((8, 512), F32), pltpu.VMEM((8, 512), F32)],
        compiler_params=_cp("parallel", "arbitrary"))(e3, linit, s3, s3, sinit, a_re, a_im)


_GC = math.sqrt(2.0 / math.pi)


def gelu_fwd(y, name):
    t, w = y.shape
    tm = min(512, t)

    def body(y_ref, o_ref):
        v = y_ref[...]
        o_ref[...] = (0.5 * v * (1.0 + jnp.tanh(_GC * (v + 0.044715 * v * v * v)))).astype(BF16)

    return pl.pallas_call(body, name=name, grid=(t // tm,), in_specs=[_row(tm, w)], out_specs=_row(tm, w),
                          out_shape=jax.ShapeDtypeStruct((t, w), BF16), compiler_params=_cp("parallel"))(y)


def gelu_bwd(y, dg, name):
    t, w = y.shape
    tm = min(512, t)

    def body(y_ref, d_ref, o_ref):
        v = y_ref[...]
        th = jnp.tanh(_GC * (v + 0.044715 * v * v * v))
        o_ref[...] = d_ref[...] * (0.5 * (1.0 + th) + 0.5 * v * (1.0 - th * th) * _GC * (1.0 + 3.0 * 0.044715 * v * v))

    return pl.pallas_call(body, name=name, grid=(t // tm,), in_specs=[_row(tm, w), _row(tm, w)], out_specs=_row(tm, w),
                          out_shape=jax.ShapeDtypeStruct((t, w), F32), compiler_params=_cp("parallel"))(y, dg)


def merge_fwd(proj, pa, glu, name):
    t = pa.shape[0]
    tm = min(256, t)

    def body(g_ref, pa_ref, glu_ref, o_ref):
        pb = glu_ref[:, 0:D].astype(F32) * _sigmoid(glu_ref[:, D:2 * D].astype(F32))
        o_ref[...] = (_sigmoid(g_ref[:, 0:D].astype(F32)) * pa_ref[...].astype(F32)
                      + _sigmoid(g_ref[:, D:2 * D].astype(F32)) * pb).astype(BF16)

    return pl.pallas_call(body, name=name, grid=(t // tm,), in_specs=[_row(tm, 2 * D), _row(tm, D), _row(tm, 2 * D)],
                          out_specs=_row(tm, D), out_shape=jax.ShapeDtypeStruct((t, D), BF16),
                          compiler_params=_cp("parallel"))(proj, pa, glu)


def merge_bwd(proj, pa, glu, dm, name):
    t = pa.shape[0]
    tm = min(256, t)

    def body(g_ref, pa_ref, glu_ref, dm_ref, dpa_ref, dglu_ref, dg_ref):
        dmv = dm_ref[...]
        pav = pa_ref[...].astype(F32)
        sa = _sigmoid(g_ref[:, 0:D].astype(F32))
        sb = _sigmoid(g_ref[:, D:2 * D].astype(F32))
        ga = glu_ref[:, 0:D].astype(F32)
        sg = _sigmoid(glu_ref[:, D:2 * D].astype(F32))
        pb = ga * sg
        dpb = sb * dmv
        dpa_ref[...] = (sa * dmv).astype(BF16)
        dglu_ref[:, 0:D] = (dpb * sg).astype(BF16)
        dglu_ref[:, D:2 * D] = (dpb * pb * (1.0 - sg)).astype(BF16)
        dg_ref[:, 0:D] = (dmv * pav * sa * (1.0 - sa)).astype(BF16)
        dg_ref[:, D:2 * D] = (dmv * pb * sb * (1.0 - sb)).astype(BF16)

    return pl.pallas_call(
        body, name=name, grid=(t // tm,),
        in_specs=[_row(tm, 2 * D), _row(tm, D), _row(tm, 2 * D), _row(tm, D)],
        out_specs=[_row(tm, D), _row(tm, 2 * D), _row(tm, 2 * D)],
        out_shape=[jax.ShapeDtypeStruct((t, D), BF16), jax.ShapeDtypeStruct((t, 2 * D), BF16),
                   jax.ShapeDtypeStruct((t, 2 * D), BF16)],
        compiler_params=_cp("parallel"))(proj, pa, glu, dm)


def adamw(w, parts, m, v, name):
    r, c = w.shape
    p = parts.shape[0]
    tr = r if r <= 128 else 128
    c1 = 1.0 - ADAM_B1 ** ADAM_STEP
    c2 = 1.0 - ADAM_B2 ** ADAM_STEP

    def body(w_ref, p_ref, m_ref, v_ref, g_ref, d_ref, nm_ref, nv_ref):
        g = p_ref[0].astype(F32)
        for k in range(1, p):
            g = g + p_ref[k].astype(F32)
        mn = ADAM_B1 * m_ref[...] + (1.0 - ADAM_B1) * g
        vn = ADAM_B2 * v_ref[...] + (1.0 - ADAM_B2) * (g * g)
        g_ref[...] = g
        nm_ref[...] = mn
        nv_ref[...] = vn
        d_ref[...] = -ADAM_LR * ((mn / c1) / (jnp.sqrt(vn / c2) + ADAM_EPS) + ADAM_WD * w_ref[...])

    spec = pl.BlockSpec((tr, c), lambda i: (i, 0))
    o = jax.ShapeDtypeStruct((r, c), F32)
    return pl.pallas_call(
        body, name=name, grid=(pl.cdiv(r, tr),),
        in_specs=[spec, pl.BlockSpec((p, tr, c), lambda i: (0, i, 0)), spec, spec],
        out_specs=[spec, spec, spec, spec], out_shape=[o, o, o, o], compiler_params=_cp("parallel"))(w, parts, m, v)


def _s5_discretise(lambda_re, lambda_im, log_dt, b_re, b_im):
    dt = jnp.exp(log_dt)[:, None]
    lr = jnp.minimum(lambda_re, -1e-4)
    li = lambda_im
    mag = jnp.exp(lr * dt)
    ar = mag * jnp.cos(li * dt)
    ai = mag * jnp.sin(li * dt)
    den = lr * lr + li * li
    nr = ar - 1.0
    kr = (nr * lr + ai * li) / den
    ki = (ai * lr - nr * li) / den
    bbar_re = kr[..., None] * b_re - ki[..., None] * b_im
    bbar_im = kr[..., None] * b_im + ki[..., None] * b_re
    return ar, ai, bbar_re, bbar_im


def _block_diag(v):
    a, b = v.shape[2], v.shape[3]
    eye = jnp.eye(8, dtype=v.dtype)[None, :, None, :, None]
    return (v[:, :, :, None, :] * eye).reshape(8, 8 * a, 8 * b)


def _diag_blocks(m, a, b):
    eye = jnp.eye(8, dtype=m.dtype)[None, :, None, :, None]
    return jnp.sum(m.reshape(8, 8, a, 8, b) * eye, axis=3)


def _bsg_of(bb_re, bb_im):
    f = lambda b: _block_diag(b.reshape(8, 8, 64, 16).transpose(0, 1, 3, 2))
    return jnp.concatenate([f(bb_re), f(bb_im)], axis=2)


def _bsg_diag(dbsg):
    f = lambda x: _diag_blocks(x, 16, 64).transpose(0, 1, 3, 2).reshape(64, 64, 16)
    return f(dbsg[:, :, 0:512]), f(dbsg[:, :, 512:1024])


def _csg_of(c_re, c_im):
    f = lambda c: _block_diag(c.reshape(8, 8, 16, 64).transpose(0, 1, 3, 2))
    return jnp.concatenate([f(c_re), -f(c_im)], axis=1)


def _csg_diag(dcsg):
    f = lambda x: _diag_blocks(x, 64, 16).transpose(0, 1, 3, 2).reshape(64, 16, 64)
    return f(dcsg[:, 0:512, :]), -f(dcsg[:, 512:1024, :])


def _perm(a, t):
    return a.reshape(8, t // 8, a.shape[1]).transpose(1, 0, 2).reshape(t, a.shape[1])


def _unperm(a, t):
    return a.reshape(t // 8, 8, a.shape[1]).transpose(1, 0, 2).reshape(t, a.shape[1])


def _cols(g):
    return g.transpose(1, 0, 2).reshape(g.shape[1], N_DEV * g.shape[2])


def _rows(g):
    return g.reshape(N_DEV * g.shape[1], g.shape[2])


def _col_parts(g):
    r, c = g.shape
    return g.reshape(r, N_DEV, c // N_DEV).transpose(1, 0, 2)


def _row_parts(g):
    r, c = g.shape
    return g.reshape(N_DEV, r // N_DEV, c)


FB, FBP = D_FF // N_DEV, D_FFP // N_DEV


def _pad_ffn_in_shard(w):
    return jnp.pad(w.reshape(D, 2, FB), ((0, 0), (0, 0), (0, FBP - FB))).reshape(D, 2 * FBP)


def _unpad_ffn_in_shard(g):
    return g.reshape(D, 2, FBP)[:, :, :FB].reshape(D, 2 * FB)


def _pad_ffn_out_shard(w):
    return jnp.pad(w, ((0, FBP - FB), (0, 0)))


def sum_parts(parts, name):
    p, r, c = parts.shape
    tr = 256

    def body(p_ref, o_ref):
        g = p_ref[0].astype(F32)
        for k in range(1, p):
            g = g + p_ref[k].astype(F32)
        o_ref[...] = g

    return pl.pallas_call(body, name=name, grid=(r // tr,), in_specs=[pl.BlockSpec((p, tr, c), lambda i: (0, i, 0))],
                          out_specs=pl.BlockSpec((tr, c), lambda i: (i, 0)),
                          out_shape=jax.ShapeDtypeStruct((r, c), F32), compiler_params=_cp("parallel"))(parts)


def _pad_w_in(w):
    z = jnp.zeros((D, INP - P_DT - NH), w.dtype)
    return jnp.concatenate([w[:, O_GA:O_GB], w[:, O_GB:IN_COLS], w[:, 0:O_XBC], w[:, O_XBC:O_DT], w[:, O_U:O_GA],
                            w[:, O_DT:O_U], z], axis=1)


def _unpad_w_in(g):
    return jnp.concatenate([g[:, P_Z:P_XBC], g[:, P_XBC:P_U], g[:, P_DT:P_DT + NH], g[:, P_U:P_DT],
                            g[:, 0:D], g[:, D:2 * D]], axis=1)


_PACK = (("b_ada", 18432), ("norm_ffn1", 2048), ("norm_mix", 2048), ("conv_b", 3072), ("dt_bias", 32), ("a_log", 32),
         ("d_ssd", 32), ("ssd_norm_w", 2048), ("s5_lambda_re", 4096), ("s5_lambda_im", 4096), ("s5_b_re", 65536),
         ("s5_b_im", 65536), ("s5_c_re", 65536), ("s5_c_im", 65536), ("s5_d", 1024), ("s5_log_dt", 64),
         ("norm_ffn2", 2048), ("norm_final", 2048), ("loss", 1))
_PACK_ROWS = 304
_PACK_W = 1024


def _pack(d):
    flat = jnp.concatenate([d[k].reshape(-1).astype(F32) for k, _ in _PACK])
    return jnp.pad(flat, (0, _PACK_ROWS * _PACK_W - flat.shape[0])).reshape(_PACK_ROWS, _PACK_W)


def _unpack(a):
    flat = a.reshape(-1)
    out, off = {}, 0
    for k, n in _PACK:
        out[k] = flat[off:off + n]
        off += n
    return out


_TA = dict(tm=512, tn=512, tk=8192)


def _ffn_bwd(df, h, ab, act, w_in_t, w_out_t, tag):
    dab = ffn_dab(df, w_out_t, ab, tag + "_dab")
    dw_out = mm(act, df, ta=True, out_dtype=BF16, i_outer=True, name=tag + "_dwout", **_TA)
    dw_in, (x_out,) = mm(h, dab, ta=True, b_halves=True, out_dtype=BF16, i_outer=True, name=tag + "_dwin",
                         comm=[("xc", _row_parts(dw_out))], **_TA)
    dh, (x_in,) = mm(dab, w_in_t, a_halves=True, tk=5632, name=tag + "_dh", comm=[("xcc", dw_in)])
    g_in = _unpad_ffn_in_shard(sum_parts(x_in, tag + "_dwin_sum"))
    return dh, g_in[None], x_out


def kernel(x, c, w_ada, b_ada, norm_ffn1, w_ffn1_in, w_ffn1_out, norm_mix, w_in, conv_w, conv_b, dt_bias, a_log, d_ssd, ssd_norm_w, w_a_proj, s5_lambda_re, s5_lambda_im, s5_b_re, s5_b_im, s5_c_re, s5_c_im, s5_d, s5_log_dt, w_b_glu, w_out, norm_ffn2, w_ffn2_in, w_ffn2_out, norm_final, loss_target, m_w_ada, m_b_ada, m_norm_ffn1, m_w_ffn1_in, m_w_ffn1_out, m_norm_mix, m_w_in, m_conv_w, m_conv_b, m_dt_bias, m_a_log, m_d_ssd, m_ssd_norm_w, m_w_a_proj, m_s5_lambda_re, m_s5_lambda_im, m_s5_b_re, m_s5_b_im, m_s5_c_re, m_s5_c_im, m_s5_d, m_s5_log_dt, m_w_b_glu, m_w_out, m_norm_ffn2, m_w_ffn2_in, m_w_ffn2_out, m_norm_final, v_w_ada, v_b_ada, v_norm_ffn1, v_w_ffn1_in, v_w_ffn1_out, v_norm_mix, v_w_in, v_conv_w, v_conv_b, v_dt_bias, v_a_log, v_d_ssd, v_ssd_norm_w, v_w_a_proj, v_s5_lambda_re, v_s5_lambda_im, v_s5_b_re, v_s5_b_im, v_s5_c_re, v_s5_c_im, v_s5_d, v_s5_log_dt, v_w_b_glu, v_w_out, v_norm_ffn2, v_w_ffn2_in, v_w_ffn2_out, v_norm_final):
    args = dict(locals())
    t = x.shape[1]
    me = _my_id()
    xt = x[0]
    tgt = loss_target[0]
    small = {k: args[k] for k, _ in _PACK if k != "loss"}

    bf = lambda w: w[0].astype(BF16)
    ffn_in_shard = lambda w: _pad_ffn_in_shard(bf(w))
    ffn_out_shard = lambda w: _pad_ffn_out_shard(bf(w))

    c8 = all_gather(c, "ag_c").reshape(N_DEV, D)
    b_loc = lax.dynamic_slice(b_ada, (0, me * (N_ADA * D // N_DEV)), (1, N_ADA * D // N_DEV))
    m8 = ada_fwd(c8, w_ada[0], b_loc, "ada_fwd")
    mods, wf1i, g_cw = comm_call([("xc", m8.reshape(N_DEV, 1, -1)), ("agc", ffn_in_shard(w_ffn1_in)),
                                  ("ag", conv_w[0])], "xc_mods_ag_ffn1_in")
    mods = mods.reshape(1, N_ADA * D)
    convw = _cols(g_cw)

    h1 = mod_fwd(xt, norm_ffn1, mods, 0, 1, name="mod1")
    ab1, act1, (g_f1o, g_win, g_wap) = ffn_in_act(
        h1, wf1i, "ffn1_in", fwd=0.9,
        comm=[("ag", ffn_out_shard(w_ffn1_out)), ("ag", bf(w_in)), ("ag", bf(w_a_proj))])
    wf1o, winp, wap = _rows(g_f1o), _pad_w_in(_cols(g_win)), _rows(g_wap)
    f1, (g_wo, g_wbg) = mm(act1, wf1o, tk=5632, name="ffn1_out", comm=[("ag", bf(w_out)), ("ag", bf(w_b_glu))])
    wo, wbg = _rows(g_wo), _cols(g_wbg)
    x1, h2 = mod_fwd(xt, norm_mix, mods, 3, 4, f=f1, gk=2, gscale=0.5, name="mod2")
    proj, (wf2i,) = mm(h2, winp, out_dtype=BF16, tm=1024, tn=512, i_outer=True, name="w_in",
                       comm=[("agc", ffn_in_shard(w_ffn2_in))])
    dtraw = mm(h2, winp[:, P_DT:P_DT + 128], tn=128, name="w_in_dt")
    cb_row = conv_b
    xc = conv_fwd(proj, convw, cb_row, "conv_fwd")
    row128 = lambda v: jnp.pad(v.reshape(1, -1), ((0, 0), (0, 128 - v.size)))
    dtb_row, alog_row = row128(dt_bias), row128(a_log)
    dx_row = jnp.repeat(d_ssd.reshape(-1), HP).reshape(1, 2048)
    rows = jnp.arange(128)[:, None]
    expm = ((rows % NH == jnp.arange(2048)[None, :] // HP) & (rows < 3 * NH)).astype(BF16)
    tri = (jnp.arange(LCH)[:, None] >= jnp.arange(LCH)[None, :]).astype(F32)
    y_ssd, hsave = ssd_fwd(xc, dtraw, dtb_row, alog_row, dx_row, expm, tri, "ssd_fwd")
    ya = ssd_out_fwd(y_ssd, proj, ssd_norm_w, "ssd_out")
    pa = mm(ya, wap, out_dtype=BF16, name="w_a_proj")

    s5p = (s5_lambda_re[0], s5_lambda_im[0], s5_log_dt[0], s5_b_re[0], s5_b_im[0])
    (ar, ai, bb_re, bb_im), s5_vjp = jax.vjp(_s5_discretise, *s5p)
    a_re8 = jnp.broadcast_to(ar.reshape(8, 1, 512), (8, 8, 512))
    a_im8 = jnp.broadcast_to(ai.reshape(8, 1, 512), (8, 8, 512))
    bsg = _bsg_of(bb_re, bb_im).astype(BF16)
    csg = _csg_of(s5_c_re[0], s5_c_im[0]).astype(BF16)
    d_row = s5_d.reshape(1, S5W)
    lseg = t // 8
    u_p = _perm(proj[:, P_U:P_U + S5W], t)
    bu3, ends_f = s5_in(u_p, bsg, a_re8, a_im8, "s5_in")
    sinit = s5_scan_init(ends_f, a_re8, a_im8, lseg, False, "s5_init_f")
    s3 = s5_scan_fwd(bu3, sinit, a_re8, a_im8, "s5_scan_f")
    s2 = s3.reshape(t, S5NS)
    yb_p = s5_out(s2, csg, u_p, d_row, "s5_out")
    yb = _unperm(yb_p, t)
    gy = gelu_fwd(yb, "gelu")
    glu = mm(gy, wbg, out_dtype=BF16, name="w_b_glu")
    merged = merge_fwd(proj, pa, glu, "merge")
    o = mm(merged, wo, name="w_out")
    x2, h3 = mod_fwd(x1, norm_ffn2, mods, 6, 7, f=o, gk=5, gscale=1.0, name="mod3")
    ab3, act3, (g_f2o,) = ffn_in_act(h3, wf2i, "ffn2_in", comm=[("ag", ffn_out_shard(w_ffn2_out))])
    wf2o = _rows(g_f2o)
    f3 = mm(act3, wf2o, tk=5632, name="ffn2_out")

    dx3, df3, st_fin = final_fwd_bwd(x2, f3, mods, norm_final.reshape(1, D), tgt, "final")
    dh3, x_f2i, x_f2o = _ffn_bwd(df3, h3, ab3, act3, wf2i.T, wf2o.T, "ffn2")
    dx2, do, st3 = mod_bwd(x2, dh3, dx3, norm_ffn2, mods, 7, fprev=o, gk=5, gscale=1.0, name="mod3_bwd")

    dmerged = mm(do, wo.T, name="w_out_dx")
    dwo = mm(merged, do, ta=True, out_dtype=BF16, i_outer=True, name="w_out_dw", **_TA)
    dpa, dglu, dgates = merge_bwd(proj, pa, glu, dmerged, "merge_bwd")
    dwbg = mm(gy, dglu, ta=True, out_dtype=BF16, i_outer=True, name="w_b_glu_dw", **_TA)
    dgy, (x_wo,) = mm(dglu, wbg.T, name="w_b_glu_dx", comm=[("xc", _row_parts(dwo))])
    dyb_p = _perm(gelu_bwd(yb, dgy, "gelu_bwd"), t)
    e3, dcsg, ends_b = s5_out_bwd(dyb_p, csg, s2, a_re8, a_im8, "s5_out_bwd")
    linit = s5_scan_init(ends_b, a_re8, a_im8, lseg, True, "s5_init_b")
    lam3, da8 = s5_scan_bwd(e3, linit, s3, sinit, a_re8, a_im8, "s5_scan_b")
    du_p, dbsg, dd_row = s5_in_bwd(lam3.reshape(t, S5NS), bsg, u_p, dyb_p, d_row, "s5_in_bwd")
    du = _unperm(du_p, t).astype(BF16)
    da = jnp.sum(da8, axis=1)
    dbb_re, dbb_im = _bsg_diag(dbsg)
    g_lre, g_lim, g_ldt, g_bre, g_bim = s5_vjp((da[:, 0:512].reshape(64, 64), da[:, 512:1024].reshape(64, 64),
                                                dbb_re, dbb_im))
    g_cre, g_cim = _csg_diag(dcsg)

    dwap = mm(ya, dpa, ta=True, out_dtype=BF16, i_outer=True, name="w_a_proj_dw", **_TA)
    dya, (x_wbg,) = mm(dpa, wap.T, name="w_a_proj_dx", comm=[("xc", _col_parts(dwbg))])
    dy_ssd, dz, st_sn = ssd_out_bwd(y_ssd, proj, dya, ssd_norm_w, "ssd_out_bwd")
    dxc, ddt, st_ssd = ssd_bwd(xc, dtraw, hsave, dy_ssd, dtb_row, alog_row, dx_row, expm, tri, "ssd_bwd")
    dpre, st_cv = conv_bwd_pre(proj, dxc, convw, cb_row, "conv_bwd_pre")
    dxbc = conv_bwd_in(dpre, convw, "conv_bwd_in")
    dproj = jnp.concatenate([dgates, dz, dxbc, du, ddt.astype(BF16), jnp.zeros((t, INP - P_DT - 128), BF16)], axis=1)
    dwinp, (x_wap, x_cw) = mm(h2, dproj, ta=True, out_dtype=BF16, i_outer=True, name="w_in_dw",
                              comm=[("xc", _row_parts(dwap)), ("xc", _col_parts(st_cv[0:CONV_K]))], **_TA)
    dh2, (x_win,) = mm(dproj, winp.T, tk=5376, name="w_in_dx", comm=[("xc", _col_parts(_unpad_w_in(dwinp)))])
    dx1, df1, st2 = mod_bwd(x1, dh2, dx2, norm_mix, mods, 4, fprev=f1, gk=2, gscale=0.5, name="mod2_bwd")
    dh1, x_f1i, x_f1o = _ffn_bwd(df1, h1, ab1, act1, wf1i.T, wf1o.T, "ffn1")
    gx, st1 = mod_bwd(xt, dh1, dx1, norm_ffn1, mods, 1, name="mod1_bwd")

    dmods = jnp.concatenate([st1[0], st1[1], st2[3], st2[0], st2[1], st3[3], st3[0], st3[1], st_fin[1]])
    part = {"b_ada": dmods, "norm_ffn1": st1[2], "norm_mix": st2[2], "conv_b": st_cv[4], "dt_bias": st_ssd[1, 0:NH],
            "a_log": st_ssd[0, 0:NH], "d_ssd": st_ssd[2, 0:NH], "ssd_norm_w": st_sn[0], "s5_lambda_re": g_lre,
            "s5_lambda_im": g_lim, "s5_b_re": g_bre, "s5_b_im": g_bim, "s5_c_re": g_cre, "s5_c_im": g_cim,
            "s5_d": dd_row, "s5_log_dt": g_ldt, "norm_ffn2": st3[2], "norm_final": st_fin[0],
            "loss": (0.5 / D) * jnp.sum(st_fin[2])}
    zero = {"loss": jnp.zeros((1,), F32)}
    gath = all_gather(_pack(part), "ag_small")
    sg, sd, sm, sv = adamw(_pack({**small, **zero}), gath, _pack({**{k: args["m_" + k] for k in small}, **zero}),
                           _pack({**{k: args["v_" + k] for k in small}, **zero}), "adamw_small")
    sg, sd, sm, sv = _unpack(sg), _unpack(sd), _unpack(sm), _unpack(sv)
    loss = sg["loss"][0]

    dm_loc = lax.dynamic_slice(gath.reshape(N_DEV, -1)[:, 0:N_ADA * D], (0, me * (N_ADA * D // N_DEV)),
                               (N_DEV, N_ADA * D // N_DEV))
    g_ada = ada_bwd(c8.T, dm_loc, "ada_bwd")
    big = {"w_ada": g_ada[None], "w_ffn1_in": x_f1i, "w_ffn1_out": x_f1o, "w_in": x_win, "conv_w": x_cw,
           "w_a_proj": x_wap, "w_b_glu": x_wbg, "w_out": x_wo, "w_ffn2_in": x_f2i, "w_ffn2_out": x_f2o}
    res = {}
    for k, parts in big.items():
        res[k] = adamw(args[k][0], parts, args["m_" + k][0], args["v_" + k][0], "adamw_" + k)

    names = ["w_ada", "b_ada", "norm_ffn1", "w_ffn1_in", "w_ffn1_out", "norm_mix", "w_in", "conv_w", "conv_b", "dt_bias",
             "a_log", "d_ssd", "ssd_norm_w", "w_a_proj", "s5_lambda_re", "s5_lambda_im", "s5_b_re", "s5_b_im", "s5_c_re",
             "s5_c_im", "s5_d", "s5_log_dt", "w_b_glu", "w_out", "norm_ffn2", "w_ffn2_in", "w_ffn2_out", "norm_final"]
    outs = [loss, gx[None]]
    for q, src in enumerate((sg, sd, sm, sv)):
        for k in names:
            if k in res:
                outs.append(res[k][q][None])
            else:
                outs.append(src[k].reshape(args[k].shape))
    return tuple(outs)
```

```python
import functools
import math

import jax
import jax.numpy as jnp
from jax import lax
from jax.experimental import pallas as pl
from jax.experimental.pallas import tpu as pltpu

F32 = jnp.float32
BF16 = jnp.bfloat16
HI = lax.Precision.HIGHEST

N_DEV = 8
D = 2048
D_FF = 5504
D_FFP = 5632
NH = 32
HP = 64
NG = 4
NST = 128
LCH = 128
CONV_DIM = 3072
CONV_K = 4
S5W = 1024
S5NS = 8192
N_ADA = 9
EPS = 1e-6
IN_COLS = 10272
INP = 10752
P_GATES, P_Z, P_XBC, P_U, P_DT = 0, 4096, 6144, 9216, 10240
O_XBC, O_DT, O_U, O_GA, O_GB = 2048, 5120, 5152, 6176, 8224
NEG = -1e30
VMEM_LIMIT = 56 * 1024 * 1024

ADAM_LR, ADAM_B1, ADAM_B2, ADAM_EPS, ADAM_WD, ADAM_STEP = 0.001, 0.9, 0.999, 1e-08, 0.01, 10


def _cp(*sem):
    return pltpu.CompilerParams(dimension_semantics=sem, vmem_limit_bytes=VMEM_LIMIT)


def _tile(dim, pref):
    if dim <= pref or dim % pref == 0:
        return min(dim, pref)
    for t in (2048, 1024, 512, 256, 128):
        if t <= pref and dim % t == 0:
            return t
    return dim


def _vec(w, cb=0):
    return pl.BlockSpec((1, w), lambda *_: (0, cb))


def _row(tm, w, cb=0):
    return pl.BlockSpec((tm, w), lambda i: (i, cb))


def _stats(w):
    return pl.BlockSpec((8, w), lambda *_: (0, 0))


def _sigmoid(x):
    return 1.0 / (1.0 + jnp.exp(-x))


def _softplus(x):
    return jnp.maximum(x, 0.0) + jnp.log1p(jnp.exp(-jnp.abs(x)))


def _peer(k):
    x, y, c = lax.axis_index("x"), lax.axis_index("y"), lax.axis_index("c")
    return (x ^ ((k >> 2) & 1), y ^ ((k >> 1) & 1), c ^ (k & 1))


def _my_id():
    return 4 * lax.axis_index("x") + 2 * lax.axis_index("y") + lax.axis_index("c")


def _comm_out_shape(kind, v):
    shape = {"ag": (N_DEV,) + v.shape, "xc": v.shape, "agc": (v.shape[0], N_DEV * v.shape[1]),
             "xcc": (N_DEV, v.shape[0], v.shape[1] // N_DEV)}[kind]
    return jax.ShapeDtypeStruct(shape, v.dtype)


def _comm_scratch(n):
    return [pltpu.SemaphoreType.DMA((n * N_DEV,)), pltpu.SemaphoreType.DMA((n * N_DEV,))]


class _Comm:
    def __init__(self, kinds, srcs, dsts, send_sems, recv_sems):
        self.items = list(zip(kinds, srcs, dsts))
        self.send_sems, self.recv_sems = send_sems, recv_sems
        x, y, c = lax.axis_index("x"), lax.axis_index("y"), lax.axis_index("c")
        self.me = 4 * x + 2 * y + c
        self.sib = (x, y, 1 - c)
        self.chips = [(1 - x, y), (x, 1 - y), (1 - x, 1 - y)]
        self.c = c

    @staticmethod
    def _id(p):
        return 4 * p[0] + 2 * p[1] + p[2]

    def _src(self, q, d):
        kind, src, _ = self.items[q]
        if kind == "xc":
            return src.at[d]
        if kind == "xcc":
            w = src.shape[1] // N_DEV
            return src.at[:, pl.ds(pl.multiple_of(d * w, 128), w)]
        return src

    def _slot(self, q, d):
        kind, _, dst = self.items[q]
        if kind == "agc":
            w = dst.shape[1] // N_DEV
            return dst.at[:, pl.ds(pl.multiple_of(d * w, 128), w)]
        return dst.at[d]

    def _push(self, q, k, src, slot, to):
        return pltpu.make_async_remote_copy(
            src_ref=src, dst_ref=self._slot(q, slot), send_sem=self.send_sems.at[q * N_DEV + k],
            recv_sem=self.recv_sems.at[q * N_DEV + k], device_id=to, device_id_type=pl.DeviceIdType.MESH)

    def _local(self, q):
        return pltpu.make_async_copy(self._src(q, self.me), self._slot(q, self.me), self.send_sems.at[q * N_DEV])

    def _direct(self, q):
        kind = self.items[q][0]
        if kind in ("xc", "xcc"):
            out = []
            for k in range(1, N_DEV):
                p = _peer(k)
                out.append((k, self._push(q, k, self._src(q, self._id(p)), self.me, p)))
            return out
        src = self.items[q][1]
        out = [(1, self._push(q, 1, src, self.me, self.sib))]
        for j, chip in enumerate(self.chips):
            out.append((2 + j, self._push(q, 2 + j, src, self.me, (*chip, self.c))))
        return out

    def _forwards(self, q):
        out = []
        for j, chip in enumerate(self.chips):
            slot = self._id((*chip, self.c))
            out.append((2 + j, 5 + j, self._push(q, 5 + j, self._slot(q, slot), slot, self.sib)))
        return out

    def start(self):
        for q in range(len(self.items)):
            self._local(q).start()
            for _, cp in self._direct(q):
                cp.start()

    def forward(self):
        for q, (kind, _, _) in enumerate(self.items):
            if kind not in ("ag", "agc"):
                continue
            for k_in, _, fwd in self._forwards(q):
                self._push(q, k_in, self._slot(q, self.me), self.me, self.sib).wait_recv()
                fwd.start()

    def finish(self):
        for q, (kind, _, _) in enumerate(self.items):
            self._local(q).wait()
            if kind in ("xc", "xcc"):
                for _, cp in self._direct(q):
                    cp.wait()
                continue
            for k, cp in self._direct(q):
                cp.wait_send()
                if k == 1:
                    cp.wait_recv()
            for _, _, fwd in self._forwards(q):
                fwd.wait()


def comm_call(items, name):
    kinds = [k for k, _ in items]
    n = len(items)

    def body(*refs):
        cm = _Comm(kinds, refs[:n], refs[n:2 * n], refs[2 * n], refs[2 * n + 1])
        cm.start()
        cm.forward()
        cm.finish()

    return pl.pallas_call(
        body, name=name,
        in_specs=[pl.BlockSpec(memory_space=pl.ANY)] * n, out_specs=[pl.BlockSpec(memory_space=pl.ANY)] * n,
        out_shape=[_comm_out_shape(k, v) for k, v in items], scratch_shapes=_comm_scratch(n),
    )(*[v for _, v in items])


def all_gather(v, name):
    return comm_call([("ag", v)], name)[0]


def _pcall(body, args, *, name, grid, in_specs, out_specs, out_shape, scratch_shapes=(), sem, comm=(), fwd=0.6):
    nc, n_in, n_out = len(comm), len(in_specs), len(out_shape)
    if not nc:
        return pl.pallas_call(body, name=name, grid=grid, in_specs=list(in_specs), out_specs=list(out_specs),
                              out_shape=list(out_shape), scratch_shapes=list(scratch_shapes),
                              compiler_params=_cp(*sem))(*args)
    kinds = [k for k, _ in comm]
    steps = math.prod(grid)
    fwd_step = min(int(fwd * steps), steps - 1)

    def carried(*refs):
        ins, csrc = refs[:n_in], refs[n_in:n_in + nc]
        outs, cdst = refs[n_in + nc:n_in + nc + n_out], refs[n_in + nc + n_out:n_in + 2 * nc + n_out]
        scr = refs[n_in + 2 * nc + n_out:]
        cm = _Comm(kinds, csrc, cdst, scr[-2], scr[-1])
        step = 0
        for d, g in enumerate(grid):
            step = step * g + pl.program_id(d)

        @pl.when(step == 0)
        def _():
            cm.start()

        body(*ins, *outs, *scr[:-2])

        @pl.when(step == fwd_step)
        def _():
            cm.forward()

        @pl.when(step == steps - 1)
        def _():
            cm.finish()

    hbm = pl.BlockSpec(memory_space=pl.ANY)
    out = pl.pallas_call(
        carried, name=name, grid=grid, in_specs=list(in_specs) + [hbm] * nc, out_specs=list(out_specs) + [hbm] * nc,
        out_shape=list(out_shape) + [_comm_out_shape(k, v) for k, v in comm],
        scratch_shapes=list(scratch_shapes) + _comm_scratch(nc), compiler_params=_cp(*(("arbitrary",) * len(grid))),
    )(*args, *[v for _, v in comm])
    return list(out[:n_out]), list(out[n_out:])


def mm(a, b, *, ta=False, out_dtype=F32, tm=512, tn=1024, tk=2048, i_outer=False, a_halves=False, b_halves=False,
       name, comm=()):
    if a_halves:
        m, kd = a.shape[1], 2 * a.shape[2]
    elif ta:
        kd, m = a.shape
    else:
        m, kd = a.shape
    kd2, n = (b.shape[1], 2 * b.shape[2]) if b_halves else b.shape
    assert kd == kd2 and not (ta and a_halves), (a.shape, b.shape, ta)
    tm, tn, tk = _tile(m, tm), _tile(n // 2 if b_halves else n, tn), _tile(kd // 2 if a_halves else kd, tk)
    nk = kd // tk
    nkh, njh = nk // 2, n // tn // 2
    grid = (m // tm, n // tn, nk) if i_outer else (n // tn, m // tm, nk)
    dims = (((0,) if ta else (1,), (0,)), ((), ()))

    def ix(f):
        return (lambda i, j, k: f(i, j, k)) if i_outer else (lambda j, i, k: f(i, j, k))

    def body(a_ref, b_ref, o_ref, *scr):
        p = lax.dot_general(a_ref[...], b_ref[...], dims, preferred_element_type=F32)
        if nk == 1:
            o_ref[...] = p.astype(o_ref.dtype)
        else:
            acc = scr[0]
            k = pl.program_id(2)

            @pl.when(k == 0)
            def _():
                acc[...] = p

            @pl.when(k > 0)
            def _():
                acc[...] += p

            @pl.when(k == nk - 1)
            def _():
                o_ref[...] = acc[...].astype(o_ref.dtype)

    if a_halves:
        a_spec = pl.BlockSpec((None, tm, tk), ix(lambda i, j, k: (k // nkh, i, k % nkh)))
    elif ta:
        a_spec = pl.BlockSpec((tk, tm), ix(lambda i, j, k: (k, i)))
    else:
        a_spec = pl.BlockSpec((tm, tk), ix(lambda i, j, k: (i, k)))
    if b_halves:
        b_spec = pl.BlockSpec((None, tk, tn), ix(lambda i, j, k: (j // njh, k, j % njh)))
    else:
        b_spec = pl.BlockSpec((tk, tn), ix(lambda i, j, k: (k, j)))
    out = _pcall(body, (a, b), name=name, grid=grid, in_specs=[a_spec, b_spec],
                 out_specs=[pl.BlockSpec((tm, tn), ix(lambda i, j, k: (i, j)))],
                 out_shape=[jax.ShapeDtypeStruct((m, n), out_dtype)],
                 scratch_shapes=[pltpu.VMEM((tm, tn), F32)] if nk > 1 else [],
                 sem=("parallel", "parallel", "arbitrary"), comm=comm)
    return (out[0][0], out[1]) if comm else out[0]


def ffn_in_act(h, w, name, comm=(), fwd=0.6):
    t = h.shape[0]
    tm, tn = _tile(t, 512), 512
    nj = D_FFP // tn

    def body(h_ref, wa_ref, wb_ref, ab_ref, act_ref):
        hv = h_ref[...]
        pa = _dot(hv, wa_ref[...])
        pb = _dot(hv, wb_ref[...])
        ab_ref[0] = pa.astype(BF16)
        ab_ref[1] = pb.astype(BF16)
        act_ref[...] = (pa * _sigmoid(pa) * pb).astype(BF16)

    out = _pcall(body, (h, w, w), name=name, grid=(nj, t // tm),
                 in_specs=[pl.BlockSpec((tm, D), lambda j, i: (i, 0)), pl.BlockSpec((D, tn), lambda j, i: (0, j)),
                           pl.BlockSpec((D, tn), lambda j, i: (0, nj + j))],
                 out_specs=[pl.BlockSpec((2, tm, tn), lambda j, i: (0, i, j)), pl.BlockSpec((tm, tn), lambda j, i: (i, j))],
                 out_shape=[jax.ShapeDtypeStruct((2, t, D_FFP), BF16), jax.ShapeDtypeStruct((t, D_FFP), BF16)],
                 sem=("parallel", "parallel"), comm=comm, fwd=fwd)
    return (out[0][0], out[0][1], out[1]) if comm else (out[0], out[1])


def ffn_dab(df, w_out_t, ab, name):
    t = df.shape[0]
    tm, tn = _tile(t, 1024), 512

    def body(d_ref, w_ref, ab_ref, o_ref):
        dv = _dot(d_ref[...], w_ref[...])
        a = ab_ref[0].astype(F32)
        b = ab_ref[1].astype(F32)
        s = _sigmoid(a)
        o_ref[0] = (dv * b * (s * (1.0 + a * (1.0 - s)))).astype(BF16)
        o_ref[1] = (dv * (a * s)).astype(BF16)

    return _pcall(body, (df, w_out_t, ab), name=name, grid=(D_FFP // tn, t // tm),
                  in_specs=[pl.BlockSpec((tm, D), lambda j, i: (i, 0)), pl.BlockSpec((D, tn), lambda j, i: (0, j)),
                            pl.BlockSpec((2, tm, tn), lambda j, i: (0, i, j))],
                  out_specs=[pl.BlockSpec((2, tm, tn), lambda j, i: (0, i, j))],
                  out_shape=[jax.ShapeDtypeStruct((2, t, D_FFP), BF16)], sem=("parallel", "parallel"))[0]


def ada_fwd(c8, w_loc, b_loc, name):
    n = w_loc.shape[1]
    tn = 256

    def body(c_ref, w_ref, b_ref, o_ref):
        cv = c_ref[...]
        ca = cv * _sigmoid(cv)
        o_ref[...] = jnp.dot(ca, w_ref[...], precision=HI, preferred_element_type=F32) + b_ref[...]

    return pl.pallas_call(
        body, name=name, grid=(n // tn,),
        in_specs=[pl.BlockSpec((N_DEV, D), lambda j: (0, 0)), pl.BlockSpec((D, tn), lambda j: (0, j)),
                  pl.BlockSpec((1, tn), lambda j: (0, j))],
        out_specs=pl.BlockSpec((N_DEV, tn), lambda j: (0, j)),
        out_shape=jax.ShapeDtypeStruct((N_DEV, n), F32), compiler_params=_cp("parallel"),
    )(c8, w_loc, b_loc)


def ada_bwd(c8t, dm_loc, name):
    n = dm_loc.shape[1]
    tn = 256

    def body(c_ref, d_ref, o_ref):
        cv = c_ref[...]
        ca = cv * _sigmoid(cv)
        o_ref[...] = jnp.dot(ca, d_ref[...], precision=HI, preferred_element_type=F32)

    return pl.pallas_call(
        body, name=name, grid=(n // tn,),
        in_specs=[pl.BlockSpec((D, N_DEV), lambda j: (0, 0)), pl.BlockSpec((N_DEV, tn), lambda j: (0, j))],
        out_specs=pl.BlockSpec((D, tn), lambda j: (0, j)),
        out_shape=jax.ShapeDtypeStruct((D, n), F32), compiler_params=_cp("parallel"),
    )(c8t, dm_loc)


def mod_fwd(x, nw, mods, shk, sck, *, f=None, gk=None, gscale=1.0, name):
    t = x.shape[0]
    tm = min(256, t)
    res = f is not None

    def body(*refs):
        if res:
            x_ref, f_ref, g_ref, nw_ref, sh_ref, sc_ref, x1_ref, h_ref = refs
            xv = x_ref[...] + (gscale * g_ref[...]) * f_ref[...].astype(F32)
            x1_ref[...] = xv
        else:
            x_ref, nw_ref, sh_ref, sc_ref, h_ref = refs
            xv = x_ref[...]
        r = lax.rsqrt(jnp.mean(xv * xv, axis=-1, keepdims=True) + EPS)
        h_ref[...] = ((xv * r * nw_ref[...]) * (1.0 + sc_ref[...]) + sh_ref[...]).astype(BF16)

    ins = [x] + ([f, mods] if res else []) + [nw, mods, mods]
    specs = [_row(tm, D)] + ([_row(tm, D), _vec(D, gk)] if res else []) + [_vec(D), _vec(D, shk), _vec(D, sck)]
    outs = ([jax.ShapeDtypeStruct((t, D), F32)] if res else []) + [jax.ShapeDtypeStruct((t, D), BF16)]
    ospecs = ([_row(tm, D)] if res else []) + [_row(tm, D)]
    out = pl.pallas_call(body, name=name, grid=(t // tm,), in_specs=specs, out_specs=ospecs, out_shape=outs,
                         compiler_params=_cp("parallel"))(*ins)
    return out if res else out[0]


def final_fwd_bwd(x2, f3, mods, nf, tgt, name):
    t = x2.shape[0]
    tm = min(128, t)

    def body(x_ref, f_ref, g_ref, nf_ref, t_ref, dx_ref, df_ref, st_ref):
        @pl.when(pl.program_id(0) == 0)
        def _():
            st_ref[...] = jnp.zeros_like(st_ref)

        g = 0.5 * g_ref[...]
        fv = f_ref[...].astype(F32)
        xv = x_ref[...] + g * fv
        r = lax.rsqrt(jnp.mean(xv * xv, axis=-1, keepdims=True) + EPS)
        xh = xv * r
        nfv = nf_ref[...]
        e = xh * nfv - t_ref[...]
        st_ref[2:3, :] += jnp.sum(e * e, axis=0, keepdims=True)
        dy = e * (1.0 / D)
        st_ref[0:1, :] += jnp.sum(dy * xh, axis=0, keepdims=True)
        dxh = dy * nfv
        dx = r * (dxh - xh * jnp.mean(dxh * xh, axis=-1, keepdims=True))
        dx_ref[...] = dx
        df_ref[...] = (g * dx).astype(BF16)
        st_ref[1:2, :] += 0.5 * jnp.sum(fv * dx, axis=0, keepdims=True)

    return pl.pallas_call(
        body, name=name, grid=(t // tm,),
        in_specs=[_row(tm, D), _row(tm, D), _vec(D, 8), _vec(D), _row(tm, D)],
        out_specs=[_row(tm, D), _row(tm, D), _stats(D)],
        out_shape=[jax.ShapeDtypeStruct((t, D), F32), jax.ShapeDtypeStruct((t, D), BF16),
                   jax.ShapeDtypeStruct((8, D), F32)],
        compiler_params=_cp("arbitrary"),
    )(x2, f3, mods, nf, tgt)


def mod_bwd(x_in, dh, dx_out, nw, mods, sck, *, fprev=None, gk=None, gscale=1.0, name):
    t = x_in.shape[0]
    tm = min(128, t)
    gate = fprev is not None

    def body(*refs):
        if gate:
            x_ref, dh_ref, dxo_ref, nw_ref, sc_ref, f_ref, g_ref, dx_ref, df_ref, st_ref = refs
        else:
            x_ref, dh_ref, dxo_ref, nw_ref, sc_ref, dx_ref, st_ref = refs

        @pl.when(pl.program_id(0) == 0)
        def _():
            st_ref[...] = jnp.zeros_like(st_ref)

        xv = x_ref[...]
        dhv = dh_ref[...].astype(F32)
        r = lax.rsqrt(jnp.mean(xv * xv, axis=-1, keepdims=True) + EPS)
        xh = xv * r
        nwv = nw_ref[...]
        st_ref[0:1, :] += jnp.sum(dhv, axis=0, keepdims=True)
        st_ref[1:2, :] += jnp.sum(dhv * (xh * nwv), axis=0, keepdims=True)
        dn = dhv * (1.0 + sc_ref[...])
        st_ref[2:3, :] += jnp.sum(dn * xh, axis=0, keepdims=True)
        dxh = dn * nwv
        dx = dxo_ref[...] + r * (dxh - xh * jnp.mean(dxh * xh, axis=-1, keepdims=True))
        dx_ref[...] = dx
        if gate:
            df_ref[...] = ((gscale * g_ref[...]) * dx).astype(BF16)
            st_ref[3:4, :] += gscale * jnp.sum(f_ref[...].astype(F32) * dx, axis=0, keepdims=True)

    ins = [x_in, dh, dx_out, nw, mods] + ([fprev, mods] if gate else [])
    specs = [_row(tm, D), _row(tm, D), _row(tm, D), _vec(D), _vec(D, sck)] + ([_row(tm, D), _vec(D, gk)] if gate else [])
    outs = [jax.ShapeDtypeStruct((t, D), F32)] + ([jax.ShapeDtypeStruct((t, D), BF16)] if gate else []) + \
        [jax.ShapeDtypeStruct((8, D), F32)]
    ospecs = [_row(tm, D)] + ([_row(tm, D)] if gate else []) + [_stats(D)]
    return pl.pallas_call(body, name=name, grid=(t // tm,), in_specs=specs, out_specs=ospecs, out_shape=outs,
                          compiler_params=_cp("arbitrary"))(*ins)


def _conv_pre(cur, prev8, w, b, tm):
    full = jnp.concatenate([prev8, cur], axis=0)
    pre = b + w[3:4, :] * cur
    for k in range(CONV_K - 1):
        s = CONV_K - 1 - k
        pre = pre + w[k:k + 1, :] * pltpu.roll(full, s, 0)[8:8 + tm, :]
    return pre


def conv_fwd(proj, cw_full, cb_full, name):
    t = proj.shape[0]
    tm = min(256, t)
    cwid = 1024
    cb0 = P_XBC // cwid

    def body(x_ref, p_ref, w_ref, b_ref, o_ref):
        i = pl.program_id(1)
        prev8 = jnp.where(i == 0, 0.0, p_ref[...].astype(F32)[8:16])
        pre = _conv_pre(x_ref[...].astype(F32), prev8, w_ref[...], b_ref[...], tm)
        o_ref[...] = pre * _sigmoid(pre)

    return pl.pallas_call(
        body, name=name, grid=(CONV_DIM // cwid, t // tm),
        in_specs=[pl.BlockSpec((tm, cwid), lambda j, i: (i, cb0 + j)),
                  pl.BlockSpec((16, cwid), lambda j, i: (jnp.maximum(i * (tm // 16) - 1, 0), cb0 + j)),
                  pl.BlockSpec((CONV_K, cwid), lambda j, i: (0, j)), pl.BlockSpec((1, cwid), lambda j, i: (0, j))],
        out_specs=pl.BlockSpec((tm, cwid), lambda j, i: (i, j)),
        out_shape=jax.ShapeDtypeStruct((t, CONV_DIM), F32), compiler_params=_cp("parallel", "parallel"),
    )(proj, proj, cw_full, cb_full)


def conv_bwd_pre(proj, dxc, cw_full, cb_full, name):
    t = proj.shape[0]
    tm = min(256, t)
    cwid = 1024
    cb0 = P_XBC // cwid

    def body(x_ref, p_ref, d_ref, w_ref, b_ref, o_ref, st_ref):
        i = pl.program_id(1)

        @pl.when(i == 0)
        def _():
            st_ref[...] = jnp.zeros_like(st_ref)

        cur = x_ref[...].astype(F32)
        prev8 = jnp.where(i == 0, 0.0, p_ref[...].astype(F32)[8:16])
        pre = _conv_pre(cur, prev8, w_ref[...], b_ref[...], tm)
        s = _sigmoid(pre)
        dpre = d_ref[...] * (s * (1.0 + pre * (1.0 - s)))
        o_ref[...] = dpre
        st_ref[4:5, :] += jnp.sum(dpre, axis=0, keepdims=True)
        st_ref[3:4, :] += jnp.sum(dpre * cur, axis=0, keepdims=True)
        full = jnp.concatenate([prev8, cur], axis=0)
        for k in range(CONV_K - 1):
            sft = CONV_K - 1 - k
            st_ref[k:k + 1, :] += jnp.sum(dpre * pltpu.roll(full, sft, 0)[8:8 + tm, :], axis=0, keepdims=True)

    return pl.pallas_call(
        body, name=name, grid=(CONV_DIM // cwid, t // tm),
        in_specs=[pl.BlockSpec((tm, cwid), lambda j, i: (i, cb0 + j)),
                  pl.BlockSpec((16, cwid), lambda j, i: (jnp.maximum(i * (tm // 16) - 1, 0), cb0 + j)),
                  pl.BlockSpec((tm, cwid), lambda j, i: (i, j)),
                  pl.BlockSpec((CONV_K, cwid), lambda j, i: (0, j)), pl.BlockSpec((1, cwid), lambda j, i: (0, j))],
        out_specs=[pl.BlockSpec((tm, cwid), lambda j, i: (i, j)), pl.BlockSpec((8, cwid), lambda j, i: (0, j))],
        out_shape=[jax.ShapeDtypeStruct((t, CONV_DIM), F32), jax.ShapeDtypeStruct((8, CONV_DIM), F32)],
        compiler_params=_cp("parallel", "arbitrary"),
    )(proj, proj, dxc, cw_full, cb_full)


def conv_bwd_in(dpre, cw_full, name):
    t = dpre.shape[0]
    tm = min(256, t)
    cwid = 1024
    nt = t // tm

    def body(d_ref, n_ref, w_ref, o_ref):
        i = pl.program_id(1)
        cur = d_ref[...]
        nxt = jnp.where(i == nt - 1, 0.0, n_ref[...])
        full = jnp.concatenate([cur, nxt], axis=0)
        w = w_ref[...]
        acc = w[3:4, :] * cur
        for k in range(CONV_K - 1):
            s = CONV_K - 1 - k
            acc = acc + w[k:k + 1, :] * pltpu.roll(full, tm + 8 - s, 0)[0:tm, :]
        o_ref[...] = acc.astype(BF16)

    return pl.pallas_call(
        body, name=name, grid=(CONV_DIM // cwid, nt),
        in_specs=[pl.BlockSpec((tm, cwid), lambda j, i: (i, j)),
                  pl.BlockSpec((8, cwid), lambda j, i: (jnp.minimum((i + 1) * (tm // 8), t // 8 - 1), j)),
                  pl.BlockSpec((CONV_K, cwid), lambda j, i: (0, j))],
        out_specs=pl.BlockSpec((tm, cwid), lambda j, i: (i, j)),
        out_shape=jax.ShapeDtypeStruct((t, CONV_DIM), BF16), compiler_params=_cp("parallel", "parallel"),
    )(dpre, dpre, cw_full)


def _nt(a, b):
    return lax.dot_general(a, b, (((1,), (1,)), ((), ())), preferred_element_type=F32)


def _dot(a, b):
    return jnp.dot(a, b, preferred_element_type=F32)


def _head_lanes():
    return lax.broadcasted_iota(jnp.int32, (1, 128), 1) < NH


def _expand_heads(x, e3):
    x = jnp.where(_head_lanes(), x, 0.0)
    hi = x.astype(BF16).astype(F32)
    r1 = x - hi
    mid = r1.astype(BF16).astype(F32)
    packed = hi + pltpu.roll(mid, NH, 1) + pltpu.roll(r1 - mid, 2 * NH, 1)
    return _dot(packed.astype(BF16), e3)


def _reduce_heads(v, e3):
    hi = v.astype(BF16)
    lo = (v - hi.astype(F32)).astype(BF16)
    return jnp.where(_head_lanes(), _nt(hi, e3) + _nt(lo, e3), 0.0)


def _ssd_common(dt_ref, dtb_ref, al_ref, exp_ref, tri_ref):
    a_row = jnp.where(_head_lanes(), -jnp.exp(al_ref[...]), 0.0)
    zraw = dt_ref[...] + dtb_ref[...]
    dtv = _softplus(zraw)
    cs = jnp.dot(tri_ref[...], dtv * a_row, precision=HI, preferred_element_type=F32)
    e3 = exp_ref[...]
    return a_row, zraw, dtv, cs, _expand_heads(cs, e3), _expand_heads(dtv, e3)


def ssd_fwd(xc, proj, dtb_row, alog_row, dx_row, expm, tri, name):
    t = xc.shape[0]
    nc = t // LCH

    def body(xs_ref, bm_ref, cm_ref, dt_ref, dtb_ref, al_ref, dxr_ref, exp_ref, tri_ref, y_ref, hs_ref, h_scr):
        @pl.when(pl.program_id(0) == 0)
        def _():
            h_scr[...] = jnp.zeros_like(h_scr)

        _, _, _, cs, csx, dtx = _ssd_common(dt_ref, dtb_ref, al_ref, exp_ref, tri_ref)
        cst = cs.T
        csl = csx[LCH - 1:LCH, :]
        xs = xs_ref[...]
        xd = xs * dtx
        xdw = xd * jnp.exp(csl - csx)
        ecs = jnp.exp(csx)
        ecl = jnp.exp(csl)
        tril = lax.broadcasted_iota(jnp.int32, (LCH, LCH), 0) >= lax.broadcasted_iota(jnp.int32, (LCH, LCH), 1)
        hs_ref[...] = h_scr[...]
        for g in range(NG):
            gc = slice(g * 512, (g + 1) * 512)
            bm = bm_ref[:, g * NST:(g + 1) * NST]
            cmb = cm_ref[:, g * NST:(g + 1) * NST].astype(BF16)
            gm = _nt(cmb, bm.astype(BF16))
            hg = h_scr[:, gc]
            yo = _dot(cmb, hg.astype(BF16)) * ecs[:, gc]
            st = _dot(bm.T.astype(BF16), xdw[:, gc].astype(BF16))
            for r in range(8):
                h = g * 8 + r
                hc = slice(h * HP, (h + 1) * HP)
                seg = cs[:, h:h + 1] - cst[h:h + 1, :]
                m = (gm * jnp.exp(jnp.where(tril, seg, NEG))).astype(BF16)
                yd = _dot(m, xd[:, hc].astype(BF16))
                y_ref[:, hc] = yd + yo[:, r * HP:(r + 1) * HP] + dxr_ref[:, hc] * xs[:, hc]
            h_scr[:, gc] = ecl[:, gc] * hg + st

    return pl.pallas_call(
        body, name=name, grid=(nc,),
        in_specs=[pl.BlockSpec((LCH, 2048), lambda c: (c, 0)), pl.BlockSpec((LCH, 512), lambda c: (c, 4)),
                  pl.BlockSpec((LCH, 512), lambda c: (c, 5)), pl.BlockSpec((LCH, 128), lambda c: (c, 0)),
                  _vec(128), _vec(128), _vec(2048), pl.BlockSpec((128, 2048), lambda c: (0, 0)),
                  pl.BlockSpec((LCH, LCH), lambda c: (0, 0))],
        out_specs=[pl.BlockSpec((LCH, 2048), lambda c: (c, 0)), pl.BlockSpec((None, NST, 2048), lambda c: (c, 0, 0))],
        out_shape=[jax.ShapeDtypeStruct((t, 2048), F32), jax.ShapeDtypeStruct((nc, NST, 2048), F32)],
        scratch_shapes=[pltpu.VMEM((NST, 2048), F32)],
        compiler_params=_cp("arbitrary"),
    )(xc, xc, xc, proj, dtb_row, alog_row, dx_row, expm, tri)


def ssd_bwd(xc, proj, hsave, dy, dtb_row, alog_row, dx_row, expm, tri, name):
    t = xc.shape[0]
    nc = t // LCH

    def body(xs_ref, bm_ref, cm_ref, dt_ref, hs_ref, dy_ref, dtb_ref, al_ref, dxr_ref, exp_ref, tri_ref,
             dxc_ref, ddt_ref, st_ref, dh_scr, dxd_scr, dcsx_scr):
        @pl.when(pl.program_id(0) == 0)
        def _():
            dh_scr[...] = jnp.zeros_like(dh_scr)
            st_ref[...] = jnp.zeros_like(st_ref)

        a_row, zraw, dtv, cs, csx, dtx = _ssd_common(dt_ref, dtb_ref, al_ref, exp_ref, tri_ref)
        e = exp_ref[...]
        cst = cs.T
        csl = csx[LCH - 1:LCH, :]
        xs = xs_ref[...]
        xd = xs * dtx
        wend = jnp.exp(csl - csx)
        xdw = xd * wend
        ecs = jnp.exp(csx)
        ecl = jnp.exp(csl)
        ri = lax.broadcasted_iota(jnp.int32, (LCH, LCH), 0)
        ci = lax.broadcasted_iota(jnp.int32, (LCH, LCH), 1)
        tril = ri >= ci
        triu = ri <= ci
        lane = lax.broadcasted_iota(jnp.int32, (1, 128), 1)
        dyv = dy_ref[...]
        dxr = dxr_ref[...]
        st_ref[2:3, :] += _reduce_heads(jnp.sum(dyv * xs, axis=0, keepdims=True), e)
        dcs = jnp.zeros((LCH, 128), F32)
        for g in range(NG):
            gc = slice(g * 512, (g + 1) * 512)
            bmb = bm_ref[:, g * NST:(g + 1) * NST].astype(BF16)
            cm = cm_ref[:, g * NST:(g + 1) * NST]
            cmb = cm.astype(BF16)
            hg = hs_ref[:, gc]
            hgb = hg.astype(BF16)
            dhc = dh_scr[:, gc]
            dhcb = dhc.astype(BF16)
            dyg = dyv[:, gc]
            yo = _dot(cmb, hgb) * ecs[:, gc]
            dq = (dyg * ecs[:, gc]).astype(BF16)
            dcm = _nt(dq, hgb)
            dh_yo = _dot(cm.T.astype(BF16), dq)
            dxdw = _dot(bmb, dhcb)
            dbm = _nt(xdw[:, gc].astype(BF16), dhcb)
            tt = dxdw * xdw[:, gc]
            dcsx_g = dyg * yo - tt
            dcsl_g = jnp.sum(tt, axis=0, keepdims=True) + jnp.sum(dhc * hg, axis=0, keepdims=True) * ecl[:, gc]
            dxd_scr[:, gc] = dxdw * wend[:, gc]
            dh_scr[:, gc] = ecl[:, gc] * dhc + dh_yo
            gm = _nt(cmb, bmb)
            gmt = _nt(bmb, cmb)
            dg = jnp.zeros((LCH, LCH), F32)
            dgt = jnp.zeros((LCH, LCH), F32)
            for r in range(8):
                h = g * 8 + r
                hc = slice(h * HP, (h + 1) * HP)
                seg = cs[:, h:h + 1] - cst[h:h + 1, :]
                lm = jnp.exp(jnp.where(tril, seg, NEG))
                lmt = jnp.exp(jnp.where(triu, -seg, NEG))
                mm_ = gm * lm
                mmt = gmt * lmt
                xdh = xd[:, hc].astype(BF16)
                dyh = dyv[:, hc].astype(BF16)
                dm = _nt(dyh, xdh)
                dmt = _nt(xdh, dyh)
                dxd_scr[:, hc] += _dot(mmt.astype(BF16), dyh)
                rs = jnp.sum(dm * mm_, axis=1, keepdims=True) - jnp.sum(dmt * mmt, axis=1, keepdims=True)
                dcs = dcs + rs * jnp.where(lane == h, 1.0, 0.0)
                dg = dg + dm * lm
                dgt = dgt + dmt * lmt
            dcm = dcm + _dot(dg.astype(BF16), bmb)
            dbm = dbm + _dot(dgt.astype(BF16), cmb)
            dxc_ref[:, 2048 + g * NST:2048 + (g + 1) * NST] = dbm
            dxc_ref[:, 2560 + g * NST:2560 + (g + 1) * NST] = dcm
            dcsx_scr[:, gc] = dcsx_g
            dcsx_scr[LCH - 1:LCH, gc] += dcsl_g
        dxd = dxd_scr[...]
        dxc_ref[:, 0:2048] = dxr * dyv + dxd * dtx
        ddtv = _reduce_heads(dxd * xs, e)
        dcs = dcs + _reduce_heads(dcsx_scr[...], e)
        dda =lax.dot_general(tri_ref[...], dcs, (((0,), (0,)), ((), ())), precision=HI, preferred_element_type=F32)
        ddtv = ddtv + dda * a_row
        st_ref[0:1, :] += jnp.sum(dda * dtv, axis=0, keepdims=True) * a_row
        ddt = ddtv * _sigmoid(zraw)
        ddt_ref[...] = ddt
        st_ref[1:2, :] += jnp.sum(ddt, axis=0, keepdims=True)

    rc = lambda c: nc - 1 - c
    return pl.pallas_call(
        body, name=name, grid=(nc,),
        in_specs=[pl.BlockSpec((LCH, 2048), lambda c: (rc(c), 0)), pl.BlockSpec((LCH, 512), lambda c: (rc(c), 4)),
                  pl.BlockSpec((LCH, 512), lambda c: (rc(c), 5)),
                  pl.BlockSpec((LCH, 128), lambda c: (rc(c), 0)),
                  pl.BlockSpec((None, NST, 2048), lambda c: (rc(c), 0, 0)),
                  pl.BlockSpec((LCH, 2048), lambda c: (rc(c), 0)),
                  _vec(128), _vec(128), _vec(2048), pl.BlockSpec((128, 2048), lambda c: (0, 0)),
                  pl.BlockSpec((LCH, LCH), lambda c: (0, 0))],
        out_specs=[pl.BlockSpec((LCH, CONV_DIM), lambda c: (rc(c), 0)), pl.BlockSpec((LCH, 128), lambda c: (rc(c), 0)),
                   _stats(128)],
        out_shape=[jax.ShapeDtypeStruct((t, CONV_DIM), F32), jax.ShapeDtypeStruct((t, 128), F32),
                   jax.ShapeDtypeStruct((8, 128), F32)],
        scratch_shapes=[pltpu.VMEM((NST, 2048), F32), pltpu.VMEM((LCH, 2048), F32), pltpu.VMEM((LCH, 2048), F32)],
        compiler_params=_cp("arbitrary"),
    )(xc, xc, xc, proj, hsave, dy, dtb_row, alog_row, dx_row, expm, tri)


def ssd_out_fwd(y, proj, nw, name):
    t = y.shape[0]
    tm = min(256, t)

    def body(y_ref, z_ref, nw_ref, o_ref):
        for g in range(NG):
            gc = slice(g * 512, (g + 1) * 512)
            z = z_ref[:, gc].astype(F32)
            yz = y_ref[:, gc] * (z * _sigmoid(z))
            r = lax.rsqrt(jnp.mean(yz * yz, axis=-1, keepdims=True) + EPS)
            o_ref[:, gc] = (yz * r * nw_ref[:, gc]).astype(BF16)

    return pl.pallas_call(body, name=name, grid=(t // tm,),
                          in_specs=[_row(tm, 2048), _row(tm, 2048, P_Z // 2048), _vec(2048)],
                          out_specs=_row(tm, 2048), out_shape=jax.ShapeDtypeStruct((t, 2048), BF16),
                          compiler_params=_cp("parallel"))(y, proj, nw)


def ssd_out_bwd(y, proj, dya, nw, name):
    t = y.shape[0]
    tm = min(256, t)

    def body(y_ref, z_ref, d_ref, nw_ref, dy_ref, dz_ref, st_ref):
        @pl.when(pl.program_id(0) == 0)
        def _():
            st_ref[...] = jnp.zeros_like(st_ref)

        for g in range(NG):
            gc = slice(g * 512, (g + 1) * 512)
            z = z_ref[:, gc].astype(F32)
            yv = y_ref[:, gc]
            s = _sigmoid(z)
            sz = z * s
            yz = yv * sz
            r = lax.rsqrt(jnp.mean(yz * yz, axis=-1, keepdims=True) + EPS)
            yzn = yz * r
            dv = d_ref[:, gc].astype(F32)
            st_ref[0:1, gc] += jnp.sum(dv * yzn, axis=0, keepdims=True)
            dyn = dv * nw_ref[:, gc]
            dyz = r * (dyn - yzn * jnp.mean(dyn * yzn, axis=-1, keepdims=True))
            dy_ref[:, gc] = dyz * sz
            dz_ref[:, gc] = (dyz * yv * (s * (1.0 + z * (1.0 - s)))).astype(BF16)

    return pl.pallas_call(
        body, name=name, grid=(t // tm,),
        in_specs=[_row(tm, 2048), _row(tm, 2048, P_Z // 2048), _row(tm, 2048), _vec(2048)],
        out_specs=[_row(tm, 2048), _row(tm, 2048), _stats(2048)],
        out_shape=[jax.ShapeDtypeStruct((t, 2048), F32), jax.ShapeDtypeStruct((t, 2048), BF16),
                   jax.ShapeDtypeStruct((8, 2048), F32)],
        compiler_params=_cp("arbitrary"))(y, proj, dya, nw)


def _cstep(ar, ai, sr, si, br, bi):
    return ar * sr - ai * si + br, ar * si + ai * sr + bi


def _halves(v):
    return (v[0:8, 0:512], v[0:8, 512:1024]), (v[8:16, 0:512], v[8:16, 512:1024])


def _slab(r1, i1, r2, i2):
    return jnp.concatenate([jnp.concatenate([r1, i1], axis=1), jnp.concatenate([r2, i2], axis=1)], axis=0).astype(BF16)


def _local_ends(x_ref, nslab, ar, ai, sr_scr, si_scr, end_ref, first, last, reverse):
    @pl.when(first)
    def _():
        sr_scr[...] = jnp.zeros_like(sr_scr)
        si_scr[...] = jnp.zeros_like(si_scr)

    def step(k, carry):
        s1, s2 = _halves(x_ref[nslab - 1 - k if reverse else k].astype(F32))
        if reverse:
            s1, s2 = s2, s1
        return _cstep(ar, ai, *_cstep(ar, ai, carry[0], carry[1], *s1), *s2)

    sr, si = lax.fori_loop(0, nslab, step, (sr_scr[...], si_scr[...]), unroll=4)
    sr_scr[...] = sr
    si_scr[...] = si

    @pl.when(last)
    def _():
        end_ref[:, 0:512] = sr
        end_ref[:, 512:1024] = si


def s5_in(u, bsg, a_re, a_im, name):
    t = u.shape[0]
    tm = min(512, t)
    nt = t // tm

    def body(u_ref, b_ref, ar_ref, ai_ref, o_ref, e_ref, sr_scr, si_scr):
        i = pl.program_id(1)
        o_ref[...] = _dot(u_ref[...].astype(BF16), b_ref[...]).astype(BF16).reshape(tm // 16, 16, 1024)
        _local_ends(o_ref, tm // 16, ar_ref[...], ai_ref[...], sr_scr, si_scr, e_ref, i == 0, i == nt - 1, False)

    return pl.pallas_call(
        body, name=name, grid=(8, nt),
        in_specs=[pl.BlockSpec((tm, 128), lambda s, i: (i, s)), pl.BlockSpec((None, 128, 1024), lambda s, i: (s, 0, 0)),
                  pl.BlockSpec((None, 8, 512), lambda s, i: (s, 0, 0)), pl.BlockSpec((None, 8, 512), lambda s, i: (s, 0, 0))],
        out_specs=[pl.BlockSpec((tm // 16, 16, 1024), lambda s, i: (i, 0, s)),
                   pl.BlockSpec((None, 8, 1024), lambda s, i: (s, 0, 0))],
        out_shape=[jax.ShapeDtypeStruct((t // 16, 16, S5NS), BF16), jax.ShapeDtypeStruct((8, 8, 1024), F32)],
        scratch_shapes=[pltpu.VMEM((8, 512), F32), pltpu.VMEM((8, 512), F32)],
        compiler_params=_cp("parallel", "arbitrary"))(u, bsg, a_re, a_im)


def s5_out_bwd(dy, csg, s, a_re, a_im, name):
    t = dy.shape[0]
    tm = min(512, t)
    nt = t // tm

    def body(dy_ref, c_ref, s_ref, ar_ref, ai_ref, e_ref, dc_ref, end_ref, sr_scr, si_scr):
        i = pl.program_id(1)

        @pl.when(i == 0)
        def _():
            dc_ref[...] = jnp.zeros_like(dc_ref)

        dyb = dy_ref[...].astype(BF16)
        e_ref[...] = _nt(dyb, c_ref[...]).astype(BF16).reshape(tm // 16, 16, 1024)
        dc_ref[...] += lax.dot_general(s_ref[...], dyb, (((0,), (0,)), ((), ())), preferred_element_type=F32)
        _local_ends(e_ref, tm // 16, ar_ref[...], -ai_ref[...], sr_scr, si_scr, end_ref, i == 0, i == nt - 1, True)

    rv = lambda i: nt - 1 - i
    return pl.pallas_call(
        body, name=name, grid=(8, nt),
        in_specs=[pl.BlockSpec((tm, 128), lambda s, i: (rv(i), s)), pl.BlockSpec((None, 1024, 128), lambda s, i: (s, 0, 0)),
                  pl.BlockSpec((tm, 1024), lambda s, i: (rv(i), s)),
                  pl.BlockSpec((None, 8, 512), lambda s, i: (s, 0, 0)), pl.BlockSpec((None, 8, 512), lambda s, i: (s, 0, 0))],
        out_specs=[pl.BlockSpec((tm // 16, 16, 1024), lambda s, i: (rv(i), 0, s)),
                   pl.BlockSpec((None, 1024, 128), lambda s, i: (s, 0, 0)),
                   pl.BlockSpec((None, 8, 1024), lambda s, i: (s, 0, 0))],
        out_shape=[jax.ShapeDtypeStruct((t // 16, 16, S5NS), BF16), jax.ShapeDtypeStruct((8, 1024, 128), F32),
                   jax.ShapeDtypeStruct((8, 8, 1024), F32)],
        scratch_shapes=[pltpu.VMEM((8, 512), F32), pltpu.VMEM((8, 512), F32)],
        compiler_params=_cp("parallel", "arbitrary"))(dy, csg, s, a_re, a_im)


def s5_scan_init(ends, a_re, a_im, lseg, reverse, name):
    nsq = int(math.log2(lseg))
    assert 2 ** nsq == lseg
    sgn = -1.0 if reverse else 1.0
    order = list(range(7, -1, -1)) if reverse else list(range(8))

    def body(e_ref, ar_ref, ai_ref, o_ref):
        pr = ar_ref[0:1, :]
        pi = sgn * ai_ref[0:1, :]
        for _ in range(nsq):
            pr, pi = pr * pr - pi * pi, 2.0 * pr * pi
        prev_r = jnp.zeros((1, 512), F32)
        prev_i = jnp.zeros((1, 512), F32)
        j0 = order[0]
        o_ref[j0:j0 + 1, 0:512] = prev_r
        o_ref[j0:j0 + 1, 512:1024] = prev_i
        for idx in range(1, 8):
            j, jp = order[idx], order[idx - 1]
            prev_r, prev_i = _cstep(pr, pi, prev_r, prev_i, e_ref[jp:jp + 1, 0:512], e_ref[jp:jp + 1, 512:1024])
            o_ref[j:j + 1, 0:512] = prev_r
            o_ref[j:j + 1, 512:1024] = prev_i

    return pl.pallas_call(
        body, name=name, grid=(8,),
        in_specs=[pl.BlockSpec((None, 8, 1024), lambda s: (s, 0, 0)), pl.BlockSpec((None, 8, 512), lambda s: (s, 0, 0)),
                  pl.BlockSpec((None, 8, 512), lambda s: (s, 0, 0))],
        out_specs=pl.BlockSpec((None, 8, 1024), lambda s: (s, 0, 0)),
        out_shape=jax.ShapeDtypeStruct((8, 8, 1024), F32), compiler_params=_cp("parallel"))(ends, a_re, a_im)


def s5_scan_fwd(b3, init, a_re, a_im, csg, u, d_row, name):
    nslab = b3.shape[0]
    ti = min(64, nslab)
    nb = nslab // ti
    rows = 16 * ti

    def body(b_ref, i_ref, ar_ref, ai_ref, c_ref, u_ref, d_ref, o_ref, y_ref, sr_scr, si_scr):
        @pl.when(pl.program_id(1) == 0)
        def _():
            sr_scr[...] = i_ref[:, 0:512]
            si_scr[...] = i_ref[:, 512:1024]

        ar = ar_ref[...]
        ai = ai_ref[...]

        def step(k, carry):
            b1, b2 = _halves(b_ref[k].astype(F32))
            r1, i1 = _cstep(ar, ai, carry[0], carry[1], *b1)
            r2, i2 = _cstep(ar, ai, r1, i1, *b2)
            o_ref[k] = _slab(r1, i1, r2, i2)
            return r2, i2

        sr, si = lax.fori_loop(0, ti, step, (sr_scr[...], si_scr[...]), unroll=4)
        sr_scr[...] = sr
        si_scr[...] = si
        y_ref[...] = _dot(o_ref[...].reshape(rows, 1024), c_ref[...]) + d_ref[...] * u_ref[...].astype(F32)

    return pl.pallas_call(
        body, name=name, grid=(8, nb),
        in_specs=[pl.BlockSpec((ti, 16, 1024), lambda s, tb: (tb, 0, s)), pl.BlockSpec((None, 8, 1024), lambda s, tb: (s, 0, 0)),
                  pl.BlockSpec((None, 8, 512), lambda s, tb: (s, 0, 0)), pl.BlockSpec((None, 8, 512), lambda s, tb: (s, 0, 0)),
                  pl.BlockSpec((None, 1024, 128), lambda s, tb: (s, 0, 0)), pl.BlockSpec((rows, 128), lambda s, tb: (tb, s)),
                  pl.BlockSpec((1, 128), lambda s, tb: (0, s))],
        out_specs=[pl.BlockSpec((ti, 16, 1024), lambda s, tb: (tb, 0, s)), pl.BlockSpec((rows, 128), lambda s, tb: (tb, s))],
        out_shape=[jax.ShapeDtypeStruct(b3.shape, BF16), jax.ShapeDtypeStruct((16 * nslab, S5W), F32)],
        scratch_shapes=[pltpu.VMEM((8, 512), F32), pltpu.VMEM((8, 512), F32)],
        compiler_params=_cp("parallel", "arbitrary"))(b3, init, a_re, a_im, csg, u, d_row)


def s5_scan_bwd(e3, linit, s3, sinit, a_re, a_im, bsg, u, dy, d_row, name):
    nslab = e3.shape[0]
    ti = min(64, nslab)
    nb = nslab // ti
    rows = 16 * ti

    def body(e_ref, li_ref, s_ref, sh_ref, si0_ref, ar_ref, ai_ref, b_ref, u_ref, dy_ref, d_ref,
             du_ref, db_ref, dd_ref, da_ref, o_ref, lr_scr, lim_scr):
        tb = pl.program_id(1)

        @pl.when(tb == 0)
        def _():
            lr_scr[...] = li_ref[:, 0:512]
            lim_scr[...] = li_ref[:, 512:1024]
            da_ref[...] = jnp.zeros_like(da_ref)
            db_ref[...] = jnp.zeros_like(db_ref)
            dd_ref[...] = jnp.zeros_like(dd_ref)

        ar = ar_ref[...]
        ai = -ai_ref[...]

        def slab(kk, lr, li, dar, dai, sp):
            e1, e2 = _halves(e_ref[kk].astype(F32))
            s1, _ = _halves(s_ref[kk].astype(F32))
            r2, i2 = _cstep(ar, ai, lr, li, *e2)
            dar = dar + r2 * s1[0] + i2 * s1[1]
            dai = dai + i2 * s1[0] - r2 * s1[1]
            r1, i1 = _cstep(ar, ai, r2, i2, *e1)
            dar = dar + r1 * sp[0] + i1 * sp[1]
            dai = dai + i1 * sp[0] - r1 * sp[1]
            o_ref[kk] = _slab(r1, i1, r2, i2)
            return r1, i1, dar, dai

        def step(k, carry):
            kk = ti - 1 - k
            return slab(kk, *carry, _halves(s_ref[kk - 1].astype(F32))[1])

        z = jnp.zeros((8, 512), F32)
        lr, li, dar, dai = lax.fori_loop(0, ti - 1, step, (lr_scr[...], lim_scr[...], z, z), unroll=2)
        first = tb == nb - 1
        halo = _halves(sh_ref[0].astype(F32))[1]
        sp = (jnp.where(first, si0_ref[:, 0:512], halo[0]), jnp.where(first, si0_ref[:, 512:1024], halo[1]))
        lr, li, dar, dai = slab(0, lr, li, dar, dai, sp)
        lr_scr[...] = lr
        lim_scr[...] = li
        da_ref[:, 0:512] += dar
        da_ref[:, 512:1024] += dai
        lb = o_ref[...].reshape(rows, 1024)
        uv = u_ref[...].astype(F32)
        dyv = dy_ref[...]
        du_ref[...] = _nt(lb, b_ref[...]) + d_ref[...] * dyv
        db_ref[...] += lax.dot_general(uv.astype(BF16), lb, (((0,), (0,)), ((), ())), preferred_element_type=F32)
        dd_ref[...] += jnp.sum(dyv * uv, axis=0, keepdims=True)

    rb = lambda tb: nb - 1 - tb
    return pl.pallas_call(
        body, name=name, grid=(8, nb),
        in_specs=[pl.BlockSpec((ti, 16, 1024), lambda s, tb: (rb(tb), 0, s)),
                  pl.BlockSpec((None, 8, 1024), lambda s, tb: (s, 0, 0)),
                  pl.BlockSpec((ti, 16, 1024), lambda s, tb: (rb(tb), 0, s)),
                  pl.BlockSpec((1, 16, 1024), lambda s, tb: (jnp.maximum(rb(tb) * ti - 1, 0), 0, s)),
                  pl.BlockSpec((None, 8, 1024), lambda s, tb: (s, 0, 0)),
                  pl.BlockSpec((None, 8, 512), lambda s, tb: (s, 0, 0)), pl.BlockSpec((None, 8, 512), lambda s, tb: (s, 0, 0)),
                  pl.BlockSpec((None, 128, 1024), lambda s, tb: (s, 0, 0)),
                  pl.BlockSpec((rows, 128), lambda s, tb: (rb(tb), s)), pl.BlockSpec((rows, 128), lambda s, tb: (rb(tb), s)),
                  pl.BlockSpec((1, 128), lambda s, tb: (0, s))],
        out_specs=[pl.BlockSpec((rows, 128), lambda s, tb: (rb(tb), s)),
                   pl.BlockSpec((None, 128, 1024), lambda s, tb: (s, 0, 0)), pl.BlockSpec((1, 128), lambda s, tb: (0, s)),
                   pl.BlockSpec((None, 8, 1024), lambda s, tb: (s, 0, 0))],
        out_shape=[jax.ShapeDtypeStruct((16 * nslab, S5W), F32), jax.ShapeDtypeStruct((8, 128, 1024), F32),
                   jax.ShapeDtypeStruct((1, S5W), F32), jax.ShapeDtypeStruct((8, 8, 1024), F32)],
        scratch_shapes=[pltpu.VMEM((ti, 16, 1024), BF16), pltpu.VMEM((8, 512), F32), pltpu.VMEM((8, 512), F32)],
        compiler_params=_cp("parallel", "arbitrary"))(e3, linit, s3, s3, sinit, a_re, a_im, bsg, u, dy, d_row)


_GC = math.sqrt(2.0 / math.pi)


def gelu_fwd(y, name):
    t, w = y.shape
    tm = min(512, t)

    def body(y_ref, o_ref):
        v = y_ref[...]
        o_ref[...] = (0.5 * v * (1.0 + jnp.tanh(_GC * (v + 0.044715 * v * v * v)))).astype(BF16)

    return pl.pallas_call(body, name=name, grid=(t // tm,), in_specs=[_row(tm, w)], out_specs=_row(tm, w),
                          out_shape=jax.ShapeDtypeStruct((t, w), BF16), compiler_params=_cp("parallel"))(y)


def gelu_bwd(y, dg, name):
    t, w = y.shape
    tm = min(512, t)

    def body(y_ref, d_ref, o_ref):
        v = y_ref[...]
        th = jnp.tanh(_GC * (v + 0.044715 * v * v * v))
        o_ref[...] = d_ref[...].astype(F32) * (0.5 * (1.0 + th) + 0.5 * v * (1.0 - th * th) * _GC * (1.0 + 3.0 * 0.044715 * v * v))

    return pl.pallas_call(body, name=name, grid=(t // tm,), in_specs=[_row(tm, w), _row(tm, w)], out_specs=_row(tm, w),
                          out_shape=jax.ShapeDtypeStruct((t, w), F32), compiler_params=_cp("parallel"))(y, dg)


def merge_fwd(proj, pa, glu, name):
    t = pa.shape[0]
    tm = min(256, t)

    def body(g_ref, pa_ref, glu_ref, o_ref):
        pb = glu_ref[:, 0:D].astype(F32) * _sigmoid(glu_ref[:, D:2 * D].astype(F32))
        o_ref[...] = (_sigmoid(g_ref[:, 0:D].astype(F32)) * pa_ref[...].astype(F32)
                      + _sigmoid(g_ref[:, D:2 * D].astype(F32)) * pb).astype(BF16)

    return pl.pallas_call(body, name=name, grid=(t // tm,), in_specs=[_row(tm, 2 * D), _row(tm, D), _row(tm, 2 * D)],
                          out_specs=_row(tm, D), out_shape=jax.ShapeDtypeStruct((t, D), BF16),
                          compiler_params=_cp("parallel"))(proj, pa, glu)


def merge_bwd(proj, pa, glu, dm, name):
    t = pa.shape[0]
    tm = min(256, t)

    def body(g_ref, pa_ref, glu_ref, dm_ref, dpa_ref, dglu_ref, dg_ref):
        dmv = dm_ref[...].astype(F32)
        pav = pa_ref[...].astype(F32)
        sa = _sigmoid(g_ref[:, 0:D].astype(F32))
        sb = _sigmoid(g_ref[:, D:2 * D].astype(F32))
        ga = glu_ref[:, 0:D].astype(F32)
        sg = _sigmoid(glu_ref[:, D:2 * D].astype(F32))
        pb = ga * sg
        dpb = sb * dmv
        dpa_ref[...] = (sa * dmv).astype(BF16)
        dglu_ref[:, 0:D] = (dpb * sg).astype(BF16)
        dglu_ref[:, D:2 * D] = (dpb * pb * (1.0 - sg)).astype(BF16)
        dg_ref[:, 0:D] = (dmv * pav * sa * (1.0 - sa)).astype(BF16)
        dg_ref[:, D:2 * D] = (dmv * pb * sb * (1.0 - sb)).astype(BF16)

    return pl.pallas_call(
        body, name=name, grid=(t // tm,),
        in_specs=[_row(tm, 2 * D), _row(tm, D), _row(tm, 2 * D), _row(tm, D)],
        out_specs=[_row(tm, D), _row(tm, 2 * D), _row(tm, 2 * D)],
        out_shape=[jax.ShapeDtypeStruct((t, D), BF16), jax.ShapeDtypeStruct((t, 2 * D), BF16),
                   jax.ShapeDtypeStruct((t, 2 * D), BF16)],
        compiler_params=_cp("parallel"))(proj, pa, glu, dm)


def adamw(w, parts, m, v, name):
    r, c = w.shape
    p = parts.shape[0]
    tr = r if r <= 128 else 128
    c1 = 1.0 - ADAM_B1 ** ADAM_STEP
    c2 = 1.0 - ADAM_B2 ** ADAM_STEP

    def body(w_ref, p_ref, m_ref, v_ref, g_ref, d_ref, nm_ref, nv_ref):
        g = p_ref[0].astype(F32)
        for k in range(1, p):
            g = g + p_ref[k].astype(F32)
        mn = ADAM_B1 * m_ref[...] + (1.0 - ADAM_B1) * g
        vn = ADAM_B2 * v_ref[...] + (1.0 - ADAM_B2) * (g * g)
        g_ref[...] = g
        nm_ref[...] = mn
        nv_ref[...] = vn
        d_ref[...] = -ADAM_LR * ((mn / c1) / (jnp.sqrt(vn / c2) + ADAM_EPS) + ADAM_WD * w_ref[...])

    spec = pl.BlockSpec((tr, c), lambda i: (i, 0))
    o = jax.ShapeDtypeStruct((r, c), F32)
    return pl.pallas_call(
        body, name=name, grid=(pl.cdiv(r, tr),),
        in_specs=[spec, pl.BlockSpec((p, tr, c), lambda i: (0, i, 0)), spec, spec],
        out_specs=[spec, spec, spec, spec], out_shape=[o, o, o, o], compiler_params=_cp("parallel"))(w, parts, m, v)


def _s5_discretise(lambda_re, lambda_im, log_dt, b_re, b_im):
    dt = jnp.exp(log_dt)[:, None]
    lr = jnp.minimum(lambda_re, -1e-4)
    li = lambda_im
    mag = jnp.exp(lr * dt)
    ar = mag * jnp.cos(li * dt)
    ai = mag * jnp.sin(li * dt)
    den = lr * lr + li * li
    nr = ar - 1.0
    kr = (nr * lr + ai * li) / den
    ki = (ai * lr - nr * li) / den
    bbar_re = kr[..., None] * b_re - ki[..., None] * b_im
    bbar_im = kr[..., None] * b_im + ki[..., None] * b_re
    return ar, ai, bbar_re, bbar_im


def _block_diag(v):
    a, b = v.shape[2], v.shape[3]
    eye = jnp.eye(8, dtype=v.dtype)[None, :, None, :, None]
    return (v[:, :, :, None, :] * eye).reshape(8, 8 * a, 8 * b)


def _diag_blocks(m, a, b):
    eye = jnp.eye(8, dtype=m.dtype)[None, :, None, :, None]
    return jnp.sum(m.reshape(8, 8, a, 8, b) * eye, axis=3)


def _bsg_of(bb_re, bb_im):
    f = lambda b: _block_diag(b.reshape(8, 8, 64, 16).transpose(0, 1, 3, 2))
    return jnp.concatenate([f(bb_re), f(bb_im)], axis=2)


def _bsg_diag(dbsg):
    f = lambda x: _diag_blocks(x, 16, 64).transpose(0, 1, 3, 2).reshape(64, 64, 16)
    return f(dbsg[:, :, 0:512]), f(dbsg[:, :, 512:1024])


def _csg_of(c_re, c_im):
    f = lambda c: _block_diag(c.reshape(8, 8, 16, 64).transpose(0, 1, 3, 2))
    return jnp.concatenate([f(c_re), -f(c_im)], axis=1)


def _csg_diag(dcsg):
    f = lambda x: _diag_blocks(x, 64, 16).transpose(0, 1, 3, 2).reshape(64, 16, 64)
    return f(dcsg[:, 0:512, :]), -f(dcsg[:, 512:1024, :])


def _perm(a, t):
    return a.reshape(8, t // 8, a.shape[1]).transpose(1, 0, 2).reshape(t, a.shape[1])


def _unperm(a, t):
    return a.reshape(t // 8, 8, a.shape[1]).transpose(1, 0, 2).reshape(t, a.shape[1])


def _cols(g):
    return g.transpose(1, 0, 2).reshape(g.shape[1], N_DEV * g.shape[2])


def _rows(g):
    return g.reshape(N_DEV * g.shape[1], g.shape[2])


def _col_parts(g):
    r, c = g.shape
    return g.reshape(r, N_DEV, c // N_DEV).transpose(1, 0, 2)


def _row_parts(g):
    r, c = g.shape
    return g.reshape(N_DEV, r // N_DEV, c)


FB, FBP = D_FF // N_DEV, D_FFP // N_DEV


def _pad_ffn_in_shard(w):
    return jnp.pad(w.reshape(D, 2, FB), ((0, 0), (0, 0), (0, FBP - FB))).reshape(D, 2 * FBP)


def _unpad_ffn_in_shard(g):
    return g.reshape(D, 2, FBP)[:, :, :FB].reshape(D, 2 * FB)


def _pad_ffn_out_shard(w):
    return jnp.pad(w, ((0, FBP - FB), (0, 0)))


def sum_parts(parts, name):
    p, r, c = parts.shape
    tr = 256

    def body(p_ref, o_ref):
        g = p_ref[0].astype(F32)
        for k in range(1, p):
            g = g + p_ref[k].astype(F32)
        o_ref[...] = g

    return pl.pallas_call(body, name=name, grid=(r // tr,), in_specs=[pl.BlockSpec((p, tr, c), lambda i: (0, i, 0))],
                          out_specs=pl.BlockSpec((tr, c), lambda i: (i, 0)),
                          out_shape=jax.ShapeDtypeStruct((r, c), F32), compiler_params=_cp("parallel"))(parts)


def _pad_w_in(w):
    z = jnp.zeros((D, INP - P_DT - NH), w.dtype)
    return jnp.concatenate([w[:, O_GA:O_GB], w[:, O_GB:IN_COLS], w[:, 0:O_XBC], w[:, O_XBC:O_DT], w[:, O_U:O_GA],
                            w[:, O_DT:O_U], z], axis=1)


def _unpad_w_in(g):
    return jnp.concatenate([g[:, P_Z:P_XBC], g[:, P_XBC:P_U], g[:, P_DT:P_DT + NH], g[:, P_U:P_DT],
                            g[:, 0:D], g[:, D:2 * D]], axis=1)


_PACK = (("b_ada", 18432), ("norm_ffn1", 2048), ("norm_mix", 2048), ("conv_b", 3072), ("dt_bias", 32), ("a_log", 32),
         ("d_ssd", 32), ("ssd_norm_w", 2048), ("s5_lambda_re", 4096), ("s5_lambda_im", 4096), ("s5_b_re", 65536),
         ("s5_b_im", 65536), ("s5_c_re", 65536), ("s5_c_im", 65536), ("s5_d", 1024), ("s5_log_dt", 64),
         ("norm_ffn2", 2048), ("norm_final", 2048), ("loss", 1))
_PACK_ROWS = 304
_PACK_W = 1024


def _pack(d):
    flat = jnp.concatenate([d[k].reshape(-1).astype(F32) for k, _ in _PACK])
    return jnp.pad(flat, (0, _PACK_ROWS * _PACK_W - flat.shape[0])).reshape(_PACK_ROWS, _PACK_W)


def _unpack(a):
    flat = a.reshape(-1)
    out, off = {}, 0
    for k, n in _PACK:
        out[k] = flat[off:off + n]
        off += n
    return out


_TA = dict(tm=512, tn=512, tk=8192)


def _ffn_bwd(df, h, ab, act, w_in_t, w_out_t, tag):
    dab = ffn_dab(df, w_out_t, ab, tag + "_dab")
    dw_out = mm(act, df, ta=True, out_dtype=BF16, i_outer=True, name=tag + "_dwout", **_TA)
    dw_in, (x_out,) = mm(h, dab, ta=True, b_halves=True, out_dtype=BF16, i_outer=True, name=tag + "_dwin",
                         comm=[("xc", _row_parts(dw_out))], **_TA)
    dh, (x_in,) = mm(dab, w_in_t, a_halves=True, out_dtype=BF16, tk=5632, name=tag + "_dh", comm=[("xcc", dw_in)])
    g_in = _unpad_ffn_in_shard(sum_parts(x_in, tag + "_dwin_sum"))
    return dh, g_in[None], x_out


def kernel(x, c, w_ada, b_ada, norm_ffn1, w_ffn1_in, w_ffn1_out, norm_mix, w_in, conv_w, conv_b, dt_bias, a_log, d_ssd, ssd_norm_w, w_a_proj, s5_lambda_re, s5_lambda_im, s5_b_re, s5_b_im, s5_c_re, s5_c_im, s5_d, s5_log_dt, w_b_glu, w_out, norm_ffn2, w_ffn2_in, w_ffn2_out, norm_final, loss_target, m_w_ada, m_b_ada, m_norm_ffn1, m_w_ffn1_in, m_w_ffn1_out, m_norm_mix, m_w_in, m_conv_w, m_conv_b, m_dt_bias, m_a_log, m_d_ssd, m_ssd_norm_w, m_w_a_proj, m_s5_lambda_re, m_s5_lambda_im, m_s5_b_re, m_s5_b_im, m_s5_c_re, m_s5_c_im, m_s5_d, m_s5_log_dt, m_w_b_glu, m_w_out, m_norm_ffn2, m_w_ffn2_in, m_w_ffn2_out, m_norm_final, v_w_ada, v_b_ada, v_norm_ffn1, v_w_ffn1_in, v_w_ffn1_out, v_norm_mix, v_w_in, v_conv_w, v_conv_b, v_dt_bias, v_a_log, v_d_ssd, v_ssd_norm_w, v_w_a_proj, v_s5_lambda_re, v_s5_lambda_im, v_s5_b_re, v_s5_b_im, v_s5_c_re, v_s5_c_im, v_s5_d, v_s5_log_dt, v_w_b_glu, v_w_out, v_norm_ffn2, v_w_ffn2_in, v_w_ffn2_out, v_norm_final):
    args = dict(locals())
    t = x.shape[1]
    me = _my_id()
    xt = x[0]
    tgt = loss_target[0]
    small = {k: args[k] for k, _ in _PACK if k != "loss"}

    bf = lambda w: w[0].astype(BF16)
    ffn_in_shard = lambda w: _pad_ffn_in_shard(bf(w))
    ffn_out_shard = lambda w: _pad_ffn_out_shard(bf(w))

    c8 = all_gather(c, "ag_c").reshape(N_DEV, D)
    b_loc = lax.dynamic_slice(b_ada, (0, me * (N_ADA * D // N_DEV)), (1, N_ADA * D // N_DEV))
    m8 = ada_fwd(c8, w_ada[0], b_loc, "ada_fwd")
    mods, wf1i, g_cw = comm_call([("xc", m8.reshape(N_DEV, 1, -1)), ("agc", ffn_in_shard(w_ffn1_in)),
                                  ("ag", conv_w[0])], "xc_mods_ag_ffn1_in")
    mods = mods.reshape(1, N_ADA * D)
    convw = _cols(g_cw)

    h1 = mod_fwd(xt, norm_ffn1, mods, 0, 1, name="mod1")
    ab1, act1, (g_f1o, g_win, g_wap) = ffn_in_act(
        h1, wf1i, "ffn1_in", fwd=0.9,
        comm=[("ag", ffn_out_shard(w_ffn1_out)), ("ag", bf(w_in)), ("ag", bf(w_a_proj))])
    wf1o, winp, wap = _rows(g_f1o), _pad_w_in(_cols(g_win)), _rows(g_wap)
    f1, (g_wo, g_wbg) = mm(act1, wf1o, out_dtype=BF16, tk=5632, name="ffn1_out",
                           comm=[("ag", bf(w_out)), ("ag", bf(w_b_glu))])
    wo, wbg = _rows(g_wo), _cols(g_wbg)
    x1, h2 = mod_fwd(xt, norm_mix, mods, 3, 4, f=f1, gk=2, gscale=0.5, name="mod2")
    proj, (wf2i,) = mm(h2, winp, out_dtype=BF16, tm=1024, tn=512, i_outer=True, name="w_in",
                       comm=[("agc", ffn_in_shard(w_ffn2_in))])
    dtraw = mm(h2, winp[:, P_DT:P_DT + 128], tn=128, name="w_in_dt")
    cb_row = conv_b
    xc = conv_fwd(proj, convw, cb_row, "conv_fwd")
    row128 = lambda v: jnp.pad(v.reshape(1, -1), ((0, 0), (0, 128 - v.size)))
    dtb_row, alog_row = row128(dt_bias), row128(a_log)
    dx_row = jnp.repeat(d_ssd.reshape(-1), HP).reshape(1, 2048)
    rows = jnp.arange(128)[:, None]
    expm = ((rows % NH == jnp.arange(2048)[None, :] // HP) & (rows < 3 * NH)).astype(BF16)
    tri = (jnp.arange(LCH)[:, None] >= jnp.arange(LCH)[None, :]).astype(F32)
    y_ssd, hsave = ssd_fwd(xc, dtraw, dtb_row, alog_row, dx_row, expm, tri, "ssd_fwd")
    ya = ssd_out_fwd(y_ssd, proj, ssd_norm_w, "ssd_out")
    pa = mm(ya, wap, out_dtype=BF16, name="w_a_proj")

    s5p = (s5_lambda_re[0], s5_lambda_im[0], s5_log_dt[0], s5_b_re[0], s5_b_im[0])
    (ar, ai, bb_re, bb_im), s5_vjp = jax.vjp(_s5_discretise, *s5p)
    a_re8 = jnp.broadcast_to(ar.reshape(8, 1, 512), (8, 8, 512))
    a_im8 = jnp.broadcast_to(ai.reshape(8, 1, 512), (8, 8, 512))
    bsg = _bsg_of(bb_re, bb_im).astype(BF16)
    csg = _csg_of(s5_c_re[0], s5_c_im[0]).astype(BF16)
    d_row = s5_d.reshape(1, S5W)
    lseg = t // 8
    u_p = _perm(proj[:, P_U:P_U + S5W], t)
    bu3, ends_f = s5_in(u_p, bsg, a_re8, a_im8, "s5_in")
    sinit = s5_scan_init(ends_f, a_re8, a_im8, lseg, False, "s5_init_f")
    s3, yb_p = s5_scan_fwd(bu3, sinit, a_re8, a_im8, csg, u_p, d_row, "s5_scan_f")
    s2 = s3.reshape(t, S5NS)
    yb = _unperm(yb_p, t)
    gy = gelu_fwd(yb, "gelu")
    glu = mm(gy, wbg, out_dtype=BF16, name="w_b_glu")
    merged = merge_fwd(proj, pa, glu, "merge")
    o = mm(merged, wo, out_dtype=BF16, name="w_out")
    x2, h3 = mod_fwd(x1, norm_ffn2, mods, 6, 7, f=o, gk=5, gscale=1.0, name="mod3")
    ab3, act3, (g_f2o, g_f1it) = ffn_in_act(
        h3, wf2i, "ffn2_in", comm=[("ag", ffn_out_shard(w_ffn2_out)), ("ag", ffn_in_shard(w_ffn1_in).T)])
    wf2o = _rows(g_f2o)
    f3, (g_f2it,) = mm(act3, wf2o, out_dtype=BF16, tk=5632, name="ffn2_out", comm=[("ag", ffn_in_shard(w_ffn2_in).T)])
    wf1i_t, wf2i_t = _rows(g_f1it), _rows(g_f2it)

    dx3, df3, st_fin = final_fwd_bwd(x2, f3, mods, norm_final.reshape(1, D), tgt, "final")
    dh3, x_f2i, x_f2o = _ffn_bwd(df3, h3, ab3, act3, wf2i_t, wf2o.T, "ffn2")
    dx2, do, st3 = mod_bwd(x2, dh3, dx3, norm_ffn2, mods, 7, fprev=o, gk=5, gscale=1.0, name="mod3_bwd")

    dmerged = mm(do, wo.T, out_dtype=BF16, name="w_out_dx")
    dwo = mm(merged, do, ta=True, out_dtype=BF16, i_outer=True, name="w_out_dw", **_TA)
    dpa, dglu, dgates = merge_bwd(proj, pa, glu, dmerged, "merge_bwd")
    dwbg = mm(gy, dglu, ta=True, out_dtype=BF16, i_outer=True, name="w_b_glu_dw", **_TA)
    dgy, (x_wo,) = mm(dglu, wbg.T, out_dtype=BF16, name="w_b_glu_dx", comm=[("xc", _row_parts(dwo))])
    dyb_p = _perm(gelu_bwd(yb, dgy, "gelu_bwd"), t)
    e3, dcsg, ends_b = s5_out_bwd(dyb_p, csg, s2, a_re8, a_im8, "s5_out_bwd")
    linit = s5_scan_init(ends_b, a_re8, a_im8, lseg, True, "s5_init_b")
    du_p, dbsg, dd_row, da8 = s5_scan_bwd(e3, linit, s3, sinit, a_re8, a_im8, bsg, u_p, dyb_p, d_row, "s5_scan_b")
    du = _unperm(du_p, t).astype(BF16)
    da = jnp.sum(da8, axis=1)
    dbb_re, dbb_im = _bsg_diag(dbsg)
    g_lre, g_lim, g_ldt, g_bre, g_bim = s5_vjp((da[:, 0:512].reshape(64, 64), da[:, 512:1024].reshape(64, 64),
                                                dbb_re, dbb_im))
    g_cre, g_cim = _csg_diag(dcsg)

    dwap = mm(ya, dpa, ta=True, out_dtype=BF16, i_outer=True, name="w_a_proj_dw", **_TA)
    dya, (x_wbg,) = mm(dpa, wap.T, out_dtype=BF16, name="w_a_proj_dx", comm=[("xc", _col_parts(dwbg))])
    dy_ssd, dz, st_sn = ssd_out_bwd(y_ssd, proj, dya, ssd_norm_w, "ssd_out_bwd")
    dxc, ddt, st_ssd = ssd_bwd(xc, dtraw, hsave, dy_ssd, dtb_row, alog_row, dx_row, expm, tri, "ssd_bwd")
    dpre, st_cv = conv_bwd_pre(proj, dxc, convw, cb_row, "conv_bwd_pre")
    dxbc = conv_bwd_in(dpre, convw, "conv_bwd_in")
    dproj = jnp.concatenate([dgates, dz, dxbc, du, ddt.astype(BF16), jnp.zeros((t, INP - P_DT - 128), BF16)], axis=1)
    dwinp, (x_wap, x_cw) = mm(h2, dproj, ta=True, out_dtype=BF16, i_outer=True, name="w_in_dw",
                              comm=[("xc", _row_parts(dwap)), ("xc", _col_parts(st_cv[0:CONV_K]))], **_TA)
    dh2, (x_win,) = mm(dproj, winp.T, out_dtype=BF16, tk=5376, name="w_in_dx",
                       comm=[("xc", _col_parts(_unpad_w_in(dwinp)))])
    dx1, df1, st2 = mod_bwd(x1, dh2, dx2, norm_mix, mods, 4, fprev=f1, gk=2, gscale=0.5, name="mod2_bwd")
    dh1, x_f1i, x_f1o = _ffn_bwd(df1, h1, ab1, act1, wf1i_t, wf1o.T, "ffn1")
    gx, st1 = mod_bwd(xt, dh1, dx1, norm_ffn1, mods, 1, name="mod1_bwd")

    dmods = jnp.concatenate([st1[0], st1[1], st2[3], st2[0], st2[1], st3[3], st3[0], st3[1], st_fin[1]])
    part = {"b_ada": dmods, "norm_ffn1": st1[2], "norm_mix": st2[2], "conv_b": st_cv[4], "dt_bias": st_ssd[1, 0:NH],
            "a_log": st_ssd[0, 0:NH], "d_ssd": st_ssd[2, 0:NH], "ssd_norm_w": st_sn[0], "s5_lambda_re": g_lre,
            "s5_lambda_im": g_lim, "s5_b_re": g_bre, "s5_b_im": g_bim, "s5_c_re": g_cre, "s5_c_im": g_cim,
            "s5_d": dd_row, "s5_log_dt": g_ldt, "norm_ffn2": st3[2], "norm_final": st_fin[0],
            "loss": (0.5 / D) * jnp.sum(st_fin[2])}
    zero = {"loss": jnp.zeros((1,), F32)}
    gath = all_gather(_pack(part), "ag_small")
    sg, sd, sm, sv = adamw(_pack({**small, **zero}), gath, _pack({**{k: args["m_" + k] for k in small}, **zero}),
                           _pack({**{k: args["v_" + k] for k in small}, **zero}), "adamw_small")
    sg, sd, sm, sv = _unpack(sg), _unpack(sd), _unpack(sm), _unpack(sv)
    loss = sg["loss"][0]

    dm_loc = lax.dynamic_slice(gath.reshape(N_DEV, -1)[:, 0:N_ADA * D], (0, me * (N_ADA * D // N_DEV)),
                               (N_DEV, N_ADA * D // N_DEV))
    g_ada = ada_bwd(c8.T, dm_loc, "ada_bwd")
    big = {"w_ada": g_ada[None], "w_ffn1_in": x_f1i, "w_ffn1_out": x_f1o, "w_in": x_win, "conv_w": x_cw,
           "w_a_proj": x_wap, "w_b_glu": x_wbg, "w_out": x_wo, "w_ffn2_in": x_f2i, "w_ffn2_out": x_f2o}
    res = {}
    for k, parts in big.items():
        res[k] = adamw(args[k][0], parts, args["m_" + k][0], args["v_" + k][0], "adamw_" + k)

    names = ["w_ada", "b_ada", "norm_ffn1", "w_ffn1_in", "w_ffn1_out", "norm_mix", "w_in", "conv_w", "conv_b", "dt_bias",
             "a_log", "d_ssd", "ssd_norm_w", "w_a_proj", "s5_lambda_re", "s5_lambda_im", "s5_b_re", "s5_b_im", "s5_c_re",
             "s5_c_im", "s5_d", "s5_log_dt", "w_b_glu", "w_out", "norm_ffn2", "w_ffn2_in", "w_ffn2_out", "norm_final"]
    outs = [loss, gx[None]]
    for q, src in enumerate((sg, sd, sm, sv)):
        for k in names:
            if k in res:
                outs.append(res[k][q][None])
            else:
                outs.append(src[k].reshape(args[k].shape))
    return tuple(outs)
```

```python
import functools
import math

import jax
import jax.numpy as jnp
from jax import lax
from jax.experimental import pallas as pl
from jax.experimental.pallas import tpu as pltpu

F32 = jnp.float32
BF16 = jnp.bfloat16
HI = lax.Precision.HIGHEST

N_DEV = 8
D = 2048
D_FF = 5504
D_FFP = 5632
NH = 32
HP = 64
NG = 4
NST = 128
LCH = 128
CONV_DIM = 3072
CONV_K = 4
S5W = 1024
S5NS = 8192
N_ADA = 9
EPS = 1e-6
IN_COLS = 10272
INP = 10752
P_GATES, P_Z, P_XBC, P_U, P_DT = 0, 4096, 6144, 9216, 10240
O_XBC, O_DT, O_U, O_GA, O_GB = 2048, 5120, 5152, 6176, 8224
NEG = -1e30
VMEM_LIMIT = 56 * 1024 * 1024

ADAM_LR, ADAM_B1, ADAM_B2, ADAM_EPS, ADAM_WD, ADAM_STEP = 0.001, 0.9, 0.999, 1e-08, 0.01, 10


def _cp(*sem):
    return pltpu.CompilerParams(dimension_semantics=sem, vmem_limit_bytes=VMEM_LIMIT)


def _tile(dim, pref):
    if dim <= pref or dim % pref == 0:
        return min(dim, pref)
    for t in (2048, 1024, 512, 256, 128):
        if t <= pref and dim % t == 0:
            return t
    return dim


def _vec(w, cb=0):
    return pl.BlockSpec((1, w), lambda *_: (0, cb))


def _row(tm, w, cb=0):
    return pl.BlockSpec((tm, w), lambda i: (i, cb))


def _stats(w):
    return pl.BlockSpec((8, w), lambda *_: (0, 0))


_HBM = pl.BlockSpec(memory_space=pl.ANY)


def _sigmoid(x):
    return 1.0 / (1.0 + jnp.exp(-x))


def _softplus(x):
    return jnp.maximum(x, 0.0) + jnp.log1p(jnp.exp(-jnp.abs(x)))


def _peer(k):
    x, y, c = lax.axis_index("x"), lax.axis_index("y"), lax.axis_index("c")
    return (x ^ ((k >> 2) & 1), y ^ ((k >> 1) & 1), c ^ (k & 1))


def _my_id():
    return 4 * lax.axis_index("x") + 2 * lax.axis_index("y") + lax.axis_index("c")


def _comm_out_shape(kind, v):
    shape = {"ag": (N_DEV,) + v.shape, "xc": v.shape, "agc": (v.shape[0], N_DEV * v.shape[1]),
             "xcc": (N_DEV, v.shape[0], v.shape[1] // N_DEV)}[kind]
    return jax.ShapeDtypeStruct(shape, v.dtype)


def _comm_scratch(n):
    return [pltpu.SemaphoreType.DMA((n * N_DEV,)), pltpu.SemaphoreType.DMA((n * N_DEV,))]


class _Comm:
    def __init__(self, kinds, srcs, dsts, send_sems, recv_sems):
        self.items = list(zip(kinds, srcs, dsts))
        self.send_sems, self.recv_sems = send_sems, recv_sems
        x, y, c = lax.axis_index("x"), lax.axis_index("y"), lax.axis_index("c")
        self.me = 4 * x + 2 * y + c
        self.sib = (x, y, 1 - c)
        self.chips = [(1 - x, y), (x, 1 - y), (1 - x, 1 - y)]
        self.c = c

    @staticmethod
    def _id(p):
        return 4 * p[0] + 2 * p[1] + p[2]

    def _src(self, q, d):
        kind, src, _ = self.items[q]
        if kind == "xc":
            return src.at[d]
        if kind == "xcc":
            w = src.shape[1] // N_DEV
            return src.at[:, pl.ds(pl.multiple_of(d * w, 128), w)]
        return src

    def _slot(self, q, d):
        kind, _, dst = self.items[q]
        if kind == "agc":
            w = dst.shape[1] // N_DEV
            return dst.at[:, pl.ds(pl.multiple_of(d * w, 128), w)]
        return dst.at[d]

    def _push(self, q, k, src, slot, to):
        return pltpu.make_async_remote_copy(
            src_ref=src, dst_ref=self._slot(q, slot), send_sem=self.send_sems.at[q * N_DEV + k],
            recv_sem=self.recv_sems.at[q * N_DEV + k], device_id=to, device_id_type=pl.DeviceIdType.MESH)

    def _local(self, q):
        return pltpu.make_async_copy(self._src(q, self.me), self._slot(q, self.me), self.send_sems.at[q * N_DEV])

    def _direct(self, q):
        kind = self.items[q][0]
        if kind in ("xc", "xcc"):
            out = []
            for k in range(1, N_DEV):
                p = _peer(k)
                out.append((k, self._push(q, k, self._src(q, self._id(p)), self.me, p)))
            return out
        src = self.items[q][1]
        out = [(1, self._push(q, 1, src, self.me, self.sib))]
        for j, chip in enumerate(self.chips):
            out.append((2 + j, self._push(q, 2 + j, src, self.me, (*chip, self.c))))
        return out

    def _forwards(self, q):
        out = []
        for j, chip in enumerate(self.chips):
            slot = self._id((*chip, self.c))
            out.append((2 + j, 5 + j, self._push(q, 5 + j, self._slot(q, slot), slot, self.sib)))
        return out

    def start(self):
        for q in range(len(self.items)):
            self._local(q).start()
            for _, cp in self._direct(q):
                cp.start()

    def forward(self):
        for q, (kind, _, _) in enumerate(self.items):
            if kind not in ("ag", "agc"):
                continue
            for k_in, _, fwd in self._forwards(q):
                self._push(q, k_in, self._slot(q, self.me), self.me, self.sib).wait_recv()
                fwd.start()

    def finish(self):
        for q, (kind, _, _) in enumerate(self.items):
            self._local(q).wait()
            if kind in ("xc", "xcc"):
                for _, cp in self._direct(q):
                    cp.wait()
                continue
            for k, cp in self._direct(q):
                cp.wait_send()
                if k == 1:
                    cp.wait_recv()
            for _, _, fwd in self._forwards(q):
                fwd.wait()


def comm_call(items, name):
    kinds = [k for k, _ in items]
    n = len(items)

    def body(*refs):
        cm = _Comm(kinds, refs[:n], refs[n:2 * n], refs[2 * n], refs[2 * n + 1])
        cm.start()
        cm.forward()
        cm.finish()

    return pl.pallas_call(
        body, name=name,
        in_specs=[pl.BlockSpec(memory_space=pl.ANY)] * n, out_specs=[pl.BlockSpec(memory_space=pl.ANY)] * n,
        out_shape=[_comm_out_shape(k, v) for k, v in items], scratch_shapes=_comm_scratch(n),
    )(*[v for _, v in items])


def all_gather(v, name):
    return comm_call([("ag", v)], name)[0]


def _pcall(body, args, *, name, grid, in_specs, out_specs, out_shape, scratch_shapes=(), sem, comm=(), fwd=0.85):
    nc, n_in, n_out = len(comm), len(in_specs), len(out_shape)
    if not nc:
        return pl.pallas_call(body, name=name, grid=grid, in_specs=list(in_specs), out_specs=list(out_specs),
                              out_shape=list(out_shape), scratch_shapes=list(scratch_shapes),
                              compiler_params=_cp(*sem))(*args)
    kinds = [k for k, _ in comm]
    steps = math.prod(grid)
    fwd_step = min(int(fwd * steps), steps - 1)

    def carried(*refs):
        ins, csrc = refs[:n_in], refs[n_in:n_in + nc]
        outs, cdst = refs[n_in + nc:n_in + nc + n_out], refs[n_in + nc + n_out:n_in + 2 * nc + n_out]
        scr = refs[n_in + 2 * nc + n_out:]
        cm = _Comm(kinds, csrc, cdst, scr[-2], scr[-1])
        step = 0
        for d, g in enumerate(grid):
            step = step * g + pl.program_id(d)

        @pl.when(step == 0)
        def _():
            cm.start()

        body(*ins, *outs, *scr[:-2])

        @pl.when(step == fwd_step)
        def _():
            cm.forward()

        @pl.when(step == steps - 1)
        def _():
            cm.finish()

    hbm = pl.BlockSpec(memory_space=pl.ANY)
    out = pl.pallas_call(
        carried, name=name, grid=grid, in_specs=list(in_specs) + [hbm] * nc, out_specs=list(out_specs) + [hbm] * nc,
        out_shape=list(out_shape) + [_comm_out_shape(k, v) for k, v in comm],
        scratch_shapes=list(scratch_shapes) + _comm_scratch(nc), compiler_params=_cp(*(("arbitrary",) * len(grid))),
    )(*args, *[v for _, v in comm])
    return list(out[:n_out]), list(out[n_out:])


def mm(a, b, *, ta=False, out_dtype=F32, tm=512, tn=1024, tk=2048, i_outer=False, a_halves=False, b_halves=False,
       name, comm=()):
    if a_halves:
        m, kd = a.shape[1], 2 * a.shape[2]
    elif ta:
        kd, m = a.shape
    else:
        m, kd = a.shape
    kd2, n = (b.shape[1], 2 * b.shape[2]) if b_halves else b.shape
    assert kd == kd2 and not (ta and a_halves), (a.shape, b.shape, ta)
    tm, tn, tk = _tile(m, tm), _tile(n // 2 if b_halves else n, tn), _tile(kd // 2 if a_halves else kd, tk)
    nk = kd // tk
    nkh, njh = nk // 2, n // tn // 2
    grid = (m // tm, n // tn, nk) if i_outer else (n // tn, m // tm, nk)
    dims = (((0,) if ta else (1,), (0,)), ((), ()))

    def ix(f):
        return (lambda i, j, k: f(i, j, k)) if i_outer else (lambda j, i, k: f(i, j, k))

    def body(a_ref, b_ref, o_ref, *scr):
        p = lax.dot_general(a_ref[...], b_ref[...], dims, preferred_element_type=F32)
        if nk == 1:
            o_ref[...] = p.astype(o_ref.dtype)
        else:
            acc = scr[0]
            k = pl.program_id(2)

            @pl.when(k == 0)
            def _():
                acc[...] = p

            @pl.when(k > 0)
            def _():
                acc[...] += p

            @pl.when(k == nk - 1)
            def _():
                o_ref[...] = acc[...].astype(o_ref.dtype)

    if a_halves:
        a_spec = pl.BlockSpec((None, tm, tk), ix(lambda i, j, k: (k // nkh, i, k % nkh)))
    elif ta:
        a_spec = pl.BlockSpec((tk, tm), ix(lambda i, j, k: (k, i)))
    else:
        a_spec = pl.BlockSpec((tm, tk), ix(lambda i, j, k: (i, k)))
    if b_halves:
        b_spec = pl.BlockSpec((None, tk, tn), ix(lambda i, j, k: (j // njh, k, j % njh)))
    else:
        b_spec = pl.BlockSpec((tk, tn), ix(lambda i, j, k: (k, j)))
    out = _pcall(body, (a, b), name=name, grid=grid, in_specs=[a_spec, b_spec],
                 out_specs=[pl.BlockSpec((tm, tn), ix(lambda i, j, k: (i, j)))],
                 out_shape=[jax.ShapeDtypeStruct((m, n), out_dtype)],
                 scratch_shapes=[pltpu.VMEM((tm, tn), F32)] if nk > 1 else [],
                 sem=("parallel", "parallel", "arbitrary"), comm=comm)
    return (out[0][0], out[1]) if comm else out[0]


def ffn_in_act(h, w, name, comm=(), fwd=0.85):
    t = h.shape[0]
    tm, tn = _tile(t, 512), 512
    nj = D_FFP // tn

    def body(h_ref, wa_ref, wb_ref, ab_ref, act_ref):
        hv = h_ref[...]
        pa = _dot(hv, wa_ref[...])
        pb = _dot(hv, wb_ref[...])
        ab_ref[0] = pa.astype(BF16)
        ab_ref[1] = pb.astype(BF16)
        act_ref[...] = (pa * _sigmoid(pa) * pb).astype(BF16)

    out = _pcall(body, (h, w, w), name=name, grid=(nj, t // tm),
                 in_specs=[pl.BlockSpec((tm, D), lambda j, i: (i, 0)), pl.BlockSpec((D, tn), lambda j, i: (0, j)),
                           pl.BlockSpec((D, tn), lambda j, i: (0, nj + j))],
                 out_specs=[pl.BlockSpec((2, tm, tn), lambda j, i: (0, i, j)), pl.BlockSpec((tm, tn), lambda j, i: (i, j))],
                 out_shape=[jax.ShapeDtypeStruct((2, t, D_FFP), BF16), jax.ShapeDtypeStruct((t, D_FFP), BF16)],
                 sem=("parallel", "parallel"), comm=comm, fwd=fwd)
    return (out[0][0], out[0][1], out[1]) if comm else (out[0], out[1])


def ffn_dab(df, w_out_t, ab, name, comm=()):
    t = df.shape[0]
    tm, tn = _tile(t, 1024), 512

    def body(d_ref, w_ref, ab_ref, o_ref):
        dv = _dot(d_ref[...], w_ref[...])
        a = ab_ref[0].astype(F32)
        b = ab_ref[1].astype(F32)
        s = _sigmoid(a)
        o_ref[0] = (dv * b * (s * (1.0 + a * (1.0 - s)))).astype(BF16)
        o_ref[1] = (dv * (a * s)).astype(BF16)

    out = _pcall(body, (df, w_out_t, ab), name=name, grid=(D_FFP // tn, t // tm),
                 in_specs=[pl.BlockSpec((tm, D), lambda j, i: (i, 0)), pl.BlockSpec((D, tn), lambda j, i: (0, j)),
                           pl.BlockSpec((2, tm, tn), lambda j, i: (0, i, j))],
                 out_specs=[pl.BlockSpec((2, tm, tn), lambda j, i: (0, i, j))],
                 out_shape=[jax.ShapeDtypeStruct((2, t, D_FFP), BF16)], sem=("parallel", "parallel"), comm=comm, fwd=0.85)
    return (out[0][0], out[1]) if comm else out[0]


def ada_fwd(c8, w_loc, b_loc, name):
    n = w_loc.shape[1]
    tn = 256

    def body(c_ref, w_ref, b_ref, o_ref):
        cv = c_ref[...]
        ca = cv * _sigmoid(cv)
        o_ref[...] = jnp.dot(ca, w_ref[...], precision=HI, preferred_element_type=F32) + b_ref[...]

    return pl.pallas_call(
        body, name=name, grid=(n // tn,),
        in_specs=[pl.BlockSpec((N_DEV, D), lambda j: (0, 0)), pl.BlockSpec((D, tn), lambda j: (0, j)),
                  pl.BlockSpec((1, tn), lambda j: (0, j))],
        out_specs=pl.BlockSpec((N_DEV, tn), lambda j: (0, j)),
        out_shape=jax.ShapeDtypeStruct((N_DEV, n), F32), compiler_params=_cp("parallel"),
    )(c8, w_loc, b_loc)


def ada_bwd(c8t, dm_loc, name):
    n = dm_loc.shape[1]
    tn = 256

    def body(c_ref, d_ref, o_ref):
        cv = c_ref[...]
        ca = cv * _sigmoid(cv)
        o_ref[...] = jnp.dot(ca, d_ref[...], precision=HI, preferred_element_type=F32)

    return pl.pallas_call(
        body, name=name, grid=(n // tn,),
        in_specs=[pl.BlockSpec((D, N_DEV), lambda j: (0, 0)), pl.BlockSpec((N_DEV, tn), lambda j: (0, j))],
        out_specs=pl.BlockSpec((D, tn), lambda j: (0, j)),
        out_shape=jax.ShapeDtypeStruct((D, n), F32), compiler_params=_cp("parallel"),
    )(c8t, dm_loc)


def mod_fwd(x, nw, mods, shk, sck, *, f=None, gk=None, gscale=1.0, name):
    t = x.shape[0]
    tm = min(256, t)
    res = f is not None

    def body(*refs):
        if res:
            x_ref, f_ref, g_ref, nw_ref, sh_ref, sc_ref, x1_ref, h_ref = refs
            xv = x_ref[...] + (gscale * g_ref[...]) * f_ref[...].astype(F32)
            x1_ref[...] = xv
        else:
            x_ref, nw_ref, sh_ref, sc_ref, h_ref = refs
            xv = x_ref[...]
        r = lax.rsqrt(jnp.mean(xv * xv, axis=-1, keepdims=True) + EPS)
        h_ref[...] = ((xv * r * nw_ref[...]) * (1.0 + sc_ref[...]) + sh_ref[...]).astype(BF16)

    ins = [x] + ([f, mods] if res else []) + [nw, mods, mods]
    specs = [_row(tm, D)] + ([_row(tm, D), _vec(D, gk)] if res else []) + [_vec(D), _vec(D, shk), _vec(D, sck)]
    outs = ([jax.ShapeDtypeStruct((t, D), F32)] if res else []) + [jax.ShapeDtypeStruct((t, D), BF16)]
    ospecs = ([_row(tm, D)] if res else []) + [_row(tm, D)]
    out = pl.pallas_call(body, name=name, grid=(t // tm,), in_specs=specs, out_specs=ospecs, out_shape=outs,
                         compiler_params=_cp("parallel"))(*ins)
    return out if res else out[0]


def final_fwd_bwd(x2, f3, mods, nf, tgt, name):
    t = x2.shape[0]
    tm = min(128, t)

    def body(x_ref, f_ref, g_ref, nf_ref, t_ref, dx_ref, df_ref, st_ref):
        @pl.when(pl.program_id(0) == 0)
        def _():
            st_ref[...] = jnp.zeros_like(st_ref)

        g = 0.5 * g_ref[...]
        fv = f_ref[...].astype(F32)
        xv = x_ref[...] + g * fv
        r = lax.rsqrt(jnp.mean(xv * xv, axis=-1, keepdims=True) + EPS)
        xh = xv * r
        nfv = nf_ref[...]
        e = xh * nfv - t_ref[...]
        st_ref[2:3, :] += jnp.sum(e * e, axis=0, keepdims=True)
        dy = e * (1.0 / D)
        st_ref[0:1, :] += jnp.sum(dy * xh, axis=0, keepdims=True)
        dxh = dy * nfv
        dx = r * (dxh - xh * jnp.mean(dxh * xh, axis=-1, keepdims=True))
        dx_ref[...] = dx
        df_ref[...] = (g * dx).astype(BF16)
        st_ref[1:2, :] += 0.5 * jnp.sum(fv * dx, axis=0, keepdims=True)

    return pl.pallas_call(
        body, name=name, grid=(t // tm,),
        in_specs=[_row(tm, D), _row(tm, D), _vec(D, 8), _vec(D), _row(tm, D)],
        out_specs=[_row(tm, D), _row(tm, D), _stats(D)],
        out_shape=[jax.ShapeDtypeStruct((t, D), F32), jax.ShapeDtypeStruct((t, D), BF16),
                   jax.ShapeDtypeStruct((8, D), F32)],
        compiler_params=_cp("arbitrary"),
    )(x2, f3, mods, nf, tgt)


def mod_bwd(x_in, dh, dx_out, nw, mods, sck, *, fprev=None, gk=None, gscale=1.0, name):
    t = x_in.shape[0]
    tm = min(128, t)
    gate = fprev is not None

    def body(*refs):
        if gate:
            x_ref, dh_ref, dxo_ref, nw_ref, sc_ref, f_ref, g_ref, dx_ref, df_ref, st_ref = refs
        else:
            x_ref, dh_ref, dxo_ref, nw_ref, sc_ref, dx_ref, st_ref = refs

        @pl.when(pl.program_id(0) == 0)
        def _():
            st_ref[...] = jnp.zeros_like(st_ref)

        xv = x_ref[...]
        dhv = dh_ref[...].astype(F32)
        r = lax.rsqrt(jnp.mean(xv * xv, axis=-1, keepdims=True) + EPS)
        xh = xv * r
        nwv = nw_ref[...]
        st_ref[0:1, :] += jnp.sum(dhv, axis=0, keepdims=True)
        st_ref[1:2, :] += jnp.sum(dhv * (xh * nwv), axis=0, keepdims=True)
        dn = dhv * (1.0 + sc_ref[...])
        st_ref[2:3, :] += jnp.sum(dn * xh, axis=0, keepdims=True)
        dxh = dn * nwv
        dx = dxo_ref[...] + r * (dxh - xh * jnp.mean(dxh * xh, axis=-1, keepdims=True))
        dx_ref[...] = dx
        if gate:
            df_ref[...] = ((gscale * g_ref[...]) * dx).astype(BF16)
            st_ref[3:4, :] += gscale * jnp.sum(f_ref[...].astype(F32) * dx, axis=0, keepdims=True)

    ins = [x_in, dh, dx_out, nw, mods] + ([fprev, mods] if gate else [])
    specs = [_row(tm, D), _row(tm, D), _row(tm, D), _vec(D), _vec(D, sck)] + ([_row(tm, D), _vec(D, gk)] if gate else [])
    outs = [jax.ShapeDtypeStruct((t, D), F32)] + ([jax.ShapeDtypeStruct((t, D), BF16)] if gate else []) + \
        [jax.ShapeDtypeStruct((8, D), F32)]
    ospecs = [_row(tm, D)] + ([_row(tm, D)] if gate else []) + [_stats(D)]
    return pl.pallas_call(body, name=name, grid=(t // tm,), in_specs=specs, out_specs=ospecs, out_shape=outs,
                          compiler_params=_cp("arbitrary"))(*ins)


def _conv_pre(cur, prev8, w, b, tm):
    full = jnp.concatenate([prev8, cur], axis=0)
    pre = b + w[3:4, :] * cur
    for k in range(CONV_K - 1):
        s = CONV_K - 1 - k
        pre = pre + w[k:k + 1, :] * pltpu.roll(full, s, 0)[8:8 + tm, :]
    return pre


def conv_fwd(proj, cw_full, cb_full, name):
    t = proj.shape[0]
    tm = min(256, t)
    cwid = 1024
    cb0 = P_XBC // cwid

    def body(x_ref, p_ref, w_ref, b_ref, o_ref):
        i = pl.program_id(1)
        prev8 = jnp.where(i == 0, 0.0, p_ref[...].astype(F32)[8:16])
        pre = _conv_pre(x_ref[...].astype(F32), prev8, w_ref[...], b_ref[...], tm)
        o_ref[...] = pre * _sigmoid(pre)

    return pl.pallas_call(
        body, name=name, grid=(CONV_DIM // cwid, t // tm),
        in_specs=[pl.BlockSpec((tm, cwid), lambda j, i: (i, cb0 + j)),
                  pl.BlockSpec((16, cwid), lambda j, i: (jnp.maximum(i * (tm // 16) - 1, 0), cb0 + j)),
                  pl.BlockSpec((CONV_K, cwid), lambda j, i: (0, j)), pl.BlockSpec((1, cwid), lambda j, i: (0, j))],
        out_specs=pl.BlockSpec((tm, cwid), lambda j, i: (i, j)),
        out_shape=jax.ShapeDtypeStruct((t, CONV_DIM), F32), compiler_params=_cp("parallel", "parallel"),
    )(proj, proj, cw_full, cb_full)


def conv_bwd_pre(proj, dxc, cw_full, cb_full, name):
    t = proj.shape[0]
    tm = min(256, t)
    cwid = 1024
    cb0 = P_XBC // cwid

    def body(x_ref, p_ref, d_ref, w_ref, b_ref, o_ref, st_ref):
        i = pl.program_id(1)

        @pl.when(i == 0)
        def _():
            st_ref[...] = jnp.zeros_like(st_ref)

        cur = x_ref[...].astype(F32)
        prev8 = jnp.where(i == 0, 0.0, p_ref[...].astype(F32)[8:16])
        pre = _conv_pre(cur, prev8, w_ref[...], b_ref[...], tm)
        s = _sigmoid(pre)
        dpre = d_ref[...] * (s * (1.0 + pre * (1.0 - s)))
        o_ref[...] = dpre
        st_ref[4:5, :] += jnp.sum(dpre, axis=0, keepdims=True)
        st_ref[3:4, :] += jnp.sum(dpre * cur, axis=0, keepdims=True)
        full = jnp.concatenate([prev8, cur], axis=0)
        for k in range(CONV_K - 1):
            sft = CONV_K - 1 - k
            st_ref[k:k + 1, :] += jnp.sum(dpre * pltpu.roll(full, sft, 0)[8:8 + tm, :], axis=0, keepdims=True)

    return pl.pallas_call(
        body, name=name, grid=(CONV_DIM // cwid, t // tm),
        in_specs=[pl.BlockSpec((tm, cwid), lambda j, i: (i, cb0 + j)),
                  pl.BlockSpec((16, cwid), lambda j, i: (jnp.maximum(i * (tm // 16) - 1, 0), cb0 + j)),
                  pl.BlockSpec((tm, cwid), lambda j, i: (i, j)),
                  pl.BlockSpec((CONV_K, cwid), lambda j, i: (0, j)), pl.BlockSpec((1, cwid), lambda j, i: (0, j))],
        out_specs=[pl.BlockSpec((tm, cwid), lambda j, i: (i, j)), pl.BlockSpec((8, cwid), lambda j, i: (0, j))],
        out_shape=[jax.ShapeDtypeStruct((t, CONV_DIM), F32), jax.ShapeDtypeStruct((8, CONV_DIM), F32)],
        compiler_params=_cp("parallel", "arbitrary"),
    )(proj, proj, dxc, cw_full, cb_full)


def conv_bwd_in(dpre, cw_full, dproj, name):
    t = dpre.shape[0]
    tm = min(256, t)
    cwid = 1024
    nt = t // tm

    def body(d_ref, n_ref, w_ref, _, o_ref):
        i = pl.program_id(1)
        cur = d_ref[...]
        nxt = jnp.where(i == nt - 1, 0.0, n_ref[...])
        full = jnp.concatenate([cur, nxt], axis=0)
        w = w_ref[...]
        acc = w[3:4, :] * cur
        for k in range(CONV_K - 1):
            s = CONV_K - 1 - k
            acc = acc + w[k:k + 1, :] * pltpu.roll(full, tm + 8 - s, 0)[0:tm, :]
        o_ref[...] = acc.astype(BF16)

    return pl.pallas_call(
        body, name=name, grid=(CONV_DIM // cwid, nt),
        in_specs=[pl.BlockSpec((tm, cwid), lambda j, i: (i, j)),
                  pl.BlockSpec((8, cwid), lambda j, i: (jnp.minimum((i + 1) * (tm // 8), t // 8 - 1), j)),
                  pl.BlockSpec((CONV_K, cwid), lambda j, i: (0, j)), _HBM],
        out_specs=pl.BlockSpec((tm, cwid), lambda j, i: (i, P_XBC // cwid + j)),
        out_shape=jax.ShapeDtypeStruct(dproj.shape, BF16), input_output_aliases={3: 0},
        compiler_params=_cp("parallel", "parallel"),
    )(dpre, dpre, cw_full, dproj)


def _nt(a, b):
    return lax.dot_general(a, b, (((1,), (1,)), ((), ())), preferred_element_type=F32)


def _dot(a, b):
    return jnp.dot(a, b, preferred_element_type=F32)


def _head_lanes():
    return lax.broadcasted_iota(jnp.int32, (1, 128), 1) < NH


def _expand_heads(x, e3):
    x = jnp.where(_head_lanes(), x, 0.0)
    hi = x.astype(BF16).astype(F32)
    r1 = x - hi
    mid = r1.astype(BF16).astype(F32)
    packed = hi + pltpu.roll(mid, NH, 1) + pltpu.roll(r1 - mid, 2 * NH, 1)
    return _dot(packed.astype(BF16), e3)


def _reduce_heads(v, e3):
    hi = v.astype(BF16)
    lo = (v - hi.astype(F32)).astype(BF16)
    return jnp.where(_head_lanes(), _nt(hi, e3) + _nt(lo, e3), 0.0)


def _ssd_common(dt_ref, dtb_ref, al_ref, exp_ref, tri_ref):
    a_row = jnp.where(_head_lanes(), -jnp.exp(al_ref[...]), 0.0)
    zraw = dt_ref[...] + dtb_ref[...]
    dtv = _softplus(zraw)
    cs = jnp.dot(tri_ref[...], dtv * a_row, precision=HI, preferred_element_type=F32)
    e3 = exp_ref[...]
    return a_row, zraw, dtv, cs, _expand_heads(cs, e3), _expand_heads(dtv, e3)


def ssd_fwd(xc, proj, dtb_row, alog_row, dx_row, expm, tri, name):
    t = xc.shape[0]
    nc = t // LCH

    def body(xs_ref, bm_ref, cm_ref, dt_ref, dtb_ref, al_ref, dxr_ref, exp_ref, tri_ref, y_ref, hs_ref, h_scr):
        @pl.when(pl.program_id(0) == 0)
        def _():
            h_scr[...] = jnp.zeros_like(h_scr)

        _, _, _, cs, csx, dtx = _ssd_common(dt_ref, dtb_ref, al_ref, exp_ref, tri_ref)
        cst = cs.T
        csl = csx[LCH - 1:LCH, :]
        xs = xs_ref[...]
        xd = xs * dtx
        xdw = xd * jnp.exp(csl - csx)
        ecs = jnp.exp(csx)
        ecl = jnp.exp(csl)
        tril = lax.broadcasted_iota(jnp.int32, (LCH, LCH), 0) >= lax.broadcasted_iota(jnp.int32, (LCH, LCH), 1)
        hs_ref[...] = h_scr[...]
        for g in range(NG):
            gc = slice(g * 512, (g + 1) * 512)
            bm = bm_ref[:, g * NST:(g + 1) * NST]
            cmb = cm_ref[:, g * NST:(g + 1) * NST].astype(BF16)
            gm = _nt(cmb, bm.astype(BF16))
            hg = h_scr[:, gc]
            yo = _dot(cmb, hg.astype(BF16)) * ecs[:, gc]
            st = _dot(bm.T.astype(BF16), xdw[:, gc].astype(BF16))
            for r in range(8):
                h = g * 8 + r
                hc = slice(h * HP, (h + 1) * HP)
                seg = cs[:, h:h + 1] - cst[h:h + 1, :]
                m = (gm * jnp.exp(jnp.where(tril, seg, NEG))).astype(BF16)
                yd = _dot(m, xd[:, hc].astype(BF16))
                y_ref[:, hc] = yd + yo[:, r * HP:(r + 1) * HP] + dxr_ref[:, hc] * xs[:, hc]
            h_scr[:, gc] = ecl[:, gc] * hg + st

    return pl.pallas_call(
        body, name=name, grid=(nc,),
        in_specs=[pl.BlockSpec((LCH, 2048), lambda c: (c, 0)), pl.BlockSpec((LCH, 512), lambda c: (c, 4)),
                  pl.BlockSpec((LCH, 512), lambda c: (c, 5)), pl.BlockSpec((LCH, 128), lambda c: (c, 0)),
                  _vec(128), _vec(128), _vec(2048), pl.BlockSpec((128, 2048), lambda c: (0, 0)),
                  pl.BlockSpec((LCH, LCH), lambda c: (0, 0))],
        out_specs=[pl.BlockSpec((LCH, 2048), lambda c: (c, 0)), pl.BlockSpec((None, NST, 2048), lambda c: (c, 0, 0))],
        out_shape=[jax.ShapeDtypeStruct((t, 2048), F32), jax.ShapeDtypeStruct((nc, NST, 2048), F32)],
        scratch_shapes=[pltpu.VMEM((NST, 2048), F32)],
        compiler_params=_cp("arbitrary"),
    )(xc, xc, xc, proj, dtb_row, alog_row, dx_row, expm, tri)


def ssd_bwd(xc, proj, hsave, dy, dtb_row, alog_row, dx_row, expm, tri, dproj, name):
    t = xc.shape[0]
    nc = t // LCH

    def body(xs_ref, bm_ref, cm_ref, dt_ref, hs_ref, dy_ref, dtb_ref, al_ref, dxr_ref, exp_ref, tri_ref, _,
             dxc_ref, ddt_ref, st_ref, dh_scr, dxd_scr, dcsx_scr):
        @pl.when(pl.program_id(0) == 0)
        def _():
            dh_scr[...] = jnp.zeros_like(dh_scr)
            st_ref[...] = jnp.zeros_like(st_ref)

        a_row, zraw, dtv, cs, csx, dtx = _ssd_common(dt_ref, dtb_ref, al_ref, exp_ref, tri_ref)
        e = exp_ref[...]
        cst = cs.T
        csl = csx[LCH - 1:LCH, :]
        xs = xs_ref[...]
        xd = xs * dtx
        wend = jnp.exp(csl - csx)
        xdw = xd * wend
        ecs = jnp.exp(csx)
        ecl = jnp.exp(csl)
        ri = lax.broadcasted_iota(jnp.int32, (LCH, LCH), 0)
        ci = lax.broadcasted_iota(jnp.int32, (LCH, LCH), 1)
        tril = ri >= ci
        triu = ri <= ci
        lane = lax.broadcasted_iota(jnp.int32, (1, 128), 1)
        dyv = dy_ref[...]
        dxr = dxr_ref[...]
        st_ref[2:3, :] += _reduce_heads(jnp.sum(dyv * xs, axis=0, keepdims=True), e)
        dcs = jnp.zeros((LCH, 128), F32)
        for g in range(NG):
            gc = slice(g * 512, (g + 1) * 512)
            bmb = bm_ref[:, g * NST:(g + 1) * NST].astype(BF16)
            cm = cm_ref[:, g * NST:(g + 1) * NST]
            cmb = cm.astype(BF16)
            hg = hs_ref[:, gc]
            hgb = hg.astype(BF16)
            dhc = dh_scr[:, gc]
            dhcb = dhc.astype(BF16)
            dyg = dyv[:, gc]
            yo = _dot(cmb, hgb) * ecs[:, gc]
            dq = (dyg * ecs[:, gc]).astype(BF16)
            dcm = _nt(dq, hgb)
            dh_yo = _dot(cm.T.astype(BF16), dq)
            dxdw = _dot(bmb, dhcb)
            dbm = _nt(xdw[:, gc].astype(BF16), dhcb)
            tt = dxdw * xdw[:, gc]
            dcsx_g = dyg * yo - tt
            dcsl_g = jnp.sum(tt, axis=0, keepdims=True) + jnp.sum(dhc * hg, axis=0, keepdims=True) * ecl[:, gc]
            dxd_scr[:, gc] = dxdw * wend[:, gc]
            dh_scr[:, gc] = ecl[:, gc] * dhc + dh_yo
            gm = _nt(cmb, bmb)
            gmt = _nt(bmb, cmb)
            dg = jnp.zeros((LCH, LCH), F32)
            dgt = jnp.zeros((LCH, LCH), F32)
            for r in range(8):
                h = g * 8 + r
                hc = slice(h * HP, (h + 1) * HP)
                seg = cs[:, h:h + 1] - cst[h:h + 1, :]
                lm = jnp.exp(jnp.where(tril, seg, NEG))
                lmt = jnp.exp(jnp.where(triu, -seg, NEG))
                mm_ = gm * lm
                mmt = gmt * lmt
                xdh = xd[:, hc].astype(BF16)
                dyh = dyv[:, hc].astype(BF16)
                dm = _nt(dyh, xdh)
                dmt = _nt(xdh, dyh)
                dxd_scr[:, hc] += _dot(mmt.astype(BF16), dyh)
                rs = jnp.sum(dm * mm_, axis=1, keepdims=True) - jnp.sum(dmt * mmt, axis=1, keepdims=True)
                dcs = dcs + rs * jnp.where(lane == h, 1.0, 0.0)
                dg = dg + dm * lm
                dgt = dgt + dmt * lmt
            dcm = dcm + _dot(dg.astype(BF16), bmb)
            dbm = dbm + _dot(dgt.astype(BF16), cmb)
            dxc_ref[:, 2048 + g * NST:2048 + (g + 1) * NST] = dbm
            dxc_ref[:, 2560 + g * NST:2560 + (g + 1) * NST] = dcm
            dcsx_scr[:, gc] = dcsx_g
            dcsx_scr[LCH - 1:LCH, gc] += dcsl_g
        dxd = dxd_scr[...]
        dxc_ref[:, 0:2048] = dxr * dyv + dxd * dtx
        ddtv = _reduce_heads(dxd * xs, e)
        dcs = dcs + _reduce_heads(dcsx_scr[...], e)
        dda =lax.dot_general(tri_ref[...], dcs, (((0,), (0,)), ((), ())), precision=HI, preferred_element_type=F32)
        ddtv = ddtv + dda * a_row
        st_ref[0:1, :] += jnp.sum(dda * dtv, axis=0, keepdims=True) * a_row
        ddt = ddtv * _sigmoid(zraw)
        ddt_ref[:, 0:128] = ddt.astype(BF16)
        ddt_ref[:, 128:INP - P_DT] = jnp.zeros((LCH, INP - P_DT - 128), BF16)
        st_ref[1:2, :] += jnp.sum(ddt, axis=0, keepdims=True)

    rc = lambda c: nc - 1 - c
    return pl.pallas_call(
        body, name=name, grid=(nc,),
        in_specs=[pl.BlockSpec((LCH, 2048), lambda c: (rc(c), 0)), pl.BlockSpec((LCH, 512), lambda c: (rc(c), 4)),
                  pl.BlockSpec((LCH, 512), lambda c: (rc(c), 5)),
                  pl.BlockSpec((LCH, 128), lambda c: (rc(c), 0)),
                  pl.BlockSpec((None, NST, 2048), lambda c: (rc(c), 0, 0)),
                  pl.BlockSpec((LCH, 2048), lambda c: (rc(c), 0)),
                  _vec(128), _vec(128), _vec(2048), pl.BlockSpec((128, 2048), lambda c: (0, 0)),
                  pl.BlockSpec((LCH, LCH), lambda c: (0, 0)), _HBM],
        out_specs=[pl.BlockSpec((LCH, CONV_DIM), lambda c: (rc(c), 0)),
                   pl.BlockSpec((LCH, INP - P_DT), lambda c: (rc(c), P_DT // (INP - P_DT))), _stats(128)],
        out_shape=[jax.ShapeDtypeStruct((t, CONV_DIM), F32), jax.ShapeDtypeStruct(dproj.shape, BF16),
                   jax.ShapeDtypeStruct((8, 128), F32)],
        scratch_shapes=[pltpu.VMEM((NST, 2048), F32), pltpu.VMEM((LCH, 2048), F32), pltpu.VMEM((LCH, 2048), F32)],
        input_output_aliases={11: 1}, compiler_params=_cp("arbitrary"),
    )(xc, xc, xc, proj, hsave, dy, dtb_row, alog_row, dx_row, expm, tri, dproj)


def ssd_out_fwd(y, proj, nw, name):
    t = y.shape[0]
    tm = min(256, t)

    def body(y_ref, z_ref, nw_ref, o_ref):
        for g in range(NG):
            gc = slice(g * 512, (g + 1) * 512)
            z = z_ref[:, gc].astype(F32)
            yz = y_ref[:, gc] * (z * _sigmoid(z))
            r = lax.rsqrt(jnp.mean(yz * yz, axis=-1, keepdims=True) + EPS)
            o_ref[:, gc] = (yz * r * nw_ref[:, gc]).astype(BF16)

    return pl.pallas_call(body, name=name, grid=(t // tm,),
                          in_specs=[_row(tm, 2048), _row(tm, 2048, P_Z // 2048), _vec(2048)],
                          out_specs=_row(tm, 2048), out_shape=jax.ShapeDtypeStruct((t, 2048), BF16),
                          compiler_params=_cp("parallel"))(y, proj, nw)


def ssd_out_bwd(y, proj, dya, nw, dproj, name):
    t = y.shape[0]
    tm = min(256, t)

    def body(y_ref, z_ref, d_ref, nw_ref, _, dy_ref, dz_ref, st_ref):
        @pl.when(pl.program_id(0) == 0)
        def _():
            st_ref[...] = jnp.zeros_like(st_ref)

        for g in range(NG):
            gc = slice(g * 512, (g + 1) * 512)
            z = z_ref[:, gc].astype(F32)
            yv = y_ref[:, gc]
            s = _sigmoid(z)
            sz = z * s
            yz = yv * sz
            r = lax.rsqrt(jnp.mean(yz * yz, axis=-1, keepdims=True) + EPS)
            yzn = yz * r
            dv = d_ref[:, gc].astype(F32)
            st_ref[0:1, gc] += jnp.sum(dv * yzn, axis=0, keepdims=True)
            dyn = dv * nw_ref[:, gc]
            dyz = r * (dyn - yzn * jnp.mean(dyn * yzn, axis=-1, keepdims=True))
            dy_ref[:, gc] = dyz * sz
            dz_ref[:, gc] = (dyz * yv * (s * (1.0 + z * (1.0 - s)))).astype(BF16)

    return pl.pallas_call(
        body, name=name, grid=(t // tm,),
        in_specs=[_row(tm, 2048), _row(tm, 2048, P_Z // 2048), _row(tm, 2048), _vec(2048), _HBM],
        out_specs=[_row(tm, 2048), _row(tm, 2048, P_Z // 2048), _stats(2048)],
        out_shape=[jax.ShapeDtypeStruct((t, 2048), F32), jax.ShapeDtypeStruct(dproj.shape, BF16),
                   jax.ShapeDtypeStruct((8, 2048), F32)],
        input_output_aliases={4: 1}, compiler_params=_cp("arbitrary"))(y, proj, dya, nw, dproj)


def _cstep(ar, ai, sr, si, br, bi):
    return ar * sr - ai * si + br, ar * si + ai * sr + bi


def _halves(v):
    return (v[0:8, 0:512], v[0:8, 512:1024]), (v[8:16, 0:512], v[8:16, 512:1024])


def _slab(r1, i1, r2, i2):
    return jnp.concatenate([jnp.concatenate([r1, i1], axis=1), jnp.concatenate([r2, i2], axis=1)], axis=0).astype(BF16)


def _local_ends(x_ref, nslab, ar, ai, sr_scr, si_scr, end_ref, first, last, reverse):
    @pl.when(first)
    def _():
        sr_scr[...] = jnp.zeros_like(sr_scr)
        si_scr[...] = jnp.zeros_like(si_scr)

    def step(k, carry):
        s1, s2 = _halves(x_ref[nslab - 1 - k if reverse else k].astype(F32))
        if reverse:
            s1, s2 = s2, s1
        return _cstep(ar, ai, *_cstep(ar, ai, carry[0], carry[1], *s1), *s2)

    sr, si = lax.fori_loop(0, nslab, step, (sr_scr[...], si_scr[...]), unroll=4)
    sr_scr[...] = sr
    si_scr[...] = si

    @pl.when(last)
    def _():
        end_ref[:, 0:512] = sr
        end_ref[:, 512:1024] = si


def s5_in(u, bsg, a_re, a_im, name):
    t = u.shape[0]
    tm = min(512, t)
    nt = t // tm

    def body(u_ref, b_ref, ar_ref, ai_ref, o_ref, e_ref, sr_scr, si_scr):
        i = pl.program_id(1)
        o_ref[...] = _dot(u_ref[...].astype(BF16), b_ref[...]).astype(BF16).reshape(tm // 16, 16, 1024)
        _local_ends(o_ref, tm // 16, ar_ref[...], ai_ref[...], sr_scr, si_scr, e_ref, i == 0, i == nt - 1, False)

    return pl.pallas_call(
        body, name=name, grid=(8, nt),
        in_specs=[pl.BlockSpec((tm, 128), lambda s, i: (i, s)), pl.BlockSpec((None, 128, 1024), lambda s, i: (s, 0, 0)),
                  pl.BlockSpec((None, 8, 512), lambda s, i: (s, 0, 0)), pl.BlockSpec((None, 8, 512), lambda s, i: (s, 0, 0))],
        out_specs=[pl.BlockSpec((tm // 16, 16, 1024), lambda s, i: (i, 0, s)),
                   pl.BlockSpec((None, 8, 1024), lambda s, i: (s, 0, 0))],
        out_shape=[jax.ShapeDtypeStruct((t // 16, 16, S5NS), BF16), jax.ShapeDtypeStruct((8, 8, 1024), F32)],
        scratch_shapes=[pltpu.VMEM((8, 512), F32), pltpu.VMEM((8, 512), F32)],
        compiler_params=_cp("parallel", "arbitrary"))(u, bsg, a_re, a_im)


def s5_out_bwd(dy, csg, s, a_re, a_im, name):
    t = dy.shape[0]
    tm = min(512, t)
    nt = t // tm

    def body(dy_ref, c_ref, s_ref, ar_ref, ai_ref, e_ref, dc_ref, end_ref, sr_scr, si_scr):
        i = pl.program_id(1)

        @pl.when(i == 0)
        def _():
            dc_ref[...] = jnp.zeros_like(dc_ref)

        dyb = dy_ref[...].astype(BF16)
        e_ref[...] = _nt(dyb, c_ref[...]).astype(BF16).reshape(tm // 16, 16, 1024)
        dc_ref[...] += lax.dot_general(s_ref[...], dyb, (((0,), (0,)), ((), ())), preferred_element_type=F32)
        _local_ends(e_ref, tm // 16, ar_ref[...], -ai_ref[...], sr_scr, si_scr, end_ref, i == 0, i == nt - 1, True)

    rv = lambda i: nt - 1 - i
    return pl.pallas_call(
        body, name=name, grid=(8, nt),
        in_specs=[pl.BlockSpec((tm, 128), lambda s, i: (rv(i), s)), pl.BlockSpec((None, 1024, 128), lambda s, i: (s, 0, 0)),
                  pl.BlockSpec((tm, 1024), lambda s, i: (rv(i), s)),
                  pl.BlockSpec((None, 8, 512), lambda s, i: (s, 0, 0)), pl.BlockSpec((None, 8, 512), lambda s, i: (s, 0, 0))],
        out_specs=[pl.BlockSpec((tm // 16, 16, 1024), lambda s, i: (rv(i), 0, s)),
                   pl.BlockSpec((None, 1024, 128), lambda s, i: (s, 0, 0)),
                   pl.BlockSpec((None, 8, 1024), lambda s, i: (s, 0, 0))],
        out_shape=[jax.ShapeDtypeStruct((t // 16, 16, S5NS), BF16), jax.ShapeDtypeStruct((8, 1024, 128), F32),
                   jax.ShapeDtypeStruct((8, 8, 1024), F32)],
        scratch_shapes=[pltpu.VMEM((8, 512), F32), pltpu.VMEM((8, 512), F32)],
        compiler_params=_cp("parallel", "arbitrary"))(dy, csg, s, a_re, a_im)


def s5_scan_init(ends, a_re, a_im, lseg, reverse, name):
    nsq = int(math.log2(lseg))
    assert 2 ** nsq == lseg
    sgn = -1.0 if reverse else 1.0
    order = list(range(7, -1, -1)) if reverse else list(range(8))

    def body(e_ref, ar_ref, ai_ref, o_ref):
        pr = ar_ref[0:1, :]
        pi = sgn * ai_ref[0:1, :]
        for _ in range(nsq):
            pr, pi = pr * pr - pi * pi, 2.0 * pr * pi
        prev_r = jnp.zeros((1, 512), F32)
        prev_i = jnp.zeros((1, 512), F32)
        j0 = order[0]
        o_ref[j0:j0 + 1, 0:512] = prev_r
        o_ref[j0:j0 + 1, 512:1024] = prev_i
        for idx in range(1, 8):
            j, jp = order[idx], order[idx - 1]
            prev_r, prev_i = _cstep(pr, pi, prev_r, prev_i, e_ref[jp:jp + 1, 0:512], e_ref[jp:jp + 1, 512:1024])
            o_ref[j:j + 1, 0:512] = prev_r
            o_ref[j:j + 1, 512:1024] = prev_i

    return pl.pallas_call(
        body, name=name, grid=(8,),
        in_specs=[pl.BlockSpec((None, 8, 1024), lambda s: (s, 0, 0)), pl.BlockSpec((None, 8, 512), lambda s: (s, 0, 0)),
                  pl.BlockSpec((None, 8, 512), lambda s: (s, 0, 0))],
        out_specs=pl.BlockSpec((None, 8, 1024), lambda s: (s, 0, 0)),
        out_shape=jax.ShapeDtypeStruct((8, 8, 1024), F32), compiler_params=_cp("parallel"))(ends, a_re, a_im)


def s5_scan_fwd(b3, init, a_re, a_im, csg, u, d_row, name):
    nslab = b3.shape[0]
    ti = min(64, nslab)
    nb = nslab // ti
    rows = 16 * ti

    def body(b_ref, i_ref, ar_ref, ai_ref, c_ref, u_ref, d_ref, o_ref, y_ref, sr_scr, si_scr):
        @pl.when(pl.program_id(1) == 0)
        def _():
            sr_scr[...] = i_ref[:, 0:512]
            si_scr[...] = i_ref[:, 512:1024]

        ar = ar_ref[...]
        ai = ai_ref[...]

        def step(k, carry):
            b1, b2 = _halves(b_ref[k].astype(F32))
            r1, i1 = _cstep(ar, ai, carry[0], carry[1], *b1)
            r2, i2 = _cstep(ar, ai, r1, i1, *b2)
            o_ref[k] = _slab(r1, i1, r2, i2)
            return r2, i2

        sr, si = lax.fori_loop(0, ti, step, (sr_scr[...], si_scr[...]), unroll=4)
        sr_scr[...] = sr
        si_scr[...] = si
        y_ref[...] = _dot(o_ref[...].reshape(rows, 1024), c_ref[...]) + d_ref[...] * u_ref[...].astype(F32)

    return pl.pallas_call(
        body, name=name, grid=(8, nb),
        in_specs=[pl.BlockSpec((ti, 16, 1024), lambda s, tb: (tb, 0, s)), pl.BlockSpec((None, 8, 1024), lambda s, tb: (s, 0, 0)),
                  pl.BlockSpec((None, 8, 512), lambda s, tb: (s, 0, 0)), pl.BlockSpec((None, 8, 512), lambda s, tb: (s, 0, 0)),
                  pl.BlockSpec((None, 1024, 128), lambda s, tb: (s, 0, 0)), pl.BlockSpec((rows, 128), lambda s, tb: (tb, s)),
                  pl.BlockSpec((1, 128), lambda s, tb: (0, s))],
        out_specs=[pl.BlockSpec((ti, 16, 1024), lambda s, tb: (tb, 0, s)), pl.BlockSpec((rows, 128), lambda s, tb: (tb, s))],
        out_shape=[jax.ShapeDtypeStruct(b3.shape, BF16), jax.ShapeDtypeStruct((16 * nslab, S5W), F32)],
        scratch_shapes=[pltpu.VMEM((8, 512), F32), pltpu.VMEM((8, 512), F32)],
        compiler_params=_cp("parallel", "arbitrary"))(b3, init, a_re, a_im, csg, u, d_row)


def s5_scan_bwd(e3, linit, s3, sinit, a_re, a_im, bsg, u, dy, d_row, name):
    nslab = e3.shape[0]
    ti = min(64, nslab)
    nb = nslab // ti
    rows = 16 * ti

    def body(e_ref, li_ref, s_ref, sh_ref, si0_ref, ar_ref, ai_ref, b_ref, u_ref, dy_ref, d_ref,
             du_ref, db_ref, dd_ref, da_ref, o_ref, lr_scr, lim_scr):
        tb = pl.program_id(1)

        @pl.when(tb == 0)
        def _():
            lr_scr[...] = li_ref[:, 0:512]
            lim_scr[...] = li_ref[:, 512:1024]
            da_ref[...] = jnp.zeros_like(da_ref)
            db_ref[...] = jnp.zeros_like(db_ref)
            dd_ref[...] = jnp.zeros_like(dd_ref)

        ar = ar_ref[...]
        ai = -ai_ref[...]

        def slab(kk, lr, li, dar, dai, sp):
            e1, e2 = _halves(e_ref[kk].astype(F32))
            s1, _ = _halves(s_ref[kk].astype(F32))
            r2, i2 = _cstep(ar, ai, lr, li, *e2)
            dar = dar + r2 * s1[0] + i2 * s1[1]
            dai = dai + i2 * s1[0] - r2 * s1[1]
            r1, i1 = _cstep(ar, ai, r2, i2, *e1)
            dar = dar + r1 * sp[0] + i1 * sp[1]
            dai = dai + i1 * sp[0] - r1 * sp[1]
            o_ref[kk] = _slab(r1, i1, r2, i2)
            return r1, i1, dar, dai

        def step(k, carry):
            kk = ti - 1 - k
            return slab(kk, *carry, _halves(s_ref[kk - 1].astype(F32))[1])

        z = jnp.zeros((8, 512), F32)
        lr, li, dar, dai = lax.fori_loop(0, ti - 1, step, (lr_scr[...], lim_scr[...], z, z), unroll=2)
        first = tb == nb - 1
        halo = _halves(sh_ref[0].astype(F32))[1]
        sp = (jnp.where(first, si0_ref[:, 0:512], halo[0]), jnp.where(first, si0_ref[:, 512:1024], halo[1]))
        lr, li, dar, dai = slab(0, lr, li, dar, dai, sp)
        lr_scr[...] = lr
        lim_scr[...] = li
        da_ref[:, 0:512] += dar
        da_ref[:, 512:1024] += dai
        lb = o_ref[...].reshape(rows, 1024)
        uv = u_ref[...].astype(F32)
        dyv = dy_ref[...]
        du_ref[...] = _nt(lb, b_ref[...]) + d_ref[...] * dyv
        db_ref[...] += lax.dot_general(uv.astype(BF16), lb, (((0,), (0,)), ((), ())), preferred_element_type=F32)
        dd_ref[...] += jnp.sum(dyv * uv, axis=0, keepdims=True)

    rb = lambda tb: nb - 1 - tb
    return pl.pallas_call(
        body, name=name, grid=(8, nb),
        in_specs=[pl.BlockSpec((ti, 16, 1024), lambda s, tb: (rb(tb), 0, s)),
                  pl.BlockSpec((None, 8, 1024), lambda s, tb: (s, 0, 0)),
                  pl.BlockSpec((ti, 16, 1024), lambda s, tb: (rb(tb), 0, s)),
                  pl.BlockSpec((1, 16, 1024), lambda s, tb: (jnp.maximum(rb(tb) * ti - 1, 0), 0, s)),
                  pl.BlockSpec((None, 8, 1024), lambda s, tb: (s, 0, 0)),
                  pl.BlockSpec((None, 8, 512), lambda s, tb: (s, 0, 0)), pl.BlockSpec((None, 8, 512), lambda s, tb: (s, 0, 0)),
                  pl.BlockSpec((None, 128, 1024), lambda s, tb: (s, 0, 0)),
                  pl.BlockSpec((rows, 128), lambda s, tb: (rb(tb), s)), pl.BlockSpec((rows, 128), lambda s, tb: (rb(tb), s)),
                  pl.BlockSpec((1, 128), lambda s, tb: (0, s))],
        out_specs=[pl.BlockSpec((rows, 128), lambda s, tb: (rb(tb), s)),
                   pl.BlockSpec((None, 128, 1024), lambda s, tb: (s, 0, 0)), pl.BlockSpec((1, 128), lambda s, tb: (0, s)),
                   pl.BlockSpec((None, 8, 1024), lambda s, tb: (s, 0, 0))],
        out_shape=[jax.ShapeDtypeStruct((16 * nslab, S5W), F32), jax.ShapeDtypeStruct((8, 128, 1024), F32),
                   jax.ShapeDtypeStruct((1, S5W), F32), jax.ShapeDtypeStruct((8, 8, 1024), F32)],
        scratch_shapes=[pltpu.VMEM((ti, 16, 1024), BF16), pltpu.VMEM((8, 512), F32), pltpu.VMEM((8, 512), F32)],
        compiler_params=_cp("parallel", "arbitrary"))(e3, linit, s3, s3, sinit, a_re, a_im, bsg, u, dy, d_row)


_GC = math.sqrt(2.0 / math.pi)


def gelu_fwd(y, name):
    t, w = y.shape
    tm = min(512, t)

    def body(y_ref, o_ref):
        v = y_ref[...]
        o_ref[...] = (0.5 * v * (1.0 + jnp.tanh(_GC * (v + 0.044715 * v * v * v)))).astype(BF16)

    return pl.pallas_call(body, name=name, grid=(t // tm,), in_specs=[_row(tm, w)], out_specs=_row(tm, w),
                          out_shape=jax.ShapeDtypeStruct((t, w), BF16), compiler_params=_cp("parallel"))(y)


def gelu_bwd(y, dg, name):
    t, w = y.shape
    tm = min(512, t)

    def body(y_ref, d_ref, o_ref):
        v = y_ref[...]
        th = jnp.tanh(_GC * (v + 0.044715 * v * v * v))
        o_ref[...] = d_ref[...].astype(F32) * (0.5 * (1.0 + th) + 0.5 * v * (1.0 - th * th) * _GC * (1.0 + 3.0 * 0.044715 * v * v))

    return pl.pallas_call(body, name=name, grid=(t // tm,), in_specs=[_row(tm, w), _row(tm, w)], out_specs=_row(tm, w),
                          out_shape=jax.ShapeDtypeStruct((t, w), F32), compiler_params=_cp("parallel"))(y, dg)


def merge_fwd(proj, pa, glu, name):
    t = pa.shape[0]
    tm = min(256, t)

    def body(g_ref, pa_ref, glu_ref, o_ref):
        pb = glu_ref[:, 0:D].astype(F32) * _sigmoid(glu_ref[:, D:2 * D].astype(F32))
        o_ref[...] = (_sigmoid(g_ref[:, 0:D].astype(F32)) * pa_ref[...].astype(F32)
                      + _sigmoid(g_ref[:, D:2 * D].astype(F32)) * pb).astype(BF16)

    return pl.pallas_call(body, name=name, grid=(t // tm,), in_specs=[_row(tm, 2 * D), _row(tm, D), _row(tm, 2 * D)],
                          out_specs=_row(tm, D), out_shape=jax.ShapeDtypeStruct((t, D), BF16),
                          compiler_params=_cp("parallel"))(proj, pa, glu)


def merge_bwd(proj, pa, glu, dm, dproj, name):
    t = pa.shape[0]
    tm = min(256, t)

    def body(g_ref, pa_ref, glu_ref, dm_ref, _, dpa_ref, dglu_ref, dg_ref):
        dmv = dm_ref[...].astype(F32)
        pav = pa_ref[...].astype(F32)
        sa = _sigmoid(g_ref[:, 0:D].astype(F32))
        sb = _sigmoid(g_ref[:, D:2 * D].astype(F32))
        ga = glu_ref[:, 0:D].astype(F32)
        sg = _sigmoid(glu_ref[:, D:2 * D].astype(F32))
        pb = ga * sg
        dpb = sb * dmv
        dpa_ref[...] = (sa * dmv).astype(BF16)
        dglu_ref[:, 0:D] = (dpb * sg).astype(BF16)
        dglu_ref[:, D:2 * D] = (dpb * pb * (1.0 - sg)).astype(BF16)
        dg_ref[:, 0:D] = (dmv * pav * sa * (1.0 - sa)).astype(BF16)
        dg_ref[:, D:2 * D] = (dmv * pb * sb * (1.0 - sb)).astype(BF16)

    return pl.pallas_call(
        body, name=name, grid=(t // tm,),
        in_specs=[_row(tm, 2 * D), _row(tm, D), _row(tm, 2 * D), _row(tm, D), _HBM],
        out_specs=[_row(tm, D), _row(tm, 2 * D), _row(tm, 2 * D, P_GATES // (2 * D))],
        out_shape=[jax.ShapeDtypeStruct((t, D), BF16), jax.ShapeDtypeStruct((t, 2 * D), BF16),
                   jax.ShapeDtypeStruct(dproj.shape, BF16)],
        input_output_aliases={4: 2}, compiler_params=_cp("parallel"))(proj, pa, glu, dm, dproj)


def adamw(w, parts, m, v, name):
    r, c = w.shape
    p = parts.shape[0]
    tr = r if r <= 128 else 128
    c1 = 1.0 - ADAM_B1 ** ADAM_STEP
    c2 = 1.0 - ADAM_B2 ** ADAM_STEP

    def body(w_ref, p_ref, m_ref, v_ref, g_ref, d_ref, nm_ref, nv_ref):
        g = p_ref[0].astype(F32)
        for k in range(1, p):
            g = g + p_ref[k].astype(F32)
        mn = ADAM_B1 * m_ref[...] + (1.0 - ADAM_B1) * g
        vn = ADAM_B2 * v_ref[...] + (1.0 - ADAM_B2) * (g * g)
        g_ref[...] = g
        nm_ref[...] = mn
        nv_ref[...] = vn
        d_ref[...] = -ADAM_LR * ((mn / c1) / (jnp.sqrt(vn / c2) + ADAM_EPS) + ADAM_WD * w_ref[...])

    spec = pl.BlockSpec((tr, c), lambda i: (i, 0))
    o = jax.ShapeDtypeStruct((r, c), F32)
    return pl.pallas_call(
        body, name=name, grid=(pl.cdiv(r, tr),),
        in_specs=[spec, pl.BlockSpec((p, tr, c), lambda i: (0, i, 0)), spec, spec],
        out_specs=[spec, spec, spec, spec], out_shape=[o, o, o, o], compiler_params=_cp("parallel"))(w, parts, m, v)


def _s5_discretise(lambda_re, lambda_im, log_dt, b_re, b_im):
    dt = jnp.exp(log_dt)[:, None]
    lr = jnp.minimum(lambda_re, -1e-4)
    li = lambda_im
    mag = jnp.exp(lr * dt)
    ar = mag * jnp.cos(li * dt)
    ai = mag * jnp.sin(li * dt)
    den = lr * lr + li * li
    nr = ar - 1.0
    kr = (nr * lr + ai * li) / den
    ki = (ai * lr - nr * li) / den
    bbar_re = kr[..., None] * b_re - ki[..., None] * b_im
    bbar_im = kr[..., None] * b_im + ki[..., None] * b_re
    return ar, ai, bbar_re, bbar_im


def _block_diag(v):
    a, b = v.shape[2], v.shape[3]
    eye = jnp.eye(8, dtype=v.dtype)[None, :, None, :, None]
    return (v[:, :, :, None, :] * eye).reshape(8, 8 * a, 8 * b)


def _diag_blocks(m, a, b):
    eye = jnp.eye(8, dtype=m.dtype)[None, :, None, :, None]
    return jnp.sum(m.reshape(8, 8, a, 8, b) * eye, axis=3)


def _bsg_of(bb_re, bb_im):
    f = lambda b: _block_diag(b.reshape(8, 8, 64, 16).transpose(0, 1, 3, 2))
    return jnp.concatenate([f(bb_re), f(bb_im)], axis=2)


def _bsg_diag(dbsg):
    f = lambda x: _diag_blocks(x, 16, 64).transpose(0, 1, 3, 2).reshape(64, 64, 16)
    return f(dbsg[:, :, 0:512]), f(dbsg[:, :, 512:1024])


def _csg_of(c_re, c_im):
    f = lambda c: _block_diag(c.reshape(8, 8, 16, 64).transpose(0, 1, 3, 2))
    return jnp.concatenate([f(c_re), -f(c_im)], axis=1)


def _csg_diag(dcsg):
    f = lambda x: _diag_blocks(x, 64, 16).transpose(0, 1, 3, 2).reshape(64, 16, 64)
    return f(dcsg[:, 0:512, :]), -f(dcsg[:, 512:1024, :])


def _perm(a, t):
    return a.reshape(8, t // 8, a.shape[1]).transpose(1, 0, 2).reshape(t, a.shape[1])


def _unperm(a, t):
    return a.reshape(t // 8, 8, a.shape[1]).transpose(1, 0, 2).reshape(t, a.shape[1])


def _cols(g):
    return g.transpose(1, 0, 2).reshape(g.shape[1], N_DEV * g.shape[2])


def _rows(g):
    return g.reshape(N_DEV * g.shape[1], g.shape[2])


def _col_parts(g):
    r, c = g.shape
    return g.reshape(r, N_DEV, c // N_DEV).transpose(1, 0, 2)


def _row_parts(g):
    r, c = g.shape
    return g.reshape(N_DEV, r // N_DEV, c)


FB, FBP = D_FF // N_DEV, D_FFP // N_DEV


def _pad_ffn_in_shard(w):
    return jnp.pad(w.reshape(D, 2, FB), ((0, 0), (0, 0), (0, FBP - FB))).reshape(D, 2 * FBP)


def _unpad_ffn_in_shard(g):
    return g.reshape(D, 2, FBP)[:, :, :FB].reshape(D, 2 * FB)


def _pad_ffn_out_shard(w):
    return jnp.pad(w, ((0, FBP - FB), (0, 0)))


def sum_parts(parts, name):
    p, r, c = parts.shape
    tr = 256

    def body(p_ref, o_ref):
        g = p_ref[0].astype(F32)
        for k in range(1, p):
            g = g + p_ref[k].astype(F32)
        o_ref[...] = g

    return pl.pallas_call(body, name=name, grid=(r // tr,), in_specs=[pl.BlockSpec((p, tr, c), lambda i: (0, i, 0))],
                          out_specs=pl.BlockSpec((tr, c), lambda i: (i, 0)),
                          out_shape=jax.ShapeDtypeStruct((r, c), F32), compiler_params=_cp("parallel"))(parts)


def _pad_w_in(w):
    z = jnp.zeros((D, INP - P_DT - NH), w.dtype)
    return jnp.concatenate([w[:, O_GA:O_GB], w[:, O_GB:IN_COLS], w[:, 0:O_XBC], w[:, O_XBC:O_DT], w[:, O_U:O_GA],
                            w[:, O_DT:O_U], z], axis=1)


def _unpad_w_in(g):
    return jnp.concatenate([g[:, P_Z:P_XBC], g[:, P_XBC:P_U], g[:, P_DT:P_DT + NH], g[:, P_U:P_DT],
                            g[:, 0:D], g[:, D:2 * D]], axis=1)


_PACK = (("b_ada", 18432), ("norm_ffn1", 2048), ("norm_mix", 2048), ("conv_b", 3072), ("dt_bias", 32), ("a_log", 32),
         ("d_ssd", 32), ("ssd_norm_w", 2048), ("s5_lambda_re", 4096), ("s5_lambda_im", 4096), ("s5_b_re", 65536),
         ("s5_b_im", 65536), ("s5_c_re", 65536), ("s5_c_im", 65536), ("s5_d", 1024), ("s5_log_dt", 64),
         ("norm_ffn2", 2048), ("norm_final", 2048), ("loss", 1))
_PACK_ROWS = 304
_PACK_W = 1024


def _pack(d):
    flat = jnp.concatenate([d[k].reshape(-1).astype(F32) for k, _ in _PACK])
    return jnp.pad(flat, (0, _PACK_ROWS * _PACK_W - flat.shape[0])).reshape(_PACK_ROWS, _PACK_W)


def _unpack(a):
    flat = a.reshape(-1)
    out, off = {}, 0
    for k, n in _PACK:
        out[k] = flat[off:off + n]
        off += n
    return out


_TA = dict(tm=512, tn=512, tk=8192)


def _ffn_bwd(df, h, ab, act, w_in_shard_t, w_out_t, tag):
    dab, (g_wt,) = ffn_dab(df, w_out_t, ab, tag + "_dab", comm=[("ag", w_in_shard_t)])
    w_in_t = _rows(g_wt)
    dw_out = mm(act, df, ta=True, out_dtype=BF16, i_outer=True, name=tag + "_dwout", **_TA)
    dw_in, (x_out,) = mm(h, dab, ta=True, b_halves=True, out_dtype=BF16, i_outer=True, name=tag + "_dwin",
                         comm=[("xc", _row_parts(dw_out))], **_TA)
    dh, (x_in,) = mm(dab, w_in_t, a_halves=True, out_dtype=BF16, tk=5632, name=tag + "_dh", comm=[("xcc", dw_in)])
    g_in = _unpad_ffn_in_shard(sum_parts(x_in, tag + "_dwin_sum"))
    return dh, g_in[None], x_out


def kernel(x, c, w_ada, b_ada, norm_ffn1, w_ffn1_in, w_ffn1_out, norm_mix, w_in, conv_w, conv_b, dt_bias, a_log, d_ssd, ssd_norm_w, w_a_proj, s5_lambda_re, s5_lambda_im, s5_b_re, s5_b_im, s5_c_re, s5_c_im, s5_d, s5_log_dt, w_b_glu, w_out, norm_ffn2, w_ffn2_in, w_ffn2_out, norm_final, loss_target, m_w_ada, m_b_ada, m_norm_ffn1, m_w_ffn1_in, m_w_ffn1_out, m_norm_mix, m_w_in, m_conv_w, m_conv_b, m_dt_bias, m_a_log, m_d_ssd, m_ssd_norm_w, m_w_a_proj, m_s5_lambda_re, m_s5_lambda_im, m_s5_b_re, m_s5_b_im, m_s5_c_re, m_s5_c_im, m_s5_d, m_s5_log_dt, m_w_b_glu, m_w_out, m_norm_ffn2, m_w_ffn2_in, m_w_ffn2_out, m_norm_final, v_w_ada, v_b_ada, v_norm_ffn1, v_w_ffn1_in, v_w_ffn1_out, v_norm_mix, v_w_in, v_conv_w, v_conv_b, v_dt_bias, v_a_log, v_d_ssd, v_ssd_norm_w, v_w_a_proj, v_s5_lambda_re, v_s5_lambda_im, v_s5_b_re, v_s5_b_im, v_s5_c_re, v_s5_c_im, v_s5_d, v_s5_log_dt, v_w_b_glu, v_w_out, v_norm_ffn2, v_w_ffn2_in, v_w_ffn2_out, v_norm_final):
    args = dict(locals())
    t = x.shape[1]
    me = _my_id()
    xt = x[0]
    tgt = loss_target[0]
    small = {k: args[k] for k, _ in _PACK if k != "loss"}

    bf = lambda w: w[0].astype(BF16)
    ffn_in_shard = lambda w: _pad_ffn_in_shard(bf(w))
    ffn_out_shard = lambda w: _pad_ffn_out_shard(bf(w))

    c8 = all_gather(c, "ag_c").reshape(N_DEV, D)
    b_loc = lax.dynamic_slice(b_ada, (0, me * (N_ADA * D // N_DEV)), (1, N_ADA * D // N_DEV))
    m8 = ada_fwd(c8, w_ada[0], b_loc, "ada_fwd")
    mods, wf1i, g_cw = comm_call([("xc", m8.reshape(N_DEV, 1, -1)), ("agc", ffn_in_shard(w_ffn1_in)),
                                  ("ag", conv_w[0])], "xc_mods_ag_ffn1_in")
    mods = mods.reshape(1, N_ADA * D)
    convw = _cols(g_cw)

    h1 = mod_fwd(xt, norm_ffn1, mods, 0, 1, name="mod1")
    ab1, act1, (g_f1o, g_win, g_wap) = ffn_in_act(
        h1, wf1i, "ffn1_in", fwd=0.9,
        comm=[("ag", ffn_out_shard(w_ffn1_out)), ("ag", bf(w_in)), ("ag", bf(w_a_proj))])
    wf1o, winp, wap = _rows(g_f1o), _pad_w_in(_cols(g_win)), _rows(g_wap)
    f1, (g_wo, g_wbg) = mm(act1, wf1o, out_dtype=BF16, tk=5632, name="ffn1_out",
                           comm=[("ag", bf(w_out)), ("ag", bf(w_b_glu))])
    wo, wbg = _rows(g_wo), _cols(g_wbg)
    x1, h2 = mod_fwd(xt, norm_mix, mods, 3, 4, f=f1, gk=2, gscale=0.5, name="mod2")
    proj, (wf2i,) = mm(h2, winp, out_dtype=BF16, tm=1024, tn=512, i_outer=True, name="w_in",
                       comm=[("agc", ffn_in_shard(w_ffn2_in))])
    dtraw = mm(h2, winp[:, P_DT:P_DT + 128], tn=128, name="w_in_dt")
    cb_row = conv_b
    xc = conv_fwd(proj, convw, cb_row, "conv_fwd")
    row128 = lambda v: jnp.pad(v.reshape(1, -1), ((0, 0), (0, 128 - v.size)))
    dtb_row, alog_row = row128(dt_bias), row128(a_log)
    dx_row = jnp.repeat(d_ssd.reshape(-1), HP).reshape(1, 2048)
    rows = jnp.arange(128)[:, None]
    expm = ((rows % NH == jnp.arange(2048)[None, :] // HP) & (rows < 3 * NH)).astype(BF16)
    tri = (jnp.arange(LCH)[:, None] >= jnp.arange(LCH)[None, :]).astype(F32)
    y_ssd, hsave = ssd_fwd(xc, dtraw, dtb_row, alog_row, dx_row, expm, tri, "ssd_fwd")
    ya = ssd_out_fwd(y_ssd, proj, ssd_norm_w, "ssd_out")
    pa = mm(ya, wap, out_dtype=BF16, name="w_a_proj")

    s5p = (s5_lambda_re[0], s5_lambda_im[0], s5_log_dt[0], s5_b_re[0], s5_b_im[0])
    (ar, ai, bb_re, bb_im), s5_vjp = jax.vjp(_s5_discretise, *s5p)
    a_re8 = jnp.broadcast_to(ar.reshape(8, 1, 512), (8, 8, 512))
    a_im8 = jnp.broadcast_to(ai.reshape(8, 1, 512), (8, 8, 512))
    bsg = _bsg_of(bb_re, bb_im).astype(BF16)
    csg = _csg_of(s5_c_re[0], s5_c_im[0]).astype(BF16)
    d_row = s5_d.reshape(1, S5W)
    lseg = t // 8
    u_p = _perm(proj[:, P_U:P_U + S5W], t)
    bu3, ends_f = s5_in(u_p, bsg, a_re8, a_im8, "s5_in")
    sinit = s5_scan_init(ends_f, a_re8, a_im8, lseg, False, "s5_init_f")
    s3, yb_p = s5_scan_fwd(bu3, sinit, a_re8, a_im8, csg, u_p, d_row, "s5_scan_f")
    s2 = s3.reshape(t, S5NS)
    yb = _unperm(yb_p, t)
    gy = gelu_fwd(yb, "gelu")
    glu = mm(gy, wbg, out_dtype=BF16, name="w_b_glu")
    merged = merge_fwd(proj, pa, glu, "merge")
    o = mm(merged, wo, out_dtype=BF16, name="w_out")
    x2, h3 = mod_fwd(x1, norm_ffn2, mods, 6, 7, f=o, gk=5, gscale=1.0, name="mod3")
    ab3, act3, (g_f2o,) = ffn_in_act(h3, wf2i, "ffn2_in", comm=[("ag", ffn_out_shard(w_ffn2_out))])
    wf2o = _rows(g_f2o)
    f3 = mm(act3, wf2o, out_dtype=BF16, tk=5632, name="ffn2_out")

    dx3, df3, st_fin = final_fwd_bwd(x2, f3, mods, norm_final.reshape(1, D), tgt, "final")
    dh3, x_f2i, x_f2o = _ffn_bwd(df3, h3, ab3, act3, ffn_in_shard(w_ffn2_in).T, wf2o.T, "ffn2")
    dx2, do, st3 = mod_bwd(x2, dh3, dx3, norm_ffn2, mods, 7, fprev=o, gk=5, gscale=1.0, name="mod3_bwd")

    dmerged = mm(do, wo.T, out_dtype=BF16, name="w_out_dx")
    dwo = mm(merged, do, ta=True, out_dtype=BF16, i_outer=True, name="w_out_dw", **_TA)
    dpa, dglu, dproj = merge_bwd(proj, pa, glu, dmerged, lax.empty((t, INP), BF16), "merge_bwd")
    dwbg = mm(gy, dglu, ta=True, out_dtype=BF16, i_outer=True, name="w_b_glu_dw", **_TA)
    dgy, (x_wo,) = mm(dglu, wbg.T, out_dtype=BF16, name="w_b_glu_dx", comm=[("xc", _row_parts(dwo))])
    dyb_p = _perm(gelu_bwd(yb, dgy, "gelu_bwd"), t)
    e3, dcsg, ends_b = s5_out_bwd(dyb_p, csg, s2, a_re8, a_im8, "s5_out_bwd")
    linit = s5_scan_init(ends_b, a_re8, a_im8, lseg, True, "s5_init_b")
    du_p, dbsg, dd_row, da8 = s5_scan_bwd(e3, linit, s3, sinit, a_re8, a_im8, bsg, u_p, dyb_p, d_row, "s5_scan_b")
    du = _unperm(du_p, t).astype(BF16)
    da = jnp.sum(da8, axis=1)
    dbb_re, dbb_im = _bsg_diag(dbsg)
    g_lre, g_lim, g_ldt, g_bre, g_bim = s5_vjp((da[:, 0:512].reshape(64, 64), da[:, 512:1024].reshape(64, 64),
                                                dbb_re, dbb_im))
    g_cre, g_cim = _csg_diag(dcsg)

    dwap = mm(ya, dpa, ta=True, out_dtype=BF16, i_outer=True, name="w_a_proj_dw", **_TA)
    dya, (x_wbg,) = mm(dpa, wap.T, out_dtype=BF16, name="w_a_proj_dx", comm=[("xc", _col_parts(dwbg))])
    dy_ssd, dproj, st_sn = ssd_out_bwd(y_ssd, proj, dya, ssd_norm_w, dproj, "ssd_out_bwd")
    dxc, dproj, st_ssd = ssd_bwd(xc, dtraw, hsave, dy_ssd, dtb_row, alog_row, dx_row, expm, tri, dproj, "ssd_bwd")
    dpre, st_cv = conv_bwd_pre(proj, dxc, convw, cb_row, "conv_bwd_pre")
    dproj = conv_bwd_in(dpre, convw, dproj, "conv_bwd_in")
    dproj = lax.dynamic_update_slice(dproj, du, (0, P_U))
    dwinp, (x_wap, x_cw) = mm(h2, dproj, ta=True, out_dtype=BF16, i_outer=True, name="w_in_dw",
                              comm=[("xc", _row_parts(dwap)), ("xc", _col_parts(st_cv[0:CONV_K]))], **_TA)
    dh2, (x_win,) = mm(dproj, winp.T, out_dtype=BF16, tk=5376, name="w_in_dx",
                       comm=[("xc", _col_parts(_unpad_w_in(dwinp)))])
    dx1, df1, st2 = mod_bwd(x1, dh2, dx2, norm_mix, mods, 4, fprev=f1, gk=2, gscale=0.5, name="mod2_bwd")
    dh1, x_f1i, x_f1o = _ffn_bwd(df1, h1, ab1, act1, ffn_in_shard(w_ffn1_in).T, wf1o.T, "ffn1")
    gx, st1 = mod_bwd(xt, dh1, dx1, norm_ffn1, mods, 1, name="mod1_bwd")

    dmods = jnp.concatenate([st1[0], st1[1], st2[3], st2[0], st2[1], st3[3], st3[0], st3[1], st_fin[1]])
    part = {"b_ada": dmods, "norm_ffn1": st1[2], "norm_mix": st2[2], "conv_b": st_cv[4], "dt_bias": st_ssd[1, 0:NH],
            "a_log": st_ssd[0, 0:NH], "d_ssd": st_ssd[2, 0:NH], "ssd_norm_w": st_sn[0], "s5_lambda_re": g_lre,
            "s5_lambda_im": g_lim, "s5_b_re": g_bre, "s5_b_im": g_bim, "s5_c_re": g_cre, "s5_c_im": g_cim,
            "s5_d": dd_row, "s5_log_dt": g_ldt, "norm_ffn2": st3[2], "norm_final": st_fin[0],
            "loss": (0.5 / D) * jnp.sum(st_fin[2])}
    zero = {"loss": jnp.zeros((1,), F32)}
    gath = all_gather(_pack(part), "ag_small")
    sg, sd, sm, sv = adamw(_pack({**small, **zero}), gath, _pack({**{k: args["m_" + k] for k in small}, **zero}),
                           _pack({**{k: args["v_" + k] for k in small}, **zero}), "adamw_small")
    sg, sd, sm, sv = _unpack(sg), _unpack(sd), _unpack(sm), _unpack(sv)
    loss = sg["loss"][0]

    dm_loc = lax.dynamic_slice(gath.reshape(N_DEV, -1)[:, 0:N_ADA * D], (0, me * (N_ADA * D // N_DEV)),
                               (N_DEV, N_ADA * D // N_DEV))
    g_ada = ada_bwd(c8.T, dm_loc, "ada_bwd")
    big = {"w_ada": g_ada[None], "w_ffn1_in": x_f1i, "w_ffn1_out": x_f1o, "w_in": x_win, "conv_w": x_cw,
           "w_a_proj": x_wap, "w_b_glu": x_wbg, "w_out": x_wo, "w_ffn2_in": x_f2i, "w_ffn2_out": x_f2o}
    res = {}
    for k, parts in big.items():
        res[k] = adamw(args[k][0], parts, args["m_" + k][0], args["v_" + k][0], "adamw_" + k)

    names = ["w_ada", "b_ada", "norm_ffn1", "w_ffn1_in", "w_ffn1_out", "norm_mix", "w_in", "conv_w", "conv_b", "dt_bias",
             "a_log", "d_ssd", "ssd_norm_w", "w_a_proj", "s5_lambda_re", "s5_lambda_im", "s5_b_re", "s5_b_im", "s5_c_re",
             "s5_c_im", "s5_d", "s5_log_dt", "w_b_glu", "w_out", "norm_ffn2", "w_ffn2_in", "w_ffn2_out", "norm_final"]
    outs = [loss, gx[None]]
    for q, src in enumerate((sg, sd, sm, sv)):
        for k in names:
            if k in res:
                outs.append(res[k][q][None])
            else:
                outs.append(src[k].reshape(args[k].shape))
    return tuple(outs)
```

```python
import functools
import math

import jax
import jax.numpy as jnp
from jax import lax
from jax.experimental import pallas as pl
from jax.experimental.pallas import tpu as pltpu

F32 = jnp.float32
BF16 = jnp.bfloat16
HI = lax.Precision.HIGHEST

N_DEV = 8
D = 2048
D_FF = 5504
D_FFP = 5632
NH = 32
HP = 64
NG = 4
NST = 128
LCH = 128
CONV_DIM = 3072
CONV_K = 4
S5W = 1024
S5NS = 8192
N_ADA = 9
EPS = 1e-6
IN_COLS = 10272
INP = 10752
P_GATES, P_Z, P_XBC, P_U, P_DT = 0, 4096, 6144, 9216, 10240
O_XBC, O_DT, O_U, O_GA, O_GB = 2048, 5120, 5152, 6176, 8224
NEG = -1e30
VMEM_LIMIT = 56 * 1024 * 1024

ADAM_LR, ADAM_B1, ADAM_B2, ADAM_EPS, ADAM_WD, ADAM_STEP = 0.001, 0.9, 0.999, 1e-08, 0.01, 10


def _cp(*sem):
    return pltpu.CompilerParams(dimension_semantics=sem, vmem_limit_bytes=VMEM_LIMIT)


def _tile(dim, pref):
    if dim <= pref or dim % pref == 0:
        return min(dim, pref)
    for t in (2048, 1024, 512, 256, 128):
        if t <= pref and dim % t == 0:
            return t
    return dim


def _vec(w, cb=0):
    return pl.BlockSpec((1, w), lambda *_: (0, cb))


def _row(tm, w, cb=0):
    return pl.BlockSpec((tm, w), lambda i: (i, cb))


def _stats(w):
    return pl.BlockSpec((8, w), lambda *_: (0, 0))


_HBM = pl.BlockSpec(memory_space=pl.ANY)


def _sigmoid(x):
    return 1.0 / (1.0 + jnp.exp(-x))


def _softplus(x):
    return jnp.maximum(x, 0.0) + jnp.log1p(jnp.exp(-jnp.abs(x)))


def _peer(k):
    x, y, c = lax.axis_index("x"), lax.axis_index("y"), lax.axis_index("c")
    return (x ^ ((k >> 2) & 1), y ^ ((k >> 1) & 1), c ^ (k & 1))


def _my_id():
    return 4 * lax.axis_index("x") + 2 * lax.axis_index("y") + lax.axis_index("c")


def _comm_out_shape(kind, v):
    shape = {"ag": (N_DEV,) + v.shape, "xc": v.shape, "agc": (v.shape[0], N_DEV * v.shape[1]),
             "xcc": (N_DEV, v.shape[0], v.shape[1] // N_DEV)}[kind]
    return jax.ShapeDtypeStruct(shape, v.dtype)


def _comm_scratch(n):
    return [pltpu.SemaphoreType.DMA((n * N_DEV,)), pltpu.SemaphoreType.DMA((n * N_DEV,))]


class _Comm:
    def __init__(self, kinds, srcs, dsts, send_sems, recv_sems):
        self.items = list(zip(kinds, srcs, dsts))
        self.send_sems, self.recv_sems = send_sems, recv_sems
        x, y, c = lax.axis_index("x"), lax.axis_index("y"), lax.axis_index("c")
        self.me = 4 * x + 2 * y + c
        self.sib = (x, y, 1 - c)
        self.chips = [(1 - x, y), (x, 1 - y), (1 - x, 1 - y)]
        self.c = c

    @staticmethod
    def _id(p):
        return 4 * p[0] + 2 * p[1] + p[2]

    def _src(self, q, d):
        kind, src, _ = self.items[q]
        if kind == "xc":
            return src.at[d]
        if kind == "xcc":
            w = src.shape[1] // N_DEV
            return src.at[:, pl.ds(pl.multiple_of(d * w, 128), w)]
        return src

    def _slot(self, q, d):
        kind, _, dst = self.items[q]
        if kind == "agc":
            w = dst.shape[1] // N_DEV
            return dst.at[:, pl.ds(pl.multiple_of(d * w, 128), w)]
        return dst.at[d]

    def _push(self, q, k, src, slot, to):
        return pltpu.make_async_remote_copy(
            src_ref=src, dst_ref=self._slot(q, slot), send_sem=self.send_sems.at[q * N_DEV + k],
            recv_sem=self.recv_sems.at[q * N_DEV + k], device_id=to, device_id_type=pl.DeviceIdType.MESH)

    def _local(self, q):
        return pltpu.make_async_copy(self._src(q, self.me), self._slot(q, self.me), self.send_sems.at[q * N_DEV])

    def _direct(self, q):
        kind = self.items[q][0]
        if kind in ("xc", "xcc"):
            out = []
            for k in range(1, N_DEV):
                p = _peer(k)
                out.append((k, self._push(q, k, self._src(q, self._id(p)), self.me, p)))
            return out
        src = self.items[q][1]
        out = [(1, self._push(q, 1, src, self.me, self.sib))]
        for j, chip in enumerate(self.chips):
            out.append((2 + j, self._push(q, 2 + j, src, self.me, (*chip, self.c))))
        return out

    def _forwards(self, q):
        out = []
        for j, chip in enumerate(self.chips):
            slot = self._id((*chip, self.c))
            out.append((2 + j, 5 + j, self._push(q, 5 + j, self._slot(q, slot), slot, self.sib)))
        return out

    def start(self):
        for q in range(len(self.items)):
            self._local(q).start()
            for _, cp in self._direct(q):
                cp.start()

    def forward(self):
        for q, (kind, _, _) in enumerate(self.items):
            if kind not in ("ag", "agc"):
                continue
            for k_in, _, fwd in self._forwards(q):
                self._push(q, k_in, self._slot(q, self.me), self.me, self.sib).wait_recv()
                fwd.start()

    def finish(self):
        for q, (kind, _, _) in enumerate(self.items):
            self._local(q).wait()
            if kind in ("xc", "xcc"):
                for _, cp in self._direct(q):
                    cp.wait()
                continue
            for k, cp in self._direct(q):
                cp.wait_send()
                if k == 1:
                    cp.wait_recv()
            for _, _, fwd in self._forwards(q):
                fwd.wait()


def comm_call(items, name):
    kinds = [k for k, _ in items]
    n = len(items)

    def body(*refs):
        cm = _Comm(kinds, refs[:n], refs[n:2 * n], refs[2 * n], refs[2 * n + 1])
        cm.start()
        cm.forward()
        cm.finish()

    return pl.pallas_call(
        body, name=name,
        in_specs=[pl.BlockSpec(memory_space=pl.ANY)] * n, out_specs=[pl.BlockSpec(memory_space=pl.ANY)] * n,
        out_shape=[_comm_out_shape(k, v) for k, v in items], scratch_shapes=_comm_scratch(n),
    )(*[v for _, v in items])


def all_gather(v, name):
    return comm_call([("ag", v)], name)[0]


def _pcall(body, args, *, name, grid, in_specs, out_specs, out_shape, scratch_shapes=(), sem, comm=(), fwd=0.85):
    nc, n_in, n_out = len(comm), len(in_specs), len(out_shape)
    if not nc:
        return pl.pallas_call(body, name=name, grid=grid, in_specs=list(in_specs), out_specs=list(out_specs),
                              out_shape=list(out_shape), scratch_shapes=list(scratch_shapes),
                              compiler_params=_cp(*sem))(*args)
    kinds = [k for k, _ in comm]
    steps = math.prod(grid)
    fwd_step = min(int(fwd * steps), steps - 1)

    def carried(*refs):
        ins, csrc = refs[:n_in], refs[n_in:n_in + nc]
        outs, cdst = refs[n_in + nc:n_in + nc + n_out], refs[n_in + nc + n_out:n_in + 2 * nc + n_out]
        scr = refs[n_in + 2 * nc + n_out:]
        cm = _Comm(kinds, csrc, cdst, scr[-2], scr[-1])
        step = 0
        for d, g in enumerate(grid):
            step = step * g + pl.program_id(d)

        @pl.when(step == 0)
        def _():
            cm.start()

        body(*ins, *outs, *scr[:-2])

        @pl.when(step == fwd_step)
        def _():
            cm.forward()

        @pl.when(step == steps - 1)
        def _():
            cm.finish()

    hbm = pl.BlockSpec(memory_space=pl.ANY)
    out = pl.pallas_call(
        carried, name=name, grid=grid, in_specs=list(in_specs) + [hbm] * nc, out_specs=list(out_specs) + [hbm] * nc,
        out_shape=list(out_shape) + [_comm_out_shape(k, v) for k, v in comm],
        scratch_shapes=list(scratch_shapes) + _comm_scratch(nc), compiler_params=_cp(*(("arbitrary",) * len(grid))),
    )(*args, *[v for _, v in comm])
    return list(out[:n_out]), list(out[n_out:])


def mm(a, b, *, ta=False, out_dtype=F32, tm=512, tn=1024, tk=2048, i_outer=False, a_halves=False, b_halves=False,
       name, comm=()):
    if a_halves:
        m, kd = a.shape[1], 2 * a.shape[2]
    elif ta:
        kd, m = a.shape
    else:
        m, kd = a.shape
    kd2, n = (b.shape[1], 2 * b.shape[2]) if b_halves else b.shape
    assert kd == kd2 and not (ta and a_halves), (a.shape, b.shape, ta)
    tm, tn, tk = _tile(m, tm), _tile(n // 2 if b_halves else n, tn), _tile(kd // 2 if a_halves else kd, tk)
    nk = kd // tk
    nkh, njh = nk // 2, n // tn // 2
    grid = (m // tm, n // tn, nk) if i_outer else (n // tn, m // tm, nk)
    dims = (((0,) if ta else (1,), (0,)), ((), ()))

    def ix(f):
        return (lambda i, j, k: f(i, j, k)) if i_outer else (lambda j, i, k: f(i, j, k))

    def body(a_ref, b_ref, o_ref, *scr):
        p = lax.dot_general(a_ref[...], b_ref[...], dims, preferred_element_type=F32)
        if nk == 1:
            o_ref[...] = p.astype(o_ref.dtype)
        else:
            acc = scr[0]
            k = pl.program_id(2)

            @pl.when(k == 0)
            def _():
                acc[...] = p

            @pl.when(k > 0)
            def _():
                acc[...] += p

            @pl.when(k == nk - 1)
            def _():
                o_ref[...] = acc[...].astype(o_ref.dtype)

    if a_halves:
        a_spec = pl.BlockSpec((None, tm, tk), ix(lambda i, j, k: (k // nkh, i, k % nkh)))
    elif ta:
        a_spec = pl.BlockSpec((tk, tm), ix(lambda i, j, k: (k, i)))
    else:
        a_spec = pl.BlockSpec((tm, tk), ix(lambda i, j, k: (i, k)))
    if b_halves:
        b_spec = pl.BlockSpec((None, tk, tn), ix(lambda i, j, k: (j // njh, k, j % njh)))
    else:
        b_spec = pl.BlockSpec((tk, tn), ix(lambda i, j, k: (k, j)))
    out = _pcall(body, (a, b), name=name, grid=grid, in_specs=[a_spec, b_spec],
                 out_specs=[pl.BlockSpec((tm, tn), ix(lambda i, j, k: (i, j)))],
                 out_shape=[jax.ShapeDtypeStruct((m, n), out_dtype)],
                 scratch_shapes=[pltpu.VMEM((tm, tn), F32)] if nk > 1 else [],
                 sem=("parallel", "parallel", "arbitrary"), comm=comm)
    return (out[0][0], out[1]) if comm else out[0]


def ffn_in_act(h, w, name, comm=(), fwd=0.85):
    t = h.shape[0]
    tm, tn = _tile(t, 512), 512
    nj = D_FFP // tn

    def body(h_ref, wa_ref, wb_ref, ab_ref, act_ref):
        hv = h_ref[...]
        pa = _dot(hv, wa_ref[...])
        pb = _dot(hv, wb_ref[...])
        ab_ref[0] = pa.astype(BF16)
        ab_ref[1] = pb.astype(BF16)
        act_ref[...] = (pa * _sigmoid(pa) * pb).astype(BF16)

    out = _pcall(body, (h, w, w), name=name, grid=(nj, t // tm),
                 in_specs=[pl.BlockSpec((tm, D), lambda j, i: (i, 0)), pl.BlockSpec((D, tn), lambda j, i: (0, j)),
                           pl.BlockSpec((D, tn), lambda j, i: (0, nj + j))],
                 out_specs=[pl.BlockSpec((2, tm, tn), lambda j, i: (0, i, j)), pl.BlockSpec((tm, tn), lambda j, i: (i, j))],
                 out_shape=[jax.ShapeDtypeStruct((2, t, D_FFP), BF16), jax.ShapeDtypeStruct((t, D_FFP), BF16)],
                 sem=("parallel", "parallel"), comm=comm, fwd=fwd)
    return (out[0][0], out[0][1], out[1]) if comm else (out[0], out[1])


def ffn_dab(df, w_out_t, ab, name, comm=()):
    t = df.shape[0]
    tm, tn = _tile(t, 1024), 512

    def body(d_ref, w_ref, ab_ref, o_ref):
        dv = _dot(d_ref[...], w_ref[...])
        a = ab_ref[0].astype(F32)
        b = ab_ref[1].astype(F32)
        s = _sigmoid(a)
        o_ref[0] = (dv * b * (s * (1.0 + a * (1.0 - s)))).astype(BF16)
        o_ref[1] = (dv * (a * s)).astype(BF16)

    out = _pcall(body, (df, w_out_t, ab), name=name, grid=(D_FFP // tn, t // tm),
                 in_specs=[pl.BlockSpec((tm, D), lambda j, i: (i, 0)), pl.BlockSpec((D, tn), lambda j, i: (0, j)),
                           pl.BlockSpec((2, tm, tn), lambda j, i: (0, i, j))],
                 out_specs=[pl.BlockSpec((2, tm, tn), lambda j, i: (0, i, j))],
                 out_shape=[jax.ShapeDtypeStruct((2, t, D_FFP), BF16)], sem=("parallel", "parallel"), comm=comm, fwd=0.85)
    return (out[0][0], out[1]) if comm else out[0]


def ada_fwd(c8, w_loc, b_loc, name):
    n = w_loc.shape[1]
    tn = 256

    def body(c_ref, w_ref, b_ref, o_ref):
        cv = c_ref[...]
        ca = cv * _sigmoid(cv)
        o_ref[...] = jnp.dot(ca, w_ref[...], precision=HI, preferred_element_type=F32) + b_ref[...]

    return pl.pallas_call(
        body, name=name, grid=(n // tn,),
        in_specs=[pl.BlockSpec((N_DEV, D), lambda j: (0, 0)), pl.BlockSpec((D, tn), lambda j: (0, j)),
                  pl.BlockSpec((1, tn), lambda j: (0, j))],
        out_specs=pl.BlockSpec((N_DEV, tn), lambda j: (0, j)),
        out_shape=jax.ShapeDtypeStruct((N_DEV, n), F32), compiler_params=_cp("parallel"),
    )(c8, w_loc, b_loc)


def ada_bwd(c8t, dm_loc, name):
    n = dm_loc.shape[1]
    tn = 256

    def body(c_ref, d_ref, o_ref):
        cv = c_ref[...]
        ca = cv * _sigmoid(cv)
        o_ref[...] = jnp.dot(ca, d_ref[...], precision=HI, preferred_element_type=F32)

    return pl.pallas_call(
        body, name=name, grid=(n // tn,),
        in_specs=[pl.BlockSpec((D, N_DEV), lambda j: (0, 0)), pl.BlockSpec((N_DEV, tn), lambda j: (0, j))],
        out_specs=pl.BlockSpec((D, tn), lambda j: (0, j)),
        out_shape=jax.ShapeDtypeStruct((D, n), F32), compiler_params=_cp("parallel"),
    )(c8t, dm_loc)


def mod_fwd(x, nw, mods, shk, sck, *, f=None, gk=None, gscale=1.0, name, comm=()):
    t = x.shape[0]
    tm = min(256, t)
    res = f is not None

    def body(*refs):
        if res:
            x_ref, f_ref, g_ref, nw_ref, sh_ref, sc_ref, x1_ref, h_ref = refs
            xv = x_ref[...] + (gscale * g_ref[...]) * f_ref[...].astype(F32)
            x1_ref[...] = xv
        else:
            x_ref, nw_ref, sh_ref, sc_ref, h_ref = refs
            xv = x_ref[...]
        r = lax.rsqrt(jnp.mean(xv * xv, axis=-1, keepdims=True) + EPS)
        h_ref[...] = ((xv * r * nw_ref[...]) * (1.0 + sc_ref[...]) + sh_ref[...]).astype(BF16)

    ins = [x] + ([f, mods] if res else []) + [nw, mods, mods]
    specs = [_row(tm, D)] + ([_row(tm, D), _vec(D, gk)] if res else []) + [_vec(D), _vec(D, shk), _vec(D, sck)]
    outs = ([jax.ShapeDtypeStruct((t, D), F32)] if res else []) + [jax.ShapeDtypeStruct((t, D), BF16)]
    ospecs = ([_row(tm, D)] if res else []) + [_row(tm, D)]
    out = _pcall(body, ins, name=name, grid=(t // tm,), in_specs=specs, out_specs=ospecs, out_shape=outs,
                 sem=("parallel",), comm=comm)
    if comm:
        return (out[0] if res else out[0][0]), out[1]
    return out if res else out[0]


def final_fwd_bwd(x2, f3, mods, nf, tgt, name):
    t = x2.shape[0]
    tm = min(256, t)

    def body(x_ref, f_ref, g_ref, nf_ref, t_ref, dx_ref, df_ref, st_ref):
        @pl.when(pl.program_id(0) == 0)
        def _():
            st_ref[...] = jnp.zeros_like(st_ref)

        g = 0.5 * g_ref[...]
        fv = f_ref[...].astype(F32)
        xv = x_ref[...] + g * fv
        r = lax.rsqrt(jnp.mean(xv * xv, axis=-1, keepdims=True) + EPS)
        xh = xv * r
        nfv = nf_ref[...]
        e = xh * nfv - t_ref[...]
        st_ref[2:3, :] += jnp.sum(e * e, axis=0, keepdims=True)
        dy = e * (1.0 / D)
        st_ref[0:1, :] += jnp.sum(dy * xh, axis=0, keepdims=True)
        dxh = dy * nfv
        dx = r * (dxh - xh * jnp.mean(dxh * xh, axis=-1, keepdims=True))
        dx_ref[...] = dx
        df_ref[...] = (g * dx).astype(BF16)
        st_ref[1:2, :] += 0.5 * jnp.sum(fv * dx, axis=0, keepdims=True)

    return pl.pallas_call(
        body, name=name, grid=(t // tm,),
        in_specs=[_row(tm, D), _row(tm, D), _vec(D, 8), _vec(D), _row(tm, D)],
        out_specs=[_row(tm, D), _row(tm, D), _stats(D)],
        out_shape=[jax.ShapeDtypeStruct((t, D), F32), jax.ShapeDtypeStruct((t, D), BF16),
                   jax.ShapeDtypeStruct((8, D), F32)],
        compiler_params=_cp("arbitrary"),
    )(x2, f3, mods, nf, tgt)


def mod_bwd(x_in, dh, dx_out, nw, mods, sck, *, fprev=None, gk=None, gscale=1.0, name):
    t = x_in.shape[0]
    tm = min(256, t)
    gate = fprev is not None

    def body(*refs):
        if gate:
            x_ref, dh_ref, dxo_ref, nw_ref, sc_ref, f_ref, g_ref, dx_ref, df_ref, st_ref = refs
        else:
            x_ref, dh_ref, dxo_ref, nw_ref, sc_ref, dx_ref, st_ref = refs

        @pl.when(pl.program_id(0) == 0)
        def _():
            st_ref[...] = jnp.zeros_like(st_ref)

        xv = x_ref[...]
        dhv = dh_ref[...].astype(F32)
        r = lax.rsqrt(jnp.mean(xv * xv, axis=-1, keepdims=True) + EPS)
        xh = xv * r
        nwv = nw_ref[...]
        st_ref[0:1, :] += jnp.sum(dhv, axis=0, keepdims=True)
        st_ref[1:2, :] += jnp.sum(dhv * (xh * nwv), axis=0, keepdims=True)
        dn = dhv * (1.0 + sc_ref[...])
        st_ref[2:3, :] += jnp.sum(dn * xh, axis=0, keepdims=True)
        dxh = dn * nwv
        dx = dxo_ref[...] + r * (dxh - xh * jnp.mean(dxh * xh, axis=-1, keepdims=True))
        dx_ref[...] = dx
        if gate:
            df_ref[...] = ((gscale * g_ref[...]) * dx).astype(BF16)
            st_ref[3:4, :] += gscale * jnp.sum(f_ref[...].astype(F32) * dx, axis=0, keepdims=True)

    ins = [x_in, dh, dx_out, nw, mods] + ([fprev, mods] if gate else [])
    specs = [_row(tm, D), _row(tm, D), _row(tm, D), _vec(D), _vec(D, sck)] + ([_row(tm, D), _vec(D, gk)] if gate else [])
    outs = [jax.ShapeDtypeStruct((t, D), F32)] + ([jax.ShapeDtypeStruct((t, D), BF16)] if gate else []) + \
        [jax.ShapeDtypeStruct((8, D), F32)]
    ospecs = [_row(tm, D)] + ([_row(tm, D)] if gate else []) + [_stats(D)]
    return pl.pallas_call(body, name=name, grid=(t // tm,), in_specs=specs, out_specs=ospecs, out_shape=outs,
                          compiler_params=_cp("arbitrary"))(*ins)


def _conv_pre(cur, prev8, w, b, tm):
    full = jnp.concatenate([prev8, cur], axis=0)
    pre = b + w[3:4, :] * cur
    for k in range(CONV_K - 1):
        s = CONV_K - 1 - k
        pre = pre + w[k:k + 1, :] * pltpu.roll(full, s, 0)[8:8 + tm, :]
    return pre


def conv_fwd(proj, cw_full, cb_full, name):
    t = proj.shape[0]
    tm = min(256, t)
    cwid = 1024
    cb0 = P_XBC // cwid

    def body(x_ref, p_ref, w_ref, b_ref, o_ref):
        i = pl.program_id(1)
        prev8 = jnp.where(i == 0, 0.0, p_ref[...].astype(F32)[8:16])
        pre = _conv_pre(x_ref[...].astype(F32), prev8, w_ref[...], b_ref[...], tm)
        o_ref[...] = pre * _sigmoid(pre)

    return pl.pallas_call(
        body, name=name, grid=(CONV_DIM // cwid, t // tm),
        in_specs=[pl.BlockSpec((tm, cwid), lambda j, i: (i, cb0 + j)),
                  pl.BlockSpec((16, cwid), lambda j, i: (jnp.maximum(i * (tm // 16) - 1, 0), cb0 + j)),
                  pl.BlockSpec((CONV_K, cwid), lambda j, i: (0, j)), pl.BlockSpec((1, cwid), lambda j, i: (0, j))],
        out_specs=pl.BlockSpec((tm, cwid), lambda j, i: (i, j)),
        out_shape=jax.ShapeDtypeStruct((t, CONV_DIM), F32), compiler_params=_cp("parallel", "parallel"),
    )(proj, proj, cw_full, cb_full)


def conv_bwd_pre(proj, dxc, cw_full, cb_full, name):
    t = proj.shape[0]
    tm = min(256, t)
    cwid = 1024
    cb0 = P_XBC // cwid

    def body(x_ref, p_ref, d_ref, w_ref, b_ref, o_ref, st_ref):
        i = pl.program_id(1)

        @pl.when(i == 0)
        def _():
            st_ref[...] = jnp.zeros_like(st_ref)

        cur = x_ref[...].astype(F32)
        prev8 = jnp.where(i == 0, 0.0, p_ref[...].astype(F32)[8:16])
        pre = _conv_pre(cur, prev8, w_ref[...], b_ref[...], tm)
        s = _sigmoid(pre)
        dpre = d_ref[...] * (s * (1.0 + pre * (1.0 - s)))
        o_ref[...] = dpre
        st_ref[4:5, :] += jnp.sum(dpre, axis=0, keepdims=True)
        st_ref[3:4, :] += jnp.sum(dpre * cur, axis=0, keepdims=True)
        full = jnp.concatenate([prev8, cur], axis=0)
        for k in range(CONV_K - 1):
            sft = CONV_K - 1 - k
            st_ref[k:k + 1, :] += jnp.sum(dpre * pltpu.roll(full, sft, 0)[8:8 + tm, :], axis=0, keepdims=True)

    return pl.pallas_call(
        body, name=name, grid=(CONV_DIM // cwid, t // tm),
        in_specs=[pl.BlockSpec((tm, cwid), lambda j, i: (i, cb0 + j)),
                  pl.BlockSpec((16, cwid), lambda j, i: (jnp.maximum(i * (tm // 16) - 1, 0), cb0 + j)),
                  pl.BlockSpec((tm, cwid), lambda j, i: (i, j)),
                  pl.BlockSpec((CONV_K, cwid), lambda j, i: (0, j)), pl.BlockSpec((1, cwid), lambda j, i: (0, j))],
        out_specs=[pl.BlockSpec((tm, cwid), lambda j, i: (i, j)), pl.BlockSpec((8, cwid), lambda j, i: (0, j))],
        out_shape=[jax.ShapeDtypeStruct((t, CONV_DIM), F32), jax.ShapeDtypeStruct((8, CONV_DIM), F32)],
        compiler_params=_cp("parallel", "arbitrary"),
    )(proj, proj, dxc, cw_full, cb_full)


def conv_bwd_in(dpre, cw_full, dproj, name):
    t = dpre.shape[0]
    tm = min(256, t)
    cwid = 1024
    nt = t // tm

    def body(d_ref, n_ref, w_ref, _, o_ref):
        i = pl.program_id(1)
        cur = d_ref[...]
        nxt = jnp.where(i == nt - 1, 0.0, n_ref[...])
        full = jnp.concatenate([cur, nxt], axis=0)
        w = w_ref[...]
        acc = w[3:4, :] * cur
        for k in range(CONV_K - 1):
            s = CONV_K - 1 - k
            acc = acc + w[k:k + 1, :] * pltpu.roll(full, tm + 8 - s, 0)[0:tm, :]
        o_ref[...] = acc.astype(BF16)

    return pl.pallas_call(
        body, name=name, grid=(CONV_DIM // cwid, nt),
        in_specs=[pl.BlockSpec((tm, cwid), lambda j, i: (i, j)),
                  pl.BlockSpec((8, cwid), lambda j, i: (jnp.minimum((i + 1) * (tm // 8), t // 8 - 1), j)),
                  pl.BlockSpec((CONV_K, cwid), lambda j, i: (0, j)), _HBM],
        out_specs=pl.BlockSpec((tm, cwid), lambda j, i: (i, P_XBC // cwid + j)),
        out_shape=jax.ShapeDtypeStruct(dproj.shape, BF16), input_output_aliases={3: 0},
        compiler_params=_cp("parallel", "parallel"),
    )(dpre, dpre, cw_full, dproj)


def _nt(a, b):
    return lax.dot_general(a, b, (((1,), (1,)), ((), ())), preferred_element_type=F32)


def _dot(a, b):
    return jnp.dot(a, b, preferred_element_type=F32)


def _head_lanes():
    return lax.broadcasted_iota(jnp.int32, (1, 128), 1) < NH


def _expand_heads(x, e3):
    x = jnp.where(_head_lanes(), x, 0.0)
    hi = x.astype(BF16).astype(F32)
    r1 = x - hi
    mid = r1.astype(BF16).astype(F32)
    packed = hi + pltpu.roll(mid, NH, 1) + pltpu.roll(r1 - mid, 2 * NH, 1)
    return _dot(packed.astype(BF16), e3)


def _reduce_heads(v, e3):
    hi = v.astype(BF16)
    lo = (v - hi.astype(F32)).astype(BF16)
    return jnp.where(_head_lanes(), _nt(hi, e3) + _nt(lo, e3), 0.0)


def _ssd_common(dt_ref, dtb_ref, al_ref, exp_ref, tri_ref):
    a_row = jnp.where(_head_lanes(), -jnp.exp(al_ref[...]), 0.0)
    zraw = dt_ref[...] + dtb_ref[...]
    dtv = _softplus(zraw)
    cs = jnp.dot(tri_ref[...], dtv * a_row, precision=HI, preferred_element_type=F32)
    e3 = exp_ref[...]
    return a_row, zraw, dtv, cs, _expand_heads(cs, e3), _expand_heads(dtv, e3)


def ssd_fwd(xc, proj, dtb_row, alog_row, dx_row, expm, tri, name):
    t = xc.shape[0]
    nc = t // LCH

    def body(xs_ref, bm_ref, cm_ref, dt_ref, dtb_ref, al_ref, dxr_ref, exp_ref, tri_ref, y_ref, hs_ref, h_scr):
        @pl.when(pl.program_id(0) == 0)
        def _():
            h_scr[...] = jnp.zeros_like(h_scr)

        _, _, _, cs, csx, dtx = _ssd_common(dt_ref, dtb_ref, al_ref, exp_ref, tri_ref)
        cst = cs.T
        csl = csx[LCH - 1:LCH, :]
        xs = xs_ref[...]
        xd = xs * dtx
        xdw = xd * jnp.exp(csl - csx)
        ecs = jnp.exp(csx)
        ecl = jnp.exp(csl)
        tril = lax.broadcasted_iota(jnp.int32, (LCH, LCH), 0) >= lax.broadcasted_iota(jnp.int32, (LCH, LCH), 1)
        hs_ref[...] = h_scr[...]
        for g in range(NG):
            gc = slice(g * 512, (g + 1) * 512)
            bm = bm_ref[:, g * NST:(g + 1) * NST]
            cmb = cm_ref[:, g * NST:(g + 1) * NST].astype(BF16)
            gm = _nt(cmb, bm.astype(BF16))
            hg = h_scr[:, gc]
            yo = _dot(cmb, hg.astype(BF16)) * ecs[:, gc]
            st = _dot(bm.T.astype(BF16), xdw[:, gc].astype(BF16))
            for r in range(8):
                h = g * 8 + r
                hc = slice(h * HP, (h + 1) * HP)
                seg = cs[:, h:h + 1] - cst[h:h + 1, :]
                m = (gm * jnp.exp(jnp.where(tril, seg, NEG))).astype(BF16)
                yd = _dot(m, xd[:, hc].astype(BF16))
                y_ref[:, hc] = yd + yo[:, r * HP:(r + 1) * HP] + dxr_ref[:, hc] * xs[:, hc]
            h_scr[:, gc] = ecl[:, gc] * hg + st

    return pl.pallas_call(
        body, name=name, grid=(nc,),
        in_specs=[pl.BlockSpec((LCH, 2048), lambda c: (c, 0)), pl.BlockSpec((LCH, 512), lambda c: (c, 4)),
                  pl.BlockSpec((LCH, 512), lambda c: (c, 5)), pl.BlockSpec((LCH, 128), lambda c: (c, 0)),
                  _vec(128), _vec(128), _vec(2048), pl.BlockSpec((128, 2048), lambda c: (0, 0)),
                  pl.BlockSpec((LCH, LCH), lambda c: (0, 0))],
        out_specs=[pl.BlockSpec((LCH, 2048), lambda c: (c, 0)), pl.BlockSpec((None, NST, 2048), lambda c: (c, 0, 0))],
        out_shape=[jax.ShapeDtypeStruct((t, 2048), F32), jax.ShapeDtypeStruct((nc, NST, 2048), F32)],
        scratch_shapes=[pltpu.VMEM((NST, 2048), F32)],
        compiler_params=_cp("arbitrary"),
    )(xc, xc, xc, proj, dtb_row, alog_row, dx_row, expm, tri)


def ssd_bwd(xc, proj, hsave, dy, dtb_row, alog_row, dx_row, expm, tri, dproj, name):
    t = xc.shape[0]
    nc = t // LCH

    def body(xs_ref, bm_ref, cm_ref, dt_ref, hs_ref, dy_ref, dtb_ref, al_ref, dxr_ref, exp_ref, tri_ref, _,
             dxc_ref, ddt_ref, st_ref, dh_scr, dxd_scr, dcsx_scr):
        @pl.when(pl.program_id(0) == 0)
        def _():
            dh_scr[...] = jnp.zeros_like(dh_scr)
            st_ref[...] = jnp.zeros_like(st_ref)

        a_row, zraw, dtv, cs, csx, dtx = _ssd_common(dt_ref, dtb_ref, al_ref, exp_ref, tri_ref)
        e = exp_ref[...]
        cst = cs.T
        csl = csx[LCH - 1:LCH, :]
        xs = xs_ref[...]
        xd = xs * dtx
        wend = jnp.exp(csl - csx)
        xdw = xd * wend
        ecs = jnp.exp(csx)
        ecl = jnp.exp(csl)
        ri = lax.broadcasted_iota(jnp.int32, (LCH, LCH), 0)
        ci = lax.broadcasted_iota(jnp.int32, (LCH, LCH), 1)
        tril = ri >= ci
        triu = ri <= ci
        lane = lax.broadcasted_iota(jnp.int32, (1, 128), 1)
        dyv = dy_ref[...]
        dxr = dxr_ref[...]
        st_ref[2:3, :] += _reduce_heads(jnp.sum(dyv * xs, axis=0, keepdims=True), e)
        dcs = jnp.zeros((LCH, 128), F32)
        for g in range(NG):
            gc = slice(g * 512, (g + 1) * 512)
            bmb = bm_ref[:, g * NST:(g + 1) * NST].astype(BF16)
            cm = cm_ref[:, g * NST:(g + 1) * NST]
            cmb = cm.astype(BF16)
            hg = hs_ref[:, gc]
            hgb = hg.astype(BF16)
            dhc = dh_scr[:, gc]
            dhcb = dhc.astype(BF16)
            dyg = dyv[:, gc]
            yo = _dot(cmb, hgb) * ecs[:, gc]
            dq = (dyg * ecs[:, gc]).astype(BF16)
            dcm = _nt(dq, hgb)
            dh_yo = _dot(cm.T.astype(BF16), dq)
            dxdw = _dot(bmb, dhcb)
            dbm = _nt(xdw[:, gc].astype(BF16), dhcb)
            tt = dxdw * xdw[:, gc]
            dcsx_g = dyg * yo - tt
            dcsl_g = jnp.sum(tt, axis=0, keepdims=True) + jnp.sum(dhc * hg, axis=0, keepdims=True) * ecl[:, gc]
            dxd_scr[:, gc] = dxdw * wend[:, gc]
            dh_scr[:, gc] = ecl[:, gc] * dhc + dh_yo
            gm = _nt(cmb, bmb)
            gmt = _nt(bmb, cmb)
            dg = jnp.zeros((LCH, LCH), F32)
            dgt = jnp.zeros((LCH, LCH), F32)
            for r in range(8):
                h = g * 8 + r
                hc = slice(h * HP, (h + 1) * HP)
                seg = cs[:, h:h + 1] - cst[h:h + 1, :]
                lm = jnp.exp(jnp.where(tril, seg, NEG))
                lmt = jnp.exp(jnp.where(triu, -seg, NEG))
                mm_ = gm * lm
                mmt = gmt * lmt
                xdh = xd[:, hc].astype(BF16)
                dyh = dyv[:, hc].astype(BF16)
                dm = _nt(dyh, xdh)
                dmt = _nt(xdh, dyh)
                dxd_scr[:, hc] += _dot(mmt.astype(BF16), dyh)
                rs = jnp.sum(dm * mm_, axis=1, keepdims=True) - jnp.sum(dmt * mmt, axis=1, keepdims=True)
                dcs = dcs + rs * jnp.where(lane == h, 1.0, 0.0)
                dg = dg + dm * lm
                dgt = dgt + dmt * lmt
            dcm = dcm + _dot(dg.astype(BF16), bmb)
            dbm = dbm + _dot(dgt.astype(BF16), cmb)
            dxc_ref[:, 2048 + g * NST:2048 + (g + 1) * NST] = dbm
            dxc_ref[:, 2560 + g * NST:2560 + (g + 1) * NST] = dcm
            dcsx_scr[:, gc] = dcsx_g
            dcsx_scr[LCH - 1:LCH, gc] += dcsl_g
        dxd = dxd_scr[...]
        dxc_ref[:, 0:2048] = dxr * dyv + dxd * dtx
        ddtv = _reduce_heads(dxd * xs, e)
        dcs = dcs + _reduce_heads(dcsx_scr[...], e)
        dda =lax.dot_general(tri_ref[...], dcs, (((0,), (0,)), ((), ())), precision=HI, preferred_element_type=F32)
        ddtv = ddtv + dda * a_row
        st_ref[0:1, :] += jnp.sum(dda * dtv, axis=0, keepdims=True) * a_row
        ddt = ddtv * _sigmoid(zraw)
        ddt_ref[:, 0:128] = ddt.astype(BF16)
        ddt_ref[:, 128:INP - P_DT] = jnp.zeros((LCH, INP - P_DT - 128), BF16)
        st_ref[1:2, :] += jnp.sum(ddt, axis=0, keepdims=True)

    rc = lambda c: nc - 1 - c
    return pl.pallas_call(
        body, name=name, grid=(nc,),
        in_specs=[pl.BlockSpec((LCH, 2048), lambda c: (rc(c), 0)), pl.BlockSpec((LCH, 512), lambda c: (rc(c), 4)),
                  pl.BlockSpec((LCH, 512), lambda c: (rc(c), 5)),
                  pl.BlockSpec((LCH, 128), lambda c: (rc(c), 0)),
                  pl.BlockSpec((None, NST, 2048), lambda c: (rc(c), 0, 0)),
                  pl.BlockSpec((LCH, 2048), lambda c: (rc(c), 0)),
                  _vec(128), _vec(128), _vec(2048), pl.BlockSpec((128, 2048), lambda c: (0, 0)),
                  pl.BlockSpec((LCH, LCH), lambda c: (0, 0)), _HBM],
        out_specs=[pl.BlockSpec((LCH, CONV_DIM), lambda c: (rc(c), 0)),
                   pl.BlockSpec((LCH, INP - P_DT), lambda c: (rc(c), P_DT // (INP - P_DT))), _stats(128)],
        out_shape=[jax.ShapeDtypeStruct((t, CONV_DIM), F32), jax.ShapeDtypeStruct(dproj.shape, BF16),
                   jax.ShapeDtypeStruct((8, 128), F32)],
        scratch_shapes=[pltpu.VMEM((NST, 2048), F32), pltpu.VMEM((LCH, 2048), F32), pltpu.VMEM((LCH, 2048), F32)],
        input_output_aliases={11: 1}, compiler_params=_cp("arbitrary"),
    )(xc, xc, xc, proj, hsave, dy, dtb_row, alog_row, dx_row, expm, tri, dproj)


def ssd_out_fwd(y, proj, nw, name):
    t = y.shape[0]
    tm = min(256, t)

    def body(y_ref, z_ref, nw_ref, o_ref):
        for g in range(NG):
            gc = slice(g * 512, (g + 1) * 512)
            z = z_ref[:, gc].astype(F32)
            yz = y_ref[:, gc] * (z * _sigmoid(z))
            r = lax.rsqrt(jnp.mean(yz * yz, axis=-1, keepdims=True) + EPS)
            o_ref[:, gc] = (yz * r * nw_ref[:, gc]).astype(BF16)

    return pl.pallas_call(body, name=name, grid=(t // tm,),
                          in_specs=[_row(tm, 2048), _row(tm, 2048, P_Z // 2048), _vec(2048)],
                          out_specs=_row(tm, 2048), out_shape=jax.ShapeDtypeStruct((t, 2048), BF16),
                          compiler_params=_cp("parallel"))(y, proj, nw)


def ssd_out_bwd(y, proj, dya, nw, dproj, name):
    t = y.shape[0]
    tm = min(256, t)

    def body(y_ref, z_ref, d_ref, nw_ref, _, dy_ref, dz_ref, st_ref):
        @pl.when(pl.program_id(0) == 0)
        def _():
            st_ref[...] = jnp.zeros_like(st_ref)

        for g in range(NG):
            gc = slice(g * 512, (g + 1) * 512)
            z = z_ref[:, gc].astype(F32)
            yv = y_ref[:, gc]
            s = _sigmoid(z)
            sz = z * s
            yz = yv * sz
            r = lax.rsqrt(jnp.mean(yz * yz, axis=-1, keepdims=True) + EPS)
            yzn = yz * r
            dv = d_ref[:, gc].astype(F32)
            st_ref[0:1, gc] += jnp.sum(dv * yzn, axis=0, keepdims=True)
            dyn = dv * nw_ref[:, gc]
            dyz = r * (dyn - yzn * jnp.mean(dyn * yzn, axis=-1, keepdims=True))
            dy_ref[:, gc] = dyz * sz
            dz_ref[:, gc] = (dyz * yv * (s * (1.0 + z * (1.0 - s)))).astype(BF16)

    return pl.pallas_call(
        body, name=name, grid=(t // tm,),
        in_specs=[_row(tm, 2048), _row(tm, 2048, P_Z // 2048), _row(tm, 2048), _vec(2048), _HBM],
        out_specs=[_row(tm, 2048), _row(tm, 2048, P_Z // 2048), _stats(2048)],
        out_shape=[jax.ShapeDtypeStruct((t, 2048), F32), jax.ShapeDtypeStruct(dproj.shape, BF16),
                   jax.ShapeDtypeStruct((8, 2048), F32)],
        input_output_aliases={4: 1}, compiler_params=_cp("arbitrary"))(y, proj, dya, nw, dproj)


def _cstep(ar, ai, sr, si, br, bi):
    return ar * sr - ai * si + br, ar * si + ai * sr + bi


def _halves(v):
    return (v[0:8, 0:512], v[0:8, 512:1024]), (v[8:16, 0:512], v[8:16, 512:1024])


def _slab(r1, i1, r2, i2):
    return jnp.concatenate([jnp.concatenate([r1, i1], axis=1), jnp.concatenate([r2, i2], axis=1)], axis=0).astype(BF16)


def _local_ends(x_ref, nslab, ar, ai, sr_scr, si_scr, end_ref, first, last, reverse):
    @pl.when(first)
    def _():
        sr_scr[...] = jnp.zeros_like(sr_scr)
        si_scr[...] = jnp.zeros_like(si_scr)

    def step(k, carry):
        s1, s2 = _halves(x_ref[nslab - 1 - k if reverse else k].astype(F32))
        if reverse:
            s1, s2 = s2, s1
        return _cstep(ar, ai, *_cstep(ar, ai, carry[0], carry[1], *s1), *s2)

    sr, si = lax.fori_loop(0, nslab, step, (sr_scr[...], si_scr[...]), unroll=4)
    sr_scr[...] = sr
    si_scr[...] = si

    @pl.when(last)
    def _():
        end_ref[:, 0:512] = sr
        end_ref[:, 512:1024] = si


def s5_in(u, bsg, a_re, a_im, name):
    t = u.shape[0]
    tm = min(512, t)
    nt = t // tm

    def body(u_ref, b_ref, ar_ref, ai_ref, o_ref, e_ref, sr_scr, si_scr):
        i = pl.program_id(1)
        o_ref[...] = _dot(u_ref[...].astype(BF16), b_ref[...]).astype(BF16).reshape(tm // 16, 16, 1024)
        _local_ends(o_ref, tm // 16, ar_ref[...], ai_ref[...], sr_scr, si_scr, e_ref, i == 0, i == nt - 1, False)

    return pl.pallas_call(
        body, name=name, grid=(8, nt),
        in_specs=[pl.BlockSpec((tm, 128), lambda s, i: (i, s)), pl.BlockSpec((None, 128, 1024), lambda s, i: (s, 0, 0)),
                  pl.BlockSpec((None, 8, 512), lambda s, i: (s, 0, 0)), pl.BlockSpec((None, 8, 512), lambda s, i: (s, 0, 0))],
        out_specs=[pl.BlockSpec((tm // 16, 16, 1024), lambda s, i: (i, 0, s)),
                   pl.BlockSpec((None, 8, 1024), lambda s, i: (s, 0, 0))],
        out_shape=[jax.ShapeDtypeStruct((t // 16, 16, S5NS), BF16), jax.ShapeDtypeStruct((8, 8, 1024), F32)],
        scratch_shapes=[pltpu.VMEM((8, 512), F32), pltpu.VMEM((8, 512), F32)],
        compiler_params=_cp("parallel", "arbitrary"))(u, bsg, a_re, a_im)


def s5_out_bwd(dy, csg, s, a_re, a_im, name):
    t = dy.shape[0]
    tm = min(512, t)
    nt = t // tm

    def body(dy_ref, c_ref, s_ref, ar_ref, ai_ref, e_ref, dc_ref, end_ref, sr_scr, si_scr):
        i = pl.program_id(1)

        @pl.when(i == 0)
        def _():
            dc_ref[...] = jnp.zeros_like(dc_ref)

        dyb = dy_ref[...].astype(BF16)
        e_ref[...] = _nt(dyb, c_ref[...]).astype(BF16).reshape(tm // 16, 16, 1024)
        dc_ref[...] += lax.dot_general(s_ref[...], dyb, (((0,), (0,)), ((), ())), preferred_element_type=F32)
        _local_ends(e_ref, tm // 16, ar_ref[...], -ai_ref[...], sr_scr, si_scr, end_ref, i == 0, i == nt - 1, True)

    rv = lambda i: nt - 1 - i
    return pl.pallas_call(
        body, name=name, grid=(8, nt),
        in_specs=[pl.BlockSpec((tm, 128), lambda s, i: (rv(i), s)), pl.BlockSpec((None, 1024, 128), lambda s, i: (s, 0, 0)),
                  pl.BlockSpec((tm, 1024), lambda s, i: (rv(i), s)),
                  pl.BlockSpec((None, 8, 512), lambda s, i: (s, 0, 0)), pl.BlockSpec((None, 8, 512), lambda s, i: (s, 0, 0))],
        out_specs=[pl.BlockSpec((tm // 16, 16, 1024), lambda s, i: (rv(i), 0, s)),
                   pl.BlockSpec((None, 1024, 128), lambda s, i: (s, 0, 0)),
                   pl.BlockSpec((None, 8, 1024), lambda s, i: (s, 0, 0))],
        out_shape=[jax.ShapeDtypeStruct((t // 16, 16, S5NS), BF16), jax.ShapeDtypeStruct((8, 1024, 128), F32),
                   jax.ShapeDtypeStruct((8, 8, 1024), F32)],
        scratch_shapes=[pltpu.VMEM((8, 512), F32), pltpu.VMEM((8, 512), F32)],
        compiler_params=_cp("parallel", "arbitrary"))(dy, csg, s, a_re, a_im)


def s5_scan_init(ends, a_re, a_im, lseg, reverse, name):
    nsq = int(math.log2(lseg))
    assert 2 ** nsq == lseg
    sgn = -1.0 if reverse else 1.0
    order = list(range(7, -1, -1)) if reverse else list(range(8))

    def body(e_ref, ar_ref, ai_ref, o_ref):
        pr = ar_ref[0:1, :]
        pi = sgn * ai_ref[0:1, :]
        for _ in range(nsq):
            pr, pi = pr * pr - pi * pi, 2.0 * pr * pi
        prev_r = jnp.zeros((1, 512), F32)
        prev_i = jnp.zeros((1, 512), F32)
        j0 = order[0]
        o_ref[j0:j0 + 1, 0:512] = prev_r
        o_ref[j0:j0 + 1, 512:1024] = prev_i
        for idx in range(1, 8):
            j, jp = order[idx], order[idx - 1]
            prev_r, prev_i = _cstep(pr, pi, prev_r, prev_i, e_ref[jp:jp + 1, 0:512], e_ref[jp:jp + 1, 512:1024])
            o_ref[j:j + 1, 0:512] = prev_r
            o_ref[j:j + 1, 512:1024] = prev_i

    return pl.pallas_call(
        body, name=name, grid=(8,),
        in_specs=[pl.BlockSpec((None, 8, 1024), lambda s: (s, 0, 0)), pl.BlockSpec((None, 8, 512), lambda s: (s, 0, 0)),
                  pl.BlockSpec((None, 8, 512), lambda s: (s, 0, 0))],
        out_specs=pl.BlockSpec((None, 8, 1024), lambda s: (s, 0, 0)),
        out_shape=jax.ShapeDtypeStruct((8, 8, 1024), F32), compiler_params=_cp("parallel"))(ends, a_re, a_im)


def s5_scan_fwd(b3, init, a_re, a_im, csg, u, d_row, name):
    nslab = b3.shape[0]
    ti = min(64, nslab)
    nb = nslab // ti
    rows = 16 * ti

    def body(b_ref, i_ref, ar_ref, ai_ref, c_ref, u_ref, d_ref, o_ref, y_ref, sr_scr, si_scr):
        @pl.when(pl.program_id(1) == 0)
        def _():
            sr_scr[...] = i_ref[:, 0:512]
            si_scr[...] = i_ref[:, 512:1024]

        ar = ar_ref[...]
        ai = ai_ref[...]

        def step(k, carry):
            b1, b2 = _halves(b_ref[k].astype(F32))
            r1, i1 = _cstep(ar, ai, carry[0], carry[1], *b1)
            r2, i2 = _cstep(ar, ai, r1, i1, *b2)
            o_ref[k] = _slab(r1, i1, r2, i2)
            return r2, i2

        sr, si = lax.fori_loop(0, ti, step, (sr_scr[...], si_scr[...]), unroll=4)
        sr_scr[...] = sr
        si_scr[...] = si
        y_ref[...] = _dot(o_ref[...].reshape(rows, 1024), c_ref[...]) + d_ref[...] * u_ref[...].astype(F32)

    return pl.pallas_call(
        body, name=name, grid=(8, nb),
        in_specs=[pl.BlockSpec((ti, 16, 1024), lambda s, tb: (tb, 0, s)), pl.BlockSpec((None, 8, 1024), lambda s, tb: (s, 0, 0)),
                  pl.BlockSpec((None, 8, 512), lambda s, tb: (s, 0, 0)), pl.BlockSpec((None, 8, 512), lambda s, tb: (s, 0, 0)),
                  pl.BlockSpec((None, 1024, 128), lambda s, tb: (s, 0, 0)), pl.BlockSpec((rows, 128), lambda s, tb: (tb, s)),
                  pl.BlockSpec((1, 128), lambda s, tb: (0, s))],
        out_specs=[pl.BlockSpec((ti, 16, 1024), lambda s, tb: (tb, 0, s)), pl.BlockSpec((rows, 128), lambda s, tb: (tb, s))],
        out_shape=[jax.ShapeDtypeStruct(b3.shape, BF16), jax.ShapeDtypeStruct((16 * nslab, S5W), F32)],
        scratch_shapes=[pltpu.VMEM((8, 512), F32), pltpu.VMEM((8, 512), F32)],
        compiler_params=_cp("parallel", "arbitrary"))(b3, init, a_re, a_im, csg, u, d_row)


def s5_scan_bwd(e3, linit, s3, sinit, a_re, a_im, bsg, u, dy, d_row, name):
    nslab = e3.shape[0]
    ti = min(64, nslab)
    nb = nslab // ti
    rows = 16 * ti

    def body(e_ref, li_ref, s_ref, sh_ref, si0_ref, ar_ref, ai_ref, b_ref, u_ref, dy_ref, d_ref,
             du_ref, db_ref, dd_ref, da_ref, o_ref, lr_scr, lim_scr):
        tb = pl.program_id(1)

        @pl.when(tb == 0)
        def _():
            lr_scr[...] = li_ref[:, 0:512]
            lim_scr[...] = li_ref[:, 512:1024]
            da_ref[...] = jnp.zeros_like(da_ref)
            db_ref[...] = jnp.zeros_like(db_ref)
            dd_ref[...] = jnp.zeros_like(dd_ref)

        ar = ar_ref[...]
        ai = -ai_ref[...]

        def slab(kk, lr, li, dar, dai, sp):
            e1, e2 = _halves(e_ref[kk].astype(F32))
            s1, _ = _halves(s_ref[kk].astype(F32))
            r2, i2 = _cstep(ar, ai, lr, li, *e2)
            dar = dar + r2 * s1[0] + i2 * s1[1]
            dai = dai + i2 * s1[0] - r2 * s1[1]
            r1, i1 = _cstep(ar, ai, r2, i2, *e1)
            dar = dar + r1 * sp[0] + i1 * sp[1]
            dai = dai + i1 * sp[0] - r1 * sp[1]
            o_ref[kk] = _slab(r1, i1, r2, i2)
            return r1, i1, dar, dai

        def step(k, carry):
            kk = ti - 1 - k
            return slab(kk, *carry, _halves(s_ref[kk - 1].astype(F32))[1])

        z = jnp.zeros((8, 512), F32)
        lr, li, dar, dai = lax.fori_loop(0, ti - 1, step, (lr_scr[...], lim_scr[...], z, z), unroll=2)
        first = tb == nb - 1
        halo = _halves(sh_ref[0].astype(F32))[1]
        sp = (jnp.where(first, si0_ref[:, 0:512], halo[0]), jnp.where(first, si0_ref[:, 512:1024], halo[1]))
        lr, li, dar, dai = slab(0, lr, li, dar, dai, sp)
        lr_scr[...] = lr
        lim_scr[...] = li
        da_ref[:, 0:512] += dar
        da_ref[:, 512:1024] += dai
        lb = o_ref[...].reshape(rows, 1024)
        uv = u_ref[...].astype(F32)
        dyv = dy_ref[...]
        du_ref[...] = _nt(lb, b_ref[...]) + d_ref[...] * dyv
        db_ref[...] += lax.dot_general(uv.astype(BF16), lb, (((0,), (0,)), ((), ())), preferred_element_type=F32)
        dd_ref[...] += jnp.sum(dyv * uv, axis=0, keepdims=True)

    rb = lambda tb: nb - 1 - tb
    return pl.pallas_call(
        body, name=name, grid=(8, nb),
        in_specs=[pl.BlockSpec((ti, 16, 1024), lambda s, tb: (rb(tb), 0, s)),
                  pl.BlockSpec((None, 8, 1024), lambda s, tb: (s, 0, 0)),
                  pl.BlockSpec((ti, 16, 1024), lambda s, tb: (rb(tb), 0, s)),
                  pl.BlockSpec((1, 16, 1024), lambda s, tb: (jnp.maximum(rb(tb) * ti - 1, 0), 0, s)),
                  pl.BlockSpec((None, 8, 1024), lambda s, tb: (s, 0, 0)),
                  pl.BlockSpec((None, 8, 512), lambda s, tb: (s, 0, 0)), pl.BlockSpec((None, 8, 512), lambda s, tb: (s, 0, 0)),
                  pl.BlockSpec((None, 128, 1024), lambda s, tb: (s, 0, 0)),
                  pl.BlockSpec((rows, 128), lambda s, tb: (rb(tb), s)), pl.BlockSpec((rows, 128), lambda s, tb: (rb(tb), s)),
                  pl.BlockSpec((1, 128), lambda s, tb: (0, s))],
        out_specs=[pl.BlockSpec((rows, 128), lambda s, tb: (rb(tb), s)),
                   pl.BlockSpec((None, 128, 1024), lambda s, tb: (s, 0, 0)), pl.BlockSpec((1, 128), lambda s, tb: (0, s)),
                   pl.BlockSpec((None, 8, 1024), lambda s, tb: (s, 0, 0))],
        out_shape=[jax.ShapeDtypeStruct((16 * nslab, S5W), F32), jax.ShapeDtypeStruct((8, 128, 1024), F32),
                   jax.ShapeDtypeStruct((1, S5W), F32), jax.ShapeDtypeStruct((8, 8, 1024), F32)],
        scratch_shapes=[pltpu.VMEM((ti, 16, 1024), BF16), pltpu.VMEM((8, 512), F32), pltpu.VMEM((8, 512), F32)],
        compiler_params=_cp("parallel", "arbitrary"))(e3, linit, s3, s3, sinit, a_re, a_im, bsg, u, dy, d_row)


_GC = math.sqrt(2.0 / math.pi)


def gelu_fwd(y, name):
    t, w = y.shape
    tm = min(512, t)

    def body(y_ref, o_ref):
        v = y_ref[...]
        o_ref[...] = (0.5 * v * (1.0 + jnp.tanh(_GC * (v + 0.044715 * v * v * v)))).astype(BF16)

    return pl.pallas_call(body, name=name, grid=(t // tm,), in_specs=[_row(tm, w)], out_specs=_row(tm, w),
                          out_shape=jax.ShapeDtypeStruct((t, w), BF16), compiler_params=_cp("parallel"))(y)


def gelu_bwd(y, dg, name):
    t, w = y.shape
    tm = min(512, t)

    def body(y_ref, d_ref, o_ref):
        v = y_ref[...]
        th = jnp.tanh(_GC * (v + 0.044715 * v * v * v))
        o_ref[...] = d_ref[...].astype(F32) * (0.5 * (1.0 + th) + 0.5 * v * (1.0 - th * th) * _GC * (1.0 + 3.0 * 0.044715 * v * v))

    return pl.pallas_call(body, name=name, grid=(t // tm,), in_specs=[_row(tm, w), _row(tm, w)], out_specs=_row(tm, w),
                          out_shape=jax.ShapeDtypeStruct((t, w), F32), compiler_params=_cp("parallel"))(y, dg)


def merge_fwd(proj, pa, glu, name):
    t = pa.shape[0]
    tm = min(256, t)

    def body(g_ref, pa_ref, glu_ref, o_ref):
        pb = glu_ref[:, 0:D].astype(F32) * _sigmoid(glu_ref[:, D:2 * D].astype(F32))
        o_ref[...] = (_sigmoid(g_ref[:, 0:D].astype(F32)) * pa_ref[...].astype(F32)
                      + _sigmoid(g_ref[:, D:2 * D].astype(F32)) * pb).astype(BF16)

    return pl.pallas_call(body, name=name, grid=(t // tm,), in_specs=[_row(tm, 2 * D), _row(tm, D), _row(tm, 2 * D)],
                          out_specs=_row(tm, D), out_shape=jax.ShapeDtypeStruct((t, D), BF16),
                          compiler_params=_cp("parallel"))(proj, pa, glu)


def merge_bwd(proj, pa, glu, dm, dproj, name):
    t = pa.shape[0]
    tm = min(256, t)

    def body(g_ref, pa_ref, glu_ref, dm_ref, _, dpa_ref, dglu_ref, dg_ref):
        dmv = dm_ref[...].astype(F32)
        pav = pa_ref[...].astype(F32)
        sa = _sigmoid(g_ref[:, 0:D].astype(F32))
        sb = _sigmoid(g_ref[:, D:2 * D].astype(F32))
        ga = glu_ref[:, 0:D].astype(F32)
        sg = _sigmoid(glu_ref[:, D:2 * D].astype(F32))
        pb = ga * sg
        dpb = sb * dmv
        dpa_ref[...] = (sa * dmv).astype(BF16)
        dglu_ref[:, 0:D] = (dpb * sg).astype(BF16)
        dglu_ref[:, D:2 * D] = (dpb * pb * (1.0 - sg)).astype(BF16)
        dg_ref[:, 0:D] = (dmv * pav * sa * (1.0 - sa)).astype(BF16)
        dg_ref[:, D:2 * D] = (dmv * pb * sb * (1.0 - sb)).astype(BF16)

    return pl.pallas_call(
        body, name=name, grid=(t // tm,),
        in_specs=[_row(tm, 2 * D), _row(tm, D), _row(tm, 2 * D), _row(tm, D), _HBM],
        out_specs=[_row(tm, D), _row(tm, 2 * D), _row(tm, 2 * D, P_GATES // (2 * D))],
        out_shape=[jax.ShapeDtypeStruct((t, D), BF16), jax.ShapeDtypeStruct((t, 2 * D), BF16),
                   jax.ShapeDtypeStruct(dproj.shape, BF16)],
        input_output_aliases={4: 2}, compiler_params=_cp("parallel"))(proj, pa, glu, dm, dproj)


def adamw(w, parts, m, v, name):
    r, c = w.shape
    p = parts.shape[0]
    tr = r if r <= 128 else 128
    c1 = 1.0 - ADAM_B1 ** ADAM_STEP
    c2 = 1.0 - ADAM_B2 ** ADAM_STEP

    def body(w_ref, p_ref, m_ref, v_ref, g_ref, d_ref, nm_ref, nv_ref):
        g = p_ref[0].astype(F32)
        for k in range(1, p):
            g = g + p_ref[k].astype(F32)
        mn = ADAM_B1 * m_ref[...] + (1.0 - ADAM_B1) * g
        vn = ADAM_B2 * v_ref[...] + (1.0 - ADAM_B2) * (g * g)
        g_ref[...] = g
        nm_ref[...] = mn
        nv_ref[...] = vn
        d_ref[...] = -ADAM_LR * ((mn / c1) / (jnp.sqrt(vn / c2) + ADAM_EPS) + ADAM_WD * w_ref[...])

    spec = pl.BlockSpec((tr, c), lambda i: (i, 0))
    o = jax.ShapeDtypeStruct((r, c), F32)
    return pl.pallas_call(
        body, name=name, grid=(pl.cdiv(r, tr),),
        in_specs=[spec, pl.BlockSpec((p, tr, c), lambda i: (0, i, 0)), spec, spec],
        out_specs=[spec, spec, spec, spec], out_shape=[o, o, o, o], compiler_params=_cp("parallel"))(w, parts, m, v)


def _s5_discretise(lambda_re, lambda_im, log_dt, b_re, b_im):
    dt = jnp.exp(log_dt)[:, None]
    lr = jnp.minimum(lambda_re, -1e-4)
    li = lambda_im
    mag = jnp.exp(lr * dt)
    ar = mag * jnp.cos(li * dt)
    ai = mag * jnp.sin(li * dt)
    den = lr * lr + li * li
    nr = ar - 1.0
    kr = (nr * lr + ai * li) / den
    ki = (ai * lr - nr * li) / den
    bbar_re = kr[..., None] * b_re - ki[..., None] * b_im
    bbar_im = kr[..., None] * b_im + ki[..., None] * b_re
    return ar, ai, bbar_re, bbar_im


def _block_diag(v):
    a, b = v.shape[2], v.shape[3]
    eye = jnp.eye(8, dtype=v.dtype)[None, :, None, :, None]
    return (v[:, :, :, None, :] * eye).reshape(8, 8 * a, 8 * b)


def _diag_blocks(m, a, b):
    eye = jnp.eye(8, dtype=m.dtype)[None, :, None, :, None]
    return jnp.sum(m.reshape(8, 8, a, 8, b) * eye, axis=3)


def _bsg_of(bb_re, bb_im):
    f = lambda b: _block_diag(b.reshape(8, 8, 64, 16).transpose(0, 1, 3, 2))
    return jnp.concatenate([f(bb_re), f(bb_im)], axis=2)


def _bsg_diag(dbsg):
    f = lambda x: _diag_blocks(x, 16, 64).transpose(0, 1, 3, 2).reshape(64, 64, 16)
    return f(dbsg[:, :, 0:512]), f(dbsg[:, :, 512:1024])


def _csg_of(c_re, c_im):
    f = lambda c: _block_diag(c.reshape(8, 8, 16, 64).transpose(0, 1, 3, 2))
    return jnp.concatenate([f(c_re), -f(c_im)], axis=1)


def _csg_diag(dcsg):
    f = lambda x: _diag_blocks(x, 64, 16).transpose(0, 1, 3, 2).reshape(64, 16, 64)
    return f(dcsg[:, 0:512, :]), -f(dcsg[:, 512:1024, :])


def _perm(a, t):
    return a.reshape(8, t // 8, a.shape[1]).transpose(1, 0, 2).reshape(t, a.shape[1])


def _unperm(a, t):
    return a.reshape(t // 8, 8, a.shape[1]).transpose(1, 0, 2).reshape(t, a.shape[1])


def _cols(g):
    return g.transpose(1, 0, 2).reshape(g.shape[1], N_DEV * g.shape[2])


def _rows(g):
    return g.reshape(N_DEV * g.shape[1], g.shape[2])


def _col_parts(g):
    r, c = g.shape
    return g.reshape(r, N_DEV, c // N_DEV).transpose(1, 0, 2)


def _row_parts(g):
    r, c = g.shape
    return g.reshape(N_DEV, r // N_DEV, c)


FB, FBP = D_FF // N_DEV, D_FFP // N_DEV


def _pad_ffn_in_shard(w):
    return jnp.pad(w.reshape(D, 2, FB), ((0, 0), (0, 0), (0, FBP - FB))).reshape(D, 2 * FBP)


def _unpad_ffn_in_shard(g):
    return g.reshape(D, 2, FBP)[:, :, :FB].reshape(D, 2 * FB)


def _pad_ffn_out_shard(w):
    return jnp.pad(w, ((0, FBP - FB), (0, 0)))


def sum_parts(parts, name):
    p, r, c = parts.shape
    tr = 256

    def body(p_ref, o_ref):
        g = p_ref[0].astype(F32)
        for k in range(1, p):
            g = g + p_ref[k].astype(F32)
        o_ref[...] = g

    return pl.pallas_call(body, name=name, grid=(r // tr,), in_specs=[pl.BlockSpec((p, tr, c), lambda i: (0, i, 0))],
                          out_specs=pl.BlockSpec((tr, c), lambda i: (i, 0)),
                          out_shape=jax.ShapeDtypeStruct((r, c), F32), compiler_params=_cp("parallel"))(parts)


def _pad_w_in(w):
    z = jnp.zeros((D, INP - P_DT - NH), w.dtype)
    return jnp.concatenate([w[:, O_GA:O_GB], w[:, O_GB:IN_COLS], w[:, 0:O_XBC], w[:, O_XBC:O_DT], w[:, O_U:O_GA],
                            w[:, O_DT:O_U], z], axis=1)


def _unpad_w_in(g):
    return jnp.concatenate([g[:, P_Z:P_XBC], g[:, P_XBC:P_U], g[:, P_DT:P_DT + NH], g[:, P_U:P_DT],
                            g[:, 0:D], g[:, D:2 * D]], axis=1)


_PACK = (("b_ada", 18432), ("norm_ffn1", 2048), ("norm_mix", 2048), ("conv_b", 3072), ("dt_bias", 32), ("a_log", 32),
         ("d_ssd", 32), ("ssd_norm_w", 2048), ("s5_lambda_re", 4096), ("s5_lambda_im", 4096), ("s5_b_re", 65536),
         ("s5_b_im", 65536), ("s5_c_re", 65536), ("s5_c_im", 65536), ("s5_d", 1024), ("s5_log_dt", 64),
         ("norm_ffn2", 2048), ("norm_final", 2048), ("loss", 1))
_PACK_ROWS = 304
_PACK_W = 1024


def _pack(d):
    flat = jnp.concatenate([d[k].reshape(-1).astype(F32) for k, _ in _PACK])
    return jnp.pad(flat, (0, _PACK_ROWS * _PACK_W - flat.shape[0])).reshape(_PACK_ROWS, _PACK_W)


def _unpack(a):
    flat = a.reshape(-1)
    out, off = {}, 0
    for k, n in _PACK:
        out[k] = flat[off:off + n]
        off += n
    return out


_TA = dict(tm=512, tn=512, tk=8192)


def _ffn_bwd(df, h, ab, act, w_in_shard_t, w_out_t, tag):
    dab, (g_wt,) = ffn_dab(df, w_out_t, ab, tag + "_dab", comm=[("ag", w_in_shard_t)])
    w_in_t = _rows(g_wt)
    dw_out = mm(act, df, ta=True, out_dtype=BF16, i_outer=True, name=tag + "_dwout", **_TA)
    dw_in, (x_out,) = mm(h, dab, ta=True, b_halves=True, out_dtype=BF16, i_outer=True, name=tag + "_dwin",
                         comm=[("xc", _row_parts(dw_out))], **_TA)
    dh, (x_in,) = mm(dab, w_in_t, a_halves=True, out_dtype=BF16, tk=5632, name=tag + "_dh", comm=[("xcc", dw_in)])
    g_in = _unpad_ffn_in_shard(sum_parts(x_in, tag + "_dwin_sum"))
    return dh, g_in[None], x_out


def kernel(x, c, w_ada, b_ada, norm_ffn1, w_ffn1_in, w_ffn1_out, norm_mix, w_in, conv_w, conv_b, dt_bias, a_log, d_ssd, ssd_norm_w, w_a_proj, s5_lambda_re, s5_lambda_im, s5_b_re, s5_b_im, s5_c_re, s5_c_im, s5_d, s5_log_dt, w_b_glu, w_out, norm_ffn2, w_ffn2_in, w_ffn2_out, norm_final, loss_target, m_w_ada, m_b_ada, m_norm_ffn1, m_w_ffn1_in, m_w_ffn1_out, m_norm_mix, m_w_in, m_conv_w, m_conv_b, m_dt_bias, m_a_log, m_d_ssd, m_ssd_norm_w, m_w_a_proj, m_s5_lambda_re, m_s5_lambda_im, m_s5_b_re, m_s5_b_im, m_s5_c_re, m_s5_c_im, m_s5_d, m_s5_log_dt, m_w_b_glu, m_w_out, m_norm_ffn2, m_w_ffn2_in, m_w_ffn2_out, m_norm_final, v_w_ada, v_b_ada, v_norm_ffn1, v_w_ffn1_in, v_w_ffn1_out, v_norm_mix, v_w_in, v_conv_w, v_conv_b, v_dt_bias, v_a_log, v_d_ssd, v_ssd_norm_w, v_w_a_proj, v_s5_lambda_re, v_s5_lambda_im, v_s5_b_re, v_s5_b_im, v_s5_c_re, v_s5_c_im, v_s5_d, v_s5_log_dt, v_w_b_glu, v_w_out, v_norm_ffn2, v_w_ffn2_in, v_w_ffn2_out, v_norm_final):
    args = dict(locals())
    t = x.shape[1]
    me = _my_id()
    xt = x[0]
    tgt = loss_target[0]
    small = {k: args[k] for k, _ in _PACK if k != "loss"}

    bf = lambda w: w[0].astype(BF16)
    ffn_in_shard = lambda w: _pad_ffn_in_shard(bf(w))
    ffn_out_shard = lambda w: _pad_ffn_out_shard(bf(w))

    c8 = all_gather(c, "ag_c").reshape(N_DEV, D)
    b_loc = lax.dynamic_slice(b_ada, (0, me * (N_ADA * D // N_DEV)), (1, N_ADA * D // N_DEV))
    m8 = ada_fwd(c8, w_ada[0], b_loc, "ada_fwd")
    mods = comm_call([("xc", m8.reshape(N_DEV, 1, -1))], "xc_mods")[0].reshape(1, N_ADA * D)

    h1, (wf1i, g_cw) = mod_fwd(xt, norm_ffn1, mods, 0, 1, name="mod1",
                               comm=[("agc", ffn_in_shard(w_ffn1_in)), ("ag", conv_w[0])])
    convw = _cols(g_cw)
    ab1, act1, (g_f1o, g_win, g_wap) = ffn_in_act(
        h1, wf1i, "ffn1_in", fwd=0.9,
        comm=[("ag", ffn_out_shard(w_ffn1_out)), ("ag", bf(w_in)), ("ag", bf(w_a_proj))])
    wf1o, winp, wap = _rows(g_f1o), _pad_w_in(_cols(g_win)), _rows(g_wap)
    f1, (g_wo, g_wbg) = mm(act1, wf1o, out_dtype=BF16, tk=5632, name="ffn1_out",
                           comm=[("ag", bf(w_out)), ("ag", bf(w_b_glu))])
    wo, wbg = _rows(g_wo), _cols(g_wbg)
    x1, h2 = mod_fwd(xt, norm_mix, mods, 3, 4, f=f1, gk=2, gscale=0.5, name="mod2")
    proj, (wf2i,) = mm(h2, winp, out_dtype=BF16, tm=1024, tn=512, i_outer=True, name="w_in",
                       comm=[("agc", ffn_in_shard(w_ffn2_in))])
    dtraw = mm(h2, winp[:, P_DT:P_DT + 128], tn=128, name="w_in_dt")
    cb_row = conv_b
    xc = conv_fwd(proj, convw, cb_row, "conv_fwd")
    row128 = lambda v: jnp.pad(v.reshape(1, -1), ((0, 0), (0, 128 - v.size)))
    dtb_row, alog_row = row128(dt_bias), row128(a_log)
    dx_row = jnp.repeat(d_ssd.reshape(-1), HP).reshape(1, 2048)
    rows = jnp.arange(128)[:, None]
    expm = ((rows % NH == jnp.arange(2048)[None, :] // HP) & (rows < 3 * NH)).astype(BF16)
    tri = (jnp.arange(LCH)[:, None] >= jnp.arange(LCH)[None, :]).astype(F32)
    y_ssd, hsave = ssd_fwd(xc, dtraw, dtb_row, alog_row, dx_row, expm, tri, "ssd_fwd")
    ya = ssd_out_fwd(y_ssd, proj, ssd_norm_w, "ssd_out")
    pa = mm(ya, wap, out_dtype=BF16, name="w_a_proj")

    s5p = (s5_lambda_re[0], s5_lambda_im[0], s5_log_dt[0], s5_b_re[0], s5_b_im[0])
    (ar, ai, bb_re, bb_im), s5_vjp = jax.vjp(_s5_discretise, *s5p)
    a_re8 = jnp.broadcast_to(ar.reshape(8, 1, 512), (8, 8, 512))
    a_im8 = jnp.broadcast_to(ai.reshape(8, 1, 512), (8, 8, 512))
    bsg = _bsg_of(bb_re, bb_im).astype(BF16)
    csg = _csg_of(s5_c_re[0], s5_c_im[0]).astype(BF16)
    d_row = s5_d.reshape(1, S5W)
    lseg = t // 8
    u_p = _perm(proj[:, P_U:P_U + S5W], t)
    bu3, ends_f = s5_in(u_p, bsg, a_re8, a_im8, "s5_in")
    sinit = s5_scan_init(ends_f, a_re8, a_im8, lseg, False, "s5_init_f")
    s3, yb_p = s5_scan_fwd(bu3, sinit, a_re8, a_im8, csg, u_p, d_row, "s5_scan_f")
    s2 = s3.reshape(t, S5NS)
    yb = _unperm(yb_p, t)
    gy = gelu_fwd(yb, "gelu")
    glu = mm(gy, wbg, out_dtype=BF16, name="w_b_glu")
    merged = merge_fwd(proj, pa, glu, "merge")
    o = mm(merged, wo, out_dtype=BF16, name="w_out")
    x2, h3 = mod_fwd(x1, norm_ffn2, mods, 6, 7, f=o, gk=5, gscale=1.0, name="mod3")
    ab3, act3, (g_f2o,) = ffn_in_act(h3, wf2i, "ffn2_in", comm=[("ag", ffn_out_shard(w_ffn2_out))])
    wf2o = _rows(g_f2o)
    f3 = mm(act3, wf2o, out_dtype=BF16, tk=5632, name="ffn2_out")

    dx3, df3, st_fin = final_fwd_bwd(x2, f3, mods, norm_final.reshape(1, D), tgt, "final")
    dh3, x_f2i, x_f2o = _ffn_bwd(df3, h3, ab3, act3, ffn_in_shard(w_ffn2_in).T, wf2o.T, "ffn2")
    dx2, do, st3 = mod_bwd(x2, dh3, dx3, norm_ffn2, mods, 7, fprev=o, gk=5, gscale=1.0, name="mod3_bwd")

    dmerged = mm(do, wo.T, out_dtype=BF16, name="w_out_dx")
    dwo = mm(merged, do, ta=True, out_dtype=BF16, i_outer=True, name="w_out_dw", **_TA)
    dpa, dglu, dproj = merge_bwd(proj, pa, glu, dmerged, lax.empty((t, INP), BF16), "merge_bwd")
    dwbg = mm(gy, dglu, ta=True, out_dtype=BF16, i_outer=True, name="w_b_glu_dw", **_TA)
    dgy, (x_wo,) = mm(dglu, wbg.T, out_dtype=BF16, name="w_b_glu_dx", comm=[("xc", _row_parts(dwo))])
    dyb_p = _perm(gelu_bwd(yb, dgy, "gelu_bwd"), t)
    e3, dcsg, ends_b = s5_out_bwd(dyb_p, csg, s2, a_re8, a_im8, "s5_out_bwd")
    linit = s5_scan_init(ends_b, a_re8, a_im8, lseg, True, "s5_init_b")
    du_p, dbsg, dd_row, da8 = s5_scan_bwd(e3, linit, s3, sinit, a_re8, a_im8, bsg, u_p, dyb_p, d_row, "s5_scan_b")
    du = _unperm(du_p, t).astype(BF16)
    da = jnp.sum(da8, axis=1)
    dbb_re, dbb_im = _bsg_diag(dbsg)
    g_lre, g_lim, g_ldt, g_bre, g_bim = s5_vjp((da[:, 0:512].reshape(64, 64), da[:, 512:1024].reshape(64, 64),
                                                dbb_re, dbb_im))
    g_cre, g_cim = _csg_diag(dcsg)

    dwap = mm(ya, dpa, ta=True, out_dtype=BF16, i_outer=True, name="w_a_proj_dw", **_TA)
    dya, (x_wbg,) = mm(dpa, wap.T, out_dtype=BF16, name="w_a_proj_dx", comm=[("xc", _col_parts(dwbg))])
    dy_ssd, dproj, st_sn = ssd_out_bwd(y_ssd, proj, dya, ssd_norm_w, dproj, "ssd_out_bwd")
    dxc, dproj, st_ssd = ssd_bwd(xc, dtraw, hsave, dy_ssd, dtb_row, alog_row, dx_row, expm, tri, dproj, "ssd_bwd")
    dpre, st_cv = conv_bwd_pre(proj, dxc, convw, cb_row, "conv_bwd_pre")
    dproj = conv_bwd_in(dpre, convw, dproj, "conv_bwd_in")
    dproj = lax.dynamic_update_slice(dproj, du, (0, P_U))
    dwinp, (x_wap, x_cw) = mm(h2, dproj, ta=True, out_dtype=BF16, i_outer=True, name="w_in_dw",
                              comm=[("xc", _row_parts(dwap)), ("xc", _col_parts(st_cv[0:CONV_K]))], **_TA)
    dh2, (x_win,) = mm(dproj, winp.T, out_dtype=BF16, tk=5376, name="w_in_dx",
                       comm=[("xc", _col_parts(_unpad_w_in(dwinp)))])
    dx1, df1, st2 = mod_bwd(x1, dh2, dx2, norm_mix, mods, 4, fprev=f1, gk=2, gscale=0.5, name="mod2_bwd")
    dh1, x_f1i, x_f1o = _ffn_bwd(df1, h1, ab1, act1, ffn_in_shard(w_ffn1_in).T, wf1o.T, "ffn1")
    gx, st1 = mod_bwd(xt, dh1, dx1, norm_ffn1, mods, 1, name="mod1_bwd")

    dmods = jnp.concatenate([st1[0], st1[1], st2[3], st2[0], st2[1], st3[3], st3[0], st3[1], st_fin[1]])
    part = {"b_ada": dmods, "norm_ffn1": st1[2], "norm_mix": st2[2], "conv_b": st_cv[4], "dt_bias": st_ssd[1, 0:NH],
            "a_log": st_ssd[0, 0:NH], "d_ssd": st_ssd[2, 0:NH], "ssd_norm_w": st_sn[0], "s5_lambda_re": g_lre,
            "s5_lambda_im": g_lim, "s5_b_re": g_bre, "s5_b_im": g_bim, "s5_c_re": g_cre, "s5_c_im": g_cim,
            "s5_d": dd_row, "s5_log_dt": g_ldt, "norm_ffn2": st3[2], "norm_final": st_fin[0],
            "loss": (0.5 / D) * jnp.sum(st_fin[2])}
    zero = {"loss": jnp.zeros((1,), F32)}
    gath = all_gather(_pack(part), "ag_small")
    sg, sd, sm, sv = adamw(_pack({**small, **zero}), gath, _pack({**{k: args["m_" + k] for k in small}, **zero}),
                           _pack({**{k: args["v_" + k] for k in small}, **zero}), "adamw_small")
    sg, sd, sm, sv = _unpack(sg), _unpack(sd), _unpack(sm), _unpack(sv)
    loss = sg["loss"][0]

    dm_loc = lax.dynamic_slice(gath.reshape(N_DEV, -1)[:, 0:N_ADA * D], (0, me * (N_ADA * D // N_DEV)),
                               (N_DEV, N_ADA * D // N_DEV))
    g_ada = ada_bwd(c8.T, dm_loc, "ada_bwd")
    big = {"w_ada": g_ada[None], "w_ffn1_in": x_f1i, "w_ffn1_out": x_f1o, "w_in": x_win, "conv_w": x_cw,
           "w_a_proj": x_wap, "w_b_glu": x_wbg, "w_out": x_wo, "w_ffn2_in": x_f2i, "w_ffn2_out": x_f2o}
    res = {}
    for k, parts in big.items():
        res[k] = adamw(args[k][0], parts, args["m_" + k][0], args["v_" + k][0], "adamw_" + k)

    names = ["w_ada", "b_ada", "norm_ffn1", "w_ffn1_in", "w_ffn1_out", "norm_mix", "w_in", "conv_w", "conv_b", "dt_bias",
             "a_log", "d_ssd", "ssd_norm_w", "w_a_proj", "s5_lambda_re", "s5_lambda_im", "s5_b_re", "s5_b_im", "s5_c_re",
             "s5_c_im", "s5_d", "s5_log_dt", "w_b_glu", "w_out", "norm_ffn2", "w_ffn2_in", "w_ffn2_out", "norm_final"]
    outs = [loss, gx[None]]
    for q, src in enumerate((sg, sd, sm, sv)):
        for k in names:
            if k in res:
                outs.append(res[k][q][None])
            else:
                outs.append(src[k].reshape(args[k].shape))
    return tuple(outs)
```

```python
import functools
import math

import jax
import jax.numpy as jnp
from jax import lax
from jax.experimental import pallas as pl
from jax.experimental.pallas import tpu as pltpu

F32 = jnp.float32
BF16 = jnp.bfloat16
HI = lax.Precision.HIGHEST

N_DEV = 8
D = 2048
D_FF = 5504
D_FFP = 5632
NH = 32
HP = 64
NG = 4
NST = 128
LCH = 128
CONV_DIM = 3072
CONV_K = 4
S5W = 1024
S5NS = 8192
N_ADA = 9
EPS = 1e-6
IN_COLS = 10272
INP = 10752
P_GATES, P_Z, P_XBC, P_U, P_DT = 0, 4096, 6144, 9216, 10240
O_XBC, O_DT, O_U, O_GA, O_GB = 2048, 5120, 5152, 6176, 8224
NEG = -1e30
VMEM_LIMIT = 56 * 1024 * 1024

ADAM_LR, ADAM_B1, ADAM_B2, ADAM_EPS, ADAM_WD, ADAM_STEP = 0.001, 0.9, 0.999, 1e-08, 0.01, 10


def _cp(*sem):
    return pltpu.CompilerParams(dimension_semantics=sem, vmem_limit_bytes=VMEM_LIMIT)


def _tile(dim, pref):
    if dim <= pref or dim % pref == 0:
        return min(dim, pref)
    for t in (2048, 1024, 512, 256, 128):
        if t <= pref and dim % t == 0:
            return t
    return dim


def _vec(w, cb=0):
    return pl.BlockSpec((1, w), lambda *_: (0, cb))


def _row(tm, w, cb=0):
    return pl.BlockSpec((tm, w), lambda i: (i, cb))


def _stats(w):
    return pl.BlockSpec((8, w), lambda *_: (0, 0))


_HBM = pl.BlockSpec(memory_space=pl.ANY)


def _sigmoid(x):
    return 1.0 / (1.0 + jnp.exp(-x))


def _softplus(x):
    return jnp.maximum(x, 0.0) + jnp.log1p(jnp.exp(-jnp.abs(x)))


def _peer(k):
    x, y, c = lax.axis_index("x"), lax.axis_index("y"), lax.axis_index("c")
    return (x ^ ((k >> 2) & 1), y ^ ((k >> 1) & 1), c ^ (k & 1))


def _my_id():
    return 4 * lax.axis_index("x") + 2 * lax.axis_index("y") + lax.axis_index("c")


def _comm_out_shape(kind, v):
    shape = {"ag": (N_DEV,) + v.shape, "xc": v.shape, "agc": (v.shape[0], N_DEV * v.shape[1]),
             "xcc": (N_DEV, v.shape[0], v.shape[1] // N_DEV)}[kind]
    return jax.ShapeDtypeStruct(shape, v.dtype)


def _comm_scratch(n):
    return [pltpu.SemaphoreType.DMA((n * N_DEV,)), pltpu.SemaphoreType.DMA((n * N_DEV,))]


class _Comm:
    def __init__(self, kinds, srcs, dsts, send_sems, recv_sems):
        self.items = list(zip(kinds, srcs, dsts))
        self.send_sems, self.recv_sems = send_sems, recv_sems
        x, y, c = lax.axis_index("x"), lax.axis_index("y"), lax.axis_index("c")
        self.me = 4 * x + 2 * y + c
        self.sib = (x, y, 1 - c)
        self.chips = [(1 - x, y), (x, 1 - y), (1 - x, 1 - y)]
        self.c = c

    @staticmethod
    def _id(p):
        return 4 * p[0] + 2 * p[1] + p[2]

    def _src(self, q, d):
        kind, src, _ = self.items[q]
        if kind == "xc":
            return src.at[d]
        if kind == "xcc":
            w = src.shape[1] // N_DEV
            return src.at[:, pl.ds(pl.multiple_of(d * w, 128), w)]
        return src

    def _slot(self, q, d):
        kind, _, dst = self.items[q]
        if kind == "agc":
            w = dst.shape[1] // N_DEV
            return dst.at[:, pl.ds(pl.multiple_of(d * w, 128), w)]
        return dst.at[d]

    def _push(self, q, k, src, slot, to):
        return pltpu.make_async_remote_copy(
            src_ref=src, dst_ref=self._slot(q, slot), send_sem=self.send_sems.at[q * N_DEV + k],
            recv_sem=self.recv_sems.at[q * N_DEV + k], device_id=to, device_id_type=pl.DeviceIdType.MESH)

    def _local(self, q):
        return pltpu.make_async_copy(self._src(q, self.me), self._slot(q, self.me), self.send_sems.at[q * N_DEV])

    def _direct(self, q):
        kind = self.items[q][0]
        if kind in ("xc", "xcc"):
            out = []
            for k in range(1, N_DEV):
                p = _peer(k)
                out.append((k, self._push(q, k, self._src(q, self._id(p)), self.me, p)))
            return out
        src = self.items[q][1]
        out = [(1, self._push(q, 1, src, self.me, self.sib))]
        for j, chip in enumerate(self.chips):
            out.append((2 + j, self._push(q, 2 + j, src, self.me, (*chip, self.c))))
        return out

    def _forwards(self, q):
        out = []
        for j, chip in enumerate(self.chips):
            slot = self._id((*chip, self.c))
            out.append((2 + j, 5 + j, self._push(q, 5 + j, self._slot(q, slot), slot, self.sib)))
        return out

    def start(self):
        for q in range(len(self.items)):
            self._local(q).start()
            for _, cp in self._direct(q):
                cp.start()

    def forward(self):
        for q, (kind, _, _) in enumerate(self.items):
            if kind not in ("ag", "agc"):
                continue
            for k_in, _, fwd in self._forwards(q):
                self._push(q, k_in, self._slot(q, self.me), self.me, self.sib).wait_recv()
                fwd.start()

    def finish(self):
        for q, (kind, _, _) in enumerate(self.items):
            self._local(q).wait()
            if kind in ("xc", "xcc"):
                for _, cp in self._direct(q):
                    cp.wait()
                continue
            for k, cp in self._direct(q):
                cp.wait_send()
                if k == 1:
                    cp.wait_recv()
            for _, _, fwd in self._forwards(q):
                fwd.wait()


def comm_call(items, name):
    kinds = [k for k, _ in items]
    n = len(items)

    def body(*refs):
        cm = _Comm(kinds, refs[:n], refs[n:2 * n], refs[2 * n], refs[2 * n + 1])
        cm.start()
        cm.forward()
        cm.finish()

    return pl.pallas_call(
        body, name=name,
        in_specs=[pl.BlockSpec(memory_space=pl.ANY)] * n, out_specs=[pl.BlockSpec(memory_space=pl.ANY)] * n,
        out_shape=[_comm_out_shape(k, v) for k, v in items], scratch_shapes=_comm_scratch(n),
    )(*[v for _, v in items])


def all_gather(v, name):
    return comm_call([("ag", v)], name)[0]


def _pcall(body, args, *, name, grid, in_specs, out_specs, out_shape, scratch_shapes=(), sem, comm=(), fwd=0.85):
    nc, n_in, n_out = len(comm), len(in_specs), len(out_shape)
    if not nc:
        return pl.pallas_call(body, name=name, grid=grid, in_specs=list(in_specs), out_specs=list(out_specs),
                              out_shape=list(out_shape), scratch_shapes=list(scratch_shapes),
                              compiler_params=_cp(*sem))(*args)
    kinds = [k for k, _ in comm]
    steps = math.prod(grid)
    fwd_step = min(int(fwd * steps), steps - 1)

    def carried(*refs):
        ins, csrc = refs[:n_in], refs[n_in:n_in + nc]
        outs, cdst = refs[n_in + nc:n_in + nc + n_out], refs[n_in + nc + n_out:n_in + 2 * nc + n_out]
        scr = refs[n_in + 2 * nc + n_out:]
        cm = _Comm(kinds, csrc, cdst, scr[-2], scr[-1])
        step = 0
        for d, g in enumerate(grid):
            step = step * g + pl.program_id(d)

        @pl.when(step == 0)
        def _():
            cm.start()

        body(*ins, *outs, *scr[:-2])

        @pl.when(step == fwd_step)
        def _():
            cm.forward()

        @pl.when(step == steps - 1)
        def _():
            cm.finish()

    hbm = pl.BlockSpec(memory_space=pl.ANY)
    out = pl.pallas_call(
        carried, name=name, grid=grid, in_specs=list(in_specs) + [hbm] * nc, out_specs=list(out_specs) + [hbm] * nc,
        out_shape=list(out_shape) + [_comm_out_shape(k, v) for k, v in comm],
        scratch_shapes=list(scratch_shapes) + _comm_scratch(nc), compiler_params=_cp(*(("arbitrary",) * len(grid))),
    )(*args, *[v for _, v in comm])
    return list(out[:n_out]), list(out[n_out:])


def mm(a, b, *, ta=False, out_dtype=F32, tm=512, tn=1024, tk=2048, i_outer=False, a_halves=False, b_halves=False,
       name, comm=()):
    if a_halves:
        m, kd = a.shape[1], 2 * a.shape[2]
    elif ta:
        kd, m = a.shape
    else:
        m, kd = a.shape
    kd2, n = (b.shape[1], 2 * b.shape[2]) if b_halves else b.shape
    assert kd == kd2 and not (ta and a_halves), (a.shape, b.shape, ta)
    tm, tn, tk = _tile(m, tm), _tile(n // 2 if b_halves else n, tn), _tile(kd // 2 if a_halves else kd, tk)
    nk = kd // tk
    nkh, njh = nk // 2, n // tn // 2
    grid = (m // tm, n // tn, nk) if i_outer else (n // tn, m // tm, nk)
    dims = (((0,) if ta else (1,), (0,)), ((), ()))

    def ix(f):
        return (lambda i, j, k: f(i, j, k)) if i_outer else (lambda j, i, k: f(i, j, k))

    def body(a_ref, b_ref, o_ref, *scr):
        p = lax.dot_general(a_ref[...], b_ref[...], dims, preferred_element_type=F32)
        if nk == 1:
            o_ref[...] = p.astype(o_ref.dtype)
        else:
            acc = scr[0]
            k = pl.program_id(2)

            @pl.when(k == 0)
            def _():
                acc[...] = p

            @pl.when(k > 0)
            def _():
                acc[...] += p

            @pl.when(k == nk - 1)
            def _():
                o_ref[...] = acc[...].astype(o_ref.dtype)

    if a_halves:
        a_spec = pl.BlockSpec((None, tm, tk), ix(lambda i, j, k: (k // nkh, i, k % nkh)))
    elif ta:
        a_spec = pl.BlockSpec((tk, tm), ix(lambda i, j, k: (k, i)))
    else:
        a_spec = pl.BlockSpec((tm, tk), ix(lambda i, j, k: (i, k)))
    if b_halves:
        b_spec = pl.BlockSpec((None, tk, tn), ix(lambda i, j, k: (j // njh, k, j % njh)))
    else:
        b_spec = pl.BlockSpec((tk, tn), ix(lambda i, j, k: (k, j)))
    out = _pcall(body, (a, b), name=name, grid=grid, in_specs=[a_spec, b_spec],
                 out_specs=[pl.BlockSpec((tm, tn), ix(lambda i, j, k: (i, j)))],
                 out_shape=[jax.ShapeDtypeStruct((m, n), out_dtype)],
                 scratch_shapes=[pltpu.VMEM((tm, tn), F32)] if nk > 1 else [],
                 sem=("parallel", "parallel", "arbitrary"), comm=comm)
    return (out[0][0], out[1]) if comm else out[0]


def ffn_in_act(h, w, name, comm=(), fwd=0.85):
    t = h.shape[0]
    tm, tn = _tile(t, 512), 512
    nj = D_FFP // tn

    def body(h_ref, wa_ref, wb_ref, ab_ref, act_ref):
        hv = h_ref[...]
        pa = _dot(hv, wa_ref[...])
        pb = _dot(hv, wb_ref[...])
        ab_ref[0] = pa.astype(BF16)
        ab_ref[1] = pb.astype(BF16)
        act_ref[...] = (pa * _sigmoid(pa) * pb).astype(BF16)

    out = _pcall(body, (h, w, w), name=name, grid=(nj, t // tm),
                 in_specs=[pl.BlockSpec((tm, D), lambda j, i: (i, 0)), pl.BlockSpec((D, tn), lambda j, i: (0, j)),
                           pl.BlockSpec((D, tn), lambda j, i: (0, nj + j))],
                 out_specs=[pl.BlockSpec((2, tm, tn), lambda j, i: (0, i, j)), pl.BlockSpec((tm, tn), lambda j, i: (i, j))],
                 out_shape=[jax.ShapeDtypeStruct((2, t, D_FFP), BF16), jax.ShapeDtypeStruct((t, D_FFP), BF16)],
                 sem=("parallel", "parallel"), comm=comm, fwd=fwd)
    return (out[0][0], out[0][1], out[1]) if comm else (out[0], out[1])


def ffn_dab(df, w_out_t, ab, name, comm=()):
    t = df.shape[0]
    tm, tn = _tile(t, 1024), 512

    def body(d_ref, w_ref, ab_ref, o_ref):
        dv = _dot(d_ref[...], w_ref[...])
        a = ab_ref[0].astype(F32)
        b = ab_ref[1].astype(F32)
        s = _sigmoid(a)
        o_ref[0] = (dv * b * (s * (1.0 + a * (1.0 - s)))).astype(BF16)
        o_ref[1] = (dv * (a * s)).astype(BF16)

    out = _pcall(body, (df, w_out_t, ab), name=name, grid=(D_FFP // tn, t // tm),
                 in_specs=[pl.BlockSpec((tm, D), lambda j, i: (i, 0)), pl.BlockSpec((D, tn), lambda j, i: (0, j)),
                           pl.BlockSpec((2, tm, tn), lambda j, i: (0, i, j))],
                 out_specs=[pl.BlockSpec((2, tm, tn), lambda j, i: (0, i, j))],
                 out_shape=[jax.ShapeDtypeStruct((2, t, D_FFP), BF16)], sem=("parallel", "parallel"), comm=comm, fwd=0.85)
    return (out[0][0], out[1]) if comm else out[0]


def ada_fwd(c8, w_loc, b_loc, name):
    n = w_loc.shape[1]
    tn = 256

    def body(c_ref, w_ref, b_ref, o_ref):
        cv = c_ref[...]
        ca = cv * _sigmoid(cv)
        o_ref[...] = jnp.dot(ca, w_ref[...], precision=HI, preferred_element_type=F32) + b_ref[...]

    return pl.pallas_call(
        body, name=name, grid=(n // tn,),
        in_specs=[pl.BlockSpec((N_DEV, D), lambda j: (0, 0)), pl.BlockSpec((D, tn), lambda j: (0, j)),
                  pl.BlockSpec((1, tn), lambda j: (0, j))],
        out_specs=pl.BlockSpec((N_DEV, tn), lambda j: (0, j)),
        out_shape=jax.ShapeDtypeStruct((N_DEV, n), F32), compiler_params=_cp("parallel"),
    )(c8, w_loc, b_loc)


def ada_bwd(c8t, dm_loc, name):
    n = dm_loc.shape[1]
    tn = 256

    def body(c_ref, d_ref, o_ref):
        cv = c_ref[...]
        ca = cv * _sigmoid(cv)
        o_ref[...] = jnp.dot(ca, d_ref[...], precision=HI, preferred_element_type=F32)

    return pl.pallas_call(
        body, name=name, grid=(n // tn,),
        in_specs=[pl.BlockSpec((D, N_DEV), lambda j: (0, 0)), pl.BlockSpec((N_DEV, tn), lambda j: (0, j))],
        out_specs=pl.BlockSpec((D, tn), lambda j: (0, j)),
        out_shape=jax.ShapeDtypeStruct((D, n), F32), compiler_params=_cp("parallel"),
    )(c8t, dm_loc)


def mod_fwd(x, nw, mods, shk, sck, *, f=None, gk=None, gscale=1.0, name, comm=()):
    t = x.shape[0]
    tm = min(256, t)
    res = f is not None

    def body(*refs):
        if res:
            x_ref, f_ref, g_ref, nw_ref, sh_ref, sc_ref, x1_ref, h_ref = refs
            xv = x_ref[...] + (gscale * g_ref[...]) * f_ref[...].astype(F32)
            x1_ref[...] = xv
        else:
            x_ref, nw_ref, sh_ref, sc_ref, h_ref = refs
            xv = x_ref[...]
        r = lax.rsqrt(jnp.mean(xv * xv, axis=-1, keepdims=True) + EPS)
        h_ref[...] = ((xv * r * nw_ref[...]) * (1.0 + sc_ref[...]) + sh_ref[...]).astype(BF16)

    ins = [x] + ([f, mods] if res else []) + [nw, mods, mods]
    specs = [_row(tm, D)] + ([_row(tm, D), _vec(D, gk)] if res else []) + [_vec(D), _vec(D, shk), _vec(D, sck)]
    outs = ([jax.ShapeDtypeStruct((t, D), F32)] if res else []) + [jax.ShapeDtypeStruct((t, D), BF16)]
    ospecs = ([_row(tm, D)] if res else []) + [_row(tm, D)]
    out = _pcall(body, ins, name=name, grid=(t // tm,), in_specs=specs, out_specs=ospecs, out_shape=outs,
                 sem=("parallel",), comm=comm)
    if comm:
        return (out[0] if res else out[0][0]), out[1]
    return out if res else out[0]


def final_fwd_bwd(x2, f3, mods, nf, tgt, name):
    t = x2.shape[0]
    tm = min(256, t)

    def body(x_ref, f_ref, g_ref, nf_ref, t_ref, dx_ref, df_ref, st_ref):
        @pl.when(pl.program_id(0) == 0)
        def _():
            st_ref[...] = jnp.zeros_like(st_ref)

        g = 0.5 * g_ref[...]
        fv = f_ref[...].astype(F32)
        xv = x_ref[...] + g * fv
        r = lax.rsqrt(jnp.mean(xv * xv, axis=-1, keepdims=True) + EPS)
        xh = xv * r
        nfv = nf_ref[...]
        e = xh * nfv - t_ref[...]
        st_ref[2:3, :] += jnp.sum(e * e, axis=0, keepdims=True)
        dy = e * (1.0 / D)
        st_ref[0:1, :] += jnp.sum(dy * xh, axis=0, keepdims=True)
        dxh = dy * nfv
        dx = r * (dxh - xh * jnp.mean(dxh * xh, axis=-1, keepdims=True))
        dx_ref[...] = dx
        df_ref[...] = (g * dx).astype(BF16)
        st_ref[1:2, :] += 0.5 * jnp.sum(fv * dx, axis=0, keepdims=True)

    return pl.pallas_call(
        body, name=name, grid=(t // tm,),
        in_specs=[_row(tm, D), _row(tm, D), _vec(D, 8), _vec(D), _row(tm, D)],
        out_specs=[_row(tm, D), _row(tm, D), _stats(D)],
        out_shape=[jax.ShapeDtypeStruct((t, D), F32), jax.ShapeDtypeStruct((t, D), BF16),
                   jax.ShapeDtypeStruct((8, D), F32)],
        compiler_params=_cp("arbitrary"),
    )(x2, f3, mods, nf, tgt)


def mod_bwd(x_in, dh, dx_out, nw, mods, sck, *, fprev=None, gk=None, gscale=1.0, name):
    t = x_in.shape[0]
    tm = min(256, t)
    gate = fprev is not None

    def body(*refs):
        if gate:
            x_ref, dh_ref, dxo_ref, nw_ref, sc_ref, f_ref, g_ref, dx_ref, df_ref, st_ref = refs
        else:
            x_ref, dh_ref, dxo_ref, nw_ref, sc_ref, dx_ref, st_ref = refs

        @pl.when(pl.program_id(0) == 0)
        def _():
            st_ref[...] = jnp.zeros_like(st_ref)

        xv = x_ref[...]
        dhv = dh_ref[...].astype(F32)
        r = lax.rsqrt(jnp.mean(xv * xv, axis=-1, keepdims=True) + EPS)
        xh = xv * r
        nwv = nw_ref[...]
        st_ref[0:1, :] += jnp.sum(dhv, axis=0, keepdims=True)
        st_ref[1:2, :] += jnp.sum(dhv * (xh * nwv), axis=0, keepdims=True)
        dn = dhv * (1.0 + sc_ref[...])
        st_ref[2:3, :] += jnp.sum(dn * xh, axis=0, keepdims=True)
        dxh = dn * nwv
        dx = dxo_ref[...] + r * (dxh - xh * jnp.mean(dxh * xh, axis=-1, keepdims=True))
        dx_ref[...] = dx
        if gate:
            df_ref[...] = ((gscale * g_ref[...]) * dx).astype(BF16)
            st_ref[3:4, :] += gscale * jnp.sum(f_ref[...].astype(F32) * dx, axis=0, keepdims=True)

    ins = [x_in, dh, dx_out, nw, mods] + ([fprev, mods] if gate else [])
    specs = [_row(tm, D), _row(tm, D), _row(tm, D), _vec(D), _vec(D, sck)] + ([_row(tm, D), _vec(D, gk)] if gate else [])
    outs = [jax.ShapeDtypeStruct((t, D), F32)] + ([jax.ShapeDtypeStruct((t, D), BF16)] if gate else []) + \
        [jax.ShapeDtypeStruct((8, D), F32)]
    ospecs = [_row(tm, D)] + ([_row(tm, D)] if gate else []) + [_stats(D)]
    return pl.pallas_call(body, name=name, grid=(t // tm,), in_specs=specs, out_specs=ospecs, out_shape=outs,
                          compiler_params=_cp("arbitrary"))(*ins)


def _conv_pre(cur, prev8, w, b, tm):
    full = jnp.concatenate([prev8, cur], axis=0)
    pre = b + w[3:4, :] * cur
    for k in range(CONV_K - 1):
        s = CONV_K - 1 - k
        pre = pre + w[k:k + 1, :] * pltpu.roll(full, s, 0)[8:8 + tm, :]
    return pre


def conv_fwd(proj, cw_full, cb_full, name):
    t = proj.shape[0]
    tm = min(256, t)
    cwid = 1024
    cb0 = P_XBC // cwid

    def body(x_ref, p_ref, w_ref, b_ref, o_ref):
        i = pl.program_id(1)
        prev8 = jnp.where(i == 0, 0.0, p_ref[...].astype(F32)[8:16])
        pre = _conv_pre(x_ref[...].astype(F32), prev8, w_ref[...], b_ref[...], tm)
        o_ref[...] = pre * _sigmoid(pre)

    return pl.pallas_call(
        body, name=name, grid=(CONV_DIM // cwid, t // tm),
        in_specs=[pl.BlockSpec((tm, cwid), lambda j, i: (i, cb0 + j)),
                  pl.BlockSpec((16, cwid), lambda j, i: (jnp.maximum(i * (tm // 16) - 1, 0), cb0 + j)),
                  pl.BlockSpec((CONV_K, cwid), lambda j, i: (0, j)), pl.BlockSpec((1, cwid), lambda j, i: (0, j))],
        out_specs=pl.BlockSpec((tm, cwid), lambda j, i: (i, j)),
        out_shape=jax.ShapeDtypeStruct((t, CONV_DIM), F32), compiler_params=_cp("parallel", "parallel"),
    )(proj, proj, cw_full, cb_full)


def conv_bwd_pre(proj, dxc, cw_full, cb_full, name):
    t = proj.shape[0]
    tm = min(256, t)
    cwid = 1024
    cb0 = P_XBC // cwid

    def body(x_ref, p_ref, d_ref, w_ref, b_ref, o_ref, st_ref):
        i = pl.program_id(1)

        @pl.when(i == 0)
        def _():
            st_ref[...] = jnp.zeros_like(st_ref)

        cur = x_ref[...].astype(F32)
        prev8 = jnp.where(i == 0, 0.0, p_ref[...].astype(F32)[8:16])
        pre = _conv_pre(cur, prev8, w_ref[...], b_ref[...], tm)
        s = _sigmoid(pre)
        dpre = d_ref[...] * (s * (1.0 + pre * (1.0 - s)))
        o_ref[...] = dpre
        st_ref[4:5, :] += jnp.sum(dpre, axis=0, keepdims=True)
        st_ref[3:4, :] += jnp.sum(dpre * cur, axis=0, keepdims=True)
        full = jnp.concatenate([prev8, cur], axis=0)
        for k in range(CONV_K - 1):
            sft = CONV_K - 1 - k
            st_ref[k:k + 1, :] += jnp.sum(dpre * pltpu.roll(full, sft, 0)[8:8 + tm, :], axis=0, keepdims=True)

    return pl.pallas_call(
        body, name=name, grid=(CONV_DIM // cwid, t // tm),
        in_specs=[pl.BlockSpec((tm, cwid), lambda j, i: (i, cb0 + j)),
                  pl.BlockSpec((16, cwid), lambda j, i: (jnp.maximum(i * (tm // 16) - 1, 0), cb0 + j)),
                  pl.BlockSpec((tm, cwid), lambda j, i: (i, j)),
                  pl.BlockSpec((CONV_K, cwid), lambda j, i: (0, j)), pl.BlockSpec((1, cwid), lambda j, i: (0, j))],
        out_specs=[pl.BlockSpec((tm, cwid), lambda j, i: (i, j)), pl.BlockSpec((8, cwid), lambda j, i: (0, j))],
        out_shape=[jax.ShapeDtypeStruct((t, CONV_DIM), F32), jax.ShapeDtypeStruct((8, CONV_DIM), F32)],
        compiler_params=_cp("parallel", "arbitrary"),
    )(proj, proj, dxc, cw_full, cb_full)


def conv_bwd_in(dpre, cw_full, dproj, name):
    t = dpre.shape[0]
    tm = min(256, t)
    cwid = 1024
    nt = t // tm

    def body(d_ref, n_ref, w_ref, _, o_ref):
        i = pl.program_id(1)
        cur = d_ref[...]
        nxt = jnp.where(i == nt - 1, 0.0, n_ref[...])
        full = jnp.concatenate([cur, nxt], axis=0)
        w = w_ref[...]
        acc = w[3:4, :] * cur
        for k in range(CONV_K - 1):
            s = CONV_K - 1 - k
            acc = acc + w[k:k + 1, :] * pltpu.roll(full, tm + 8 - s, 0)[0:tm, :]
        o_ref[...] = acc.astype(BF16)

    return pl.pallas_call(
        body, name=name, grid=(CONV_DIM // cwid, nt),
        in_specs=[pl.BlockSpec((tm, cwid), lambda j, i: (i, j)),
                  pl.BlockSpec((8, cwid), lambda j, i: (jnp.minimum((i + 1) * (tm // 8), t // 8 - 1), j)),
                  pl.BlockSpec((CONV_K, cwid), lambda j, i: (0, j)), _HBM],
        out_specs=pl.BlockSpec((tm, cwid), lambda j, i: (i, P_XBC // cwid + j)),
        out_shape=jax.ShapeDtypeStruct(dproj.shape, BF16), input_output_aliases={3: 0},
        compiler_params=_cp("parallel", "parallel"),
    )(dpre, dpre, cw_full, dproj)


def _nt(a, b):
    return lax.dot_general(a, b, (((1,), (1,)), ((), ())), preferred_element_type=F32)


def _dot(a, b):
    return jnp.dot(a, b, preferred_element_type=F32)


def _head_lanes():
    return lax.broadcasted_iota(jnp.int32, (1, 128), 1) < NH


def _expand_heads(x, e3):
    x = jnp.where(_head_lanes(), x, 0.0)
    hi = x.astype(BF16).astype(F32)
    r1 = x - hi
    mid = r1.astype(BF16).astype(F32)
    packed = hi + pltpu.roll(mid, NH, 1) + pltpu.roll(r1 - mid, 2 * NH, 1)
    return _dot(packed.astype(BF16), e3)


def _reduce_heads(v, e3):
    hi = v.astype(BF16)
    lo = (v - hi.astype(F32)).astype(BF16)
    return jnp.where(_head_lanes(), _nt(hi, e3) + _nt(lo, e3), 0.0)


def _ssd_common(dt_ref, dtb_ref, al_ref, exp_ref, tri_ref):
    a_row = jnp.where(_head_lanes(), -jnp.exp(al_ref[...]), 0.0)
    zraw = dt_ref[...] + dtb_ref[...]
    dtv = _softplus(zraw)
    cs = jnp.dot(tri_ref[...], dtv * a_row, precision=HI, preferred_element_type=F32)
    e3 = exp_ref[...]
    return a_row, zraw, dtv, cs, _expand_heads(cs, e3), _expand_heads(dtv, e3)


def ssd_fwd(xc, proj, dtb_row, alog_row, dx_row, expm, tri, name):
    t = xc.shape[0]
    nc = t // LCH

    def body(xs_ref, bm_ref, cm_ref, dt_ref, dtb_ref, al_ref, dxr_ref, exp_ref, tri_ref, y_ref, hs_ref, h_scr):
        @pl.when(pl.program_id(0) == 0)
        def _():
            h_scr[...] = jnp.zeros_like(h_scr)

        _, _, _, cs, csx, dtx = _ssd_common(dt_ref, dtb_ref, al_ref, exp_ref, tri_ref)
        cst = cs.T
        csl = csx[LCH - 1:LCH, :]
        xs = xs_ref[...]
        xd = xs * dtx
        xdw = xd * jnp.exp(csl - csx)
        ecs = jnp.exp(csx)
        ecl = jnp.exp(csl)
        tril = lax.broadcasted_iota(jnp.int32, (LCH, LCH), 0) >= lax.broadcasted_iota(jnp.int32, (LCH, LCH), 1)
        hs_ref[...] = h_scr[...]
        for g in range(NG):
            gc = slice(g * 512, (g + 1) * 512)
            bm = bm_ref[:, g * NST:(g + 1) * NST]
            cmb = cm_ref[:, g * NST:(g + 1) * NST].astype(BF16)
            gm = _nt(cmb, bm.astype(BF16))
            hg = h_scr[:, gc]
            yo = _dot(cmb, hg.astype(BF16)) * ecs[:, gc]
            st = _dot(bm.T.astype(BF16), xdw[:, gc].astype(BF16))
            for r in range(8):
                h = g * 8 + r
                hc = slice(h * HP, (h + 1) * HP)
                seg = cs[:, h:h + 1] - cst[h:h + 1, :]
                m = (gm * jnp.exp(jnp.where(tril, seg, NEG))).astype(BF16)
                yd = _dot(m, xd[:, hc].astype(BF16))
                y_ref[:, hc] = yd + yo[:, r * HP:(r + 1) * HP] + dxr_ref[:, hc] * xs[:, hc]
            h_scr[:, gc] = ecl[:, gc] * hg + st

    return pl.pallas_call(
        body, name=name, grid=(nc,),
        in_specs=[pl.BlockSpec((LCH, 2048), lambda c: (c, 0)), pl.BlockSpec((LCH, 512), lambda c: (c, 4)),
                  pl.BlockSpec((LCH, 512), lambda c: (c, 5)), pl.BlockSpec((LCH, 128), lambda c: (c, 0)),
                  _vec(128), _vec(128), _vec(2048), pl.BlockSpec((128, 2048), lambda c: (0, 0)),
                  pl.BlockSpec((LCH, LCH), lambda c: (0, 0))],
        out_specs=[pl.BlockSpec((LCH, 2048), lambda c: (c, 0)), pl.BlockSpec((None, NST, 2048), lambda c: (c, 0, 0))],
        out_shape=[jax.ShapeDtypeStruct((t, 2048), F32), jax.ShapeDtypeStruct((nc, NST, 2048), F32)],
        scratch_shapes=[pltpu.VMEM((NST, 2048), F32)],
        compiler_params=_cp("arbitrary"),
    )(xc, xc, xc, proj, dtb_row, alog_row, dx_row, expm, tri)


def ssd_bwd(xc, proj, hsave, dy, dtb_row, alog_row, dx_row, expm, tri, dproj, name):
    t = xc.shape[0]
    nc = t // LCH

    def body(xs_ref, bm_ref, cm_ref, dt_ref, hs_ref, dy_ref, dtb_ref, al_ref, dxr_ref, exp_ref, tri_ref, _,
             dxc_ref, ddt_ref, st_ref, dh_scr, dxd_scr, dcsx_scr):
        @pl.when(pl.program_id(0) == 0)
        def _():
            dh_scr[...] = jnp.zeros_like(dh_scr)
            st_ref[...] = jnp.zeros_like(st_ref)

        a_row, zraw, dtv, cs, csx, dtx = _ssd_common(dt_ref, dtb_ref, al_ref, exp_ref, tri_ref)
        e = exp_ref[...]
        cst = cs.T
        csl = csx[LCH - 1:LCH, :]
        xs = xs_ref[...]
        xd = xs * dtx
        wend = jnp.exp(csl - csx)
        xdw = xd * wend
        ecs = jnp.exp(csx)
        ecl = jnp.exp(csl)
        ri = lax.broadcasted_iota(jnp.int32, (LCH, LCH), 0)
        ci = lax.broadcasted_iota(jnp.int32, (LCH, LCH), 1)
        tril = ri >= ci
        triu = ri <= ci
        lane = lax.broadcasted_iota(jnp.int32, (1, 128), 1)
        dyv = dy_ref[...]
        dxr = dxr_ref[...]
        st_ref[2:3, :] += _reduce_heads(jnp.sum(dyv * xs, axis=0, keepdims=True), e)
        dcs = jnp.zeros((LCH, 128), F32)
        for g in range(NG):
            gc = slice(g * 512, (g + 1) * 512)
            bmb = bm_ref[:, g * NST:(g + 1) * NST].astype(BF16)
            cm = cm_ref[:, g * NST:(g + 1) * NST]
            cmb = cm.astype(BF16)
            hg = hs_ref[:, gc]
            hgb = hg.astype(BF16)
            dhc = dh_scr[:, gc]
            dhcb = dhc.astype(BF16)
            dyg = dyv[:, gc]
            yo = _dot(cmb, hgb) * ecs[:, gc]
            dq = (dyg * ecs[:, gc]).astype(BF16)
            dcm = _nt(dq, hgb)
            dh_yo = _dot(cm.T.astype(BF16), dq)
            dxdw = _dot(bmb, dhcb)
            dbm = _nt(xdw[:, gc].astype(BF16), dhcb)
            tt = dxdw * xdw[:, gc]
            dcsx_g = dyg * yo - tt
            dcsl_g = jnp.sum(tt, axis=0, keepdims=True) + jnp.sum(dhc * hg, axis=0, keepdims=True) * ecl[:, gc]
            dxd_scr[:, gc] = dxdw * wend[:, gc]
            dh_scr[:, gc] = ecl[:, gc] * dhc + dh_yo
            gm = _nt(cmb, bmb)
            gmt = _nt(bmb, cmb)
            dg = jnp.zeros((LCH, LCH), F32)
            dgt = jnp.zeros((LCH, LCH), F32)
            for r in range(8):
                h = g * 8 + r
                hc = slice(h * HP, (h + 1) * HP)
                seg = cs[:, h:h + 1] - cst[h:h + 1, :]
                lm = jnp.exp(jnp.where(tril, seg, NEG))
                lmt = jnp.exp(jnp.where(triu, -seg, NEG))
                mm_ = gm * lm
                mmt = gmt * lmt
                xdh = xd[:, hc].astype(BF16)
                dyh = dyv[:, hc].astype(BF16)
                dm = _nt(dyh, xdh)
                dmt = _nt(xdh, dyh)
                dxd_scr[:, hc] += _dot(mmt.astype(BF16), dyh)
                rs = jnp.sum(dm * mm_, axis=1, keepdims=True) - jnp.sum(dmt * mmt, axis=1, keepdims=True)
                dcs = dcs + rs * jnp.where(lane == h, 1.0, 0.0)
                dg = dg + dm * lm
                dgt = dgt + dmt * lmt
            dcm = dcm + _dot(dg.astype(BF16), bmb)
            dbm = dbm + _dot(dgt.astype(BF16), cmb)
            dxc_ref[:, 2048 + g * NST:2048 + (g + 1) * NST] = dbm
            dxc_ref[:, 2560 + g * NST:2560 + (g + 1) * NST] = dcm
            dcsx_scr[:, gc] = dcsx_g
            dcsx_scr[LCH - 1:LCH, gc] += dcsl_g
        dxd = dxd_scr[...]
        dxc_ref[:, 0:2048] = dxr * dyv + dxd * dtx
        ddtv = _reduce_heads(dxd * xs, e)
        dcs = dcs + _reduce_heads(dcsx_scr[...], e)
        dda =lax.dot_general(tri_ref[...], dcs, (((0,), (0,)), ((), ())), precision=HI, preferred_element_type=F32)
        ddtv = ddtv + dda * a_row
        st_ref[0:1, :] += jnp.sum(dda * dtv, axis=0, keepdims=True) * a_row
        ddt = ddtv * _sigmoid(zraw)
        ddt_ref[:, 0:128] = ddt.astype(BF16)
        ddt_ref[:, 128:INP - P_DT] = jnp.zeros((LCH, INP - P_DT - 128), BF16)
        st_ref[1:2, :] += jnp.sum(ddt, axis=0, keepdims=True)

    rc = lambda c: nc - 1 - c
    return pl.pallas_call(
        body, name=name, grid=(nc,),
        in_specs=[pl.BlockSpec((LCH, 2048), lambda c: (rc(c), 0)), pl.BlockSpec((LCH, 512), lambda c: (rc(c), 4)),
                  pl.BlockSpec((LCH, 512), lambda c: (rc(c), 5)),
                  pl.BlockSpec((LCH, 128), lambda c: (rc(c), 0)),
                  pl.BlockSpec((None, NST, 2048), lambda c: (rc(c), 0, 0)),
                  pl.BlockSpec((LCH, 2048), lambda c: (rc(c), 0)),
                  _vec(128), _vec(128), _vec(2048), pl.BlockSpec((128, 2048), lambda c: (0, 0)),
                  pl.BlockSpec((LCH, LCH), lambda c: (0, 0)), _HBM],
        out_specs=[pl.BlockSpec((LCH, CONV_DIM), lambda c: (rc(c), 0)),
                   pl.BlockSpec((LCH, INP - P_DT), lambda c: (rc(c), P_DT // (INP - P_DT))), _stats(128)],
        out_shape=[jax.ShapeDtypeStruct((t, CONV_DIM), F32), jax.ShapeDtypeStruct(dproj.shape, BF16),
                   jax.ShapeDtypeStruct((8, 128), F32)],
        scratch_shapes=[pltpu.VMEM((NST, 2048), F32), pltpu.VMEM((LCH, 2048), F32), pltpu.VMEM((LCH, 2048), F32)],
        input_output_aliases={11: 1}, compiler_params=_cp("arbitrary"),
    )(xc, xc, xc, proj, hsave, dy, dtb_row, alog_row, dx_row, expm, tri, dproj)


def ssd_out_fwd(y, proj, nw, name):
    t = y.shape[0]
    tm = min(256, t)

    def body(y_ref, z_ref, nw_ref, o_ref):
        for g in range(NG):
            gc = slice(g * 512, (g + 1) * 512)
            z = z_ref[:, gc].astype(F32)
            yz = y_ref[:, gc] * (z * _sigmoid(z))
            r = lax.rsqrt(jnp.mean(yz * yz, axis=-1, keepdims=True) + EPS)
            o_ref[:, gc] = (yz * r * nw_ref[:, gc]).astype(BF16)

    return pl.pallas_call(body, name=name, grid=(t // tm,),
                          in_specs=[_row(tm, 2048), _row(tm, 2048, P_Z // 2048), _vec(2048)],
                          out_specs=_row(tm, 2048), out_shape=jax.ShapeDtypeStruct((t, 2048), BF16),
                          compiler_params=_cp("parallel"))(y, proj, nw)


def ssd_out_bwd(y, proj, dya, nw, dproj, name):
    t = y.shape[0]
    tm = min(256, t)

    def body(y_ref, z_ref, d_ref, nw_ref, _, dy_ref, dz_ref, st_ref):
        @pl.when(pl.program_id(0) == 0)
        def _():
            st_ref[...] = jnp.zeros_like(st_ref)

        for g in range(NG):
            gc = slice(g * 512, (g + 1) * 512)
            z = z_ref[:, gc].astype(F32)
            yv = y_ref[:, gc]
            s = _sigmoid(z)
            sz = z * s
            yz = yv * sz
            r = lax.rsqrt(jnp.mean(yz * yz, axis=-1, keepdims=True) + EPS)
            yzn = yz * r
            dv = d_ref[:, gc].astype(F32)
            st_ref[0:1, gc] += jnp.sum(dv * yzn, axis=0, keepdims=True)
            dyn = dv * nw_ref[:, gc]
            dyz = r * (dyn - yzn * jnp.mean(dyn * yzn, axis=-1, keepdims=True))
            dy_ref[:, gc] = dyz * sz
            dz_ref[:, gc] = (dyz * yv * (s * (1.0 + z * (1.0 - s)))).astype(BF16)

    return pl.pallas_call(
        body, name=name, grid=(t // tm,),
        in_specs=[_row(tm, 2048), _row(tm, 2048, P_Z // 2048), _row(tm, 2048), _vec(2048), _HBM],
        out_specs=[_row(tm, 2048), _row(tm, 2048, P_Z // 2048), _stats(2048)],
        out_shape=[jax.ShapeDtypeStruct((t, 2048), F32), jax.ShapeDtypeStruct(dproj.shape, BF16),
                   jax.ShapeDtypeStruct((8, 2048), F32)],
        input_output_aliases={4: 1}, compiler_params=_cp("arbitrary"))(y, proj, dya, nw, dproj)


def _cstep(ar, ai, sr, si, br, bi):
    return ar * sr - ai * si + br, ar * si + ai * sr + bi


def _halves(v):
    return (v[0:8, 0:512], v[0:8, 512:1024]), (v[8:16, 0:512], v[8:16, 512:1024])


def _slab(r1, i1, r2, i2):
    return jnp.concatenate([jnp.concatenate([r1, i1], axis=1), jnp.concatenate([r2, i2], axis=1)], axis=0).astype(BF16)


def _local_ends(x_ref, nslab, ar, ai, sr_scr, si_scr, end_ref, first, last, reverse):
    @pl.when(first)
    def _():
        sr_scr[...] = jnp.zeros_like(sr_scr)
        si_scr[...] = jnp.zeros_like(si_scr)

    def step(k, carry):
        s1, s2 = _halves(x_ref[nslab - 1 - k if reverse else k].astype(F32))
        if reverse:
            s1, s2 = s2, s1
        return _cstep(ar, ai, *_cstep(ar, ai, carry[0], carry[1], *s1), *s2)

    sr, si = lax.fori_loop(0, nslab, step, (sr_scr[...], si_scr[...]), unroll=4)
    sr_scr[...] = sr
    si_scr[...] = si

    @pl.when(last)
    def _():
        end_ref[:, 0:512] = sr
        end_ref[:, 512:1024] = si


def s5_in(u, bsg, a_re, a_im, name):
    t = u.shape[0]
    tm = min(512, t)
    nt = t // tm

    def body(u_ref, b_ref, ar_ref, ai_ref, o_ref, e_ref, sr_scr, si_scr):
        i = pl.program_id(1)
        o_ref[...] = _dot(u_ref[...].astype(BF16), b_ref[...]).astype(BF16).reshape(tm // 16, 16, 1024)
        _local_ends(o_ref, tm // 16, ar_ref[...], ai_ref[...], sr_scr, si_scr, e_ref, i == 0, i == nt - 1, False)

    return pl.pallas_call(
        body, name=name, grid=(8, nt),
        in_specs=[pl.BlockSpec((tm, 128), lambda s, i: (i, s)), pl.BlockSpec((None, 128, 1024), lambda s, i: (s, 0, 0)),
                  pl.BlockSpec((None, 8, 512), lambda s, i: (s, 0, 0)), pl.BlockSpec((None, 8, 512), lambda s, i: (s, 0, 0))],
        out_specs=[pl.BlockSpec((tm // 16, 16, 1024), lambda s, i: (i, 0, s)),
                   pl.BlockSpec((None, 8, 1024), lambda s, i: (s, 0, 0))],
        out_shape=[jax.ShapeDtypeStruct((t // 16, 16, S5NS), BF16), jax.ShapeDtypeStruct((8, 8, 1024), F32)],
        scratch_shapes=[pltpu.VMEM((8, 512), F32), pltpu.VMEM((8, 512), F32)],
        compiler_params=_cp("parallel", "arbitrary"))(u, bsg, a_re, a_im)


def s5_out_bwd(dy, csg, s, a_re, a_im, name):
    t = dy.shape[0]
    tm = min(512, t)
    nt = t // tm

    def body(dy_ref, c_ref, s_ref, ar_ref, ai_ref, e_ref, dc_ref, end_ref, sr_scr, si_scr):
        i = pl.program_id(1)

        @pl.when(i == 0)
        def _():
            dc_ref[...] = jnp.zeros_like(dc_ref)

        dyb = dy_ref[...].astype(BF16)
        e_ref[...] = _nt(dyb, c_ref[...]).astype(BF16).reshape(tm // 16, 16, 1024)
        dc_ref[...] += lax.dot_general(s_ref[...], dyb, (((0,), (0,)), ((), ())), preferred_element_type=F32)
        _local_ends(e_ref, tm // 16, ar_ref[...], -ai_ref[...], sr_scr, si_scr, end_ref, i == 0, i == nt - 1, True)

    rv = lambda i: nt - 1 - i
    return pl.pallas_call(
        body, name=name, grid=(8, nt),
        in_specs=[pl.BlockSpec((tm, 128), lambda s, i: (rv(i), s)), pl.BlockSpec((None, 1024, 128), lambda s, i: (s, 0, 0)),
                  pl.BlockSpec((tm, 1024), lambda s, i: (rv(i), s)),
                  pl.BlockSpec((None, 8, 512), lambda s, i: (s, 0, 0)), pl.BlockSpec((None, 8, 512), lambda s, i: (s, 0, 0))],
        out_specs=[pl.BlockSpec((tm // 16, 16, 1024), lambda s, i: (rv(i), 0, s)),
                   pl.BlockSpec((None, 1024, 128), lambda s, i: (s, 0, 0)),
                   pl.BlockSpec((None, 8, 1024), lambda s, i: (s, 0, 0))],
        out_shape=[jax.ShapeDtypeStruct((t // 16, 16, S5NS), BF16), jax.ShapeDtypeStruct((8, 1024, 128), F32),
                   jax.ShapeDtypeStruct((8, 8, 1024), F32)],
        scratch_shapes=[pltpu.VMEM((8, 512), F32), pltpu.VMEM((8, 512), F32)],
        compiler_params=_cp("parallel", "arbitrary"))(dy, csg, s, a_re, a_im)


def s5_scan_init(ends, a_re, a_im, lseg, reverse, name):
    nsq = int(math.log2(lseg))
    assert 2 ** nsq == lseg
    sgn = -1.0 if reverse else 1.0
    order = list(range(7, -1, -1)) if reverse else list(range(8))

    def body(e_ref, ar_ref, ai_ref, o_ref):
        pr = ar_ref[0:1, :]
        pi = sgn * ai_ref[0:1, :]
        for _ in range(nsq):
            pr, pi = pr * pr - pi * pi, 2.0 * pr * pi
        prev_r = jnp.zeros((1, 512), F32)
        prev_i = jnp.zeros((1, 512), F32)
        j0 = order[0]
        o_ref[j0:j0 + 1, 0:512] = prev_r
        o_ref[j0:j0 + 1, 512:1024] = prev_i
        for idx in range(1, 8):
            j, jp = order[idx], order[idx - 1]
            prev_r, prev_i = _cstep(pr, pi, prev_r, prev_i, e_ref[jp:jp + 1, 0:512], e_ref[jp:jp + 1, 512:1024])
            o_ref[j:j + 1, 0:512] = prev_r
            o_ref[j:j + 1, 512:1024] = prev_i

    return pl.pallas_call(
        body, name=name, grid=(8,),
        in_specs=[pl.BlockSpec((None, 8, 1024), lambda s: (s, 0, 0)), pl.BlockSpec((None, 8, 512), lambda s: (s, 0, 0)),
                  pl.BlockSpec((None, 8, 512), lambda s: (s, 0, 0))],
        out_specs=pl.BlockSpec((None, 8, 1024), lambda s: (s, 0, 0)),
        out_shape=jax.ShapeDtypeStruct((8, 8, 1024), F32), compiler_params=_cp("parallel"))(ends, a_re, a_im)


def s5_scan_fwd(b3, init, a_re, a_im, csg, u, d_row, name):
    nslab = b3.shape[0]
    ti = min(64, nslab)
    nb = nslab // ti
    rows = 16 * ti

    def body(b_ref, i_ref, ar_ref, ai_ref, c_ref, u_ref, d_ref, o_ref, y_ref, sr_scr, si_scr):
        @pl.when(pl.program_id(1) == 0)
        def _():
            sr_scr[...] = i_ref[:, 0:512]
            si_scr[...] = i_ref[:, 512:1024]

        ar = ar_ref[...]
        ai = ai_ref[...]

        def step(k, carry):
            b1, b2 = _halves(b_ref[k].astype(F32))
            r1, i1 = _cstep(ar, ai, carry[0], carry[1], *b1)
            r2, i2 = _cstep(ar, ai, r1, i1, *b2)
            o_ref[k] = _slab(r1, i1, r2, i2)
            return r2, i2

        sr, si = lax.fori_loop(0, ti, step, (sr_scr[...], si_scr[...]), unroll=4)
        sr_scr[...] = sr
        si_scr[...] = si
        y_ref[...] = _dot(o_ref[...].reshape(rows, 1024), c_ref[...]) + d_ref[...] * u_ref[...].astype(F32)

    return pl.pallas_call(
        body, name=name, grid=(8, nb),
        in_specs=[pl.BlockSpec((ti, 16, 1024), lambda s, tb: (tb, 0, s)), pl.BlockSpec((None, 8, 1024), lambda s, tb: (s, 0, 0)),
                  pl.BlockSpec((None, 8, 512), lambda s, tb: (s, 0, 0)), pl.BlockSpec((None, 8, 512), lambda s, tb: (s, 0, 0)),
                  pl.BlockSpec((None, 1024, 128), lambda s, tb: (s, 0, 0)), pl.BlockSpec((rows, 128), lambda s, tb: (tb, s)),
                  pl.BlockSpec((1, 128), lambda s, tb: (0, s))],
        out_specs=[pl.BlockSpec((ti, 16, 1024), lambda s, tb: (tb, 0, s)), pl.BlockSpec((rows, 128), lambda s, tb: (tb, s))],
        out_shape=[jax.ShapeDtypeStruct(b3.shape, BF16), jax.ShapeDtypeStruct((16 * nslab, S5W), F32)],
        scratch_shapes=[pltpu.VMEM((8, 512), F32), pltpu.VMEM((8, 512), F32)],
        compiler_params=_cp("parallel", "arbitrary"))(b3, init, a_re, a_im, csg, u, d_row)


def s5_scan_bwd(e3, linit, s3, sinit, a_re, a_im, bsg, u, dy, d_row, name):
    nslab = e3.shape[0]
    ti = min(64, nslab)
    nb = nslab // ti
    rows = 16 * ti

    def body(e_ref, li_ref, s_ref, sh_ref, si0_ref, ar_ref, ai_ref, b_ref, u_ref, dy_ref, d_ref,
             du_ref, db_ref, dd_ref, da_ref, o_ref, lr_scr, lim_scr):
        tb = pl.program_id(1)

        @pl.when(tb == 0)
        def _():
            lr_scr[...] = li_ref[:, 0:512]
            lim_scr[...] = li_ref[:, 512:1024]
            da_ref[...] = jnp.zeros_like(da_ref)
            db_ref[...] = jnp.zeros_like(db_ref)
            dd_ref[...] = jnp.zeros_like(dd_ref)

        ar = ar_ref[...]
        ai = -ai_ref[...]

        def slab(kk, lr, li, dar, dai, sp):
            e1, e2 = _halves(e_ref[kk].astype(F32))
            s1, _ = _halves(s_ref[kk].astype(F32))
            r2, i2 = _cstep(ar, ai, lr, li, *e2)
            dar = dar + r2 * s1[0] + i2 * s1[1]
            dai = dai + i2 * s1[0] - r2 * s1[1]
            r1, i1 = _cstep(ar, ai, r2, i2, *e1)
            dar = dar + r1 * sp[0] + i1 * sp[1]
            dai = dai + i1 * sp[0] - r1 * sp[1]
            o_ref[kk] = _slab(r1, i1, r2, i2)
            return r1, i1, dar, dai

        def step(k, carry):
            kk = ti - 1 - k
            return slab(kk, *carry, _halves(s_ref[kk - 1].astype(F32))[1])

        z = jnp.zeros((8, 512), F32)
        lr, li, dar, dai = lax.fori_loop(0, ti - 1, step, (lr_scr[...], lim_scr[...], z, z), unroll=2)
        first = tb == nb - 1
        halo = _halves(sh_ref[0].astype(F32))[1]
        sp = (jnp.where(first, si0_ref[:, 0:512], halo[0]), jnp.where(first, si0_ref[:, 512:1024], halo[1]))
        lr, li, dar, dai = slab(0, lr, li, dar, dai, sp)
        lr_scr[...] = lr
        lim_scr[...] = li
        da_ref[:, 0:512] += dar
        da_ref[:, 512:1024] += dai
        lb = o_ref[...].reshape(rows, 1024)
        uv = u_ref[...].astype(F32)
        dyv = dy_ref[...]
        du_ref[...] = _nt(lb, b_ref[...]) + d_ref[...] * dyv
        db_ref[...] += lax.dot_general(uv.astype(BF16), lb, (((0,), (0,)), ((), ())), preferred_element_type=F32)
        dd_ref[...] += jnp.sum(dyv * uv, axis=0, keepdims=True)

    rb = lambda tb: nb - 1 - tb
    return pl.pallas_call(
        body, name=name, grid=(8, nb),
        in_specs=[pl.BlockSpec((ti, 16, 1024), lambda s, tb: (rb(tb), 0, s)),
                  pl.BlockSpec((None, 8, 1024), lambda s, tb: (s, 0, 0)),
                  pl.BlockSpec((ti, 16, 1024), lambda s, tb: (rb(tb), 0, s)),
                  pl.BlockSpec((1, 16, 1024), lambda s, tb: (jnp.maximum(rb(tb) * ti - 1, 0), 0, s)),
                  pl.BlockSpec((None, 8, 1024), lambda s, tb: (s, 0, 0)),
                  pl.BlockSpec((None, 8, 512), lambda s, tb: (s, 0, 0)), pl.BlockSpec((None, 8, 512), lambda s, tb: (s, 0, 0)),
                  pl.BlockSpec((None, 128, 1024), lambda s, tb: (s, 0, 0)),
                  pl.BlockSpec((rows, 128), lambda s, tb: (rb(tb), s)), pl.BlockSpec((rows, 128), lambda s, tb: (rb(tb), s)),
                  pl.BlockSpec((1, 128), lambda s, tb: (0, s))],
        out_specs=[pl.BlockSpec((rows, 128), lambda s, tb: (rb(tb), s)),
                   pl.BlockSpec((None, 128, 1024), lambda s, tb: (s, 0, 0)), pl.BlockSpec((1, 128), lambda s, tb: (0, s)),
                   pl.BlockSpec((None, 8, 1024), lambda s, tb: (s, 0, 0))],
        out_shape=[jax.ShapeDtypeStruct((16 * nslab, S5W), F32), jax.ShapeDtypeStruct((8, 128, 1024), F32),
                   jax.ShapeDtypeStruct((1, S5W), F32), jax.ShapeDtypeStruct((8, 8, 1024), F32)],
        scratch_shapes=[pltpu.VMEM((ti, 16, 1024), BF16), pltpu.VMEM((8, 512), F32), pltpu.VMEM((8, 512), F32)],
        compiler_params=_cp("parallel", "arbitrary"))(e3, linit, s3, s3, sinit, a_re, a_im, bsg, u, dy, d_row)


_GC = math.sqrt(2.0 / math.pi)


def gelu_fwd(y, name):
    t, w = y.shape
    tm = min(512, t)

    def body(y_ref, o_ref):
        v = y_ref[...]
        o_ref[...] = (0.5 * v * (1.0 + jnp.tanh(_GC * (v + 0.044715 * v * v * v)))).astype(BF16)

    return pl.pallas_call(body, name=name, grid=(t // tm,), in_specs=[_row(tm, w)], out_specs=_row(tm, w),
                          out_shape=jax.ShapeDtypeStruct((t, w), BF16), compiler_params=_cp("parallel"))(y)


def gelu_bwd(y, dg, name):
    t, w = y.shape
    tm = min(512, t)

    def body(y_ref, d_ref, o_ref):
        v = y_ref[...]
        th = jnp.tanh(_GC * (v + 0.044715 * v * v * v))
        o_ref[...] = d_ref[...].astype(F32) * (0.5 * (1.0 + th) + 0.5 * v * (1.0 - th * th) * _GC * (1.0 + 3.0 * 0.044715 * v * v))

    return pl.pallas_call(body, name=name, grid=(t // tm,), in_specs=[_row(tm, w), _row(tm, w)], out_specs=_row(tm, w),
                          out_shape=jax.ShapeDtypeStruct((t, w), F32), compiler_params=_cp("parallel"))(y, dg)


def merge_fwd(proj, pa, glu, name):
    t = pa.shape[0]
    tm = min(256, t)

    def body(g_ref, pa_ref, glu_ref, o_ref):
        pb = glu_ref[:, 0:D].astype(F32) * _sigmoid(glu_ref[:, D:2 * D].astype(F32))
        o_ref[...] = (_sigmoid(g_ref[:, 0:D].astype(F32)) * pa_ref[...].astype(F32)
                      + _sigmoid(g_ref[:, D:2 * D].astype(F32)) * pb).astype(BF16)

    return pl.pallas_call(body, name=name, grid=(t // tm,), in_specs=[_row(tm, 2 * D), _row(tm, D), _row(tm, 2 * D)],
                          out_specs=_row(tm, D), out_shape=jax.ShapeDtypeStruct((t, D), BF16),
                          compiler_params=_cp("parallel"))(proj, pa, glu)


def merge_bwd(proj, pa, glu, dm, dproj, name):
    t = pa.shape[0]
    tm = min(256, t)

    def body(g_ref, pa_ref, glu_ref, dm_ref, _, dpa_ref, dglu_ref, dg_ref):
        dmv = dm_ref[...].astype(F32)
        pav = pa_ref[...].astype(F32)
        sa = _sigmoid(g_ref[:, 0:D].astype(F32))
        sb = _sigmoid(g_ref[:, D:2 * D].astype(F32))
        ga = glu_ref[:, 0:D].astype(F32)
        sg = _sigmoid(glu_ref[:, D:2 * D].astype(F32))
        pb = ga * sg
        dpb = sb * dmv
        dpa_ref[...] = (sa * dmv).astype(BF16)
        dglu_ref[:, 0:D] = (dpb * sg).astype(BF16)
        dglu_ref[:, D:2 * D] = (dpb * pb * (1.0 - sg)).astype(BF16)
        dg_ref[:, 0:D] = (dmv * pav * sa * (1.0 - sa)).astype(BF16)
        dg_ref[:, D:2 * D] = (dmv * pb * sb * (1.0 - sb)).astype(BF16)

    return pl.pallas_call(
        body, name=name, grid=(t // tm,),
        in_specs=[_row(tm, 2 * D), _row(tm, D), _row(tm, 2 * D), _row(tm, D), _HBM],
        out_specs=[_row(tm, D), _row(tm, 2 * D), _row(tm, 2 * D, P_GATES // (2 * D))],
        out_shape=[jax.ShapeDtypeStruct((t, D), BF16), jax.ShapeDtypeStruct((t, 2 * D), BF16),
                   jax.ShapeDtypeStruct(dproj.shape, BF16)],
        input_output_aliases={4: 2}, compiler_params=_cp("parallel"))(proj, pa, glu, dm, dproj)


def adamw(w, parts, m, v, name):
    r, c = w.shape
    p = parts.shape[0]
    tr = r if r <= 128 else 128
    c1 = 1.0 - ADAM_B1 ** ADAM_STEP
    c2 = 1.0 - ADAM_B2 ** ADAM_STEP

    def body(w_ref, p_ref, m_ref, v_ref, g_ref, d_ref, nm_ref, nv_ref):
        g = p_ref[0].astype(F32)
        for k in range(1, p):
            g = g + p_ref[k].astype(F32)
        mn = ADAM_B1 * m_ref[...] + (1.0 - ADAM_B1) * g
        vn = ADAM_B2 * v_ref[...] + (1.0 - ADAM_B2) * (g * g)
        g_ref[...] = g
        nm_ref[...] = mn
        nv_ref[...] = vn
        d_ref[...] = -ADAM_LR * ((mn / c1) / (jnp.sqrt(vn / c2) + ADAM_EPS) + ADAM_WD * w_ref[...])

    spec = pl.BlockSpec((tr, c), lambda i: (i, 0))
    o = jax.ShapeDtypeStruct((r, c), F32)
    return pl.pallas_call(
        body, name=name, grid=(pl.cdiv(r, tr),),
        in_specs=[spec, pl.BlockSpec((p, tr, c), lambda i: (0, i, 0)), spec, spec],
        out_specs=[spec, spec, spec, spec], out_shape=[o, o, o, o], compiler_params=_cp("parallel"))(w, parts, m, v)


def _s5_discretise(lambda_re, lambda_im, log_dt, b_re, b_im):
    dt = jnp.exp(log_dt)[:, None]
    lr = jnp.minimum(lambda_re, -1e-4)
    li = lambda_im
    mag = jnp.exp(lr * dt)
    ar = mag * jnp.cos(li * dt)
    ai = mag * jnp.sin(li * dt)
    den = lr * lr + li * li
    nr = ar - 1.0
    kr = (nr * lr + ai * li) / den
    ki = (ai * lr - nr * li) / den
    bbar_re = kr[..., None] * b_re - ki[..., None] * b_im
    bbar_im = kr[..., None] * b_im + ki[..., None] * b_re
    return ar, ai, bbar_re, bbar_im


def _block_diag(v):
    a, b = v.shape[2], v.shape[3]
    eye = jnp.eye(8, dtype=v.dtype)[None, :, None, :, None]
    return (v[:, :, :, None, :] * eye).reshape(8, 8 * a, 8 * b)


def _diag_blocks(m, a, b):
    eye = jnp.eye(8, dtype=m.dtype)[None, :, None, :, None]
    return jnp.sum(m.reshape(8, 8, a, 8, b) * eye, axis=3)


def _bsg_of(bb_re, bb_im):
    f = lambda b: _block_diag(b.reshape(8, 8, 64, 16).transpose(0, 1, 3, 2))
    return jnp.concatenate([f(bb_re), f(bb_im)], axis=2)


def _bsg_diag(dbsg):
    f = lambda x: _diag_blocks(x, 16, 64).transpose(0, 1, 3, 2).reshape(64, 64, 16)
    return f(dbsg[:, :, 0:512]), f(dbsg[:, :, 512:1024])


def _csg_of(c_re, c_im):
    f = lambda c: _block_diag(c.reshape(8, 8, 16, 64).transpose(0, 1, 3, 2))
    return jnp.concatenate([f(c_re), -f(c_im)], axis=1)


def _csg_diag(dcsg):
    f = lambda x: _diag_blocks(x, 64, 16).transpose(0, 1, 3, 2).reshape(64, 16, 64)
    return f(dcsg[:, 0:512, :]), -f(dcsg[:, 512:1024, :])


def _perm(a, t):
    return a.reshape(8, t // 8, a.shape[1]).transpose(1, 0, 2).reshape(t, a.shape[1])


def _unperm(a, t):
    return a.reshape(t // 8, 8, a.shape[1]).transpose(1, 0, 2).reshape(t, a.shape[1])


def _cols(g):
    return g.transpose(1, 0, 2).reshape(g.shape[1], N_DEV * g.shape[2])


def _rows(g):
    return g.reshape(N_DEV * g.shape[1], g.shape[2])


def _col_parts(g):
    r, c = g.shape
    return g.reshape(r, N_DEV, c // N_DEV).transpose(1, 0, 2)


def _row_parts(g):
    r, c = g.shape
    return g.reshape(N_DEV, r // N_DEV, c)


FB, FBP = D_FF // N_DEV, D_FFP // N_DEV


def _pad_ffn_in_shard(w):
    return jnp.pad(w.reshape(D, 2, FB), ((0, 0), (0, 0), (0, FBP - FB))).reshape(D, 2 * FBP)


def _unpad_ffn_in_shard(g):
    return g.reshape(D, 2, FBP)[:, :, :FB].reshape(D, 2 * FB)


def _pad_ffn_out_shard(w):
    return jnp.pad(w, ((0, FBP - FB), (0, 0)))


def sum_parts(parts, name):
    p, r, c = parts.shape
    tr = 256

    def body(p_ref, o_ref):
        g = p_ref[0].astype(F32)
        for k in range(1, p):
            g = g + p_ref[k].astype(F32)
        o_ref[...] = g

    return pl.pallas_call(body, name=name, grid=(r // tr,), in_specs=[pl.BlockSpec((p, tr, c), lambda i: (0, i, 0))],
                          out_specs=pl.BlockSpec((tr, c), lambda i: (i, 0)),
                          out_shape=jax.ShapeDtypeStruct((r, c), F32), compiler_params=_cp("parallel"))(parts)


def _pad_w_in(w):
    z = jnp.zeros((D, INP - P_DT - NH), w.dtype)
    return jnp.concatenate([w[:, O_GA:O_GB], w[:, O_GB:IN_COLS], w[:, 0:O_XBC], w[:, O_XBC:O_DT], w[:, O_U:O_GA],
                            w[:, O_DT:O_U], z], axis=1)


def _unpad_w_in(g):
    return jnp.concatenate([g[:, P_Z:P_XBC], g[:, P_XBC:P_U], g[:, P_DT:P_DT + NH], g[:, P_U:P_DT],
                            g[:, 0:D], g[:, D:2 * D]], axis=1)


_PACK = (("b_ada_lo", 4096), ("norm_ffn1", 2048), ("pad_late", 2048), ("b_ada_hi", 14336), ("norm_mix", 2048),
         ("conv_b", 3072), ("dt_bias", 32), ("a_log", 32), ("d_ssd", 32), ("ssd_norm_w", 2048),
         ("s5_lambda_re", 4096), ("s5_lambda_im", 4096), ("s5_b_re", 65536), ("s5_b_im", 65536), ("s5_c_re", 65536),
         ("s5_c_im", 65536), ("s5_d", 1024), ("s5_log_dt", 64), ("norm_ffn2", 2048), ("norm_final", 2048), ("loss", 1))
_PACK_ROWS = 304
_PACK_W = 1024
_LATE_ROWS = 8
_B_LO = 4096


def _pack(d):
    d = dict(d)
    if "b_ada" in d:
        b = d.pop("b_ada").reshape(-1)
        d["b_ada_lo"], d["b_ada_hi"] = b[:_B_LO], b[_B_LO:]
    flat = jnp.concatenate([d[k].reshape(-1).astype(F32) if k in d else jnp.zeros((n,), F32) for k, n in _PACK])
    return jnp.pad(flat, (0, _PACK_ROWS * _PACK_W - flat.shape[0])).reshape(_PACK_ROWS, _PACK_W)


def _unpack(a):
    flat = a.reshape(-1)
    out, off = {}, 0
    for k, n in _PACK:
        out[k] = flat[off:off + n]
        off += n
    out["b_ada"] = jnp.concatenate([out["b_ada_lo"], out["b_ada_hi"]])
    return out


_TA = dict(tm=512, tn=512, tk=8192)


def _ffn_bwd(df, h, ab, act, w_in_shard_t, w_out_t, tag, extra=()):
    dab, (g_wt,) = ffn_dab(df, w_out_t, ab, tag + "_dab", comm=[("ag", w_in_shard_t)])
    w_in_t = _rows(g_wt)
    dw_out = mm(act, df, ta=True, out_dtype=BF16, i_outer=True, name=tag + "_dwout", **_TA)
    dw_in, (x_out, *x_extra) = mm(h, dab, ta=True, b_halves=True, out_dtype=BF16, i_outer=True, name=tag + "_dwin",
                                  comm=[("xc", _row_parts(dw_out)), *extra], **_TA)
    dh, (x_in,) = mm(dab, w_in_t, a_halves=True, out_dtype=BF16, tk=5632, name=tag + "_dh", comm=[("xcc", dw_in)])
    g_in = _unpad_ffn_in_shard(sum_parts(x_in, tag + "_dwin_sum"))
    return dh, g_in[None], x_out, x_extra


def kernel(x, c, w_ada, b_ada, norm_ffn1, w_ffn1_in, w_ffn1_out, norm_mix, w_in, conv_w, conv_b, dt_bias, a_log, d_ssd, ssd_norm_w, w_a_proj, s5_lambda_re, s5_lambda_im, s5_b_re, s5_b_im, s5_c_re, s5_c_im, s5_d, s5_log_dt, w_b_glu, w_out, norm_ffn2, w_ffn2_in, w_ffn2_out, norm_final, loss_target, m_w_ada, m_b_ada, m_norm_ffn1, m_w_ffn1_in, m_w_ffn1_out, m_norm_mix, m_w_in, m_conv_w, m_conv_b, m_dt_bias, m_a_log, m_d_ssd, m_ssd_norm_w, m_w_a_proj, m_s5_lambda_re, m_s5_lambda_im, m_s5_b_re, m_s5_b_im, m_s5_c_re, m_s5_c_im, m_s5_d, m_s5_log_dt, m_w_b_glu, m_w_out, m_norm_ffn2, m_w_ffn2_in, m_w_ffn2_out, m_norm_final, v_w_ada, v_b_ada, v_norm_ffn1, v_w_ffn1_in, v_w_ffn1_out, v_norm_mix, v_w_in, v_conv_w, v_conv_b, v_dt_bias, v_a_log, v_d_ssd, v_ssd_norm_w, v_w_a_proj, v_s5_lambda_re, v_s5_lambda_im, v_s5_b_re, v_s5_b_im, v_s5_c_re, v_s5_c_im, v_s5_d, v_s5_log_dt, v_w_b_glu, v_w_out, v_norm_ffn2, v_w_ffn2_in, v_w_ffn2_out, v_norm_final):
    args = dict(locals())
    t = x.shape[1]
    me = _my_id()
    xt = x[0]
    tgt = loss_target[0]
    small = ["b_ada", "norm_ffn1", "norm_mix", "conv_b", "dt_bias", "a_log", "d_ssd", "ssd_norm_w", "s5_lambda_re",
             "s5_lambda_im", "s5_b_re", "s5_b_im", "s5_c_re", "s5_c_im", "s5_d", "s5_log_dt", "norm_ffn2", "norm_final"]

    bf = lambda w: w[0].astype(BF16)
    ffn_in_shard = lambda w: _pad_ffn_in_shard(bf(w))
    ffn_out_shard = lambda w: _pad_ffn_out_shard(bf(w))

    c8 = all_gather(c, "ag_c").reshape(N_DEV, D)
    b_loc = lax.dynamic_slice(b_ada, (0, me * (N_ADA * D // N_DEV)), (1, N_ADA * D // N_DEV))
    m8 = ada_fwd(c8, w_ada[0], b_loc, "ada_fwd")
    mods = comm_call([("xc", m8.reshape(N_DEV, 1, -1))], "xc_mods")[0].reshape(1, N_ADA * D)

    h1, (wf1i, g_cw) = mod_fwd(xt, norm_ffn1, mods, 0, 1, name="mod1",
                               comm=[("agc", ffn_in_shard(w_ffn1_in)), ("ag", conv_w[0])])
    convw = _cols(g_cw)
    ab1, act1, (g_f1o, g_win, g_wap) = ffn_in_act(
        h1, wf1i, "ffn1_in", fwd=0.9,
        comm=[("ag", ffn_out_shard(w_ffn1_out)), ("ag", bf(w_in)), ("ag", bf(w_a_proj))])
    wf1o, winp, wap = _rows(g_f1o), _pad_w_in(_cols(g_win)), _rows(g_wap)
    f1, (g_wo, g_wbg) = mm(act1, wf1o, out_dtype=BF16, tk=5632, name="ffn1_out",
                           comm=[("ag", bf(w_out)), ("ag", bf(w_b_glu))])
    wo, wbg = _rows(g_wo), _cols(g_wbg)
    x1, h2 = mod_fwd(xt, norm_mix, mods, 3, 4, f=f1, gk=2, gscale=0.5, name="mod2")
    proj, (wf2i,) = mm(h2, winp, out_dtype=BF16, tm=1024, tn=512, i_outer=True, name="w_in",
                       comm=[("agc", ffn_in_shard(w_ffn2_in))])
    dtraw = mm(h2, winp[:, P_DT:P_DT + 128], tn=128, name="w_in_dt")
    cb_row = conv_b
    xc = conv_fwd(proj, convw, cb_row, "conv_fwd")
    row128 = lambda v: jnp.pad(v.reshape(1, -1), ((0, 0), (0, 128 - v.size)))
    dtb_row, alog_row = row128(dt_bias), row128(a_log)
    dx_row = jnp.repeat(d_ssd.reshape(-1), HP).reshape(1, 2048)
    rows = jnp.arange(128)[:, None]
    expm = ((rows % NH == jnp.arange(2048)[None, :] // HP) & (rows < 3 * NH)).astype(BF16)
    tri = (jnp.arange(LCH)[:, None] >= jnp.arange(LCH)[None, :]).astype(F32)
    y_ssd, hsave = ssd_fwd(xc, dtraw, dtb_row, alog_row, dx_row, expm, tri, "ssd_fwd")
    ya = ssd_out_fwd(y_ssd, proj, ssd_norm_w, "ssd_out")
    pa = mm(ya, wap, out_dtype=BF16, name="w_a_proj")

    s5p = (s5_lambda_re[0], s5_lambda_im[0], s5_log_dt[0], s5_b_re[0], s5_b_im[0])
    (ar, ai, bb_re, bb_im), s5_vjp = jax.vjp(_s5_discretise, *s5p)
    a_re8 = jnp.broadcast_to(ar.reshape(8, 1, 512), (8, 8, 512))
    a_im8 = jnp.broadcast_to(ai.reshape(8, 1, 512), (8, 8, 512))
    bsg = _bsg_of(bb_re, bb_im).astype(BF16)
    csg = _csg_of(s5_c_re[0], s5_c_im[0]).astype(BF16)
    d_row = s5_d.reshape(1, S5W)
    lseg = t // 8
    u_p = _perm(proj[:, P_U:P_U + S5W], t)
    bu3, ends_f = s5_in(u_p, bsg, a_re8, a_im8, "s5_in")
    sinit = s5_scan_init(ends_f, a_re8, a_im8, lseg, False, "s5_init_f")
    s3, yb_p = s5_scan_fwd(bu3, sinit, a_re8, a_im8, csg, u_p, d_row, "s5_scan_f")
    s2 = s3.reshape(t, S5NS)
    yb = _unperm(yb_p, t)
    gy = gelu_fwd(yb, "gelu")
    glu = mm(gy, wbg, out_dtype=BF16, name="w_b_glu")
    merged = merge_fwd(proj, pa, glu, "merge")
    o = mm(merged, wo, out_dtype=BF16, name="w_out")
    x2, h3 = mod_fwd(x1, norm_ffn2, mods, 6, 7, f=o, gk=5, gscale=1.0, name="mod3")
    ab3, act3, (g_f2o,) = ffn_in_act(h3, wf2i, "ffn2_in", comm=[("ag", ffn_out_shard(w_ffn2_out))])
    wf2o = _rows(g_f2o)
    f3 = mm(act3, wf2o, out_dtype=BF16, tk=5632, name="ffn2_out")

    dx3, df3, st_fin = final_fwd_bwd(x2, f3, mods, norm_final.reshape(1, D), tgt, "final")
    dh3, x_f2i, x_f2o, _ = _ffn_bwd(df3, h3, ab3, act3, ffn_in_shard(w_ffn2_in).T, wf2o.T, "ffn2")
    dx2, do, st3 = mod_bwd(x2, dh3, dx3, norm_ffn2, mods, 7, fprev=o, gk=5, gscale=1.0, name="mod3_bwd")

    dmerged = mm(do, wo.T, out_dtype=BF16, name="w_out_dx")
    dwo = mm(merged, do, ta=True, out_dtype=BF16, i_outer=True, name="w_out_dw", **_TA)
    dpa, dglu, dproj = merge_bwd(proj, pa, glu, dmerged, lax.empty((t, INP), BF16), "merge_bwd")
    dwbg = mm(gy, dglu, ta=True, out_dtype=BF16, i_outer=True, name="w_b_glu_dw", **_TA)
    dgy, (x_wo,) = mm(dglu, wbg.T, out_dtype=BF16, name="w_b_glu_dx", comm=[("xc", _row_parts(dwo))])
    dyb_p = _perm(gelu_bwd(yb, dgy, "gelu_bwd"), t)
    e3, dcsg, ends_b = s5_out_bwd(dyb_p, csg, s2, a_re8, a_im8, "s5_out_bwd")
    linit = s5_scan_init(ends_b, a_re8, a_im8, lseg, True, "s5_init_b")
    du_p, dbsg, dd_row, da8 = s5_scan_bwd(e3, linit, s3, sinit, a_re8, a_im8, bsg, u_p, dyb_p, d_row, "s5_scan_b")
    du = _unperm(du_p, t).astype(BF16)
    da = jnp.sum(da8, axis=1)
    dbb_re, dbb_im = _bsg_diag(dbsg)
    g_lre, g_lim, g_ldt, g_bre, g_bim = s5_vjp((da[:, 0:512].reshape(64, 64), da[:, 512:1024].reshape(64, 64),
                                                dbb_re, dbb_im))
    g_cre, g_cim = _csg_diag(dcsg)

    dwap = mm(ya, dpa, ta=True, out_dtype=BF16, i_outer=True, name="w_a_proj_dw", **_TA)
    dya, (x_wbg,) = mm(dpa, wap.T, out_dtype=BF16, name="w_a_proj_dx", comm=[("xc", _col_parts(dwbg))])
    dy_ssd, dproj, st_sn = ssd_out_bwd(y_ssd, proj, dya, ssd_norm_w, dproj, "ssd_out_bwd")
    dxc, dproj, st_ssd = ssd_bwd(xc, dtraw, hsave, dy_ssd, dtb_row, alog_row, dx_row, expm, tri, dproj, "ssd_bwd")
    dpre, st_cv = conv_bwd_pre(proj, dxc, convw, cb_row, "conv_bwd_pre")
    dproj = conv_bwd_in(dpre, convw, dproj, "conv_bwd_in")
    dproj = lax.dynamic_update_slice(dproj, du, (0, P_U))
    dwinp, (x_wap, x_cw) = mm(h2, dproj, ta=True, out_dtype=BF16, i_outer=True, name="w_in_dw",
                              comm=[("xc", _row_parts(dwap)), ("xc", _col_parts(st_cv[0:CONV_K]))], **_TA)
    dh2, (x_win,) = mm(dproj, winp.T, out_dtype=BF16, tk=5376, name="w_in_dx",
                       comm=[("xc", _col_parts(_unpad_w_in(dwinp)))])
    dx1, df1, st2 = mod_bwd(x1, dh2, dx2, norm_mix, mods, 4, fprev=f1, gk=2, gscale=0.5, name="mod2_bwd")
    part = {"b_ada_hi": jnp.concatenate([st2[3], st2[0], st2[1], st3[3], st3[0], st3[1], st_fin[1]]),
            "norm_mix": st2[2], "conv_b": st_cv[4], "dt_bias": st_ssd[1, 0:NH],
            "a_log": st_ssd[0, 0:NH], "d_ssd": st_ssd[2, 0:NH], "ssd_norm_w": st_sn[0], "s5_lambda_re": g_lre,
            "s5_lambda_im": g_lim, "s5_b_re": g_bre, "s5_b_im": g_bim, "s5_c_re": g_cre, "s5_c_im": g_cim,
            "s5_d": dd_row, "s5_log_dt": g_ldt, "norm_ffn2": st3[2], "norm_final": st_fin[0],
            "loss": (0.5 / D) * jnp.sum(st_fin[2])}
    dh1, x_f1i, x_f1o, (gath_early,) = _ffn_bwd(df1, h1, ab1, act1, ffn_in_shard(w_ffn1_in).T, wf1o.T, "ffn1",
                                                extra=[("ag", _pack(part)[_LATE_ROWS:])])
    gx, st1 = mod_bwd(xt, dh1, dx1, norm_ffn1, mods, 1, name="mod1_bwd")

    late = _pack({"b_ada_lo": jnp.concatenate([st1[0], st1[1]]), "norm_ffn1": st1[2]})[:_LATE_ROWS]
    gath = jnp.concatenate([all_gather(late, "ag_small"), gath_early], axis=1)
    sg, sd, sm, sv = adamw(_pack({k: args[k] for k in small}), gath, _pack({k: args["m_" + k] for k in small}),
                           _pack({k: args["v_" + k] for k in small}), "adamw_small")
    sg, sd, sm, sv = _unpack(sg), _unpack(sd), _unpack(sm), _unpack(sv)
    loss = sg["loss"][0]

    gflat = gath.reshape(N_DEV, -1)
    hi0 = _LATE_ROWS * _PACK_W
    dm8 = jnp.concatenate([gflat[:, 0:_B_LO], gflat[:, hi0:hi0 + N_ADA * D - _B_LO]], axis=1)
    dm_loc = lax.dynamic_slice(dm8, (0, me * (N_ADA * D // N_DEV)), (N_DEV, N_ADA * D // N_DEV))
    g_ada = ada_bwd(c8.T, dm_loc, "ada_bwd")
    big = {"w_ada": g_ada[None], "w_ffn1_in": x_f1i, "w_ffn1_out": x_f1o, "w_in": x_win, "conv_w": x_cw,
           "w_a_proj": x_wap, "w_b_glu": x_wbg, "w_out": x_wo, "w_ffn2_in": x_f2i, "w_ffn2_out": x_f2o}
    res = {}
    for k, parts in big.items():
        res[k] = adamw(args[k][0], parts, args["m_" + k][0], args["v_" + k][0], "adamw_" + k)

    names = ["w_ada", "b_ada", "norm_ffn1", "w_ffn1_in", "w_ffn1_out", "norm_mix", "w_in", "conv_w", "conv_b", "dt_bias",
             "a_log", "d_ssd", "ssd_norm_w", "w_a_proj", "s5_lambda_re", "s5_lambda_im", "s5_b_re", "s5_b_im", "s5_c_re",
             "s5_c_im", "s5_d", "s5_log_dt", "w_b_glu", "w_out", "norm_ffn2", "w_ffn2_in", "w_ffn2_out", "norm_final"]
    outs = [loss, gx[None]]
    for q, src in enumerate((sg, sd, sm, sv)):
        for k in names:
            if k in res:
                outs.append(res[k][q][None])
            else:
                outs.append(src[k].reshape(args[k].shape))
    return tuple(outs)
```

```python
import functools
import math

import jax
import jax.numpy as jnp
from jax import lax
from jax.experimental import pallas as pl
from jax.experimental.pallas import tpu as pltpu

F32 = jnp.float32
BF16 = jnp.bfloat16
HI = lax.Precision.HIGHEST

N_DEV = 8
D = 2048
D_FF = 5504
D_FFP = 5632
NH = 32
HP = 64
NG = 4
NST = 128
LCH = 128
CONV_DIM = 3072
CONV_K = 4
S5W = 1024
S5NS = 8192
N_ADA = 9
EPS = 1e-6
IN_COLS = 10272
INP = 10752
P_GATES, P_Z, P_XBC, P_U, P_DT = 0, 4096, 6144, 9216, 10240
O_XBC, O_DT, O_U, O_GA, O_GB = 2048, 5120, 5152, 6176, 8224
NEG = -1e30
VMEM_LIMIT = 56 * 1024 * 1024

ADAM_LR, ADAM_B1, ADAM_B2, ADAM_EPS, ADAM_WD, ADAM_STEP = 0.001, 0.9, 0.999, 1e-08, 0.01, 10


def _cp(*sem):
    return pltpu.CompilerParams(dimension_semantics=sem, vmem_limit_bytes=VMEM_LIMIT)


def _tile(dim, pref):
    if dim <= pref or dim % pref == 0:
        return min(dim, pref)
    for t in (2048, 1024, 512, 256, 128):
        if t <= pref and dim % t == 0:
            return t
    return dim


def _vec(w, cb=0):
    return pl.BlockSpec((1, w), lambda *_: (0, cb))


def _row(tm, w, cb=0):
    return pl.BlockSpec((tm, w), lambda i: (i, cb))


def _stats(w):
    return pl.BlockSpec((8, w), lambda *_: (0, 0))


_HBM = pl.BlockSpec(memory_space=pl.ANY)


def _sigmoid(x):
    return 1.0 / (1.0 + jnp.exp(-x))


def _softplus(x):
    return jnp.maximum(x, 0.0) + jnp.log1p(jnp.exp(-jnp.abs(x)))


def _peer(k):
    x, y, c = lax.axis_index("x"), lax.axis_index("y"), lax.axis_index("c")
    return (x ^ ((k >> 2) & 1), y ^ ((k >> 1) & 1), c ^ (k & 1))


def _my_id():
    return 4 * lax.axis_index("x") + 2 * lax.axis_index("y") + lax.axis_index("c")


def _comm_out_shape(kind, v):
    shape = {"ag": (N_DEV,) + v.shape, "xc": v.shape, "agc": (v.shape[0], N_DEV * v.shape[1]),
             "xcc": (N_DEV, v.shape[0], v.shape[1] // N_DEV)}[kind]
    return jax.ShapeDtypeStruct(shape, v.dtype)


def _comm_scratch(n):
    return [pltpu.SemaphoreType.DMA((n * N_DEV,)), pltpu.SemaphoreType.DMA((n * N_DEV,))]


class _Comm:
    def __init__(self, kinds, srcs, dsts, send_sems, recv_sems):
        self.items = list(zip(kinds, srcs, dsts))
        self.send_sems, self.recv_sems = send_sems, recv_sems
        x, y, c = lax.axis_index("x"), lax.axis_index("y"), lax.axis_index("c")
        self.me = 4 * x + 2 * y + c
        self.sib = (x, y, 1 - c)
        self.chips = [(1 - x, y), (x, 1 - y), (1 - x, 1 - y)]
        self.c = c

    @staticmethod
    def _id(p):
        return 4 * p[0] + 2 * p[1] + p[2]

    def _src(self, q, d):
        kind, src, _ = self.items[q]
        if kind == "xc":
            return src.at[d]
        if kind == "xcc":
            w = src.shape[1] // N_DEV
            return src.at[:, pl.ds(pl.multiple_of(d * w, 128), w)]
        return src

    def _slot(self, q, d):
        kind, _, dst = self.items[q]
        if kind == "agc":
            w = dst.shape[1] // N_DEV
            return dst.at[:, pl.ds(pl.multiple_of(d * w, 128), w)]
        return dst.at[d]

    def _push(self, q, k, src, slot, to):
        return pltpu.make_async_remote_copy(
            src_ref=src, dst_ref=self._slot(q, slot), send_sem=self.send_sems.at[q * N_DEV + k],
            recv_sem=self.recv_sems.at[q * N_DEV + k], device_id=to, device_id_type=pl.DeviceIdType.MESH)

    def _local(self, q):
        return pltpu.make_async_copy(self._src(q, self.me), self._slot(q, self.me), self.send_sems.at[q * N_DEV])

    def _direct(self, q):
        kind = self.items[q][0]
        if kind in ("xc", "xcc"):
            out = []
            for k in range(1, N_DEV):
                p = _peer(k)
                out.append((k, self._push(q, k, self._src(q, self._id(p)), self.me, p)))
            return out
        src = self.items[q][1]
        out = [(1, self._push(q, 1, src, self.me, self.sib))]
        for j, chip in enumerate(self.chips):
            out.append((2 + j, self._push(q, 2 + j, src, self.me, (*chip, self.c))))
        return out

    def _forwards(self, q):
        out = []
        for j, chip in enumerate(self.chips):
            slot = self._id((*chip, self.c))
            out.append((2 + j, 5 + j, self._push(q, 5 + j, self._slot(q, slot), slot, self.sib)))
        return out

    def start(self):
        for q in range(len(self.items)):
            self._local(q).start()
            for _, cp in self._direct(q):
                cp.start()

    def forward(self):
        for q, (kind, _, _) in enumerate(self.items):
            if kind not in ("ag", "agc"):
                continue
            for k_in, _, fwd in self._forwards(q):
                self._push(q, k_in, self._slot(q, self.me), self.me, self.sib).wait_recv()
                fwd.start()

    def finish(self):
        for q, (kind, _, _) in enumerate(self.items):
            self._local(q).wait()
            if kind in ("xc", "xcc"):
                for _, cp in self._direct(q):
                    cp.wait()
                continue
            for k, cp in self._direct(q):
                cp.wait_send()
                if k == 1:
                    cp.wait_recv()
            for _, _, fwd in self._forwards(q):
                fwd.wait()


def comm_call(items, name):
    kinds = [k for k, _ in items]
    n = len(items)

    def body(*refs):
        cm = _Comm(kinds, refs[:n], refs[n:2 * n], refs[2 * n], refs[2 * n + 1])
        cm.start()
        cm.forward()
        cm.finish()

    return pl.pallas_call(
        body, name=name,
        in_specs=[pl.BlockSpec(memory_space=pl.ANY)] * n, out_specs=[pl.BlockSpec(memory_space=pl.ANY)] * n,
        out_shape=[_comm_out_shape(k, v) for k, v in items], scratch_shapes=_comm_scratch(n),
    )(*[v for _, v in items])


def all_gather(v, name):
    return comm_call([("ag", v)], name)[0]


def _pcall(body, args, *, name, grid, in_specs, out_specs, out_shape, scratch_shapes=(), sem, comm=(), fwd=0.85):
    nc, n_in, n_out = len(comm), len(in_specs), len(out_shape)
    if not nc:
        return pl.pallas_call(body, name=name, grid=grid, in_specs=list(in_specs), out_specs=list(out_specs),
                              out_shape=list(out_shape), scratch_shapes=list(scratch_shapes),
                              compiler_params=_cp(*sem))(*args)
    kinds = [k for k, _ in comm]
    steps = math.prod(grid)
    fwd_step = min(int(fwd * steps), steps - 1)

    def carried(*refs):
        ins, csrc = refs[:n_in], refs[n_in:n_in + nc]
        outs, cdst = refs[n_in + nc:n_in + nc + n_out], refs[n_in + nc + n_out:n_in + 2 * nc + n_out]
        scr = refs[n_in + 2 * nc + n_out:]
        cm = _Comm(kinds, csrc, cdst, scr[-2], scr[-1])
        step = 0
        for d, g in enumerate(grid):
            step = step * g + pl.program_id(d)

        @pl.when(step == 0)
        def _():
            cm.start()

        body(*ins, *outs, *scr[:-2])

        @pl.when(step == fwd_step)
        def _():
            cm.forward()

        @pl.when(step == steps - 1)
        def _():
            cm.finish()

    hbm = pl.BlockSpec(memory_space=pl.ANY)
    out = pl.pallas_call(
        carried, name=name, grid=grid, in_specs=list(in_specs) + [hbm] * nc, out_specs=list(out_specs) + [hbm] * nc,
        out_shape=list(out_shape) + [_comm_out_shape(k, v) for k, v in comm],
        scratch_shapes=list(scratch_shapes) + _comm_scratch(nc), compiler_params=_cp(*(("arbitrary",) * len(grid))),
    )(*args, *[v for _, v in comm])
    return list(out[:n_out]), list(out[n_out:])


def mm(a, b, *, ta=False, out_dtype=F32, tm=512, tn=1024, tk=2048, i_outer=False, a_halves=False, b_halves=False,
       name, comm=()):
    if a_halves:
        m, kd = a.shape[1], 2 * a.shape[2]
    elif ta:
        kd, m = a.shape
    else:
        m, kd = a.shape
    kd2, n = (b.shape[1], 2 * b.shape[2]) if b_halves else b.shape
    assert kd == kd2 and not (ta and a_halves), (a.shape, b.shape, ta)
    tm, tn, tk = _tile(m, tm), _tile(n // 2 if b_halves else n, tn), _tile(kd // 2 if a_halves else kd, tk)
    nk = kd // tk
    nkh, njh = nk // 2, n // tn // 2
    grid = (m // tm, n // tn, nk) if i_outer else (n // tn, m // tm, nk)
    dims = (((0,) if ta else (1,), (0,)), ((), ()))

    def ix(f):
        return (lambda i, j, k: f(i, j, k)) if i_outer else (lambda j, i, k: f(i, j, k))

    def body(a_ref, b_ref, o_ref, *scr):
        p = lax.dot_general(a_ref[...], b_ref[...], dims, preferred_element_type=F32)
        if nk == 1:
            o_ref[...] = p.astype(o_ref.dtype)
        else:
            acc = scr[0]
            k = pl.program_id(2)

            @pl.when(k == 0)
            def _():
                acc[...] = p

            @pl.when(k > 0)
            def _():
                acc[...] += p

            @pl.when(k == nk - 1)
            def _():
                o_ref[...] = acc[...].astype(o_ref.dtype)

    if a_halves:
        a_spec = pl.BlockSpec((None, tm, tk), ix(lambda i, j, k: (k // nkh, i, k % nkh)))
    elif ta:
        a_spec = pl.BlockSpec((tk, tm), ix(lambda i, j, k: (k, i)))
    else:
        a_spec = pl.BlockSpec((tm, tk), ix(lambda i, j, k: (i, k)))
    if b_halves:
        b_spec = pl.BlockSpec((None, tk, tn), ix(lambda i, j, k: (j // njh, k, j % njh)))
    else:
        b_spec = pl.BlockSpec((tk, tn), ix(lambda i, j, k: (k, j)))
    out = _pcall(body, (a, b), name=name, grid=grid, in_specs=[a_spec, b_spec],
                 out_specs=[pl.BlockSpec((tm, tn), ix(lambda i, j, k: (i, j)))],
                 out_shape=[jax.ShapeDtypeStruct((m, n), out_dtype)],
                 scratch_shapes=[pltpu.VMEM((tm, tn), F32)] if nk > 1 else [],
                 sem=("parallel", "parallel", "arbitrary"), comm=comm)
    return (out[0][0], out[1]) if comm else out[0]


def ffn_in_act(h, w, name, comm=(), fwd=0.85):
    t = h.shape[0]
    tm, tn = _tile(t, 512), 512
    nj = D_FFP // tn

    def body(h_ref, wa_ref, wb_ref, ab_ref, act_ref):
        hv = h_ref[...]
        pa = _dot(hv, wa_ref[...])
        pb = _dot(hv, wb_ref[...])
        ab_ref[0] = pa.astype(BF16)
        ab_ref[1] = pb.astype(BF16)
        act_ref[...] = (pa * _sigmoid(pa) * pb).astype(BF16)

    out = _pcall(body, (h, w, w), name=name, grid=(nj, t // tm),
                 in_specs=[pl.BlockSpec((tm, D), lambda j, i: (i, 0)), pl.BlockSpec((D, tn), lambda j, i: (0, j)),
                           pl.BlockSpec((D, tn), lambda j, i: (0, nj + j))],
                 out_specs=[pl.BlockSpec((2, tm, tn), lambda j, i: (0, i, j)), pl.BlockSpec((tm, tn), lambda j, i: (i, j))],
                 out_shape=[jax.ShapeDtypeStruct((2, t, D_FFP), BF16), jax.ShapeDtypeStruct((t, D_FFP), BF16)],
                 sem=("parallel", "parallel"), comm=comm, fwd=fwd)
    return (out[0][0], out[0][1], out[1]) if comm else (out[0], out[1])


def ffn_dab(df, w_out_t, ab, name, comm=()):
    t = df.shape[0]
    tm, tn = _tile(t, 1024), 512

    def body(d_ref, w_ref, ab_ref, o_ref):
        dv = _dot(d_ref[...], w_ref[...])
        a = ab_ref[0].astype(F32)
        b = ab_ref[1].astype(F32)
        s = _sigmoid(a)
        o_ref[0] = (dv * b * (s * (1.0 + a * (1.0 - s)))).astype(BF16)
        o_ref[1] = (dv * (a * s)).astype(BF16)

    out = _pcall(body, (df, w_out_t, ab), name=name, grid=(D_FFP // tn, t // tm),
                 in_specs=[pl.BlockSpec((tm, D), lambda j, i: (i, 0)), pl.BlockSpec((D, tn), lambda j, i: (0, j)),
                           pl.BlockSpec((2, tm, tn), lambda j, i: (0, i, j))],
                 out_specs=[pl.BlockSpec((2, tm, tn), lambda j, i: (0, i, j))],
                 out_shape=[jax.ShapeDtypeStruct((2, t, D_FFP), BF16)], sem=("parallel", "parallel"), comm=comm, fwd=0.85)
    return (out[0][0], out[1]) if comm else out[0]


def ada_fwd(c8, w_loc, b_loc, name):
    n = w_loc.shape[1]
    tn = 256

    def body(c_ref, w_ref, b_ref, o_ref):
        cv = c_ref[...]
        ca = cv * _sigmoid(cv)
        o_ref[...] = jnp.dot(ca, w_ref[...], precision=HI, preferred_element_type=F32) + b_ref[...]

    return pl.pallas_call(
        body, name=name, grid=(n // tn,),
        in_specs=[pl.BlockSpec((N_DEV, D), lambda j: (0, 0)), pl.BlockSpec((D, tn), lambda j: (0, j)),
                  pl.BlockSpec((1, tn), lambda j: (0, j))],
        out_specs=pl.BlockSpec((N_DEV, tn), lambda j: (0, j)),
        out_shape=jax.ShapeDtypeStruct((N_DEV, n), F32), compiler_params=_cp("parallel"),
    )(c8, w_loc, b_loc)


def ada_bwd(c8t, dm_loc, name):
    n = dm_loc.shape[1]
    tn = 256

    def body(c_ref, d_ref, o_ref):
        cv = c_ref[...]
        ca = cv * _sigmoid(cv)
        o_ref[...] = jnp.dot(ca, d_ref[...], precision=HI, preferred_element_type=F32)

    return pl.pallas_call(
        body, name=name, grid=(n // tn,),
        in_specs=[pl.BlockSpec((D, N_DEV), lambda j: (0, 0)), pl.BlockSpec((N_DEV, tn), lambda j: (0, j))],
        out_specs=pl.BlockSpec((D, tn), lambda j: (0, j)),
        out_shape=jax.ShapeDtypeStruct((D, n), F32), compiler_params=_cp("parallel"),
    )(c8t, dm_loc)


def mod_fwd(x, nw, mods, shk, sck, *, f=None, gk=None, gscale=1.0, name, comm=()):
    t = x.shape[0]
    tm = min(256, t)
    res = f is not None

    def body(*refs):
        if res:
            x_ref, f_ref, g_ref, nw_ref, sh_ref, sc_ref, x1_ref, h_ref = refs
            xv = x_ref[...] + (gscale * g_ref[...]) * f_ref[...].astype(F32)
            x1_ref[...] = xv
        else:
            x_ref, nw_ref, sh_ref, sc_ref, h_ref = refs
            xv = x_ref[...]
        r = lax.rsqrt(jnp.mean(xv * xv, axis=-1, keepdims=True) + EPS)
        h_ref[...] = ((xv * r * nw_ref[...]) * (1.0 + sc_ref[...]) + sh_ref[...]).astype(BF16)

    ins = [x] + ([f, mods] if res else []) + [nw, mods, mods]
    specs = [_row(tm, D)] + ([_row(tm, D), _vec(D, gk)] if res else []) + [_vec(D), _vec(D, shk), _vec(D, sck)]
    outs = ([jax.ShapeDtypeStruct((t, D), F32)] if res else []) + [jax.ShapeDtypeStruct((t, D), BF16)]
    ospecs = ([_row(tm, D)] if res else []) + [_row(tm, D)]
    out = _pcall(body, ins, name=name, grid=(t // tm,), in_specs=specs, out_specs=ospecs, out_shape=outs,
                 sem=("parallel",), comm=comm)
    if comm:
        return (out[0] if res else out[0][0]), out[1]
    return out if res else out[0]


def final_fwd_bwd(x2, f3, mods, nf, tgt, name):
    t = x2.shape[0]
    tm = min(256, t)

    def body(x_ref, f_ref, g_ref, nf_ref, t_ref, dx_ref, df_ref, st_ref):
        @pl.when(pl.program_id(0) == 0)
        def _():
            st_ref[...] = jnp.zeros_like(st_ref)

        g = 0.5 * g_ref[...]
        fv = f_ref[...].astype(F32)
        xv = x_ref[...] + g * fv
        r = lax.rsqrt(jnp.mean(xv * xv, axis=-1, keepdims=True) + EPS)
        xh = xv * r
        nfv = nf_ref[...]
        e = xh * nfv - t_ref[...]
        st_ref[2:3, :] += jnp.sum(e * e, axis=0, keepdims=True)
        dy = e * (1.0 / D)
        st_ref[0:1, :] += jnp.sum(dy * xh, axis=0, keepdims=True)
        dxh = dy * nfv
        dx = r * (dxh - xh * jnp.mean(dxh * xh, axis=-1, keepdims=True))
        dx_ref[...] = dx
        df_ref[...] = (g * dx).astype(BF16)
        st_ref[1:2, :] += 0.5 * jnp.sum(fv * dx, axis=0, keepdims=True)

    return pl.pallas_call(
        body, name=name, grid=(t // tm,),
        in_specs=[_row(tm, D), _row(tm, D), _vec(D, 8), _vec(D), _row(tm, D)],
        out_specs=[_row(tm, D), _row(tm, D), _stats(D)],
        out_shape=[jax.ShapeDtypeStruct((t, D), F32), jax.ShapeDtypeStruct((t, D), BF16),
                   jax.ShapeDtypeStruct((8, D), F32)],
        compiler_params=_cp("arbitrary"),
    )(x2, f3, mods, nf, tgt)


def mod_bwd(x_in, dh, dx_out, nw, mods, sck, *, fprev=None, gk=None, gscale=1.0, name):
    t = x_in.shape[0]
    tm = min(256, t)
    gate = fprev is not None

    def body(*refs):
        if gate:
            x_ref, dh_ref, dxo_ref, nw_ref, sc_ref, f_ref, g_ref, dx_ref, df_ref, st_ref = refs
        else:
            x_ref, dh_ref, dxo_ref, nw_ref, sc_ref, dx_ref, st_ref = refs

        @pl.when(pl.program_id(0) == 0)
        def _():
            st_ref[...] = jnp.zeros_like(st_ref)

        xv = x_ref[...]
        dhv = dh_ref[...].astype(F32)
        r = lax.rsqrt(jnp.mean(xv * xv, axis=-1, keepdims=True) + EPS)
        xh = xv * r
        nwv = nw_ref[...]
        st_ref[0:1, :] += jnp.sum(dhv, axis=0, keepdims=True)
        st_ref[1:2, :] += jnp.sum(dhv * (xh * nwv), axis=0, keepdims=True)
        dn = dhv * (1.0 + sc_ref[...])
        st_ref[2:3, :] += jnp.sum(dn * xh, axis=0, keepdims=True)
        dxh = dn * nwv
        dx = dxo_ref[...] + r * (dxh - xh * jnp.mean(dxh * xh, axis=-1, keepdims=True))
        dx_ref[...] = dx
        if gate:
            df_ref[...] = ((gscale * g_ref[...]) * dx).astype(BF16)
            st_ref[3:4, :] += gscale * jnp.sum(f_ref[...].astype(F32) * dx, axis=0, keepdims=True)

    ins = [x_in, dh, dx_out, nw, mods] + ([fprev, mods] if gate else [])
    specs = [_row(tm, D), _row(tm, D), _row(tm, D), _vec(D), _vec(D, sck)] + ([_row(tm, D), _vec(D, gk)] if gate else [])
    outs = [jax.ShapeDtypeStruct((t, D), F32)] + ([jax.ShapeDtypeStruct((t, D), BF16)] if gate else []) + \
        [jax.ShapeDtypeStruct((8, D), F32)]
    ospecs = [_row(tm, D)] + ([_row(tm, D)] if gate else []) + [_stats(D)]
    return pl.pallas_call(body, name=name, grid=(t // tm,), in_specs=specs, out_specs=ospecs, out_shape=outs,
                          compiler_params=_cp("arbitrary"))(*ins)


def _conv_pre(cur, prev8, w, b, tm):
    full = jnp.concatenate([prev8, cur], axis=0)
    pre = b + w[3:4, :] * cur
    for k in range(CONV_K - 1):
        s = CONV_K - 1 - k
        pre = pre + w[k:k + 1, :] * pltpu.roll(full, s, 0)[8:8 + tm, :]
    return pre


def conv_fwd(proj, cw_full, cb_full, name):
    t = proj.shape[0]
    tm = min(512, t)
    cwid = 1024
    cb0 = P_XBC // cwid

    def body(x_ref, p_ref, w_ref, b_ref, o_ref):
        i = pl.program_id(1)
        prev8 = jnp.where(i == 0, 0.0, p_ref[...].astype(F32)[8:16])
        pre = _conv_pre(x_ref[...].astype(F32), prev8, w_ref[...], b_ref[...], tm)
        o_ref[...] = pre * _sigmoid(pre)

    return pl.pallas_call(
        body, name=name, grid=(CONV_DIM // cwid, t // tm),
        in_specs=[pl.BlockSpec((tm, cwid), lambda j, i: (i, cb0 + j)),
                  pl.BlockSpec((16, cwid), lambda j, i: (jnp.maximum(i * (tm // 16) - 1, 0), cb0 + j)),
                  pl.BlockSpec((CONV_K, cwid), lambda j, i: (0, j)), pl.BlockSpec((1, cwid), lambda j, i: (0, j))],
        out_specs=pl.BlockSpec((tm, cwid), lambda j, i: (i, j)),
        out_shape=jax.ShapeDtypeStruct((t, CONV_DIM), F32), compiler_params=_cp("parallel", "parallel"),
    )(proj, proj, cw_full, cb_full)


def conv_bwd_pre(proj, dxc, cw_full, cb_full, name):
    t = proj.shape[0]
    tm = min(512, t)
    cwid = 1024
    cb0 = P_XBC // cwid

    def body(x_ref, p_ref, d_ref, w_ref, b_ref, o_ref, st_ref):
        i = pl.program_id(1)

        @pl.when(i == 0)
        def _():
            st_ref[...] = jnp.zeros_like(st_ref)

        cur = x_ref[...].astype(F32)
        prev8 = jnp.where(i == 0, 0.0, p_ref[...].astype(F32)[8:16])
        pre = _conv_pre(cur, prev8, w_ref[...], b_ref[...], tm)
        s = _sigmoid(pre)
        dpre = d_ref[...] * (s * (1.0 + pre * (1.0 - s)))
        o_ref[...] = dpre
        st_ref[4:5, :] += jnp.sum(dpre, axis=0, keepdims=True)
        st_ref[3:4, :] += jnp.sum(dpre * cur, axis=0, keepdims=True)
        full = jnp.concatenate([prev8, cur], axis=0)
        for k in range(CONV_K - 1):
            sft = CONV_K - 1 - k
            st_ref[k:k + 1, :] += jnp.sum(dpre * pltpu.roll(full, sft, 0)[8:8 + tm, :], axis=0, keepdims=True)

    return pl.pallas_call(
        body, name=name, grid=(CONV_DIM // cwid, t // tm),
        in_specs=[pl.BlockSpec((tm, cwid), lambda j, i: (i, cb0 + j)),
                  pl.BlockSpec((16, cwid), lambda j, i: (jnp.maximum(i * (tm // 16) - 1, 0), cb0 + j)),
                  pl.BlockSpec((tm, cwid), lambda j, i: (i, j)),
                  pl.BlockSpec((CONV_K, cwid), lambda j, i: (0, j)), pl.BlockSpec((1, cwid), lambda j, i: (0, j))],
        out_specs=[pl.BlockSpec((tm, cwid), lambda j, i: (i, j)), pl.BlockSpec((8, cwid), lambda j, i: (0, j))],
        out_shape=[jax.ShapeDtypeStruct((t, CONV_DIM), F32), jax.ShapeDtypeStruct((8, CONV_DIM), F32)],
        compiler_params=_cp("parallel", "arbitrary"),
    )(proj, proj, dxc, cw_full, cb_full)


def conv_bwd_in(dpre, cw_full, dproj, name):
    t = dpre.shape[0]
    tm = min(512, t)
    cwid = 1024
    nt = t // tm

    def body(d_ref, n_ref, w_ref, _, o_ref):
        i = pl.program_id(1)
        cur = d_ref[...]
        nxt = jnp.where(i == nt - 1, 0.0, n_ref[...])
        full = jnp.concatenate([cur, nxt], axis=0)
        w = w_ref[...]
        acc = w[3:4, :] * cur
        for k in range(CONV_K - 1):
            s = CONV_K - 1 - k
            acc = acc + w[k:k + 1, :] * pltpu.roll(full, tm + 8 - s, 0)[0:tm, :]
        o_ref[...] = acc.astype(BF16)

    return pl.pallas_call(
        body, name=name, grid=(CONV_DIM // cwid, nt),
        in_specs=[pl.BlockSpec((tm, cwid), lambda j, i: (i, j)),
                  pl.BlockSpec((8, cwid), lambda j, i: (jnp.minimum((i + 1) * (tm // 8), t // 8 - 1), j)),
                  pl.BlockSpec((CONV_K, cwid), lambda j, i: (0, j)), _HBM],
        out_specs=pl.BlockSpec((tm, cwid), lambda j, i: (i, P_XBC // cwid + j)),
        out_shape=jax.ShapeDtypeStruct(dproj.shape, BF16), input_output_aliases={3: 0},
        compiler_params=_cp("parallel", "parallel"),
    )(dpre, dpre, cw_full, dproj)


def _nt(a, b):
    return lax.dot_general(a, b, (((1,), (1,)), ((), ())), preferred_element_type=F32)


def _dot(a, b):
    return jnp.dot(a, b, preferred_element_type=F32)


def _head_lanes():
    return lax.broadcasted_iota(jnp.int32, (1, 128), 1) < NH


def _expand_heads(x, e3):
    x = jnp.where(_head_lanes(), x, 0.0)
    hi = x.astype(BF16).astype(F32)
    r1 = x - hi
    mid = r1.astype(BF16).astype(F32)
    packed = hi + pltpu.roll(mid, NH, 1) + pltpu.roll(r1 - mid, 2 * NH, 1)
    return _dot(packed.astype(BF16), e3)


def _reduce_heads(v, e3):
    hi = v.astype(BF16)
    lo = (v - hi.astype(F32)).astype(BF16)
    return jnp.where(_head_lanes(), _nt(hi, e3) + _nt(lo, e3), 0.0)


def _ssd_common(dt_ref, dtb_ref, al_ref, exp_ref, tri_ref):
    a_row = jnp.where(_head_lanes(), -jnp.exp(al_ref[...]), 0.0)
    zraw = dt_ref[...] + dtb_ref[...]
    dtv = _softplus(zraw)
    cs = jnp.dot(tri_ref[...], dtv * a_row, precision=HI, preferred_element_type=F32)
    e3 = exp_ref[...]
    return a_row, zraw, dtv, cs, _expand_heads(cs, e3), _expand_heads(dtv, e3)


def ssd_fwd(xc, proj, dtb_row, alog_row, dx_row, expm, tri, name):
    t = xc.shape[0]
    nc = t // LCH

    def body(xs_ref, bm_ref, cm_ref, dt_ref, dtb_ref, al_ref, dxr_ref, exp_ref, tri_ref, y_ref, hs_ref, h_scr):
        @pl.when(pl.program_id(0) == 0)
        def _():
            h_scr[...] = jnp.zeros_like(h_scr)

        _, _, _, cs, csx, dtx = _ssd_common(dt_ref, dtb_ref, al_ref, exp_ref, tri_ref)
        cst = cs.T
        csl = csx[LCH - 1:LCH, :]
        xs = xs_ref[...]
        xd = xs * dtx
        xdw = xd * jnp.exp(csl - csx)
        ecs = jnp.exp(csx)
        ecl = jnp.exp(csl)
        tril = lax.broadcasted_iota(jnp.int32, (LCH, LCH), 0) >= lax.broadcasted_iota(jnp.int32, (LCH, LCH), 1)
        hs_ref[...] = h_scr[...]
        for g in range(NG):
            gc = slice(g * 512, (g + 1) * 512)
            bm = bm_ref[:, g * NST:(g + 1) * NST]
            cmb = cm_ref[:, g * NST:(g + 1) * NST].astype(BF16)
            gm = _nt(cmb, bm.astype(BF16))
            hg = h_scr[:, gc]
            yo = _dot(cmb, hg.astype(BF16)) * ecs[:, gc]
            st = _dot(bm.T.astype(BF16), xdw[:, gc].astype(BF16))
            for r in range(8):
                h = g * 8 + r
                hc = slice(h * HP, (h + 1) * HP)
                seg = cs[:, h:h + 1] - cst[h:h + 1, :]
                m = (gm * jnp.exp(jnp.where(tril, seg, NEG))).astype(BF16)
                yd = _dot(m, xd[:, hc].astype(BF16))
                y_ref[:, hc] = yd + yo[:, r * HP:(r + 1) * HP] + dxr_ref[:, hc] * xs[:, hc]
            h_scr[:, gc] = ecl[:, gc] * hg + st

    return pl.pallas_call(
        body, name=name, grid=(nc,),
        in_specs=[pl.BlockSpec((LCH, 2048), lambda c: (c, 0)), pl.BlockSpec((LCH, 512), lambda c: (c, 4)),
                  pl.BlockSpec((LCH, 512), lambda c: (c, 5)), pl.BlockSpec((LCH, 128), lambda c: (c, 0)),
                  _vec(128), _vec(128), _vec(2048), pl.BlockSpec((128, 2048), lambda c: (0, 0)),
                  pl.BlockSpec((LCH, LCH), lambda c: (0, 0))],
        out_specs=[pl.BlockSpec((LCH, 2048), lambda c: (c, 0)), pl.BlockSpec((None, NST, 2048), lambda c: (c, 0, 0))],
        out_shape=[jax.ShapeDtypeStruct((t, 2048), F32), jax.ShapeDtypeStruct((nc, NST, 2048), F32)],
        scratch_shapes=[pltpu.VMEM((NST, 2048), F32)],
        compiler_params=_cp("arbitrary"),
    )(xc, xc, xc, proj, dtb_row, alog_row, dx_row, expm, tri)


def ssd_bwd(xc, proj, hsave, dy, dtb_row, alog_row, dx_row, expm, tri, dproj, name):
    t = xc.shape[0]
    nc = t // LCH

    def body(xs_ref, bm_ref, cm_ref, dt_ref, hs_ref, dy_ref, dtb_ref, al_ref, dxr_ref, exp_ref, tri_ref, _,
             dxc_ref, ddt_ref, st_ref, dh_scr, dxd_scr, dcsx_scr):
        @pl.when(pl.program_id(0) == 0)
        def _():
            dh_scr[...] = jnp.zeros_like(dh_scr)
            st_ref[...] = jnp.zeros_like(st_ref)

        a_row, zraw, dtv, cs, csx, dtx = _ssd_common(dt_ref, dtb_ref, al_ref, exp_ref, tri_ref)
        e = exp_ref[...]
        cst = cs.T
        csl = csx[LCH - 1:LCH, :]
        xs = xs_ref[...]
        xd = xs * dtx
        wend = jnp.exp(csl - csx)
        xdw = xd * wend
        ecs = jnp.exp(csx)
        ecl = jnp.exp(csl)
        ri = lax.broadcasted_iota(jnp.int32, (LCH, LCH), 0)
        ci = lax.broadcasted_iota(jnp.int32, (LCH, LCH), 1)
        tril = ri >= ci
        triu = ri <= ci
        lane = lax.broadcasted_iota(jnp.int32, (1, 128), 1)
        dyv = dy_ref[...]
        dxr = dxr_ref[...]
        st_ref[2:3, :] += _reduce_heads(jnp.sum(dyv * xs, axis=0, keepdims=True), e)
        dcs = jnp.zeros((LCH, 128), F32)
        for g in range(NG):
            gc = slice(g * 512, (g + 1) * 512)
            bmb = bm_ref[:, g * NST:(g + 1) * NST].astype(BF16)
            cm = cm_ref[:, g * NST:(g + 1) * NST]
            cmb = cm.astype(BF16)
            hg = hs_ref[:, gc]
            hgb = hg.astype(BF16)
            dhc = dh_scr[:, gc]
            dhcb = dhc.astype(BF16)
            dyg = dyv[:, gc]
            yo = _dot(cmb, hgb) * ecs[:, gc]
            dq = (dyg * ecs[:, gc]).astype(BF16)
            dcm = _nt(dq, hgb)
            dh_yo = _dot(cm.T.astype(BF16), dq)
            dxdw = _dot(bmb, dhcb)
            dbm = _nt(xdw[:, gc].astype(BF16), dhcb)
            tt = dxdw * xdw[:, gc]
            dcsx_g = dyg * yo - tt
            dcsl_g = jnp.sum(tt, axis=0, keepdims=True) + jnp.sum(dhc * hg, axis=0, keepdims=True) * ecl[:, gc]
            dxd_scr[:, gc] = dxdw * wend[:, gc]
            dh_scr[:, gc] = ecl[:, gc] * dhc + dh_yo
            gm = _nt(cmb, bmb)
            gmt = _nt(bmb, cmb)
            dg = jnp.zeros((LCH, LCH), F32)
            dgt = jnp.zeros((LCH, LCH), F32)
            for r in range(8):
                h = g * 8 + r
                hc = slice(h * HP, (h + 1) * HP)
                seg = cs[:, h:h + 1] - cst[h:h + 1, :]
                lm = jnp.exp(jnp.where(tril, seg, NEG))
                lmt = jnp.exp(jnp.where(triu, -seg, NEG))
                mm_ = gm * lm
                mmt = gmt * lmt
                xdh = xd[:, hc].astype(BF16)
                dyh = dyv[:, hc].astype(BF16)
                dm = _nt(dyh, xdh)
                dmt = _nt(xdh, dyh)
                dxd_scr[:, hc] += _dot(mmt.astype(BF16), dyh)
                rs = jnp.sum(dm * mm_, axis=1, keepdims=True) - jnp.sum(dmt * mmt, axis=1, keepdims=True)
                dcs = dcs + rs * jnp.where(lane == h, 1.0, 0.0)
                dg = dg + dm * lm
                dgt = dgt + dmt * lmt
            dcm = dcm + _dot(dg.astype(BF16), bmb)
            dbm = dbm + _dot(dgt.astype(BF16), cmb)
            dxc_ref[:, 2048 + g * NST:2048 + (g + 1) * NST] = dbm
            dxc_ref[:, 2560 + g * NST:2560 + (g + 1) * NST] = dcm
            dcsx_scr[:, gc] = dcsx_g
            dcsx_scr[LCH - 1:LCH, gc] += dcsl_g
        dxd = dxd_scr[...]
        dxc_ref[:, 0:2048] = dxr * dyv + dxd * dtx
        ddtv = _reduce_heads(dxd * xs, e)
        dcs = dcs + _reduce_heads(dcsx_scr[...], e)
        dda =lax.dot_general(tri_ref[...], dcs, (((0,), (0,)), ((), ())), precision=HI, preferred_element_type=F32)
        ddtv = ddtv + dda * a_row
        st_ref[0:1, :] += jnp.sum(dda * dtv, axis=0, keepdims=True) * a_row
        ddt = ddtv * _sigmoid(zraw)
        ddt_ref[:, 0:128] = ddt.astype(BF16)
        ddt_ref[:, 128:INP - P_DT] = jnp.zeros((LCH, INP - P_DT - 128), BF16)
        st_ref[1:2, :] += jnp.sum(ddt, axis=0, keepdims=True)

    rc = lambda c: nc - 1 - c
    return pl.pallas_call(
        body, name=name, grid=(nc,),
        in_specs=[pl.BlockSpec((LCH, 2048), lambda c: (rc(c), 0)), pl.BlockSpec((LCH, 512), lambda c: (rc(c), 4)),
                  pl.BlockSpec((LCH, 512), lambda c: (rc(c), 5)),
                  pl.BlockSpec((LCH, 128), lambda c: (rc(c), 0)),
                  pl.BlockSpec((None, NST, 2048), lambda c: (rc(c), 0, 0)),
                  pl.BlockSpec((LCH, 2048), lambda c: (rc(c), 0)),
                  _vec(128), _vec(128), _vec(2048), pl.BlockSpec((128, 2048), lambda c: (0, 0)),
                  pl.BlockSpec((LCH, LCH), lambda c: (0, 0)), _HBM],
        out_specs=[pl.BlockSpec((LCH, CONV_DIM), lambda c: (rc(c), 0)),
                   pl.BlockSpec((LCH, INP - P_DT), lambda c: (rc(c), P_DT // (INP - P_DT))), _stats(128)],
        out_shape=[jax.ShapeDtypeStruct((t, CONV_DIM), F32), jax.ShapeDtypeStruct(dproj.shape, BF16),
                   jax.ShapeDtypeStruct((8, 128), F32)],
        scratch_shapes=[pltpu.VMEM((NST, 2048), F32), pltpu.VMEM((LCH, 2048), F32), pltpu.VMEM((LCH, 2048), F32)],
        input_output_aliases={11: 1}, compiler_params=_cp("arbitrary"),
    )(xc, xc, xc, proj, hsave, dy, dtb_row, alog_row, dx_row, expm, tri, dproj)


def ssd_out_fwd(y, proj, nw, name):
    t = y.shape[0]
    tm = min(512, t)

    def body(y_ref, z_ref, nw_ref, o_ref):
        for g in range(NG):
            gc = slice(g * 512, (g + 1) * 512)
            z = z_ref[:, gc].astype(F32)
            yz = y_ref[:, gc] * (z * _sigmoid(z))
            r = lax.rsqrt(jnp.mean(yz * yz, axis=-1, keepdims=True) + EPS)
            o_ref[:, gc] = (yz * r * nw_ref[:, gc]).astype(BF16)

    return pl.pallas_call(body, name=name, grid=(t // tm,),
                          in_specs=[_row(tm, 2048), _row(tm, 2048, P_Z // 2048), _vec(2048)],
                          out_specs=_row(tm, 2048), out_shape=jax.ShapeDtypeStruct((t, 2048), BF16),
                          compiler_params=_cp("parallel"))(y, proj, nw)


def ssd_out_bwd(y, proj, dya, nw, dproj, name):
    t = y.shape[0]
    tm = min(512, t)

    def body(y_ref, z_ref, d_ref, nw_ref, _, dy_ref, dz_ref, st_ref):
        @pl.when(pl.program_id(0) == 0)
        def _():
            st_ref[...] = jnp.zeros_like(st_ref)

        for g in range(NG):
            gc = slice(g * 512, (g + 1) * 512)
            z = z_ref[:, gc].astype(F32)
            yv = y_ref[:, gc]
            s = _sigmoid(z)
            sz = z * s
            yz = yv * sz
            r = lax.rsqrt(jnp.mean(yz * yz, axis=-1, keepdims=True) + EPS)
            yzn = yz * r
            dv = d_ref[:, gc].astype(F32)
            st_ref[0:1, gc] += jnp.sum(dv * yzn, axis=0, keepdims=True)
            dyn = dv * nw_ref[:, gc]
            dyz = r * (dyn - yzn * jnp.mean(dyn * yzn, axis=-1, keepdims=True))
            dy_ref[:, gc] = dyz * sz
            dz_ref[:, gc] = (dyz * yv * (s * (1.0 + z * (1.0 - s)))).astype(BF16)

    return pl.pallas_call(
        body, name=name, grid=(t // tm,),
        in_specs=[_row(tm, 2048), _row(tm, 2048, P_Z // 2048), _row(tm, 2048), _vec(2048), _HBM],
        out_specs=[_row(tm, 2048), _row(tm, 2048, P_Z // 2048), _stats(2048)],
        out_shape=[jax.ShapeDtypeStruct((t, 2048), F32), jax.ShapeDtypeStruct(dproj.shape, BF16),
                   jax.ShapeDtypeStruct((8, 2048), F32)],
        input_output_aliases={4: 1}, compiler_params=_cp("arbitrary"))(y, proj, dya, nw, dproj)


def _cstep(ar, ai, sr, si, br, bi):
    return ar * sr - ai * si + br, ar * si + ai * sr + bi


def _halves(v):
    return (v[0:8, 0:512], v[0:8, 512:1024]), (v[8:16, 0:512], v[8:16, 512:1024])


def _slab(r1, i1, r2, i2):
    return jnp.concatenate([jnp.concatenate([r1, i1], axis=1), jnp.concatenate([r2, i2], axis=1)], axis=0).astype(BF16)


def _local_ends(x_ref, nslab, ar, ai, sr_scr, si_scr, end_ref, first, last, reverse):
    @pl.when(first)
    def _():
        sr_scr[...] = jnp.zeros_like(sr_scr)
        si_scr[...] = jnp.zeros_like(si_scr)

    def step(k, carry):
        s1, s2 = _halves(x_ref[nslab - 1 - k if reverse else k].astype(F32))
        if reverse:
            s1, s2 = s2, s1
        return _cstep(ar, ai, *_cstep(ar, ai, carry[0], carry[1], *s1), *s2)

    sr, si = lax.fori_loop(0, nslab, step, (sr_scr[...], si_scr[...]), unroll=4)
    sr_scr[...] = sr
    si_scr[...] = si

    @pl.when(last)
    def _():
        end_ref[:, 0:512] = sr
        end_ref[:, 512:1024] = si


def s5_in(u, bsg, a_re, a_im, name):
    t = u.shape[0]
    tm = min(512, t)
    nt = t // tm

    def body(u_ref, b_ref, ar_ref, ai_ref, o_ref, e_ref, sr_scr, si_scr):
        i = pl.program_id(1)
        o_ref[...] = _dot(u_ref[...].astype(BF16), b_ref[...]).astype(BF16).reshape(tm // 16, 16, 1024)
        _local_ends(o_ref, tm // 16, ar_ref[...], ai_ref[...], sr_scr, si_scr, e_ref, i == 0, i == nt - 1, False)

    return pl.pallas_call(
        body, name=name, grid=(8, nt),
        in_specs=[pl.BlockSpec((tm, 128), lambda s, i: (i, s)), pl.BlockSpec((None, 128, 1024), lambda s, i: (s, 0, 0)),
                  pl.BlockSpec((None, 8, 512), lambda s, i: (s, 0, 0)), pl.BlockSpec((None, 8, 512), lambda s, i: (s, 0, 0))],
        out_specs=[pl.BlockSpec((tm // 16, 16, 1024), lambda s, i: (i, 0, s)),
                   pl.BlockSpec((None, 8, 1024), lambda s, i: (s, 0, 0))],
        out_shape=[jax.ShapeDtypeStruct((t // 16, 16, S5NS), BF16), jax.ShapeDtypeStruct((8, 8, 1024), F32)],
        scratch_shapes=[pltpu.VMEM((8, 512), F32), pltpu.VMEM((8, 512), F32)],
        compiler_params=_cp("parallel", "arbitrary"))(u, bsg, a_re, a_im)


def s5_out_bwd(dy, csg, s, a_re, a_im, name):
    t = dy.shape[0]
    tm = min(512, t)
    nt = t // tm

    def body(dy_ref, c_ref, s_ref, ar_ref, ai_ref, e_ref, dc_ref, end_ref, sr_scr, si_scr):
        i = pl.program_id(1)

        @pl.when(i == 0)
        def _():
            dc_ref[...] = jnp.zeros_like(dc_ref)

        dyb = dy_ref[...].astype(BF16)
        e_ref[...] = _nt(dyb, c_ref[...]).astype(BF16).reshape(tm // 16, 16, 1024)
        dc_ref[...] += lax.dot_general(s_ref[...], dyb, (((0,), (0,)), ((), ())), preferred_element_type=F32)
        _local_ends(e_ref, tm // 16, ar_ref[...], -ai_ref[...], sr_scr, si_scr, end_ref, i == 0, i == nt - 1, True)

    rv = lambda i: nt - 1 - i
    return pl.pallas_call(
        body, name=name, grid=(8, nt),
        in_specs=[pl.BlockSpec((tm, 128), lambda s, i: (rv(i), s)), pl.BlockSpec((None, 1024, 128), lambda s, i: (s, 0, 0)),
                  pl.BlockSpec((tm, 1024), lambda s, i: (rv(i), s)),
                  pl.BlockSpec((None, 8, 512), lambda s, i: (s, 0, 0)), pl.BlockSpec((None, 8, 512), lambda s, i: (s, 0, 0))],
        out_specs=[pl.BlockSpec((tm // 16, 16, 1024), lambda s, i: (rv(i), 0, s)),
                   pl.BlockSpec((None, 1024, 128), lambda s, i: (s, 0, 0)),
                   pl.BlockSpec((None, 8, 1024), lambda s, i: (s, 0, 0))],
        out_shape=[jax.ShapeDtypeStruct((t // 16, 16, S5NS), BF16), jax.ShapeDtypeStruct((8, 1024, 128), F32),
                   jax.ShapeDtypeStruct((8, 8, 1024), F32)],
        scratch_shapes=[pltpu.VMEM((8, 512), F32), pltpu.VMEM((8, 512), F32)],
        compiler_params=_cp("parallel", "arbitrary"))(dy, csg, s, a_re, a_im)


def s5_scan_init(ends, a_re, a_im, lseg, reverse, name):
    nsq = int(math.log2(lseg))
    assert 2 ** nsq == lseg
    sgn = -1.0 if reverse else 1.0
    order = list(range(7, -1, -1)) if reverse else list(range(8))

    def body(e_ref, ar_ref, ai_ref, o_ref):
        pr = ar_ref[0:1, :]
        pi = sgn * ai_ref[0:1, :]
        for _ in range(nsq):
            pr, pi = pr * pr - pi * pi, 2.0 * pr * pi
        prev_r = jnp.zeros((1, 512), F32)
        prev_i = jnp.zeros((1, 512), F32)
        j0 = order[0]
        o_ref[j0:j0 + 1, 0:512] = prev_r
        o_ref[j0:j0 + 1, 512:1024] = prev_i
        for idx in range(1, 8):
            j, jp = order[idx], order[idx - 1]
            prev_r, prev_i = _cstep(pr, pi, prev_r, prev_i, e_ref[jp:jp + 1, 0:512], e_ref[jp:jp + 1, 512:1024])
            o_ref[j:j + 1, 0:512] = prev_r
            o_ref[j:j + 1, 512:1024] = prev_i

    return pl.pallas_call(
        body, name=name, grid=(8,),
        in_specs=[pl.BlockSpec((None, 8, 1024), lambda s: (s, 0, 0)), pl.BlockSpec((None, 8, 512), lambda s: (s, 0, 0)),
                  pl.BlockSpec((None, 8, 512), lambda s: (s, 0, 0))],
        out_specs=pl.BlockSpec((None, 8, 1024), lambda s: (s, 0, 0)),
        out_shape=jax.ShapeDtypeStruct((8, 8, 1024), F32), compiler_params=_cp("parallel"))(ends, a_re, a_im)


def s5_scan_fwd(b3, init, a_re, a_im, csg, u, d_row, name):
    nslab = b3.shape[0]
    ti = min(64, nslab)
    nb = nslab // ti
    rows = 16 * ti

    def body(b_ref, i_ref, ar_ref, ai_ref, c_ref, u_ref, d_ref, o_ref, y_ref, sr_scr, si_scr):
        @pl.when(pl.program_id(1) == 0)
        def _():
            sr_scr[...] = i_ref[:, 0:512]
            si_scr[...] = i_ref[:, 512:1024]

        ar = ar_ref[...]
        ai = ai_ref[...]

        def step(k, carry):
            b1, b2 = _halves(b_ref[k].astype(F32))
            r1, i1 = _cstep(ar, ai, carry[0], carry[1], *b1)
            r2, i2 = _cstep(ar, ai, r1, i1, *b2)
            o_ref[k] = _slab(r1, i1, r2, i2)
            return r2, i2

        sr, si = lax.fori_loop(0, ti, step, (sr_scr[...], si_scr[...]), unroll=4)
        sr_scr[...] = sr
        si_scr[...] = si
        y_ref[...] = _dot(o_ref[...].reshape(rows, 1024), c_ref[...]) + d_ref[...] * u_ref[...].astype(F32)

    return pl.pallas_call(
        body, name=name, grid=(8, nb),
        in_specs=[pl.BlockSpec((ti, 16, 1024), lambda s, tb: (tb, 0, s)), pl.BlockSpec((None, 8, 1024), lambda s, tb: (s, 0, 0)),
                  pl.BlockSpec((None, 8, 512), lambda s, tb: (s, 0, 0)), pl.BlockSpec((None, 8, 512), lambda s, tb: (s, 0, 0)),
                  pl.BlockSpec((None, 1024, 128), lambda s, tb: (s, 0, 0)), pl.BlockSpec((rows, 128), lambda s, tb: (tb, s)),
                  pl.BlockSpec((1, 128), lambda s, tb: (0, s))],
        out_specs=[pl.BlockSpec((ti, 16, 1024), lambda s, tb: (tb, 0, s)), pl.BlockSpec((rows, 128), lambda s, tb: (tb, s))],
        out_shape=[jax.ShapeDtypeStruct(b3.shape, BF16), jax.ShapeDtypeStruct((16 * nslab, S5W), F32)],
        scratch_shapes=[pltpu.VMEM((8, 512), F32), pltpu.VMEM((8, 512), F32)],
        compiler_params=_cp("parallel", "arbitrary"))(b3, init, a_re, a_im, csg, u, d_row)


def s5_scan_bwd(e3, linit, s3, sinit, a_re, a_im, bsg, u, dy, d_row, name):
    nslab = e3.shape[0]
    ti = min(64, nslab)
    nb = nslab // ti
    rows = 16 * ti

    def body(e_ref, li_ref, s_ref, sh_ref, si0_ref, ar_ref, ai_ref, b_ref, u_ref, dy_ref, d_ref,
             du_ref, db_ref, dd_ref, da_ref, o_ref, lr_scr, lim_scr):
        tb = pl.program_id(1)

        @pl.when(tb == 0)
        def _():
            lr_scr[...] = li_ref[:, 0:512]
            lim_scr[...] = li_ref[:, 512:1024]
            da_ref[...] = jnp.zeros_like(da_ref)
            db_ref[...] = jnp.zeros_like(db_ref)
            dd_ref[...] = jnp.zeros_like(dd_ref)

        ar = ar_ref[...]
        ai = -ai_ref[...]

        def slab(kk, lr, li, dar, dai, sp):
            e1, e2 = _halves(e_ref[kk].astype(F32))
            s1, _ = _halves(s_ref[kk].astype(F32))
            r2, i2 = _cstep(ar, ai, lr, li, *e2)
            dar = dar + r2 * s1[0] + i2 * s1[1]
            dai = dai + i2 * s1[0] - r2 * s1[1]
            r1, i1 = _cstep(ar, ai, r2, i2, *e1)
            dar = dar + r1 * sp[0] + i1 * sp[1]
            dai = dai + i1 * sp[0] - r1 * sp[1]
            o_ref[kk] = _slab(r1, i1, r2, i2)
            return r1, i1, dar, dai

        def step(k, carry):
            kk = ti - 1 - k
            return slab(kk, *carry, _halves(s_ref[kk - 1].astype(F32))[1])

        z = jnp.zeros((8, 512), F32)
        lr, li, dar, dai = lax.fori_loop(0, ti - 1, step, (lr_scr[...], lim_scr[...], z, z), unroll=2)
        first = tb == nb - 1
        halo = _halves(sh_ref[0].astype(F32))[1]
        sp = (jnp.where(first, si0_ref[:, 0:512], halo[0]), jnp.where(first, si0_ref[:, 512:1024], halo[1]))
        lr, li, dar, dai = slab(0, lr, li, dar, dai, sp)
        lr_scr[...] = lr
        lim_scr[...] = li
        da_ref[:, 0:512] += dar
        da_ref[:, 512:1024] += dai
        lb = o_ref[...].reshape(rows, 1024)
        uv = u_ref[...].astype(F32)
        dyv = dy_ref[...]
        du_ref[...] = _nt(lb, b_ref[...]) + d_ref[...] * dyv
        db_ref[...] += lax.dot_general(uv.astype(BF16), lb, (((0,), (0,)), ((), ())), preferred_element_type=F32)
        dd_ref[...] += jnp.sum(dyv * uv, axis=0, keepdims=True)

    rb = lambda tb: nb - 1 - tb
    return pl.pallas_call(
        body, name=name, grid=(8, nb),
        in_specs=[pl.BlockSpec((ti, 16, 1024), lambda s, tb: (rb(tb), 0, s)),
                  pl.BlockSpec((None, 8, 1024), lambda s, tb: (s, 0, 0)),
                  pl.BlockSpec((ti, 16, 1024), lambda s, tb: (rb(tb), 0, s)),
                  pl.BlockSpec((1, 16, 1024), lambda s, tb: (jnp.maximum(rb(tb) * ti - 1, 0), 0, s)),
                  pl.BlockSpec((None, 8, 1024), lambda s, tb: (s, 0, 0)),
                  pl.BlockSpec((None, 8, 512), lambda s, tb: (s, 0, 0)), pl.BlockSpec((None, 8, 512), lambda s, tb: (s, 0, 0)),
                  pl.BlockSpec((None, 128, 1024), lambda s, tb: (s, 0, 0)),
                  pl.BlockSpec((rows, 128), lambda s, tb: (rb(tb), s)), pl.BlockSpec((rows, 128), lambda s, tb: (rb(tb), s)),
                  pl.BlockSpec((1, 128), lambda s, tb: (0, s))],
        out_specs=[pl.BlockSpec((rows, 128), lambda s, tb: (rb(tb), s)),
                   pl.BlockSpec((None, 128, 1024), lambda s, tb: (s, 0, 0)), pl.BlockSpec((1, 128), lambda s, tb: (0, s)),
                   pl.BlockSpec((None, 8, 1024), lambda s, tb: (s, 0, 0))],
        out_shape=[jax.ShapeDtypeStruct((16 * nslab, S5W), F32), jax.ShapeDtypeStruct((8, 128, 1024), F32),
                   jax.ShapeDtypeStruct((1, S5W), F32), jax.ShapeDtypeStruct((8, 8, 1024), F32)],
        scratch_shapes=[pltpu.VMEM((ti, 16, 1024), BF16), pltpu.VMEM((8, 512), F32), pltpu.VMEM((8, 512), F32)],
        compiler_params=_cp("parallel", "arbitrary"))(e3, linit, s3, s3, sinit, a_re, a_im, bsg, u, dy, d_row)


_GC = math.sqrt(2.0 / math.pi)


def gelu_fwd(y, name):
    t, w = y.shape
    tm = min(512, t)

    def body(y_ref, o_ref):
        v = y_ref[...]
        o_ref[...] = (0.5 * v * (1.0 + jnp.tanh(_GC * (v + 0.044715 * v * v * v)))).astype(BF16)

    return pl.pallas_call(body, name=name, grid=(t // tm,), in_specs=[_row(tm, w)], out_specs=_row(tm, w),
                          out_shape=jax.ShapeDtypeStruct((t, w), BF16), compiler_params=_cp("parallel"))(y)


def gelu_bwd(y, dg, name):
    t, w = y.shape
    tm = min(512, t)

    def body(y_ref, d_ref, o_ref):
        v = y_ref[...]
        th = jnp.tanh(_GC * (v + 0.044715 * v * v * v))
        o_ref[...] = d_ref[...].astype(F32) * (0.5 * (1.0 + th) + 0.5 * v * (1.0 - th * th) * _GC * (1.0 + 3.0 * 0.044715 * v * v))

    return pl.pallas_call(body, name=name, grid=(t // tm,), in_specs=[_row(tm, w), _row(tm, w)], out_specs=_row(tm, w),
                          out_shape=jax.ShapeDtypeStruct((t, w), F32), compiler_params=_cp("parallel"))(y, dg)


def merge_fwd(proj, pa, glu, name):
    t = pa.shape[0]
    tm = min(256, t)

    def body(g_ref, pa_ref, glu_ref, o_ref):
        pb = glu_ref[:, 0:D].astype(F32) * _sigmoid(glu_ref[:, D:2 * D].astype(F32))
        o_ref[...] = (_sigmoid(g_ref[:, 0:D].astype(F32)) * pa_ref[...].astype(F32)
                      + _sigmoid(g_ref[:, D:2 * D].astype(F32)) * pb).astype(BF16)

    return pl.pallas_call(body, name=name, grid=(t // tm,), in_specs=[_row(tm, 2 * D), _row(tm, D), _row(tm, 2 * D)],
                          out_specs=_row(tm, D), out_shape=jax.ShapeDtypeStruct((t, D), BF16),
                          compiler_params=_cp("parallel"))(proj, pa, glu)


def merge_bwd(proj, pa, glu, dm, dproj, name):
    t = pa.shape[0]
    tm = min(256, t)

    def body(g_ref, pa_ref, glu_ref, dm_ref, _, dpa_ref, dglu_ref, dg_ref):
        dmv = dm_ref[...].astype(F32)
        pav = pa_ref[...].astype(F32)
        sa = _sigmoid(g_ref[:, 0:D].astype(F32))
        sb = _sigmoid(g_ref[:, D:2 * D].astype(F32))
        ga = glu_ref[:, 0:D].astype(F32)
        sg = _sigmoid(glu_ref[:, D:2 * D].astype(F32))
        pb = ga * sg
        dpb = sb * dmv
        dpa_ref[...] = (sa * dmv).astype(BF16)
        dglu_ref[:, 0:D] = (dpb * sg).astype(BF16)
        dglu_ref[:, D:2 * D] = (dpb * pb * (1.0 - sg)).astype(BF16)
        dg_ref[:, 0:D] = (dmv * pav * sa * (1.0 - sa)).astype(BF16)
        dg_ref[:, D:2 * D] = (dmv * pb * sb * (1.0 - sb)).astype(BF16)

    return pl.pallas_call(
        body, name=name, grid=(t // tm,),
        in_specs=[_row(tm, 2 * D), _row(tm, D), _row(tm, 2 * D), _row(tm, D), _HBM],
        out_specs=[_row(tm, D), _row(tm, 2 * D), _row(tm, 2 * D, P_GATES // (2 * D))],
        out_shape=[jax.ShapeDtypeStruct((t, D), BF16), jax.ShapeDtypeStruct((t, 2 * D), BF16),
                   jax.ShapeDtypeStruct(dproj.shape, BF16)],
        input_output_aliases={4: 2}, compiler_params=_cp("parallel"))(proj, pa, glu, dm, dproj)


def adamw(w, parts, m, v, name):
    r, c = w.shape
    p = parts.shape[0]
    tr = r if r <= 128 else 128
    c1 = 1.0 - ADAM_B1 ** ADAM_STEP
    c2 = 1.0 - ADAM_B2 ** ADAM_STEP

    def body(w_ref, p_ref, m_ref, v_ref, g_ref, d_ref, nm_ref, nv_ref):
        g = p_ref[0].astype(F32)
        for k in range(1, p):
            g = g + p_ref[k].astype(F32)
        mn = ADAM_B1 * m_ref[...] + (1.0 - ADAM_B1) * g
        vn = ADAM_B2 * v_ref[...] + (1.0 - ADAM_B2) * (g * g)
        g_ref[...] = g
        nm_ref[...] = mn
        nv_ref[...] = vn
        d_ref[...] = -ADAM_LR * ((mn / c1) / (jnp.sqrt(vn / c2) + ADAM_EPS) + ADAM_WD * w_ref[...])

    spec = pl.BlockSpec((tr, c), lambda i: (i, 0))
    o = jax.ShapeDtypeStruct((r, c), F32)
    return pl.pallas_call(
        body, name=name, grid=(pl.cdiv(r, tr),),
        in_specs=[spec, pl.BlockSpec((p, tr, c), lambda i: (0, i, 0)), spec, spec],
        out_specs=[spec, spec, spec, spec], out_shape=[o, o, o, o], compiler_params=_cp("parallel"))(w, parts, m, v)


def _s5_discretise(lambda_re, lambda_im, log_dt, b_re, b_im):
    dt = jnp.exp(log_dt)[:, None]
    lr = jnp.minimum(lambda_re, -1e-4)
    li = lambda_im
    mag = jnp.exp(lr * dt)
    ar = mag * jnp.cos(li * dt)
    ai = mag * jnp.sin(li * dt)
    den = lr * lr + li * li
    nr = ar - 1.0
    kr = (nr * lr + ai * li) / den
    ki = (ai * lr - nr * li) / den
    bbar_re = kr[..., None] * b_re - ki[..., None] * b_im
    bbar_im = kr[..., None] * b_im + ki[..., None] * b_re
    return ar, ai, bbar_re, bbar_im


def _block_diag(v):
    a, b = v.shape[2], v.shape[3]
    eye = jnp.eye(8, dtype=v.dtype)[None, :, None, :, None]
    return (v[:, :, :, None, :] * eye).reshape(8, 8 * a, 8 * b)


def _diag_blocks(m, a, b):
    eye = jnp.eye(8, dtype=m.dtype)[None, :, None, :, None]
    return jnp.sum(m.reshape(8, 8, a, 8, b) * eye, axis=3)


def _bsg_of(bb_re, bb_im):
    f = lambda b: _block_diag(b.reshape(8, 8, 64, 16).transpose(0, 1, 3, 2))
    return jnp.concatenate([f(bb_re), f(bb_im)], axis=2)


def _bsg_diag(dbsg):
    f = lambda x: _diag_blocks(x, 16, 64).transpose(0, 1, 3, 2).reshape(64, 64, 16)
    return f(dbsg[:, :, 0:512]), f(dbsg[:, :, 512:1024])


def _csg_of(c_re, c_im):
    f = lambda c: _block_diag(c.reshape(8, 8, 16, 64).transpose(0, 1, 3, 2))
    return jnp.concatenate([f(c_re), -f(c_im)], axis=1)


def _csg_diag(dcsg):
    f = lambda x: _diag_blocks(x, 64, 16).transpose(0, 1, 3, 2).reshape(64, 16, 64)
    return f(dcsg[:, 0:512, :]), -f(dcsg[:, 512:1024, :])


def _perm(a, t):
    return a.reshape(8, t // 8, a.shape[1]).transpose(1, 0, 2).reshape(t, a.shape[1])


def _unperm(a, t):
    return a.reshape(t // 8, 8, a.shape[1]).transpose(1, 0, 2).reshape(t, a.shape[1])


def _cols(g):
    return g.transpose(1, 0, 2).reshape(g.shape[1], N_DEV * g.shape[2])


def _rows(g):
    return g.reshape(N_DEV * g.shape[1], g.shape[2])


def _col_parts(g):
    r, c = g.shape
    return g.reshape(r, N_DEV, c // N_DEV).transpose(1, 0, 2)


def _row_parts(g):
    r, c = g.shape
    return g.reshape(N_DEV, r // N_DEV, c)


FB, FBP = D_FF // N_DEV, D_FFP // N_DEV


def _pad_ffn_in_shard(w):
    return jnp.pad(w.reshape(D, 2, FB), ((0, 0), (0, 0), (0, FBP - FB))).reshape(D, 2 * FBP)


def _unpad_ffn_in_shard(g):
    return g.reshape(D, 2, FBP)[:, :, :FB].reshape(D, 2 * FB)


def _pad_ffn_out_shard(w):
    return jnp.pad(w, ((0, FBP - FB), (0, 0)))


def sum_parts(parts, name):
    p, r, c = parts.shape
    tr = 256

    def body(p_ref, o_ref):
        g = p_ref[0].astype(F32)
        for k in range(1, p):
            g = g + p_ref[k].astype(F32)
        o_ref[...] = g

    return pl.pallas_call(body, name=name, grid=(r // tr,), in_specs=[pl.BlockSpec((p, tr, c), lambda i: (0, i, 0))],
                          out_specs=pl.BlockSpec((tr, c), lambda i: (i, 0)),
                          out_shape=jax.ShapeDtypeStruct((r, c), F32), compiler_params=_cp("parallel"))(parts)


def _pad_w_in(w):
    z = jnp.zeros((D, INP - P_DT - NH), w.dtype)
    return jnp.concatenate([w[:, O_GA:O_GB], w[:, O_GB:IN_COLS], w[:, 0:O_XBC], w[:, O_XBC:O_DT], w[:, O_U:O_GA],
                            w[:, O_DT:O_U], z], axis=1)


def _unpad_w_in(g):
    return jnp.concatenate([g[:, P_Z:P_XBC], g[:, P_XBC:P_U], g[:, P_DT:P_DT + NH], g[:, P_U:P_DT],
                            g[:, 0:D], g[:, D:2 * D]], axis=1)


_PACK = (("b_ada", 18432), ("norm_ffn1", 2048), ("norm_mix", 2048), ("conv_b", 3072), ("dt_bias", 32), ("a_log", 32),
         ("d_ssd", 32), ("ssd_norm_w", 2048), ("s5_lambda_re", 4096), ("s5_lambda_im", 4096), ("s5_b_re", 65536),
         ("s5_b_im", 65536), ("s5_c_re", 65536), ("s5_c_im", 65536), ("s5_d", 1024), ("s5_log_dt", 64),
         ("norm_ffn2", 2048), ("norm_final", 2048), ("loss", 1))
_PACK_ROWS = 304
_PACK_W = 1024


def _pack(d):
    flat = jnp.concatenate([d[k].reshape(-1).astype(F32) for k, _ in _PACK])
    return jnp.pad(flat, (0, _PACK_ROWS * _PACK_W - flat.shape[0])).reshape(_PACK_ROWS, _PACK_W)


def _unpack(a):
    flat = a.reshape(-1)
    out, off = {}, 0
    for k, n in _PACK:
        out[k] = flat[off:off + n]
        off += n
    return out


_TA = dict(tm=512, tn=512, tk=8192)


def _ffn_bwd(df, h, ab, act, w_in_shard_t, w_out_t, tag):
    dab, (g_wt,) = ffn_dab(df, w_out_t, ab, tag + "_dab", comm=[("ag", w_in_shard_t)])
    w_in_t = _rows(g_wt)
    dw_out = mm(act, df, ta=True, out_dtype=BF16, i_outer=True, name=tag + "_dwout", **_TA)
    dw_in, (x_out,) = mm(h, dab, ta=True, b_halves=True, out_dtype=BF16, i_outer=True, name=tag + "_dwin",
                         comm=[("xc", _row_parts(dw_out))], **_TA)
    dh, (x_in,) = mm(dab, w_in_t, a_halves=True, out_dtype=BF16, tk=5632, name=tag + "_dh", comm=[("xcc", dw_in)])
    g_in = _unpad_ffn_in_shard(sum_parts(x_in, tag + "_dwin_sum"))
    return dh, g_in[None], x_out


def kernel(x, c, w_ada, b_ada, norm_ffn1, w_ffn1_in, w_ffn1_out, norm_mix, w_in, conv_w, conv_b, dt_bias, a_log, d_ssd, ssd_norm_w, w_a_proj, s5_lambda_re, s5_lambda_im, s5_b_re, s5_b_im, s5_c_re, s5_c_im, s5_d, s5_log_dt, w_b_glu, w_out, norm_ffn2, w_ffn2_in, w_ffn2_out, norm_final, loss_target, m_w_ada, m_b_ada, m_norm_ffn1, m_w_ffn1_in, m_w_ffn1_out, m_norm_mix, m_w_in, m_conv_w, m_conv_b, m_dt_bias, m_a_log, m_d_ssd, m_ssd_norm_w, m_w_a_proj, m_s5_lambda_re, m_s5_lambda_im, m_s5_b_re, m_s5_b_im, m_s5_c_re, m_s5_c_im, m_s5_d, m_s5_log_dt, m_w_b_glu, m_w_out, m_norm_ffn2, m_w_ffn2_in, m_w_ffn2_out, m_norm_final, v_w_ada, v_b_ada, v_norm_ffn1, v_w_ffn1_in, v_w_ffn1_out, v_norm_mix, v_w_in, v_conv_w, v_conv_b, v_dt_bias, v_a_log, v_d_ssd, v_ssd_norm_w, v_w_a_proj, v_s5_lambda_re, v_s5_lambda_im, v_s5_b_re, v_s5_b_im, v_s5_c_re, v_s5_c_im, v_s5_d, v_s5_log_dt, v_w_b_glu, v_w_out, v_norm_ffn2, v_w_ffn2_in, v_w_ffn2_out, v_norm_final):
    args = dict(locals())
    t = x.shape[1]
    me = _my_id()
    xt = x[0]
    tgt = loss_target[0]
    small = {k: args[k] for k, _ in _PACK if k != "loss"}

    bf = lambda w: w[0].astype(BF16)
    ffn_in_shard = lambda w: _pad_ffn_in_shard(bf(w))
    ffn_out_shard = lambda w: _pad_ffn_out_shard(bf(w))

    c8 = all_gather(c, "ag_c").reshape(N_DEV, D)
    b_loc = lax.dynamic_slice(b_ada, (0, me * (N_ADA * D // N_DEV)), (1, N_ADA * D // N_DEV))
    m8 = ada_fwd(c8, w_ada[0], b_loc, "ada_fwd")
    mods = comm_call([("xc", m8.reshape(N_DEV, 1, -1))], "xc_mods")[0].reshape(1, N_ADA * D)

    h1, (wf1i, g_cw) = mod_fwd(xt, norm_ffn1, mods, 0, 1, name="mod1",
                               comm=[("agc", ffn_in_shard(w_ffn1_in)), ("ag", conv_w[0])])
    convw = _cols(g_cw)
    ab1, act1, (g_f1o, g_win, g_wap) = ffn_in_act(
        h1, wf1i, "ffn1_in", fwd=0.9,
        comm=[("ag", ffn_out_shard(w_ffn1_out)), ("ag", bf(w_in)), ("ag", bf(w_a_proj))])
    wf1o, winp, wap = _rows(g_f1o), _pad_w_in(_cols(g_win)), _rows(g_wap)
    f1, (g_wo, g_wbg) = mm(act1, wf1o, out_dtype=BF16, tk=5632, name="ffn1_out",
                           comm=[("ag", bf(w_out)), ("ag", bf(w_b_glu))])
    wo, wbg = _rows(g_wo), _cols(g_wbg)
    x1, h2 = mod_fwd(xt, norm_mix, mods, 3, 4, f=f1, gk=2, gscale=0.5, name="mod2")
    proj, (wf2i,) = mm(h2, winp, out_dtype=BF16, tm=1024, tn=512, i_outer=True, name="w_in",
                       comm=[("agc", ffn_in_shard(w_ffn2_in))])
    dtraw = mm(h2, winp[:, P_DT:P_DT + 128], tn=128, name="w_in_dt")
    cb_row = conv_b
    xc = conv_fwd(proj, convw, cb_row, "conv_fwd")
    row128 = lambda v: jnp.pad(v.reshape(1, -1), ((0, 0), (0, 128 - v.size)))
    dtb_row, alog_row = row128(dt_bias), row128(a_log)
    dx_row = jnp.repeat(d_ssd.reshape(-1), HP).reshape(1, 2048)
    rows = jnp.arange(128)[:, None]
    expm = ((rows % NH == jnp.arange(2048)[None, :] // HP) & (rows < 3 * NH)).astype(BF16)
    tri = (jnp.arange(LCH)[:, None] >= jnp.arange(LCH)[None, :]).astype(F32)
    y_ssd, hsave = ssd_fwd(xc, dtraw, dtb_row, alog_row, dx_row, expm, tri, "ssd_fwd")
    ya = ssd_out_fwd(y_ssd, proj, ssd_norm_w, "ssd_out")
    pa = mm(ya, wap, out_dtype=BF16, name="w_a_proj")

    s5p = (s5_lambda_re[0], s5_lambda_im[0], s5_log_dt[0], s5_b_re[0], s5_b_im[0])
    (ar, ai, bb_re, bb_im), s5_vjp = jax.vjp(_s5_discretise, *s5p)
    a_re8 = jnp.broadcast_to(ar.reshape(8, 1, 512), (8, 8, 512))
    a_im8 = jnp.broadcast_to(ai.reshape(8, 1, 512), (8, 8, 512))
    bsg = _bsg_of(bb_re, bb_im).astype(BF16)
    csg = _csg_of(s5_c_re[0], s5_c_im[0]).astype(BF16)
    d_row = s5_d.reshape(1, S5W)
    lseg = t // 8
    u_p = _perm(proj[:, P_U:P_U + S5W], t)
    bu3, ends_f = s5_in(u_p, bsg, a_re8, a_im8, "s5_in")
    sinit = s5_scan_init(ends_f, a_re8, a_im8, lseg, False, "s5_init_f")
    s3, yb_p = s5_scan_fwd(bu3, sinit, a_re8, a_im8, csg, u_p, d_row, "s5_scan_f")
    s2 = s3.reshape(t, S5NS)
    yb = _unperm(yb_p, t)
    gy = gelu_fwd(yb, "gelu")
    glu = mm(gy, wbg, out_dtype=BF16, name="w_b_glu")
    merged = merge_fwd(proj, pa, glu, "merge")
    o = mm(merged, wo, out_dtype=BF16, name="w_out")
    x2, h3 = mod_fwd(x1, norm_ffn2, mods, 6, 7, f=o, gk=5, gscale=1.0, name="mod3")
    ab3, act3, (g_f2o,) = ffn_in_act(h3, wf2i, "ffn2_in", comm=[("ag", ffn_out_shard(w_ffn2_out))])
    wf2o = _rows(g_f2o)
    f3 = mm(act3, wf2o, out_dtype=BF16, tk=5632, name="ffn2_out")

    dx3, df3, st_fin = final_fwd_bwd(x2, f3, mods, norm_final.reshape(1, D), tgt, "final")
    dh3, x_f2i, x_f2o = _ffn_bwd(df3, h3, ab3, act3, ffn_in_shard(w_ffn2_in).T, wf2o.T, "ffn2")
    dx2, do, st3 = mod_bwd(x2, dh3, dx3, norm_ffn2, mods, 7, fprev=o, gk=5, gscale=1.0, name="mod3_bwd")

    dmerged = mm(do, wo.T, out_dtype=BF16, name="w_out_dx")
    dwo = mm(merged, do, ta=True, out_dtype=BF16, i_outer=True, name="w_out_dw", **_TA)
    dpa, dglu, dproj = merge_bwd(proj, pa, glu, dmerged, lax.empty((t, INP), BF16), "merge_bwd")
    dwbg = mm(gy, dglu, ta=True, out_dtype=BF16, i_outer=True, name="w_b_glu_dw", **_TA)
    dgy, (x_wo,) = mm(dglu, wbg.T, out_dtype=BF16, name="w_b_glu_dx", comm=[("xc", _row_parts(dwo))])
    dyb_p = _perm(gelu_bwd(yb, dgy, "gelu_bwd"), t)
    e3, dcsg, ends_b = s5_out_bwd(dyb_p, csg, s2, a_re8, a_im8, "s5_out_bwd")
    linit = s5_scan_init(ends_b, a_re8, a_im8, lseg, True, "s5_init_b")
    du_p, dbsg, dd_row, da8 = s5_scan_bwd(e3, linit, s3, sinit, a_re8, a_im8, bsg, u_p, dyb_p, d_row, "s5_scan_b")
    du = _unperm(du_p, t).astype(BF16)
    da = jnp.sum(da8, axis=1)
    dbb_re, dbb_im = _bsg_diag(dbsg)
    g_lre, g_lim, g_ldt, g_bre, g_bim = s5_vjp((da[:, 0:512].reshape(64, 64), da[:, 512:1024].reshape(64, 64),
                                                dbb_re, dbb_im))
    g_cre, g_cim = _csg_diag(dcsg)

    dwap = mm(ya, dpa, ta=True, out_dtype=BF16, i_outer=True, name="w_a_proj_dw", **_TA)
    dya, (x_wbg,) = mm(dpa, wap.T, out_dtype=BF16, name="w_a_proj_dx", comm=[("xc", _col_parts(dwbg))])
    dy_ssd, dproj, st_sn = ssd_out_bwd(y_ssd, proj, dya, ssd_norm_w, dproj, "ssd_out_bwd")
    dxc, dproj, st_ssd = ssd_bwd(xc, dtraw, hsave, dy_ssd, dtb_row, alog_row, dx_row, expm, tri, dproj, "ssd_bwd")
    dpre, st_cv = conv_bwd_pre(proj, dxc, convw, cb_row, "conv_bwd_pre")
    dproj = conv_bwd_in(dpre, convw, dproj, "conv_bwd_in")
    dproj = lax.dynamic_update_slice(dproj, du, (0, P_U))
    dwinp, (x_wap, x_cw) = mm(h2, dproj, ta=True, out_dtype=BF16, i_outer=True, name="w_in_dw",
                              comm=[("xc", _row_parts(dwap)), ("xc", _col_parts(st_cv[0:CONV_K]))], **_TA)
    dh2, (x_win,) = mm(dproj, winp.T, out_dtype=BF16, tk=5376, name="w_in_dx",
                       comm=[("xc", _col_parts(_unpad_w_in(dwinp)))])
    dx1, df1, st2 = mod_bwd(x1, dh2, dx2, norm_mix, mods, 4, fprev=f1, gk=2, gscale=0.5, name="mod2_bwd")
    dh1, x_f1i, x_f1o = _ffn_bwd(df1, h1, ab1, act1, ffn_in_shard(w_ffn1_in).T, wf1o.T, "ffn1")
    gx, st1 = mod_bwd(xt, dh1, dx1, norm_ffn1, mods, 1, name="mod1_bwd")

    dmods = jnp.concatenate([st1[0], st1[1], st2[3], st2[0], st2[1], st3[3], st3[0], st3[1], st_fin[1]])
    part = {"b_ada": dmods, "norm_ffn1": st1[2], "norm_mix": st2[2], "conv_b": st_cv[4], "dt_bias": st_ssd[1, 0:NH],
            "a_log": st_ssd[0, 0:NH], "d_ssd": st_ssd[2, 0:NH], "ssd_norm_w": st_sn[0], "s5_lambda_re": g_lre,
            "s5_lambda_im": g_lim, "s5_b_re": g_bre, "s5_b_im": g_bim, "s5_c_re": g_cre, "s5_c_im": g_cim,
            "s5_d": dd_row, "s5_log_dt": g_ldt, "norm_ffn2": st3[2], "norm_final": st_fin[0],
            "loss": (0.5 / D) * jnp.sum(st_fin[2])}
    zero = {"loss": jnp.zeros((1,), F32)}
    gath = all_gather(_pack(part), "ag_small")
    sg, sd, sm, sv = adamw(_pack({**small, **zero}), gath, _pack({**{k: args["m_" + k] for k in small}, **zero}),
                           _pack({**{k: args["v_" + k] for k in small}, **zero}), "adamw_small")
    sg, sd, sm, sv = _unpack(sg), _unpack(sd), _unpack(sm), _unpack(sv)
    loss = sg["loss"][0]

    dm_loc = lax.dynamic_slice(gath.reshape(N_DEV, -1)[:, 0:N_ADA * D], (0, me * (N_ADA * D // N_DEV)),
                               (N_DEV, N_ADA * D // N_DEV))
    g_ada = ada_bwd(c8.T, dm_loc, "ada_bwd")
    big = {"w_ada": g_ada[None], "w_ffn1_in": x_f1i, "w_ffn1_out": x_f1o, "w_in": x_win, "conv_w": x_cw,
           "w_a_proj": x_wap, "w_b_glu": x_wbg, "w_out": x_wo, "w_ffn2_in": x_f2i, "w_ffn2_out": x_f2o}
    res = {}
    for k, parts in big.items():
        res[k] = adamw(args[k][0], parts, args["m_" + k][0], args["v_" + k][0], "adamw_" + k)

    names = ["w_ada", "b_ada", "norm_ffn1", "w_ffn1_in", "w_ffn1_out", "norm_mix", "w_in", "conv_w", "conv_b", "dt_bias",
             "a_log", "d_ssd", "ssd_norm_w", "w_a_proj", "s5_lambda_re", "s5_lambda_im", "s5_b_re", "s5_b_im", "s5_c_re",
             "s5_c_im", "s5_d", "s5_log_dt", "w_b_glu", "w_out", "norm_ffn2", "w_ffn2_in", "w_ffn2_out", "norm_final"]
    outs = [loss, gx[None]]
    for q, src in enumerate((sg, sd, sm, sv)):
        for k in names:
            if k in res:
                outs.append(res[k][q][None])
            else:
                outs.append(src[k].reshape(args[k].shape))
    return tuple(outs)
```

```python
import functools
import math

import jax
import jax.numpy as jnp
from jax import lax
from jax.experimental import pallas as pl
from jax.experimental.pallas import tpu as pltpu

F32 = jnp.float32
BF16 = jnp.bfloat16
HI = lax.Precision.HIGHEST

N_DEV = 8
D = 2048
D_FF = 5504
D_FFP = 5632
NH = 32
HP = 64
NG = 4
NST = 128
LCH = 128
CONV_DIM = 3072
CONV_K = 4
S5W = 1024
S5NS = 8192
N_ADA = 9
EPS = 1e-6
IN_COLS = 10272
INP = 10752
P_GATES, P_Z, P_XBC, P_U, P_DT = 0, 4096, 6144, 9216, 10240
O_XBC, O_DT, O_U, O_GA, O_GB = 2048, 5120, 5152, 6176, 8224
NEG = -1e30
VMEM_LIMIT = 56 * 1024 * 1024

ADAM_LR, ADAM_B1, ADAM_B2, ADAM_EPS, ADAM_WD, ADAM_STEP = 0.001, 0.9, 0.999, 1e-08, 0.01, 10


def _cp(*sem):
    return pltpu.CompilerParams(dimension_semantics=sem, vmem_limit_bytes=VMEM_LIMIT)


def _tile(dim, pref):
    if dim <= pref or dim % pref == 0:
        return min(dim, pref)
    for t in (2048, 1024, 512, 256, 128):
        if t <= pref and dim % t == 0:
            return t
    return dim


def _vec(w, cb=0):
    return pl.BlockSpec((1, w), lambda *_: (0, cb))


def _row(tm, w, cb=0):
    return pl.BlockSpec((tm, w), lambda i: (i, cb))


def _stats(w):
    return pl.BlockSpec((8, w), lambda *_: (0, 0))


_HBM = pl.BlockSpec(memory_space=pl.ANY)


def _sigmoid(x):
    return 1.0 / (1.0 + jnp.exp(-x))


def _softplus(x):
    return jnp.maximum(x, 0.0) + jnp.log1p(jnp.exp(-jnp.abs(x)))


def _peer(k):
    x, y, c = lax.axis_index("x"), lax.axis_index("y"), lax.axis_index("c")
    return (x ^ ((k >> 2) & 1), y ^ ((k >> 1) & 1), c ^ (k & 1))


def _my_id():
    return 4 * lax.axis_index("x") + 2 * lax.axis_index("y") + lax.axis_index("c")


def _comm_out_shape(kind, v):
    shape = {"ag": (N_DEV,) + v.shape, "xc": v.shape, "agc": (v.shape[0], N_DEV * v.shape[1]),
             "xcc": (N_DEV, v.shape[0], v.shape[1] // N_DEV)}[kind]
    return jax.ShapeDtypeStruct(shape, v.dtype)


def _comm_scratch(n):
    return [pltpu.SemaphoreType.DMA((n * N_DEV,)), pltpu.SemaphoreType.DMA((n * N_DEV,))]


class _Comm:
    def __init__(self, kinds, srcs, dsts, send_sems, recv_sems):
        self.items = list(zip(kinds, srcs, dsts))
        self.send_sems, self.recv_sems = send_sems, recv_sems
        x, y, c = lax.axis_index("x"), lax.axis_index("y"), lax.axis_index("c")
        self.me = 4 * x + 2 * y + c
        self.sib = (x, y, 1 - c)
        self.chips = [(1 - x, y), (x, 1 - y), (1 - x, 1 - y)]
        self.c = c

    @staticmethod
    def _id(p):
        return 4 * p[0] + 2 * p[1] + p[2]

    def _src(self, q, d):
        kind, src, _ = self.items[q]
        if kind == "xc":
            return src.at[d]
        if kind == "xcc":
            w = src.shape[1] // N_DEV
            return src.at[:, pl.ds(pl.multiple_of(d * w, 128), w)]
        return src

    def _slot(self, q, d):
        kind, _, dst = self.items[q]
        if kind == "agc":
            w = dst.shape[1] // N_DEV
            return dst.at[:, pl.ds(pl.multiple_of(d * w, 128), w)]
        return dst.at[d]

    def _push(self, q, k, src, slot, to):
        return pltpu.make_async_remote_copy(
            src_ref=src, dst_ref=self._slot(q, slot), send_sem=self.send_sems.at[q * N_DEV + k],
            recv_sem=self.recv_sems.at[q * N_DEV + k], device_id=to, device_id_type=pl.DeviceIdType.MESH)

    def _local(self, q):
        return pltpu.make_async_copy(self._src(q, self.me), self._slot(q, self.me), self.send_sems.at[q * N_DEV])

    def _direct(self, q):
        kind = self.items[q][0]
        if kind in ("xc", "xcc"):
            out = []
            for k in range(1, N_DEV):
                p = _peer(k)
                out.append((k, self._push(q, k, self._src(q, self._id(p)), self.me, p)))
            return out
        src = self.items[q][1]
        out = [(1, self._push(q, 1, src, self.me, self.sib))]
        for j, chip in enumerate(self.chips):
            out.append((2 + j, self._push(q, 2 + j, src, self.me, (*chip, self.c))))
        return out

    def _forwards(self, q):
        out = []
        for j, chip in enumerate(self.chips):
            slot = self._id((*chip, self.c))
            out.append((2 + j, 5 + j, self._push(q, 5 + j, self._slot(q, slot), slot, self.sib)))
        return out

    def start(self):
        for q in range(len(self.items)):
            self._local(q).start()
            for _, cp in self._direct(q):
                cp.start()

    def forward(self):
        for q, (kind, _, _) in enumerate(self.items):
            if kind not in ("ag", "agc"):
                continue
            for k_in, _, fwd in self._forwards(q):
                self._push(q, k_in, self._slot(q, self.me), self.me, self.sib).wait_recv()
                fwd.start()

    def finish(self):
        for q, (kind, _, _) in enumerate(self.items):
            self._local(q).wait()
            if kind in ("xc", "xcc"):
                for _, cp in self._direct(q):
                    cp.wait()
                continue
            for k, cp in self._direct(q):
                cp.wait_send()
                if k == 1:
                    cp.wait_recv()
            for _, _, fwd in self._forwards(q):
                fwd.wait()


def comm_call(items, name):
    kinds = [k for k, _ in items]
    n = len(items)

    def body(*refs):
        cm = _Comm(kinds, refs[:n], refs[n:2 * n], refs[2 * n], refs[2 * n + 1])
        cm.start()
        cm.forward()
        cm.finish()

    return pl.pallas_call(
        body, name=name,
        in_specs=[pl.BlockSpec(memory_space=pl.ANY)] * n, out_specs=[pl.BlockSpec(memory_space=pl.ANY)] * n,
        out_shape=[_comm_out_shape(k, v) for k, v in items], scratch_shapes=_comm_scratch(n),
    )(*[v for _, v in items])


def all_gather(v, name):
    return comm_call([("ag", v)], name)[0]


def _pcall(body, args, *, name, grid, in_specs, out_specs, out_shape, scratch_shapes=(), sem, comm=(), fwd=0.85):
    nc, n_in, n_out = len(comm), len(in_specs), len(out_shape)
    if not nc:
        return pl.pallas_call(body, name=name, grid=grid, in_specs=list(in_specs), out_specs=list(out_specs),
                              out_shape=list(out_shape), scratch_shapes=list(scratch_shapes),
                              compiler_params=_cp(*sem))(*args)
    kinds = [k for k, _ in comm]
    steps = math.prod(grid)
    fwd_step = min(int(fwd * steps), steps - 1)

    def carried(*refs):
        ins, csrc = refs[:n_in], refs[n_in:n_in + nc]
        outs, cdst = refs[n_in + nc:n_in + nc + n_out], refs[n_in + nc + n_out:n_in + 2 * nc + n_out]
        scr = refs[n_in + 2 * nc + n_out:]
        cm = _Comm(kinds, csrc, cdst, scr[-2], scr[-1])
        step = 0
        for d, g in enumerate(grid):
            step = step * g + pl.program_id(d)

        @pl.when(step == 0)
        def _():
            cm.start()

        body(*ins, *outs, *scr[:-2])

        @pl.when(step == fwd_step)
        def _():
            cm.forward()

        @pl.when(step == steps - 1)
        def _():
            cm.finish()

    hbm = pl.BlockSpec(memory_space=pl.ANY)
    out = pl.pallas_call(
        carried, name=name, grid=grid, in_specs=list(in_specs) + [hbm] * nc, out_specs=list(out_specs) + [hbm] * nc,
        out_shape=list(out_shape) + [_comm_out_shape(k, v) for k, v in comm],
        scratch_shapes=list(scratch_shapes) + _comm_scratch(nc), compiler_params=_cp(*(("arbitrary",) * len(grid))),
    )(*args, *[v for _, v in comm])
    return list(out[:n_out]), list(out[n_out:])


def mm(a, b, *, ta=False, out_dtype=F32, tm=512, tn=1024, tk=2048, i_outer=False, a_halves=False, b_halves=False,
       name, comm=()):
    if a_halves:
        m, kd = a.shape[1], 2 * a.shape[2]
    elif ta:
        kd, m = a.shape
    else:
        m, kd = a.shape
    kd2, n = (b.shape[1], 2 * b.shape[2]) if b_halves else b.shape
    assert kd == kd2 and not (ta and a_halves), (a.shape, b.shape, ta)
    tm, tn, tk = _tile(m, tm), _tile(n // 2 if b_halves else n, tn), _tile(kd // 2 if a_halves else kd, tk)
    nk = kd // tk
    nkh, njh = nk // 2, n // tn // 2
    grid = (m // tm, n // tn, nk) if i_outer else (n // tn, m // tm, nk)
    dims = (((0,) if ta else (1,), (0,)), ((), ()))

    def ix(f):
        return (lambda i, j, k: f(i, j, k)) if i_outer else (lambda j, i, k: f(i, j, k))

    def body(a_ref, b_ref, o_ref, *scr):
        p = lax.dot_general(a_ref[...], b_ref[...], dims, preferred_element_type=F32)
        if nk == 1:
            o_ref[...] = p.astype(o_ref.dtype)
        else:
            acc = scr[0]
            k = pl.program_id(2)

            @pl.when(k == 0)
            def _():
                acc[...] = p

            @pl.when(k > 0)
            def _():
                acc[...] += p

            @pl.when(k == nk - 1)
            def _():
                o_ref[...] = acc[...].astype(o_ref.dtype)

    if a_halves:
        a_spec = pl.BlockSpec((None, tm, tk), ix(lambda i, j, k: (k // nkh, i, k % nkh)))
    elif ta:
        a_spec = pl.BlockSpec((tk, tm), ix(lambda i, j, k: (k, i)))
    else:
        a_spec = pl.BlockSpec((tm, tk), ix(lambda i, j, k: (i, k)))
    if b_halves:
        b_spec = pl.BlockSpec((None, tk, tn), ix(lambda i, j, k: (j // njh, k, j % njh)))
    else:
        b_spec = pl.BlockSpec((tk, tn), ix(lambda i, j, k: (k, j)))
    out = _pcall(body, (a, b), name=name, grid=grid, in_specs=[a_spec, b_spec],
                 out_specs=[pl.BlockSpec((tm, tn), ix(lambda i, j, k: (i, j)))],
                 out_shape=[jax.ShapeDtypeStruct((m, n), out_dtype)],
                 scratch_shapes=[pltpu.VMEM((tm, tn), F32)] if nk > 1 else [],
                 sem=("parallel", "parallel", "arbitrary"), comm=comm)
    return (out[0][0], out[1]) if comm else out[0]


def ffn_in_act(h, w, name, comm=(), fwd=0.85):
    t = h.shape[0]
    tm, tn = _tile(t, 512), 512
    nj = D_FFP // tn

    def body(h_ref, wa_ref, wb_ref, ab_ref, act_ref):
        hv = h_ref[...]
        pa = _dot(hv, wa_ref[...])
        pb = _dot(hv, wb_ref[...])
        ab_ref[0] = pa.astype(BF16)
        ab_ref[1] = pb.astype(BF16)
        act_ref[...] = (pa * _sigmoid(pa) * pb).astype(BF16)

    out = _pcall(body, (h, w, w), name=name, grid=(nj, t // tm),
                 in_specs=[pl.BlockSpec((tm, D), lambda j, i: (i, 0)), pl.BlockSpec((D, tn), lambda j, i: (0, j)),
                           pl.BlockSpec((D, tn), lambda j, i: (0, nj + j))],
                 out_specs=[pl.BlockSpec((2, tm, tn), lambda j, i: (0, i, j)), pl.BlockSpec((tm, tn), lambda j, i: (i, j))],
                 out_shape=[jax.ShapeDtypeStruct((2, t, D_FFP), BF16), jax.ShapeDtypeStruct((t, D_FFP), BF16)],
                 sem=("parallel", "parallel"), comm=comm, fwd=fwd)
    return (out[0][0], out[0][1], out[1]) if comm else (out[0], out[1])


def ffn_dab(df, w_out_t, ab, name, comm=()):
    t = df.shape[0]
    tm, tn = _tile(t, 1024), 512

    def body(d_ref, w_ref, ab_ref, o_ref):
        dv = _dot(d_ref[...], w_ref[...])
        a = ab_ref[0].astype(F32)
        b = ab_ref[1].astype(F32)
        s = _sigmoid(a)
        o_ref[0] = (dv * b * (s * (1.0 + a * (1.0 - s)))).astype(BF16)
        o_ref[1] = (dv * (a * s)).astype(BF16)

    out = _pcall(body, (df, w_out_t, ab), name=name, grid=(D_FFP // tn, t // tm),
                 in_specs=[pl.BlockSpec((tm, D), lambda j, i: (i, 0)), pl.BlockSpec((D, tn), lambda j, i: (0, j)),
                           pl.BlockSpec((2, tm, tn), lambda j, i: (0, i, j))],
                 out_specs=[pl.BlockSpec((2, tm, tn), lambda j, i: (0, i, j))],
                 out_shape=[jax.ShapeDtypeStruct((2, t, D_FFP), BF16)], sem=("parallel", "parallel"), comm=comm, fwd=0.85)
    return (out[0][0], out[1]) if comm else out[0]


def ada_fwd(c8, w_loc, b_loc, name):
    n = w_loc.shape[1]
    tn = 256

    def body(c_ref, w_ref, b_ref, o_ref):
        cv = c_ref[...]
        ca = cv * _sigmoid(cv)
        o_ref[...] = jnp.dot(ca, w_ref[...], precision=HI, preferred_element_type=F32) + b_ref[...]

    return pl.pallas_call(
        body, name=name, grid=(n // tn,),
        in_specs=[pl.BlockSpec((N_DEV, D), lambda j: (0, 0)), pl.BlockSpec((D, tn), lambda j: (0, j)),
                  pl.BlockSpec((1, tn), lambda j: (0, j))],
        out_specs=pl.BlockSpec((N_DEV, tn), lambda j: (0, j)),
        out_shape=jax.ShapeDtypeStruct((N_DEV, n), F32), compiler_params=_cp("parallel"),
    )(c8, w_loc, b_loc)


def ada_bwd(c8t, dm_loc, name):
    n = dm_loc.shape[1]
    tn = 256

    def body(c_ref, d_ref, o_ref):
        cv = c_ref[...]
        ca = cv * _sigmoid(cv)
        o_ref[...] = jnp.dot(ca, d_ref[...], precision=HI, preferred_element_type=F32)

    return pl.pallas_call(
        body, name=name, grid=(n // tn,),
        in_specs=[pl.BlockSpec((D, N_DEV), lambda j: (0, 0)), pl.BlockSpec((N_DEV, tn), lambda j: (0, j))],
        out_specs=pl.BlockSpec((D, tn), lambda j: (0, j)),
        out_shape=jax.ShapeDtypeStruct((D, n), F32), compiler_params=_cp("parallel"),
    )(c8t, dm_loc)


def mod_fwd(x, nw, mods, shk, sck, *, f=None, gk=None, gscale=1.0, name, comm=()):
    t = x.shape[0]
    tm = min(512, t)
    res = f is not None

    def body(*refs):
        if res:
            x_ref, f_ref, g_ref, nw_ref, sh_ref, sc_ref, x1_ref, h_ref = refs
            xv = x_ref[...] + (gscale * g_ref[...]) * f_ref[...].astype(F32)
            x1_ref[...] = xv
        else:
            x_ref, nw_ref, sh_ref, sc_ref, h_ref = refs
            xv = x_ref[...]
        r = lax.rsqrt(jnp.mean(xv * xv, axis=-1, keepdims=True) + EPS)
        h_ref[...] = ((xv * r * nw_ref[...]) * (1.0 + sc_ref[...]) + sh_ref[...]).astype(BF16)

    ins = [x] + ([f, mods] if res else []) + [nw, mods, mods]
    specs = [_row(tm, D)] + ([_row(tm, D), _vec(D, gk)] if res else []) + [_vec(D), _vec(D, shk), _vec(D, sck)]
    outs = ([jax.ShapeDtypeStruct((t, D), F32)] if res else []) + [jax.ShapeDtypeStruct((t, D), BF16)]
    ospecs = ([_row(tm, D)] if res else []) + [_row(tm, D)]
    out = _pcall(body, ins, name=name, grid=(t // tm,), in_specs=specs, out_specs=ospecs, out_shape=outs,
                 sem=("parallel",), comm=comm)
    if comm:
        return (out[0] if res else out[0][0]), out[1]
    return out if res else out[0]


def final_fwd_bwd(x2, f3, mods, nf, tgt, name):
    t = x2.shape[0]
    tm = min(256, t)

    def body(x_ref, f_ref, g_ref, nf_ref, t_ref, dx_ref, df_ref, st_ref):
        @pl.when(pl.program_id(0) == 0)
        def _():
            st_ref[...] = jnp.zeros_like(st_ref)

        g = 0.5 * g_ref[...]
        fv = f_ref[...].astype(F32)
        xv = x_ref[...] + g * fv
        r = lax.rsqrt(jnp.mean(xv * xv, axis=-1, keepdims=True) + EPS)
        xh = xv * r
        nfv = nf_ref[...]
        e = xh * nfv - t_ref[...]
        st_ref[2:3, :] += jnp.sum(e * e, axis=0, keepdims=True)
        dy = e * (1.0 / D)
        st_ref[0:1, :] += jnp.sum(dy * xh, axis=0, keepdims=True)
        dxh = dy * nfv
        dx = r * (dxh - xh * jnp.mean(dxh * xh, axis=-1, keepdims=True))
        dx_ref[...] = dx
        df_ref[...] = (g * dx).astype(BF16)
        st_ref[1:2, :] += 0.5 * jnp.sum(fv * dx, axis=0, keepdims=True)

    return pl.pallas_call(
        body, name=name, grid=(t // tm,),
        in_specs=[_row(tm, D), _row(tm, D), _vec(D, 8), _vec(D), _row(tm, D)],
        out_specs=[_row(tm, D), _row(tm, D), _stats(D)],
        out_shape=[jax.ShapeDtypeStruct((t, D), F32), jax.ShapeDtypeStruct((t, D), BF16),
                   jax.ShapeDtypeStruct((8, D), F32)],
        compiler_params=_cp("arbitrary"),
    )(x2, f3, mods, nf, tgt)


def mod_bwd(x_in, dh, dx_out, nw, mods, sck, *, fprev=None, gk=None, gscale=1.0, name):
    t = x_in.shape[0]
    tm = min(256, t)
    gate = fprev is not None

    def body(*refs):
        if gate:
            x_ref, dh_ref, dxo_ref, nw_ref, sc_ref, f_ref, g_ref, dx_ref, df_ref, st_ref = refs
        else:
            x_ref, dh_ref, dxo_ref, nw_ref, sc_ref, dx_ref, st_ref = refs

        @pl.when(pl.program_id(0) == 0)
        def _():
            st_ref[...] = jnp.zeros_like(st_ref)

        xv = x_ref[...]
        dhv = dh_ref[...].astype(F32)
        r = lax.rsqrt(jnp.mean(xv * xv, axis=-1, keepdims=True) + EPS)
        xh = xv * r
        nwv = nw_ref[...]
        st_ref[0:1, :] += jnp.sum(dhv, axis=0, keepdims=True)
        st_ref[1:2, :] += jnp.sum(dhv * (xh * nwv), axis=0, keepdims=True)
        dn = dhv * (1.0 + sc_ref[...])
        st_ref[2:3, :] += jnp.sum(dn * xh, axis=0, keepdims=True)
        dxh = dn * nwv
        dx = dxo_ref[...] + r * (dxh - xh * jnp.mean(dxh * xh, axis=-1, keepdims=True))
        dx_ref[...] = dx
        if gate:
            df_ref[...] = ((gscale * g_ref[...]) * dx).astype(BF16)
            st_ref[3:4, :] += gscale * jnp.sum(f_ref[...].astype(F32) * dx, axis=0, keepdims=True)

    ins = [x_in, dh, dx_out, nw, mods] + ([fprev, mods] if gate else [])
    specs = [_row(tm, D), _row(tm, D), _row(tm, D), _vec(D), _vec(D, sck)] + ([_row(tm, D), _vec(D, gk)] if gate else [])
    outs = [jax.ShapeDtypeStruct((t, D), F32)] + ([jax.ShapeDtypeStruct((t, D), BF16)] if gate else []) + \
        [jax.ShapeDtypeStruct((8, D), F32)]
    ospecs = [_row(tm, D)] + ([_row(tm, D)] if gate else []) + [_stats(D)]
    return pl.pallas_call(body, name=name, grid=(t // tm,), in_specs=specs, out_specs=ospecs, out_shape=outs,
                          compiler_params=_cp("arbitrary"))(*ins)


def _conv_pre(cur, prev8, w, b, tm):
    full = jnp.concatenate([prev8, cur], axis=0)
    pre = b + w[3:4, :] * cur
    for k in range(CONV_K - 1):
        s = CONV_K - 1 - k
        pre = pre + w[k:k + 1, :] * pltpu.roll(full, s, 0)[8:8 + tm, :]
    return pre


def conv_fwd(proj, cw_full, cb_full, name):
    t = proj.shape[0]
    tm = min(512, t)
    cwid = 1024
    cb0 = P_XBC // cwid

    def body(x_ref, p_ref, w_ref, b_ref, o_ref):
        i = pl.program_id(1)
        prev8 = jnp.where(i == 0, 0.0, p_ref[...].astype(F32)[8:16])
        pre = _conv_pre(x_ref[...].astype(F32), prev8, w_ref[...], b_ref[...], tm)
        o_ref[...] = pre * _sigmoid(pre)

    return pl.pallas_call(
        body, name=name, grid=(CONV_DIM // cwid, t // tm),
        in_specs=[pl.BlockSpec((tm, cwid), lambda j, i: (i, cb0 + j)),
                  pl.BlockSpec((16, cwid), lambda j, i: (jnp.maximum(i * (tm // 16) - 1, 0), cb0 + j)),
                  pl.BlockSpec((CONV_K, cwid), lambda j, i: (0, j)), pl.BlockSpec((1, cwid), lambda j, i: (0, j))],
        out_specs=pl.BlockSpec((tm, cwid), lambda j, i: (i, j)),
        out_shape=jax.ShapeDtypeStruct((t, CONV_DIM), F32), compiler_params=_cp("parallel", "parallel"),
    )(proj, proj, cw_full, cb_full)


def conv_bwd_pre(proj, dxc, cw_full, cb_full, name):
    t = proj.shape[0]
    tm = min(512, t)
    cwid = 1024
    cb0 = P_XBC // cwid

    def body(x_ref, p_ref, d_ref, w_ref, b_ref, o_ref, st_ref):
        i = pl.program_id(1)

        @pl.when(i == 0)
        def _():
            st_ref[...] = jnp.zeros_like(st_ref)

        cur = x_ref[...].astype(F32)
        prev8 = jnp.where(i == 0, 0.0, p_ref[...].astype(F32)[8:16])
        pre = _conv_pre(cur, prev8, w_ref[...], b_ref[...], tm)
        s = _sigmoid(pre)
        dpre = d_ref[...] * (s * (1.0 + pre * (1.0 - s)))
        o_ref[...] = dpre
        st_ref[4:5, :] += jnp.sum(dpre, axis=0, keepdims=True)
        st_ref[3:4, :] += jnp.sum(dpre * cur, axis=0, keepdims=True)
        full = jnp.concatenate([prev8, cur], axis=0)
        for k in range(CONV_K - 1):
            sft = CONV_K - 1 - k
            st_ref[k:k + 1, :] += jnp.sum(dpre * pltpu.roll(full, sft, 0)[8:8 + tm, :], axis=0, keepdims=True)

    return pl.pallas_call(
        body, name=name, grid=(CONV_DIM // cwid, t // tm),
        in_specs=[pl.BlockSpec((tm, cwid), lambda j, i: (i, cb0 + j)),
                  pl.BlockSpec((16, cwid), lambda j, i: (jnp.maximum(i * (tm // 16) - 1, 0), cb0 + j)),
                  pl.BlockSpec((tm, cwid), lambda j, i: (i, j)),
                  pl.BlockSpec((CONV_K, cwid), lambda j, i: (0, j)), pl.BlockSpec((1, cwid), lambda j, i: (0, j))],
        out_specs=[pl.BlockSpec((tm, cwid), lambda j, i: (i, j)), pl.BlockSpec((8, cwid), lambda j, i: (0, j))],
        out_shape=[jax.ShapeDtypeStruct((t, CONV_DIM), F32), jax.ShapeDtypeStruct((8, CONV_DIM), F32)],
        compiler_params=_cp("parallel", "arbitrary"),
    )(proj, proj, dxc, cw_full, cb_full)


def conv_bwd_in(dpre, cw_full, dproj, name):
    t = dpre.shape[0]
    tm = min(512, t)
    cwid = 1024
    nt = t // tm

    def body(d_ref, n_ref, w_ref, _, o_ref):
        i = pl.program_id(1)
        cur = d_ref[...]
        nxt = jnp.where(i == nt - 1, 0.0, n_ref[...])
        full = jnp.concatenate([cur, nxt], axis=0)
        w = w_ref[...]
        acc = w[3:4, :] * cur
        for k in range(CONV_K - 1):
            s = CONV_K - 1 - k
            acc = acc + w[k:k + 1, :] * pltpu.roll(full, tm + 8 - s, 0)[0:tm, :]
        o_ref[...] = acc.astype(BF16)

    return pl.pallas_call(
        body, name=name, grid=(CONV_DIM // cwid, nt),
        in_specs=[pl.BlockSpec((tm, cwid), lambda j, i: (i, j)),
                  pl.BlockSpec((8, cwid), lambda j, i: (jnp.minimum((i + 1) * (tm // 8), t // 8 - 1), j)),
                  pl.BlockSpec((CONV_K, cwid), lambda j, i: (0, j)), _HBM],
        out_specs=pl.BlockSpec((tm, cwid), lambda j, i: (i, P_XBC // cwid + j)),
        out_shape=jax.ShapeDtypeStruct(dproj.shape, BF16), input_output_aliases={3: 0},
        compiler_params=_cp("parallel", "parallel"),
    )(dpre, dpre, cw_full, dproj)


def _nt(a, b):
    return lax.dot_general(a, b, (((1,), (1,)), ((), ())), preferred_element_type=F32)


def _dot(a, b):
    return jnp.dot(a, b, preferred_element_type=F32)


def _head_lanes():
    return lax.broadcasted_iota(jnp.int32, (1, 128), 1) < NH


def _expand_heads(x, e3):
    x = jnp.where(_head_lanes(), x, 0.0)
    hi = x.astype(BF16).astype(F32)
    r1 = x - hi
    mid = r1.astype(BF16).astype(F32)
    packed = hi + pltpu.roll(mid, NH, 1) + pltpu.roll(r1 - mid, 2 * NH, 1)
    return _dot(packed.astype(BF16), e3)


def _reduce_heads(v, e3):
    hi = v.astype(BF16)
    lo = (v - hi.astype(F32)).astype(BF16)
    return jnp.where(_head_lanes(), _nt(hi, e3) + _nt(lo, e3), 0.0)


def _ssd_common(dt_ref, dtb_ref, al_ref, exp_ref, tri_ref):
    a_row = jnp.where(_head_lanes(), -jnp.exp(al_ref[...]), 0.0)
    zraw = dt_ref[...] + dtb_ref[...]
    dtv = _softplus(zraw)
    cs = jnp.dot(tri_ref[...], dtv * a_row, precision=HI, preferred_element_type=F32)
    e3 = exp_ref[...]
    return a_row, zraw, dtv, cs, _expand_heads(cs, e3), _expand_heads(dtv, e3)


def ssd_fwd(xc, proj, dtb_row, alog_row, dx_row, expm, tri, name):
    t = xc.shape[0]
    nc = t // LCH

    def body(xs_ref, bm_ref, cm_ref, dt_ref, dtb_ref, al_ref, dxr_ref, exp_ref, tri_ref, y_ref, hs_ref, h_scr):
        @pl.when(pl.program_id(0) == 0)
        def _():
            h_scr[...] = jnp.zeros_like(h_scr)

        _, _, _, cs, csx, dtx = _ssd_common(dt_ref, dtb_ref, al_ref, exp_ref, tri_ref)
        cst = cs.T
        csl = csx[LCH - 1:LCH, :]
        xs = xs_ref[...]
        xd = xs * dtx
        xdw = xd * jnp.exp(csl - csx)
        ecs = jnp.exp(csx)
        ecl = jnp.exp(csl)
        tril = lax.broadcasted_iota(jnp.int32, (LCH, LCH), 0) >= lax.broadcasted_iota(jnp.int32, (LCH, LCH), 1)
        hs_ref[...] = h_scr[...]
        for g in range(NG):
            gc = slice(g * 512, (g + 1) * 512)
            bm = bm_ref[:, g * NST:(g + 1) * NST]
            cmb = cm_ref[:, g * NST:(g + 1) * NST].astype(BF16)
            gm = _nt(cmb, bm.astype(BF16))
            hg = h_scr[:, gc]
            yo = _dot(cmb, hg.astype(BF16)) * ecs[:, gc]
            st = _dot(bm.T.astype(BF16), xdw[:, gc].astype(BF16))
            for r in range(8):
                h = g * 8 + r
                hc = slice(h * HP, (h + 1) * HP)
                seg = cs[:, h:h + 1] - cst[h:h + 1, :]
                m = (gm * jnp.exp(jnp.where(tril, seg, NEG))).astype(BF16)
                yd = _dot(m, xd[:, hc].astype(BF16))
                y_ref[:, hc] = yd + yo[:, r * HP:(r + 1) * HP] + dxr_ref[:, hc] * xs[:, hc]
            h_scr[:, gc] = ecl[:, gc] * hg + st

    return pl.pallas_call(
        body, name=name, grid=(nc,),
        in_specs=[pl.BlockSpec((LCH, 2048), lambda c: (c, 0)), pl.BlockSpec((LCH, 512), lambda c: (c, 4)),
                  pl.BlockSpec((LCH, 512), lambda c: (c, 5)), pl.BlockSpec((LCH, 128), lambda c: (c, 0)),
                  _vec(128), _vec(128), _vec(2048), pl.BlockSpec((128, 2048), lambda c: (0, 0)),
                  pl.BlockSpec((LCH, LCH), lambda c: (0, 0))],
        out_specs=[pl.BlockSpec((LCH, 2048), lambda c: (c, 0)), pl.BlockSpec((None, NST, 2048), lambda c: (c, 0, 0))],
        out_shape=[jax.ShapeDtypeStruct((t, 2048), F32), jax.ShapeDtypeStruct((nc, NST, 2048), F32)],
        scratch_shapes=[pltpu.VMEM((NST, 2048), F32)],
        compiler_params=_cp("arbitrary"),
    )(xc, xc, xc, proj, dtb_row, alog_row, dx_row, expm, tri)


def ssd_bwd(xc, proj, hsave, dy, dtb_row, alog_row, dx_row, expm, tri, dproj, name):
    t = xc.shape[0]
    nc = t // LCH

    def body(xs_ref, bm_ref, cm_ref, dt_ref, hs_ref, dy_ref, dtb_ref, al_ref, dxr_ref, exp_ref, tri_ref, _,
             dxc_ref, ddt_ref, st_ref, dh_scr, dxd_scr, dcsx_scr):
        @pl.when(pl.program_id(0) == 0)
        def _():
            dh_scr[...] = jnp.zeros_like(dh_scr)
            st_ref[...] = jnp.zeros_like(st_ref)

        a_row, zraw, dtv, cs, csx, dtx = _ssd_common(dt_ref, dtb_ref, al_ref, exp_ref, tri_ref)
        e = exp_ref[...]
        cst = cs.T
        csl = csx[LCH - 1:LCH, :]
        xs = xs_ref[...]
        xd = xs * dtx
        wend = jnp.exp(csl - csx)
        xdw = xd * wend
        ecs = jnp.exp(csx)
        ecl = jnp.exp(csl)
        ri = lax.broadcasted_iota(jnp.int32, (LCH, LCH), 0)
        ci = lax.broadcasted_iota(jnp.int32, (LCH, LCH), 1)
        tril = ri >= ci
        triu = ri <= ci
        lane = lax.broadcasted_iota(jnp.int32, (1, 128), 1)
        dyv = dy_ref[...]
        dxr = dxr_ref[...]
        st_ref[2:3, :] += _reduce_heads(jnp.sum(dyv * xs, axis=0, keepdims=True), e)
        dcs = jnp.zeros((LCH, 128), F32)
        for g in range(NG):
            gc = slice(g * 512, (g + 1) * 512)
            bmb = bm_ref[:, g * NST:(g + 1) * NST].astype(BF16)
            cm = cm_ref[:, g * NST:(g + 1) * NST]
            cmb = cm.astype(BF16)
            hg = hs_ref[:, gc]
            hgb = hg.astype(BF16)
            dhc = dh_scr[:, gc]
            dhcb = dhc.astype(BF16)
            dyg = dyv[:, gc]
            yo = _dot(cmb, hgb) * ecs[:, gc]
            dq = (dyg * ecs[:, gc]).astype(BF16)
            dcm = _nt(dq, hgb)
            dh_yo = _dot(cm.T.astype(BF16), dq)
            dxdw = _dot(bmb, dhcb)
            dbm = _nt(xdw[:, gc].astype(BF16), dhcb)
            tt = dxdw * xdw[:, gc]
            dcsx_g = dyg * yo - tt
            dcsl_g = jnp.sum(tt, axis=0, keepdims=True) + jnp.sum(dhc * hg, axis=0, keepdims=True) * ecl[:, gc]
            dxd_scr[:, gc] = dxdw * wend[:, gc]
            dh_scr[:, gc] = ecl[:, gc] * dhc + dh_yo
            gm = _nt(cmb, bmb)
            gmt = _nt(bmb, cmb)
            dg = jnp.zeros((LCH, LCH), F32)
            dgt = jnp.zeros((LCH, LCH), F32)
            for r in range(8):
                h = g * 8 + r
                hc = slice(h * HP, (h + 1) * HP)
                seg = cs[:, h:h + 1] - cst[h:h + 1, :]
                lm = jnp.exp(jnp.where(tril, seg, NEG))
                lmt = jnp.exp(jnp.where(triu, -seg, NEG))
                mm_ = gm * lm
                mmt = gmt * lmt
                xdh = xd[:, hc].astype(BF16)
                dyh = dyv[:, hc].astype(BF16)
                dm = _nt(dyh, xdh)
                dmt = _nt(xdh, dyh)
                dxd_scr[:, hc] += _dot(mmt.astype(BF16), dyh)
                rs = jnp.sum(dm * mm_, axis=1, keepdims=True) - jnp.sum(dmt * mmt, axis=1, keepdims=True)
                dcs = dcs + rs * jnp.where(lane == h, 1.0, 0.0)
                dg = dg + dm * lm
                dgt = dgt + dmt * lmt
            dcm = dcm + _dot(dg.astype(BF16), bmb)
            dbm = dbm + _dot(dgt.astype(BF16), cmb)
            dxc_ref[:, 2048 + g * NST:2048 + (g + 1) * NST] = dbm
            dxc_ref[:, 2560 + g * NST:2560 + (g + 1) * NST] = dcm
            dcsx_scr[:, gc] = dcsx_g
            dcsx_scr[LCH - 1:LCH, gc] += dcsl_g
        dxd = dxd_scr[...]
        dxc_ref[:, 0:2048] = dxr * dyv + dxd * dtx
        ddtv = _reduce_heads(dxd * xs, e)
        dcs = dcs + _reduce_heads(dcsx_scr[...], e)
        dda =lax.dot_general(tri_ref[...], dcs, (((0,), (0,)), ((), ())), precision=HI, preferred_element_type=F32)
        ddtv = ddtv + dda * a_row
        st_ref[0:1, :] += jnp.sum(dda * dtv, axis=0, keepdims=True) * a_row
        ddt = ddtv * _sigmoid(zraw)
        ddt_ref[:, 0:128] = ddt.astype(BF16)
        ddt_ref[:, 128:INP - P_DT] = jnp.zeros((LCH, INP - P_DT - 128), BF16)
        st_ref[1:2, :] += jnp.sum(ddt, axis=0, keepdims=True)

    rc = lambda c: nc - 1 - c
    return pl.pallas_call(
        body, name=name, grid=(nc,),
        in_specs=[pl.BlockSpec((LCH, 2048), lambda c: (rc(c), 0)), pl.BlockSpec((LCH, 512), lambda c: (rc(c), 4)),
                  pl.BlockSpec((LCH, 512), lambda c: (rc(c), 5)),
                  pl.BlockSpec((LCH, 128), lambda c: (rc(c), 0)),
                  pl.BlockSpec((None, NST, 2048), lambda c: (rc(c), 0, 0)),
                  pl.BlockSpec((LCH, 2048), lambda c: (rc(c), 0)),
                  _vec(128), _vec(128), _vec(2048), pl.BlockSpec((128, 2048), lambda c: (0, 0)),
                  pl.BlockSpec((LCH, LCH), lambda c: (0, 0)), _HBM],
        out_specs=[pl.BlockSpec((LCH, CONV_DIM), lambda c: (rc(c), 0)),
                   pl.BlockSpec((LCH, INP - P_DT), lambda c: (rc(c), P_DT // (INP - P_DT))), _stats(128)],
        out_shape=[jax.ShapeDtypeStruct((t, CONV_DIM), F32), jax.ShapeDtypeStruct(dproj.shape, BF16),
                   jax.ShapeDtypeStruct((8, 128), F32)],
        scratch_shapes=[pltpu.VMEM((NST, 2048), F32), pltpu.VMEM((LCH, 2048), F32), pltpu.VMEM((LCH, 2048), F32)],
        input_output_aliases={11: 1}, compiler_params=_cp("arbitrary"),
    )(xc, xc, xc, proj, hsave, dy, dtb_row, alog_row, dx_row, expm, tri, dproj)


def ssd_out_fwd(y, proj, nw, name):
    t = y.shape[0]
    tm = min(512, t)

    def body(y_ref, z_ref, nw_ref, o_ref):
        for g in range(NG):
            gc = slice(g * 512, (g + 1) * 512)
            z = z_ref[:, gc].astype(F32)
            yz = y_ref[:, gc] * (z * _sigmoid(z))
            r = lax.rsqrt(jnp.mean(yz * yz, axis=-1, keepdims=True) + EPS)
            o_ref[:, gc] = (yz * r * nw_ref[:, gc]).astype(BF16)

    return pl.pallas_call(body, name=name, grid=(t // tm,),
                          in_specs=[_row(tm, 2048), _row(tm, 2048, P_Z // 2048), _vec(2048)],
                          out_specs=_row(tm, 2048), out_shape=jax.ShapeDtypeStruct((t, 2048), BF16),
                          compiler_params=_cp("parallel"))(y, proj, nw)


def ssd_out_bwd(y, proj, dya, nw, dproj, name):
    t = y.shape[0]
    tm = min(512, t)

    def body(y_ref, z_ref, d_ref, nw_ref, _, dy_ref, dz_ref, st_ref):
        @pl.when(pl.program_id(0) == 0)
        def _():
            st_ref[...] = jnp.zeros_like(st_ref)

        for g in range(NG):
            gc = slice(g * 512, (g + 1) * 512)
            z = z_ref[:, gc].astype(F32)
            yv = y_ref[:, gc]
            s = _sigmoid(z)
            sz = z * s
            yz = yv * sz
            r = lax.rsqrt(jnp.mean(yz * yz, axis=-1, keepdims=True) + EPS)
            yzn = yz * r
            dv = d_ref[:, gc].astype(F32)
            st_ref[0:1, gc] += jnp.sum(dv * yzn, axis=0, keepdims=True)
            dyn = dv * nw_ref[:, gc]
            dyz = r * (dyn - yzn * jnp.mean(dyn * yzn, axis=-1, keepdims=True))
            dy_ref[:, gc] = dyz * sz
            dz_ref[:, gc] = (dyz * yv * (s * (1.0 + z * (1.0 - s)))).astype(BF16)

    return pl.pallas_call(
        body, name=name, grid=(t // tm,),
        in_specs=[_row(tm, 2048), _row(tm, 2048, P_Z // 2048), _row(tm, 2048), _vec(2048), _HBM],
        out_specs=[_row(tm, 2048), _row(tm, 2048, P_Z // 2048), _stats(2048)],
        out_shape=[jax.ShapeDtypeStruct((t, 2048), F32), jax.ShapeDtypeStruct(dproj.shape, BF16),
                   jax.ShapeDtypeStruct((8, 2048), F32)],
        input_output_aliases={4: 1}, compiler_params=_cp("arbitrary"))(y, proj, dya, nw, dproj)


def _cstep(ar, ai, sr, si, br, bi):
    return ar * sr - ai * si + br, ar * si + ai * sr + bi


def _halves(v):
    return (v[0:8, 0:512], v[0:8, 512:1024]), (v[8:16, 0:512], v[8:16, 512:1024])


def _slab(r1, i1, r2, i2):
    return jnp.concatenate([jnp.concatenate([r1, i1], axis=1), jnp.concatenate([r2, i2], axis=1)], axis=0).astype(BF16)


def _local_ends(x_ref, nslab, ar, ai, sr_scr, si_scr, end_ref, first, last, reverse):
    @pl.when(first)
    def _():
        sr_scr[...] = jnp.zeros_like(sr_scr)
        si_scr[...] = jnp.zeros_like(si_scr)

    def step(k, carry):
        s1, s2 = _halves(x_ref[nslab - 1 - k if reverse else k].astype(F32))
        if reverse:
            s1, s2 = s2, s1
        return _cstep(ar, ai, *_cstep(ar, ai, carry[0], carry[1], *s1), *s2)

    sr, si = lax.fori_loop(0, nslab, step, (sr_scr[...], si_scr[...]), unroll=4)
    sr_scr[...] = sr
    si_scr[...] = si

    @pl.when(last)
    def _():
        end_ref[:, 0:512] = sr
        end_ref[:, 512:1024] = si


def s5_in(u, bsg, a_re, a_im, name):
    t = u.shape[0]
    tm = min(512, t)
    nt = t // tm

    def body(u_ref, b_ref, ar_ref, ai_ref, o_ref, e_ref, sr_scr, si_scr):
        i = pl.program_id(1)
        o_ref[...] = _dot(u_ref[...].astype(BF16), b_ref[...]).astype(BF16).reshape(tm // 16, 16, 1024)
        _local_ends(o_ref, tm // 16, ar_ref[...], ai_ref[...], sr_scr, si_scr, e_ref, i == 0, i == nt - 1, False)

    return pl.pallas_call(
        body, name=name, grid=(8, nt),
        in_specs=[pl.BlockSpec((tm, 128), lambda s, i: (i, s)), pl.BlockSpec((None, 128, 1024), lambda s, i: (s, 0, 0)),
                  pl.BlockSpec((None, 8, 512), lambda s, i: (s, 0, 0)), pl.BlockSpec((None, 8, 512), lambda s, i: (s, 0, 0))],
        out_specs=[pl.BlockSpec((tm // 16, 16, 1024), lambda s, i: (i, 0, s)),
                   pl.BlockSpec((None, 8, 1024), lambda s, i: (s, 0, 0))],
        out_shape=[jax.ShapeDtypeStruct((t // 16, 16, S5NS), BF16), jax.ShapeDtypeStruct((8, 8, 1024), F32)],
        scratch_shapes=[pltpu.VMEM((8, 512), F32), pltpu.VMEM((8, 512), F32)],
        compiler_params=_cp("parallel", "arbitrary"))(u, bsg, a_re, a_im)


def s5_out_bwd(dy, csg, s, a_re, a_im, name):
    t = dy.shape[0]
    tm = min(512, t)
    nt = t // tm

    def body(dy_ref, c_ref, s_ref, ar_ref, ai_ref, e_ref, dc_ref, end_ref, sr_scr, si_scr):
        i = pl.program_id(1)

        @pl.when(i == 0)
        def _():
            dc_ref[...] = jnp.zeros_like(dc_ref)

        dyb = dy_ref[...].astype(BF16)
        e_ref[...] = _nt(dyb, c_ref[...]).astype(BF16).reshape(tm // 16, 16, 1024)
        dc_ref[...] += lax.dot_general(s_ref[...], dyb, (((0,), (0,)), ((), ())), preferred_element_type=F32)
        _local_ends(e_ref, tm // 16, ar_ref[...], -ai_ref[...], sr_scr, si_scr, end_ref, i == 0, i == nt - 1, True)

    rv = lambda i: nt - 1 - i
    return pl.pallas_call(
        body, name=name, grid=(8, nt),
        in_specs=[pl.BlockSpec((tm, 128), lambda s, i: (rv(i), s)), pl.BlockSpec((None, 1024, 128), lambda s, i: (s, 0, 0)),
                  pl.BlockSpec((tm, 1024), lambda s, i: (rv(i), s)),
                  pl.BlockSpec((None, 8, 512), lambda s, i: (s, 0, 0)), pl.BlockSpec((None, 8, 512), lambda s, i: (s, 0, 0))],
        out_specs=[pl.BlockSpec((tm // 16, 16, 1024), lambda s, i: (rv(i), 0, s)),
                   pl.BlockSpec((None, 1024, 128), lambda s, i: (s, 0, 0)),
                   pl.BlockSpec((None, 8, 1024), lambda s, i: (s, 0, 0))],
        out_shape=[jax.ShapeDtypeStruct((t // 16, 16, S5NS), BF16), jax.ShapeDtypeStruct((8, 1024, 128), F32),
                   jax.ShapeDtypeStruct((8, 8, 1024), F32)],
        scratch_shapes=[pltpu.VMEM((8, 512), F32), pltpu.VMEM((8, 512), F32)],
        compiler_params=_cp("parallel", "arbitrary"))(dy, csg, s, a_re, a_im)


def s5_scan_init(ends, a_re, a_im, lseg, reverse, name):
    nsq = int(math.log2(lseg))
    assert 2 ** nsq == lseg
    sgn = -1.0 if reverse else 1.0
    order = list(range(7, -1, -1)) if reverse else list(range(8))

    def body(e_ref, ar_ref, ai_ref, o_ref):
        pr = ar_ref[0:1, :]
        pi = sgn * ai_ref[0:1, :]
        for _ in range(nsq):
            pr, pi = pr * pr - pi * pi, 2.0 * pr * pi
        prev_r = jnp.zeros((1, 512), F32)
        prev_i = jnp.zeros((1, 512), F32)
        j0 = order[0]
        o_ref[j0:j0 + 1, 0:512] = prev_r
        o_ref[j0:j0 + 1, 512:1024] = prev_i
        for idx in range(1, 8):
            j, jp = order[idx], order[idx - 1]
            prev_r, prev_i = _cstep(pr, pi, prev_r, prev_i, e_ref[jp:jp + 1, 0:512], e_ref[jp:jp + 1, 512:1024])
            o_ref[j:j + 1, 0:512] = prev_r
            o_ref[j:j + 1, 512:1024] = prev_i

    return pl.pallas_call(
        body, name=name, grid=(8,),
        in_specs=[pl.BlockSpec((None, 8, 1024), lambda s: (s, 0, 0)), pl.BlockSpec((None, 8, 512), lambda s: (s, 0, 0)),
                  pl.BlockSpec((None, 8, 512), lambda s: (s, 0, 0))],
        out_specs=pl.BlockSpec((None, 8, 1024), lambda s: (s, 0, 0)),
        out_shape=jax.ShapeDtypeStruct((8, 8, 1024), F32), compiler_params=_cp("parallel"))(ends, a_re, a_im)


def s5_scan_fwd(b3, init, a_re, a_im, csg, u, d_row, name):
    nslab = b3.shape[0]
    ti = min(64, nslab)
    nb = nslab // ti
    rows = 16 * ti

    def body(b_ref, i_ref, ar_ref, ai_ref, c_ref, u_ref, d_ref, o_ref, y_ref, sr_scr, si_scr):
        @pl.when(pl.program_id(1) == 0)
        def _():
            sr_scr[...] = i_ref[:, 0:512]
            si_scr[...] = i_ref[:, 512:1024]

        ar = ar_ref[...]
        ai = ai_ref[...]

        def step(k, carry):
            b1, b2 = _halves(b_ref[k].astype(F32))
            r1, i1 = _cstep(ar, ai, carry[0], carry[1], *b1)
            r2, i2 = _cstep(ar, ai, r1, i1, *b2)
            o_ref[k] = _slab(r1, i1, r2, i2)
            return r2, i2

        sr, si = lax.fori_loop(0, ti, step, (sr_scr[...], si_scr[...]), unroll=4)
        sr_scr[...] = sr
        si_scr[...] = si
        y_ref[...] = _dot(o_ref[...].reshape(rows, 1024), c_ref[...]) + d_ref[...] * u_ref[...].astype(F32)

    return pl.pallas_call(
        body, name=name, grid=(8, nb),
        in_specs=[pl.BlockSpec((ti, 16, 1024), lambda s, tb: (tb, 0, s)), pl.BlockSpec((None, 8, 1024), lambda s, tb: (s, 0, 0)),
                  pl.BlockSpec((None, 8, 512), lambda s, tb: (s, 0, 0)), pl.BlockSpec((None, 8, 512), lambda s, tb: (s, 0, 0)),
                  pl.BlockSpec((None, 1024, 128), lambda s, tb: (s, 0, 0)), pl.BlockSpec((rows, 128), lambda s, tb: (tb, s)),
                  pl.BlockSpec((1, 128), lambda s, tb: (0, s))],
        out_specs=[pl.BlockSpec((ti, 16, 1024), lambda s, tb: (tb, 0, s)), pl.BlockSpec((rows, 128), lambda s, tb: (tb, s))],
        out_shape=[jax.ShapeDtypeStruct(b3.shape, BF16), jax.ShapeDtypeStruct((16 * nslab, S5W), F32)],
        scratch_shapes=[pltpu.VMEM((8, 512), F32), pltpu.VMEM((8, 512), F32)],
        compiler_params=_cp("parallel", "arbitrary"))(b3, init, a_re, a_im, csg, u, d_row)


def s5_scan_bwd(e3, linit, s3, sinit, a_re, a_im, bsg, u, dy, d_row, name):
    nslab = e3.shape[0]
    ti = min(64, nslab)
    nb = nslab // ti
    rows = 16 * ti

    def body(e_ref, li_ref, s_ref, sh_ref, si0_ref, ar_ref, ai_ref, b_ref, u_ref, dy_ref, d_ref,
             du_ref, db_ref, dd_ref, da_ref, o_ref, lr_scr, lim_scr):
        tb = pl.program_id(1)

        @pl.when(tb == 0)
        def _():
            lr_scr[...] = li_ref[:, 0:512]
            lim_scr[...] = li_ref[:, 512:1024]
            da_ref[...] = jnp.zeros_like(da_ref)
            db_ref[...] = jnp.zeros_like(db_ref)
            dd_ref[...] = jnp.zeros_like(dd_ref)

        ar = ar_ref[...]
        ai = -ai_ref[...]

        def slab(kk, lr, li, dar, dai, sp):
            e1, e2 = _halves(e_ref[kk].astype(F32))
            s1, _ = _halves(s_ref[kk].astype(F32))
            r2, i2 = _cstep(ar, ai, lr, li, *e2)
            dar = dar + r2 * s1[0] + i2 * s1[1]
            dai = dai + i2 * s1[0] - r2 * s1[1]
            r1, i1 = _cstep(ar, ai, r2, i2, *e1)
            dar = dar + r1 * sp[0] + i1 * sp[1]
            dai = dai + i1 * sp[0] - r1 * sp[1]
            o_ref[kk] = _slab(r1, i1, r2, i2)
            return r1, i1, dar, dai

        def step(k, carry):
            kk = ti - 1 - k
            return slab(kk, *carry, _halves(s_ref[kk - 1].astype(F32))[1])

        z = jnp.zeros((8, 512), F32)
        lr, li, dar, dai = lax.fori_loop(0, ti - 1, step, (lr_scr[...], lim_scr[...], z, z), unroll=2)
        first = tb == nb - 1
        halo = _halves(sh_ref[0].astype(F32))[1]
        sp = (jnp.where(first, si0_ref[:, 0:512], halo[0]), jnp.where(first, si0_ref[:, 512:1024], halo[1]))
        lr, li, dar, dai = slab(0, lr, li, dar, dai, sp)
        lr_scr[...] = lr
        lim_scr[...] = li
        da_ref[:, 0:512] += dar
        da_ref[:, 512:1024] += dai
        lb = o_ref[...].reshape(rows, 1024)
        uv = u_ref[...].astype(F32)
        dyv = dy_ref[...]
        du_ref[...] = _nt(lb, b_ref[...]) + d_ref[...] * dyv
        db_ref[...] += lax.dot_general(uv.astype(BF16), lb, (((0,), (0,)), ((), ())), preferred_element_type=F32)
        dd_ref[...] += jnp.sum(dyv * uv, axis=0, keepdims=True)

    rb = lambda tb: nb - 1 - tb
    return pl.pallas_call(
        body, name=name, grid=(8, nb),
        in_specs=[pl.BlockSpec((ti, 16, 1024), lambda s, tb: (rb(tb), 0, s)),
                  pl.BlockSpec((None, 8, 1024), lambda s, tb: (s, 0, 0)),
                  pl.BlockSpec((ti, 16, 1024), lambda s, tb: (rb(tb), 0, s)),
                  pl.BlockSpec((1, 16, 1024), lambda s, tb: (jnp.maximum(rb(tb) * ti - 1, 0), 0, s)),
                  pl.BlockSpec((None, 8, 1024), lambda s, tb: (s, 0, 0)),
                  pl.BlockSpec((None, 8, 512), lambda s, tb: (s, 0, 0)), pl.BlockSpec((None, 8, 512), lambda s, tb: (s, 0, 0)),
                  pl.BlockSpec((None, 128, 1024), lambda s, tb: (s, 0, 0)),
                  pl.BlockSpec((rows, 128), lambda s, tb: (rb(tb), s)), pl.BlockSpec((rows, 128), lambda s, tb: (rb(tb), s)),
                  pl.BlockSpec((1, 128), lambda s, tb: (0, s))],
        out_specs=[pl.BlockSpec((rows, 128), lambda s, tb: (rb(tb), s)),
                   pl.BlockSpec((None, 128, 1024), lambda s, tb: (s, 0, 0)), pl.BlockSpec((1, 128), lambda s, tb: (0, s)),
                   pl.BlockSpec((None, 8, 1024), lambda s, tb: (s, 0, 0))],
        out_shape=[jax.ShapeDtypeStruct((16 * nslab, S5W), F32), jax.ShapeDtypeStruct((8, 128, 1024), F32),
                   jax.ShapeDtypeStruct((1, S5W), F32), jax.ShapeDtypeStruct((8, 8, 1024), F32)],
        scratch_shapes=[pltpu.VMEM((ti, 16, 1024), BF16), pltpu.VMEM((8, 512), F32), pltpu.VMEM((8, 512), F32)],
        compiler_params=_cp("parallel", "arbitrary"))(e3, linit, s3, s3, sinit, a_re, a_im, bsg, u, dy, d_row)


_GC = math.sqrt(2.0 / math.pi)


def gelu_fwd(y, name):
    t, w = y.shape
    tm = min(512, t)

    def body(y_ref, o_ref):
        v = y_ref[...]
        o_ref[...] = (0.5 * v * (1.0 + jnp.tanh(_GC * (v + 0.044715 * v * v * v)))).astype(BF16)

    return pl.pallas_call(body, name=name, grid=(t // tm,), in_specs=[_row(tm, w)], out_specs=_row(tm, w),
                          out_shape=jax.ShapeDtypeStruct((t, w), BF16), compiler_params=_cp("parallel"))(y)


def gelu_bwd(y, dg, name):
    t, w = y.shape
    tm = min(512, t)

    def body(y_ref, d_ref, o_ref):
        v = y_ref[...]
        th = jnp.tanh(_GC * (v + 0.044715 * v * v * v))
        o_ref[...] = d_ref[...].astype(F32) * (0.5 * (1.0 + th) + 0.5 * v * (1.0 - th * th) * _GC * (1.0 + 3.0 * 0.044715 * v * v))

    return pl.pallas_call(body, name=name, grid=(t // tm,), in_specs=[_row(tm, w), _row(tm, w)], out_specs=_row(tm, w),
                          out_shape=jax.ShapeDtypeStruct((t, w), F32), compiler_params=_cp("parallel"))(y, dg)


def merge_fwd(proj, pa, glu, name):
    t = pa.shape[0]
    tm = min(256, t)

    def body(g_ref, pa_ref, glu_ref, o_ref):
        pb = glu_ref[:, 0:D].astype(F32) * _sigmoid(glu_ref[:, D:2 * D].astype(F32))
        o_ref[...] = (_sigmoid(g_ref[:, 0:D].astype(F32)) * pa_ref[...].astype(F32)
                      + _sigmoid(g_ref[:, D:2 * D].astype(F32)) * pb).astype(BF16)

    return pl.pallas_call(body, name=name, grid=(t // tm,), in_specs=[_row(tm, 2 * D), _row(tm, D), _row(tm, 2 * D)],
                          out_specs=_row(tm, D), out_shape=jax.ShapeDtypeStruct((t, D), BF16),
                          compiler_params=_cp("parallel"))(proj, pa, glu)


def merge_bwd(proj, pa, glu, dm, dproj, name):
    t = pa.shape[0]
    tm = min(256, t)

    def body(g_ref, pa_ref, glu_ref, dm_ref, _, dpa_ref, dglu_ref, dg_ref):
        dmv = dm_ref[...].astype(F32)
        pav = pa_ref[...].astype(F32)
        sa = _sigmoid(g_ref[:, 0:D].astype(F32))
        sb = _sigmoid(g_ref[:, D:2 * D].astype(F32))
        ga = glu_ref[:, 0:D].astype(F32)
        sg = _sigmoid(glu_ref[:, D:2 * D].astype(F32))
        pb = ga * sg
        dpb = sb * dmv
        dpa_ref[...] = (sa * dmv).astype(BF16)
        dglu_ref[:, 0:D] = (dpb * sg).astype(BF16)
        dglu_ref[:, D:2 * D] = (dpb * pb * (1.0 - sg)).astype(BF16)
        dg_ref[:, 0:D] = (dmv * pav * sa * (1.0 - sa)).astype(BF16)
        dg_ref[:, D:2 * D] = (dmv * pb * sb * (1.0 - sb)).astype(BF16)

    return pl.pallas_call(
        body, name=name, grid=(t // tm,),
        in_specs=[_row(tm, 2 * D), _row(tm, D), _row(tm, 2 * D), _row(tm, D), _HBM],
        out_specs=[_row(tm, D), _row(tm, 2 * D), _row(tm, 2 * D, P_GATES // (2 * D))],
        out_shape=[jax.ShapeDtypeStruct((t, D), BF16), jax.ShapeDtypeStruct((t, 2 * D), BF16),
                   jax.ShapeDtypeStruct(dproj.shape, BF16)],
        input_output_aliases={4: 2}, compiler_params=_cp("parallel"))(proj, pa, glu, dm, dproj)


def adamw(w, parts, m, v, name):
    r, c = w.shape
    p = parts.shape[0]
    tr = r if r <= 128 else 128
    c1 = 1.0 - ADAM_B1 ** ADAM_STEP
    c2 = 1.0 - ADAM_B2 ** ADAM_STEP

    def body(w_ref, p_ref, m_ref, v_ref, g_ref, d_ref, nm_ref, nv_ref):
        g = p_ref[0].astype(F32)
        for k in range(1, p):
            g = g + p_ref[k].astype(F32)
        mn = ADAM_B1 * m_ref[...] + (1.0 - ADAM_B1) * g
        vn = ADAM_B2 * v_ref[...] + (1.0 - ADAM_B2) * (g * g)
        g_ref[...] = g
        nm_ref[...] = mn
        nv_ref[...] = vn
        d_ref[...] = -ADAM_LR * ((mn / c1) / (jnp.sqrt(vn / c2) + ADAM_EPS) + ADAM_WD * w_ref[...])

    spec = pl.BlockSpec((tr, c), lambda i: (i, 0))
    o = jax.ShapeDtypeStruct((r, c), F32)
    return pl.pallas_call(
        body, name=name, grid=(pl.cdiv(r, tr),),
        in_specs=[spec, pl.BlockSpec((p, tr, c), lambda i: (0, i, 0)), spec, spec],
        out_specs=[spec, spec, spec, spec], out_shape=[o, o, o, o], compiler_params=_cp("parallel"))(w, parts, m, v)


def _s5_discretise(lambda_re, lambda_im, log_dt, b_re, b_im):
    dt = jnp.exp(log_dt)[:, None]
    lr = jnp.minimum(lambda_re, -1e-4)
    li = lambda_im
    mag = jnp.exp(lr * dt)
    ar = mag * jnp.cos(li * dt)
    ai = mag * jnp.sin(li * dt)
    den = lr * lr + li * li
    nr = ar - 1.0
    kr = (nr * lr + ai * li) / den
    ki = (ai * lr - nr * li) / den
    bbar_re = kr[..., None] * b_re - ki[..., None] * b_im
    bbar_im = kr[..., None] * b_im + ki[..., None] * b_re
    return ar, ai, bbar_re, bbar_im


def _block_diag(v):
    a, b = v.shape[2], v.shape[3]
    eye = jnp.eye(8, dtype=v.dtype)[None, :, None, :, None]
    return (v[:, :, :, None, :] * eye).reshape(8, 8 * a, 8 * b)


def _diag_blocks(m, a, b):
    eye = jnp.eye(8, dtype=m.dtype)[None, :, None, :, None]
    return jnp.sum(m.reshape(8, 8, a, 8, b) * eye, axis=3)


def _bsg_of(bb_re, bb_im):
    f = lambda b: _block_diag(b.reshape(8, 8, 64, 16).transpose(0, 1, 3, 2))
    return jnp.concatenate([f(bb_re), f(bb_im)], axis=2)


def _bsg_diag(dbsg):
    f = lambda x: _diag_blocks(x, 16, 64).transpose(0, 1, 3, 2).reshape(64, 64, 16)
    return f(dbsg[:, :, 0:512]), f(dbsg[:, :, 512:1024])


def _csg_of(c_re, c_im):
    f = lambda c: _block_diag(c.reshape(8, 8, 16, 64).transpose(0, 1, 3, 2))
    return jnp.concatenate([f(c_re), -f(c_im)], axis=1)


def _csg_diag(dcsg):
    f = lambda x: _diag_blocks(x, 64, 16).transpose(0, 1, 3, 2).reshape(64, 16, 64)
    return f(dcsg[:, 0:512, :]), -f(dcsg[:, 512:1024, :])


def _perm(a, t):
    return a.reshape(8, t // 8, a.shape[1]).transpose(1, 0, 2).reshape(t, a.shape[1])


def _unperm(a, t):
    return a.reshape(t // 8, 8, a.shape[1]).transpose(1, 0, 2).reshape(t, a.shape[1])


def _cols(g):
    return g.transpose(1, 0, 2).reshape(g.shape[1], N_DEV * g.shape[2])


def _rows(g):
    return g.reshape(N_DEV * g.shape[1], g.shape[2])


def _col_parts(g):
    r, c = g.shape
    return g.reshape(r, N_DEV, c // N_DEV).transpose(1, 0, 2)


def _row_parts(g):
    r, c = g.shape
    return g.reshape(N_DEV, r // N_DEV, c)


FB, FBP = D_FF // N_DEV, D_FFP // N_DEV


def _pad_ffn_in_shard(w):
    return jnp.pad(w.reshape(D, 2, FB), ((0, 0), (0, 0), (0, FBP - FB))).reshape(D, 2 * FBP)


def _unpad_ffn_in_shard(g):
    return g.reshape(D, 2, FBP)[:, :, :FB].reshape(D, 2 * FB)


def _pad_ffn_out_shard(w):
    return jnp.pad(w, ((0, FBP - FB), (0, 0)))


def sum_parts(parts, name):
    p, r, c = parts.shape
    tr = 256

    def body(p_ref, o_ref):
        g = p_ref[0].astype(F32)
        for k in range(1, p):
            g = g + p_ref[k].astype(F32)
        o_ref[...] = g

    return pl.pallas_call(body, name=name, grid=(r // tr,), in_specs=[pl.BlockSpec((p, tr, c), lambda i: (0, i, 0))],
                          out_specs=pl.BlockSpec((tr, c), lambda i: (i, 0)),
                          out_shape=jax.ShapeDtypeStruct((r, c), F32), compiler_params=_cp("parallel"))(parts)


def _pad_w_in(w):
    z = jnp.zeros((D, INP - P_DT - NH), w.dtype)
    return jnp.concatenate([w[:, O_GA:O_GB], w[:, O_GB:IN_COLS], w[:, 0:O_XBC], w[:, O_XBC:O_DT], w[:, O_U:O_GA],
                            w[:, O_DT:O_U], z], axis=1)


def _unpad_w_in(g):
    return jnp.concatenate([g[:, P_Z:P_XBC], g[:, P_XBC:P_U], g[:, P_DT:P_DT + NH], g[:, P_U:P_DT],
                            g[:, 0:D], g[:, D:2 * D]], axis=1)


_PACK = (("b_ada", 18432), ("norm_ffn1", 2048), ("norm_mix", 2048), ("conv_b", 3072), ("dt_bias", 32), ("a_log", 32),
         ("d_ssd", 32), ("ssd_norm_w", 2048), ("s5_lambda_re", 4096), ("s5_lambda_im", 4096), ("s5_b_re", 65536),
         ("s5_b_im", 65536), ("s5_c_re", 65536), ("s5_c_im", 65536), ("s5_d", 1024), ("s5_log_dt", 64),
         ("norm_ffn2", 2048), ("norm_final", 2048), ("loss", 1))
_PACK_ROWS = 304
_PACK_W = 1024


def _pack(d):
    flat = jnp.concatenate([d[k].reshape(-1).astype(F32) for k, _ in _PACK])
    return jnp.pad(flat, (0, _PACK_ROWS * _PACK_W - flat.shape[0])).reshape(_PACK_ROWS, _PACK_W)


def _unpack(a):
    flat = a.reshape(-1)
    out, off = {}, 0
    for k, n in _PACK:
        out[k] = flat[off:off + n]
        off += n
    return out


_TA = dict(tm=512, tn=512, tk=8192)


def _ffn_bwd(df, h, ab, act, w_in_shard_t, w_out_t, tag):
    dab, (g_wt,) = ffn_dab(df, w_out_t, ab, tag + "_dab", comm=[("ag", w_in_shard_t)])
    w_in_t = _rows(g_wt)
    dw_out = mm(act, df, ta=True, out_dtype=BF16, i_outer=True, name=tag + "_dwout", **_TA)
    dw_in, (x_out,) = mm(h, dab, ta=True, b_halves=True, out_dtype=BF16, i_outer=True, name=tag + "_dwin",
                         comm=[("xc", _row_parts(dw_out))], **_TA)
    dh, (x_in,) = mm(dab, w_in_t, a_halves=True, out_dtype=BF16, tk=5632, name=tag + "_dh", comm=[("xcc", dw_in)])
    g_in = _unpad_ffn_in_shard(sum_parts(x_in, tag + "_dwin_sum"))
    return dh, g_in[None], x_out


def kernel(x, c, w_ada, b_ada, norm_ffn1, w_ffn1_in, w_ffn1_out, norm_mix, w_in, conv_w, conv_b, dt_bias, a_log, d_ssd, ssd_norm_w, w_a_proj, s5_lambda_re, s5_lambda_im, s5_b_re, s5_b_im, s5_c_re, s5_c_im, s5_d, s5_log_dt, w_b_glu, w_out, norm_ffn2, w_ffn2_in, w_ffn2_out, norm_final, loss_target, m_w_ada, m_b_ada, m_norm_ffn1, m_w_ffn1_in, m_w_ffn1_out, m_norm_mix, m_w_in, m_conv_w, m_conv_b, m_dt_bias, m_a_log, m_d_ssd, m_ssd_norm_w, m_w_a_proj, m_s5_lambda_re, m_s5_lambda_im, m_s5_b_re, m_s5_b_im, m_s5_c_re, m_s5_c_im, m_s5_d, m_s5_log_dt, m_w_b_glu, m_w_out, m_norm_ffn2, m_w_ffn2_in, m_w_ffn2_out, m_norm_final, v_w_ada, v_b_ada, v_norm_ffn1, v_w_ffn1_in, v_w_ffn1_out, v_norm_mix, v_w_in, v_conv_w, v_conv_b, v_dt_bias, v_a_log, v_d_ssd, v_ssd_norm_w, v_w_a_proj, v_s5_lambda_re, v_s5_lambda_im, v_s5_b_re, v_s5_b_im, v_s5_c_re, v_s5_c_im, v_s5_d, v_s5_log_dt, v_w_b_glu, v_w_out, v_norm_ffn2, v_w_ffn2_in, v_w_ffn2_out, v_norm_final):
    args = dict(locals())
    t = x.shape[1]
    me = _my_id()
    xt = x[0]
    tgt = loss_target[0]
    small = {k: args[k] for k, _ in _PACK if k != "loss"}

    bf = lambda w: w[0].astype(BF16)
    ffn_in_shard = lambda w: _pad_ffn_in_shard(bf(w))
    ffn_out_shard = lambda w: _pad_ffn_out_shard(bf(w))

    c8 = all_gather(c, "ag_c").reshape(N_DEV, D)
    b_loc = lax.dynamic_slice(b_ada, (0, me * (N_ADA * D // N_DEV)), (1, N_ADA * D // N_DEV))
    m8 = ada_fwd(c8, w_ada[0], b_loc, "ada_fwd")
    mods = comm_call([("xc", m8.reshape(N_DEV, 1, -1))], "xc_mods")[0].reshape(1, N_ADA * D)

    h1, (wf1i, g_cw) = mod_fwd(xt, norm_ffn1, mods, 0, 1, name="mod1",
                               comm=[("agc", ffn_in_shard(w_ffn1_in)), ("ag", conv_w[0])])
    convw = _cols(g_cw)
    ab1, act1, (g_f1o, g_win, g_wap) = ffn_in_act(
        h1, wf1i, "ffn1_in", fwd=0.9,
        comm=[("ag", ffn_out_shard(w_ffn1_out)), ("ag", bf(w_in)), ("ag", bf(w_a_proj))])
    wf1o, winp, wap = _rows(g_f1o), _pad_w_in(_cols(g_win)), _rows(g_wap)
    f1, (g_wo, g_wbg) = mm(act1, wf1o, out_dtype=BF16, tk=5632, name="ffn1_out",
                           comm=[("ag", bf(w_out)), ("ag", bf(w_b_glu))])
    wo, wbg = _rows(g_wo), _cols(g_wbg)
    x1, h2 = mod_fwd(xt, norm_mix, mods, 3, 4, f=f1, gk=2, gscale=0.5, name="mod2")
    proj, (wf2i,) = mm(h2, winp, out_dtype=BF16, tm=1024, tn=512, i_outer=True, name="w_in",
                       comm=[("agc", ffn_in_shard(w_ffn2_in))])
    dtraw = mm(h2, winp[:, P_DT:P_DT + 128], tn=128, name="w_in_dt")
    cb_row = conv_b
    xc = conv_fwd(proj, convw, cb_row, "conv_fwd")
    row128 = lambda v: jnp.pad(v.reshape(1, -1), ((0, 0), (0, 128 - v.size)))
    dtb_row, alog_row = row128(dt_bias), row128(a_log)
    dx_row = jnp.repeat(d_ssd.reshape(-1), HP).reshape(1, 2048)
    rows = jnp.arange(128)[:, None]
    expm = ((rows % NH == jnp.arange(2048)[None, :] // HP) & (rows < 3 * NH)).astype(BF16)
    tri = (jnp.arange(LCH)[:, None] >= jnp.arange(LCH)[None, :]).astype(F32)
    y_ssd, hsave = ssd_fwd(xc, dtraw, dtb_row, alog_row, dx_row, expm, tri, "ssd_fwd")
    ya = ssd_out_fwd(y_ssd, proj, ssd_norm_w, "ssd_out")
    pa = mm(ya, wap, out_dtype=BF16, name="w_a_proj")

    s5p = (s5_lambda_re[0], s5_lambda_im[0], s5_log_dt[0], s5_b_re[0], s5_b_im[0])
    (ar, ai, bb_re, bb_im), s5_vjp = jax.vjp(_s5_discretise, *s5p)
    a_re8 = jnp.broadcast_to(ar.reshape(8, 1, 512), (8, 8, 512))
    a_im8 = jnp.broadcast_to(ai.reshape(8, 1, 512), (8, 8, 512))
    bsg = _bsg_of(bb_re, bb_im).astype(BF16)
    csg = _csg_of(s5_c_re[0], s5_c_im[0]).astype(BF16)
    d_row = s5_d.reshape(1, S5W)
    lseg = t // 8
    u_p = _perm(proj[:, P_U:P_U + S5W], t)
    bu3, ends_f = s5_in(u_p, bsg, a_re8, a_im8, "s5_in")
    sinit = s5_scan_init(ends_f, a_re8, a_im8, lseg, False, "s5_init_f")
    s3, yb_p = s5_scan_fwd(bu3, sinit, a_re8, a_im8, csg, u_p, d_row, "s5_scan_f")
    s2 = s3.reshape(t, S5NS)
    yb = _unperm(yb_p, t)
    gy = gelu_fwd(yb, "gelu")
    glu = mm(gy, wbg, out_dtype=BF16, name="w_b_glu")
    merged = merge_fwd(proj, pa, glu, "merge")
    o = mm(merged, wo, out_dtype=BF16, name="w_out")
    x2, h3 = mod_fwd(x1, norm_ffn2, mods, 6, 7, f=o, gk=5, gscale=1.0, name="mod3")
    ab3, act3, (g_f2o,) = ffn_in_act(h3, wf2i, "ffn2_in", comm=[("ag", ffn_out_shard(w_ffn2_out))])
    wf2o = _rows(g_f2o)
    f3 = mm(act3, wf2o, out_dtype=BF16, tk=5632, name="ffn2_out")

    dx3, df3, st_fin = final_fwd_bwd(x2, f3, mods, norm_final.reshape(1, D), tgt, "final")
    dh3, x_f2i, x_f2o = _ffn_bwd(df3, h3, ab3, act3, ffn_in_shard(w_ffn2_in).T, wf2o.T, "ffn2")
    dx2, do, st3 = mod_bwd(x2, dh3, dx3, norm_ffn2, mods, 7, fprev=o, gk=5, gscale=1.0, name="mod3_bwd")

    dmerged = mm(do, wo.T, out_dtype=BF16, name="w_out_dx")
    dwo = mm(merged, do, ta=True, out_dtype=BF16, i_outer=True, name="w_out_dw", **_TA)
    dpa, dglu, dproj = merge_bwd(proj, pa, glu, dmerged, lax.empty((t, INP), BF16), "merge_bwd")
    dwbg = mm(gy, dglu, ta=True, out_dtype=BF16, i_outer=True, name="w_b_glu_dw", **_TA)
    dgy, (x_wo,) = mm(dglu, wbg.T, out_dtype=BF16, name="w_b_glu_dx", comm=[("xc", _row_parts(dwo))])
    dyb_p = _perm(gelu_bwd(yb, dgy, "gelu_bwd"), t)
    e3, dcsg, ends_b = s5_out_bwd(dyb_p, csg, s2, a_re8, a_im8, "s5_out_bwd")
    linit = s5_scan_init(ends_b, a_re8, a_im8, lseg, True, "s5_init_b")
    du_p, dbsg, dd_row, da8 = s5_scan_bwd(e3, linit, s3, sinit, a_re8, a_im8, bsg, u_p, dyb_p, d_row, "s5_scan_b")
    du = _unperm(du_p, t).astype(BF16)
    da = jnp.sum(da8, axis=1)
    dbb_re, dbb_im = _bsg_diag(dbsg)
    g_lre, g_lim, g_ldt, g_bre, g_bim = s5_vjp((da[:, 0:512].reshape(64, 64), da[:, 512:1024].reshape(64, 64),
                                                dbb_re, dbb_im))
    g_cre, g_cim = _csg_diag(dcsg)

    dwap = mm(ya, dpa, ta=True, out_dtype=BF16, i_outer=True, name="w_a_proj_dw", **_TA)
    dya, (x_wbg,) = mm(dpa, wap.T, out_dtype=BF16, name="w_a_proj_dx", comm=[("xc", _col_parts(dwbg))])
    dy_ssd, dproj, st_sn = ssd_out_bwd(y_ssd, proj, dya, ssd_norm_w, dproj, "ssd_out_bwd")
    dxc, dproj, st_ssd = ssd_bwd(xc, dtraw, hsave, dy_ssd, dtb_row, alog_row, dx_row, expm, tri, dproj, "ssd_bwd")
    dpre, st_cv = conv_bwd_pre(proj, dxc, convw, cb_row, "conv_bwd_pre")
    dproj = conv_bwd_in(dpre, convw, dproj, "conv_bwd_in")
    dproj = lax.dynamic_update_slice(dproj, du, (0, P_U))
    dwinp, (x_wap, x_cw) = mm(h2, dproj, ta=True, out_dtype=BF16, i_outer=True, name="w_in_dw",
                              comm=[("xc", _row_parts(dwap)), ("xc", _col_parts(st_cv[0:CONV_K]))], **_TA)
    dh2, (x_win,) = mm(dproj, winp.T, out_dtype=BF16, tk=5376, name="w_in_dx",
                       comm=[("xc", _col_parts(_unpad_w_in(dwinp)))])
    dx1, df1, st2 = mod_bwd(x1, dh2, dx2, norm_mix, mods, 4, fprev=f1, gk=2, gscale=0.5, name="mod2_bwd")
    dh1, x_f1i, x_f1o = _ffn_bwd(df1, h1, ab1, act1, ffn_in_shard(w_ffn1_in).T, wf1o.T, "ffn1")
    gx, st1 = mod_bwd(xt, dh1, dx1, norm_ffn1, mods, 1, name="mod1_bwd")

    dmods = jnp.concatenate([st1[0], st1[1], st2[3], st2[0], st2[1], st3[3], st3[0], st3[1], st_fin[1]])
    part = {"b_ada": dmods, "norm_ffn1": st1[2], "norm_mix": st2[2], "conv_b": st_cv[4], "dt_bias": st_ssd[1, 0:NH],
            "a_log": st_ssd[0, 0:NH], "d_ssd": st_ssd[2, 0:NH], "ssd_norm_w": st_sn[0], "s5_lambda_re": g_lre,
            "s5_lambda_im": g_lim, "s5_b_re": g_bre, "s5_b_im": g_bim, "s5_c_re": g_cre, "s5_c_im": g_cim,
            "s5_d": dd_row, "s5_log_dt": g_ldt, "norm_ffn2": st3[2], "norm_final": st_fin[0],
            "loss": (0.5 / D) * jnp.sum(st_fin[2])}
    zero = {"loss": jnp.zeros((1,), F32)}
    gath = all_gather(_pack(part), "ag_small")
    sg, sd, sm, sv = adamw(_pack({**small, **zero}), gath, _pack({**{k: args["m_" + k] for k in small}, **zero}),
                           _pack({**{k: args["v_" + k] for k in small}, **zero}), "adamw_small")
    sg, sd, sm, sv = _unpack(sg), _unpack(sd), _unpack(sm), _unpack(sv)
    loss = sg["loss"][0]

    dm_loc = lax.dynamic_slice(gath.reshape(N_DEV, -1)[:, 0:N_ADA * D], (0, me * (N_ADA * D // N_DEV)),
                               (N_DEV, N_ADA * D // N_DEV))
    g_ada = ada_bwd(c8.T, dm_loc, "ada_bwd")
    big = {"w_ada": g_ada[None], "w_ffn1_in": x_f1i, "w_ffn1_out": x_f1o, "w_in": x_win, "conv_w": x_cw,
           "w_a_proj": x_wap, "w_b_glu": x_wbg, "w_out": x_wo, "w_ffn2_in": x_f2i, "w_ffn2_out": x_f2o}
    res = {}
    for k, parts in big.items():
        res[k] = adamw(args[k][0], parts, args["m_" + k][0], args["v_" + k][0], "adamw_" + k)

    names = ["w_ada", "b_ada", "norm_ffn1", "w_ffn1_in", "w_ffn1_out", "norm_mix", "w_in", "conv_w", "conv_b", "dt_bias",
             "a_log", "d_ssd", "ssd_norm_w", "w_a_proj", "s5_lambda_re", "s5_lambda_im", "s5_b_re", "s5_b_im", "s5_c_re",
             "s5_c_im", "s5_d", "s5_log_dt", "w_b_glu", "w_out", "norm_ffn2", "w_ffn2_in", "w_ffn2_out", "norm_final"]
    outs = [loss, gx[None]]
    for q, src in enumerate((sg, sd, sm, sv)):
        for k in names:
            if k in res:
                outs.append(res[k][q][None])
            else:
                outs.append(src[k].reshape(args[k].shape))
    return tuple(outs)
```
